```python
import jax, jax.numpy as jnp
from jax import lax
import numpy as np

D_MODEL = 1024
BATCH = 8
SEQ = 8192
DEPTH = 1

CHUNK = 64
RNN_WIDTH = 1024
RNN_HEADS = 8
RNN_HEAD_DIM = RNN_WIDTH // RNN_HEADS
CONV_WIDTH = 4
LRU_C = 8.0
SGU_WIDTH = 1024
SGU_GROUPS = 8
SGU_GROUP_DIM = SGU_WIDTH // SGU_GROUPS
SGU_BLOCK = 2 * CHUNK
D_FF = 3 * D_MODEL
FFN_CONV_WIDTH = 3
N_BRANCHES = 2
N_MOD = 6
EPS = 1e-6
IN_COLS = 2 * RNN_WIDTH + 2 * SGU_WIDTH + N_BRANCHES * D_MODEL

kernel_name = "hybrid_rglru_sgu_convffn_block"


def _rmsnorm(x, g):
    x32 = x.astype(jnp.float32)
    y = x32 * lax.rsqrt(jnp.mean(x32 * x32, axis=-1, keepdims=True) + EPS)
    return (y * g.astype(jnp.float32)).astype(x.dtype)


def _layernorm(x, g, b):
    x32 = x.astype(jnp.float32)
    mu = jnp.mean(x32, axis=-1, keepdims=True)
    var = jnp.mean(jnp.square(x32 - mu), axis=-1, keepdims=True)
    y = (x32 - mu) * lax.rsqrt(var + EPS)
    return (y * g.astype(jnp.float32) + b.astype(jnp.float32)).astype(x.dtype)


def _modulated_norm(x, g, shift, scale):
    return _rmsnorm(x, g) * (1.0 + scale[:, None, :]) + shift[:, None, :]


def _causal_dwconv(x, w, b):
    k_width = w.shape[0]
    seq = x.shape[1]
    xp = jnp.pad(x, ((0, 0), (k_width - 1, 0), (0, 0)))
    y = b + xp[:, 0:seq, :] * w[0]
    for k in range(1, k_width):
        y = y + xp[:, k:k + seq, :] * w[k]
    return y


def _block_diag(x, w, b):
    bsz, seq, _ = x.shape
    xh = x.reshape(bsz, seq, RNN_HEADS, RNN_HEAD_DIM)
    y = jnp.einsum("bshi,hij->bshj", xh, w)
    return y.reshape(bsz, seq, RNN_WIDTH) + b


def _lin_combine(left, right):
    a_l, u_l = left
    a_r, u_r = right
    return a_l * a_r, a_r * u_l + u_r


def _rg_lru(x, w_a, b_a, w_x, b_x, lam):
    r = jax.nn.sigmoid(_block_diag(x, w_a, b_a).astype(jnp.float32))
    i = jax.nn.sigmoid(_block_diag(x, w_x, b_x).astype(jnp.float32))
    log_a = LRU_C * r * jax.nn.log_sigmoid(lam.astype(jnp.float32))
    a = jnp.exp(log_a)
    mult = jnp.sqrt(-jnp.expm1(2.0 * log_a))
    u = mult * (i * x.astype(jnp.float32))
    _, h = lax.associative_scan(_lin_combine, (a, u), axis=1)
    return h.astype(x.dtype)


def _spatial_gating(u, v, ln_g, ln_b, w_s, b_s):
    bsz, seq, _ = v.shape
    n_blk = seq // SGU_BLOCK
    v = _layernorm(v, ln_g, ln_b)
    vb = v.reshape(bsz, n_blk, SGU_BLOCK, SGU_GROUPS, SGU_GROUP_DIM)
    mask = jnp.tril(jnp.ones((SGU_BLOCK, SGU_BLOCK), dtype=w_s.dtype))
    mixed = jnp.einsum("gts,bnsgd->bntgd", w_s * mask, vb)
    mixed = mixed + jnp.transpose(b_s)[None, None, :, :, None]
    return u * mixed.reshape(bsz, seq, SGU_WIDTH)


def _fwd_setup_inputs(seed: int = 0) -> dict:
    key = jax.random.key(seed)
    ks = jax.random.split(key, 26)

    def nrm(k, shape, scale):
        return jax.random.normal(k, shape, jnp.float32) * scale

    a_c = jax.random.uniform(ks[12], (DEPTH, RNN_WIDTH), jnp.float32, 0.9, 0.999)
    s = a_c ** (1.0 / LRU_C)
    lru_lambda = jnp.log(s) - jnp.log1p(-s)

    return {
        "x": nrm(ks[0], (BATCH, SEQ, D_MODEL), 1.0),
        "c": nrm(ks[1], (BATCH, D_MODEL), 1.0),
        "w_ada": nrm(ks[2], (DEPTH, D_MODEL, N_MOD * D_MODEL), D_MODEL ** -0.5),
        "b_ada": nrm(ks[3], (DEPTH, N_MOD * D_MODEL), 0.02),
        "norm_mix_g": 1.0 + nrm(ks[4], (DEPTH, D_MODEL), 0.05),
        "w_in": nrm(ks[5], (DEPTH, D_MODEL, IN_COLS), D_MODEL ** -0.5),
        "rnn_conv_w": nrm(ks[6], (DEPTH, CONV_WIDTH, RNN_WIDTH), CONV_WIDTH ** -0.5),
        "rnn_conv_b": nrm(ks[7], (DEPTH, RNN_WIDTH), 0.02),
        "lru_w_a": nrm(ks[8], (DEPTH, RNN_HEADS, RNN_HEAD_DIM, RNN_HEAD_DIM), RNN_HEAD_DIM ** -0.5),
        "lru_b_a": nrm(ks[9], (DEPTH, RNN_WIDTH), 0.02),
        "lru_w_x": nrm(ks[10], (DEPTH, RNN_HEADS, RNN_HEAD_DIM, RNN_HEAD_DIM), RNN_HEAD_DIM ** -0.5),
        "lru_b_x": nrm(ks[11], (DEPTH, RNN_WIDTH), 0.02),
        "lru_lambda": lru_lambda,
        "sgu_ln_g": 1.0 + nrm(ks[13], (DEPTH, SGU_WIDTH), 0.05),
        "sgu_ln_b": nrm(ks[14], (DEPTH, SGU_WIDTH), 0.02),
        "sgu_w_s": nrm(ks[15], (DEPTH, SGU_GROUPS, SGU_BLOCK, SGU_BLOCK), SGU_BLOCK ** -0.5),
        "sgu_b_s": 1.0 + nrm(ks[16], (DEPTH, SGU_GROUPS, SGU_BLOCK), 0.1),
        "w_branch_a": nrm(ks[17], (DEPTH, RNN_WIDTH, D_MODEL), RNN_WIDTH ** -0.5),
        "w_branch_b": nrm(ks[18], (DEPTH, SGU_WIDTH, D_MODEL), SGU_WIDTH ** -0.5),
        "w_out": nrm(ks[19], (DEPTH, D_MODEL, D_MODEL), D_MODEL ** -0.5),
        "norm_ffn_g": 1.0 + nrm(ks[20], (DEPTH, D_MODEL), 0.05),
        "w_up": nrm(ks[21], (DEPTH, D_MODEL, 2 * D_FF), D_MODEL ** -0.5),
        "ffn_conv_w": nrm(ks[22], (DEPTH, FFN_CONV_WIDTH, 2 * D_FF), FFN_CONV_WIDTH ** -0.5),
        "ffn_conv_b": nrm(ks[23], (DEPTH, 2 * D_FF), 0.02),
        "w_down": nrm(ks[24], (DEPTH, D_FF, D_MODEL), D_FF ** -0.5),
        "norm_final_g": 1.0 + nrm(ks[25], (D_MODEL,), 0.05),
    }


def _fwd_reference(x, c, w_ada, b_ada, norm_mix_g, w_in, rnn_conv_w, rnn_conv_b,
              lru_w_a, lru_b_a, lru_w_x, lru_b_x, lru_lambda,
              sgu_ln_g, sgu_ln_b, sgu_w_s, sgu_b_s,
              w_branch_a, w_branch_b, w_out,
              norm_ffn_g, w_up, ffn_conv_w, ffn_conv_b, w_down, norm_final_g):
    split_idx = [RNN_WIDTH, 2 * RNN_WIDTH, 2 * RNN_WIDTH + SGU_WIDTH,
                 2 * RNN_WIDTH + 2 * SGU_WIDTH, 2 * RNN_WIDTH + 2 * SGU_WIDTH + D_MODEL]
    c_act = jax.nn.silu(c)
    for l in range(DEPTH):
        mod = c_act @ w_ada[l] + b_ada[l]
        shift1, scale1, gate1, shift2, scale2, gate2 = jnp.split(mod, N_MOD, axis=-1)

        h = _modulated_norm(x, norm_mix_g[l], shift1, scale1)
        z = h @ w_in[l]
        xr, gr, zu, zv, ga, gb = jnp.split(z, split_idx, axis=-1)

        xr = _causal_dwconv(xr, rnn_conv_w[l], rnn_conv_b[l])
        y_a = _rg_lru(xr, lru_w_a[l], lru_b_a[l], lru_w_x[l], lru_b_x[l], lru_lambda[l])
        y_a = (y_a * jax.nn.gelu(gr)) @ w_branch_a[l]

        y_b = _spatial_gating(jax.nn.gelu(zu), jax.nn.gelu(zv), sgu_ln_g[l], sgu_ln_b[l],
                              sgu_w_s[l], sgu_b_s[l])
        y_b = y_b @ w_branch_b[l]

        merged = jax.nn.sigmoid(ga) * y_a + jax.nn.sigmoid(gb) * y_b
        x = x + gate1[:, None, :] * (merged @ w_out[l])

        h = _modulated_norm(x, norm_ffn_g[l], shift2, scale2)
        hid = _causal_dwconv(h @ w_up[l], ffn_conv_w[l], ffn_conv_b[l])
        act, val = jnp.split(hid, 2, axis=-1)
        x = x + gate2[:, None, :] * ((jax.nn.gelu(act) * val) @ w_down[l])

    return _rmsnorm(x, norm_final_g)


import jax as _jax
import jax.numpy as _jnp

TWIN_FORMAT = 'train_step'
FWD_PARAMS = ['x', 'c', 'w_ada', 'b_ada', 'norm_mix_g', 'w_in', 'rnn_conv_w', 'rnn_conv_b', 'lru_w_a', 'lru_b_a', 'lru_w_x', 'lru_b_x', 'lru_lambda', 'sgu_ln_g', 'sgu_ln_b', 'sgu_w_s', 'sgu_b_s', 'w_branch_a', 'w_branch_b', 'w_out', 'norm_ffn_g', 'w_up', 'ffn_conv_w', 'ffn_conv_b', 'w_down', 'norm_final_g']
TWIN_WEIGHTS = ['w_ada', 'b_ada', 'norm_mix_g', 'w_in', 'rnn_conv_w', 'rnn_conv_b', 'lru_w_a', 'lru_b_a', 'lru_w_x', 'lru_b_x', 'lru_lambda', 'sgu_ln_g', 'sgu_ln_b', 'sgu_w_s', 'sgu_b_s', 'w_branch_a', 'w_branch_b', 'w_out', 'norm_ffn_g', 'w_up', 'ffn_conv_w', 'ffn_conv_b', 'w_down', 'norm_final_g']
TWIN_DIFF_INPUT = 'x'
TWIN_INPUTS = ['x', 'c', 'w_ada', 'b_ada', 'norm_mix_g', 'w_in', 'rnn_conv_w', 'rnn_conv_b', 'lru_w_a', 'lru_b_a', 'lru_w_x', 'lru_b_x', 'lru_lambda', 'sgu_ln_g', 'sgu_ln_b', 'sgu_w_s', 'sgu_b_s', 'w_branch_a', 'w_branch_b', 'w_out', 'norm_ffn_g', 'w_up', 'ffn_conv_w', 'ffn_conv_b', 'w_down', 'norm_final_g', 'loss_target', 'm_w_ada', 'm_b_ada', 'm_norm_mix_g', 'm_w_in', 'm_rnn_conv_w', 'm_rnn_conv_b', 'm_lru_w_a', 'm_lru_b_a', 'm_lru_w_x', 'm_lru_b_x', 'm_lru_lambda', 'm_sgu_ln_g', 'm_sgu_ln_b', 'm_sgu_w_s', 'm_sgu_b_s', 'm_w_branch_a', 'm_w_branch_b', 'm_w_out', 'm_norm_ffn_g', 'm_w_up', 'm_ffn_conv_w', 'm_ffn_conv_b', 'm_w_down', 'm_norm_final_g', 'v_w_ada', 'v_b_ada', 'v_norm_mix_g', 'v_w_in', 'v_rnn_conv_w', 'v_rnn_conv_b', 'v_lru_w_a', 'v_lru_b_a', 'v_lru_w_x', 'v_lru_b_x', 'v_lru_lambda', 'v_sgu_ln_g', 'v_sgu_ln_b', 'v_sgu_w_s', 'v_sgu_b_s', 'v_w_branch_a', 'v_w_branch_b', 'v_w_out', 'v_norm_ffn_g', 'v_w_up', 'v_ffn_conv_w', 'v_ffn_conv_b', 'v_w_down', 'v_norm_final_g']
TWIN_OUTPUTS = ['loss', 'grad_x', 'grad_w_ada', 'grad_b_ada', 'grad_norm_mix_g', 'grad_w_in', 'grad_rnn_conv_w', 'grad_rnn_conv_b', 'grad_lru_w_a', 'grad_lru_b_a', 'grad_lru_w_x', 'grad_lru_b_x', 'grad_lru_lambda', 'grad_sgu_ln_g', 'grad_sgu_ln_b', 'grad_sgu_w_s', 'grad_sgu_b_s', 'grad_w_branch_a', 'grad_w_branch_b', 'grad_w_out', 'grad_norm_ffn_g', 'grad_w_up', 'grad_ffn_conv_w', 'grad_ffn_conv_b', 'grad_w_down', 'grad_norm_final_g', 'delta_w_ada', 'delta_b_ada', 'delta_norm_mix_g', 'delta_w_in', 'delta_rnn_conv_w', 'delta_rnn_conv_b', 'delta_lru_w_a', 'delta_lru_b_a', 'delta_lru_w_x', 'delta_lru_b_x', 'delta_lru_lambda', 'delta_sgu_ln_g', 'delta_sgu_ln_b', 'delta_sgu_w_s', 'delta_sgu_b_s', 'delta_w_branch_a', 'delta_w_branch_b', 'delta_w_out', 'delta_norm_ffn_g', 'delta_w_up', 'delta_ffn_conv_w', 'delta_ffn_conv_b', 'delta_w_down', 'delta_norm_final_g', 'new_m_w_ada', 'new_m_b_ada', 'new_m_norm_mix_g', 'new_m_w_in', 'new_m_rnn_conv_w', 'new_m_rnn_conv_b', 'new_m_lru_w_a', 'new_m_lru_b_a', 'new_m_lru_w_x', 'new_m_lru_b_x', 'new_m_lru_lambda', 'new_m_sgu_ln_g', 'new_m_sgu_ln_b', 'new_m_sgu_w_s', 'new_m_sgu_b_s', 'new_m_w_branch_a', 'new_m_w_branch_b', 'new_m_w_out', 'new_m_norm_ffn_g', 'new_m_w_up', 'new_m_ffn_conv_w', 'new_m_ffn_conv_b', 'new_m_w_down', 'new_m_norm_final_g', 'new_v_w_ada', 'new_v_b_ada', 'new_v_norm_mix_g', 'new_v_w_in', 'new_v_rnn_conv_w', 'new_v_rnn_conv_b', 'new_v_lru_w_a', 'new_v_lru_b_a', 'new_v_lru_w_x', 'new_v_lru_b_x', 'new_v_lru_lambda', 'new_v_sgu_ln_g', 'new_v_sgu_ln_b', 'new_v_sgu_w_s', 'new_v_sgu_b_s', 'new_v_w_branch_a', 'new_v_w_branch_b', 'new_v_w_out', 'new_v_norm_ffn_g', 'new_v_w_up', 'new_v_ffn_conv_w', 'new_v_ffn_conv_b', 'new_v_w_down', 'new_v_norm_final_g']
TWIN_LEAF_KINDS = {'loss': 'loss', 'grad_x': 'grad_x', 'grad_w_ada': 'grad_w', 'grad_b_ada': 'grad_w', 'grad_norm_mix_g': 'grad_w', 'grad_w_in': 'grad_w', 'grad_rnn_conv_w': 'grad_w', 'grad_rnn_conv_b': 'grad_w', 'grad_lru_w_a': 'grad_w', 'grad_lru_b_a': 'grad_w', 'grad_lru_w_x': 'grad_w', 'grad_lru_b_x': 'grad_w', 'grad_lru_lambda': 'grad_w', 'grad_sgu_ln_g': 'grad_w', 'grad_sgu_ln_b': 'grad_w', 'grad_sgu_w_s': 'grad_w', 'grad_sgu_b_s': 'grad_w', 'grad_w_branch_a': 'grad_w', 'grad_w_branch_b': 'grad_w', 'grad_w_out': 'grad_w', 'grad_norm_ffn_g': 'grad_w', 'grad_w_up': 'grad_w', 'grad_ffn_conv_w': 'grad_w', 'grad_ffn_conv_b': 'grad_w', 'grad_w_down': 'grad_w', 'grad_norm_final_g': 'grad_w', 'delta_w_ada': 'delta_w', 'delta_b_ada': 'delta_w', 'delta_norm_mix_g': 'delta_w', 'delta_w_in': 'delta_w', 'delta_rnn_conv_w': 'delta_w', 'delta_rnn_conv_b': 'delta_w', 'delta_lru_w_a': 'delta_w', 'delta_lru_b_a': 'delta_w', 'delta_lru_w_x': 'delta_w', 'delta_lru_b_x': 'delta_w', 'delta_lru_lambda': 'delta_w', 'delta_sgu_ln_g': 'delta_w', 'delta_sgu_ln_b': 'delta_w', 'delta_sgu_w_s': 'delta_w', 'delta_sgu_b_s': 'delta_w', 'delta_w_branch_a': 'delta_w', 'delta_w_branch_b': 'delta_w', 'delta_w_out': 'delta_w', 'delta_norm_ffn_g': 'delta_w', 'delta_w_up': 'delta_w', 'delta_ffn_conv_w': 'delta_w', 'delta_ffn_conv_b': 'delta_w', 'delta_w_down': 'delta_w', 'delta_norm_final_g': 'delta_w', 'new_m_w_ada': 'new_m', 'new_m_b_ada': 'new_m', 'new_m_norm_mix_g': 'new_m', 'new_m_w_in': 'new_m', 'new_m_rnn_conv_w': 'new_m', 'new_m_rnn_conv_b': 'new_m', 'new_m_lru_w_a': 'new_m', 'new_m_lru_b_a': 'new_m', 'new_m_lru_w_x': 'new_m', 'new_m_lru_b_x': 'new_m', 'new_m_lru_lambda': 'new_m', 'new_m_sgu_ln_g': 'new_m', 'new_m_sgu_ln_b': 'new_m', 'new_m_sgu_w_s': 'new_m', 'new_m_sgu_b_s': 'new_m', 'new_m_w_branch_a': 'new_m', 'new_m_w_branch_b': 'new_m', 'new_m_w_out': 'new_m', 'new_m_norm_ffn_g': 'new_m', 'new_m_w_up': 'new_m', 'new_m_ffn_conv_w': 'new_m', 'new_m_ffn_conv_b': 'new_m', 'new_m_w_down': 'new_m', 'new_m_norm_final_g': 'new_m', 'new_v_w_ada': 'new_v', 'new_v_b_ada': 'new_v', 'new_v_norm_mix_g': 'new_v', 'new_v_w_in': 'new_v', 'new_v_rnn_conv_w': 'new_v', 'new_v_rnn_conv_b': 'new_v', 'new_v_lru_w_a': 'new_v', 'new_v_lru_b_a': 'new_v', 'new_v_lru_w_x': 'new_v', 'new_v_lru_b_x': 'new_v', 'new_v_lru_lambda': 'new_v', 'new_v_sgu_ln_g': 'new_v', 'new_v_sgu_ln_b': 'new_v', 'new_v_sgu_w_s': 'new_v', 'new_v_sgu_b_s': 'new_v', 'new_v_w_branch_a': 'new_v', 'new_v_w_branch_b': 'new_v', 'new_v_w_out': 'new_v', 'new_v_norm_ffn_g': 'new_v', 'new_v_w_up': 'new_v', 'new_v_ffn_conv_w': 'new_v', 'new_v_ffn_conv_b': 'new_v', 'new_v_w_down': 'new_v', 'new_v_norm_final_g': 'new_v'}


def _forward(args):
    return _fwd_reference(*[args[k] for k in FWD_PARAMS])


def _output_shape():
    def fwd():
        inp = _fwd_setup_inputs(0)
        return _fwd_reference(*[inp[k] for k in FWD_PARAMS])
    out = _jax.eval_shape(fwd)
    return out.shape, out.dtype

N_MICROBATCH = 1
ADAM_LR = 0.001
ADAM_B1 = 0.9
ADAM_B2 = 0.999
ADAM_EPS = 1e-08
ADAM_WD = 0.01
ADAM_STEP = 10
PER_EXAMPLE_BATCH_AXIS = {'x': 0, 'c': 0, 'loss_target': 0}
SHARED_INPUTS = []
_WEIGHT_DTYPES = {'w_ada': _jnp.float32, 'b_ada': _jnp.float32, 'norm_mix_g': _jnp.float32, 'w_in': _jnp.float32, 'rnn_conv_w': _jnp.float32, 'rnn_conv_b': _jnp.float32, 'lru_w_a': _jnp.float32, 'lru_b_a': _jnp.float32, 'lru_w_x': _jnp.float32, 'lru_b_x': _jnp.float32, 'lru_lambda': _jnp.float32, 'sgu_ln_g': _jnp.float32, 'sgu_ln_b': _jnp.float32, 'sgu_w_s': _jnp.float32, 'sgu_b_s': _jnp.float32, 'w_branch_a': _jnp.float32, 'w_branch_b': _jnp.float32, 'w_out': _jnp.float32, 'norm_ffn_g': _jnp.float32, 'w_up': _jnp.float32, 'ffn_conv_w': _jnp.float32, 'ffn_conv_b': _jnp.float32, 'w_down': _jnp.float32, 'norm_final_g': _jnp.float32}
MOMENT_SCALE = {'w_ada': 4.824586e-01, 'b_ada': 1.038882e+00, 'norm_mix_g': 2.561106e-01, 'w_in': 2.397016e-01, 'rnn_conv_w': 6.339265e-01, 'rnn_conv_b': 8.722702e-01, 'lru_w_a': 5.368442e-02, 'lru_b_a': 8.141718e-02, 'lru_w_x': 1.198776e-01, 'lru_b_x': 2.115952e-01, 'lru_lambda': 2.502228e-01, 'sgu_ln_g': 4.483251e-02, 'sgu_ln_b': 4.446879e-02, 'sgu_w_s': 4.437520e-02, 'sgu_b_s': 6.762570e-02, 'w_branch_a': 4.787645e-01, 'w_branch_b': 9.675316e-02, 'w_out': 4.801644e-01, 'norm_ffn_g': 1.634627e-01, 'w_up': 7.449098e-02, 'ffn_conv_w': 7.552902e-02, 'ffn_conv_b': 6.567034e-02, 'w_down': 1.305631e-01, 'norm_final_g': 6.568924e+01}


def _to_microbatches(a, axis):
    t = _jnp.moveaxis(a, axis, 0)
    t = t.reshape((N_MICROBATCH, t.shape[0] // N_MICROBATCH) + t.shape[1:])
    return _jnp.moveaxis(t, 1, axis + 1)


def setup_inputs(seed: int = 0) -> dict:
    inp = _fwd_setup_inputs(seed)
    key = _jax.random.fold_in(_jax.random.key(seed), 7919)
    shape, _ = _output_shape()
    out = dict(inp)
    out["loss_target"] = _jax.random.normal(_jax.random.fold_in(key, 0), shape, _jnp.float32)
    for i, name in enumerate(TWIN_WEIGHTS):
        w = inp[name].astype(_jnp.float32)
        if MOMENT_SCALE is None:
            s = _jnp.sqrt(_jnp.mean(_jnp.square(w)) + 1e-30)
        else:
            s = MOMENT_SCALE[name]
        km, kv = _jax.random.split(_jax.random.fold_in(key, i + 1))
        out[name] = w
        out["m_" + name] = s * _jax.random.normal(km, w.shape, _jnp.float32)
        out["v_" + name] = (s * s) * _jax.random.uniform(kv, w.shape, _jnp.float32, 0.5, 1.5)
    if N_MICROBATCH > 1:
        for name, axis in PER_EXAMPLE_BATCH_AXIS.items():
            out[name] = _to_microbatches(out[name], axis)
    return {'x': out['x'], 'c': out['c'], 'w_ada': out['w_ada'], 'b_ada': out['b_ada'], 'norm_mix_g': out['norm_mix_g'], 'w_in': out['w_in'], 'rnn_conv_w': out['rnn_conv_w'], 'rnn_conv_b': out['rnn_conv_b'], 'lru_w_a': out['lru_w_a'], 'lru_b_a': out['lru_b_a'], 'lru_w_x': out['lru_w_x'], 'lru_b_x': out['lru_b_x'], 'lru_lambda': out['lru_lambda'], 'sgu_ln_g': out['sgu_ln_g'], 'sgu_ln_b': out['sgu_ln_b'], 'sgu_w_s': out['sgu_w_s'], 'sgu_b_s': out['sgu_b_s'], 'w_branch_a': out['w_branch_a'], 'w_branch_b': out['w_branch_b'], 'w_out': out['w_out'], 'norm_ffn_g': out['norm_ffn_g'], 'w_up': out['w_up'], 'ffn_conv_w': out['ffn_conv_w'], 'ffn_conv_b': out['ffn_conv_b'], 'w_down': out['w_down'], 'norm_final_g': out['norm_final_g'], 'loss_target': out['loss_target'], 'm_w_ada': out['m_w_ada'], 'm_b_ada': out['m_b_ada'], 'm_norm_mix_g': out['m_norm_mix_g'], 'm_w_in': out['m_w_in'], 'm_rnn_conv_w': out['m_rnn_conv_w'], 'm_rnn_conv_b': out['m_rnn_conv_b'], 'm_lru_w_a': out['m_lru_w_a'], 'm_lru_b_a': out['m_lru_b_a'], 'm_lru_w_x': out['m_lru_w_x'], 'm_lru_b_x': out['m_lru_b_x'], 'm_lru_lambda': out['m_lru_lambda'], 'm_sgu_ln_g': out['m_sgu_ln_g'], 'm_sgu_ln_b': out['m_sgu_ln_b'], 'm_sgu_w_s': out['m_sgu_w_s'], 'm_sgu_b_s': out['m_sgu_b_s'], 'm_w_branch_a': out['m_w_branch_a'], 'm_w_branch_b': out['m_w_branch_b'], 'm_w_out': out['m_w_out'], 'm_norm_ffn_g': out['m_norm_ffn_g'], 'm_w_up': out['m_w_up'], 'm_ffn_conv_w': out['m_ffn_conv_w'], 'm_ffn_conv_b': out['m_ffn_conv_b'], 'm_w_down': out['m_w_down'], 'm_norm_final_g': out['m_norm_final_g'], 'v_w_ada': out['v_w_ada'], 'v_b_ada': out['v_b_ada'], 'v_norm_mix_g': out['v_norm_mix_g'], 'v_w_in': out['v_w_in'], 'v_rnn_conv_w': out['v_rnn_conv_w'], 'v_rnn_conv_b': out['v_rnn_conv_b'], 'v_lru_w_a': out['v_lru_w_a'], 'v_lru_b_a': out['v_lru_b_a'], 'v_lru_w_x': out['v_lru_w_x'], 'v_lru_b_x': out['v_lru_b_x'], 'v_lru_lambda': out['v_lru_lambda'], 'v_sgu_ln_g': out['v_sgu_ln_g'], 'v_sgu_ln_b': out['v_sgu_ln_b'], 'v_sgu_w_s': out['v_sgu_w_s'], 'v_sgu_b_s': out['v_sgu_b_s'], 'v_w_branch_a': out['v_w_branch_a'], 'v_w_branch_b': out['v_w_branch_b'], 'v_w_out': out['v_w_out'], 'v_norm_ffn_g': out['v_norm_ffn_g'], 'v_w_up': out['v_w_up'], 'v_ffn_conv_w': out['v_ffn_conv_w'], 'v_ffn_conv_b': out['v_ffn_conv_b'], 'v_w_down': out['v_w_down'], 'v_norm_final_g': out['v_norm_final_g']}


def _loss(weights, diff, rest, loss_target):
    with _jax.named_scope("forward"):
        args = {**rest, TWIN_DIFF_INPUT: diff, **{k: w.astype(_WEIGHT_DTYPES[k]) for k, w in weights.items()}}
        y = _forward(args)
    with _jax.named_scope("loss_head"):
        err = _jnp.square(y.astype(_jnp.float32) - loss_target)
        return 0.5 * _jnp.sum(_jnp.mean(err, axis=-1)) if err.ndim else 0.5 * err


def _adamw(w, g, m, v):
    m = ADAM_B1 * m + (1.0 - ADAM_B1) * g
    v = ADAM_B2 * v + (1.0 - ADAM_B2) * _jnp.square(g)
    m_hat = m / (1.0 - ADAM_B1 ** ADAM_STEP)
    v_hat = v / (1.0 - ADAM_B2 ** ADAM_STEP)
    delta = -ADAM_LR * (m_hat / (_jnp.sqrt(v_hat) + ADAM_EPS) + ADAM_WD * w)
    return delta, m, v


def reference(x, c, w_ada, b_ada, norm_mix_g, w_in, rnn_conv_w, rnn_conv_b, lru_w_a, lru_b_a, lru_w_x, lru_b_x, lru_lambda, sgu_ln_g, sgu_ln_b, sgu_w_s, sgu_b_s, w_branch_a, w_branch_b, w_out, norm_ffn_g, w_up, ffn_conv_w, ffn_conv_b, w_down, norm_final_g, loss_target, m_w_ada, m_b_ada, m_norm_mix_g, m_w_in, m_rnn_conv_w, m_rnn_conv_b, m_lru_w_a, m_lru_b_a, m_lru_w_x, m_lru_b_x, m_lru_lambda, m_sgu_ln_g, m_sgu_ln_b, m_sgu_w_s, m_sgu_b_s, m_w_branch_a, m_w_branch_b, m_w_out, m_norm_ffn_g, m_w_up, m_ffn_conv_w, m_ffn_conv_b, m_w_down, m_norm_final_g, v_w_ada, v_b_ada, v_norm_mix_g, v_w_in, v_rnn_conv_w, v_rnn_conv_b, v_lru_w_a, v_lru_b_a, v_lru_w_x, v_lru_b_x, v_lru_lambda, v_sgu_ln_g, v_sgu_ln_b, v_sgu_w_s, v_sgu_b_s, v_w_branch_a, v_w_branch_b, v_w_out, v_norm_ffn_g, v_w_up, v_ffn_conv_w, v_ffn_conv_b, v_w_down, v_norm_final_g):
    given = dict(x=x, c=c, w_ada=w_ada, b_ada=b_ada, norm_mix_g=norm_mix_g, w_in=w_in, rnn_conv_w=rnn_conv_w, rnn_conv_b=rnn_conv_b, lru_w_a=lru_w_a, lru_b_a=lru_b_a, lru_w_x=lru_w_x, lru_b_x=lru_b_x, lru_lambda=lru_lambda, sgu_ln_g=sgu_ln_g, sgu_ln_b=sgu_ln_b, sgu_w_s=sgu_w_s, sgu_b_s=sgu_b_s, w_branch_a=w_branch_a, w_branch_b=w_branch_b, w_out=w_out, norm_ffn_g=norm_ffn_g, w_up=w_up, ffn_conv_w=ffn_conv_w, ffn_conv_b=ffn_conv_b, w_down=w_down, norm_final_g=norm_final_g, loss_target=loss_target, m_w_ada=m_w_ada, m_b_ada=m_b_ada, m_norm_mix_g=m_norm_mix_g, m_w_in=m_w_in, m_rnn_conv_w=m_rnn_conv_w, m_rnn_conv_b=m_rnn_conv_b, m_lru_w_a=m_lru_w_a, m_lru_b_a=m_lru_b_a, m_lru_w_x=m_lru_w_x, m_lru_b_x=m_lru_b_x, m_lru_lambda=m_lru_lambda, m_sgu_ln_g=m_sgu_ln_g, m_sgu_ln_b=m_sgu_ln_b, m_sgu_w_s=m_sgu_w_s, m_sgu_b_s=m_sgu_b_s, m_w_branch_a=m_w_branch_a, m_w_branch_b=m_w_branch_b, m_w_out=m_w_out, m_norm_ffn_g=m_norm_ffn_g, m_w_up=m_w_up, m_ffn_conv_w=m_ffn_conv_w, m_ffn_conv_b=m_ffn_conv_b, m_w_down=m_w_down, m_norm_final_g=m_norm_final_g, v_w_ada=v_w_ada, v_b_ada=v_b_ada, v_norm_mix_g=v_norm_mix_g, v_w_in=v_w_in, v_rnn_conv_w=v_rnn_conv_w, v_rnn_conv_b=v_rnn_conv_b, v_lru_w_a=v_lru_w_a, v_lru_b_a=v_lru_b_a, v_lru_w_x=v_lru_w_x, v_lru_b_x=v_lru_b_x, v_lru_lambda=v_lru_lambda, v_sgu_ln_g=v_sgu_ln_g, v_sgu_ln_b=v_sgu_ln_b, v_sgu_w_s=v_sgu_w_s, v_sgu_b_s=v_sgu_b_s, v_w_branch_a=v_w_branch_a, v_w_branch_b=v_w_branch_b, v_w_out=v_w_out, v_norm_ffn_g=v_norm_ffn_g, v_w_up=v_w_up, v_ffn_conv_w=v_ffn_conv_w, v_ffn_conv_b=v_ffn_conv_b, v_w_down=v_w_down, v_norm_final_g=v_norm_final_g)
    weights = {n: given[n] for n in TWIN_WEIGHTS}
    shared = {n: given[n] for n in SHARED_INPUTS}
    per_example = {n: given[n] for n in ['x', 'c']}
    grad_fn = _jax.value_and_grad(_loss, argnums=(0, 1))

    def one_microbatch(ex, loss_target):
        ex = dict(ex)
        diff = ex.pop(TWIN_DIFF_INPUT)
        return grad_fn(weights, diff, {**shared, **ex}, loss_target)

    if N_MICROBATCH == 1:
        loss, (grad_w, grad_x) = one_microbatch(per_example, given["loss_target"])
    else:
        def body(carry, xs):
            loss_sum, grad_sum = carry
            l_k, (gw_k, gx_k) = one_microbatch(xs[0], xs[1])
            with _jax.named_scope("update"):
                return (loss_sum + l_k, _jax.tree.map(_jnp.add, grad_sum, gw_k)), gx_k

        init = (_jnp.zeros((), _jnp.float32), _jax.tree.map(_jnp.zeros_like, weights))
        (loss, grad_w), grad_x = _jax.lax.scan(body, init, (per_example, given["loss_target"]))
    with _jax.named_scope("update"):
        delta_w, new_m, new_v = {}, {}, {}
        for n in TWIN_WEIGHTS:
            delta_w[n], new_m[n], new_v[n] = _adamw(weights[n], grad_w[n], given["m_" + n], given["v_" + n])
    return (loss, grad_x, *[grad_w[n] for n in TWIN_WEIGHTS], *[delta_w[n] for n in TWIN_WEIGHTS],
            *[new_m[n] for n in TWIN_WEIGHTS], *[new_v[n] for n in TWIN_WEIGHTS])
```

```python
import math

import jax
import jax.numpy as jnp
from jax import lax
from jax.experimental import pallas as pl
from jax.experimental.pallas import tpu as pltpu

F32 = jnp.float32
BF16 = jnp.bfloat16
MESH_IDS = pl.DeviceIdType.MESH

D = 1024
NH = 8
HD = 128
NCOL_IN = 6 * D
DFF = 3 * D
N_DEV = 8
EPS = 1e-6
LRU_C = 8.0
ADAM_LR, ADAM_B1, ADAM_B2, ADAM_EPS, ADAM_WD, ADAM_STEP = 0.001, 0.9, 0.999, 1e-08, 0.01, 10

SUBLANES = 8
HALO = 16
VMEM_LIMIT = 56 * 1024 * 1024
GELU_K = math.sqrt(2.0 / math.pi)
GELU_C = 0.044715


def _cparams(n_axes):
    return pltpu.CompilerParams(dimension_semantics=("arbitrary",) * n_axes, vmem_limit_bytes=VMEM_LIMIT)


def _const_spec(shape, single_buffer=False):
    nd = len(shape)
    if single_buffer:
        return pl.BlockSpec(shape, lambda *_: (0,) * nd, pipeline_mode=pl.Buffered(1))
    return pl.BlockSpec(shape, lambda *_: (0,) * nd)


def _tile_big(t):
    return min(512, t)


def _tile_seq(t):
    return min(256, t)


def _row_tile(rows):
    if rows <= 512:
        return rows
    return next(tr for tr in range(512, 0, -SUBLANES) if rows % tr == 0)


def _gelu_t(x):
    t = jnp.tanh(GELU_K * (x + GELU_C * (x * x * x)))
    return 0.5 * x * (1.0 + t), t


def _gelu_grad(x, t):
    return 0.5 * (1.0 + t) + 0.5 * x * (1.0 - t * t) * (GELU_K * (1.0 + 3.0 * GELU_C * x * x))


def _sigmoid(x):
    return 1.0 / (1.0 + jnp.exp(-x))


def _log_sigmoid(x):
    return -(jnp.maximum(-x, 0.0) + jnp.log1p(jnp.exp(-jnp.abs(x))))


def _row_iota(cols):
    return lax.broadcasted_iota(jnp.int32, (SUBLANES, cols), 0)


def _shift_down(x, k, prev8):
    if k == 0:
        return x
    r = pltpu.roll(x, k, 0)
    p = pltpu.roll(prev8, k, 0)
    head = jnp.where(_row_iota(x.shape[1]) < k, p, r[:SUBLANES])
    return jnp.concatenate([head, r[SUBLANES:]], axis=0)


def _shift_up(x, k, next8):
    if k == 0:
        return x
    n = x.shape[0]
    r = pltpu.roll(x, n - k, 0)
    q = pltpu.roll(next8, SUBLANES - k, 0)
    tail = jnp.where(_row_iota(x.shape[1]) >= SUBLANES - k, q, r[n - SUBLANES:])
    return jnp.concatenate([r[:n - SUBLANES], tail], axis=0)


def _heads_nn(x_bf, w_ref):
    return jnp.concatenate(
        [jnp.dot(x_bf[:, h * HD:(h + 1) * HD], w_ref[h], preferred_element_type=F32) for h in range(NH)], axis=1)


def _heads_nt(x_bf, w_ref):
    return jnp.concatenate(
        [lax.dot_general(x_bf[:, h * HD:(h + 1) * HD], w_ref[h], (((1,), (1,)), ((), ())), preferred_element_type=F32)
         for h in range(NH)], axis=1)


def _dot_nt(a, b):
    return lax.dot_general(a, b, (((1,), (1,)), ((), ())), preferred_element_type=F32)


def _dot_tn(a, b):
    return lax.dot_general(a, b, (((0,), (0,)), ((), ())), preferred_element_type=F32)


def _colsum(x):
    return jnp.sum(x, axis=0, keepdims=True)


def _prev_halo_map(tm, col):
    return lambda i, *_: (jnp.maximum(i * (tm // HALO) - 1, 0), col)


def _norm_proj(x, g, scale, shift, w, name):
    t, n = x.shape[0], w.shape[1]
    tm = _tile_big(t)

    def body(x_ref, g_ref, sc_ref, sh_ref, w_ref, h_ref, z_ref):
        xv = x_ref[...]
        r = lax.rsqrt(jnp.mean(xv * xv, axis=-1, keepdims=True) + EPS)
        hb = ((xv * r * g_ref[...]) * (1.0 + sc_ref[...]) + sh_ref[...]).astype(BF16)
        h_ref[...] = hb
        for c0 in range(0, n, D):
            z_ref[:, c0:c0 + D] = jnp.dot(hb, w_ref[:, c0:c0 + D], preferred_element_type=F32).astype(BF16)

    vec = _const_spec((1, D))
    return pl.pallas_call(
        body, name=name, grid=(t // tm,),
        in_specs=[pl.BlockSpec((tm, D), lambda i: (i, 0)), vec, vec, vec, _const_spec((D, n), True)],
        out_specs=[pl.BlockSpec((tm, D), lambda i: (i, 0)), pl.BlockSpec((tm, n), lambda i: (i, 0))],
        out_shape=[jax.ShapeDtypeStruct((t, D), BF16), jax.ShapeDtypeStruct((t, n), BF16)],
        compiler_params=_cparams(1),
    )(x, g, scale, shift, w)


def _lru_gates(xc, wa_ref, ba, wx_ref, bx, ls):
    xb = xc.astype(BF16)
    ra = _sigmoid(_heads_nn(xb, wa_ref) + ba)
    ia = _sigmoid(_heads_nn(xb, wx_ref) + bx)
    la = LRU_C * ra * ls
    a = jnp.exp(la)
    mult = jnp.sqrt(-jnp.tanh(la) * (1.0 + a * a))
    return ra, ia, a, mult


def _conv4(xr, prev8, cw_ref, cb):
    return (cb + cw_ref[3:4, :] * xr + cw_ref[2:3, :] * _shift_down(xr, 1, prev8)
            + cw_ref[1:2, :] * _shift_down(xr, 2, prev8) + cw_ref[0:1, :] * _shift_down(xr, 3, prev8))


def _rnn_fwd(z, cw, cb, wa, ba, wx, bx, lam):
    t = z.shape[0]
    tm = _tile_seq(t)
    ngrp = tm // SUBLANES

    def body(xr_ref, xp_ref, gr_ref, cw_ref, cb_ref, wa_ref, ba_ref, wx_ref, bx_ref, lam_ref,
             h_ref, ya_ref, carry_ref, a_scr, u_scr):
        i = pl.program_id(0)

        @pl.when(i == 0)
        def _():
            carry_ref[...] = jnp.zeros_like(carry_ref)

        xr = xr_ref[...].astype(F32)
        prev8 = jnp.where(i == 0, 0.0, xp_ref[...].astype(F32)[HALO - SUBLANES:])
        xc = _conv4(xr, prev8, cw_ref, cb_ref[...])
        _, ia, a, mult = _lru_gates(xc, wa_ref, ba_ref[...], wx_ref, bx_ref[...], _log_sigmoid(lam_ref[...]))
        a_scr[...] = a
        u_scr[...] = mult * (ia * xc)
        row = _row_iota(D)

        def grp(j, carry):
            r0 = pl.multiple_of(j * SUBLANES, SUBLANES)
            av = a_scr[pl.ds(r0, SUBLANES), :]
            uv = u_scr[pl.ds(r0, SUBLANES), :]
            for d in (1, 2, 4):
                m = row >= d
                uv = jnp.where(m, av * pltpu.roll(uv, d, 0) + uv, uv)
                av = jnp.where(m, av * pltpu.roll(av, d, 0), av)
            hv = uv + av * carry
            h_ref[pl.ds(r0, SUBLANES), :] = hv
            return hv[SUBLANES - 1:SUBLANES, :]

        carry_ref[0:1, :] = lax.fori_loop(0, ngrp, grp, carry_ref[0:1, :])
        gg, _ = _gelu_t(gr_ref[...].astype(F32))
        ya_ref[...] = (h_ref[...] * gg).astype(BF16)

    vec = _const_spec((1, D))
    wspec = _const_spec((NH, HD, HD))
    return pl.pallas_call(
        body, name="rnn_fwd", grid=(t // tm,),
        in_specs=[pl.BlockSpec((tm, D), lambda i: (i, 0)), pl.BlockSpec((HALO, D), _prev_halo_map(tm, 0)),
                  pl.BlockSpec((tm, D), lambda i: (i, 1)), _const_spec((4, D)), vec, wspec, vec, wspec, vec, vec],
        out_specs=[pl.BlockSpec((tm, D), lambda i: (i, 0)), pl.BlockSpec((tm, D), lambda i: (i, 0))],
        out_shape=[jax.ShapeDtypeStruct((t, D), F32), jax.ShapeDtypeStruct((t, D), BF16)],
        scratch_shapes=[pltpu.VMEM((SUBLANES, D), F32), pltpu.VMEM((tm, D), F32), pltpu.VMEM((tm, D), F32)],
        compiler_params=_cparams(1),
    )(z, z, z, cw, cb, wa, ba, wx, bx, lam)


def _sgu_core(zu, zv, lng, lnb, wm_ref, bst_ref):
    gu, tu = _gelu_t(zu)
    gv, tv = _gelu_t(zv)
    mu = jnp.mean(gv, axis=-1, keepdims=True)
    cen = gv - mu
    rstd = lax.rsqrt(jnp.mean(cen * cen, axis=-1, keepdims=True) + EPS)
    vhat = cen * rstd
    vln = vhat * lng + lnb
    vb = vln.astype(BF16)
    rows = []
    for b0 in range(0, zu.shape[0], HD):
        rows.append(jnp.concatenate(
            [jnp.dot(wm_ref[g], vb[b0:b0 + HD, g * HD:(g + 1) * HD], preferred_element_type=F32)
             + bst_ref[:, g:g + 1] for g in range(NH)], axis=1))
    mixed = jnp.concatenate(rows, axis=0) if len(rows) > 1 else rows[0]
    return gu, tu, tv, rstd, vhat, vb, mixed


def _sgu_fwd(z, lng, lnb, wm, bst):
    t = z.shape[0]
    tm = _tile_seq(t)

    def body(zu_ref, zv_ref, lng_ref, lnb_ref, wm_ref, bst_ref, yb_ref):
        gu, _, _, _, _, _, mixed = _sgu_core(zu_ref[...].astype(F32), zv_ref[...].astype(F32),
                                             lng_ref[...], lnb_ref[...], wm_ref, bst_ref)
        yb_ref[...] = (gu * mixed).astype(BF16)

    vec = _const_spec((1, D))
    return pl.pallas_call(
        body, name="sgu_fwd", grid=(t // tm,),
        in_specs=[pl.BlockSpec((tm, D), lambda i: (i, 2)), pl.BlockSpec((tm, D), lambda i: (i, 3)), vec, vec,
                  _const_spec((NH, HD, HD)), _const_spec((HD, NH))],
        out_specs=pl.BlockSpec((tm, D), lambda i: (i, 0)),
        out_shape=jax.ShapeDtypeStruct((t, D), BF16),
        compiler_params=_cparams(1),
    )(z, z, lng, lnb, wm, bst)


def _merge_fwd(ya_pre, yb_pre, z, x, gate1, wba, wbb, wout):
    t = x.shape[0]
    tm = _tile_big(t)

    def body(yap_ref, ybp_ref, ga_ref, gb_ref, x_ref, g1_ref, wba_ref, wbb_ref, wo_ref,
             x2_ref, ya_ref, yb_ref, mg_ref, o1_ref):
        ya = jnp.dot(yap_ref[...], wba_ref[...], preferred_element_type=F32)
        yb = jnp.dot(ybp_ref[...], wbb_ref[...], preferred_element_type=F32)
        merged = _sigmoid(ga_ref[...].astype(F32)) * ya + _sigmoid(gb_ref[...].astype(F32)) * yb
        mb = merged.astype(BF16)
        o1 = jnp.dot(mb, wo_ref[...], preferred_element_type=F32)
        x2_ref[...] = x_ref[...] + g1_ref[...] * o1
        ya_ref[...] = ya.astype(BF16)
        yb_ref[...] = yb.astype(BF16)
        mg_ref[...] = mb
        o1_ref[...] = o1.astype(BF16)

    tile = pl.BlockSpec((tm, D), lambda i: (i, 0))
    wspec = _const_spec((D, D))
    bshape = jax.ShapeDtypeStruct((t, D), BF16)
    return pl.pallas_call(
        body, name="merge_fwd", grid=(t // tm,),
        in_specs=[tile, tile, pl.BlockSpec((tm, D), lambda i: (i, 4)), pl.BlockSpec((tm, D), lambda i: (i, 5)),
                  tile, _const_spec((1, D)), wspec, wspec, wspec],
        out_specs=[tile] * 5,
        out_shape=[jax.ShapeDtypeStruct((t, D), F32), bshape, bshape, bshape, bshape],
        compiler_params=_cparams(1),
    )(ya_pre, yb_pre, z, z, x, gate1, wba, wbb, wout)


def _conv3(u, prev8, cw_ref, cb):
    return cb + cw_ref[2:3, :] * u + cw_ref[1:2, :] * _shift_down(u, 1, prev8) + cw_ref[0:1, :] * _shift_down(u, 2, prev8)


def _ffn_mid_fwd(up, cw, cb):
    t = up.shape[0]
    tm = _tile_big(t)
    nc = DFF // D

    def body(ua_ref, uap_ref, uv_ref, uvp_ref, cwa_ref, cwv_ref, cba_ref, cbv_ref, ff_ref):
        first = pl.program_id(0) == 0
        pa = jnp.where(first, 0.0, uap_ref[...].astype(F32)[HALO - SUBLANES:])
        pv = jnp.where(first, 0.0, uvp_ref[...].astype(F32)[HALO - SUBLANES:])
        act = _conv3(ua_ref[...].astype(F32), pa, cwa_ref, cba_ref[...])
        val = _conv3(uv_ref[...].astype(F32), pv, cwv_ref, cbv_ref[...])
        ga, _ = _gelu_t(act)
        ff_ref[...] = (ga * val).astype(BF16)

    return pl.pallas_call(
        body, name="ffn_mid_fwd", grid=(t // tm, nc),
        in_specs=[pl.BlockSpec((tm, D), lambda i, c: (i, c)),
                  pl.BlockSpec((HALO, D), lambda i, c: (jnp.maximum(i * (tm // HALO) - 1, 0), c)),
                  pl.BlockSpec((tm, D), lambda i, c: (i, nc + c)),
                  pl.BlockSpec((HALO, D), lambda i, c: (jnp.maximum(i * (tm // HALO) - 1, 0), nc + c)),
                  pl.BlockSpec((3, D), lambda i, c: (0, c)), pl.BlockSpec((3, D), lambda i, c: (0, nc + c)),
                  pl.BlockSpec((1, D), lambda i, c: (0, c)), pl.BlockSpec((1, D), lambda i, c: (0, nc + c))],
        out_specs=pl.BlockSpec((tm, D), lambda i, c: (i, c)),
        out_shape=jax.ShapeDtypeStruct((t, DFF), BF16),
        compiler_params=_cparams(2),
    )(up, up, up, up, cw, cw, cb, cb)


def _ffn_out_loss(ff, wd, x2, target, gate2, gfin):
    t = x2.shape[0]
    tm = _tile_big(t)

    def body(ff_ref, wd_ref, x2_ref, tg_ref, g2_ref, gf_ref, dx3_ref, loss_ref, dgf_ref, dg2_ref):
        @pl.when(pl.program_id(0) == 0)
        def _():
            loss_ref[...] = jnp.zeros_like(loss_ref)
            dgf_ref[...] = jnp.zeros_like(dgf_ref)
            dg2_ref[...] = jnp.zeros_like(dg2_ref)

        o2 = jnp.dot(ff_ref[...], wd_ref[...], preferred_element_type=F32)
        x3 = x2_ref[...] + g2_ref[...] * o2
        r = lax.rsqrt(jnp.mean(x3 * x3, axis=-1, keepdims=True) + EPS)
        xhat = x3 * r
        err = xhat * gf_ref[...] - tg_ref[...]
        loss_ref[...] += 0.5 * jnp.sum(jnp.mean(err * err, axis=-1, keepdims=True), axis=0, keepdims=True)
        dy = err * (1.0 / D)
        dgf_ref[...] += _colsum(dy * xhat)
        dxh = dy * gf_ref[...]
        dx3 = r * (dxh - xhat * jnp.mean(dxh * xhat, axis=-1, keepdims=True))
        dx3_ref[...] = dx3
        dg2_ref[...] += _colsum(dx3 * o2)

    tile = pl.BlockSpec((tm, D), lambda i: (i, 0))
    vec = _const_spec((1, D))
    return pl.pallas_call(
        body, name="ffn_out_loss", grid=(t // tm,),
        in_specs=[pl.BlockSpec((tm, DFF), lambda i: (i, 0)), _const_spec((DFF, D), True), tile, tile, vec, vec],
        out_specs=[tile, _const_spec((1, 1)), vec, vec],
        out_shape=[jax.ShapeDtypeStruct((t, D), F32), jax.ShapeDtypeStruct((1, 1), F32),
                   jax.ShapeDtypeStruct((1, D), F32), jax.ShapeDtypeStruct((1, D), F32)],
        compiler_params=_cparams(1),
    )(ff, wd, x2, target, gate2, gfin)


def _ffn_down_bwd(dx3, gate2, ff, up, cw, cb, wd):
    t = dx3.shape[0]
    tm = _tile_big(t)
    nc = DFF // D

    def body(dx3_ref, g2_ref, ff_ref, ua_ref, uap_ref, uv_ref, uvp_ref, cwa_ref, cwv_ref, cba_ref, cbv_ref, wd_ref,
             da_ref, dv_ref, dwd_ref, dcwa_ref, dcwv_ref, dcba_ref, dcbv_ref):
        i = pl.program_id(1)

        @pl.when(i == 0)
        def _():
            for r in (dwd_ref, dcwa_ref, dcwv_ref, dcba_ref, dcbv_ref):
                r[...] = jnp.zeros_like(r)

        first = i == 0
        ua = ua_ref[...].astype(F32)
        uv = uv_ref[...].astype(F32)
        pa = jnp.where(first, 0.0, uap_ref[...].astype(F32)[HALO - SUBLANES:])
        pv = jnp.where(first, 0.0, uvp_ref[...].astype(F32)[HALO - SUBLANES:])
        act = _conv3(ua, pa, cwa_ref, cba_ref[...])
        val = _conv3(uv, pv, cwv_ref, cbv_ref[...])
        do2 = (dx3_ref[...] * g2_ref[...]).astype(BF16)
        dwd_ref[...] += _dot_tn(ff_ref[...], do2)
        dff = _dot_nt(do2, wd_ref[...])
        ga, ta = _gelu_t(act)
        dact = dff * val * _gelu_grad(act, ta)
        dval = dff * ga
        da_ref[...] = dact.astype(BF16)
        dv_ref[...] = dval.astype(BF16)
        dcba_ref[...] += _colsum(dact)
        dcbv_ref[...] += _colsum(dval)
        for k in range(3):
            dcwa_ref[k:k + 1, :] += _colsum(dact * _shift_down(ua, 2 - k, pa))
            dcwv_ref[k:k + 1, :] += _colsum(dval * _shift_down(uv, 2 - k, pv))

    def halo(col):
        return lambda c, i: (jnp.maximum(i * (tm // HALO) - 1, 0), col(c))

    return pl.pallas_call(
        body, name="ffn_down_bwd", grid=(nc, t // tm),
        in_specs=[pl.BlockSpec((tm, D), lambda c, i: (i, 0)), pl.BlockSpec((1, D), lambda c, i: (0, 0)),
                  pl.BlockSpec((tm, D), lambda c, i: (i, c)),
                  pl.BlockSpec((tm, D), lambda c, i: (i, c)), pl.BlockSpec((HALO, D), halo(lambda c: c)),
                  pl.BlockSpec((tm, D), lambda c, i: (i, nc + c)), pl.BlockSpec((HALO, D), halo(lambda c: nc + c)),
                  pl.BlockSpec((3, D), lambda c, i: (0, c)), pl.BlockSpec((3, D), lambda c, i: (0, nc + c)),
                  pl.BlockSpec((1, D), lambda c, i: (0, c)), pl.BlockSpec((1, D), lambda c, i: (0, nc + c)),
                  pl.BlockSpec((D, D), lambda c, i: (c, 0))],
        out_specs=[pl.BlockSpec((tm, D), lambda c, i: (i, c)), pl.BlockSpec((tm, D), lambda c, i: (i, c)),
                   pl.BlockSpec((D, D), lambda c, i: (c, 0)),
                   pl.BlockSpec((3, D), lambda c, i: (0, c)), pl.BlockSpec((3, D), lambda c, i: (0, c)),
                   pl.BlockSpec((1, D), lambda c, i: (0, c)), pl.BlockSpec((1, D), lambda c, i: (0, c))],
        out_shape=[jax.ShapeDtypeStruct((t, DFF), BF16), jax.ShapeDtypeStruct((t, DFF), BF16),
                   jax.ShapeDtypeStruct((DFF, D), F32),
                   jax.ShapeDtypeStruct((3, DFF), F32), jax.ShapeDtypeStruct((3, DFF), F32),
                   jax.ShapeDtypeStruct((1, DFF), F32), jax.ShapeDtypeStruct((1, DFF), F32)],
        compiler_params=_cparams(2),
    )(dx3, gate2, ff, up, up, up, up, cw, cw, cb, cb, wd)


def _modnorm_bwd(dh, xv, g, scale):
    r = lax.rsqrt(jnp.mean(xv * xv, axis=-1, keepdims=True) + EPS)
    xhat = xv * r
    dxn = dh * (1.0 + scale)
    dxh = dxn * g
    dx = r * (dxh - xhat * jnp.mean(dxh * xhat, axis=-1, keepdims=True))
    return dx, _colsum(dh), _colsum(dh * (xhat * g)), _colsum(dxn * xhat)


def _ffn_up_bwd(dact, dval, cw, wup, x2, dx3, gffn, scale2, o1, gate1):
    t = x2.shape[0]
    tm = _tile_seq(t)
    nt = t // tm
    nc = DFF // D

    def body(da_ref, dan_ref, dv_ref, dvn_ref, cw_ref, w_ref, x2_ref, dx3_ref, g_ref, sc_ref, o1_ref, g1_ref,
             dup_ref, dx2_ref, do1_ref, dsh_ref, dsc_ref, dg_ref, dg1_ref):
        i = pl.program_id(0)

        @pl.when(i == 0)
        def _():
            for r in (dsh_ref, dsc_ref, dg_ref, dg1_ref):
                r[...] = jnp.zeros_like(r)

        last = i == nt - 1
        dh = jnp.zeros((tm, D), F32)
        for half, (d_ref, dn_ref) in enumerate(((da_ref, dan_ref), (dv_ref, dvn_ref))):
            nxt = jnp.where(last, 0.0, dn_ref[...].astype(F32)[:SUBLANES])
            for c in range(nc):
                c0 = half * DFF + c * D
                dv = d_ref[:, c * D:(c + 1) * D].astype(F32)
                nx = nxt[:, c * D:(c + 1) * D]
                dup = (cw_ref[2:3, c0:c0 + D] * dv + cw_ref[1:2, c0:c0 + D] * _shift_up(dv, 1, nx)
                       + cw_ref[0:1, c0:c0 + D] * _shift_up(dv, 2, nx)).astype(BF16)
                dup_ref[:, c0:c0 + D] = dup
                dh = dh + _dot_nt(dup, w_ref[:, c0:c0 + D])
        dxn, dsh, dsc, dg = _modnorm_bwd(dh, x2_ref[...], g_ref[...], sc_ref[...])
        dx2 = dx3_ref[...] + dxn
        dx2_ref[...] = dx2
        do1_ref[...] = (dx2 * g1_ref[...]).astype(BF16)
        dsh_ref[...] += dsh
        dsc_ref[...] += dsc
        dg_ref[...] += dg
        dg1_ref[...] += _colsum(dx2 * o1_ref[...].astype(F32))

    tile = pl.BlockSpec((tm, D), lambda i: (i, 0))
    wide = pl.BlockSpec((tm, DFF), lambda i: (i, 0))
    nxt = pl.BlockSpec((HALO, DFF), lambda i: (jnp.minimum((i + 1) * (tm // HALO), t // HALO - 1), 0))
    vec = _const_spec((1, D))
    vshape = jax.ShapeDtypeStruct((1, D), F32)
    return pl.pallas_call(
        body, name="ffn_up_bwd", grid=(nt,),
        in_specs=[wide, nxt, wide, nxt, _const_spec((3, 2 * DFF)), _const_spec((D, 2 * DFF), True),
                  tile, tile, vec, vec, tile, vec],
        out_specs=[pl.BlockSpec((tm, 2 * DFF), lambda i: (i, 0)), tile, tile, vec, vec, vec, vec],
        out_shape=[jax.ShapeDtypeStruct((t, 2 * DFF), BF16), jax.ShapeDtypeStruct((t, D), F32),
                   jax.ShapeDtypeStruct((t, D), BF16), vshape, vshape, vshape, vshape],
        compiler_params=_cparams(1),
    )(dact, dact, dval, dval, cw, wup, x2, dx3, gffn, scale2, o1, gate1)


def _xt_y(a, b, name):
    t, k = a.shape
    n = b.shape[1]
    tm = _tile_big(t)
    bn = 768 if n % 768 == 0 else D

    def body(a_ref, b_ref, o_ref):
        @pl.when(pl.program_id(1) == 0)
        def _():
            o_ref[...] = jnp.zeros_like(o_ref)

        o_ref[...] += _dot_tn(a_ref[...], b_ref[...])

    return pl.pallas_call(
        body, name=name, grid=(n // bn, t // tm),
        in_specs=[pl.BlockSpec((tm, k), lambda j, i: (i, 0)), pl.BlockSpec((tm, bn), lambda j, i: (i, j))],
        out_specs=pl.BlockSpec((k, bn), lambda j, i: (0, j)),
        out_shape=jax.ShapeDtypeStruct((k, n), F32),
        compiler_params=_cparams(2),
    )(a, b)


def _out_bwd(do1, wout, merged, ya, yb, z):
    t = do1.shape[0]
    tm = _tile_big(t)

    def body(do1_ref, wo_ref, mg_ref, ya_ref, yb_ref, ga_ref, gb_ref, dya_ref, dyb_ref, dz_ref, dwo_ref):
        @pl.when(pl.program_id(0) == 0)
        def _():
            dwo_ref[...] = jnp.zeros_like(dwo_ref)

        do1v = do1_ref[...]
        dwo_ref[...] += _dot_tn(mg_ref[...], do1v)
        dm = _dot_nt(do1v, wo_ref[...])
        sa = _sigmoid(ga_ref[...].astype(F32))
        sb = _sigmoid(gb_ref[...].astype(F32))
        dya_ref[...] = (dm * sa).astype(BF16)
        dyb_ref[...] = (dm * sb).astype(BF16)
        dz_ref[:, 0:D] = (dm * ya_ref[...].astype(F32) * sa * (1.0 - sa)).astype(BF16)
        dz_ref[:, D:2 * D] = (dm * yb_ref[...].astype(F32) * sb * (1.0 - sb)).astype(BF16)

    tile = pl.BlockSpec((tm, D), lambda i: (i, 0))
    bshape = jax.ShapeDtypeStruct((t, D), BF16)
    return pl.pallas_call(
        body, name="out_bwd", grid=(t // tm,),
        in_specs=[tile, _const_spec((D, D)), tile, tile, tile,
                  pl.BlockSpec((tm, D), lambda i: (i, 4)), pl.BlockSpec((tm, D), lambda i: (i, 5))],
        out_specs=[tile, tile, pl.BlockSpec((tm, 2 * D), lambda i: (i, 2)), _const_spec((D, D))],
        out_shape=[bshape, bshape, jax.ShapeDtypeStruct((t, NCOL_IN), BF16), jax.ShapeDtypeStruct((D, D), F32)],
        compiler_params=_cparams(1),
    )(do1, wout, merged, ya, yb, z, z)


def _branch_bwd(dy, y_pre, wb, name):
    t = dy.shape[0]
    tm = _tile_big(t)

    def body(dy_ref, yp_ref, wb_ref, dyp_ref, dwb_ref):
        @pl.when(pl.program_id(0) == 0)
        def _():
            dwb_ref[...] = jnp.zeros_like(dwb_ref)

        dyv = dy_ref[...]
        dwb_ref[...] += _dot_tn(yp_ref[...], dyv)
        dyp_ref[...] = _dot_nt(dyv, wb_ref[...]).astype(BF16)

    tile = pl.BlockSpec((tm, D), lambda i: (i, 0))
    return pl.pallas_call(
        body, name=name, grid=(t // tm,),
        in_specs=[tile, tile, _const_spec((D, D))],
        out_specs=[tile, _const_spec((D, D))],
        out_shape=[jax.ShapeDtypeStruct((t, D), BF16), jax.ShapeDtypeStruct((D, D), F32)],
        compiler_params=_cparams(1),
    )(dy, y_pre, wb)


def _rnn_bwd(dyap, z, h, dz, cw, cb, wa, ba, wx, bx, lam):
    t = z.shape[0]
    tm = _tile_seq(t)
    nt = t // tm
    ngrp = tm // SUBLANES
    hpt = tm // HALO

    def body(dyap_ref, xr_ref, xp_ref, gr_ref, h_ref, hp_ref, dz_any, cw_ref, cb_ref, wa_ref, ba_ref, wx_ref, bx_ref,
             lam_ref, dz_ref, dcw_ref, dcb_ref, dwa_ref, dba_ref, dwx_ref, dbx_ref, dlam_ref,
             a_first, g_first, dxc_first, b_scr, d_scr, g_scr):
        del dz_any
        i = pl.program_id(0)

        @pl.when(i == 0)
        def _():
            for r in (dcw_ref, dcb_ref, dwa_ref, dba_ref, dwx_ref, dbx_ref, dlam_ref, a_first, g_first, dxc_first):
                r[...] = jnp.zeros_like(r)

        first_tile = i == nt - 1
        xr = xr_ref[...].astype(F32)
        prev8 = jnp.where(first_tile, 0.0, xp_ref[...].astype(F32)[HALO - SUBLANES:])
        xc = _conv4(xr, prev8, cw_ref, cb_ref[...])
        lam_v = lam_ref[...]
        ls = _log_sigmoid(lam_v)
        ra, ia, a, mult = _lru_gates(xc, wa_ref, ba_ref[...], wx_ref, bx_ref[...], ls)
        hv = h_ref[...]
        hprev8 = jnp.where(first_tile, 0.0, hp_ref[...][HALO - SUBLANES:])
        h_prev = _shift_down(hv, 1, hprev8)
        grv = gr_ref[...].astype(F32)
        gg, tg = _gelu_t(grv)
        dyap_v = dyap_ref[...].astype(F32)
        dz_ref[:, D:2 * D] = (dyap_v * hv * _gelu_grad(grv, tg)).astype(BF16)

        b_scr[...] = _shift_up(a, 1, a_first[...])
        d_scr[...] = dyap_v * gg
        row = _row_iota(D)

        def grp(jj, carry):
            r0 = pl.multiple_of((ngrp - 1 - jj) * SUBLANES, SUBLANES)
            bv = b_scr[pl.ds(r0, SUBLANES), :]
            dv = d_scr[pl.ds(r0, SUBLANES), :]
            for d in (1, 2, 4):
                m = row < SUBLANES - d
                dv = jnp.where(m, dv + bv * pltpu.roll(dv, SUBLANES - d, 0), dv)
                bv = jnp.where(m, bv * pltpu.roll(bv, SUBLANES - d, 0), bv)
            gv = dv + bv * carry
            g_scr[pl.ds(r0, SUBLANES), :] = gv
            return gv[0:1, :]

        lax.fori_loop(0, ngrp, grp, g_first[0:1, :])
        g = g_scr[...]
        a_first[...] = a[:SUBLANES]
        g_first[...] = g[:SUBLANES]

        da = g * h_prev
        gx = g * xc
        dmult = gx * ia
        dia = gx * mult
        dxc = g * (mult * ia)
        dla = da * a - dmult * (a * a) / mult
        dra = dla * (LRU_C * ls)
        dlam_ref[...] += _colsum(dla * ra) * (LRU_C * _sigmoid(-lam_v))
        dpa = dra * ra * (1.0 - ra)
        dpx = dia * ia * (1.0 - ia)
        dba_ref[...] += _colsum(dpa)
        dbx_ref[...] += _colsum(dpx)
        dpab = dpa.astype(BF16)
        dpxb = dpx.astype(BF16)
        xcb = xc.astype(BF16)
        for hd in range(NH):
            sl = slice(hd * HD, (hd + 1) * HD)
            dwa_ref[hd] += _dot_tn(xcb[:, sl], dpab[:, sl])
            dwx_ref[hd] += _dot_tn(xcb[:, sl], dpxb[:, sl])
        dxc = dxc + _heads_nt(dpab, wa_ref) + _heads_nt(dpxb, wx_ref)

        nxt = dxc_first[...]
        dxr = (cw_ref[3:4, :] * dxc + cw_ref[2:3, :] * _shift_up(dxc, 1, nxt)
               + cw_ref[1:2, :] * _shift_up(dxc, 2, nxt) + cw_ref[0:1, :] * _shift_up(dxc, 3, nxt))
        dz_ref[:, 0:D] = dxr.astype(BF16)
        dxc_first[...] = dxc[:SUBLANES]
        dcb_ref[...] += _colsum(dxc)
        for k in range(4):
            dcw_ref[k:k + 1, :] += _colsum(dxc * _shift_down(xr, 3 - k, prev8))

    def rev(col):
        return lambda i: (nt - 1 - i, col)

    def rev_halo(col):
        return lambda i: (jnp.maximum((nt - 1 - i) * hpt - 1, 0), col)

    vec = _const_spec((1, D))
    wspec = _const_spec((NH, HD, HD))
    vshape = jax.ShapeDtypeStruct((1, D), F32)
    wshape = jax.ShapeDtypeStruct((NH, HD, HD), F32)
    outs = pl.pallas_call(
        body, name="rnn_bwd", grid=(nt,),
        in_specs=[pl.BlockSpec((tm, D), rev(0)), pl.BlockSpec((tm, D), rev(0)), pl.BlockSpec((HALO, D), rev_halo(0)),
                  pl.BlockSpec((tm, D), rev(1)), pl.BlockSpec((tm, D), rev(0)), pl.BlockSpec((HALO, D), rev_halo(0)),
                  pl.BlockSpec(memory_space=pl.ANY),
                  _const_spec((4, D)), vec, wspec, vec, wspec, vec, vec],
        out_specs=[pl.BlockSpec((tm, 2 * D), rev(0)), _const_spec((4, D)), vec, wspec, vec, wspec, vec, vec],
        out_shape=[jax.ShapeDtypeStruct((t, NCOL_IN), BF16), jax.ShapeDtypeStruct((4, D), F32), vshape,
                   wshape, vshape, wshape, vshape, vshape],
        scratch_shapes=[pltpu.VMEM((SUBLANES, D), F32), pltpu.VMEM((SUBLANES, D), F32), pltpu.VMEM((SUBLANES, D), F32),
                        pltpu.VMEM((tm, D), F32), pltpu.VMEM((tm, D), F32), pltpu.VMEM((tm, D), F32)],
        input_output_aliases={6: 0},
        compiler_params=_cparams(1),
    )(dyap, z, z, z, h, h, dz, cw, cb, wa, ba, wx, bx, lam)
    return outs


def _sgu_bwd(dybp, z, dz, lng, lnb, wm, wmt, bst, mask):
    t = z.shape[0]
    tm = _tile_seq(t)

    def body(dybp_ref, zu_ref, zv_ref, dz_any, lng_ref, lnb_ref, wm_ref, wmt_ref, bst_ref, mask_ref,
             dz_ref, dws_ref, dbst_ref, dlng_ref, dlnb_ref):
        del dz_any

        @pl.when(pl.program_id(0) == 0)
        def _():
            for r in (dws_ref, dbst_ref, dlng_ref, dlnb_ref):
                r[...] = jnp.zeros_like(r)

        zu = zu_ref[...].astype(F32)
        zv = zv_ref[...].astype(F32)
        lng_v = lng_ref[...]
        gu, tu, tv, rstd, vhat, vb, mixed = _sgu_core(zu, zv, lng_v, lnb_ref[...], wm_ref, bst_ref)
        dyb = dybp_ref[...].astype(F32)
        dz_ref[:, 0:D] = (dyb * mixed * _gelu_grad(zu, tu)).astype(BF16)
        dmix = dyb * gu
        dmb = dmix.astype(BF16)
        rows = []
        lane = lax.broadcasted_iota(jnp.int32, (HD, NH), 1)
        dbst = jnp.zeros((HD, NH), F32)
        for b0 in range(0, tm, HD):
            cols = []
            for g in range(NH):
                sl = slice(g * HD, (g + 1) * HD)
                dmg = dmb[b0:b0 + HD, sl]
                dws_ref[g] += _dot_nt(dmg, vb[b0:b0 + HD, sl]) * mask_ref[...]
                cols.append(jnp.dot(wmt_ref[g], dmg, preferred_element_type=F32))
                dbst = dbst + jnp.where(lane == g, jnp.sum(dmix[b0:b0 + HD, sl], axis=1, keepdims=True), 0.0)
            rows.append(jnp.concatenate(cols, axis=1))
        dbst_ref[...] += dbst
        dvln = jnp.concatenate(rows, axis=0) if len(rows) > 1 else rows[0]
        dlng_ref[...] += _colsum(dvln * vhat)
        dlnb_ref[...] += _colsum(dvln)
        dvh = dvln * lng_v
        dgv = rstd * (dvh - jnp.mean(dvh, axis=-1, keepdims=True)
                      - vhat * jnp.mean(dvh * vhat, axis=-1, keepdims=True))
        dz_ref[:, D:2 * D] = (dgv * _gelu_grad(zv, tv)).astype(BF16)

    vec = _const_spec((1, D))
    wspec = _const_spec((NH, HD, HD))
    vshape = jax.ShapeDtypeStruct((1, D), F32)
    return pl.pallas_call(
        body, name="sgu_bwd", grid=(t // tm,),
        in_specs=[pl.BlockSpec((tm, D), lambda i: (i, 0)), pl.BlockSpec((tm, D), lambda i: (i, 2)),
                  pl.BlockSpec((tm, D), lambda i: (i, 3)), pl.BlockSpec(memory_space=pl.ANY),
                  vec, vec, wspec, wspec, _const_spec((HD, NH)), _const_spec((HD, HD))],
        out_specs=[pl.BlockSpec((tm, 2 * D), lambda i: (i, 1)), wspec, _const_spec((HD, NH)), vec, vec],
        out_shape=[jax.ShapeDtypeStruct((t, NCOL_IN), BF16), jax.ShapeDtypeStruct((NH, HD, HD), F32),
                   jax.ShapeDtypeStruct((HD, NH), F32), vshape, vshape],
        input_output_aliases={3: 0},
        compiler_params=_cparams(1),
    )(dybp, z, z, dz, lng, lnb, wm, wmt, bst, mask)


def _in_bwd(dz, win, x, dx2, g, scale1):
    t = x.shape[0]
    tm = _tile_big(t)

    def body(dz_ref, w_ref, x_ref, dx2_ref, g_ref, sc_ref, dx_ref, dsh_ref, dsc_ref, dg_ref):
        @pl.when(pl.program_id(0) == 0)
        def _():
            for r in (dsh_ref, dsc_ref, dg_ref):
                r[...] = jnp.zeros_like(r)

        dh = jnp.zeros((tm, D), F32)
        for c0 in range(0, NCOL_IN, D):
            dh = dh + _dot_nt(dz_ref[:, c0:c0 + D], w_ref[:, c0:c0 + D])
        dxn, dsh, dsc, dg = _modnorm_bwd(dh, x_ref[...], g_ref[...], sc_ref[...])
        dx_ref[...] = dx2_ref[...] + dxn
        dsh_ref[...] += dsh
        dsc_ref[...] += dsc
        dg_ref[...] += dg

    tile = pl.BlockSpec((tm, D), lambda i: (i, 0))
    vec = _const_spec((1, D))
    vshape = jax.ShapeDtypeStruct((1, D), F32)
    return pl.pallas_call(
        body, name="in_bwd", grid=(t // tm,),
        in_specs=[pl.BlockSpec((tm, NCOL_IN), lambda i: (i, 0)), _const_spec((D, NCOL_IN), True), tile, tile, vec, vec],
        out_specs=[tile, vec, vec, vec],
        out_shape=[jax.ShapeDtypeStruct((t, D), F32), vshape, vshape, vshape],
        compiler_params=_cparams(1),
    )(dz, win, x, dx2, g, scale1)


def _mod_cols(c_all, w_ada, b_cols):
    nb, cols = c_all.shape[0], w_ada.shape[1]

    def body(c_ref, w_ref, b_ref, o_ref):
        cv = c_ref[...]
        ca = (cv * _sigmoid(cv)).astype(BF16)
        o_ref[...] = jnp.dot(ca, w_ref[...].astype(BF16), preferred_element_type=F32) + b_ref[...]

    return pl.pallas_call(body, name="mod_cols", out_shape=jax.ShapeDtypeStruct((nb, cols), F32))(c_all, w_ada, b_cols)


def _ada_grad(c_all, dmod_cols):
    cols = dmod_cols.shape[1]

    def body(c_ref, d_ref, o_ref):
        cv = c_ref[...]
        ca = (cv * _sigmoid(cv)).astype(BF16)
        o_ref[...] = _dot_tn(ca, d_ref[...].astype(BF16))

    return pl.pallas_call(body, name="ada_grad", out_shape=jax.ShapeDtypeStruct((D, cols), F32))(c_all, dmod_cols)


def _adamw(w, m, v, parts, name):
    rows, cols = w.shape
    tr = _row_tile(rows)
    stacked = [p.ndim == 3 for p in parts]
    bc1 = 1.0 - ADAM_B1 ** ADAM_STEP
    bc2 = 1.0 - ADAM_B2 ** ADAM_STEP

    def body(*refs):
        w_ref, m_ref, v_ref = refs[:3]
        p_refs = refs[3:3 + len(parts)]
        g_ref, d_ref, mo_ref, vo_ref = refs[3 + len(parts):]
        g = None
        for p_ref, st in zip(p_refs, stacked):
            terms = [p_ref[k].astype(F32) for k in range(p_ref.shape[0])] if st else [p_ref[...].astype(F32)]
            for term in terms:
                g = term if g is None else g + term
        mn = ADAM_B1 * m_ref[...] + (1.0 - ADAM_B1) * g
        vn = ADAM_B2 * v_ref[...] + (1.0 - ADAM_B2) * (g * g)
        g_ref[...] = g
        mo_ref[...] = mn
        vo_ref[...] = vn
        d_ref[...] = -ADAM_LR * ((mn / bc1) / (jnp.sqrt(vn / bc2) + ADAM_EPS) + ADAM_WD * w_ref[...])

    tile = pl.BlockSpec((tr, cols), lambda i: (i, 0))
    p_specs = [pl.BlockSpec((p.shape[0], tr, cols), lambda i: (0, i, 0)) if st else tile for p, st in zip(parts, stacked)]
    shp = jax.ShapeDtypeStruct((rows, cols), F32)
    return pl.pallas_call(
        body, name=name, grid=(rows // tr,),
        in_specs=[tile, tile, tile] + p_specs, out_specs=[tile] * 4, out_shape=[shp] * 4,
        compiler_params=_cparams(1),
    )(w, m, v, *parts)


def _pair_sum(keep, recv, name):
    _, rows, cols = keep.shape
    tr = _row_tile(rows)

    def body(k_ref, r_ref, own_ref, pay_ref):
        own_ref[...] = k_ref[0] + r_ref[0]
        for j in range(3):
            pay_ref[j] = (k_ref[j + 1] + r_ref[j + 1]).astype(BF16)

    blk = pl.BlockSpec((4, tr, cols), lambda i: (0, i, 0))
    return pl.pallas_call(
        body, name=name, grid=(rows // tr,),
        in_specs=[blk, blk],
        out_specs=[pl.BlockSpec((tr, cols), lambda i: (i, 0)), pl.BlockSpec((3, tr, cols), lambda i: (0, i, 0))],
        out_shape=[jax.ShapeDtypeStruct((rows, cols), F32), jax.ShapeDtypeStruct((3, rows, cols), BF16)],
        compiler_params=_cparams(1),
    )(keep, recv)


def _mesh_pos():
    return lax.axis_index("x"), lax.axis_index("y"), lax.axis_index("c")


def _other_chips(x, y):
    return [(1 - x, y), (x, 1 - y), (1 - x, 1 - y)]


def _block_of(ref, axis, index, size):
    if axis == 0:
        return ref.at[index]
    return ref.at[:, pl.ds(pl.multiple_of(index * size, 128), size)]


def _all_gather(shards, axes, name):
    n = len(shards)
    per = 7

    def body(*refs):
        ins, outs = refs[:n], refs[n:2 * n]
        send_sems, recv_sems, local_sems = refs[2 * n:]
        x, y, c = _mesh_pos()
        me, sibling = (x, y, c), (x, y, 1 - c)
        chips = _other_chips(x, y)

        def rows(a, pos):
            return _block_of(outs[a], axes[a], 4 * pos[0] + 2 * pos[1] + pos[2], shards[a].shape[-1])

        def copy(a, k, block, to, src=None):
            return pltpu.make_async_remote_copy(
                src_ref=rows(a, block) if src is None else src, dst_ref=rows(a, block),
                send_sem=send_sems.at[a * per + k], recv_sem=recv_sems.at[a * per + k],
                device_id=to, device_id_type=MESH_IDS)

        mine = [pltpu.make_async_copy(ins[a], rows(a, me), local_sems.at[a]) for a in range(n)]
        for cp in mine:
            cp.start()
        first = []
        for a in range(n):
            first.append(copy(a, 0, me, sibling, src=ins[a]))
            first += [copy(a, 1 + j, me, (*chip, c), src=ins[a]) for j, chip in enumerate(chips)]
        for cp in first:
            cp.start()
        passed = []
        for j, chip in enumerate(chips):
            for a in range(n):
                copy(a, 1 + j, (*chip, c), me).wait_recv()
                fwd = copy(a, 4 + j, (*chip, c), sibling)
                fwd.start()
                passed.append(fwd)
        for a in range(n):
            copy(a, 0, sibling, me).wait_recv()
            for j, chip in enumerate(chips):
                copy(a, 4 + j, (*chip, 1 - c), me).wait_recv()
        for cp in first + passed:
            cp.wait_send()
        for cp in mine:
            cp.wait()

    def full_shape(s, ax):
        return (N_DEV,) + s.shape if ax == 0 else s.shape[:-1] + (N_DEV * s.shape[-1],)

    any_spec = pl.BlockSpec(memory_space=pl.ANY)
    return pl.pallas_call(
        body, name=name,
        in_specs=[any_spec] * n, out_specs=[any_spec] * n,
        out_shape=[jax.ShapeDtypeStruct(full_shape(s, ax), s.dtype) for s, ax in zip(shards, axes)],
        scratch_shapes=[pltpu.SemaphoreType.DMA((n * per,)), pltpu.SemaphoreType.DMA((n * per,)),
                        pltpu.SemaphoreType.DMA((n,))],
    )(*shards)


def _chip_blocks(x, y):
    return [(x, y)] + _other_chips(x, y)


def _sibling_exchange(grads, axes, name):
    n = len(grads)

    def blk_shape(g, ax):
        return g.shape[1:] if ax == 0 else (g.shape[0], g.shape[1] // N_DEV)

    def body(*refs):
        ins = refs[:n]
        keeps, recvs = refs[n:2 * n], refs[2 * n:3 * n]
        send_sems, recv_sems, local_sems = refs[3 * n:]
        x, y, c = _mesh_pos()
        sibling = (x, y, 1 - c)
        chips = _chip_blocks(x, y)
        copies, local = [], []
        for a in range(n):
            size = blk_shape(grads[a], axes[a])[-1]
            for j, (px, py) in enumerate(chips):
                theirs = _block_of(ins[a], axes[a], 4 * px + 2 * py + (1 - c), size)
                ours = _block_of(ins[a], axes[a], 4 * px + 2 * py + c, size)
                copies.append(pltpu.make_async_remote_copy(
                    src_ref=theirs, dst_ref=recvs[a].at[j], send_sem=send_sems.at[4 * a + j],
                    recv_sem=recv_sems.at[4 * a + j], device_id=sibling, device_id_type=MESH_IDS))
                local.append(pltpu.make_async_copy(ours, keeps[a].at[j], local_sems.at[4 * a + j]))
        for cp in copies + local:
            cp.start()
        for cp in copies:
            cp.wait_recv()
        for cp in copies:
            cp.wait_send()
        for cp in local:
            cp.wait()

    any_spec = pl.BlockSpec(memory_space=pl.ANY)
    shapes = [jax.ShapeDtypeStruct((4,) + blk_shape(g, ax), g.dtype) for g, ax in zip(grads, axes)]
    outs = pl.pallas_call(
        body, name=name,
        in_specs=[any_spec] * n, out_specs=[any_spec] * (2 * n), out_shape=shapes + shapes,
        scratch_shapes=[pltpu.SemaphoreType.DMA((4 * n,)), pltpu.SemaphoreType.DMA((4 * n,)),
                        pltpu.SemaphoreType.DMA((4 * n,))],
    )(*grads)
    return outs[:n], outs[n:]


def _chip_exchange(payloads, name):
    n = len(payloads)

    def body(*refs):
        ins, outs = refs[:n], refs[n:2 * n]
        send_sems, recv_sems = refs[2 * n:]
        x, y, c = _mesh_pos()
        copies = []
        for a in range(n):
            for j, chip in enumerate(_other_chips(x, y)):
                copies.append(pltpu.make_async_remote_copy(
                    src_ref=ins[a].at[j], dst_ref=outs[a].at[j], send_sem=send_sems.at[3 * a + j],
                    recv_sem=recv_sems.at[3 * a + j], device_id=(*chip, c), device_id_type=MESH_IDS))
        for cp in copies:
            cp.start()
        for cp in copies:
            cp.wait_recv()
        for cp in copies:
            cp.wait_send()

    any_spec = pl.BlockSpec(memory_space=pl.ANY)
    return pl.pallas_call(
        body, name=name,
        in_specs=[any_spec] * n, out_specs=[any_spec] * n,
        out_shape=[jax.ShapeDtypeStruct(p.shape, p.dtype) for p in payloads],
        scratch_shapes=[pltpu.SemaphoreType.DMA((3 * n,)), pltpu.SemaphoreType.DMA((3 * n,))],
    )(*payloads)


def _local_step(x, target, mod, win, wup, wba, wbb, wout, wd, p):
    shift1, scale1, gate1, shift2, scale2, gate2 = (mod[k] for k in range(6))
    wa, wx = p["lru_w_a"].astype(BF16), p["lru_w_x"].astype(BF16)
    mask = jnp.tril(jnp.ones((HD, HD), F32))
    wm = (p["sgu_w_s"] * mask).astype(BF16)
    wmt = jnp.swapaxes(wm, 1, 2)
    bst = jnp.transpose(p["sgu_b_s"])

    h1, z = _norm_proj(x, p["norm_mix_g"], scale1, shift1, win, "mix_proj")
    hstate, ya_pre = _rnn_fwd(z, p["rnn_conv_w"], p["rnn_conv_b"], wa, p["lru_b_a"], wx, p["lru_b_x"], p["lru_lambda"])
    yb_pre = _sgu_fwd(z, p["sgu_ln_g"], p["sgu_ln_b"], wm, bst)
    x2, ya, yb, merged, o1 = _merge_fwd(ya_pre, yb_pre, z, x, gate1, wba, wbb, wout)
    h2, up = _norm_proj(x2, p["norm_ffn_g"], scale2, shift2, wup, "ffn_proj")
    ff = _ffn_mid_fwd(up, p["ffn_conv_w"], p["ffn_conv_b"])
    dx3, loss, d_gfin, d_gate2 = _ffn_out_loss(ff, wd, x2, target, gate2, p["norm_final_g"])

    dact, dval, d_wd, dcw_a, dcw_v, dcb_a, dcb_v = _ffn_down_bwd(dx3, gate2, ff, up, p["ffn_conv_w"], p["ffn_conv_b"], wd)
    dup, dx2, do1, d_shift2, d_scale2, d_gffn, d_gate1 = _ffn_up_bwd(
        dact, dval, p["ffn_conv_w"], wup, x2, dx3, p["norm_ffn_g"], scale2, o1, gate1)
    d_wup = _xt_y(h2, dup, "w_up_grad")

    dya, dyb, dz, d_wout = _out_bwd(do1, wout, merged, ya, yb, z)
    dya_pre, d_wba = _branch_bwd(dya, ya_pre, wba, "branch_a_bwd")
    dyb_pre, d_wbb = _branch_bwd(dyb, yb_pre, wbb, "branch_b_bwd")
    dz, d_cw, d_cb, d_wa, d_ba, d_wx, d_bx, d_lam = _rnn_bwd(
        dya_pre, z, hstate, dz, p["rnn_conv_w"], p["rnn_conv_b"], wa, p["lru_b_a"], wx, p["lru_b_x"], p["lru_lambda"])
    dz, d_ws, d_bst, d_lng, d_lnb = _sgu_bwd(dyb_pre, z, dz, p["sgu_ln_g"], p["sgu_ln_b"], wm, wmt, bst, mask)
    grad_x, d_shift1, d_scale1, d_gmix = _in_bwd(dz, win, x, dx2, p["norm_mix_g"], scale1)
    d_win = _xt_y(h1, dz, "w_in_grad")

    small = {
        "norm_mix_g": d_gmix, "rnn_conv_w": d_cw, "rnn_conv_b": d_cb, "lru_w_a": d_wa, "lru_b_a": d_ba,
        "lru_w_x": d_wx, "lru_b_x": d_bx, "lru_lambda": d_lam, "sgu_ln_g": d_lng, "sgu_ln_b": d_lnb,
        "sgu_w_s": d_ws, "sgu_b_s": jnp.transpose(d_bst), "norm_ffn_g": d_gffn,
        "ffn_conv_w": jnp.concatenate([dcw_a, dcw_v], axis=1), "ffn_conv_b": jnp.concatenate([dcb_a, dcb_v], axis=1),
        "norm_final_g": d_gfin,
    }
    dmod = jnp.stack([d_shift1, d_scale1, d_gate1, d_shift2, d_scale2, d_gate2])
    big = {"w_in": d_win, "w_up": d_wup, "w_branch_a": d_wba, "w_branch_b": d_wbb, "w_out": d_wout, "w_down": d_wd}
    return loss, grad_x, big, small, dmod


REPLICATED = ["b_ada", "norm_mix_g", "rnn_conv_b", "lru_w_a", "lru_b_a", "lru_w_x", "lru_b_x", "lru_lambda",
              "sgu_ln_g", "sgu_ln_b", "sgu_w_s", "sgu_b_s", "norm_ffn_g", "ffn_conv_b", "norm_final_g"]
COL_SHARDED = ["rnn_conv_w", "ffn_conv_w"]
SMALL_NAMES = REPLICATED + COL_SHARDED
BIG_NAMES = ["w_in", "w_up", "w_branch_a", "w_branch_b", "w_out", "w_down"]
BIG_AXES = [1, 1, 0, 0, 0, 0]
WEIGHTS = ["w_ada", "b_ada", "norm_mix_g", "w_in", "rnn_conv_w", "rnn_conv_b", "lru_w_a", "lru_b_a", "lru_w_x",
           "lru_b_x", "lru_lambda", "sgu_ln_g", "sgu_ln_b", "sgu_w_s", "sgu_b_s", "w_branch_a", "w_branch_b",
           "w_out", "norm_ffn_g", "w_up", "ffn_conv_w", "ffn_conv_b", "w_down", "norm_final_g"]
LANES = 128


def _pack_rows(shape):
    return math.prod(shape) // LANES


def _pack(arrays):
    return jnp.concatenate([a.reshape(-1, LANES) for a in arrays], axis=0)


def _unpack(packed, shapes):
    out, r0 = [], 0
    for s in shapes:
        nrow = math.prod(s) // LANES
        out.append(packed[r0:r0 + nrow].reshape(s))
        r0 += nrow
    return out


def kernel(x, c, w_ada, b_ada, norm_mix_g, w_in, rnn_conv_w, rnn_conv_b, lru_w_a, lru_b_a, lru_w_x, lru_b_x, lru_lambda, sgu_ln_g, sgu_ln_b, sgu_w_s, sgu_b_s, w_branch_a, w_branch_b, w_out, norm_ffn_g, w_up, ffn_conv_w, ffn_conv_b, w_down, norm_final_g, loss_target, m_w_ada, m_b_ada, m_norm_mix_g, m_w_in, m_rnn_conv_w, m_rnn_conv_b, m_lru_w_a, m_lru_b_a, m_lru_w_x, m_lru_b_x, m_lru_lambda, m_sgu_ln_g, m_sgu_ln_b, m_sgu_w_s, m_sgu_b_s, m_w_branch_a, m_w_branch_b, m_w_out, m_norm_ffn_g, m_w_up, m_ffn_conv_w, m_ffn_conv_b, m_w_down, m_norm_final_g, v_w_ada, v_b_ada, v_norm_mix_g, v_w_in, v_rnn_conv_w, v_rnn_conv_b, v_lru_w_a, v_lru_b_a, v_lru_w_x, v_lru_b_x, v_lru_lambda, v_sgu_ln_g, v_sgu_ln_b, v_sgu_w_s, v_sgu_b_s, v_w_branch_a, v_w_branch_b, v_w_out, v_norm_ffn_g, v_w_up, v_ffn_conv_w, v_ffn_conv_b, v_w_down, v_norm_final_g):
    given = dict(locals())
    me = 4 * lax.axis_index("x") + 2 * lax.axis_index("y") + lax.axis_index("c")
    ada_cols = w_ada.shape[2]
    conv_cols = {"rnn_conv_w": rnn_conv_w.shape[2], "ffn_conv_w": ffn_conv_w.shape[2]}

    shards = [w_in[0].astype(BF16), w_up[0].astype(BF16), w_branch_a[0].astype(BF16), w_branch_b[0].astype(BF16),
              w_out[0].astype(BF16), w_down[0].astype(BF16), c.reshape(1, 1, D), rnn_conv_w[0], ffn_conv_w[0]]
    win, wup, wba, wbb, wout, wd, c_all, cw_rnn, cw_ffn = _all_gather(
        shards, BIG_AXES + [0, 1, 1], "gather_weights")
    wba, wbb, wout = (w.reshape(D, D) for w in (wba, wbb, wout))
    wd = wd.reshape(DFF, D)
    c_all = c_all.reshape(N_DEV, D)

    b_cols = lax.dynamic_slice_in_dim(b_ada, me * ada_cols, ada_cols, axis=1)
    (mod_all,) = _all_gather([_mod_cols(c_all, w_ada[0], b_cols).reshape(1, N_DEV, ada_cols)], [0], "gather_mod")
    mod_all = mod_all.reshape(N_DEV, N_DEV, ada_cols)
    mod_mine = lax.dynamic_index_in_dim(mod_all, me, axis=1, keepdims=False).reshape(6, 1, D)

    p = {n: given[n][0] for n in REPLICATED if n not in ("b_ada", "norm_final_g")}
    p = {n: (a.reshape(1, -1) if a.ndim == 1 else a) for n, a in p.items()}
    p["rnn_conv_w"], p["ffn_conv_w"] = cw_rnn, cw_ffn
    p["norm_final_g"] = norm_final_g.reshape(1, D)
    loss, grad_x, big, small, dmod = _local_step(x[0], loss_target[0], mod_mine, win, wup, wba, wbb, wout, wd, p)

    small["b_ada"] = dmod.reshape(1, 6 * D)
    rows_of = {n: _pack_rows(small[n].shape) for n in SMALL_NAMES}
    start_of = {n: sum(rows_of[q] for q in SMALL_NAMES[:k]) for k, n in enumerate(SMALL_NAMES)}
    pack = _pack([small[n] for n in SMALL_NAMES])
    (packs,) = _all_gather([pack[None]], [0], "gather_small")
    packs = packs.reshape(N_DEV, pack.shape[0], LANES)

    grads = [big[n] if ax == 1 else big[n].reshape(N_DEV, big[n].shape[0] // N_DEV, big[n].shape[1])
             for n, ax in zip(BIG_NAMES, BIG_AXES)]
    keeps, recvs = _sibling_exchange(grads, BIG_AXES, "reduce_sibling")
    sums = [_pair_sum(k, r, "pair_sum_" + n) for n, k, r in zip(BIG_NAMES, keeps, recvs)]
    landed = _chip_exchange([pay for _, pay in sums], "reduce_chips")
    out = {}
    for n, (own, _), got in zip(BIG_NAMES, sums, landed):
        out[n] = _adamw(given[n][0], given["m_" + n][0], given["v_" + n][0], [own, got], "adamw_" + n)

    dmod_all = packs[:, :rows_of["b_ada"]].reshape(N_DEV, 6 * D)
    dmod_cols = lax.dynamic_slice_in_dim(dmod_all, me * ada_cols, ada_cols, axis=1)
    out["w_ada"] = _adamw(w_ada[0], m_w_ada[0], v_w_ada[0], [_ada_grad(c_all, dmod_cols)], "adamw_w_ada")

    rep_rows = sum(rows_of[n] for n in REPLICATED)
    res = _adamw(*[_pack([given[pre + n] for n in REPLICATED]) for pre in ("", "m_", "v_")],
                 [packs[:, :rep_rows]], "adamw_small")
    unpacked = [_unpack(r, [given[n].shape for n in REPLICATED]) for r in res]
    for k, n in enumerate(REPLICATED):
        out[n] = tuple(u[k] for u in unpacked)

    for n in COL_SHARDED:
        full = packs[:, start_of[n]:start_of[n] + rows_of[n]].reshape(N_DEV, small[n].shape[0], small[n].shape[1])
        mine = lax.dynamic_slice_in_dim(full, me * conv_cols[n], conv_cols[n], axis=2)
        out[n] = _adamw(given[n][0], given["m_" + n][0], given["v_" + n][0], [mine], "adamw_" + n)

    total = lax.psum(loss[0, 0], ("x", "y", "c"))
    results = [total, grad_x[None]]
    for kind in range(4):
        results += [out[n][kind].reshape(given[n].shape) for n in WEIGHTS]
    return tuple(results)
```

```python
import math

import jax
import jax.numpy as jnp
from jax import lax
from jax.experimental import pallas as pl
from jax.experimental.pallas import tpu as pltpu

F32 = jnp.float32
BF16 = jnp.bfloat16
MESH_IDS = pl.DeviceIdType.MESH

D = 1024
NH = 8
HD = 128
NCOL_IN = 6 * D
DFF = 3 * D
N_DEV = 8
EPS = 1e-6
LRU_C = 8.0
ADAM_LR, ADAM_B1, ADAM_B2, ADAM_EPS, ADAM_WD, ADAM_STEP = 0.001, 0.9, 0.999, 1e-08, 0.01, 10

SUBLANES = 8
HALO = 16
VMEM_LIMIT = 56 * 1024 * 1024
GELU_K = math.sqrt(2.0 / math.pi)
GELU_C = 0.044715


def _cparams(n_axes):
    return pltpu.CompilerParams(dimension_semantics=("arbitrary",) * n_axes, vmem_limit_bytes=VMEM_LIMIT)


def _const_spec(shape, single_buffer=False):
    nd = len(shape)
    if single_buffer:
        return pl.BlockSpec(shape, lambda *_: (0,) * nd, pipeline_mode=pl.Buffered(1))
    return pl.BlockSpec(shape, lambda *_: (0,) * nd)


def _tile_big(t):
    return min(512, t)


def _tile_seq(t):
    return min(256, t)


def _row_tile(rows):
    if rows <= 512:
        return rows
    return next(tr for tr in range(512, 0, -SUBLANES) if rows % tr == 0)


def _gelu_t(x):
    t = jnp.tanh(GELU_K * (x + GELU_C * (x * x * x)))
    return 0.5 * x * (1.0 + t), t


def _gelu_grad(x, t):
    return 0.5 * (1.0 + t) + 0.5 * x * (1.0 - t * t) * (GELU_K * (1.0 + 3.0 * GELU_C * x * x))


def _sigmoid(x):
    return 1.0 / (1.0 + jnp.exp(-x))


def _log_sigmoid(x):
    return -(jnp.maximum(-x, 0.0) + jnp.log1p(jnp.exp(-jnp.abs(x))))


def _row_iota(cols):
    return lax.broadcasted_iota(jnp.int32, (SUBLANES, cols), 0)


def _shift_down(x, k, prev8):
    if k == 0:
        return x
    r = pltpu.roll(x, k, 0)
    p = pltpu.roll(prev8, k, 0)
    head = jnp.where(_row_iota(x.shape[1]) < k, p, r[:SUBLANES])
    return jnp.concatenate([head, r[SUBLANES:]], axis=0)


def _shift_up(x, k, next8):
    if k == 0:
        return x
    n = x.shape[0]
    r = pltpu.roll(x, n - k, 0)
    q = pltpu.roll(next8, SUBLANES - k, 0)
    tail = jnp.where(_row_iota(x.shape[1]) >= SUBLANES - k, q, r[n - SUBLANES:])
    return jnp.concatenate([r[:n - SUBLANES], tail], axis=0)


def _heads_nn(x_bf, w_ref):
    return jnp.concatenate(
        [jnp.dot(x_bf[:, h * HD:(h + 1) * HD], w_ref[h], preferred_element_type=F32) for h in range(NH)], axis=1)


def _heads_nt(x_bf, w_ref):
    return jnp.concatenate(
        [lax.dot_general(x_bf[:, h * HD:(h + 1) * HD], w_ref[h], (((1,), (1,)), ((), ())), preferred_element_type=F32)
         for h in range(NH)], axis=1)


def _dot_nt(a, b):
    return lax.dot_general(a, b, (((1,), (1,)), ((), ())), preferred_element_type=F32)


def _dot_tn(a, b):
    return lax.dot_general(a, b, (((0,), (0,)), ((), ())), preferred_element_type=F32)


def _colsum(x):
    return jnp.sum(x, axis=0, keepdims=True)


def _prev_halo_map(tm, col):
    return lambda i, *_: (jnp.maximum(i * (tm // HALO) - 1, 0), col)


def _norm_proj(x, g, scale, shift, w, name):
    t, n = x.shape[0], w.shape[1]
    tm = _tile_big(t)

    def body(x_ref, g_ref, sc_ref, sh_ref, w_ref, h_ref, z_ref):
        xv = x_ref[...]
        r = lax.rsqrt(jnp.mean(xv * xv, axis=-1, keepdims=True) + EPS)
        hb = ((xv * r * g_ref[...]) * (1.0 + sc_ref[...]) + sh_ref[...]).astype(BF16)
        h_ref[...] = hb
        for c0 in range(0, n, D):
            z_ref[:, c0:c0 + D] = jnp.dot(hb, w_ref[:, c0:c0 + D], preferred_element_type=F32).astype(BF16)

    vec = _const_spec((1, D))
    return pl.pallas_call(
        body, name=name, grid=(t // tm,),
        in_specs=[pl.BlockSpec((tm, D), lambda i: (i, 0)), vec, vec, vec, _const_spec((D, n), True)],
        out_specs=[pl.BlockSpec((tm, D), lambda i: (i, 0)), pl.BlockSpec((tm, n), lambda i: (i, 0))],
        out_shape=[jax.ShapeDtypeStruct((t, D), BF16), jax.ShapeDtypeStruct((t, n), BF16)],
        compiler_params=_cparams(1),
    )(x, g, scale, shift, w)


def _lru_gates(xc, wa_ref, ba, wx_ref, bx, ls):
    xb = xc.astype(BF16)
    ra = _sigmoid(_heads_nn(xb, wa_ref) + ba)
    ia = _sigmoid(_heads_nn(xb, wx_ref) + bx)
    la = LRU_C * ra * ls
    a = jnp.exp(la)
    mult = jnp.sqrt(-jnp.tanh(la) * (1.0 + a * a))
    return ra, ia, a, mult


def _conv4(xr, prev8, cw_ref, cb):
    return (cb + cw_ref[3:4, :] * xr + cw_ref[2:3, :] * _shift_down(xr, 1, prev8)
            + cw_ref[1:2, :] * _shift_down(xr, 2, prev8) + cw_ref[0:1, :] * _shift_down(xr, 3, prev8))


def _rnn_fwd(z, cw, cb, wa, ba, wx, bx, lam):
    t = z.shape[0]
    tm = _tile_seq(t)
    ngrp = tm // SUBLANES

    def body(xr_ref, xp_ref, gr_ref, cw_ref, cb_ref, wa_ref, ba_ref, wx_ref, bx_ref, lam_ref,
             h_ref, ya_ref, carry_ref, a_scr, u_scr):
        i = pl.program_id(0)

        @pl.when(i == 0)
        def _():
            carry_ref[...] = jnp.zeros_like(carry_ref)

        xr = xr_ref[...].astype(F32)
        prev8 = jnp.where(i == 0, 0.0, xp_ref[...].astype(F32)[HALO - SUBLANES:])
        xc = _conv4(xr, prev8, cw_ref, cb_ref[...])
        _, ia, a, mult = _lru_gates(xc, wa_ref, ba_ref[...], wx_ref, bx_ref[...], _log_sigmoid(lam_ref[...]))
        a_scr[...] = a
        u_scr[...] = mult * (ia * xc)
        row = _row_iota(D)

        def grp(j, carry):
            r0 = pl.multiple_of(j * SUBLANES, SUBLANES)
            av = a_scr[pl.ds(r0, SUBLANES), :]
            uv = u_scr[pl.ds(r0, SUBLANES), :]
            for d in (1, 2, 4):
                m = row >= d
                uv = jnp.where(m, av * pltpu.roll(uv, d, 0) + uv, uv)
                av = jnp.where(m, av * pltpu.roll(av, d, 0), av)
            hv = uv + av * carry
            h_ref[pl.ds(r0, SUBLANES), :] = hv
            return hv[SUBLANES - 1:SUBLANES, :]

        carry_ref[0:1, :] = lax.fori_loop(0, ngrp, grp, carry_ref[0:1, :])
        gg, _ = _gelu_t(gr_ref[...].astype(F32))
        ya_ref[...] = (h_ref[...] * gg).astype(BF16)

    vec = _const_spec((1, D))
    wspec = _const_spec((NH, HD, HD))
    return pl.pallas_call(
        body, name="rnn_fwd", grid=(t // tm,),
        in_specs=[pl.BlockSpec((tm, D), lambda i: (i, 0)), pl.BlockSpec((HALO, D), _prev_halo_map(tm, 0)),
                  pl.BlockSpec((tm, D), lambda i: (i, 1)), _const_spec((4, D)), vec, wspec, vec, wspec, vec, vec],
        out_specs=[pl.BlockSpec((tm, D), lambda i: (i, 0)), pl.BlockSpec((tm, D), lambda i: (i, 0))],
        out_shape=[jax.ShapeDtypeStruct((t, D), F32), jax.ShapeDtypeStruct((t, D), BF16)],
        scratch_shapes=[pltpu.VMEM((SUBLANES, D), F32), pltpu.VMEM((tm, D), F32), pltpu.VMEM((tm, D), F32)],
        compiler_params=_cparams(1),
    )(z, z, z, cw, cb, wa, ba, wx, bx, lam)


def _sgu_core(zu, zv, lng, lnb, wm_ref, bst_ref):
    gu, tu = _gelu_t(zu)
    gv, tv = _gelu_t(zv)
    mu = jnp.mean(gv, axis=-1, keepdims=True)
    cen = gv - mu
    rstd = lax.rsqrt(jnp.mean(cen * cen, axis=-1, keepdims=True) + EPS)
    vhat = cen * rstd
    vln = vhat * lng + lnb
    vb = vln.astype(BF16)
    rows = []
    for b0 in range(0, zu.shape[0], HD):
        rows.append(jnp.concatenate(
            [jnp.dot(wm_ref[g], vb[b0:b0 + HD, g * HD:(g + 1) * HD], preferred_element_type=F32)
             + bst_ref[:, g:g + 1] for g in range(NH)], axis=1))
    mixed = jnp.concatenate(rows, axis=0) if len(rows) > 1 else rows[0]
    return gu, tu, tv, rstd, vhat, vb, mixed


def _sgu_fwd(z, lng, lnb, wm, bst):
    t = z.shape[0]
    tm = _tile_seq(t)

    def body(zu_ref, zv_ref, lng_ref, lnb_ref, wm_ref, bst_ref, yb_ref):
        gu, _, _, _, _, _, mixed = _sgu_core(zu_ref[...].astype(F32), zv_ref[...].astype(F32),
                                             lng_ref[...], lnb_ref[...], wm_ref, bst_ref)
        yb_ref[...] = (gu * mixed).astype(BF16)

    vec = _const_spec((1, D))
    return pl.pallas_call(
        body, name="sgu_fwd", grid=(t // tm,),
        in_specs=[pl.BlockSpec((tm, D), lambda i: (i, 2)), pl.BlockSpec((tm, D), lambda i: (i, 3)), vec, vec,
                  _const_spec((NH, HD, HD)), _const_spec((HD, NH))],
        out_specs=pl.BlockSpec((tm, D), lambda i: (i, 0)),
        out_shape=jax.ShapeDtypeStruct((t, D), BF16),
        compiler_params=_cparams(1),
    )(z, z, lng, lnb, wm, bst)


def _merge_fwd(ya_pre, yb_pre, z, x, gate1, wba, wbb, wout):
    t = x.shape[0]
    tm = _tile_big(t)

    def body(yap_ref, ybp_ref, ga_ref, gb_ref, x_ref, g1_ref, wba_ref, wbb_ref, wo_ref,
             x2_ref, ya_ref, yb_ref, mg_ref, o1_ref):
        ya = jnp.dot(yap_ref[...], wba_ref[...], preferred_element_type=F32)
        yb = jnp.dot(ybp_ref[...], wbb_ref[...], preferred_element_type=F32)
        merged = _sigmoid(ga_ref[...].astype(F32)) * ya + _sigmoid(gb_ref[...].astype(F32)) * yb
        mb = merged.astype(BF16)
        o1 = jnp.dot(mb, wo_ref[...], preferred_element_type=F32)
        x2_ref[...] = x_ref[...] + g1_ref[...] * o1
        ya_ref[...] = ya.astype(BF16)
        yb_ref[...] = yb.astype(BF16)
        mg_ref[...] = mb
        o1_ref[...] = o1.astype(BF16)

    tile = pl.BlockSpec((tm, D), lambda i: (i, 0))
    wspec = _const_spec((D, D))
    bshape = jax.ShapeDtypeStruct((t, D), BF16)
    return pl.pallas_call(
        body, name="merge_fwd", grid=(t // tm,),
        in_specs=[tile, tile, pl.BlockSpec((tm, D), lambda i: (i, 4)), pl.BlockSpec((tm, D), lambda i: (i, 5)),
                  tile, _const_spec((1, D)), wspec, wspec, wspec],
        out_specs=[tile] * 5,
        out_shape=[jax.ShapeDtypeStruct((t, D), F32), bshape, bshape, bshape, bshape],
        compiler_params=_cparams(1),
    )(ya_pre, yb_pre, z, z, x, gate1, wba, wbb, wout)


def _conv3(u, prev8, cw_ref, cb):
    return cb + cw_ref[2:3, :] * u + cw_ref[1:2, :] * _shift_down(u, 1, prev8) + cw_ref[0:1, :] * _shift_down(u, 2, prev8)


def _ffn_mid_fwd(up, cw, cb):
    t = up.shape[0]
    tm = _tile_big(t)
    nc = DFF // D

    def body(ua_ref, uap_ref, uv_ref, uvp_ref, cwa_ref, cwv_ref, cba_ref, cbv_ref, ff_ref):
        first = pl.program_id(0) == 0
        pa = jnp.where(first, 0.0, uap_ref[...].astype(F32)[HALO - SUBLANES:])
        pv = jnp.where(first, 0.0, uvp_ref[...].astype(F32)[HALO - SUBLANES:])
        act = _conv3(ua_ref[...].astype(F32), pa, cwa_ref, cba_ref[...])
        val = _conv3(uv_ref[...].astype(F32), pv, cwv_ref, cbv_ref[...])
        ga, _ = _gelu_t(act)
        ff_ref[...] = (ga * val).astype(BF16)

    return pl.pallas_call(
        body, name="ffn_mid_fwd", grid=(t // tm, nc),
        in_specs=[pl.BlockSpec((tm, D), lambda i, c: (i, c)),
                  pl.BlockSpec((HALO, D), lambda i, c: (jnp.maximum(i * (tm // HALO) - 1, 0), c)),
                  pl.BlockSpec((tm, D), lambda i, c: (i, nc + c)),
                  pl.BlockSpec((HALO, D), lambda i, c: (jnp.maximum(i * (tm // HALO) - 1, 0), nc + c)),
                  pl.BlockSpec((3, D), lambda i, c: (0, c)), pl.BlockSpec((3, D), lambda i, c: (0, nc + c)),
                  pl.BlockSpec((1, D), lambda i, c: (0, c)), pl.BlockSpec((1, D), lambda i, c: (0, nc + c))],
        out_specs=pl.BlockSpec((tm, D), lambda i, c: (i, c)),
        out_shape=jax.ShapeDtypeStruct((t, DFF), BF16),
        compiler_params=_cparams(2),
    )(up, up, up, up, cw, cw, cb, cb)


def _ffn_out_loss(ff, wd, x2, target, gate2, gfin):
    t = x2.shape[0]
    tm = _tile_big(t)

    def body(ff_ref, wd_ref, x2_ref, tg_ref, g2_ref, gf_ref, dx3_ref, loss_ref, dgf_ref, dg2_ref):
        @pl.when(pl.program_id(0) == 0)
        def _():
            loss_ref[...] = jnp.zeros_like(loss_ref)
            dgf_ref[...] = jnp.zeros_like(dgf_ref)
            dg2_ref[...] = jnp.zeros_like(dg2_ref)

        o2 = jnp.dot(ff_ref[...], wd_ref[...], preferred_element_type=F32)
        x3 = x2_ref[...] + g2_ref[...] * o2
        r = lax.rsqrt(jnp.mean(x3 * x3, axis=-1, keepdims=True) + EPS)
        xhat = x3 * r
        err = xhat * gf_ref[...] - tg_ref[...]
        loss_ref[...] += 0.5 * jnp.sum(jnp.mean(err * err, axis=-1, keepdims=True), axis=0, keepdims=True)
        dy = err * (1.0 / D)
        dgf_ref[...] += _colsum(dy * xhat)
        dxh = dy * gf_ref[...]
        dx3 = r * (dxh - xhat * jnp.mean(dxh * xhat, axis=-1, keepdims=True))
        dx3_ref[...] = dx3
        dg2_ref[...] += _colsum(dx3 * o2)

    tile = pl.BlockSpec((tm, D), lambda i: (i, 0))
    vec = _const_spec((1, D))
    return pl.pallas_call(
        body, name="ffn_out_loss", grid=(t // tm,),
        in_specs=[pl.BlockSpec((tm, DFF), lambda i: (i, 0)), _const_spec((DFF, D), True), tile, tile, vec, vec],
        out_specs=[tile, _const_spec((1, 1)), vec, vec],
        out_shape=[jax.ShapeDtypeStruct((t, D), F32), jax.ShapeDtypeStruct((1, 1), F32),
                   jax.ShapeDtypeStruct((1, D), F32), jax.ShapeDtypeStruct((1, D), F32)],
        compiler_params=_cparams(1),
    )(ff, wd, x2, target, gate2, gfin)


def _ffn_down_bwd(dx3, gate2, ff, up, cw, cb, wd):
    t = dx3.shape[0]
    tm = _tile_big(t)
    nc = DFF // D

    def body(dx3_ref, g2_ref, ff_ref, ua_ref, uap_ref, uv_ref, uvp_ref, cwa_ref, cwv_ref, cba_ref, cbv_ref, wd_ref,
             da_ref, dv_ref, dwd_ref, dcwa_ref, dcwv_ref, dcba_ref, dcbv_ref):
        i = pl.program_id(1)

        @pl.when(i == 0)
        def _():
            for r in (dwd_ref, dcwa_ref, dcwv_ref, dcba_ref, dcbv_ref):
                r[...] = jnp.zeros_like(r)

        first = i == 0
        ua = ua_ref[...].astype(F32)
        uv = uv_ref[...].astype(F32)
        pa = jnp.where(first, 0.0, uap_ref[...].astype(F32)[HALO - SUBLANES:])
        pv = jnp.where(first, 0.0, uvp_ref[...].astype(F32)[HALO - SUBLANES:])
        act = _conv3(ua, pa, cwa_ref, cba_ref[...])
        val = _conv3(uv, pv, cwv_ref, cbv_ref[...])
        do2 = (dx3_ref[...] * g2_ref[...]).astype(BF16)
        dwd_ref[...] += _dot_tn(ff_ref[...], do2)
        dff = _dot_nt(do2, wd_ref[...])
        ga, ta = _gelu_t(act)
        dact = dff * val * _gelu_grad(act, ta)
        dval = dff * ga
        da_ref[...] = dact.astype(BF16)
        dv_ref[...] = dval.astype(BF16)
        dcba_ref[...] += _colsum(dact)
        dcbv_ref[...] += _colsum(dval)
        for k in range(3):
            dcwa_ref[k:k + 1, :] += _colsum(dact * _shift_down(ua, 2 - k, pa))
            dcwv_ref[k:k + 1, :] += _colsum(dval * _shift_down(uv, 2 - k, pv))

    def halo(col):
        return lambda c, i: (jnp.maximum(i * (tm // HALO) - 1, 0), col(c))

    return pl.pallas_call(
        body, name="ffn_down_bwd", grid=(nc, t // tm),
        in_specs=[pl.BlockSpec((tm, D), lambda c, i: (i, 0)), pl.BlockSpec((1, D), lambda c, i: (0, 0)),
                  pl.BlockSpec((tm, D), lambda c, i: (i, c)),
                  pl.BlockSpec((tm, D), lambda c, i: (i, c)), pl.BlockSpec((HALO, D), halo(lambda c: c)),
                  pl.BlockSpec((tm, D), lambda c, i: (i, nc + c)), pl.BlockSpec((HALO, D), halo(lambda c: nc + c)),
                  pl.BlockSpec((3, D), lambda c, i: (0, c)), pl.BlockSpec((3, D), lambda c, i: (0, nc + c)),
                  pl.BlockSpec((1, D), lambda c, i: (0, c)), pl.BlockSpec((1, D), lambda c, i: (0, nc + c)),
                  pl.BlockSpec((D, D), lambda c, i: (c, 0))],
        out_specs=[pl.BlockSpec((tm, D), lambda c, i: (i, c)), pl.BlockSpec((tm, D), lambda c, i: (i, c)),
                   pl.BlockSpec((D, D), lambda c, i: (c, 0)),
                   pl.BlockSpec((3, D), lambda c, i: (0, c)), pl.BlockSpec((3, D), lambda c, i: (0, c)),
                   pl.BlockSpec((1, D), lambda c, i: (0, c)), pl.BlockSpec((1, D), lambda c, i: (0, c))],
        out_shape=[jax.ShapeDtypeStruct((t, DFF), BF16), jax.ShapeDtypeStruct((t, DFF), BF16),
                   jax.ShapeDtypeStruct((DFF, D), F32),
                   jax.ShapeDtypeStruct((3, DFF), F32), jax.ShapeDtypeStruct((3, DFF), F32),
                   jax.ShapeDtypeStruct((1, DFF), F32), jax.ShapeDtypeStruct((1, DFF), F32)],
        compiler_params=_cparams(2),
    )(dx3, gate2, ff, up, up, up, up, cw, cw, cb, cb, wd)


def _modnorm_bwd(dh, xv, g, scale):
    r = lax.rsqrt(jnp.mean(xv * xv, axis=-1, keepdims=True) + EPS)
    xhat = xv * r
    dxn = dh * (1.0 + scale)
    dxh = dxn * g
    dx = r * (dxh - xhat * jnp.mean(dxh * xhat, axis=-1, keepdims=True))
    return dx, _colsum(dh), _colsum(dh * (xhat * g)), _colsum(dxn * xhat)


def _ffn_up_bwd(dact, dval, cw, wup, x2, dx3, gffn, scale2, o1, gate1):
    t = x2.shape[0]
    tm = _tile_seq(t)
    nt = t // tm
    nc = DFF // D

    def body(da_ref, dan_ref, dv_ref, dvn_ref, cw_ref, w_ref, x2_ref, dx3_ref, g_ref, sc_ref, o1_ref, g1_ref,
             dup_ref, dx2_ref, do1_ref, dsh_ref, dsc_ref, dg_ref, dg1_ref):
        i = pl.program_id(0)

        @pl.when(i == 0)
        def _():
            for r in (dsh_ref, dsc_ref, dg_ref, dg1_ref):
                r[...] = jnp.zeros_like(r)

        last = i == nt - 1
        dh = jnp.zeros((tm, D), F32)
        for half, (d_ref, dn_ref) in enumerate(((da_ref, dan_ref), (dv_ref, dvn_ref))):
            nxt = jnp.where(last, 0.0, dn_ref[...].astype(F32)[:SUBLANES])
            for c in range(nc):
                c0 = half * DFF + c * D
                dv = d_ref[:, c * D:(c + 1) * D].astype(F32)
                nx = nxt[:, c * D:(c + 1) * D]
                dup = (cw_ref[2:3, c0:c0 + D] * dv + cw_ref[1:2, c0:c0 + D] * _shift_up(dv, 1, nx)
                       + cw_ref[0:1, c0:c0 + D] * _shift_up(dv, 2, nx)).astype(BF16)
                dup_ref[:, c0:c0 + D] = dup
                dh = dh + _dot_nt(dup, w_ref[:, c0:c0 + D])
        dxn, dsh, dsc, dg = _modnorm_bwd(dh, x2_ref[...], g_ref[...], sc_ref[...])
        dx2 = dx3_ref[...] + dxn
        dx2_ref[...] = dx2
        do1_ref[...] = (dx2 * g1_ref[...]).astype(BF16)
        dsh_ref[...] += dsh
        dsc_ref[...] += dsc
        dg_ref[...] += dg
        dg1_ref[...] += _colsum(dx2 * o1_ref[...].astype(F32))

    tile = pl.BlockSpec((tm, D), lambda i: (i, 0))
    wide = pl.BlockSpec((tm, DFF), lambda i: (i, 0))
    nxt = pl.BlockSpec((HALO, DFF), lambda i: (jnp.minimum((i + 1) * (tm // HALO), t // HALO - 1), 0))
    vec = _const_spec((1, D))
    vshape = jax.ShapeDtypeStruct((1, D), F32)
    return pl.pallas_call(
        body, name="ffn_up_bwd", grid=(nt,),
        in_specs=[wide, nxt, wide, nxt, _const_spec((3, 2 * DFF)), _const_spec((D, 2 * DFF), True),
                  tile, tile, vec, vec, tile, vec],
        out_specs=[pl.BlockSpec((tm, 2 * DFF), lambda i: (i, 0)), tile, tile, vec, vec, vec, vec],
        out_shape=[jax.ShapeDtypeStruct((t, 2 * DFF), BF16), jax.ShapeDtypeStruct((t, D), F32),
                   jax.ShapeDtypeStruct((t, D), BF16), vshape, vshape, vshape, vshape],
        compiler_params=_cparams(1),
    )(dact, dact, dval, dval, cw, wup, x2, dx3, gffn, scale2, o1, gate1)


def _xt_y(a, b, name):
    t, k = a.shape
    n = b.shape[1]
    tm = _tile_big(t)
    bn = 768 if n % 768 == 0 else D

    def body(a_ref, b_ref, o_ref):
        @pl.when(pl.program_id(1) == 0)
        def _():
            o_ref[...] = jnp.zeros_like(o_ref)

        o_ref[...] += _dot_tn(a_ref[...], b_ref[...])

    return pl.pallas_call(
        body, name=name, grid=(n // bn, t // tm),
        in_specs=[pl.BlockSpec((tm, k), lambda j, i: (i, 0)), pl.BlockSpec((tm, bn), lambda j, i: (i, j))],
        out_specs=pl.BlockSpec((k, bn), lambda j, i: (0, j)),
        out_shape=jax.ShapeDtypeStruct((k, n), F32),
        compiler_params=_cparams(2),
    )(a, b)


def _out_bwd(do1, wout, merged, ya, yb, z):
    t = do1.shape[0]
    tm = _tile_big(t)

    def body(do1_ref, wo_ref, mg_ref, ya_ref, yb_ref, ga_ref, gb_ref, dya_ref, dyb_ref, dz_ref, dwo_ref):
        @pl.when(pl.program_id(0) == 0)
        def _():
            dwo_ref[...] = jnp.zeros_like(dwo_ref)

        do1v = do1_ref[...]
        dwo_ref[...] += _dot_tn(mg_ref[...], do1v)
        dm = _dot_nt(do1v, wo_ref[...])
        sa = _sigmoid(ga_ref[...].astype(F32))
        sb = _sigmoid(gb_ref[...].astype(F32))
        dya_ref[...] = (dm * sa).astype(BF16)
        dyb_ref[...] = (dm * sb).astype(BF16)
        dz_ref[:, 0:D] = (dm * ya_ref[...].astype(F32) * sa * (1.0 - sa)).astype(BF16)
        dz_ref[:, D:2 * D] = (dm * yb_ref[...].astype(F32) * sb * (1.0 - sb)).astype(BF16)

    tile = pl.BlockSpec((tm, D), lambda i: (i, 0))
    bshape = jax.ShapeDtypeStruct((t, D), BF16)
    return pl.pallas_call(
        body, name="out_bwd", grid=(t // tm,),
        in_specs=[tile, _const_spec((D, D)), tile, tile, tile,
                  pl.BlockSpec((tm, D), lambda i: (i, 4)), pl.BlockSpec((tm, D), lambda i: (i, 5))],
        out_specs=[tile, tile, pl.BlockSpec((tm, 2 * D), lambda i: (i, 2)), _const_spec((D, D))],
        out_shape=[bshape, bshape, jax.ShapeDtypeStruct((t, NCOL_IN), BF16), jax.ShapeDtypeStruct((D, D), F32)],
        compiler_params=_cparams(1),
    )(do1, wout, merged, ya, yb, z, z)


def _branch_bwd(dy, y_pre, wb, name):
    t = dy.shape[0]
    tm = _tile_big(t)

    def body(dy_ref, yp_ref, wb_ref, dyp_ref, dwb_ref):
        @pl.when(pl.program_id(0) == 0)
        def _():
            dwb_ref[...] = jnp.zeros_like(dwb_ref)

        dyv = dy_ref[...]
        dwb_ref[...] += _dot_tn(yp_ref[...], dyv)
        dyp_ref[...] = _dot_nt(dyv, wb_ref[...]).astype(BF16)

    tile = pl.BlockSpec((tm, D), lambda i: (i, 0))
    return pl.pallas_call(
        body, name=name, grid=(t // tm,),
        in_specs=[tile, tile, _const_spec((D, D))],
        out_specs=[tile, _const_spec((D, D))],
        out_shape=[jax.ShapeDtypeStruct((t, D), BF16), jax.ShapeDtypeStruct((D, D), F32)],
        compiler_params=_cparams(1),
    )(dy, y_pre, wb)


def _rnn_bwd(dyap, z, h, dz, cw, cb, wa, ba, wx, bx, lam):
    t = z.shape[0]
    tm = _tile_seq(t)
    nt = t // tm
    ngrp = tm // SUBLANES
    hpt = tm // HALO

    def body(dyap_ref, xr_ref, xp_ref, gr_ref, h_ref, hp_ref, dz_any, cw_ref, cb_ref, wa_ref, ba_ref, wx_ref, bx_ref,
             lam_ref, dz_ref, dcw_ref, dcb_ref, dwa_ref, dba_ref, dwx_ref, dbx_ref, dlam_ref,
             a_first, g_first, dxc_first, b_scr, d_scr, g_scr):
        del dz_any
        i = pl.program_id(0)

        @pl.when(i == 0)
        def _():
            for r in (dcw_ref, dcb_ref, dwa_ref, dba_ref, dwx_ref, dbx_ref, dlam_ref, a_first, g_first, dxc_first):
                r[...] = jnp.zeros_like(r)

        first_tile = i == nt - 1
        xr = xr_ref[...].astype(F32)
        prev8 = jnp.where(first_tile, 0.0, xp_ref[...].astype(F32)[HALO - SUBLANES:])
        xc = _conv4(xr, prev8, cw_ref, cb_ref[...])
        lam_v = lam_ref[...]
        ls = _log_sigmoid(lam_v)
        ra, ia, a, mult = _lru_gates(xc, wa_ref, ba_ref[...], wx_ref, bx_ref[...], ls)
        hv = h_ref[...]
        hprev8 = jnp.where(first_tile, 0.0, hp_ref[...][HALO - SUBLANES:])
        h_prev = _shift_down(hv, 1, hprev8)
        grv = gr_ref[...].astype(F32)
        gg, tg = _gelu_t(grv)
        dyap_v = dyap_ref[...].astype(F32)
        dz_ref[:, D:2 * D] = (dyap_v * hv * _gelu_grad(grv, tg)).astype(BF16)

        b_scr[...] = _shift_up(a, 1, a_first[...])
        d_scr[...] = dyap_v * gg
        row = _row_iota(D)

        def grp(jj, carry):
            r0 = pl.multiple_of((ngrp - 1 - jj) * SUBLANES, SUBLANES)
            bv = b_scr[pl.ds(r0, SUBLANES), :]
            dv = d_scr[pl.ds(r0, SUBLANES), :]
            for d in (1, 2, 4):
                m = row < SUBLANES - d
                dv = jnp.where(m, dv + bv * pltpu.roll(dv, SUBLANES - d, 0), dv)
                bv = jnp.where(m, bv * pltpu.roll(bv, SUBLANES - d, 0), bv)
            gv = dv + bv * carry
            g_scr[pl.ds(r0, SUBLANES), :] = gv
            return gv[0:1, :]

        lax.fori_loop(0, ngrp, grp, g_first[0:1, :])
        g = g_scr[...]
        a_first[...] = a[:SUBLANES]
        g_first[...] = g[:SUBLANES]

        da = g * h_prev
        gx = g * xc
        dmult = gx * ia
        dia = gx * mult
        dxc = g * (mult * ia)
        dla = da * a - dmult * (a * a) / mult
        dra = dla * (LRU_C * ls)
        dlam_ref[...] += _colsum(dla * ra) * (LRU_C * _sigmoid(-lam_v))
        dpa = dra * ra * (1.0 - ra)
        dpx = dia * ia * (1.0 - ia)
        dba_ref[...] += _colsum(dpa)
        dbx_ref[...] += _colsum(dpx)
        dpab = dpa.astype(BF16)
        dpxb = dpx.astype(BF16)
        xcb = xc.astype(BF16)
        for hd in range(NH):
            sl = slice(hd * HD, (hd + 1) * HD)
            dwa_ref[hd] += _dot_tn(xcb[:, sl], dpab[:, sl])
            dwx_ref[hd] += _dot_tn(xcb[:, sl], dpxb[:, sl])
        dxc = dxc + _heads_nt(dpab, wa_ref) + _heads_nt(dpxb, wx_ref)

        nxt = dxc_first[...]
        dxr = (cw_ref[3:4, :] * dxc + cw_ref[2:3, :] * _shift_up(dxc, 1, nxt)
               + cw_ref[1:2, :] * _shift_up(dxc, 2, nxt) + cw_ref[0:1, :] * _shift_up(dxc, 3, nxt))
        dz_ref[:, 0:D] = dxr.astype(BF16)
        dxc_first[...] = dxc[:SUBLANES]
        dcb_ref[...] += _colsum(dxc)
        for k in range(4):
            dcw_ref[k:k + 1, :] += _colsum(dxc * _shift_down(xr, 3 - k, prev8))

    def rev(col):
        return lambda i: (nt - 1 - i, col)

    def rev_halo(col):
        return lambda i: (jnp.maximum((nt - 1 - i) * hpt - 1, 0), col)

    vec = _const_spec((1, D))
    wspec = _const_spec((NH, HD, HD))
    vshape = jax.ShapeDtypeStruct((1, D), F32)
    wshape = jax.ShapeDtypeStruct((NH, HD, HD), F32)
    outs = pl.pallas_call(
        body, name="rnn_bwd", grid=(nt,),
        in_specs=[pl.BlockSpec((tm, D), rev(0)), pl.BlockSpec((tm, D), rev(0)), pl.BlockSpec((HALO, D), rev_halo(0)),
                  pl.BlockSpec((tm, D), rev(1)), pl.BlockSpec((tm, D), rev(0)), pl.BlockSpec((HALO, D), rev_halo(0)),
                  pl.BlockSpec(memory_space=pl.ANY),
                  _const_spec((4, D)), vec, wspec, vec, wspec, vec, vec],
        out_specs=[pl.BlockSpec((tm, 2 * D), rev(0)), _const_spec((4, D)), vec, wspec, vec, wspec, vec, vec],
        out_shape=[jax.ShapeDtypeStruct((t, NCOL_IN), BF16), jax.ShapeDtypeStruct((4, D), F32), vshape,
                   wshape, vshape, wshape, vshape, vshape],
        scratch_shapes=[pltpu.VMEM((SUBLANES, D), F32), pltpu.VMEM((SUBLANES, D), F32), pltpu.VMEM((SUBLANES, D), F32),
                        pltpu.VMEM((tm, D), F32), pltpu.VMEM((tm, D), F32), pltpu.VMEM((tm, D), F32)],
        input_output_aliases={6: 0},
        compiler_params=_cparams(1),
    )(dyap, z, z, z, h, h, dz, cw, cb, wa, ba, wx, bx, lam)
    return outs


def _sgu_bwd(dybp, z, dz, lng, lnb, wm, wmt, bst, mask):
    t = z.shape[0]
    tm = _tile_seq(t)

    def body(dybp_ref, zu_ref, zv_ref, dz_any, lng_ref, lnb_ref, wm_ref, wmt_ref, bst_ref, mask_ref,
             dz_ref, dws_ref, dbst_ref, dlng_ref, dlnb_ref):
        del dz_any

        @pl.when(pl.program_id(0) == 0)
        def _():
            for r in (dws_ref, dbst_ref, dlng_ref, dlnb_ref):
                r[...] = jnp.zeros_like(r)

        zu = zu_ref[...].astype(F32)
        zv = zv_ref[...].astype(F32)
        lng_v = lng_ref[...]
        gu, tu, tv, rstd, vhat, vb, mixed = _sgu_core(zu, zv, lng_v, lnb_ref[...], wm_ref, bst_ref)
        dyb = dybp_ref[...].astype(F32)
        dz_ref[:, 0:D] = (dyb * mixed * _gelu_grad(zu, tu)).astype(BF16)
        dmix = dyb * gu
        dmb = dmix.astype(BF16)
        rows = []
        lane = lax.broadcasted_iota(jnp.int32, (HD, NH), 1)
        dbst = jnp.zeros((HD, NH), F32)
        for b0 in range(0, tm, HD):
            cols = []
            for g in range(NH):
                sl = slice(g * HD, (g + 1) * HD)
                dmg = dmb[b0:b0 + HD, sl]
                dws_ref[g] += _dot_nt(dmg, vb[b0:b0 + HD, sl]) * mask_ref[...]
                cols.append(jnp.dot(wmt_ref[g], dmg, preferred_element_type=F32))
                dbst = dbst + jnp.where(lane == g, jnp.sum(dmix[b0:b0 + HD, sl], axis=1, keepdims=True), 0.0)
            rows.append(jnp.concatenate(cols, axis=1))
        dbst_ref[...] += dbst
        dvln = jnp.concatenate(rows, axis=0) if len(rows) > 1 else rows[0]
        dlng_ref[...] += _colsum(dvln * vhat)
        dlnb_ref[...] += _colsum(dvln)
        dvh = dvln * lng_v
        dgv = rstd * (dvh - jnp.mean(dvh, axis=-1, keepdims=True)
                      - vhat * jnp.mean(dvh * vhat, axis=-1, keepdims=True))
        dz_ref[:, D:2 * D] = (dgv * _gelu_grad(zv, tv)).astype(BF16)

    vec = _const_spec((1, D))
    wspec = _const_spec((NH, HD, HD))
    vshape = jax.ShapeDtypeStruct((1, D), F32)
    return pl.pallas_call(
        body, name="sgu_bwd", grid=(t // tm,),
        in_specs=[pl.BlockSpec((tm, D), lambda i: (i, 0)), pl.BlockSpec((tm, D), lambda i: (i, 2)),
                  pl.BlockSpec((tm, D), lambda i: (i, 3)), pl.BlockSpec(memory_space=pl.ANY),
                  vec, vec, wspec, wspec, _const_spec((HD, NH)), _const_spec((HD, HD))],
        out_specs=[pl.BlockSpec((tm, 2 * D), lambda i: (i, 1)), wspec, _const_spec((HD, NH)), vec, vec],
        out_shape=[jax.ShapeDtypeStruct((t, NCOL_IN), BF16), jax.ShapeDtypeStruct((NH, HD, HD), F32),
                   jax.ShapeDtypeStruct((HD, NH), F32), vshape, vshape],
        input_output_aliases={3: 0},
        compiler_params=_cparams(1),
    )(dybp, z, z, dz, lng, lnb, wm, wmt, bst, mask)


def _in_bwd(dz, win, x, dx2, g, scale1):
    t = x.shape[0]
    tm = _tile_big(t)

    def body(dz_ref, w_ref, x_ref, dx2_ref, g_ref, sc_ref, dx_ref, dsh_ref, dsc_ref, dg_ref):
        @pl.when(pl.program_id(0) == 0)
        def _():
            for r in (dsh_ref, dsc_ref, dg_ref):
                r[...] = jnp.zeros_like(r)

        dh = jnp.zeros((tm, D), F32)
        for c0 in range(0, NCOL_IN, D):
            dh = dh + _dot_nt(dz_ref[:, c0:c0 + D], w_ref[:, c0:c0 + D])
        dxn, dsh, dsc, dg = _modnorm_bwd(dh, x_ref[...], g_ref[...], sc_ref[...])
        dx_ref[...] = dx2_ref[...] + dxn
        dsh_ref[...] += dsh
        dsc_ref[...] += dsc
        dg_ref[...] += dg

    tile = pl.BlockSpec((tm, D), lambda i: (i, 0))
    vec = _const_spec((1, D))
    vshape = jax.ShapeDtypeStruct((1, D), F32)
    return pl.pallas_call(
        body, name="in_bwd", grid=(t // tm,),
        in_specs=[pl.BlockSpec((tm, NCOL_IN), lambda i: (i, 0)), _const_spec((D, NCOL_IN), True), tile, tile, vec, vec],
        out_specs=[tile, vec, vec, vec],
        out_shape=[jax.ShapeDtypeStruct((t, D), F32), vshape, vshape, vshape],
        compiler_params=_cparams(1),
    )(dz, win, x, dx2, g, scale1)


def _mod_cols(c_all, w_ada, b_cols):
    nb, cols = c_all.shape[0], w_ada.shape[1]

    def body(c_ref, w_ref, b_ref, o_ref):
        cv = c_ref[...]
        ca = (cv * _sigmoid(cv)).astype(BF16)
        o_ref[...] = jnp.dot(ca, w_ref[...].astype(BF16), preferred_element_type=F32) + b_ref[...]

    return pl.pallas_call(body, name="mod_cols", out_shape=jax.ShapeDtypeStruct((nb, cols), F32))(c_all, w_ada, b_cols)


def _ada_grad(c_all, dmod_cols):
    cols = dmod_cols.shape[1]

    def body(c_ref, d_ref, o_ref):
        cv = c_ref[...]
        ca = (cv * _sigmoid(cv)).astype(BF16)
        o_ref[...] = _dot_tn(ca, d_ref[...].astype(BF16))

    return pl.pallas_call(body, name="ada_grad", out_shape=jax.ShapeDtypeStruct((D, cols), F32))(c_all, dmod_cols)


def _adamw(w, m, v, parts, name):
    rows, cols = w.shape
    tr = _row_tile(rows)
    stacked = [p.ndim == 3 for p in parts]
    bc1 = 1.0 - ADAM_B1 ** ADAM_STEP
    bc2 = 1.0 - ADAM_B2 ** ADAM_STEP

    def body(*refs):
        w_ref, m_ref, v_ref = refs[:3]
        p_refs = refs[3:3 + len(parts)]
        g_ref, d_ref, mo_ref, vo_ref = refs[3 + len(parts):]
        g = None
        for p_ref, st in zip(p_refs, stacked):
            terms = [p_ref[k].astype(F32) for k in range(p_ref.shape[0])] if st else [p_ref[...].astype(F32)]
            for term in terms:
                g = term if g is None else g + term
        mn = ADAM_B1 * m_ref[...] + (1.0 - ADAM_B1) * g
        vn = ADAM_B2 * v_ref[...] + (1.0 - ADAM_B2) * (g * g)
        g_ref[...] = g
        mo_ref[...] = mn
        vo_ref[...] = vn
        d_ref[...] = -ADAM_LR * ((mn / bc1) / (jnp.sqrt(vn / bc2) + ADAM_EPS) + ADAM_WD * w_ref[...])

    tile = pl.BlockSpec((tr, cols), lambda i: (i, 0))
    p_specs = [pl.BlockSpec((p.shape[0], tr, cols), lambda i: (0, i, 0)) if st else tile for p, st in zip(parts, stacked)]
    shp = jax.ShapeDtypeStruct((rows, cols), F32)
    return pl.pallas_call(
        body, name=name, grid=(rows // tr,),
        in_specs=[tile, tile, tile] + p_specs, out_specs=[tile] * 4, out_shape=[shp] * 4,
        compiler_params=_cparams(1),
    )(w, m, v, *parts)


def _mesh_pos():
    return lax.axis_index("x"), lax.axis_index("y"), lax.axis_index("c")


def _other_chips(x, y):
    return [(1 - x, y), (x, 1 - y), (1 - x, 1 - y)]


def _block_of(ref, axis, index, size):
    if axis == 0:
        return ref.at[index]
    return ref.at[:, pl.ds(pl.multiple_of(index * size, 128), size)]


def _all_gather(shards, axes, name):
    n = len(shards)
    per = 7

    def body(*refs):
        ins, outs = refs[:n], refs[n:2 * n]
        send_sems, recv_sems, local_sems = refs[2 * n:]
        x, y, c = _mesh_pos()
        me, sibling = (x, y, c), (x, y, 1 - c)
        chips = _other_chips(x, y)

        def rows(a, pos):
            return _block_of(outs[a], axes[a], 4 * pos[0] + 2 * pos[1] + pos[2], shards[a].shape[-1])

        def copy(a, k, block, to, src=None):
            return pltpu.make_async_remote_copy(
                src_ref=rows(a, block) if src is None else src, dst_ref=rows(a, block),
                send_sem=send_sems.at[a * per + k], recv_sem=recv_sems.at[a * per + k],
                device_id=to, device_id_type=MESH_IDS)

        mine = [pltpu.make_async_copy(ins[a], rows(a, me), local_sems.at[a]) for a in range(n)]
        for cp in mine:
            cp.start()
        first = []
        for a in range(n):
            first.append(copy(a, 0, me, sibling, src=ins[a]))
            first += [copy(a, 1 + j, me, (*chip, c), src=ins[a]) for j, chip in enumerate(chips)]
        for cp in first:
            cp.start()
        passed = []
        for j, chip in enumerate(chips):
            for a in range(n):
                copy(a, 1 + j, (*chip, c), me).wait_recv()
                fwd = copy(a, 4 + j, (*chip, c), sibling)
                fwd.start()
                passed.append(fwd)
        for a in range(n):
            copy(a, 0, sibling, me).wait_recv()
            for j, chip in enumerate(chips):
                copy(a, 4 + j, (*chip, 1 - c), me).wait_recv()
        for cp in first + passed:
            cp.wait_send()
        for cp in mine:
            cp.wait()

    def full_shape(s, ax):
        return (N_DEV,) + s.shape if ax == 0 else s.shape[:-1] + (N_DEV * s.shape[-1],)

    any_spec = pl.BlockSpec(memory_space=pl.ANY)
    return pl.pallas_call(
        body, name=name,
        in_specs=[any_spec] * n, out_specs=[any_spec] * n,
        out_shape=[jax.ShapeDtypeStruct(full_shape(s, ax), s.dtype) for s, ax in zip(shards, axes)],
        scratch_shapes=[pltpu.SemaphoreType.DMA((n * per,)), pltpu.SemaphoreType.DMA((n * per,)),
                        pltpu.SemaphoreType.DMA((n,))],
    )(*shards)


def _chip_blocks(x, y):
    return [(x, y)] + _other_chips(x, y)


def _sibling_reduce(g, axis, name):
    rows, cols = (g.shape[1], g.shape[2]) if axis == 0 else (g.shape[0], g.shape[1] // N_DEV)
    chunk = math.gcd(rows, 64)

    def body(g_ref, own_ref, pay_ref, send_buf, keep_buf, recv_buf, send_sems, recv_sems, stage_sems, keep_sems):
        x, y, c = _mesh_pos()
        sibling = (x, y, 1 - c)
        chips = _chip_blocks(x, y)
        stage, keep, push = [], [], []
        for j, (px, py) in enumerate(chips):
            theirs = _block_of(g_ref, axis, 4 * px + 2 * py + (1 - c), cols)
            ours = _block_of(g_ref, axis, 4 * px + 2 * py + c, cols)
            stage.append(pltpu.make_async_copy(theirs, send_buf.at[j], stage_sems.at[j]))
            keep.append(pltpu.make_async_copy(ours, keep_buf.at[j], keep_sems.at[j]))
            push.append(pltpu.make_async_remote_copy(
                src_ref=send_buf.at[j], dst_ref=recv_buf.at[j], send_sem=send_sems.at[j], recv_sem=recv_sems.at[j],
                device_id=sibling, device_id_type=MESH_IDS))
        for j in range(4):
            stage[j].start()
        for j in range(4):
            keep[j].start()
        for j in range(4):
            stage[j].wait()
            push[j].start()
        for j in range(4):
            push[j].wait_recv()
            keep[j].wait()
            dst = own_ref if j == 0 else pay_ref.at[j - 1]

            def add(r, carry, j=j, dst=dst):
                sl = pl.ds(pl.multiple_of(r * chunk, chunk), chunk)
                dst[sl, :] = (keep_buf[j, sl, :] + recv_buf[j, sl, :]).astype(dst.dtype)
                return carry

            lax.fori_loop(0, rows // chunk, add, 0)
        for j in range(4):
            push[j].wait_send()

    vmem = pl.BlockSpec(memory_space=pltpu.VMEM)
    buf = pltpu.VMEM((4, rows, cols), F32)
    return pl.pallas_call(
        body, name=name,
        in_specs=[pl.BlockSpec(memory_space=pl.ANY)], out_specs=[vmem, vmem],
        out_shape=[jax.ShapeDtypeStruct((rows, cols), F32), jax.ShapeDtypeStruct((3, rows, cols), BF16)],
        scratch_shapes=[buf, buf, buf, pltpu.SemaphoreType.DMA((4,)), pltpu.SemaphoreType.DMA((4,)),
                        pltpu.SemaphoreType.DMA((4,)), pltpu.SemaphoreType.DMA((4,))],
        compiler_params=pltpu.CompilerParams(vmem_limit_bytes=VMEM_LIMIT),
    )(g)


def _chip_exchange(payloads, name):
    n = len(payloads)

    def body(*refs):
        ins, outs = refs[:n], refs[n:2 * n]
        send_sems, recv_sems = refs[2 * n:]
        x, y, c = _mesh_pos()
        copies = []
        for a in range(n):
            for j, chip in enumerate(_other_chips(x, y)):
                copies.append(pltpu.make_async_remote_copy(
                    src_ref=ins[a].at[j], dst_ref=outs[a].at[j], send_sem=send_sems.at[3 * a + j],
                    recv_sem=recv_sems.at[3 * a + j], device_id=(*chip, c), device_id_type=MESH_IDS))
        for cp in copies:
            cp.start()
        for cp in copies:
            cp.wait_recv()
        for cp in copies:
            cp.wait_send()

    any_spec = pl.BlockSpec(memory_space=pl.ANY)
    return pl.pallas_call(
        body, name=name,
        in_specs=[any_spec] * n, out_specs=[any_spec] * n,
        out_shape=[jax.ShapeDtypeStruct(p.shape, p.dtype) for p in payloads],
        scratch_shapes=[pltpu.SemaphoreType.DMA((3 * n,)), pltpu.SemaphoreType.DMA((3 * n,))],
    )(*payloads)


def _local_step(x, target, mod, win, wup, wba, wbb, wout, wd, p):
    shift1, scale1, gate1, shift2, scale2, gate2 = (mod[k] for k in range(6))
    wa, wx = p["lru_w_a"].astype(BF16), p["lru_w_x"].astype(BF16)
    mask = jnp.tril(jnp.ones((HD, HD), F32))
    wm = (p["sgu_w_s"] * mask).astype(BF16)
    wmt = jnp.swapaxes(wm, 1, 2)
    bst = jnp.transpose(p["sgu_b_s"])

    h1, z = _norm_proj(x, p["norm_mix_g"], scale1, shift1, win, "mix_proj")
    hstate, ya_pre = _rnn_fwd(z, p["rnn_conv_w"], p["rnn_conv_b"], wa, p["lru_b_a"], wx, p["lru_b_x"], p["lru_lambda"])
    yb_pre = _sgu_fwd(z, p["sgu_ln_g"], p["sgu_ln_b"], wm, bst)
    x2, ya, yb, merged, o1 = _merge_fwd(ya_pre, yb_pre, z, x, gate1, wba, wbb, wout)
    h2, up = _norm_proj(x2, p["norm_ffn_g"], scale2, shift2, wup, "ffn_proj")
    ff = _ffn_mid_fwd(up, p["ffn_conv_w"], p["ffn_conv_b"])
    dx3, loss, d_gfin, d_gate2 = _ffn_out_loss(ff, wd, x2, target, gate2, p["norm_final_g"])

    dact, dval, d_wd, dcw_a, dcw_v, dcb_a, dcb_v = _ffn_down_bwd(dx3, gate2, ff, up, p["ffn_conv_w"], p["ffn_conv_b"], wd)
    dup, dx2, do1, d_shift2, d_scale2, d_gffn, d_gate1 = _ffn_up_bwd(
        dact, dval, p["ffn_conv_w"], wup, x2, dx3, p["norm_ffn_g"], scale2, o1, gate1)
    d_wup = _xt_y(h2, dup, "w_up_grad")

    dya, dyb, dz, d_wout = _out_bwd(do1, wout, merged, ya, yb, z)
    dya_pre, d_wba = _branch_bwd(dya, ya_pre, wba, "branch_a_bwd")
    dyb_pre, d_wbb = _branch_bwd(dyb, yb_pre, wbb, "branch_b_bwd")
    dz, d_cw, d_cb, d_wa, d_ba, d_wx, d_bx, d_lam = _rnn_bwd(
        dya_pre, z, hstate, dz, p["rnn_conv_w"], p["rnn_conv_b"], wa, p["lru_b_a"], wx, p["lru_b_x"], p["lru_lambda"])
    dz, d_ws, d_bst, d_lng, d_lnb = _sgu_bwd(dyb_pre, z, dz, p["sgu_ln_g"], p["sgu_ln_b"], wm, wmt, bst, mask)
    grad_x, d_shift1, d_scale1, d_gmix = _in_bwd(dz, win, x, dx2, p["norm_mix_g"], scale1)
    d_win = _xt_y(h1, dz, "w_in_grad")

    small = {
        "norm_mix_g": d_gmix, "rnn_conv_w": d_cw, "rnn_conv_b": d_cb, "lru_w_a": d_wa, "lru_b_a": d_ba,
        "lru_w_x": d_wx, "lru_b_x": d_bx, "lru_lambda": d_lam, "sgu_ln_g": d_lng, "sgu_ln_b": d_lnb,
        "sgu_w_s": d_ws, "sgu_b_s": jnp.transpose(d_bst), "norm_ffn_g": d_gffn,
        "ffn_conv_w": jnp.concatenate([dcw_a, dcw_v], axis=1), "ffn_conv_b": jnp.concatenate([dcb_a, dcb_v], axis=1),
        "norm_final_g": d_gfin,
    }
    dmod = jnp.stack([d_shift1, d_scale1, d_gate1, d_shift2, d_scale2, d_gate2])
    big = {"w_in": d_win, "w_up": d_wup, "w_branch_a": d_wba, "w_branch_b": d_wbb, "w_out": d_wout, "w_down": d_wd}
    return loss, grad_x, big, small, dmod


REPLICATED = ["b_ada", "norm_mix_g", "rnn_conv_b", "lru_w_a", "lru_b_a", "lru_w_x", "lru_b_x", "lru_lambda",
              "sgu_ln_g", "sgu_ln_b", "sgu_w_s", "sgu_b_s", "norm_ffn_g", "ffn_conv_b", "norm_final_g"]
COL_SHARDED = ["rnn_conv_w", "ffn_conv_w"]
SMALL_NAMES = REPLICATED + COL_SHARDED
BIG_NAMES = ["w_in", "w_up", "w_branch_a", "w_branch_b", "w_out", "w_down"]
BIG_AXES = [1, 1, 0, 0, 0, 0]
WEIGHTS = ["w_ada", "b_ada", "norm_mix_g", "w_in", "rnn_conv_w", "rnn_conv_b", "lru_w_a", "lru_b_a", "lru_w_x",
           "lru_b_x", "lru_lambda", "sgu_ln_g", "sgu_ln_b", "sgu_w_s", "sgu_b_s", "w_branch_a", "w_branch_b",
           "w_out", "norm_ffn_g", "w_up", "ffn_conv_w", "ffn_conv_b", "w_down", "norm_final_g"]
LANES = 128


def _pack_rows(shape):
    return math.prod(shape) // LANES


def _pack(arrays):
    return jnp.concatenate([a.reshape(-1, LANES) for a in arrays], axis=0)


def _unpack(packed, shapes):
    out, r0 = [], 0
    for s in shapes:
        nrow = math.prod(s) // LANES
        out.append(packed[r0:r0 + nrow].reshape(s))
        r0 += nrow
    return out


def kernel(x, c, w_ada, b_ada, norm_mix_g, w_in, rnn_conv_w, rnn_conv_b, lru_w_a, lru_b_a, lru_w_x, lru_b_x, lru_lambda, sgu_ln_g, sgu_ln_b, sgu_w_s, sgu_b_s, w_branch_a, w_branch_b, w_out, norm_ffn_g, w_up, ffn_conv_w, ffn_conv_b, w_down, norm_final_g, loss_target, m_w_ada, m_b_ada, m_norm_mix_g, m_w_in, m_rnn_conv_w, m_rnn_conv_b, m_lru_w_a, m_lru_b_a, m_lru_w_x, m_lru_b_x, m_lru_lambda, m_sgu_ln_g, m_sgu_ln_b, m_sgu_w_s, m_sgu_b_s, m_w_branch_a, m_w_branch_b, m_w_out, m_norm_ffn_g, m_w_up, m_ffn_conv_w, m_ffn_conv_b, m_w_down, m_norm_final_g, v_w_ada, v_b_ada, v_norm_mix_g, v_w_in, v_rnn_conv_w, v_rnn_conv_b, v_lru_w_a, v_lru_b_a, v_lru_w_x, v_lru_b_x, v_lru_lambda, v_sgu_ln_g, v_sgu_ln_b, v_sgu_w_s, v_sgu_b_s, v_w_branch_a, v_w_branch_b, v_w_out, v_norm_ffn_g, v_w_up, v_ffn_conv_w, v_ffn_conv_b, v_w_down, v_norm_final_g):
    given = dict(locals())
    me = 4 * lax.axis_index("x") + 2 * lax.axis_index("y") + lax.axis_index("c")
    ada_cols = w_ada.shape[2]
    conv_cols = {"rnn_conv_w": rnn_conv_w.shape[2], "ffn_conv_w": ffn_conv_w.shape[2]}

    shards = [w_in[0].astype(BF16), w_up[0].astype(BF16), w_branch_a[0].astype(BF16), w_branch_b[0].astype(BF16),
              w_out[0].astype(BF16), w_down[0].astype(BF16), c.reshape(1, 1, D), rnn_conv_w[0], ffn_conv_w[0]]
    win, wup, wba, wbb, wout, wd, c_all, cw_rnn, cw_ffn = _all_gather(
        shards, BIG_AXES + [0, 1, 1], "gather_weights")
    wba, wbb, wout = (w.reshape(D, D) for w in (wba, wbb, wout))
    wd = wd.reshape(DFF, D)
    c_all = c_all.reshape(N_DEV, D)

    b_cols = lax.dynamic_slice_in_dim(b_ada, me * ada_cols, ada_cols, axis=1)
    (mod_all,) = _all_gather([_mod_cols(c_all, w_ada[0], b_cols).reshape(1, N_DEV, ada_cols)], [0], "gather_mod")
    mod_all = mod_all.reshape(N_DEV, N_DEV, ada_cols)
    mod_mine = lax.dynamic_index_in_dim(mod_all, me, axis=1, keepdims=False).reshape(6, 1, D)

    p = {n: given[n][0] for n in REPLICATED if n not in ("b_ada", "norm_final_g")}
    p = {n: (a.reshape(1, -1) if a.ndim == 1 else a) for n, a in p.items()}
    p["rnn_conv_w"], p["ffn_conv_w"] = cw_rnn, cw_ffn
    p["norm_final_g"] = norm_final_g.reshape(1, D)
    loss, grad_x, big, small, dmod = _local_step(x[0], loss_target[0], mod_mine, win, wup, wba, wbb, wout, wd, p)

    small["b_ada"] = dmod.reshape(1, 6 * D)
    rows_of = {n: _pack_rows(small[n].shape) for n in SMALL_NAMES}
    start_of = {n: sum(rows_of[q] for q in SMALL_NAMES[:k]) for k, n in enumerate(SMALL_NAMES)}
    pack = _pack([small[n] for n in SMALL_NAMES])
    (packs,) = _all_gather([pack[None]], [0], "gather_small")
    packs = packs.reshape(N_DEV, pack.shape[0], LANES)

    grads = [big[n] if ax == 1 else big[n].reshape(N_DEV, big[n].shape[0] // N_DEV, big[n].shape[1])
             for n, ax in zip(BIG_NAMES, BIG_AXES)]
    sums = [_sibling_reduce(g, ax, "reduce_sibling_" + n) for n, g, ax in zip(BIG_NAMES, grads, BIG_AXES)]
    landed = _chip_exchange([pay for _, pay in sums], "reduce_chips")
    out = {}
    for n, (own, _), got in zip(BIG_NAMES, sums, landed):
        out[n] = _adamw(given[n][0], given["m_" + n][0], given["v_" + n][0], [own, got], "adamw_" + n)

    dmod_all = packs[:, :rows_of["b_ada"]].reshape(N_DEV, 6 * D)
    dmod_cols = lax.dynamic_slice_in_dim(dmod_all, me * ada_cols, ada_cols, axis=1)
    out["w_ada"] = _adamw(w_ada[0], m_w_ada[0], v_w_ada[0], [_ada_grad(c_all, dmod_cols)], "adamw_w_ada")

    rep_rows = sum(rows_of[n] for n in REPLICATED)
    res = _adamw(*[_pack([given[pre + n] for n in REPLICATED]) for pre in ("", "m_", "v_")],
                 [packs[:, :rep_rows]], "adamw_small")
    unpacked = [_unpack(r, [given[n].shape for n in REPLICATED]) for r in res]
    for k, n in enumerate(REPLICATED):
        out[n] = tuple(u[k] for u in unpacked)

    for n in COL_SHARDED:
        full = packs[:, start_of[n]:start_of[n] + rows_of[n]].reshape(N_DEV, small[n].shape[0], small[n].shape[1])
        mine = lax.dynamic_slice_in_dim(full, me * conv_cols[n], conv_cols[n], axis=2)
        out[n] = _adamw(given[n][0], given["m_" + n][0], given["v_" + n][0], [mine], "adamw_" + n)

    total = lax.psum(loss[0, 0], ("x", "y", "c"))
    results = [total, grad_x[None]]
    for kind in range(4):
        results += [out[n][kind].reshape(given[n].shape) for n in WEIGHTS]
    return tuple(results)
```

```python
import math

import jax
import jax.numpy as jnp
from jax import lax
from jax.experimental import pallas as pl
from jax.experimental.pallas import tpu as pltpu

F32 = jnp.float32
BF16 = jnp.bfloat16
MESH_IDS = pl.DeviceIdType.MESH

D = 1024
NH = 8
HD = 128
NCOL_IN = 6 * D
DFF = 3 * D
N_DEV = 8
EPS = 1e-6
LRU_C = 8.0
ADAM_LR, ADAM_B1, ADAM_B2, ADAM_EPS, ADAM_WD, ADAM_STEP = 0.001, 0.9, 0.999, 1e-08, 0.01, 10

SUBLANES = 8
HALO = 16
VMEM_LIMIT = 56 * 1024 * 1024
GELU_K = math.sqrt(2.0 / math.pi)
GELU_C = 0.044715


def _cparams(n_axes):
    return pltpu.CompilerParams(dimension_semantics=("arbitrary",) * n_axes, vmem_limit_bytes=VMEM_LIMIT)


def _const_spec(shape, single_buffer=False):
    nd = len(shape)
    if single_buffer:
        return pl.BlockSpec(shape, lambda *_: (0,) * nd, pipeline_mode=pl.Buffered(1))
    return pl.BlockSpec(shape, lambda *_: (0,) * nd)


def _tile_big(t):
    return min(512, t)


def _tile_seq(t):
    return min(256, t)


def _row_tile(rows):
    if rows <= 512:
        return rows
    return next(tr for tr in range(512, 0, -SUBLANES) if rows % tr == 0)


def _gelu_t(x):
    t = jnp.tanh(GELU_K * (x + GELU_C * (x * x * x)))
    return 0.5 * x * (1.0 + t), t


def _gelu_grad(x, t):
    return 0.5 * (1.0 + t) + 0.5 * x * (1.0 - t * t) * (GELU_K * (1.0 + 3.0 * GELU_C * x * x))


def _sigmoid(x):
    return 1.0 / (1.0 + jnp.exp(-x))


def _log_sigmoid(x):
    return -(jnp.maximum(-x, 0.0) + jnp.log1p(jnp.exp(-jnp.abs(x))))


def _row_iota(cols):
    return lax.broadcasted_iota(jnp.int32, (SUBLANES, cols), 0)


def _shift_down(x, k, prev8):
    if k == 0:
        return x
    r = pltpu.roll(x, k, 0)
    p = pltpu.roll(prev8, k, 0)
    head = jnp.where(_row_iota(x.shape[1]) < k, p, r[:SUBLANES])
    return jnp.concatenate([head, r[SUBLANES:]], axis=0)


def _shift_up(x, k, next8):
    if k == 0:
        return x
    n = x.shape[0]
    r = pltpu.roll(x, n - k, 0)
    q = pltpu.roll(next8, SUBLANES - k, 0)
    tail = jnp.where(_row_iota(x.shape[1]) >= SUBLANES - k, q, r[n - SUBLANES:])
    return jnp.concatenate([r[:n - SUBLANES], tail], axis=0)


def _heads_nn(x_bf, w_ref):
    return jnp.concatenate(
        [jnp.dot(x_bf[:, h * HD:(h + 1) * HD], w_ref[h], preferred_element_type=F32) for h in range(NH)], axis=1)


def _heads_nt(x_bf, w_ref):
    return jnp.concatenate(
        [lax.dot_general(x_bf[:, h * HD:(h + 1) * HD], w_ref[h], (((1,), (1,)), ((), ())), preferred_element_type=F32)
         for h in range(NH)], axis=1)


def _dot_nt(a, b):
    return lax.dot_general(a, b, (((1,), (1,)), ((), ())), preferred_element_type=F32)


def _dot_tn(a, b):
    return lax.dot_general(a, b, (((0,), (0,)), ((), ())), preferred_element_type=F32)


def _colsum(x):
    return jnp.sum(x, axis=0, keepdims=True)


def _prev_halo_map(tm, col):
    return lambda i, *_: (jnp.maximum(i * (tm // HALO) - 1, 0), col)


def _norm_proj(x, g, scale, shift, w, name):
    t, n = x.shape[0], w.shape[1]
    tm = _tile_big(t)

    def body(x_ref, g_ref, sc_ref, sh_ref, w_ref, h_ref, z_ref):
        xv = x_ref[...]
        r = lax.rsqrt(jnp.mean(xv * xv, axis=-1, keepdims=True) + EPS)
        hb = ((xv * r * g_ref[...]) * (1.0 + sc_ref[...]) + sh_ref[...]).astype(BF16)
        h_ref[...] = hb
        for c0 in range(0, n, D):
            z_ref[:, c0:c0 + D] = jnp.dot(hb, w_ref[:, c0:c0 + D], preferred_element_type=F32).astype(BF16)

    vec = _const_spec((1, D))
    return pl.pallas_call(
        body, name=name, grid=(t // tm,),
        in_specs=[pl.BlockSpec((tm, D), lambda i: (i, 0)), vec, vec, vec, _const_spec((D, n), True)],
        out_specs=[pl.BlockSpec((tm, D), lambda i: (i, 0)), pl.BlockSpec((tm, n), lambda i: (i, 0))],
        out_shape=[jax.ShapeDtypeStruct((t, D), BF16), jax.ShapeDtypeStruct((t, n), BF16)],
        compiler_params=_cparams(1),
    )(x, g, scale, shift, w)


def _lru_gates(xc, wa_ref, ba, wx_ref, bx, ls):
    xb = xc.astype(BF16)
    ra = _sigmoid(_heads_nn(xb, wa_ref) + ba)
    ia = _sigmoid(_heads_nn(xb, wx_ref) + bx)
    la = LRU_C * ra * ls
    a = jnp.exp(la)
    mult = jnp.sqrt(-jnp.tanh(la) * (1.0 + a * a))
    return ra, ia, a, mult


def _conv4(xr, prev8, cw_ref, cb):
    return (cb + cw_ref[3:4, :] * xr + cw_ref[2:3, :] * _shift_down(xr, 1, prev8)
            + cw_ref[1:2, :] * _shift_down(xr, 2, prev8) + cw_ref[0:1, :] * _shift_down(xr, 3, prev8))


def _rnn_fwd(z, cw, cb, wa, ba, wx, bx, lam):
    t = z.shape[0]
    tm = _tile_seq(t)
    ngrp = tm // SUBLANES

    def body(xr_ref, xp_ref, gr_ref, cw_ref, cb_ref, wa_ref, ba_ref, wx_ref, bx_ref, lam_ref,
             h_ref, ya_ref, carry_ref, a_scr, u_scr):
        i = pl.program_id(0)

        @pl.when(i == 0)
        def _():
            carry_ref[...] = jnp.zeros_like(carry_ref)

        xr = xr_ref[...].astype(F32)
        prev8 = jnp.where(i == 0, 0.0, xp_ref[...].astype(F32)[HALO - SUBLANES:])
        xc = _conv4(xr, prev8, cw_ref, cb_ref[...])
        _, ia, a, mult = _lru_gates(xc, wa_ref, ba_ref[...], wx_ref, bx_ref[...], _log_sigmoid(lam_ref[...]))
        a_scr[...] = a
        u_scr[...] = mult * (ia * xc)
        row = _row_iota(D)

        def grp(j, carry):
            r0 = pl.multiple_of(j * SUBLANES, SUBLANES)
            av = a_scr[pl.ds(r0, SUBLANES), :]
            uv = u_scr[pl.ds(r0, SUBLANES), :]
            for d in (1, 2, 4):
                m = row >= d
                uv = jnp.where(m, av * pltpu.roll(uv, d, 0) + uv, uv)
                av = jnp.where(m, av * pltpu.roll(av, d, 0), av)
            hv = uv + av * carry
            h_ref[pl.ds(r0, SUBLANES), :] = hv
            return hv[SUBLANES - 1:SUBLANES, :]

        carry_ref[0:1, :] = lax.fori_loop(0, ngrp, grp, carry_ref[0:1, :])
        gg, _ = _gelu_t(gr_ref[...].astype(F32))
        ya_ref[...] = (h_ref[...] * gg).astype(BF16)

    vec = _const_spec((1, D))
    wspec = _const_spec((NH, HD, HD))
    return pl.pallas_call(
        body, name="rnn_fwd", grid=(t // tm,),
        in_specs=[pl.BlockSpec((tm, D), lambda i: (i, 0)), pl.BlockSpec((HALO, D), _prev_halo_map(tm, 0)),
                  pl.BlockSpec((tm, D), lambda i: (i, 1)), _const_spec((4, D)), vec, wspec, vec, wspec, vec, vec],
        out_specs=[pl.BlockSpec((tm, D), lambda i: (i, 0)), pl.BlockSpec((tm, D), lambda i: (i, 0))],
        out_shape=[jax.ShapeDtypeStruct((t, D), F32), jax.ShapeDtypeStruct((t, D), BF16)],
        scratch_shapes=[pltpu.VMEM((SUBLANES, D), F32), pltpu.VMEM((tm, D), F32), pltpu.VMEM((tm, D), F32)],
        compiler_params=_cparams(1),
    )(z, z, z, cw, cb, wa, ba, wx, bx, lam)


def _sgu_core(zu, zv, lng, lnb, wm_ref, bst_ref):
    gu, tu = _gelu_t(zu)
    gv, tv = _gelu_t(zv)
    mu = jnp.mean(gv, axis=-1, keepdims=True)
    cen = gv - mu
    rstd = lax.rsqrt(jnp.mean(cen * cen, axis=-1, keepdims=True) + EPS)
    vhat = cen * rstd
    vln = vhat * lng + lnb
    vb = vln.astype(BF16)
    rows = []
    for b0 in range(0, zu.shape[0], HD):
        rows.append(jnp.concatenate(
            [jnp.dot(wm_ref[g], vb[b0:b0 + HD, g * HD:(g + 1) * HD], preferred_element_type=F32)
             + bst_ref[:, g:g + 1] for g in range(NH)], axis=1))
    mixed = jnp.concatenate(rows, axis=0) if len(rows) > 1 else rows[0]
    return gu, tu, tv, rstd, vhat, vb, mixed


def _sgu_fwd(z, lng, lnb, wm, bst):
    t = z.shape[0]
    tm = _tile_seq(t)

    def body(zu_ref, zv_ref, lng_ref, lnb_ref, wm_ref, bst_ref, yb_ref):
        gu, _, _, _, _, _, mixed = _sgu_core(zu_ref[...].astype(F32), zv_ref[...].astype(F32),
                                             lng_ref[...], lnb_ref[...], wm_ref, bst_ref)
        yb_ref[...] = (gu * mixed).astype(BF16)

    vec = _const_spec((1, D))
    return pl.pallas_call(
        body, name="sgu_fwd", grid=(t // tm,),
        in_specs=[pl.BlockSpec((tm, D), lambda i: (i, 2)), pl.BlockSpec((tm, D), lambda i: (i, 3)), vec, vec,
                  _const_spec((NH, HD, HD)), _const_spec((HD, NH))],
        out_specs=pl.BlockSpec((tm, D), lambda i: (i, 0)),
        out_shape=jax.ShapeDtypeStruct((t, D), BF16),
        compiler_params=_cparams(1),
    )(z, z, lng, lnb, wm, bst)


def _merge_fwd(ya_pre, yb_pre, z, x, gate1, wba, wbb, wout):
    t = x.shape[0]
    tm = _tile_big(t)

    def body(yap_ref, ybp_ref, ga_ref, gb_ref, x_ref, g1_ref, wba_ref, wbb_ref, wo_ref,
             x2_ref, ya_ref, yb_ref, mg_ref, o1_ref):
        ya = jnp.dot(yap_ref[...], wba_ref[...], preferred_element_type=F32)
        yb = jnp.dot(ybp_ref[...], wbb_ref[...], preferred_element_type=F32)
        merged = _sigmoid(ga_ref[...].astype(F32)) * ya + _sigmoid(gb_ref[...].astype(F32)) * yb
        mb = merged.astype(BF16)
        o1 = jnp.dot(mb, wo_ref[...], preferred_element_type=F32)
        x2_ref[...] = x_ref[...] + g1_ref[...] * o1
        ya_ref[...] = ya.astype(BF16)
        yb_ref[...] = yb.astype(BF16)
        mg_ref[...] = mb
        o1_ref[...] = o1.astype(BF16)

    tile = pl.BlockSpec((tm, D), lambda i: (i, 0))
    wspec = _const_spec((D, D))
    bshape = jax.ShapeDtypeStruct((t, D), BF16)
    return pl.pallas_call(
        body, name="merge_fwd", grid=(t // tm,),
        in_specs=[tile, tile, pl.BlockSpec((tm, D), lambda i: (i, 4)), pl.BlockSpec((tm, D), lambda i: (i, 5)),
                  tile, _const_spec((1, D)), wspec, wspec, wspec],
        out_specs=[tile] * 5,
        out_shape=[jax.ShapeDtypeStruct((t, D), F32), bshape, bshape, bshape, bshape],
        compiler_params=_cparams(1),
    )(ya_pre, yb_pre, z, z, x, gate1, wba, wbb, wout)


def _conv3(u, prev8, cw_ref, cb):
    return cb + cw_ref[2:3, :] * u + cw_ref[1:2, :] * _shift_down(u, 1, prev8) + cw_ref[0:1, :] * _shift_down(u, 2, prev8)


def _ffn_proj_mid(x2, g, scale, shift, w, cw, cb):
    t = x2.shape[0]
    tm = _tile_big(t)
    nc = DFF // D

    def body(x_ref, g_ref, sc_ref, sh_ref, w_ref, cw_ref, cb_ref, h_ref, up_ref, ff_ref, prev_ref):
        @pl.when(pl.program_id(0) == 0)
        def _():
            prev_ref[...] = jnp.zeros_like(prev_ref)

        xv = x_ref[...]
        r = lax.rsqrt(jnp.mean(xv * xv, axis=-1, keepdims=True) + EPS)
        hb = ((xv * r * g_ref[...]) * (1.0 + sc_ref[...]) + sh_ref[...]).astype(BF16)
        h_ref[...] = hb
        for c in range(nc):
            halves = []
            for c0 in (c * D, DFF + c * D):
                ub = jnp.dot(hb, w_ref[:, c0:c0 + D], preferred_element_type=F32).astype(BF16)
                up_ref[:, c0:c0 + D] = ub
                u = ub.astype(F32)
                halves.append(_conv3(u, prev_ref[:, c0:c0 + D], cw_ref[:, c0:c0 + D], cb_ref[:, c0:c0 + D]))
                prev_ref[:, c0:c0 + D] = u[tm - SUBLANES:]
            ga, _ = _gelu_t(halves[0])
            ff_ref[:, c * D:(c + 1) * D] = (ga * halves[1]).astype(BF16)

    vec = _const_spec((1, D))
    n = 2 * DFF
    return pl.pallas_call(
        body, name="ffn_proj_mid", grid=(t // tm,),
        in_specs=[pl.BlockSpec((tm, D), lambda i: (i, 0)), vec, vec, vec, _const_spec((D, n), True),
                  _const_spec((3, n)), _const_spec((1, n))],
        out_specs=[pl.BlockSpec((tm, D), lambda i: (i, 0)), pl.BlockSpec((tm, n), lambda i: (i, 0)),
                   pl.BlockSpec((tm, DFF), lambda i: (i, 0))],
        out_shape=[jax.ShapeDtypeStruct((t, D), BF16), jax.ShapeDtypeStruct((t, n), BF16),
                   jax.ShapeDtypeStruct((t, DFF), BF16)],
        scratch_shapes=[pltpu.VMEM((SUBLANES, n), F32)],
        compiler_params=_cparams(1),
    )(x2, g, scale, shift, w, cw, cb)


def _ffn_out_loss(ff, wd, x2, target, gate2, gfin):
    t = x2.shape[0]
    tm = _tile_big(t)

    def body(ff_ref, wd_ref, x2_ref, tg_ref, g2_ref, gf_ref, dx3_ref, loss_ref, dgf_ref, dg2_ref):
        @pl.when(pl.program_id(0) == 0)
        def _():
            loss_ref[...] = jnp.zeros_like(loss_ref)
            dgf_ref[...] = jnp.zeros_like(dgf_ref)
            dg2_ref[...] = jnp.zeros_like(dg2_ref)

        o2 = jnp.dot(ff_ref[...], wd_ref[...], preferred_element_type=F32)
        x3 = x2_ref[...] + g2_ref[...] * o2
        r = lax.rsqrt(jnp.mean(x3 * x3, axis=-1, keepdims=True) + EPS)
        xhat = x3 * r
        err = xhat * gf_ref[...] - tg_ref[...]
        loss_ref[...] += 0.5 * jnp.sum(jnp.mean(err * err, axis=-1, keepdims=True), axis=0, keepdims=True)
        dy = err * (1.0 / D)
        dgf_ref[...] += _colsum(dy * xhat)
        dxh = dy * gf_ref[...]
        dx3 = r * (dxh - xhat * jnp.mean(dxh * xhat, axis=-1, keepdims=True))
        dx3_ref[...] = dx3
        dg2_ref[...] += _colsum(dx3 * o2)

    tile = pl.BlockSpec((tm, D), lambda i: (i, 0))
    vec = _const_spec((1, D))
    return pl.pallas_call(
        body, name="ffn_out_loss", grid=(t // tm,),
        in_specs=[pl.BlockSpec((tm, DFF), lambda i: (i, 0)), _const_spec((DFF, D), True), tile, tile, vec, vec],
        out_specs=[tile, _const_spec((1, 1)), vec, vec],
        out_shape=[jax.ShapeDtypeStruct((t, D), F32), jax.ShapeDtypeStruct((1, 1), F32),
                   jax.ShapeDtypeStruct((1, D), F32), jax.ShapeDtypeStruct((1, D), F32)],
        compiler_params=_cparams(1),
    )(ff, wd, x2, target, gate2, gfin)


def _ffn_down_bwd(dx3, gate2, ff, up, cw, cb, wd):
    t = dx3.shape[0]
    tm = _tile_big(t)
    nc = DFF // D

    def body(dx3_ref, g2_ref, ff_ref, ua_ref, uap_ref, uv_ref, uvp_ref, cwa_ref, cwv_ref, cba_ref, cbv_ref, wd_ref,
             da_ref, dv_ref, dwd_ref, dcwa_ref, dcwv_ref, dcba_ref, dcbv_ref):
        i = pl.program_id(1)

        @pl.when(i == 0)
        def _():
            for r in (dwd_ref, dcwa_ref, dcwv_ref, dcba_ref, dcbv_ref):
                r[...] = jnp.zeros_like(r)

        first = i == 0
        ua = ua_ref[...].astype(F32)
        uv = uv_ref[...].astype(F32)
        pa = jnp.where(first, 0.0, uap_ref[...].astype(F32)[HALO - SUBLANES:])
        pv = jnp.where(first, 0.0, uvp_ref[...].astype(F32)[HALO - SUBLANES:])
        act = _conv3(ua, pa, cwa_ref, cba_ref[...])
        val = _conv3(uv, pv, cwv_ref, cbv_ref[...])
        do2 = (dx3_ref[...] * g2_ref[...]).astype(BF16)
        dwd_ref[...] += _dot_tn(ff_ref[...], do2)
        dff = _dot_nt(do2, wd_ref[...])
        ga, ta = _gelu_t(act)
        dact = dff * val * _gelu_grad(act, ta)
        dval = dff * ga
        da_ref[...] = dact.astype(BF16)
        dv_ref[...] = dval.astype(BF16)
        dcba_ref[...] += _colsum(dact)
        dcbv_ref[...] += _colsum(dval)
        for k in range(3):
            dcwa_ref[k:k + 1, :] += _colsum(dact * _shift_down(ua, 2 - k, pa))
            dcwv_ref[k:k + 1, :] += _colsum(dval * _shift_down(uv, 2 - k, pv))

    def halo(col):
        return lambda c, i: (jnp.maximum(i * (tm // HALO) - 1, 0), col(c))

    return pl.pallas_call(
        body, name="ffn_down_bwd", grid=(nc, t // tm),
        in_specs=[pl.BlockSpec((tm, D), lambda c, i: (i, 0)), pl.BlockSpec((1, D), lambda c, i: (0, 0)),
                  pl.BlockSpec((tm, D), lambda c, i: (i, c)),
                  pl.BlockSpec((tm, D), lambda c, i: (i, c)), pl.BlockSpec((HALO, D), halo(lambda c: c)),
                  pl.BlockSpec((tm, D), lambda c, i: (i, nc + c)), pl.BlockSpec((HALO, D), halo(lambda c: nc + c)),
                  pl.BlockSpec((3, D), lambda c, i: (0, c)), pl.BlockSpec((3, D), lambda c, i: (0, nc + c)),
                  pl.BlockSpec((1, D), lambda c, i: (0, c)), pl.BlockSpec((1, D), lambda c, i: (0, nc + c)),
                  pl.BlockSpec((D, D), lambda c, i: (c, 0))],
        out_specs=[pl.BlockSpec((tm, D), lambda c, i: (i, c)), pl.BlockSpec((tm, D), lambda c, i: (i, c)),
                   pl.BlockSpec((D, D), lambda c, i: (c, 0)),
                   pl.BlockSpec((3, D), lambda c, i: (0, c)), pl.BlockSpec((3, D), lambda c, i: (0, c)),
                   pl.BlockSpec((1, D), lambda c, i: (0, c)), pl.BlockSpec((1, D), lambda c, i: (0, c))],
        out_shape=[jax.ShapeDtypeStruct((t, DFF), BF16), jax.ShapeDtypeStruct((t, DFF), BF16),
                   jax.ShapeDtypeStruct((DFF, D), F32),
                   jax.ShapeDtypeStruct((3, DFF), F32), jax.ShapeDtypeStruct((3, DFF), F32),
                   jax.ShapeDtypeStruct((1, DFF), F32), jax.ShapeDtypeStruct((1, DFF), F32)],
        compiler_params=_cparams(2),
    )(dx3, gate2, ff, up, up, up, up, cw, cw, cb, cb, wd)


def _modnorm_bwd(dh, xv, g, scale):
    r = lax.rsqrt(jnp.mean(xv * xv, axis=-1, keepdims=True) + EPS)
    xhat = xv * r
    dxn = dh * (1.0 + scale)
    dxh = dxn * g
    dx = r * (dxh - xhat * jnp.mean(dxh * xhat, axis=-1, keepdims=True))
    return dx, _colsum(dh), _colsum(dh * (xhat * g)), _colsum(dxn * xhat)


def _ffn_up_bwd(dact, dval, cw, wup, x2, dx3, gffn, scale2, o1, gate1):
    t = x2.shape[0]
    tm = _tile_seq(t)
    nt = t // tm
    nc = DFF // D

    def body(da_ref, dan_ref, dv_ref, dvn_ref, cw_ref, w_ref, x2_ref, dx3_ref, g_ref, sc_ref, o1_ref, g1_ref,
             dup_ref, dx2_ref, do1_ref, dsh_ref, dsc_ref, dg_ref, dg1_ref):
        i = pl.program_id(0)

        @pl.when(i == 0)
        def _():
            for r in (dsh_ref, dsc_ref, dg_ref, dg1_ref):
                r[...] = jnp.zeros_like(r)

        last = i == nt - 1
        dh = jnp.zeros((tm, D), F32)
        for half, (d_ref, dn_ref) in enumerate(((da_ref, dan_ref), (dv_ref, dvn_ref))):
            nxt = jnp.where(last, 0.0, dn_ref[...].astype(F32)[:SUBLANES])
            for c in range(nc):
                c0 = half * DFF + c * D
                dv = d_ref[:, c * D:(c + 1) * D].astype(F32)
                nx = nxt[:, c * D:(c + 1) * D]
                dup = (cw_ref[2:3, c0:c0 + D] * dv + cw_ref[1:2, c0:c0 + D] * _shift_up(dv, 1, nx)
                       + cw_ref[0:1, c0:c0 + D] * _shift_up(dv, 2, nx)).astype(BF16)
                dup_ref[:, c0:c0 + D] = dup
                dh = dh + _dot_nt(dup, w_ref[:, c0:c0 + D])
        dxn, dsh, dsc, dg = _modnorm_bwd(dh, x2_ref[...], g_ref[...], sc_ref[...])
        dx2 = dx3_ref[...] + dxn
        dx2_ref[...] = dx2
        do1_ref[...] = (dx2 * g1_ref[...]).astype(BF16)
        dsh_ref[...] += dsh
        dsc_ref[...] += dsc
        dg_ref[...] += dg
        dg1_ref[...] += _colsum(dx2 * o1_ref[...].astype(F32))

    tile = pl.BlockSpec((tm, D), lambda i: (i, 0))
    wide = pl.BlockSpec((tm, DFF), lambda i: (i, 0))
    nxt = pl.BlockSpec((HALO, DFF), lambda i: (jnp.minimum((i + 1) * (tm // HALO), t // HALO - 1), 0))
    vec = _const_spec((1, D))
    vshape = jax.ShapeDtypeStruct((1, D), F32)
    return pl.pallas_call(
        body, name="ffn_up_bwd", grid=(nt,),
        in_specs=[wide, nxt, wide, nxt, _const_spec((3, 2 * DFF)), _const_spec((D, 2 * DFF), True),
                  tile, tile, vec, vec, tile, vec],
        out_specs=[pl.BlockSpec((tm, 2 * DFF), lambda i: (i, 0)), tile, tile, vec, vec, vec, vec],
        out_shape=[jax.ShapeDtypeStruct((t, 2 * DFF), BF16), jax.ShapeDtypeStruct((t, D), F32),
                   jax.ShapeDtypeStruct((t, D), BF16), vshape, vshape, vshape, vshape],
        compiler_params=_cparams(1),
    )(dact, dact, dval, dval, cw, wup, x2, dx3, gffn, scale2, o1, gate1)


def _xt_y(a, b, name):
    t, k = a.shape
    n = b.shape[1]
    tm = min(1024, t)
    bn = 768 if n % 768 == 0 else D

    def body(a_ref, b_ref, o_ref):
        @pl.when(pl.program_id(1) == 0)
        def _():
            o_ref[...] = jnp.zeros_like(o_ref)

        o_ref[...] += _dot_tn(a_ref[...], b_ref[...])

    return pl.pallas_call(
        body, name=name, grid=(n // bn, t // tm),
        in_specs=[pl.BlockSpec((tm, k), lambda j, i: (i, 0)), pl.BlockSpec((tm, bn), lambda j, i: (i, j))],
        out_specs=pl.BlockSpec((k, bn), lambda j, i: (0, j)),
        out_shape=jax.ShapeDtypeStruct((k, n), F32),
        compiler_params=_cparams(2),
    )(a, b)


def _acc_spec(shape, index):
    return pl.BlockSpec(shape, lambda *_: index, pipeline_mode=pl.Buffered(1))


def _out_bwd(do1, wout, merged, ya, yb, z, h1):
    t = do1.shape[0]
    tm = _tile_big(t)

    def body(do1_ref, wo_ref, mg_ref, ya_ref, yb_ref, ga_ref, gb_ref, h1_ref,
             dya_ref, dyb_ref, dz_ref, dwo_ref, dwin_ref):
        @pl.when(pl.program_id(0) == 0)
        def _():
            dwo_ref[...] = jnp.zeros_like(dwo_ref)
            dwin_ref[...] = jnp.zeros_like(dwin_ref)

        do1v = do1_ref[...]
        dwo_ref[...] += _dot_tn(mg_ref[...], do1v)
        dm = _dot_nt(do1v, wo_ref[...])
        sa = _sigmoid(ga_ref[...].astype(F32))
        sb = _sigmoid(gb_ref[...].astype(F32))
        dya_ref[...] = (dm * sa).astype(BF16)
        dyb_ref[...] = (dm * sb).astype(BF16)
        dga = (dm * ya_ref[...].astype(F32) * sa * (1.0 - sa)).astype(BF16)
        dgb = (dm * yb_ref[...].astype(F32) * sb * (1.0 - sb)).astype(BF16)
        dz_ref[:, 0:D] = dga
        dz_ref[:, D:2 * D] = dgb
        h1v = h1_ref[...]
        dwin_ref[:, 0:D] += _dot_tn(h1v, dga)
        dwin_ref[:, D:2 * D] += _dot_tn(h1v, dgb)

    tile = pl.BlockSpec((tm, D), lambda i: (i, 0))
    bshape = jax.ShapeDtypeStruct((t, D), BF16)
    return pl.pallas_call(
        body, name="out_bwd", grid=(t // tm,),
        in_specs=[tile, _const_spec((D, D), True), tile, tile, tile,
                  pl.BlockSpec((tm, D), lambda i: (i, 4)), pl.BlockSpec((tm, D), lambda i: (i, 5)), tile],
        out_specs=[tile, tile, pl.BlockSpec((tm, 2 * D), lambda i: (i, 2)), _acc_spec((D, D), (0, 0)),
                   _acc_spec((D, 2 * D), (0, 2))],
        out_shape=[bshape, bshape, jax.ShapeDtypeStruct((t, NCOL_IN), BF16), jax.ShapeDtypeStruct((D, D), F32),
                   jax.ShapeDtypeStruct((D, NCOL_IN), F32)],
        compiler_params=_cparams(1),
    )(do1, wout, merged, ya, yb, z, z, h1)


def _rnn_bwd(dya, ya_pre, wba, h1, z, h, dz, dwin, cw, cb, wa, ba, wx, bx, lam):
    t = z.shape[0]
    tm = _tile_seq(t)
    nt = t // tm
    ngrp = tm // SUBLANES
    hpt = tm // HALO

    def body(dya_ref, yap_ref, wba_ref, h1_ref, xr_ref, xp_ref, gr_ref, h_ref, hp_ref, dz_any, dwin_any,
             cw_ref, cb_ref, wa_ref, ba_ref, wx_ref, bx_ref, lam_ref,
             dz_ref, dwin_ref, dwba_ref, dcw_ref, dcb_ref, dwa_ref, dba_ref, dwx_ref, dbx_ref, dlam_ref,
             a_first, g_first, dxc_first, b_scr, d_scr, g_scr):
        del dz_any, dwin_any
        i = pl.program_id(0)

        @pl.when(i == 0)
        def _():
            for r in (dwin_ref, dwba_ref, dcw_ref, dcb_ref, dwa_ref, dba_ref, dwx_ref, dbx_ref, dlam_ref,
                      a_first, g_first, dxc_first):
                r[...] = jnp.zeros_like(r)

        dya_v = dya_ref[...]
        dwba_ref[...] += _dot_tn(yap_ref[...], dya_v)
        dyap_v = _dot_nt(dya_v, wba_ref[...])
        h1v = h1_ref[...]

        first_tile = i == nt - 1
        xr = xr_ref[...].astype(F32)
        prev8 = jnp.where(first_tile, 0.0, xp_ref[...].astype(F32)[HALO - SUBLANES:])
        xc = _conv4(xr, prev8, cw_ref, cb_ref[...])
        lam_v = lam_ref[...]
        ls = _log_sigmoid(lam_v)
        ra, ia, a, mult = _lru_gates(xc, wa_ref, ba_ref[...], wx_ref, bx_ref[...], ls)
        hv = h_ref[...]
        hprev8 = jnp.where(first_tile, 0.0, hp_ref[...][HALO - SUBLANES:])
        h_prev = _shift_down(hv, 1, hprev8)
        grv = gr_ref[...].astype(F32)
        gg, tg = _gelu_t(grv)
        dgr = (dyap_v * hv * _gelu_grad(grv, tg)).astype(BF16)
        dz_ref[:, D:2 * D] = dgr
        dwin_ref[:, D:2 * D] += _dot_tn(h1v, dgr)

        b_scr[...] = _shift_up(a, 1, a_first[...])
        d_scr[...] = dyap_v * gg
        row = _row_iota(D)

        def grp(jj, carry):
            r0 = pl.multiple_of((ngrp - 1 - jj) * SUBLANES, SUBLANES)
            bv = b_scr[pl.ds(r0, SUBLANES), :]
            dv = d_scr[pl.ds(r0, SUBLANES), :]
            for d in (1, 2, 4):
                m = row < SUBLANES - d
                dv = jnp.where(m, dv + bv * pltpu.roll(dv, SUBLANES - d, 0), dv)
                bv = jnp.where(m, bv * pltpu.roll(bv, SUBLANES - d, 0), bv)
            gv = dv + bv * carry
            g_scr[pl.ds(r0, SUBLANES), :] = gv
            return gv[0:1, :]

        lax.fori_loop(0, ngrp, grp, g_first[0:1, :])
        g = g_scr[...]
        a_first[...] = a[:SUBLANES]
        g_first[...] = g[:SUBLANES]

        da = g * h_prev
        gx = g * xc
        dmult = gx * ia
        dia = gx * mult
        dxc = g * (mult * ia)
        dla = da * a - dmult * (a * a) / mult
        dra = dla * (LRU_C * ls)
        dlam_ref[...] += _colsum(dla * ra) * (LRU_C * _sigmoid(-lam_v))
        dpa = dra * ra * (1.0 - ra)
        dpx = dia * ia * (1.0 - ia)
        dba_ref[...] += _colsum(dpa)
        dbx_ref[...] += _colsum(dpx)
        dpab = dpa.astype(BF16)
        dpxb = dpx.astype(BF16)
        xcb = xc.astype(BF16)
        for hd in range(NH):
            sl = slice(hd * HD, (hd + 1) * HD)
            dwa_ref[hd] += _dot_tn(xcb[:, sl], dpab[:, sl])
            dwx_ref[hd] += _dot_tn(xcb[:, sl], dpxb[:, sl])
        dxc = dxc + _heads_nt(dpab, wa_ref) + _heads_nt(dpxb, wx_ref)

        nxt = dxc_first[...]
        dxr = (cw_ref[3:4, :] * dxc + cw_ref[2:3, :] * _shift_up(dxc, 1, nxt)
               + cw_ref[1:2, :] * _shift_up(dxc, 2, nxt) + cw_ref[0:1, :] * _shift_up(dxc, 3, nxt))
        dxrb = dxr.astype(BF16)
        dz_ref[:, 0:D] = dxrb
        dwin_ref[:, 0:D] += _dot_tn(h1v, dxrb)
        dxc_first[...] = dxc[:SUBLANES]
        dcb_ref[...] += _colsum(dxc)
        for k in range(4):
            dcw_ref[k:k + 1, :] += _colsum(dxc * _shift_down(xr, 3 - k, prev8))

    def rev(col):
        return lambda i: (nt - 1 - i, col)

    def rev_halo(col):
        return lambda i: (jnp.maximum((nt - 1 - i) * hpt - 1, 0), col)

    vec = _const_spec((1, D))
    wspec = _const_spec((NH, HD, HD))
    vshape = jax.ShapeDtypeStruct((1, D), F32)
    wshape = jax.ShapeDtypeStruct((NH, HD, HD), F32)
    any_spec = pl.BlockSpec(memory_space=pl.ANY)
    tile = pl.BlockSpec((tm, D), rev(0))
    outs = pl.pallas_call(
        body, name="rnn_bwd", grid=(nt,),
        in_specs=[tile, tile, _const_spec((D, D), True), tile,
                  tile, pl.BlockSpec((HALO, D), rev_halo(0)),
                  pl.BlockSpec((tm, D), rev(1)), tile, pl.BlockSpec((HALO, D), rev_halo(0)),
                  any_spec, any_spec,
                  _const_spec((4, D)), vec, wspec, vec, wspec, vec, vec],
        out_specs=[pl.BlockSpec((tm, 2 * D), rev(0)), _acc_spec((D, 2 * D), (0, 0)), _acc_spec((D, D), (0, 0)),
                   _const_spec((4, D)), vec, wspec, vec, wspec, vec, vec],
        out_shape=[jax.ShapeDtypeStruct((t, NCOL_IN), BF16), jax.ShapeDtypeStruct((D, NCOL_IN), F32),
                   jax.ShapeDtypeStruct((D, D), F32), jax.ShapeDtypeStruct((4, D), F32), vshape,
                   wshape, vshape, wshape, vshape, vshape],
        scratch_shapes=[pltpu.VMEM((SUBLANES, D), F32), pltpu.VMEM((SUBLANES, D), F32), pltpu.VMEM((SUBLANES, D), F32),
                        pltpu.VMEM((tm, D), F32), pltpu.VMEM((tm, D), F32), pltpu.VMEM((tm, D), F32)],
        input_output_aliases={9: 0, 10: 1},
        compiler_params=_cparams(1),
    )(dya, ya_pre, wba, h1, z, z, z, h, h, dz, dwin, cw, cb, wa, ba, wx, bx, lam)
    return outs


def _sgu_bwd(dyb, yb_pre, wbb, h1, z, dz, dwin, lng, lnb, wm, wmt, bst, mask):
    t = z.shape[0]
    tm = _tile_big(t)

    def body(dyb_ref, ybp_ref, wbb_ref, h1_ref, zu_ref, zv_ref, dz_any, dwin_any,
             lng_ref, lnb_ref, wm_ref, wmt_ref, bst_ref, mask_ref,
             dz_ref, dwin_ref, dwbb_ref, dws_ref, dbst_ref, dlng_ref, dlnb_ref):
        del dz_any, dwin_any

        @pl.when(pl.program_id(0) == 0)
        def _():
            for r in (dwin_ref, dwbb_ref, dws_ref, dbst_ref, dlng_ref, dlnb_ref):
                r[...] = jnp.zeros_like(r)

        zu = zu_ref[...].astype(F32)
        zv = zv_ref[...].astype(F32)
        lng_v = lng_ref[...]
        gu, tu, tv, rstd, vhat, vb, mixed = _sgu_core(zu, zv, lng_v, lnb_ref[...], wm_ref, bst_ref)
        dyb_v = dyb_ref[...]
        dwbb_ref[...] += _dot_tn(ybp_ref[...], dyb_v)
        dyb = _dot_nt(dyb_v, wbb_ref[...])
        h1v = h1_ref[...]
        dzu = (dyb * mixed * _gelu_grad(zu, tu)).astype(BF16)
        dz_ref[:, 0:D] = dzu
        dwin_ref[:, 0:D] += _dot_tn(h1v, dzu)
        dmix = dyb * gu
        dmb = dmix.astype(BF16)
        rows = []
        lane = lax.broadcasted_iota(jnp.int32, (HD, NH), 1)
        dbst = jnp.zeros((HD, NH), F32)
        for b0 in range(0, tm, HD):
            cols = []
            for g in range(NH):
                sl = slice(g * HD, (g + 1) * HD)
                dmg = dmb[b0:b0 + HD, sl]
                dws_ref[g] += _dot_nt(dmg, vb[b0:b0 + HD, sl]) * mask_ref[...]
                cols.append(jnp.dot(wmt_ref[g], dmg, preferred_element_type=F32))
                dbst = dbst + jnp.where(lane == g, jnp.sum(dmix[b0:b0 + HD, sl], axis=1, keepdims=True), 0.0)
            rows.append(jnp.concatenate(cols, axis=1))
        dbst_ref[...] += dbst
        dvln = jnp.concatenate(rows, axis=0) if len(rows) > 1 else rows[0]
        dlng_ref[...] += _colsum(dvln * vhat)
        dlnb_ref[...] += _colsum(dvln)
        dvh = dvln * lng_v
        dgv = rstd * (dvh - jnp.mean(dvh, axis=-1, keepdims=True)
                      - vhat * jnp.mean(dvh * vhat, axis=-1, keepdims=True))
        dzv = (dgv * _gelu_grad(zv, tv)).astype(BF16)
        dz_ref[:, D:2 * D] = dzv
        dwin_ref[:, D:2 * D] += _dot_tn(h1v, dzv)

    vec = _const_spec((1, D))
    wspec = _const_spec((NH, HD, HD))
    vshape = jax.ShapeDtypeStruct((1, D), F32)
    tile = pl.BlockSpec((tm, D), lambda i: (i, 0))
    any_spec = pl.BlockSpec(memory_space=pl.ANY)
    return pl.pallas_call(
        body, name="sgu_bwd", grid=(t // tm,),
        in_specs=[tile, tile, _const_spec((D, D), True), tile,
                  pl.BlockSpec((tm, D), lambda i: (i, 2)), pl.BlockSpec((tm, D), lambda i: (i, 3)), any_spec, any_spec,
                  vec, vec, wspec, wspec, _const_spec((HD, NH)), _const_spec((HD, HD))],
        out_specs=[pl.BlockSpec((tm, 2 * D), lambda i: (i, 1)), _acc_spec((D, 2 * D), (0, 1)), _acc_spec((D, D), (0, 0)),
                   wspec, _const_spec((HD, NH)), vec, vec],
        out_shape=[jax.ShapeDtypeStruct((t, NCOL_IN), BF16), jax.ShapeDtypeStruct((D, NCOL_IN), F32),
                   jax.ShapeDtypeStruct((D, D), F32), jax.ShapeDtypeStruct((NH, HD, HD), F32),
                   jax.ShapeDtypeStruct((HD, NH), F32), vshape, vshape],
        input_output_aliases={6: 0, 7: 1},
        compiler_params=_cparams(1),
    )(dyb, yb_pre, wbb, h1, z, z, dz, dwin, lng, lnb, wm, wmt, bst, mask)


def _in_bwd(dz, win, x, dx2, g, scale1):
    t = x.shape[0]
    tm = _tile_big(t)

    def body(dz_ref, w_ref, x_ref, dx2_ref, g_ref, sc_ref, dx_ref, dsh_ref, dsc_ref, dg_ref):
        @pl.when(pl.program_id(0) == 0)
        def _():
            for r in (dsh_ref, dsc_ref, dg_ref):
                r[...] = jnp.zeros_like(r)

        dh = jnp.zeros((tm, D), F32)
        for c0 in range(0, NCOL_IN, D):
            dh = dh + _dot_nt(dz_ref[:, c0:c0 + D], w_ref[:, c0:c0 + D])
        dxn, dsh, dsc, dg = _modnorm_bwd(dh, x_ref[...], g_ref[...], sc_ref[...])
        dx_ref[...] = dx2_ref[...] + dxn
        dsh_ref[...] += dsh
        dsc_ref[...] += dsc
        dg_ref[...] += dg

    tile = pl.BlockSpec((tm, D), lambda i: (i, 0))
    vec = _const_spec((1, D))
    vshape = jax.ShapeDtypeStruct((1, D), F32)
    return pl.pallas_call(
        body, name="in_bwd", grid=(t // tm,),
        in_specs=[pl.BlockSpec((tm, NCOL_IN), lambda i: (i, 0)), _const_spec((D, NCOL_IN), True), tile, tile, vec, vec],
        out_specs=[tile, vec, vec, vec],
        out_shape=[jax.ShapeDtypeStruct((t, D), F32), vshape, vshape, vshape],
        compiler_params=_cparams(1),
    )(dz, win, x, dx2, g, scale1)


def _mod_cols(c_all, w_ada, b_cols):
    nb, cols = c_all.shape[0], w_ada.shape[1]

    def body(c_ref, w_ref, b_ref, o_ref):
        cv = c_ref[...]
        ca = (cv * _sigmoid(cv)).astype(BF16)
        o_ref[...] = jnp.dot(ca, w_ref[...].astype(BF16), preferred_element_type=F32) + b_ref[...]

    return pl.pallas_call(body, name="mod_cols", out_shape=jax.ShapeDtypeStruct((nb, cols), F32))(c_all, w_ada, b_cols)


def _ada_grad(c_all, dmod_cols):
    cols = dmod_cols.shape[1]

    def body(c_ref, d_ref, o_ref):
        cv = c_ref[...]
        ca = (cv * _sigmoid(cv)).astype(BF16)
        o_ref[...] = _dot_tn(ca, d_ref[...].astype(BF16))

    return pl.pallas_call(body, name="ada_grad", out_shape=jax.ShapeDtypeStruct((D, cols), F32))(c_all, dmod_cols)


def _adamw(w, m, v, parts, name):
    rows, cols = w.shape
    tr = _row_tile(rows)
    stacked = [p.ndim == 3 for p in parts]
    bc1 = 1.0 - ADAM_B1 ** ADAM_STEP
    bc2 = 1.0 - ADAM_B2 ** ADAM_STEP

    def body(*refs):
        w_ref, m_ref, v_ref = refs[:3]
        p_refs = refs[3:3 + len(parts)]
        g_ref, d_ref, mo_ref, vo_ref = refs[3 + len(parts):]
        g = None
        for p_ref, st in zip(p_refs, stacked):
            terms = [p_ref[k].astype(F32) for k in range(p_ref.shape[0])] if st else [p_ref[...].astype(F32)]
            for term in terms:
                g = term if g is None else g + term
        mn = ADAM_B1 * m_ref[...] + (1.0 - ADAM_B1) * g
        vn = ADAM_B2 * v_ref[...] + (1.0 - ADAM_B2) * (g * g)
        g_ref[...] = g
        mo_ref[...] = mn
        vo_ref[...] = vn
        d_ref[...] = -ADAM_LR * ((mn / bc1) / (jnp.sqrt(vn / bc2) + ADAM_EPS) + ADAM_WD * w_ref[...])

    tile = pl.BlockSpec((tr, cols), lambda i: (i, 0))
    p_specs = [pl.BlockSpec((p.shape[0], tr, cols), lambda i: (0, i, 0)) if st else tile for p, st in zip(parts, stacked)]
    shp = jax.ShapeDtypeStruct((rows, cols), F32)
    return pl.pallas_call(
        body, name=name, grid=(rows // tr,),
        in_specs=[tile, tile, tile] + p_specs, out_specs=[tile] * 4, out_shape=[shp] * 4,
        compiler_params=_cparams(1),
    )(w, m, v, *parts)


def _mesh_pos():
    return lax.axis_index("x"), lax.axis_index("y"), lax.axis_index("c")


def _other_chips(x, y):
    return [(1 - x, y), (x, 1 - y), (1 - x, 1 - y)]


def _block_of(ref, axis, index, size):
    if axis == 0:
        return ref.at[index]
    return ref.at[:, pl.ds(pl.multiple_of(index * size, 128), size)]


def _all_gather(shards, axes, name):
    n = len(shards)
    per = 7

    def body(*refs):
        ins, outs = refs[:n], refs[n:2 * n]
        send_sems, recv_sems, local_sems = refs[2 * n:]
        x, y, c = _mesh_pos()
        me, sibling = (x, y, c), (x, y, 1 - c)
        chips = _other_chips(x, y)

        def rows(a, pos):
            return _block_of(outs[a], axes[a], 4 * pos[0] + 2 * pos[1] + pos[2], shards[a].shape[-1])

        def copy(a, k, block, to, src=None):
            return pltpu.make_async_remote_copy(
                src_ref=rows(a, block) if src is None else src, dst_ref=rows(a, block),
                send_sem=send_sems.at[a * per + k], recv_sem=recv_sems.at[a * per + k],
                device_id=to, device_id_type=MESH_IDS)

        mine = [pltpu.make_async_copy(ins[a], rows(a, me), local_sems.at[a]) for a in range(n)]
        for cp in mine:
            cp.start()
        first = []
        for a in range(n):
            first.append(copy(a, 0, me, sibling, src=ins[a]))
            first += [copy(a, 1 + j, me, (*chip, c), src=ins[a]) for j, chip in enumerate(chips)]
        for cp in first:
            cp.start()
        passed = []
        for j, chip in enumerate(chips):
            for a in range(n):
                copy(a, 1 + j, (*chip, c), me).wait_recv()
                fwd = copy(a, 4 + j, (*chip, c), sibling)
                fwd.start()
                passed.append(fwd)
        for a in range(n):
            copy(a, 0, sibling, me).wait_recv()
            for j, chip in enumerate(chips):
                copy(a, 4 + j, (*chip, 1 - c), me).wait_recv()
        for cp in first + passed:
            cp.wait_send()
        for cp in mine:
            cp.wait()

    def full_shape(s, ax):
        return (N_DEV,) + s.shape if ax == 0 else s.shape[:-1] + (N_DEV * s.shape[-1],)

    any_spec = pl.BlockSpec(memory_space=pl.ANY)
    return pl.pallas_call(
        body, name=name,
        in_specs=[any_spec] * n, out_specs=[any_spec] * n,
        out_shape=[jax.ShapeDtypeStruct(full_shape(s, ax), s.dtype) for s, ax in zip(shards, axes)],
        scratch_shapes=[pltpu.SemaphoreType.DMA((n * per,)), pltpu.SemaphoreType.DMA((n * per,)),
                        pltpu.SemaphoreType.DMA((n,))],
    )(*shards)


def _chip_blocks(x, y):
    return [(x, y)] + _other_chips(x, y)


def _sibling_reduce(g, axis, name):
    rows, cols = (g.shape[1], g.shape[2]) if axis == 0 else (g.shape[0], g.shape[1] // N_DEV)
    chunk = math.gcd(rows, 64)

    def body(g_ref, own_ref, pay_ref, send_buf, keep_buf, recv_buf, send_sems, recv_sems, stage_sems, keep_sems):
        x, y, c = _mesh_pos()
        sibling = (x, y, 1 - c)
        chips = _chip_blocks(x, y)
        stage, keep, push = [], [], []
        for j, (px, py) in enumerate(chips):
            theirs = _block_of(g_ref, axis, 4 * px + 2 * py + (1 - c), cols)
            ours = _block_of(g_ref, axis, 4 * px + 2 * py + c, cols)
            stage.append(pltpu.make_async_copy(theirs, send_buf.at[j], stage_sems.at[j]))
            keep.append(pltpu.make_async_copy(ours, keep_buf.at[j], keep_sems.at[j]))
            push.append(pltpu.make_async_remote_copy(
                src_ref=send_buf.at[j], dst_ref=recv_buf.at[j], send_sem=send_sems.at[j], recv_sem=recv_sems.at[j],
                device_id=sibling, device_id_type=MESH_IDS))
        for j in range(4):
            stage[j].start()
        for j in range(4):
            keep[j].start()
        for j in range(4):
            stage[j].wait()
            push[j].start()
        for j in range(4):
            push[j].wait_recv()
            keep[j].wait()
            dst = own_ref if j == 0 else pay_ref.at[j - 1]

            def add(r, carry, j=j, dst=dst):
                sl = pl.ds(pl.multiple_of(r * chunk, chunk), chunk)
                dst[sl, :] = (keep_buf[j, sl, :] + recv_buf[j, sl, :]).astype(dst.dtype)
                return carry

            lax.fori_loop(0, rows // chunk, add, 0)
        for j in range(4):
            push[j].wait_send()

    vmem = pl.BlockSpec(memory_space=pltpu.VMEM)
    buf = pltpu.VMEM((4, rows, cols), F32)
    return pl.pallas_call(
        body, name=name,
        in_specs=[pl.BlockSpec(memory_space=pl.ANY)], out_specs=[vmem, vmem],
        out_shape=[jax.ShapeDtypeStruct((rows, cols), F32), jax.ShapeDtypeStruct((3, rows, cols), BF16)],
        scratch_shapes=[buf, buf, buf, pltpu.SemaphoreType.DMA((4,)), pltpu.SemaphoreType.DMA((4,)),
                        pltpu.SemaphoreType.DMA((4,)), pltpu.SemaphoreType.DMA((4,))],
        compiler_params=pltpu.CompilerParams(vmem_limit_bytes=VMEM_LIMIT),
    )(g)


def _chip_exchange(payloads, name):
    n = len(payloads)

    def body(*refs):
        ins, outs = refs[:n], refs[n:2 * n]
        send_sems, recv_sems = refs[2 * n:]
        x, y, c = _mesh_pos()
        copies = []
        for a in range(n):
            for j, chip in enumerate(_other_chips(x, y)):
                copies.append(pltpu.make_async_remote_copy(
                    src_ref=ins[a].at[j], dst_ref=outs[a].at[j], send_sem=send_sems.at[3 * a + j],
                    recv_sem=recv_sems.at[3 * a + j], device_id=(*chip, c), device_id_type=MESH_IDS))
        for cp in copies:
            cp.start()
        for cp in copies:
            cp.wait_recv()
        for cp in copies:
            cp.wait_send()

    any_spec = pl.BlockSpec(memory_space=pl.ANY)
    return pl.pallas_call(
        body, name=name,
        in_specs=[any_spec] * n, out_specs=[any_spec] * n,
        out_shape=[jax.ShapeDtypeStruct(p.shape, p.dtype) for p in payloads],
        scratch_shapes=[pltpu.SemaphoreType.DMA((3 * n,)), pltpu.SemaphoreType.DMA((3 * n,))],
    )(*payloads)


def _local_step(x, target, mod, win, wup, wba, wbb, wout, wd, p):
    shift1, scale1, gate1, shift2, scale2, gate2 = (mod[k] for k in range(6))
    wa, wx = p["lru_w_a"].astype(BF16), p["lru_w_x"].astype(BF16)
    mask = jnp.tril(jnp.ones((HD, HD), F32))
    wm = (p["sgu_w_s"] * mask).astype(BF16)
    wmt = jnp.swapaxes(wm, 1, 2)
    bst = jnp.transpose(p["sgu_b_s"])

    h1, z = _norm_proj(x, p["norm_mix_g"], scale1, shift1, win, "mix_proj")
    hstate, ya_pre = _rnn_fwd(z, p["rnn_conv_w"], p["rnn_conv_b"], wa, p["lru_b_a"], wx, p["lru_b_x"], p["lru_lambda"])
    yb_pre = _sgu_fwd(z, p["sgu_ln_g"], p["sgu_ln_b"], wm, bst)
    x2, ya, yb, merged, o1 = _merge_fwd(ya_pre, yb_pre, z, x, gate1, wba, wbb, wout)
    h2, up, ff = _ffn_proj_mid(x2, p["norm_ffn_g"], scale2, shift2, wup, p["ffn_conv_w"], p["ffn_conv_b"])
    dx3, loss, d_gfin, d_gate2 = _ffn_out_loss(ff, wd, x2, target, gate2, p["norm_final_g"])

    dact, dval, d_wd, dcw_a, dcw_v, dcb_a, dcb_v = _ffn_down_bwd(dx3, gate2, ff, up, p["ffn_conv_w"], p["ffn_conv_b"], wd)
    dup, dx2, do1, d_shift2, d_scale2, d_gffn, d_gate1 = _ffn_up_bwd(
        dact, dval, p["ffn_conv_w"], wup, x2, dx3, p["norm_ffn_g"], scale2, o1, gate1)
    d_wup = _xt_y(h2, dup, "w_up_grad")

    dya, dyb, dz, d_wout, d_win = _out_bwd(do1, wout, merged, ya, yb, z, h1)
    dz, d_win, d_wba, d_cw, d_cb, d_wa, d_ba, d_wx, d_bx, d_lam = _rnn_bwd(
        dya, ya_pre, wba, h1, z, hstate, dz, d_win, p["rnn_conv_w"], p["rnn_conv_b"], wa, p["lru_b_a"], wx,
        p["lru_b_x"], p["lru_lambda"])
    dz, d_win, d_wbb, d_ws, d_bst, d_lng, d_lnb = _sgu_bwd(
        dyb, yb_pre, wbb, h1, z, dz, d_win, p["sgu_ln_g"], p["sgu_ln_b"], wm, wmt, bst, mask)
    grad_x, d_shift1, d_scale1, d_gmix = _in_bwd(dz, win, x, dx2, p["norm_mix_g"], scale1)

    small = {
        "norm_mix_g": d_gmix, "rnn_conv_w": d_cw, "rnn_conv_b": d_cb, "lru_w_a": d_wa, "lru_b_a": d_ba,
        "lru_w_x": d_wx, "lru_b_x": d_bx, "lru_lambda": d_lam, "sgu_ln_g": d_lng, "sgu_ln_b": d_lnb,
        "sgu_w_s": d_ws, "sgu_b_s": jnp.transpose(d_bst), "norm_ffn_g": d_gffn,
        "ffn_conv_w": jnp.concatenate([dcw_a, dcw_v], axis=1), "ffn_conv_b": jnp.concatenate([dcb_a, dcb_v], axis=1),
        "norm_final_g": d_gfin,
    }
    dmod = jnp.stack([d_shift1, d_scale1, d_gate1, d_shift2, d_scale2, d_gate2])
    big = {"w_in": d_win, "w_up": d_wup, "w_branch_a": d_wba, "w_branch_b": d_wbb, "w_out": d_wout, "w_down": d_wd}
    return loss, grad_x, big, small, dmod


REPLICATED = ["b_ada", "norm_mix_g", "rnn_conv_b", "lru_w_a", "lru_b_a", "lru_w_x", "lru_b_x", "lru_lambda",
              "sgu_ln_g", "sgu_ln_b", "sgu_w_s", "sgu_b_s", "norm_ffn_g", "ffn_conv_b", "norm_final_g"]
COL_SHARDED = ["rnn_conv_w", "ffn_conv_w"]
SMALL_NAMES = REPLICATED + COL_SHARDED
BIG_NAMES = ["w_in", "w_up", "w_branch_a", "w_branch_b", "w_out", "w_down"]
BIG_AXES = [1, 1, 0, 0, 0, 0]
WEIGHTS = ["w_ada", "b_ada", "norm_mix_g", "w_in", "rnn_conv_w", "rnn_conv_b", "lru_w_a", "lru_b_a", "lru_w_x",
           "lru_b_x", "lru_lambda", "sgu_ln_g", "sgu_ln_b", "sgu_w_s", "sgu_b_s", "w_branch_a", "w_branch_b",
           "w_out", "norm_ffn_g", "w_up", "ffn_conv_w", "ffn_conv_b", "w_down", "norm_final_g"]
LANES = 128


def _pack_rows(shape):
    return math.prod(shape) // LANES


def _pack(arrays):
    return jnp.concatenate([a.reshape(-1, LANES) for a in arrays], axis=0)


def _unpack(packed, shapes):
    out, r0 = [], 0
    for s in shapes:
        nrow = math.prod(s) // LANES
        out.append(packed[r0:r0 + nrow].reshape(s))
        r0 += nrow
    return out


def kernel(x, c, w_ada, b_ada, norm_mix_g, w_in, rnn_conv_w, rnn_conv_b, lru_w_a, lru_b_a, lru_w_x, lru_b_x, lru_lambda, sgu_ln_g, sgu_ln_b, sgu_w_s, sgu_b_s, w_branch_a, w_branch_b, w_out, norm_ffn_g, w_up, ffn_conv_w, ffn_conv_b, w_down, norm_final_g, loss_target, m_w_ada, m_b_ada, m_norm_mix_g, m_w_in, m_rnn_conv_w, m_rnn_conv_b, m_lru_w_a, m_lru_b_a, m_lru_w_x, m_lru_b_x, m_lru_lambda, m_sgu_ln_g, m_sgu_ln_b, m_sgu_w_s, m_sgu_b_s, m_w_branch_a, m_w_branch_b, m_w_out, m_norm_ffn_g, m_w_up, m_ffn_conv_w, m_ffn_conv_b, m_w_down, m_norm_final_g, v_w_ada, v_b_ada, v_norm_mix_g, v_w_in, v_rnn_conv_w, v_rnn_conv_b, v_lru_w_a, v_lru_b_a, v_lru_w_x, v_lru_b_x, v_lru_lambda, v_sgu_ln_g, v_sgu_ln_b, v_sgu_w_s, v_sgu_b_s, v_w_branch_a, v_w_branch_b, v_w_out, v_norm_ffn_g, v_w_up, v_ffn_conv_w, v_ffn_conv_b, v_w_down, v_norm_final_g):
    given = dict(locals())
    me = 4 * lax.axis_index("x") + 2 * lax.axis_index("y") + lax.axis_index("c")
    ada_cols = w_ada.shape[2]
    conv_cols = {"rnn_conv_w": rnn_conv_w.shape[2], "ffn_conv_w": ffn_conv_w.shape[2]}

    shards = [w_in[0].astype(BF16), w_up[0].astype(BF16), w_branch_a[0].astype(BF16), w_branch_b[0].astype(BF16),
              w_out[0].astype(BF16), w_down[0].astype(BF16), c.reshape(1, 1, D), rnn_conv_w[0], ffn_conv_w[0]]
    win, wup, wba, wbb, wout, wd, c_all, cw_rnn, cw_ffn = _all_gather(
        shards, BIG_AXES + [0, 1, 1], "gather_weights")
    wba, wbb, wout = (w.reshape(D, D) for w in (wba, wbb, wout))
    wd = wd.reshape(DFF, D)
    c_all = c_all.reshape(N_DEV, D)

    b_cols = lax.dynamic_slice_in_dim(b_ada, me * ada_cols, ada_cols, axis=1)
    (mod_all,) = _all_gather([_mod_cols(c_all, w_ada[0], b_cols).reshape(1, N_DEV, ada_cols)], [0], "gather_mod")
    mod_all = mod_all.reshape(N_DEV, N_DEV, ada_cols)
    mod_mine = lax.dynamic_index_in_dim(mod_all, me, axis=1, keepdims=False).reshape(6, 1, D)

    p = {n: given[n][0] for n in REPLICATED if n not in ("b_ada", "norm_final_g")}
    p = {n: (a.reshape(1, -1) if a.ndim == 1 else a) for n, a in p.items()}
    p["rnn_conv_w"], p["ffn_conv_w"] = cw_rnn, cw_ffn
    p["norm_final_g"] = norm_final_g.reshape(1, D)
    loss, grad_x, big, small, dmod = _local_step(x[0], loss_target[0], mod_mine, win, wup, wba, wbb, wout, wd, p)

    small["b_ada"] = dmod.reshape(1, 6 * D)
    rows_of = {n: _pack_rows(small[n].shape) for n in SMALL_NAMES}
    start_of = {n: sum(rows_of[q] for q in SMALL_NAMES[:k]) for k, n in enumerate(SMALL_NAMES)}
    pack = _pack([small[n] for n in SMALL_NAMES])
    (packs,) = _all_gather([pack[None]], [0], "gather_small")
    packs = packs.reshape(N_DEV, pack.shape[0], LANES)

    grads = [big[n] if ax == 1 else big[n].reshape(N_DEV, big[n].shape[0] // N_DEV, big[n].shape[1])
             for n, ax in zip(BIG_NAMES, BIG_AXES)]
    sums = [_sibling_reduce(g, ax, "reduce_sibling_" + n) for n, g, ax in zip(BIG_NAMES, grads, BIG_AXES)]
    landed = _chip_exchange([pay for _, pay in sums], "reduce_chips")
    out = {}
    for n, (own, _), got in zip(BIG_NAMES, sums, landed):
        out[n] = _adamw(given[n][0], given["m_" + n][0], given["v_" + n][0], [own, got], "adamw_" + n)

    dmod_all = packs[:, :rows_of["b_ada"]].reshape(N_DEV, 6 * D)
    dmod_cols = lax.dynamic_slice_in_dim(dmod_all, me * ada_cols, ada_cols, axis=1)
    out["w_ada"] = _adamw(w_ada[0], m_w_ada[0], v_w_ada[0], [_ada_grad(c_all, dmod_cols)], "adamw_w_ada")

    rep_rows = sum(rows_of[n] for n in REPLICATED)
    res = _adamw(*[_pack([given[pre + n] for n in REPLICATED]) for pre in ("", "m_", "v_")],
                 [packs[:, :rep_rows]], "adamw_small")
    unpacked = [_unpack(r, [given[n].shape for n in REPLICATED]) for r in res]
    for k, n in enumerate(REPLICATED):
        out[n] = tuple(u[k] for u in unpacked)

    for n in COL_SHARDED:
        full = packs[:, start_of[n]:start_of[n] + rows_of[n]].reshape(N_DEV, small[n].shape[0], small[n].shape[1])
        mine = lax.dynamic_slice_in_dim(full, me * conv_cols[n], conv_cols[n], axis=2)
        out[n] = _adamw(given[n][0], given["m_" + n][0], given["v_" + n][0], [mine], "adamw_" + n)

    total = lax.psum(loss[0, 0], ("x", "y", "c"))
    results = [total, grad_x[None]]
    for kind in range(4):
        results += [out[n][kind].reshape(given[n].shape) for n in WEIGHTS]
    return tuple(results)
```

```python
import math

import jax
import jax.numpy as jnp
from jax import lax
from jax.experimental import pallas as pl
from jax.experimental.pallas import tpu as pltpu

F32 = jnp.float32
BF16 = jnp.bfloat16
MESH_IDS = pl.DeviceIdType.MESH

D = 1024
NH = 8
HD = 128
NCOL_IN = 6 * D
DFF = 3 * D
N_DEV = 8
EPS = 1e-6
LRU_C = 8.0
ADAM_LR, ADAM_B1, ADAM_B2, ADAM_EPS, ADAM_WD, ADAM_STEP = 0.001, 0.9, 0.999, 1e-08, 0.01, 10

SUBLANES = 8
HALO = 16
VMEM_LIMIT = 56 * 1024 * 1024
GELU_K = math.sqrt(2.0 / math.pi)
GELU_C = 0.044715


def _cparams(n_axes):
    return pltpu.CompilerParams(dimension_semantics=("arbitrary",) * n_axes, vmem_limit_bytes=VMEM_LIMIT)


def _const_spec(shape, single_buffer=False):
    nd = len(shape)
    if single_buffer:
        return pl.BlockSpec(shape, lambda *_: (0,) * nd, pipeline_mode=pl.Buffered(1))
    return pl.BlockSpec(shape, lambda *_: (0,) * nd)


def _tile_big(t):
    return min(512, t)


def _tile_seq(t):
    return min(256, t)


def _row_tile(rows):
    if rows <= 512:
        return rows
    return next(tr for tr in range(512, 0, -SUBLANES) if rows % tr == 0)


def _gelu_t(x):
    t = jnp.tanh(GELU_K * (x + GELU_C * (x * x * x)))
    return 0.5 * x * (1.0 + t), t


def _gelu_grad(x, t):
    return 0.5 * (1.0 + t) + 0.5 * x * (1.0 - t * t) * (GELU_K * (1.0 + 3.0 * GELU_C * x * x))


def _sigmoid(x):
    return 1.0 / (1.0 + jnp.exp(-x))


def _log_sigmoid(x):
    return -(jnp.maximum(-x, 0.0) + jnp.log1p(jnp.exp(-jnp.abs(x))))


def _row_iota(cols):
    return lax.broadcasted_iota(jnp.int32, (SUBLANES, cols), 0)


def _shift_down(x, k, prev8):
    if k == 0:
        return x
    r = pltpu.roll(x, k, 0)
    p = pltpu.roll(prev8, k, 0)
    head = jnp.where(_row_iota(x.shape[1]) < k, p, r[:SUBLANES])
    return jnp.concatenate([head, r[SUBLANES:]], axis=0)


def _shift_up(x, k, next8):
    if k == 0:
        return x
    n = x.shape[0]
    r = pltpu.roll(x, n - k, 0)
    q = pltpu.roll(next8, SUBLANES - k, 0)
    tail = jnp.where(_row_iota(x.shape[1]) >= SUBLANES - k, q, r[n - SUBLANES:])
    return jnp.concatenate([r[:n - SUBLANES], tail], axis=0)


def _heads_nn(x_bf, w_ref):
    return jnp.concatenate(
        [jnp.dot(x_bf[:, h * HD:(h + 1) * HD], w_ref[h], preferred_element_type=F32) for h in range(NH)], axis=1)


def _heads_nt(x_bf, w_ref):
    return jnp.concatenate(
        [lax.dot_general(x_bf[:, h * HD:(h + 1) * HD], w_ref[h], (((1,), (1,)), ((), ())), preferred_element_type=F32)
         for h in range(NH)], axis=1)


def _dot_nt(a, b):
    return lax.dot_general(a, b, (((1,), (1,)), ((), ())), preferred_element_type=F32)


def _dot_tn(a, b):
    return lax.dot_general(a, b, (((0,), (0,)), ((), ())), preferred_element_type=F32)


def _colsum(x):
    return jnp.sum(x, axis=0, keepdims=True)


def _prev_halo_map(tm, col):
    return lambda i, *_: (jnp.maximum(i * (tm // HALO) - 1, 0), col)


def _norm_proj(x, g, scale, shift, w, name):
    t, n = x.shape[0], w.shape[1]
    tm = _tile_big(t)

    def body(x_ref, g_ref, sc_ref, sh_ref, w_ref, h_ref, z_ref):
        xv = x_ref[...]
        r = lax.rsqrt(jnp.mean(xv * xv, axis=-1, keepdims=True) + EPS)
        hb = ((xv * r * g_ref[...]) * (1.0 + sc_ref[...]) + sh_ref[...]).astype(BF16)
        h_ref[...] = hb
        for c0 in range(0, n, D):
            z_ref[:, c0:c0 + D] = jnp.dot(hb, w_ref[:, c0:c0 + D], preferred_element_type=F32).astype(BF16)

    vec = _const_spec((1, D))
    return pl.pallas_call(
        body, name=name, grid=(t // tm,),
        in_specs=[pl.BlockSpec((tm, D), lambda i: (i, 0)), vec, vec, vec, _const_spec((D, n), True)],
        out_specs=[pl.BlockSpec((tm, D), lambda i: (i, 0)), pl.BlockSpec((tm, n), lambda i: (i, 0))],
        out_shape=[jax.ShapeDtypeStruct((t, D), BF16), jax.ShapeDtypeStruct((t, n), BF16)],
        compiler_params=_cparams(1),
    )(x, g, scale, shift, w)


def _lru_gates(xc, wa_ref, ba, wx_ref, bx, ls):
    xb = xc.astype(BF16)
    ra = _sigmoid(_heads_nn(xb, wa_ref) + ba)
    ia = _sigmoid(_heads_nn(xb, wx_ref) + bx)
    la = LRU_C * ra * ls
    a = jnp.exp(la)
    mult = jnp.sqrt(-jnp.tanh(la) * (1.0 + a * a))
    return ra, ia, a, mult


def _conv4(xr, prev8, cw_ref, cb):
    return (cb + cw_ref[3:4, :] * xr + cw_ref[2:3, :] * _shift_down(xr, 1, prev8)
            + cw_ref[1:2, :] * _shift_down(xr, 2, prev8) + cw_ref[0:1, :] * _shift_down(xr, 3, prev8))


def _rnn_fwd(z, cw, cb, wa, ba, wx, bx, lam):
    t = z.shape[0]
    tm = _tile_seq(t)
    ngrp = tm // SUBLANES

    def body(xr_ref, xp_ref, gr_ref, cw_ref, cb_ref, wa_ref, ba_ref, wx_ref, bx_ref, lam_ref,
             h_ref, ya_ref, carry_ref, a_scr, u_scr):
        i = pl.program_id(0)

        @pl.when(i == 0)
        def _():
            carry_ref[...] = jnp.zeros_like(carry_ref)

        xr = xr_ref[...].astype(F32)
        prev8 = jnp.where(i == 0, 0.0, xp_ref[...].astype(F32)[HALO - SUBLANES:])
        xc = _conv4(xr, prev8, cw_ref, cb_ref[...])
        _, ia, a, mult = _lru_gates(xc, wa_ref, ba_ref[...], wx_ref, bx_ref[...], _log_sigmoid(lam_ref[...]))
        a_scr[...] = a
        u_scr[...] = mult * (ia * xc)
        row = _row_iota(D)

        def grp(j, carry):
            r0 = pl.multiple_of(j * SUBLANES, SUBLANES)
            av = a_scr[pl.ds(r0, SUBLANES), :]
            uv = u_scr[pl.ds(r0, SUBLANES), :]
            for d in (1, 2, 4):
                m = row >= d
                uv = jnp.where(m, av * pltpu.roll(uv, d, 0) + uv, uv)
                av = jnp.where(m, av * pltpu.roll(av, d, 0), av)
            hv = uv + av * carry
            h_ref[pl.ds(r0, SUBLANES), :] = hv
            return hv[SUBLANES - 1:SUBLANES, :]

        carry_ref[0:1, :] = lax.fori_loop(0, ngrp, grp, carry_ref[0:1, :])
        gg, _ = _gelu_t(gr_ref[...].astype(F32))
        ya_ref[...] = (h_ref[...] * gg).astype(BF16)

    vec = _const_spec((1, D))
    wspec = _const_spec((NH, HD, HD))
    return pl.pallas_call(
        body, name="rnn_fwd", grid=(t // tm,),
        in_specs=[pl.BlockSpec((tm, D), lambda i: (i, 0)), pl.BlockSpec((HALO, D), _prev_halo_map(tm, 0)),
                  pl.BlockSpec((tm, D), lambda i: (i, 1)), _const_spec((4, D)), vec, wspec, vec, wspec, vec, vec],
        out_specs=[pl.BlockSpec((tm, D), lambda i: (i, 0)), pl.BlockSpec((tm, D), lambda i: (i, 0))],
        out_shape=[jax.ShapeDtypeStruct((t, D), F32), jax.ShapeDtypeStruct((t, D), BF16)],
        scratch_shapes=[pltpu.VMEM((SUBLANES, D), F32), pltpu.VMEM((tm, D), F32), pltpu.VMEM((tm, D), F32)],
        compiler_params=_cparams(1),
    )(z, z, z, cw, cb, wa, ba, wx, bx, lam)


def _sgu_core(zu, zv, lng, lnb, wm_ref, bst_ref):
    gu, tu = _gelu_t(zu)
    gv, tv = _gelu_t(zv)
    mu = jnp.mean(gv, axis=-1, keepdims=True)
    cen = gv - mu
    rstd = lax.rsqrt(jnp.mean(cen * cen, axis=-1, keepdims=True) + EPS)
    vhat = cen * rstd
    vln = vhat * lng + lnb
    vb = vln.astype(BF16)
    rows = []
    for b0 in range(0, zu.shape[0], HD):
        rows.append(jnp.concatenate(
            [jnp.dot(wm_ref[g], vb[b0:b0 + HD, g * HD:(g + 1) * HD], preferred_element_type=F32)
             + bst_ref[:, g:g + 1] for g in range(NH)], axis=1))
    mixed = jnp.concatenate(rows, axis=0) if len(rows) > 1 else rows[0]
    return gu, tu, tv, rstd, vhat, vb, mixed


def _sgu_fwd(z, lng, lnb, wm, bst):
    t = z.shape[0]
    tm = _tile_seq(t)

    def body(zu_ref, zv_ref, lng_ref, lnb_ref, wm_ref, bst_ref, yb_ref):
        gu, _, _, _, _, _, mixed = _sgu_core(zu_ref[...].astype(F32), zv_ref[...].astype(F32),
                                             lng_ref[...], lnb_ref[...], wm_ref, bst_ref)
        yb_ref[...] = (gu * mixed).astype(BF16)

    vec = _const_spec((1, D))
    return pl.pallas_call(
        body, name="sgu_fwd", grid=(t // tm,),
        in_specs=[pl.BlockSpec((tm, D), lambda i: (i, 2)), pl.BlockSpec((tm, D), lambda i: (i, 3)), vec, vec,
                  _const_spec((NH, HD, HD)), _const_spec((HD, NH))],
        out_specs=pl.BlockSpec((tm, D), lambda i: (i, 0)),
        out_shape=jax.ShapeDtypeStruct((t, D), BF16),
        compiler_params=_cparams(1),
    )(z, z, lng, lnb, wm, bst)


def _merge_fwd(ya_pre, yb_pre, z, x, gate1, wba, wbb, wout):
    t = x.shape[0]
    tm = _tile_big(t)

    def body(yap_ref, ybp_ref, ga_ref, gb_ref, x_ref, g1_ref, wba_ref, wbb_ref, wo_ref,
             x2_ref, ya_ref, yb_ref, mg_ref, o1_ref):
        ya = jnp.dot(yap_ref[...], wba_ref[...], preferred_element_type=F32)
        yb = jnp.dot(ybp_ref[...], wbb_ref[...], preferred_element_type=F32)
        merged = _sigmoid(ga_ref[...].astype(F32)) * ya + _sigmoid(gb_ref[...].astype(F32)) * yb
        mb = merged.astype(BF16)
        o1 = jnp.dot(mb, wo_ref[...], preferred_element_type=F32)
        x2_ref[...] = x_ref[...] + g1_ref[...] * o1
        ya_ref[...] = ya.astype(BF16)
        yb_ref[...] = yb.astype(BF16)
        mg_ref[...] = mb
        o1_ref[...] = o1.astype(BF16)

    tile = pl.BlockSpec((tm, D), lambda i: (i, 0))
    wspec = _const_spec((D, D))
    bshape = jax.ShapeDtypeStruct((t, D), BF16)
    return pl.pallas_call(
        body, name="merge_fwd", grid=(t // tm,),
        in_specs=[tile, tile, pl.BlockSpec((tm, D), lambda i: (i, 4)), pl.BlockSpec((tm, D), lambda i: (i, 5)),
                  tile, _const_spec((1, D)), wspec, wspec, wspec],
        out_specs=[tile] * 5,
        out_shape=[jax.ShapeDtypeStruct((t, D), F32), bshape, bshape, bshape, bshape],
        compiler_params=_cparams(1),
    )(ya_pre, yb_pre, z, z, x, gate1, wba, wbb, wout)


def _conv3(u, prev8, cw_ref, cb):
    return cb + cw_ref[2:3, :] * u + cw_ref[1:2, :] * _shift_down(u, 1, prev8) + cw_ref[0:1, :] * _shift_down(u, 2, prev8)


def _ffn_proj_mid(x2, g, scale, shift, w, cw, cb):
    t = x2.shape[0]
    tm = _tile_big(t)
    nc = DFF // D

    def body(x_ref, g_ref, sc_ref, sh_ref, w_ref, cw_ref, cb_ref, h_ref, up_ref, ff_ref, prev_ref):
        @pl.when(pl.program_id(0) == 0)
        def _():
            prev_ref[...] = jnp.zeros_like(prev_ref)

        xv = x_ref[...]
        r = lax.rsqrt(jnp.mean(xv * xv, axis=-1, keepdims=True) + EPS)
        hb = ((xv * r * g_ref[...]) * (1.0 + sc_ref[...]) + sh_ref[...]).astype(BF16)
        h_ref[...] = hb
        for c in range(nc):
            halves = []
            for c0 in (c * D, DFF + c * D):
                ub = jnp.dot(hb, w_ref[:, c0:c0 + D], preferred_element_type=F32).astype(BF16)
                up_ref[:, c0:c0 + D] = ub
                u = ub.astype(F32)
                halves.append(_conv3(u, prev_ref[:, c0:c0 + D], cw_ref[:, c0:c0 + D], cb_ref[:, c0:c0 + D]))
                prev_ref[:, c0:c0 + D] = u[tm - SUBLANES:]
            ga, _ = _gelu_t(halves[0])
            ff_ref[:, c * D:(c + 1) * D] = (ga * halves[1]).astype(BF16)

    vec = _const_spec((1, D))
    n = 2 * DFF
    return pl.pallas_call(
        body, name="ffn_proj_mid", grid=(t // tm,),
        in_specs=[pl.BlockSpec((tm, D), lambda i: (i, 0)), vec, vec, vec, _const_spec((D, n), True),
                  _const_spec((3, n)), _const_spec((1, n))],
        out_specs=[pl.BlockSpec((tm, D), lambda i: (i, 0)), pl.BlockSpec((tm, n), lambda i: (i, 0)),
                   pl.BlockSpec((tm, DFF), lambda i: (i, 0))],
        out_shape=[jax.ShapeDtypeStruct((t, D), BF16), jax.ShapeDtypeStruct((t, n), BF16),
                   jax.ShapeDtypeStruct((t, DFF), BF16)],
        scratch_shapes=[pltpu.VMEM((SUBLANES, n), F32)],
        compiler_params=_cparams(1),
    )(x2, g, scale, shift, w, cw, cb)


def _ffn_out_loss(ff, wd, x2, target, gate2, gfin):
    t = x2.shape[0]
    tm = _tile_big(t)

    def body(ff_ref, wd_ref, x2_ref, tg_ref, g2_ref, gf_ref, dx3_ref, loss_ref, dgf_ref, dg2_ref):
        @pl.when(pl.program_id(0) == 0)
        def _():
            loss_ref[...] = jnp.zeros_like(loss_ref)
            dgf_ref[...] = jnp.zeros_like(dgf_ref)
            dg2_ref[...] = jnp.zeros_like(dg2_ref)

        o2 = jnp.dot(ff_ref[...], wd_ref[...], preferred_element_type=F32)
        x3 = x2_ref[...] + g2_ref[...] * o2
        r = lax.rsqrt(jnp.mean(x3 * x3, axis=-1, keepdims=True) + EPS)
        xhat = x3 * r
        err = xhat * gf_ref[...] - tg_ref[...]
        loss_ref[...] += 0.5 * jnp.sum(jnp.mean(err * err, axis=-1, keepdims=True), axis=0, keepdims=True)
        dy = err * (1.0 / D)
        dgf_ref[...] += _colsum(dy * xhat)
        dxh = dy * gf_ref[...]
        dx3 = r * (dxh - xhat * jnp.mean(dxh * xhat, axis=-1, keepdims=True))
        dx3_ref[...] = dx3
        dg2_ref[...] += _colsum(dx3 * o2)

    tile = pl.BlockSpec((tm, D), lambda i: (i, 0))
    vec = _const_spec((1, D))
    return pl.pallas_call(
        body, name="ffn_out_loss", grid=(t // tm,),
        in_specs=[pl.BlockSpec((tm, DFF), lambda i: (i, 0)), _const_spec((DFF, D), True), tile, tile, vec, vec],
        out_specs=[tile, _const_spec((1, 1)), vec, vec],
        out_shape=[jax.ShapeDtypeStruct((t, D), F32), jax.ShapeDtypeStruct((1, 1), F32),
                   jax.ShapeDtypeStruct((1, D), F32), jax.ShapeDtypeStruct((1, D), F32)],
        compiler_params=_cparams(1),
    )(ff, wd, x2, target, gate2, gfin)


def _ffn_down_bwd(dx3, gate2, ff, up, cw, cb, wd):
    t = dx3.shape[0]
    tm = _tile_big(t)
    nc = DFF // D

    def body(dx3_ref, g2_ref, ff_ref, ua_ref, uap_ref, uv_ref, uvp_ref, cwa_ref, cwv_ref, cba_ref, cbv_ref, wd_ref,
             da_ref, dv_ref, dwd_ref, dcwa_ref, dcwv_ref, dcba_ref, dcbv_ref):
        i = pl.program_id(1)

        @pl.when(i == 0)
        def _():
            for r in (dwd_ref, dcwa_ref, dcwv_ref, dcba_ref, dcbv_ref):
                r[...] = jnp.zeros_like(r)

        first = i == 0
        ua = ua_ref[...].astype(F32)
        uv = uv_ref[...].astype(F32)
        pa = jnp.where(first, 0.0, uap_ref[...].astype(F32)[HALO - SUBLANES:])
        pv = jnp.where(first, 0.0, uvp_ref[...].astype(F32)[HALO - SUBLANES:])
        act = _conv3(ua, pa, cwa_ref, cba_ref[...])
        val = _conv3(uv, pv, cwv_ref, cbv_ref[...])
        do2 = (dx3_ref[...] * g2_ref[...]).astype(BF16)
        dwd_ref[...] += _dot_tn(ff_ref[...], do2)
        dff = _dot_nt(do2, wd_ref[...])
        ga, ta = _gelu_t(act)
        dact = dff * val * _gelu_grad(act, ta)
        dval = dff * ga
        da_ref[...] = dact.astype(BF16)
        dv_ref[...] = dval.astype(BF16)
        dcba_ref[...] += _colsum(dact)
        dcbv_ref[...] += _colsum(dval)
        for k in range(3):
            dcwa_ref[k:k + 1, :] += _colsum(dact * _shift_down(ua, 2 - k, pa))
            dcwv_ref[k:k + 1, :] += _colsum(dval * _shift_down(uv, 2 - k, pv))

    def halo(col):
        return lambda c, i: (jnp.maximum(i * (tm // HALO) - 1, 0), col(c))

    return pl.pallas_call(
        body, name="ffn_down_bwd", grid=(nc, t // tm),
        in_specs=[pl.BlockSpec((tm, D), lambda c, i: (i, 0)), pl.BlockSpec((1, D), lambda c, i: (0, 0)),
                  pl.BlockSpec((tm, D), lambda c, i: (i, c)),
                  pl.BlockSpec((tm, D), lambda c, i: (i, c)), pl.BlockSpec((HALO, D), halo(lambda c: c)),
                  pl.BlockSpec((tm, D), lambda c, i: (i, nc + c)), pl.BlockSpec((HALO, D), halo(lambda c: nc + c)),
                  pl.BlockSpec((3, D), lambda c, i: (0, c)), pl.BlockSpec((3, D), lambda c, i: (0, nc + c)),
                  pl.BlockSpec((1, D), lambda c, i: (0, c)), pl.BlockSpec((1, D), lambda c, i: (0, nc + c)),
                  pl.BlockSpec((D, D), lambda c, i: (c, 0))],
        out_specs=[pl.BlockSpec((tm, D), lambda c, i: (i, c)), pl.BlockSpec((tm, D), lambda c, i: (i, c)),
                   pl.BlockSpec((D, D), lambda c, i: (c, 0)),
                   pl.BlockSpec((3, D), lambda c, i: (0, c)), pl.BlockSpec((3, D), lambda c, i: (0, c)),
                   pl.BlockSpec((1, D), lambda c, i: (0, c)), pl.BlockSpec((1, D), lambda c, i: (0, c))],
        out_shape=[jax.ShapeDtypeStruct((t, DFF), BF16), jax.ShapeDtypeStruct((t, DFF), BF16),
                   jax.ShapeDtypeStruct((DFF, D), F32),
                   jax.ShapeDtypeStruct((3, DFF), F32), jax.ShapeDtypeStruct((3, DFF), F32),
                   jax.ShapeDtypeStruct((1, DFF), F32), jax.ShapeDtypeStruct((1, DFF), F32)],
        compiler_params=_cparams(2),
    )(dx3, gate2, ff, up, up, up, up, cw, cw, cb, cb, wd)


def _modnorm_bwd(dh, xv, g, scale):
    r = lax.rsqrt(jnp.mean(xv * xv, axis=-1, keepdims=True) + EPS)
    xhat = xv * r
    dxn = dh * (1.0 + scale)
    dxh = dxn * g
    dx = r * (dxh - xhat * jnp.mean(dxh * xhat, axis=-1, keepdims=True))
    return dx, _colsum(dh), _colsum(dh * (xhat * g)), _colsum(dxn * xhat)


def _ffn_up_bwd(dact, dval, cw, wup, x2, dx3, gffn, scale2, o1, gate1):
    t = x2.shape[0]
    tm = _tile_seq(t)
    nt = t // tm
    nc = DFF // D

    def body(da_ref, dan_ref, dv_ref, dvn_ref, cw_ref, w_ref, x2_ref, dx3_ref, g_ref, sc_ref, o1_ref, g1_ref,
             dup_ref, dx2_ref, do1_ref, dsh_ref, dsc_ref, dg_ref, dg1_ref):
        i = pl.program_id(0)

        @pl.when(i == 0)
        def _():
            for r in (dsh_ref, dsc_ref, dg_ref, dg1_ref):
                r[...] = jnp.zeros_like(r)

        last = i == nt - 1
        dh = jnp.zeros((tm, D), F32)
        for half, (d_ref, dn_ref) in enumerate(((da_ref, dan_ref), (dv_ref, dvn_ref))):
            nxt = jnp.where(last, 0.0, dn_ref[...].astype(F32)[:SUBLANES])
            for c in range(nc):
                c0 = half * DFF + c * D
                dv = d_ref[:, c * D:(c + 1) * D].astype(F32)
                nx = nxt[:, c * D:(c + 1) * D]
                dup = (cw_ref[2:3, c0:c0 + D] * dv + cw_ref[1:2, c0:c0 + D] * _shift_up(dv, 1, nx)
                       + cw_ref[0:1, c0:c0 + D] * _shift_up(dv, 2, nx)).astype(BF16)
                dup_ref[:, c0:c0 + D] = dup
                dh = dh + _dot_nt(dup, w_ref[:, c0:c0 + D])
        dxn, dsh, dsc, dg = _modnorm_bwd(dh, x2_ref[...], g_ref[...], sc_ref[...])
        dx2 = dx3_ref[...] + dxn
        dx2_ref[...] = dx2
        do1_ref[...] = (dx2 * g1_ref[...]).astype(BF16)
        dsh_ref[...] += dsh
        dsc_ref[...] += dsc
        dg_ref[...] += dg
        dg1_ref[...] += _colsum(dx2 * o1_ref[...].astype(F32))

    tile = pl.BlockSpec((tm, D), lambda i: (i, 0))
    wide = pl.BlockSpec((tm, DFF), lambda i: (i, 0))
    nxt = pl.BlockSpec((HALO, DFF), lambda i: (jnp.minimum((i + 1) * (tm // HALO), t // HALO - 1), 0))
    vec = _const_spec((1, D))
    vshape = jax.ShapeDtypeStruct((1, D), F32)
    return pl.pallas_call(
        body, name="ffn_up_bwd", grid=(nt,),
        in_specs=[wide, nxt, wide, nxt, _const_spec((3, 2 * DFF)), _const_spec((D, 2 * DFF), True),
                  tile, tile, vec, vec, tile, vec],
        out_specs=[pl.BlockSpec((tm, 2 * DFF), lambda i: (i, 0)), tile, tile, vec, vec, vec, vec],
        out_shape=[jax.ShapeDtypeStruct((t, 2 * DFF), BF16), jax.ShapeDtypeStruct((t, D), F32),
                   jax.ShapeDtypeStruct((t, D), BF16), vshape, vshape, vshape, vshape],
        compiler_params=_cparams(1),
    )(dact, dact, dval, dval, cw, wup, x2, dx3, gffn, scale2, o1, gate1)


def _xt_y(a, b, name):
    t, k = a.shape
    n = b.shape[1]
    tm = min(1024, t)
    bn = 768 if n % 768 == 0 else D

    def body(a_ref, b_ref, o_ref):
        @pl.when(pl.program_id(1) == 0)
        def _():
            o_ref[...] = jnp.zeros_like(o_ref)

        o_ref[...] += _dot_tn(a_ref[...], b_ref[...])

    return pl.pallas_call(
        body, name=name, grid=(n // bn, t // tm),
        in_specs=[pl.BlockSpec((tm, k), lambda j, i: (i, 0)), pl.BlockSpec((tm, bn), lambda j, i: (i, j))],
        out_specs=pl.BlockSpec((k, bn), lambda j, i: (0, j)),
        out_shape=jax.ShapeDtypeStruct((k, n), F32),
        compiler_params=_cparams(2),
    )(a, b)


def _acc_spec(shape, index):
    return pl.BlockSpec(shape, lambda *_: index, pipeline_mode=pl.Buffered(1))


def _out_bwd(do1, wout, merged, ya, yb, z, h1):
    t = do1.shape[0]
    tm = _tile_big(t)

    def body(do1_ref, wo_ref, mg_ref, ya_ref, yb_ref, ga_ref, gb_ref, h1_ref,
             dya_ref, dyb_ref, dz_ref, dwo_ref, dwin_ref):
        @pl.when(pl.program_id(0) == 0)
        def _():
            dwo_ref[...] = jnp.zeros_like(dwo_ref)
            dwin_ref[...] = jnp.zeros_like(dwin_ref)

        do1v = do1_ref[...]
        dwo_ref[...] += _dot_tn(mg_ref[...], do1v)
        dm = _dot_nt(do1v, wo_ref[...])
        sa = _sigmoid(ga_ref[...].astype(F32))
        sb = _sigmoid(gb_ref[...].astype(F32))
        dya_ref[...] = (dm * sa).astype(BF16)
        dyb_ref[...] = (dm * sb).astype(BF16)
        dga = (dm * ya_ref[...].astype(F32) * sa * (1.0 - sa)).astype(BF16)
        dgb = (dm * yb_ref[...].astype(F32) * sb * (1.0 - sb)).astype(BF16)
        dz_ref[:, 0:D] = dga
        dz_ref[:, D:2 * D] = dgb
        h1v = h1_ref[...]
        dwin_ref[:, 0:D] += _dot_tn(h1v, dga)
        dwin_ref[:, D:2 * D] += _dot_tn(h1v, dgb)

    tile = pl.BlockSpec((tm, D), lambda i: (i, 0))
    bshape = jax.ShapeDtypeStruct((t, D), BF16)
    return pl.pallas_call(
        body, name="out_bwd", grid=(t // tm,),
        in_specs=[tile, _const_spec((D, D), True), tile, tile, tile,
                  pl.BlockSpec((tm, D), lambda i: (i, 4)), pl.BlockSpec((tm, D), lambda i: (i, 5)), tile],
        out_specs=[tile, tile, pl.BlockSpec((tm, 2 * D), lambda i: (i, 2)), _acc_spec((D, D), (0, 0)),
                   _acc_spec((D, 2 * D), (0, 2))],
        out_shape=[bshape, bshape, jax.ShapeDtypeStruct((t, NCOL_IN), BF16), jax.ShapeDtypeStruct((D, D), F32),
                   jax.ShapeDtypeStruct((D, NCOL_IN), F32)],
        compiler_params=_cparams(1),
    )(do1, wout, merged, ya, yb, z, z, h1)


def _rnn_bwd(dya, ya_pre, wba, h1, z, h, dz, dwin, cw, cb, wa, ba, wx, bx, lam):
    t = z.shape[0]
    tm = _tile_seq(t)
    nt = t // tm
    ngrp = tm // SUBLANES
    hpt = tm // HALO

    def body(dya_ref, yap_ref, wba_ref, h1_ref, xr_ref, xp_ref, gr_ref, h_ref, hp_ref, dz_any, dwin_any,
             cw_ref, cb_ref, wa_ref, ba_ref, wx_ref, bx_ref, lam_ref,
             dz_ref, dwin_ref, dwba_ref, dcw_ref, dcb_ref, dwa_ref, dba_ref, dwx_ref, dbx_ref, dlam_ref,
             a_first, g_first, dxc_first, b_scr, d_scr, g_scr):
        del dz_any, dwin_any
        i = pl.program_id(0)

        @pl.when(i == 0)
        def _():
            for r in (dwin_ref, dwba_ref, dcw_ref, dcb_ref, dwa_ref, dba_ref, dwx_ref, dbx_ref, dlam_ref,
                      a_first, g_first, dxc_first):
                r[...] = jnp.zeros_like(r)

        dya_v = dya_ref[...]
        dwba_ref[...] += _dot_tn(yap_ref[...], dya_v)
        dyap_v = _dot_nt(dya_v, wba_ref[...])
        h1v = h1_ref[...]

        first_tile = i == nt - 1
        xr = xr_ref[...].astype(F32)
        prev8 = jnp.where(first_tile, 0.0, xp_ref[...].astype(F32)[HALO - SUBLANES:])
        xc = _conv4(xr, prev8, cw_ref, cb_ref[...])
        lam_v = lam_ref[...]
        ls = _log_sigmoid(lam_v)
        ra, ia, a, mult = _lru_gates(xc, wa_ref, ba_ref[...], wx_ref, bx_ref[...], ls)
        hv = h_ref[...]
        hprev8 = jnp.where(first_tile, 0.0, hp_ref[...][HALO - SUBLANES:])
        h_prev = _shift_down(hv, 1, hprev8)
        grv = gr_ref[...].astype(F32)
        gg, tg = _gelu_t(grv)
        dgr = (dyap_v * hv * _gelu_grad(grv, tg)).astype(BF16)
        dz_ref[:, D:2 * D] = dgr
        dwin_ref[:, D:2 * D] += _dot_tn(h1v, dgr)

        b_scr[...] = _shift_up(a, 1, a_first[...])
        d_scr[...] = dyap_v * gg
        row = _row_iota(D)

        def grp(jj, carry):
            r0 = pl.multiple_of((ngrp - 1 - jj) * SUBLANES, SUBLANES)
            bv = b_scr[pl.ds(r0, SUBLANES), :]
            dv = d_scr[pl.ds(r0, SUBLANES), :]
            for d in (1, 2, 4):
                m = row < SUBLANES - d
                dv = jnp.where(m, dv + bv * pltpu.roll(dv, SUBLANES - d, 0), dv)
                bv = jnp.where(m, bv * pltpu.roll(bv, SUBLANES - d, 0), bv)
            gv = dv + bv * carry
            g_scr[pl.ds(r0, SUBLANES), :] = gv
            return gv[0:1, :]

        lax.fori_loop(0, ngrp, grp, g_first[0:1, :])
        g = g_scr[...]
        a_first[...] = a[:SUBLANES]
        g_first[...] = g[:SUBLANES]

        da = g * h_prev
        gx = g * xc
        dmult = gx * ia
        dia = gx * mult
        dxc = g * (mult * ia)
        dla = da * a - dmult * (a * a) / mult
        dra = dla * (LRU_C * ls)
        dlam_ref[...] += _colsum(dla * ra) * (LRU_C * _sigmoid(-lam_v))
        dpa = dra * ra * (1.0 - ra)
        dpx = dia * ia * (1.0 - ia)
        dba_ref[...] += _colsum(dpa)
        dbx_ref[...] += _colsum(dpx)
        dpab = dpa.astype(BF16)
        dpxb = dpx.astype(BF16)
        xcb = xc.astype(BF16)
        for hd in range(NH):
            sl = slice(hd * HD, (hd + 1) * HD)
            dwa_ref[hd] += _dot_tn(xcb[:, sl], dpab[:, sl])
            dwx_ref[hd] += _dot_tn(xcb[:, sl], dpxb[:, sl])
        dxc = dxc + _heads_nt(dpab, wa_ref) + _heads_nt(dpxb, wx_ref)

        nxt = dxc_first[...]
        dxr = (cw_ref[3:4, :] * dxc + cw_ref[2:3, :] * _shift_up(dxc, 1, nxt)
               + cw_ref[1:2, :] * _shift_up(dxc, 2, nxt) + cw_ref[0:1, :] * _shift_up(dxc, 3, nxt))
        dxrb = dxr.astype(BF16)
        dz_ref[:, 0:D] = dxrb
        dwin_ref[:, 0:D] += _dot_tn(h1v, dxrb)
        dxc_first[...] = dxc[:SUBLANES]
        dcb_ref[...] += _colsum(dxc)
        for k in range(4):
            dcw_ref[k:k + 1, :] += _colsum(dxc * _shift_down(xr, 3 - k, prev8))

    def rev(col):
        return lambda i: (nt - 1 - i, col)

    def rev_halo(col):
        return lambda i: (jnp.maximum((nt - 1 - i) * hpt - 1, 0), col)

    vec = _const_spec((1, D))
    wspec = _const_spec((NH, HD, HD))
    vshape = jax.ShapeDtypeStruct((1, D), F32)
    wshape = jax.ShapeDtypeStruct((NH, HD, HD), F32)
    any_spec = pl.BlockSpec(memory_space=pl.ANY)
    tile = pl.BlockSpec((tm, D), rev(0))
    outs = pl.pallas_call(
        body, name="rnn_bwd", grid=(nt,),
        in_specs=[tile, tile, _const_spec((D, D), True), tile,
                  tile, pl.BlockSpec((HALO, D), rev_halo(0)),
                  pl.BlockSpec((tm, D), rev(1)), tile, pl.BlockSpec((HALO, D), rev_halo(0)),
                  any_spec, any_spec,
                  _const_spec((4, D)), vec, wspec, vec, wspec, vec, vec],
        out_specs=[pl.BlockSpec((tm, 2 * D), rev(0)), _acc_spec((D, 2 * D), (0, 0)), _acc_spec((D, D), (0, 0)),
                   _const_spec((4, D)), vec, wspec, vec, wspec, vec, vec],
        out_shape=[jax.ShapeDtypeStruct((t, NCOL_IN), BF16), jax.ShapeDtypeStruct((D, NCOL_IN), F32),
                   jax.ShapeDtypeStruct((D, D), F32), jax.ShapeDtypeStruct((4, D), F32), vshape,
                   wshape, vshape, wshape, vshape, vshape],
        scratch_shapes=[pltpu.VMEM((SUBLANES, D), F32), pltpu.VMEM((SUBLANES, D), F32), pltpu.VMEM((SUBLANES, D), F32),
                        pltpu.VMEM((tm, D), F32), pltpu.VMEM((tm, D), F32), pltpu.VMEM((tm, D), F32)],
        input_output_aliases={9: 0, 10: 1},
        compiler_params=_cparams(1),
    )(dya, ya_pre, wba, h1, z, z, z, h, h, dz, dwin, cw, cb, wa, ba, wx, bx, lam)
    return outs


def _sgu_bwd(dyb, yb_pre, wbb, h1, z, dz, dwin, lng, lnb, wm, wmt, bst, mask):
    t = z.shape[0]
    tm = _tile_big(t)

    def body(dyb_ref, ybp_ref, wbb_ref, h1_ref, zu_ref, zv_ref, dz_any, dwin_any,
             lng_ref, lnb_ref, wm_ref, wmt_ref, bst_ref, mask_ref,
             dz_ref, dwin_ref, dwbb_ref, dws_ref, dbst_ref, dlng_ref, dlnb_ref):
        del dz_any, dwin_any

        @pl.when(pl.program_id(0) == 0)
        def _():
            for r in (dwin_ref, dwbb_ref, dws_ref, dbst_ref, dlng_ref, dlnb_ref):
                r[...] = jnp.zeros_like(r)

        zu = zu_ref[...].astype(F32)
        zv = zv_ref[...].astype(F32)
        lng_v = lng_ref[...]
        gu, tu, tv, rstd, vhat, vb, mixed = _sgu_core(zu, zv, lng_v, lnb_ref[...], wm_ref, bst_ref)
        dyb_v = dyb_ref[...]
        dwbb_ref[...] += _dot_tn(ybp_ref[...], dyb_v)
        dyb = _dot_nt(dyb_v, wbb_ref[...])
        h1v = h1_ref[...]
        dzu = (dyb * mixed * _gelu_grad(zu, tu)).astype(BF16)
        dz_ref[:, 0:D] = dzu
        dwin_ref[:, 0:D] += _dot_tn(h1v, dzu)
        dmix = dyb * gu
        dmb = dmix.astype(BF16)
        rows = []
        lane = lax.broadcasted_iota(jnp.int32, (HD, NH), 1)
        dbst = jnp.zeros((HD, NH), F32)
        for b0 in range(0, tm, HD):
            cols = []
            for g in range(NH):
                sl = slice(g * HD, (g + 1) * HD)
                dmg = dmb[b0:b0 + HD, sl]
                dws_ref[g] += _dot_nt(dmg, vb[b0:b0 + HD, sl]) * mask_ref[...]
                cols.append(jnp.dot(wmt_ref[g], dmg, preferred_element_type=F32))
                dbst = dbst + jnp.where(lane == g, jnp.sum(dmix[b0:b0 + HD, sl], axis=1, keepdims=True), 0.0)
            rows.append(jnp.concatenate(cols, axis=1))
        dbst_ref[...] += dbst
        dvln = jnp.concatenate(rows, axis=0) if len(rows) > 1 else rows[0]
        dlng_ref[...] += _colsum(dvln * vhat)
        dlnb_ref[...] += _colsum(dvln)
        dvh = dvln * lng_v
        dgv = rstd * (dvh - jnp.mean(dvh, axis=-1, keepdims=True)
                      - vhat * jnp.mean(dvh * vhat, axis=-1, keepdims=True))
        dzv = (dgv * _gelu_grad(zv, tv)).astype(BF16)
        dz_ref[:, D:2 * D] = dzv
        dwin_ref[:, D:2 * D] += _dot_tn(h1v, dzv)

    vec = _const_spec((1, D))
    wspec = _const_spec((NH, HD, HD))
    vshape = jax.ShapeDtypeStruct((1, D), F32)
    tile = pl.BlockSpec((tm, D), lambda i: (i, 0))
    any_spec = pl.BlockSpec(memory_space=pl.ANY)
    return pl.pallas_call(
        body, name="sgu_bwd", grid=(t // tm,),
        in_specs=[tile, tile, _const_spec((D, D), True), tile,
                  pl.BlockSpec((tm, D), lambda i: (i, 2)), pl.BlockSpec((tm, D), lambda i: (i, 3)), any_spec, any_spec,
                  vec, vec, wspec, wspec, _const_spec((HD, NH)), _const_spec((HD, HD))],
        out_specs=[pl.BlockSpec((tm, 2 * D), lambda i: (i, 1)), _acc_spec((D, 2 * D), (0, 1)), _acc_spec((D, D), (0, 0)),
                   wspec, _const_spec((HD, NH)), vec, vec],
        out_shape=[jax.ShapeDtypeStruct((t, NCOL_IN), BF16), jax.ShapeDtypeStruct((D, NCOL_IN), F32),
                   jax.ShapeDtypeStruct((D, D), F32), jax.ShapeDtypeStruct((NH, HD, HD), F32),
                   jax.ShapeDtypeStruct((HD, NH), F32), vshape, vshape],
        input_output_aliases={6: 0, 7: 1},
        compiler_params=_cparams(1),
    )(dyb, yb_pre, wbb, h1, z, z, dz, dwin, lng, lnb, wm, wmt, bst, mask)


def _in_bwd(dz, win, x, dx2, g, scale1):
    t = x.shape[0]
    tm = _tile_big(t)

    def body(dz_ref, w_ref, x_ref, dx2_ref, g_ref, sc_ref, dx_ref, dsh_ref, dsc_ref, dg_ref):
        @pl.when(pl.program_id(0) == 0)
        def _():
            for r in (dsh_ref, dsc_ref, dg_ref):
                r[...] = jnp.zeros_like(r)

        dh = jnp.zeros((tm, D), F32)
        for c0 in range(0, NCOL_IN, D):
            dh = dh + _dot_nt(dz_ref[:, c0:c0 + D], w_ref[:, c0:c0 + D])
        dxn, dsh, dsc, dg = _modnorm_bwd(dh, x_ref[...], g_ref[...], sc_ref[...])
        dx_ref[...] = dx2_ref[...] + dxn
        dsh_ref[...] += dsh
        dsc_ref[...] += dsc
        dg_ref[...] += dg

    tile = pl.BlockSpec((tm, D), lambda i: (i, 0))
    vec = _const_spec((1, D))
    vshape = jax.ShapeDtypeStruct((1, D), F32)
    return pl.pallas_call(
        body, name="in_bwd", grid=(t // tm,),
        in_specs=[pl.BlockSpec((tm, NCOL_IN), lambda i: (i, 0)), _const_spec((D, NCOL_IN), True), tile, tile, vec, vec],
        out_specs=[tile, vec, vec, vec],
        out_shape=[jax.ShapeDtypeStruct((t, D), F32), vshape, vshape, vshape],
        compiler_params=_cparams(1),
    )(dz, win, x, dx2, g, scale1)


def _mod_cols(c_all, w_ada, b_cols):
    nb, cols = c_all.shape[0], w_ada.shape[1]

    def body(c_ref, w_ref, b_ref, o_ref):
        cv = c_ref[...]
        ca = (cv * _sigmoid(cv)).astype(BF16)
        o_ref[...] = jnp.dot(ca, w_ref[...].astype(BF16), preferred_element_type=F32) + b_ref[...]

    return pl.pallas_call(body, name="mod_cols", out_shape=jax.ShapeDtypeStruct((nb, cols), F32))(c_all, w_ada, b_cols)


def _ada_grad(c_all, dmod_cols):
    cols = dmod_cols.shape[1]

    def body(c_ref, d_ref, o_ref):
        cv = c_ref[...]
        ca = (cv * _sigmoid(cv)).astype(BF16)
        o_ref[...] = _dot_tn(ca, d_ref[...].astype(BF16))

    return pl.pallas_call(body, name="ada_grad", out_shape=jax.ShapeDtypeStruct((D, cols), F32))(c_all, dmod_cols)


def _adamw(w, m, v, parts, name):
    rows, cols = w.shape
    tr = _row_tile(rows)
    stacked = [p.ndim == 3 for p in parts]
    bc1 = 1.0 - ADAM_B1 ** ADAM_STEP
    bc2 = 1.0 - ADAM_B2 ** ADAM_STEP

    def body(*refs):
        w_ref, m_ref, v_ref = refs[:3]
        p_refs = refs[3:3 + len(parts)]
        g_ref, d_ref, mo_ref, vo_ref = refs[3 + len(parts):]
        g = None
        for p_ref, st in zip(p_refs, stacked):
            terms = [p_ref[k].astype(F32) for k in range(p_ref.shape[0])] if st else [p_ref[...].astype(F32)]
            for term in terms:
                g = term if g is None else g + term
        mn = ADAM_B1 * m_ref[...] + (1.0 - ADAM_B1) * g
        vn = ADAM_B2 * v_ref[...] + (1.0 - ADAM_B2) * (g * g)
        g_ref[...] = g
        mo_ref[...] = mn
        vo_ref[...] = vn
        d_ref[...] = -ADAM_LR * ((mn / bc1) / (jnp.sqrt(vn / bc2) + ADAM_EPS) + ADAM_WD * w_ref[...])

    tile = pl.BlockSpec((tr, cols), lambda i: (i, 0))
    p_specs = [pl.BlockSpec((p.shape[0], tr, cols), lambda i: (0, i, 0)) if st else tile for p, st in zip(parts, stacked)]
    shp = jax.ShapeDtypeStruct((rows, cols), F32)
    return pl.pallas_call(
        body, name=name, grid=(rows // tr,),
        in_specs=[tile, tile, tile] + p_specs, out_specs=[tile] * 4, out_shape=[shp] * 4,
        compiler_params=_cparams(1),
    )(w, m, v, *parts)


def _mesh_pos():
    return lax.axis_index("x"), lax.axis_index("y"), lax.axis_index("c")


def _other_chips(x, y):
    return [(1 - x, y), (x, 1 - y), (1 - x, 1 - y)]


def _block_of(ref, axis, index, size):
    if axis == 0:
        return ref.at[index]
    return ref.at[:, pl.ds(pl.multiple_of(index * size, 128), size)]


def _all_gather(shards, axes, name):
    n = len(shards)
    per = 7

    def body(*refs):
        ins, outs, done = refs[:n], refs[n:2 * n], refs[2 * n]
        send_sems, recv_sems, local_sems = refs[2 * n + 1:]
        x, y, c = _mesh_pos()
        me, sibling = (x, y, c), (x, y, 1 - c)
        chips = _other_chips(x, y)

        def rows(a, pos):
            return _block_of(outs[a], axes[a], 4 * pos[0] + 2 * pos[1] + pos[2], shards[a].shape[-1])

        def copy(a, k, block, to, src=None):
            return pltpu.make_async_remote_copy(
                src_ref=rows(a, block) if src is None else src, dst_ref=rows(a, block),
                send_sem=send_sems.at[a * per + k], recv_sem=recv_sems.at[a * per + k],
                device_id=to, device_id_type=MESH_IDS)

        mine = [pltpu.make_async_copy(ins[a], rows(a, me), local_sems.at[a]) for a in range(n)]
        for cp in mine:
            cp.start()
        first = []
        for a in range(n):
            first.append(copy(a, 0, me, sibling, src=ins[a]))
            first += [copy(a, 1 + j, me, (*chip, c), src=ins[a]) for j, chip in enumerate(chips)]
        for cp in first:
            cp.start()
        passed = []
        for j, chip in enumerate(chips):
            for a in range(n):
                copy(a, 1 + j, (*chip, c), me).wait_recv()
                fwd = copy(a, 4 + j, (*chip, c), sibling)
                fwd.start()
                passed.append(fwd)
        for a in range(n):
            copy(a, 0, sibling, me).wait_recv()
            for j, chip in enumerate(chips):
                copy(a, 4 + j, (*chip, 1 - c), me).wait_recv()
        for cp in first + passed:
            cp.wait_send()
        for cp in mine:
            cp.wait()
        done[...] = jnp.zeros_like(done)

    def full_shape(s, ax):
        return (N_DEV,) + s.shape if ax == 0 else s.shape[:-1] + (N_DEV * s.shape[-1],)

    any_spec = pl.BlockSpec(memory_space=pl.ANY)
    outs = pl.pallas_call(
        body, name=name,
        in_specs=[any_spec] * n, out_specs=[any_spec] * n + [pl.BlockSpec(memory_space=pltpu.VMEM)],
        out_shape=[jax.ShapeDtypeStruct(full_shape(s, ax), s.dtype) for s, ax in zip(shards, axes)]
        + [jax.ShapeDtypeStruct((SUBLANES, LANES), F32)],
        scratch_shapes=[pltpu.SemaphoreType.DMA((n * per,)), pltpu.SemaphoreType.DMA((n * per,)),
                        pltpu.SemaphoreType.DMA((n,))],
    )(*shards)
    return outs[:n], outs[n]


def _chip_blocks(x, y):
    return [(x, y)] + _other_chips(x, y)


def _sibling_reduce(g, axis, name):
    rows, cols = (g.shape[1], g.shape[2]) if axis == 0 else (g.shape[0], g.shape[1] // N_DEV)
    chunk = math.gcd(rows, 64)

    def body(g_ref, own_ref, pay_ref, send_buf, keep_buf, recv_buf, send_sems, recv_sems, stage_sems, keep_sems):
        x, y, c = _mesh_pos()
        sibling = (x, y, 1 - c)
        chips = _chip_blocks(x, y)
        stage, keep, push = [], [], []
        for j, (px, py) in enumerate(chips):
            theirs = _block_of(g_ref, axis, 4 * px + 2 * py + (1 - c), cols)
            ours = _block_of(g_ref, axis, 4 * px + 2 * py + c, cols)
            stage.append(pltpu.make_async_copy(theirs, send_buf.at[j], stage_sems.at[j]))
            keep.append(pltpu.make_async_copy(ours, keep_buf.at[j], keep_sems.at[j]))
            push.append(pltpu.make_async_remote_copy(
                src_ref=send_buf.at[j], dst_ref=recv_buf.at[j], send_sem=send_sems.at[j], recv_sem=recv_sems.at[j],
                device_id=sibling, device_id_type=MESH_IDS))
        for j in range(4):
            stage[j].start()
        for j in range(4):
            keep[j].start()
        for j in range(4):
            stage[j].wait()
            push[j].start()
        for j in range(4):
            push[j].wait_recv()
            keep[j].wait()
            dst = own_ref if j == 0 else pay_ref.at[j - 1]

            def add(r, carry, j=j, dst=dst):
                sl = pl.ds(pl.multiple_of(r * chunk, chunk), chunk)
                dst[sl, :] = (keep_buf[j, sl, :] + recv_buf[j, sl, :]).astype(dst.dtype)
                return carry

            lax.fori_loop(0, rows // chunk, add, 0)
        for j in range(4):
            push[j].wait_send()

    vmem = pl.BlockSpec(memory_space=pltpu.VMEM)
    buf = pltpu.VMEM((4, rows, cols), F32)
    return pl.pallas_call(
        body, name=name,
        in_specs=[pl.BlockSpec(memory_space=pl.ANY)], out_specs=[vmem, vmem],
        out_shape=[jax.ShapeDtypeStruct((rows, cols), F32), jax.ShapeDtypeStruct((3, rows, cols), BF16)],
        scratch_shapes=[buf, buf, buf, pltpu.SemaphoreType.DMA((4,)), pltpu.SemaphoreType.DMA((4,)),
                        pltpu.SemaphoreType.DMA((4,)), pltpu.SemaphoreType.DMA((4,))],
        compiler_params=pltpu.CompilerParams(vmem_limit_bytes=VMEM_LIMIT),
    )(g)


_HBM_SPEC = pl.BlockSpec(memory_space=pltpu.HBM)
_SEM_SPEC = pl.BlockSpec(memory_space=pltpu.SEMAPHORE)
_SIDE_EFFECT = pltpu.SideEffectType.DATAFLOW_SIDE_EFFECTING


def _exchange_start(name, srcs, lands, plan, n_copies):
    nb = len(srcs) + len(lands)

    def body(*refs):
        bufs, send_sems, recv_sems, token = refs[:nb], refs[nb], refs[nb + 1], refs[-1]
        for cp in plan(bufs[:len(srcs)], bufs[len(srcs):], send_sems, recv_sems):
            cp.start()
        token[...] = jnp.zeros_like(token)

    arrays = list(srcs) + list(lands)
    outs = pl.pallas_call(
        body, name=name,
        out_shape=(pltpu.SemaphoreType.DMA((n_copies,)), pltpu.SemaphoreType.DMA((n_copies,)),
                   *[pltpu.HBM(a.shape, a.dtype) for a in arrays], jax.ShapeDtypeStruct((SUBLANES, LANES), F32)),
        in_specs=[_HBM_SPEC] * nb,
        out_specs=(_SEM_SPEC, _SEM_SPEC, *[_HBM_SPEC] * nb, pl.BlockSpec(memory_space=pltpu.VMEM)),
        input_output_aliases={k: 2 + k for k in range(nb)},
        compiler_params=pltpu.CompilerParams(has_side_effects=_SIDE_EFFECT),
    )(*[pltpu.with_memory_space_constraint(a, pltpu.HBM) for a in arrays])
    return outs[0], outs[1], outs[2:2 + len(srcs)], outs[2 + len(srcs):2 + nb], outs[-1]


def _exchange_wait(name, send_sems, recv_sems, srcs, lands, plan, after):
    nb = len(srcs) + len(lands)
    after = list(after)

    def body(*refs):
        bufs, send_ref, recv_ref = refs[:nb], refs[nb], refs[nb + 1]
        for cp in plan(bufs[:len(srcs)], bufs[len(srcs):], send_ref, recv_ref):
            cp.wait_send()
            cp.wait_recv()

    arrays = list(srcs) + list(lands)
    outs = pl.pallas_call(
        body, name=name,
        out_shape=tuple(pltpu.HBM(a.shape, a.dtype) for a in arrays),
        in_specs=[_HBM_SPEC] * nb + [_SEM_SPEC, _SEM_SPEC] + [pl.BlockSpec(memory_space=pl.ANY)] * len(after),
        out_specs=tuple([_HBM_SPEC] * nb),
        input_output_aliases={k: k for k in range(nb)},
        compiler_params=pltpu.CompilerParams(has_side_effects=_SIDE_EFFECT),
    )(*arrays, send_sems, recv_sems, *after)
    return outs[len(srcs):]


def _gather_plan(axes, sizes):
    def plan(src_refs, land_refs, send_sems, recv_sems):
        x, y, c = _mesh_pos()
        copies = []
        for a, (src, land) in enumerate(zip(src_refs, land_refs)):
            mine = _block_of(land, axes[a], 4 * x + 2 * y + c, sizes[a])
            for k in range(1, N_DEV):
                peer = (1 - x if k & 4 else x, 1 - y if k & 2 else y, 1 - c if k & 1 else c)
                idx = a * (N_DEV - 1) + k - 1
                copies.append(pltpu.make_async_remote_copy(
                    src_ref=src, dst_ref=mine, send_sem=send_sems.at[idx], recv_sem=recv_sems.at[idx],
                    device_id=peer, device_id_type=MESH_IDS))
        return copies
    return plan


def _chip_plan(src_refs, land_refs, send_sems, recv_sems):
    x, y, c = _mesh_pos()
    copies = []
    for a, (src, land) in enumerate(zip(src_refs, land_refs)):
        for j, chip in enumerate(_other_chips(x, y)):
            copies.append(pltpu.make_async_remote_copy(
                src_ref=src.at[j], dst_ref=land.at[j], send_sem=send_sems.at[3 * a + j],
                recv_sem=recv_sems.at[3 * a + j], device_id=(*chip, c), device_id_type=MESH_IDS))
    return copies


def _own_block_placed(shard, axis, me):
    if axis == 0:
        full = lax.empty((N_DEV,) + shard.shape, shard.dtype)
        return lax.dynamic_update_slice(full, shard[None], (me,) + (0,) * shard.ndim)
    full = lax.empty(shard.shape[:-1] + (N_DEV * shard.shape[-1],), shard.dtype)
    return lax.dynamic_update_slice(full, shard, (0,) * (shard.ndim - 1) + (me * shard.shape[-1],))


def _local_step(x, target, mod, win, late_weights, p, grads_ready=None):
    shift1, scale1, gate1, shift2, scale2, gate2 = (mod[k] for k in range(6))

    def after_token(v, token):
        return v if token is None else v + token[0:1, 0:1]
    wa, wx = p["lru_w_a"].astype(BF16), p["lru_w_x"].astype(BF16)
    mask = jnp.tril(jnp.ones((HD, HD), F32))
    wm = (p["sgu_w_s"] * mask).astype(BF16)
    wmt = jnp.swapaxes(wm, 1, 2)
    bst = jnp.transpose(p["sgu_b_s"])

    h1, z = _norm_proj(x, p["norm_mix_g"], scale1, shift1, win, "mix_proj")
    hstate, ya_pre = _rnn_fwd(z, p["rnn_conv_w"], p["rnn_conv_b"], wa, p["lru_b_a"], wx, p["lru_b_x"], p["lru_lambda"])
    yb_pre = _sgu_fwd(z, p["sgu_ln_g"], p["sgu_ln_b"], wm, bst)
    wba, wbb, wout = late_weights("merge", [ya_pre, yb_pre])
    x2, ya, yb, merged, o1 = _merge_fwd(ya_pre, yb_pre, z, x, gate1, wba, wbb, wout)
    wup = late_weights("ffn_up", [x2])
    h2, up, ff = _ffn_proj_mid(x2, p["norm_ffn_g"], scale2, shift2, wup, p["ffn_conv_w"], p["ffn_conv_b"])
    wd = late_weights("ffn_down", [ff])
    dx3, loss, d_gfin, d_gate2 = _ffn_out_loss(ff, wd, x2, target, gate2, p["norm_final_g"])

    dact, dval, d_wd, dcw_a, dcw_v, dcb_a, dcb_v = _ffn_down_bwd(dx3, gate2, ff, up, p["ffn_conv_w"], p["ffn_conv_b"], wd)
    dup, dx2, do1, d_shift2, d_scale2, d_gffn, d_gate1 = _ffn_up_bwd(
        dact, dval, p["ffn_conv_w"], wup, x2, dx3, p["norm_ffn_g"], scale2, o1, gate1)
    d_wup = _xt_y(h2, dup, "w_up_grad")
    token = grads_ready("ffn", {"w_up": d_wup, "w_down": d_wd}) if grads_ready else None

    dya, dyb, dz, d_wout, d_win = _out_bwd(do1, wout, merged, ya, yb, z, h1)
    dz, d_win, d_wba, d_cw, d_cb, d_wa, d_ba, d_wx, d_bx, d_lam = _rnn_bwd(
        dya, ya_pre, wba, h1, z, hstate, dz, d_win, p["rnn_conv_w"], p["rnn_conv_b"], wa, p["lru_b_a"], wx,
        p["lru_b_x"], after_token(p["lru_lambda"], token))
    dz, d_win, d_wbb, d_ws, d_bst, d_lng, d_lnb = _sgu_bwd(
        dyb, yb_pre, wbb, h1, z, dz, d_win, p["sgu_ln_g"], p["sgu_ln_b"], wm, wmt, bst, mask)
    mixer = {"w_in": d_win, "w_out": d_wout, "w_branch_a": d_wba, "w_branch_b": d_wbb}
    token = grads_ready("mixer", mixer) if grads_ready else None
    grad_x, d_shift1, d_scale1, d_gmix = _in_bwd(dz, win, x, dx2, after_token(p["norm_mix_g"], token), scale1)

    small = {
        "norm_mix_g": d_gmix, "rnn_conv_w": d_cw, "rnn_conv_b": d_cb, "lru_w_a": d_wa, "lru_b_a": d_ba,
        "lru_w_x": d_wx, "lru_b_x": d_bx, "lru_lambda": d_lam, "sgu_ln_g": d_lng, "sgu_ln_b": d_lnb,
        "sgu_w_s": d_ws, "sgu_b_s": jnp.transpose(d_bst), "norm_ffn_g": d_gffn,
        "ffn_conv_w": jnp.concatenate([dcw_a, dcw_v], axis=1), "ffn_conv_b": jnp.concatenate([dcb_a, dcb_v], axis=1),
        "norm_final_g": d_gfin,
    }
    dmod = jnp.stack([d_shift1, d_scale1, d_gate1, d_shift2, d_scale2, d_gate2])
    big = {"w_in": d_win, "w_up": d_wup, "w_branch_a": d_wba, "w_branch_b": d_wbb, "w_out": d_wout, "w_down": d_wd}
    return loss, grad_x, big, small, dmod


REPLICATED = ["b_ada", "norm_mix_g", "rnn_conv_b", "lru_w_a", "lru_b_a", "lru_w_x", "lru_b_x", "lru_lambda",
              "sgu_ln_g", "sgu_ln_b", "sgu_w_s", "sgu_b_s", "norm_ffn_g", "ffn_conv_b", "norm_final_g"]
COL_SHARDED = ["rnn_conv_w", "ffn_conv_w"]
SMALL_NAMES = REPLICATED + COL_SHARDED
BIG_NAMES = ["w_in", "w_up", "w_branch_a", "w_branch_b", "w_out", "w_down"]
BIG_AXES = [1, 1, 0, 0, 0, 0]
WEIGHTS = ["w_ada", "b_ada", "norm_mix_g", "w_in", "rnn_conv_w", "rnn_conv_b", "lru_w_a", "lru_b_a", "lru_w_x",
           "lru_b_x", "lru_lambda", "sgu_ln_g", "sgu_ln_b", "sgu_w_s", "sgu_b_s", "w_branch_a", "w_branch_b",
           "w_out", "norm_ffn_g", "w_up", "ffn_conv_w", "ffn_conv_b", "w_down", "norm_final_g"]
LANES = 128


def _pack_rows(shape):
    return math.prod(shape) // LANES


def _pack(arrays):
    return jnp.concatenate([a.reshape(-1, LANES) for a in arrays], axis=0)


def _unpack(packed, shapes):
    out, r0 = [], 0
    for s in shapes:
        nrow = math.prod(s) // LANES
        out.append(packed[r0:r0 + nrow].reshape(s))
        r0 += nrow
    return out


def kernel(x, c, w_ada, b_ada, norm_mix_g, w_in, rnn_conv_w, rnn_conv_b, lru_w_a, lru_b_a, lru_w_x, lru_b_x, lru_lambda, sgu_ln_g, sgu_ln_b, sgu_w_s, sgu_b_s, w_branch_a, w_branch_b, w_out, norm_ffn_g, w_up, ffn_conv_w, ffn_conv_b, w_down, norm_final_g, loss_target, m_w_ada, m_b_ada, m_norm_mix_g, m_w_in, m_rnn_conv_w, m_rnn_conv_b, m_lru_w_a, m_lru_b_a, m_lru_w_x, m_lru_b_x, m_lru_lambda, m_sgu_ln_g, m_sgu_ln_b, m_sgu_w_s, m_sgu_b_s, m_w_branch_a, m_w_branch_b, m_w_out, m_norm_ffn_g, m_w_up, m_ffn_conv_w, m_ffn_conv_b, m_w_down, m_norm_final_g, v_w_ada, v_b_ada, v_norm_mix_g, v_w_in, v_rnn_conv_w, v_rnn_conv_b, v_lru_w_a, v_lru_b_a, v_lru_w_x, v_lru_b_x, v_lru_lambda, v_sgu_ln_g, v_sgu_ln_b, v_sgu_w_s, v_sgu_b_s, v_w_branch_a, v_w_branch_b, v_w_out, v_norm_ffn_g, v_w_up, v_ffn_conv_w, v_ffn_conv_b, v_w_down, v_norm_final_g):
    given = dict(locals())
    me = 4 * lax.axis_index("x") + 2 * lax.axis_index("y") + lax.axis_index("c")
    ada_cols = w_ada.shape[2]
    conv_cols = {"rnn_conv_w": rnn_conv_w.shape[2], "ffn_conv_w": ffn_conv_w.shape[2]}

    (win, c_all, cw_rnn, cw_ffn), _ = _all_gather(
        [w_in[0].astype(BF16), c.reshape(1, 1, D), rnn_conv_w[0], ffn_conv_w[0]], [1, 0, 1, 1], "gather_first")
    c_all = c_all.reshape(N_DEV, D)

    b_cols = lax.dynamic_slice_in_dim(b_ada, me * ada_cols, ada_cols, axis=1)
    (mod_all,), mod_done = _all_gather(
        [_mod_cols(c_all, w_ada[0], b_cols).reshape(1, N_DEV, ada_cols)], [0], "gather_mod")
    mod_all = mod_all.reshape(N_DEV, N_DEV, ada_cols)
    mod_mine = lax.dynamic_index_in_dim(mod_all, me, axis=1, keepdims=False).reshape(6, 1, D)

    late_groups = {"merge": (["w_branch_a", "w_branch_b", "w_out"], [0, 0, 0]), "ffn_up": (["w_up"], [1]),
                   "ffn_down": (["w_down"], [0])}
    in_flight, started = {}, mod_done[0:1, 0:1]
    for stage, (names, axes) in late_groups.items():
        shards = [(given[n][0] + started).astype(BF16) for n in names]
        plan = _gather_plan(axes, [s.shape[-1] for s in shards])
        send, recv, srcs, lands, token = _exchange_start(
            "gather_start_" + stage, shards, [_own_block_placed(s, ax, me) for s, ax in zip(shards, axes)], plan,
            len(shards) * (N_DEV - 1))
        in_flight[stage] = (send, recv, srcs, lands, plan)
        started = started + token[0:1, 0:1]

    def late_weights(stage, after):
        send, recv, srcs, lands, plan = in_flight[stage]
        full = _exchange_wait("gather_wait_" + stage, send, recv, srcs, lands, plan, after)
        full = [w.reshape(-1, D) if ax == 0 else w for w, ax in zip(full, late_groups[stage][1])]
        return full if len(full) > 1 else full[0]

    mod_mine = mod_mine + started

    reducing = {}

    def grads_ready(stage, grads):
        names = [n for n in BIG_NAMES if n in grads]
        sums = []
        for n in names:
            ax = BIG_AXES[BIG_NAMES.index(n)]
            g = grads[n] if ax == 1 else grads[n].reshape(N_DEV, grads[n].shape[0] // N_DEV, grads[n].shape[1])
            sums.append(_sibling_reduce(g, ax, "reduce_sibling_" + n))
        pays = [pay for _, pay in sums]
        send, recv, srcs, lands, tok = _exchange_start(
            "reduce_start_" + stage, pays, [lax.empty(p_.shape, p_.dtype) for p_ in pays], _chip_plan, 3 * len(pays))
        reducing[stage] = (names, [own for own, _ in sums], send, recv, srcs, lands)
        return tok

    p = {n: given[n][0] for n in REPLICATED if n not in ("b_ada", "norm_final_g")}
    p = {n: (a.reshape(1, -1) if a.ndim == 1 else a) for n, a in p.items()}
    p["rnn_conv_w"], p["ffn_conv_w"] = cw_rnn, cw_ffn
    p["norm_final_g"] = norm_final_g.reshape(1, D)
    loss, grad_x, _, small, dmod = _local_step(x[0], loss_target[0], mod_mine, win, late_weights, p, grads_ready)

    small["b_ada"] = dmod.reshape(1, 6 * D)
    rows_of = {n: _pack_rows(small[n].shape) for n in SMALL_NAMES}
    start_of = {n: sum(rows_of[q] for q in SMALL_NAMES[:k]) for k, n in enumerate(SMALL_NAMES)}
    pack = _pack([small[n] for n in SMALL_NAMES])
    (packs,), _ = _all_gather([pack[None]], [0], "gather_small")
    packs = packs.reshape(N_DEV, pack.shape[0], LANES)

    out = {}
    for stage, (names, owns, send, recv, srcs, lands) in reducing.items():
        landed = _exchange_wait("reduce_wait_" + stage, send, recv, srcs, lands, _chip_plan, [packs])
        for n, own, got in zip(names, owns, landed):
            out[n] = _adamw(given[n][0], given["m_" + n][0], given["v_" + n][0], [own, got], "adamw_" + n)

    dmod_all = packs[:, :rows_of["b_ada"]].reshape(N_DEV, 6 * D)
    dmod_cols = lax.dynamic_slice_in_dim(dmod_all, me * ada_cols, ada_cols, axis=1)
    out["w_ada"] = _adamw(w_ada[0], m_w_ada[0], v_w_ada[0], [_ada_grad(c_all, dmod_cols)], "adamw_w_ada")

    rep_rows = sum(rows_of[n] for n in REPLICATED)
    res = _adamw(*[_pack([given[pre + n] for n in REPLICATED]) for pre in ("", "m_", "v_")],
                 [packs[:, :rep_rows]], "adamw_small")
    unpacked = [_unpack(r, [given[n].shape for n in REPLICATED]) for r in res]
    for k, n in enumerate(REPLICATED):
        out[n] = tuple(u[k] for u in unpacked)

    for n in COL_SHARDED:
        full = packs[:, start_of[n]:start_of[n] + rows_of[n]].reshape(N_DEV, small[n].shape[0], small[n].shape[1])
        mine = lax.dynamic_slice_in_dim(full, me * conv_cols[n], conv_cols[n], axis=2)
        out[n] = _adamw(given[n][0], given["m_" + n][0], given["v_" + n][0], [mine], "adamw_" + n)

    total = lax.psum(loss[0, 0], ("x", "y", "c"))
    results = [total, grad_x[None]]
    for kind in range(4):
        results += [out[n][kind].reshape(given[n].shape) for n in WEIGHTS]
    return tuple(results)
```

```python
import math

import jax
import jax.numpy as jnp
from jax import lax
from jax.experimental import pallas as pl
from jax.experimental.pallas import tpu as pltpu

F32 = jnp.float32
BF16 = jnp.bfloat16
MESH_IDS = pl.DeviceIdType.MESH

D = 1024
NH = 8
HD = 128
NCOL_IN = 6 * D
DFF = 3 * D
N_DEV = 8
EPS = 1e-6
LRU_C = 8.0
ADAM_LR, ADAM_B1, ADAM_B2, ADAM_EPS, ADAM_WD, ADAM_STEP = 0.001, 0.9, 0.999, 1e-08, 0.01, 10

SUBLANES = 8
HALO = 16
VMEM_LIMIT = 56 * 1024 * 1024
GELU_K = math.sqrt(2.0 / math.pi)
GELU_C = 0.044715


def _cparams(n_axes):
    return pltpu.CompilerParams(dimension_semantics=("arbitrary",) * n_axes, vmem_limit_bytes=VMEM_LIMIT)


def _const_spec(shape, single_buffer=False):
    nd = len(shape)
    if single_buffer:
        return pl.BlockSpec(shape, lambda *_: (0,) * nd, pipeline_mode=pl.Buffered(1))
    return pl.BlockSpec(shape, lambda *_: (0,) * nd)


def _tile_big(t):
    return min(512, t)


def _tile_seq(t):
    return min(256, t)


def _row_tile(rows):
    if rows <= 512:
        return rows
    return next(tr for tr in range(512, 0, -SUBLANES) if rows % tr == 0)


def _gelu_t(x):
    t = jnp.tanh(GELU_K * (x + GELU_C * (x * x * x)))
    return 0.5 * x * (1.0 + t), t


def _gelu_grad(x, t):
    return 0.5 * (1.0 + t) + 0.5 * x * (1.0 - t * t) * (GELU_K * (1.0 + 3.0 * GELU_C * x * x))


def _sigmoid(x):
    return 1.0 / (1.0 + jnp.exp(-x))


def _log_sigmoid(x):
    return -(jnp.maximum(-x, 0.0) + jnp.log1p(jnp.exp(-jnp.abs(x))))


def _row_iota(cols):
    return lax.broadcasted_iota(jnp.int32, (SUBLANES, cols), 0)


def _shift_down(x, k, prev8):
    if k == 0:
        return x
    r = pltpu.roll(x, k, 0)
    p = pltpu.roll(prev8, k, 0)
    head = jnp.where(_row_iota(x.shape[1]) < k, p, r[:SUBLANES])
    return jnp.concatenate([head, r[SUBLANES:]], axis=0)


def _shift_up(x, k, next8):
    if k == 0:
        return x
    n = x.shape[0]
    r = pltpu.roll(x, n - k, 0)
    q = pltpu.roll(next8, SUBLANES - k, 0)
    tail = jnp.where(_row_iota(x.shape[1]) >= SUBLANES - k, q, r[n - SUBLANES:])
    return jnp.concatenate([r[:n - SUBLANES], tail], axis=0)


def _heads_nn(x_bf, w_ref):
    return jnp.concatenate(
        [jnp.dot(x_bf[:, h * HD:(h + 1) * HD], w_ref[h], preferred_element_type=F32) for h in range(NH)], axis=1)


def _heads_nt(x_bf, w_ref):
    return jnp.concatenate(
        [lax.dot_general(x_bf[:, h * HD:(h + 1) * HD], w_ref[h], (((1,), (1,)), ((), ())), preferred_element_type=F32)
         for h in range(NH)], axis=1)


def _dot_nt(a, b):
    return lax.dot_general(a, b, (((1,), (1,)), ((), ())), preferred_element_type=F32)


def _dot_tn(a, b):
    return lax.dot_general(a, b, (((0,), (0,)), ((), ())), preferred_element_type=F32)


def _colsum(x):
    return jnp.sum(x, axis=0, keepdims=True)


def _prev_halo_map(tm, col):
    return lambda i, *_: (jnp.maximum(i * (tm // HALO) - 1, 0), col)


def _norm_proj(x, g, scale, shift, w, name):
    t, n = x.shape[0], w.shape[1]
    tm = _tile_big(t)

    def body(x_ref, g_ref, sc_ref, sh_ref, w_ref, h_ref, z_ref):
        xv = x_ref[...]
        r = lax.rsqrt(jnp.mean(xv * xv, axis=-1, keepdims=True) + EPS)
        hb = ((xv * r * g_ref[...]) * (1.0 + sc_ref[...]) + sh_ref[...]).astype(BF16)
        h_ref[...] = hb
        for c0 in range(0, n, D):
            z_ref[:, c0:c0 + D] = jnp.dot(hb, w_ref[:, c0:c0 + D], preferred_element_type=F32).astype(BF16)

    vec = _const_spec((1, D))
    return pl.pallas_call(
        body, name=name, grid=(t // tm,),
        in_specs=[pl.BlockSpec((tm, D), lambda i: (i, 0)), vec, vec, vec, _const_spec((D, n), True)],
        out_specs=[pl.BlockSpec((tm, D), lambda i: (i, 0)), pl.BlockSpec((tm, n), lambda i: (i, 0))],
        out_shape=[jax.ShapeDtypeStruct((t, D), BF16), jax.ShapeDtypeStruct((t, n), BF16)],
        compiler_params=_cparams(1),
    )(x, g, scale, shift, w)


def _lru_gates(xc, wa_ref, ba, wx_ref, bx, ls):
    xb = xc.astype(BF16)
    ra = _sigmoid(_heads_nn(xb, wa_ref) + ba)
    ia = _sigmoid(_heads_nn(xb, wx_ref) + bx)
    la = LRU_C * ra * ls
    a = jnp.exp(la)
    mult = jnp.sqrt(-jnp.tanh(la) * (1.0 + a * a))
    return ra, ia, a, mult


def _conv4(xr, prev8, cw_ref, cb):
    return (cb + cw_ref[3:4, :] * xr + cw_ref[2:3, :] * _shift_down(xr, 1, prev8)
            + cw_ref[1:2, :] * _shift_down(xr, 2, prev8) + cw_ref[0:1, :] * _shift_down(xr, 3, prev8))


def _rnn_fwd(z, cw, cb, wa, ba, wx, bx, lam):
    t = z.shape[0]
    tm = _tile_seq(t)
    ngrp = tm // SUBLANES

    def body(xr_ref, xp_ref, gr_ref, cw_ref, cb_ref, wa_ref, ba_ref, wx_ref, bx_ref, lam_ref,
             h_ref, ya_ref, carry_ref, a_scr, u_scr):
        i = pl.program_id(0)

        @pl.when(i == 0)
        def _():
            carry_ref[...] = jnp.zeros_like(carry_ref)

        xr = xr_ref[...].astype(F32)
        prev8 = jnp.where(i == 0, 0.0, xp_ref[...].astype(F32)[HALO - SUBLANES:])
        xc = _conv4(xr, prev8, cw_ref, cb_ref[...])
        _, ia, a, mult = _lru_gates(xc, wa_ref, ba_ref[...], wx_ref, bx_ref[...], _log_sigmoid(lam_ref[...]))
        a_scr[...] = a
        u_scr[...] = mult * (ia * xc)
        row = _row_iota(D)

        def grp(j, carry):
            r0 = pl.multiple_of(j * SUBLANES, SUBLANES)
            av = a_scr[pl.ds(r0, SUBLANES), :]
            uv = u_scr[pl.ds(r0, SUBLANES), :]
            for d in (1, 2, 4):
                m = row >= d
                uv = jnp.where(m, av * pltpu.roll(uv, d, 0) + uv, uv)
                av = jnp.where(m, av * pltpu.roll(av, d, 0), av)
            hv = uv + av * carry
            h_ref[pl.ds(r0, SUBLANES), :] = hv
            return hv[SUBLANES - 1:SUBLANES, :]

        carry_ref[0:1, :] = lax.fori_loop(0, ngrp, grp, carry_ref[0:1, :])
        gg, _ = _gelu_t(gr_ref[...].astype(F32))
        ya_ref[...] = (h_ref[...] * gg).astype(BF16)

    vec = _const_spec((1, D))
    wspec = _const_spec((NH, HD, HD))
    return pl.pallas_call(
        body, name="rnn_fwd", grid=(t // tm,),
        in_specs=[pl.BlockSpec((tm, D), lambda i: (i, 0)), pl.BlockSpec((HALO, D), _prev_halo_map(tm, 0)),
                  pl.BlockSpec((tm, D), lambda i: (i, 1)), _const_spec((4, D)), vec, wspec, vec, wspec, vec, vec],
        out_specs=[pl.BlockSpec((tm, D), lambda i: (i, 0)), pl.BlockSpec((tm, D), lambda i: (i, 0))],
        out_shape=[jax.ShapeDtypeStruct((t, D), F32), jax.ShapeDtypeStruct((t, D), BF16)],
        scratch_shapes=[pltpu.VMEM((SUBLANES, D), F32), pltpu.VMEM((tm, D), F32), pltpu.VMEM((tm, D), F32)],
        compiler_params=_cparams(1),
    )(z, z, z, cw, cb, wa, ba, wx, bx, lam)


def _sgu_core(zu, zv, lng, lnb, wm_ref, bst_ref):
    gu, tu = _gelu_t(zu)
    gv, tv = _gelu_t(zv)
    mu = jnp.mean(gv, axis=-1, keepdims=True)
    cen = gv - mu
    rstd = lax.rsqrt(jnp.mean(cen * cen, axis=-1, keepdims=True) + EPS)
    vhat = cen * rstd
    vln = vhat * lng + lnb
    vb = vln.astype(BF16)
    rows = []
    for b0 in range(0, zu.shape[0], HD):
        rows.append(jnp.concatenate(
            [jnp.dot(wm_ref[g], vb[b0:b0 + HD, g * HD:(g + 1) * HD], preferred_element_type=F32)
             + bst_ref[:, g:g + 1] for g in range(NH)], axis=1))
    mixed = jnp.concatenate(rows, axis=0) if len(rows) > 1 else rows[0]
    return gu, tu, tv, rstd, vhat, vb, mixed


def _sgu_fwd(z, lng, lnb, wm, bst):
    t = z.shape[0]
    tm = _tile_seq(t)

    def body(zu_ref, zv_ref, lng_ref, lnb_ref, wm_ref, bst_ref, yb_ref):
        gu, _, _, _, _, _, mixed = _sgu_core(zu_ref[...].astype(F32), zv_ref[...].astype(F32),
                                             lng_ref[...], lnb_ref[...], wm_ref, bst_ref)
        yb_ref[...] = (gu * mixed).astype(BF16)

    vec = _const_spec((1, D))
    return pl.pallas_call(
        body, name="sgu_fwd", grid=(t // tm,),
        in_specs=[pl.BlockSpec((tm, D), lambda i: (i, 2)), pl.BlockSpec((tm, D), lambda i: (i, 3)), vec, vec,
                  _const_spec((NH, HD, HD)), _const_spec((HD, NH))],
        out_specs=pl.BlockSpec((tm, D), lambda i: (i, 0)),
        out_shape=jax.ShapeDtypeStruct((t, D), BF16),
        compiler_params=_cparams(1),
    )(z, z, lng, lnb, wm, bst)


def _merge_fwd(ya_pre, yb_pre, z, x, gate1, wba, wbb, wout):
    t = x.shape[0]
    tm = _tile_big(t)

    def body(yap_ref, ybp_ref, ga_ref, gb_ref, x_ref, g1_ref, wba_ref, wbb_ref, wo_ref,
             x2_ref, ya_ref, yb_ref, mg_ref, o1_ref):
        ya = jnp.dot(yap_ref[...], wba_ref[...], preferred_element_type=F32)
        yb = jnp.dot(ybp_ref[...], wbb_ref[...], preferred_element_type=F32)
        merged = _sigmoid(ga_ref[...].astype(F32)) * ya + _sigmoid(gb_ref[...].astype(F32)) * yb
        mb = merged.astype(BF16)
        o1 = jnp.dot(mb, wo_ref[...], preferred_element_type=F32)
        x2_ref[...] = x_ref[...] + g1_ref[...] * o1
        ya_ref[...] = ya.astype(BF16)
        yb_ref[...] = yb.astype(BF16)
        mg_ref[...] = mb
        o1_ref[...] = o1.astype(BF16)

    tile = pl.BlockSpec((tm, D), lambda i: (i, 0))
    wspec = _const_spec((D, D))
    bshape = jax.ShapeDtypeStruct((t, D), BF16)
    return pl.pallas_call(
        body, name="merge_fwd", grid=(t // tm,),
        in_specs=[tile, tile, pl.BlockSpec((tm, D), lambda i: (i, 4)), pl.BlockSpec((tm, D), lambda i: (i, 5)),
                  tile, _const_spec((1, D)), wspec, wspec, wspec],
        out_specs=[tile] * 5,
        out_shape=[jax.ShapeDtypeStruct((t, D), F32), bshape, bshape, bshape, bshape],
        compiler_params=_cparams(1),
    )(ya_pre, yb_pre, z, z, x, gate1, wba, wbb, wout)


def _conv3(u, prev8, cw_ref, cb):
    return cb + cw_ref[2:3, :] * u + cw_ref[1:2, :] * _shift_down(u, 1, prev8) + cw_ref[0:1, :] * _shift_down(u, 2, prev8)


def _ffn_proj_mid(x2, g, scale, shift, w, cw, cb):
    t = x2.shape[0]
    tm = _tile_seq(t)
    nc = DFF // D

    def body(x_ref, g_ref, sc_ref, sh_ref, w_ref, cw_ref, cb_ref, h_ref, up_ref, ff_ref, fa_ref, fv_ref, prev_ref):
        @pl.when(pl.program_id(0) == 0)
        def _():
            prev_ref[...] = jnp.zeros_like(prev_ref)

        xv = x_ref[...]
        r = lax.rsqrt(jnp.mean(xv * xv, axis=-1, keepdims=True) + EPS)
        hb = ((xv * r * g_ref[...]) * (1.0 + sc_ref[...]) + sh_ref[...]).astype(BF16)
        h_ref[...] = hb
        for c in range(nc):
            halves = []
            for c0 in (c * D, DFF + c * D):
                u = jnp.dot(hb, w_ref[:, c0:c0 + D], preferred_element_type=F32)
                up_ref[:, c0:c0 + D] = u.astype(BF16)
                halves.append(_conv3(u, prev_ref[:, c0:c0 + D], cw_ref[:, c0:c0 + D], cb_ref[:, c0:c0 + D]))
                prev_ref[:, c0:c0 + D] = u[tm - SUBLANES:]
            act, val = halves
            ga, ta = _gelu_t(act)
            cols = slice(c * D, (c + 1) * D)
            ff_ref[:, cols] = (ga * val).astype(BF16)
            fa_ref[:, cols] = (val * _gelu_grad(act, ta)).astype(BF16)
            fv_ref[:, cols] = ga.astype(BF16)

    vec = _const_spec((1, D))
    n = 2 * DFF
    half = pl.BlockSpec((tm, DFF), lambda i: (i, 0))
    hshape = jax.ShapeDtypeStruct((t, DFF), BF16)
    return pl.pallas_call(
        body, name="ffn_proj_mid", grid=(t // tm,),
        in_specs=[pl.BlockSpec((tm, D), lambda i: (i, 0)), vec, vec, vec, _const_spec((D, n), True),
                  _const_spec((3, n)), _const_spec((1, n))],
        out_specs=[pl.BlockSpec((tm, D), lambda i: (i, 0)), pl.BlockSpec((tm, n), lambda i: (i, 0)), half, half, half],
        out_shape=[jax.ShapeDtypeStruct((t, D), BF16), jax.ShapeDtypeStruct((t, n), BF16), hshape, hshape, hshape],
        scratch_shapes=[pltpu.VMEM((SUBLANES, n), F32)],
        compiler_params=_cparams(1),
    )(x2, g, scale, shift, w, cw, cb)


def _ffn_out_loss(ff, wd, x2, target, gate2, gfin):
    t = x2.shape[0]
    tm = _tile_big(t)

    def body(ff_ref, wd_ref, x2_ref, tg_ref, g2_ref, gf_ref, dx3_ref, loss_ref, dgf_ref, dg2_ref):
        @pl.when(pl.program_id(0) == 0)
        def _():
            loss_ref[...] = jnp.zeros_like(loss_ref)
            dgf_ref[...] = jnp.zeros_like(dgf_ref)
            dg2_ref[...] = jnp.zeros_like(dg2_ref)

        o2 = jnp.dot(ff_ref[...], wd_ref[...], preferred_element_type=F32)
        x3 = x2_ref[...] + g2_ref[...] * o2
        r = lax.rsqrt(jnp.mean(x3 * x3, axis=-1, keepdims=True) + EPS)
        xhat = x3 * r
        err = xhat * gf_ref[...] - tg_ref[...]
        loss_ref[...] += 0.5 * jnp.sum(jnp.mean(err * err, axis=-1, keepdims=True), axis=0, keepdims=True)
        dy = err * (1.0 / D)
        dgf_ref[...] += _colsum(dy * xhat)
        dxh = dy * gf_ref[...]
        dx3 = r * (dxh - xhat * jnp.mean(dxh * xhat, axis=-1, keepdims=True))
        dx3_ref[...] = dx3
        dg2_ref[...] += _colsum(dx3 * o2)

    tile = pl.BlockSpec((tm, D), lambda i: (i, 0))
    vec = _const_spec((1, D))
    return pl.pallas_call(
        body, name="ffn_out_loss", grid=(t // tm,),
        in_specs=[pl.BlockSpec((tm, DFF), lambda i: (i, 0)), _const_spec((DFF, D), True), tile, tile, vec, vec],
        out_specs=[tile, _const_spec((1, 1)), vec, vec],
        out_shape=[jax.ShapeDtypeStruct((t, D), F32), jax.ShapeDtypeStruct((1, 1), F32),
                   jax.ShapeDtypeStruct((1, D), F32), jax.ShapeDtypeStruct((1, D), F32)],
        compiler_params=_cparams(1),
    )(ff, wd, x2, target, gate2, gfin)


def _ffn_down_bwd(dx3, gate2, ff, fa, fv, wd):
    t = dx3.shape[0]
    tm = _tile_big(t)
    nc = DFF // D

    def body(dx3_ref, g2_ref, ff_ref, fa_ref, fv_ref, wd_ref, da_ref, dv_ref, dwd_ref, dcba_ref, dcbv_ref):
        @pl.when(pl.program_id(1) == 0)
        def _():
            for r in (dwd_ref, dcba_ref, dcbv_ref):
                r[...] = jnp.zeros_like(r)

        do2 = (dx3_ref[...] * g2_ref[...]).astype(BF16)
        dwd_ref[...] += _dot_tn(ff_ref[...], do2)
        dff = _dot_nt(do2, wd_ref[...])
        dact = dff * fa_ref[...].astype(F32)
        dval = dff * fv_ref[...].astype(F32)
        da_ref[...] = dact.astype(BF16)
        dv_ref[...] = dval.astype(BF16)
        dcba_ref[...] += _colsum(dact)
        dcbv_ref[...] += _colsum(dval)

    blk = pl.BlockSpec((tm, D), lambda c, i: (i, c))
    vec = pl.BlockSpec((1, D), lambda c, i: (0, c))
    return pl.pallas_call(
        body, name="ffn_down_bwd", grid=(nc, t // tm),
        in_specs=[pl.BlockSpec((tm, D), lambda c, i: (i, 0)), pl.BlockSpec((1, D), lambda c, i: (0, 0)),
                  blk, blk, blk, pl.BlockSpec((D, D), lambda c, i: (c, 0))],
        out_specs=[blk, blk, pl.BlockSpec((D, D), lambda c, i: (c, 0)), vec, vec],
        out_shape=[jax.ShapeDtypeStruct((t, DFF), BF16), jax.ShapeDtypeStruct((t, DFF), BF16),
                   jax.ShapeDtypeStruct((DFF, D), F32),
                   jax.ShapeDtypeStruct((1, DFF), F32), jax.ShapeDtypeStruct((1, DFF), F32)],
        compiler_params=_cparams(2),
    )(dx3, gate2, ff, fa, fv, wd)


def _modnorm_bwd(dh, xv, g, scale):
    r = lax.rsqrt(jnp.mean(xv * xv, axis=-1, keepdims=True) + EPS)
    xhat = xv * r
    dxn = dh * (1.0 + scale)
    dxh = dxn * g
    dx = r * (dxh - xhat * jnp.mean(dxh * xhat, axis=-1, keepdims=True))
    return dx, _colsum(dh), _colsum(dh * (xhat * g)), _colsum(dxn * xhat)


def _ffn_up_bwd(dact, dval, up, cw, wup, x2, dx3, gffn, scale2, o1, gate1):
    t = x2.shape[0]
    tm = _tile_seq(t)
    nt = t // tm
    nc = DFF // D

    def body(da_ref, dan_ref, dv_ref, dvn_ref, up_ref, cw_ref, w_ref, x2_ref, dx3_ref, g_ref, sc_ref, o1_ref, g1_ref,
             dup_ref, dx2_ref, do1_ref, dcw_ref, dsh_ref, dsc_ref, dg_ref, dg1_ref):
        i = pl.program_id(0)

        @pl.when(i == 0)
        def _():
            for r in (dcw_ref, dsh_ref, dsc_ref, dg_ref, dg1_ref):
                r[...] = jnp.zeros_like(r)

        last = i == nt - 1
        dh = jnp.zeros((tm, D), F32)
        for half, (d_ref, dn_ref) in enumerate(((da_ref, dan_ref), (dv_ref, dvn_ref))):
            nxt = jnp.where(last, 0.0, dn_ref[...].astype(F32)[:SUBLANES])
            for c in range(nc):
                c0 = half * DFF + c * D
                dv = d_ref[:, c * D:(c + 1) * D].astype(F32)
                nx = nxt[:, c * D:(c + 1) * D]
                taps = (_shift_up(dv, 2, nx), _shift_up(dv, 1, nx), dv)
                dup = (cw_ref[2:3, c0:c0 + D] * taps[2] + cw_ref[1:2, c0:c0 + D] * taps[1]
                       + cw_ref[0:1, c0:c0 + D] * taps[0]).astype(BF16)
                upv = up_ref[:, c0:c0 + D].astype(F32)
                for k in range(3):
                    dcw_ref[k:k + 1, c0:c0 + D] += _colsum(taps[k] * upv)
                dup_ref[:, c0:c0 + D] = dup
                dh = dh + _dot_nt(dup, w_ref[:, c0:c0 + D])
        dxn, dsh, dsc, dg = _modnorm_bwd(dh, x2_ref[...], g_ref[...], sc_ref[...])
        dx2 = dx3_ref[...] + dxn
        dx2_ref[...] = dx2
        do1_ref[...] = (dx2 * g1_ref[...]).astype(BF16)
        dsh_ref[...] += dsh
        dsc_ref[...] += dsc
        dg_ref[...] += dg
        dg1_ref[...] += _colsum(dx2 * o1_ref[...].astype(F32))

    tile = pl.BlockSpec((tm, D), lambda i: (i, 0))
    wide = pl.BlockSpec((tm, DFF), lambda i: (i, 0))
    nxt = pl.BlockSpec((HALO, DFF), lambda i: (jnp.minimum((i + 1) * (tm // HALO), t // HALO - 1), 0))
    vec = _const_spec((1, D))
    vshape = jax.ShapeDtypeStruct((1, D), F32)
    return pl.pallas_call(
        body, name="ffn_up_bwd", grid=(nt,),
        in_specs=[wide, nxt, wide, nxt, pl.BlockSpec((tm, 2 * DFF), lambda i: (i, 0)),
                  _const_spec((3, 2 * DFF)), _const_spec((D, 2 * DFF), True),
                  tile, tile, vec, vec, tile, vec],
        out_specs=[pl.BlockSpec((tm, 2 * DFF), lambda i: (i, 0)), tile, tile, _const_spec((3, 2 * DFF)),
                   vec, vec, vec, vec],
        out_shape=[jax.ShapeDtypeStruct((t, 2 * DFF), BF16), jax.ShapeDtypeStruct((t, D), F32),
                   jax.ShapeDtypeStruct((t, D), BF16), jax.ShapeDtypeStruct((3, 2 * DFF), F32),
                   vshape, vshape, vshape, vshape],
        compiler_params=_cparams(1),
    )(dact, dact, dval, dval, up, cw, wup, x2, dx3, gffn, scale2, o1, gate1)


def _xt_y(a, b, name):
    t, k = a.shape
    n = b.shape[1]
    tm = min(1024, t)
    bn = 768 if n % 768 == 0 else D

    def body(a_ref, b_ref, o_ref):
        @pl.when(pl.program_id(1) == 0)
        def _():
            o_ref[...] = jnp.zeros_like(o_ref)

        o_ref[...] += _dot_tn(a_ref[...], b_ref[...])

    return pl.pallas_call(
        body, name=name, grid=(n // bn, t // tm),
        in_specs=[pl.BlockSpec((tm, k), lambda j, i: (i, 0)), pl.BlockSpec((tm, bn), lambda j, i: (i, j))],
        out_specs=pl.BlockSpec((k, bn), lambda j, i: (0, j)),
        out_shape=jax.ShapeDtypeStruct((k, n), F32),
        compiler_params=_cparams(2),
    )(a, b)


def _acc_spec(shape, index):
    return pl.BlockSpec(shape, lambda *_: index, pipeline_mode=pl.Buffered(1))


def _out_bwd(do1, wout, merged, ya, yb, z, h1):
    t = do1.shape[0]
    tm = _tile_big(t)

    def body(do1_ref, wo_ref, mg_ref, ya_ref, yb_ref, ga_ref, gb_ref, h1_ref,
             dya_ref, dyb_ref, dz_ref, dwo_ref, dwin_ref):
        @pl.when(pl.program_id(0) == 0)
        def _():
            dwo_ref[...] = jnp.zeros_like(dwo_ref)
            dwin_ref[...] = jnp.zeros_like(dwin_ref)

        do1v = do1_ref[...]
        dwo_ref[...] += _dot_tn(mg_ref[...], do1v)
        dm = _dot_nt(do1v, wo_ref[...])
        sa = _sigmoid(ga_ref[...].astype(F32))
        sb = _sigmoid(gb_ref[...].astype(F32))
        dya_ref[...] = (dm * sa).astype(BF16)
        dyb_ref[...] = (dm * sb).astype(BF16)
        dga = (dm * ya_ref[...].astype(F32) * sa * (1.0 - sa)).astype(BF16)
        dgb = (dm * yb_ref[...].astype(F32) * sb * (1.0 - sb)).astype(BF16)
        dz_ref[:, 0:D] = dga
        dz_ref[:, D:2 * D] = dgb
        h1v = h1_ref[...]
        dwin_ref[:, 0:D] += _dot_tn(h1v, dga)
        dwin_ref[:, D:2 * D] += _dot_tn(h1v, dgb)

    tile = pl.BlockSpec((tm, D), lambda i: (i, 0))
    bshape = jax.ShapeDtypeStruct((t, D), BF16)
    return pl.pallas_call(
        body, name="out_bwd", grid=(t // tm,),
        in_specs=[tile, _const_spec((D, D), True), tile, tile, tile,
                  pl.BlockSpec((tm, D), lambda i: (i, 4)), pl.BlockSpec((tm, D), lambda i: (i, 5)), tile],
        out_specs=[tile, tile, pl.BlockSpec((tm, 2 * D), lambda i: (i, 2)), _acc_spec((D, D), (0, 0)),
                   _acc_spec((D, 2 * D), (0, 2))],
        out_shape=[bshape, bshape, jax.ShapeDtypeStruct((t, NCOL_IN), BF16), jax.ShapeDtypeStruct((D, D), F32),
                   jax.ShapeDtypeStruct((D, NCOL_IN), F32)],
        compiler_params=_cparams(1),
    )(do1, wout, merged, ya, yb, z, z, h1)


def _rnn_bwd(dya, ya_pre, wba, h1, z, h, dz, dwin, cw, cb, wa, ba, wx, bx, lam):
    t = z.shape[0]
    tm = _tile_seq(t)
    nt = t // tm
    ngrp = tm // SUBLANES
    hpt = tm // HALO

    def body(dya_ref, yap_ref, wba_ref, h1_ref, xr_ref, xp_ref, gr_ref, h_ref, hp_ref, dz_any, dwin_any,
             cw_ref, cb_ref, wa_ref, ba_ref, wx_ref, bx_ref, lam_ref,
             dz_ref, dwin_ref, dwba_ref, dcw_ref, dcb_ref, dwa_ref, dba_ref, dwx_ref, dbx_ref, dlam_ref,
             a_first, g_first, dxc_first, b_scr, d_scr, g_scr):
        del dz_any, dwin_any
        i = pl.program_id(0)

        @pl.when(i == 0)
        def _():
            for r in (dwin_ref, dwba_ref, dcw_ref, dcb_ref, dwa_ref, dba_ref, dwx_ref, dbx_ref, dlam_ref,
                      a_first, g_first, dxc_first):
                r[...] = jnp.zeros_like(r)

        dya_v = dya_ref[...]
        dwba_ref[...] += _dot_tn(yap_ref[...], dya_v)
        dyap_v = _dot_nt(dya_v, wba_ref[...])
        h1v = h1_ref[...]

        first_tile = i == nt - 1
        xr = xr_ref[...].astype(F32)
        prev8 = jnp.where(first_tile, 0.0, xp_ref[...].astype(F32)[HALO - SUBLANES:])
        xc = _conv4(xr, prev8, cw_ref, cb_ref[...])
        lam_v = lam_ref[...]
        ls = _log_sigmoid(lam_v)
        ra, ia, a, mult = _lru_gates(xc, wa_ref, ba_ref[...], wx_ref, bx_ref[...], ls)
        hv = h_ref[...]
        hprev8 = jnp.where(first_tile, 0.0, hp_ref[...][HALO - SUBLANES:])
        h_prev = _shift_down(hv, 1, hprev8)
        grv = gr_ref[...].astype(F32)
        gg, tg = _gelu_t(grv)
        dgr = (dyap_v * hv * _gelu_grad(grv, tg)).astype(BF16)
        dz_ref[:, D:2 * D] = dgr
        dwin_ref[:, D:2 * D] += _dot_tn(h1v, dgr)

        b_scr[...] = _shift_up(a, 1, a_first[...])
        d_scr[...] = dyap_v * gg
        row = _row_iota(D)

        def grp(jj, carry):
            r0 = pl.multiple_of((ngrp - 1 - jj) * SUBLANES, SUBLANES)
            bv = b_scr[pl.ds(r0, SUBLANES), :]
            dv = d_scr[pl.ds(r0, SUBLANES), :]
            for d in (1, 2, 4):
                m = row < SUBLANES - d
                dv = jnp.where(m, dv + bv * pltpu.roll(dv, SUBLANES - d, 0), dv)
                bv = jnp.where(m, bv * pltpu.roll(bv, SUBLANES - d, 0), bv)
            gv = dv + bv * carry
            g_scr[pl.ds(r0, SUBLANES), :] = gv
            return gv[0:1, :]

        lax.fori_loop(0, ngrp, grp, g_first[0:1, :])
        g = g_scr[...]
        a_first[...] = a[:SUBLANES]
        g_first[...] = g[:SUBLANES]

        da = g * h_prev
        gx = g * xc
        dmult = gx * ia
        dia = gx * mult
        dxc = g * (mult * ia)
        dla = da * a - dmult * (a * a) / mult
        dra = dla * (LRU_C * ls)
        dlam_ref[...] += _colsum(dla * ra) * (LRU_C * _sigmoid(-lam_v))
        dpa = dra * ra * (1.0 - ra)
        dpx = dia * ia * (1.0 - ia)
        dba_ref[...] += _colsum(dpa)
        dbx_ref[...] += _colsum(dpx)
        dpab = dpa.astype(BF16)
        dpxb = dpx.astype(BF16)
        xcb = xc.astype(BF16)
        for hd in range(NH):
            sl = slice(hd * HD, (hd + 1) * HD)
            dwa_ref[hd] += _dot_tn(xcb[:, sl], dpab[:, sl])
            dwx_ref[hd] += _dot_tn(xcb[:, sl], dpxb[:, sl])
        dxc = dxc + _heads_nt(dpab, wa_ref) + _heads_nt(dpxb, wx_ref)

        nxt = dxc_first[...]
        dxr = (cw_ref[3:4, :] * dxc + cw_ref[2:3, :] * _shift_up(dxc, 1, nxt)
               + cw_ref[1:2, :] * _shift_up(dxc, 2, nxt) + cw_ref[0:1, :] * _shift_up(dxc, 3, nxt))
        dxrb = dxr.astype(BF16)
        dz_ref[:, 0:D] = dxrb
        dwin_ref[:, 0:D] += _dot_tn(h1v, dxrb)
        dxc_first[...] = dxc[:SUBLANES]
        dcb_ref[...] += _colsum(dxc)
        for k in range(4):
            dcw_ref[k:k + 1, :] += _colsum(dxc * _shift_down(xr, 3 - k, prev8))

    def rev(col):
        return lambda i: (nt - 1 - i, col)

    def rev_halo(col):
        return lambda i: (jnp.maximum((nt - 1 - i) * hpt - 1, 0), col)

    vec = _const_spec((1, D))
    wspec = _const_spec((NH, HD, HD))
    vshape = jax.ShapeDtypeStruct((1, D), F32)
    wshape = jax.ShapeDtypeStruct((NH, HD, HD), F32)
    any_spec = pl.BlockSpec(memory_space=pl.ANY)
    tile = pl.BlockSpec((tm, D), rev(0))
    outs = pl.pallas_call(
        body, name="rnn_bwd", grid=(nt,),
        in_specs=[tile, tile, _const_spec((D, D), True), tile,
                  tile, pl.BlockSpec((HALO, D), rev_halo(0)),
                  pl.BlockSpec((tm, D), rev(1)), tile, pl.BlockSpec((HALO, D), rev_halo(0)),
                  any_spec, any_spec,
                  _const_spec((4, D)), vec, wspec, vec, wspec, vec, vec],
        out_specs=[pl.BlockSpec((tm, 2 * D), rev(0)), _acc_spec((D, 2 * D), (0, 0)), _acc_spec((D, D), (0, 0)),
                   _const_spec((4, D)), vec, wspec, vec, wspec, vec, vec],
        out_shape=[jax.ShapeDtypeStruct((t, NCOL_IN), BF16), jax.ShapeDtypeStruct((D, NCOL_IN), F32),
                   jax.ShapeDtypeStruct((D, D), F32), jax.ShapeDtypeStruct((4, D), F32), vshape,
                   wshape, vshape, wshape, vshape, vshape],
        scratch_shapes=[pltpu.VMEM((SUBLANES, D), F32), pltpu.VMEM((SUBLANES, D), F32), pltpu.VMEM((SUBLANES, D), F32),
                        pltpu.VMEM((tm, D), F32), pltpu.VMEM((tm, D), F32), pltpu.VMEM((tm, D), F32)],
        input_output_aliases={9: 0, 10: 1},
        compiler_params=_cparams(1),
    )(dya, ya_pre, wba, h1, z, z, z, h, h, dz, dwin, cw, cb, wa, ba, wx, bx, lam)
    return outs


def _sgu_bwd(dyb, yb_pre, wbb, h1, z, dz, dwin, lng, lnb, wm, wmt, bst, mask):
    t = z.shape[0]
    tm = _tile_big(t)

    def body(dyb_ref, ybp_ref, wbb_ref, h1_ref, zu_ref, zv_ref, dz_any, dwin_any,
             lng_ref, lnb_ref, wm_ref, wmt_ref, bst_ref, mask_ref,
             dz_ref, dwin_ref, dwbb_ref, dws_ref, dbst_ref, dlng_ref, dlnb_ref):
        del dz_any, dwin_any

        @pl.when(pl.program_id(0) == 0)
        def _():
            for r in (dwin_ref, dwbb_ref, dws_ref, dbst_ref, dlng_ref, dlnb_ref):
                r[...] = jnp.zeros_like(r)

        zu = zu_ref[...].astype(F32)
        zv = zv_ref[...].astype(F32)
        lng_v = lng_ref[...]
        gu, tu, tv, rstd, vhat, vb, mixed = _sgu_core(zu, zv, lng_v, lnb_ref[...], wm_ref, bst_ref)
        dyb_v = dyb_ref[...]
        dwbb_ref[...] += _dot_tn(ybp_ref[...], dyb_v)
        dyb = _dot_nt(dyb_v, wbb_ref[...])
        h1v = h1_ref[...]
        dzu = (dyb * mixed * _gelu_grad(zu, tu)).astype(BF16)
        dz_ref[:, 0:D] = dzu
        dwin_ref[:, 0:D] += _dot_tn(h1v, dzu)
        dmix = dyb * gu
        dmb = dmix.astype(BF16)
        rows = []
        lane = lax.broadcasted_iota(jnp.int32, (HD, NH), 1)
        dbst = jnp.zeros((HD, NH), F32)
        for b0 in range(0, tm, HD):
            cols = []
            for g in range(NH):
                sl = slice(g * HD, (g + 1) * HD)
                dmg = dmb[b0:b0 + HD, sl]
                dws_ref[g] += _dot_nt(dmg, vb[b0:b0 + HD, sl]) * mask_ref[...]
                cols.append(jnp.dot(wmt_ref[g], dmg, preferred_element_type=F32))
                dbst = dbst + jnp.where(lane == g, jnp.sum(dmix[b0:b0 + HD, sl], axis=1, keepdims=True), 0.0)
            rows.append(jnp.concatenate(cols, axis=1))
        dbst_ref[...] += dbst
        dvln = jnp.concatenate(rows, axis=0) if len(rows) > 1 else rows[0]
        dlng_ref[...] += _colsum(dvln * vhat)
        dlnb_ref[...] += _colsum(dvln)
        dvh = dvln * lng_v
        dgv = rstd * (dvh - jnp.mean(dvh, axis=-1, keepdims=True)
                      - vhat * jnp.mean(dvh * vhat, axis=-1, keepdims=True))
        dzv = (dgv * _gelu_grad(zv, tv)).astype(BF16)
        dz_ref[:, D:2 * D] = dzv
        dwin_ref[:, D:2 * D] += _dot_tn(h1v, dzv)

    vec = _const_spec((1, D))
    wspec = _const_spec((NH, HD, HD))
    vshape = jax.ShapeDtypeStruct((1, D), F32)
    tile = pl.BlockSpec((tm, D), lambda i: (i, 0))
    any_spec = pl.BlockSpec(memory_space=pl.ANY)
    return pl.pallas_call(
        body, name="sgu_bwd", grid=(t // tm,),
        in_specs=[tile, tile, _const_spec((D, D), True), tile,
                  pl.BlockSpec((tm, D), lambda i: (i, 2)), pl.BlockSpec((tm, D), lambda i: (i, 3)), any_spec, any_spec,
                  vec, vec, wspec, wspec, _const_spec((HD, NH)), _const_spec((HD, HD))],
        out_specs=[pl.BlockSpec((tm, 2 * D), lambda i: (i, 1)), _acc_spec((D, 2 * D), (0, 1)), _acc_spec((D, D), (0, 0)),
                   wspec, _const_spec((HD, NH)), vec, vec],
        out_shape=[jax.ShapeDtypeStruct((t, NCOL_IN), BF16), jax.ShapeDtypeStruct((D, NCOL_IN), F32),
                   jax.ShapeDtypeStruct((D, D), F32), jax.ShapeDtypeStruct((NH, HD, HD), F32),
                   jax.ShapeDtypeStruct((HD, NH), F32), vshape, vshape],
        input_output_aliases={6: 0, 7: 1},
        compiler_params=_cparams(1),
    )(dyb, yb_pre, wbb, h1, z, z, dz, dwin, lng, lnb, wm, wmt, bst, mask)


def _in_bwd(dz, win, x, dx2, g, scale1):
    t = x.shape[0]
    tm = _tile_big(t)

    def body(dz_ref, w_ref, x_ref, dx2_ref, g_ref, sc_ref, dx_ref, dsh_ref, dsc_ref, dg_ref):
        @pl.when(pl.program_id(0) == 0)
        def _():
            for r in (dsh_ref, dsc_ref, dg_ref):
                r[...] = jnp.zeros_like(r)

        dh = jnp.zeros((tm, D), F32)
        for c0 in range(0, NCOL_IN, D):
            dh = dh + _dot_nt(dz_ref[:, c0:c0 + D], w_ref[:, c0:c0 + D])
        dxn, dsh, dsc, dg = _modnorm_bwd(dh, x_ref[...], g_ref[...], sc_ref[...])
        dx_ref[...] = dx2_ref[...] + dxn
        dsh_ref[...] += dsh
        dsc_ref[...] += dsc
        dg_ref[...] += dg

    tile = pl.BlockSpec((tm, D), lambda i: (i, 0))
    vec = _const_spec((1, D))
    vshape = jax.ShapeDtypeStruct((1, D), F32)
    return pl.pallas_call(
        body, name="in_bwd", grid=(t // tm,),
        in_specs=[pl.BlockSpec((tm, NCOL_IN), lambda i: (i, 0)), _const_spec((D, NCOL_IN), True), tile, tile, vec, vec],
        out_specs=[tile, vec, vec, vec],
        out_shape=[jax.ShapeDtypeStruct((t, D), F32), vshape, vshape, vshape],
        compiler_params=_cparams(1),
    )(dz, win, x, dx2, g, scale1)


def _mod_cols(c_all, w_ada, b_cols):
    nb, cols = c_all.shape[0], w_ada.shape[1]

    def body(c_ref, w_ref, b_ref, o_ref):
        cv = c_ref[...]
        ca = (cv * _sigmoid(cv)).astype(BF16)
        o_ref[...] = jnp.dot(ca, w_ref[...].astype(BF16), preferred_element_type=F32) + b_ref[...]

    return pl.pallas_call(body, name="mod_cols", out_shape=jax.ShapeDtypeStruct((nb, cols), F32))(c_all, w_ada, b_cols)


def _ada_grad(c_all, dmod_cols):
    cols = dmod_cols.shape[1]

    def body(c_ref, d_ref, o_ref):
        cv = c_ref[...]
        ca = (cv * _sigmoid(cv)).astype(BF16)
        o_ref[...] = _dot_tn(ca, d_ref[...].astype(BF16))

    return pl.pallas_call(body, name="ada_grad", out_shape=jax.ShapeDtypeStruct((D, cols), F32))(c_all, dmod_cols)


def _adamw(w, m, v, parts, name):
    rows, cols = w.shape
    tr = _row_tile(rows)
    stacked = [p.ndim == 3 for p in parts]
    bc1 = 1.0 - ADAM_B1 ** ADAM_STEP
    bc2 = 1.0 - ADAM_B2 ** ADAM_STEP

    def body(*refs):
        w_ref, m_ref, v_ref = refs[:3]
        p_refs = refs[3:3 + len(parts)]
        g_ref, d_ref, mo_ref, vo_ref = refs[3 + len(parts):]
        g = None
        for p_ref, st in zip(p_refs, stacked):
            terms = [p_ref[k].astype(F32) for k in range(p_ref.shape[0])] if st else [p_ref[...].astype(F32)]
            for term in terms:
                g = term if g is None else g + term
        mn = ADAM_B1 * m_ref[...] + (1.0 - ADAM_B1) * g
        vn = ADAM_B2 * v_ref[...] + (1.0 - ADAM_B2) * (g * g)
        g_ref[...] = g
        mo_ref[...] = mn
        vo_ref[...] = vn
        d_ref[...] = -ADAM_LR * ((mn / bc1) / (jnp.sqrt(vn / bc2) + ADAM_EPS) + ADAM_WD * w_ref[...])

    tile = pl.BlockSpec((tr, cols), lambda i: (i, 0))
    p_specs = [pl.BlockSpec((p.shape[0], tr, cols), lambda i: (0, i, 0)) if st else tile for p, st in zip(parts, stacked)]
    shp = jax.ShapeDtypeStruct((rows, cols), F32)
    return pl.pallas_call(
        body, name=name, grid=(rows // tr,),
        in_specs=[tile, tile, tile] + p_specs, out_specs=[tile] * 4, out_shape=[shp] * 4,
        compiler_params=_cparams(1),
    )(w, m, v, *parts)


def _mesh_pos():
    return lax.axis_index("x"), lax.axis_index("y"), lax.axis_index("c")


def _other_chips(x, y):
    return [(1 - x, y), (x, 1 - y), (1 - x, 1 - y)]


def _block_of(ref, axis, index, size):
    if axis == 0:
        return ref.at[index]
    return ref.at[:, pl.ds(pl.multiple_of(index * size, 128), size)]


def _all_gather(shards, axes, name):
    n = len(shards)
    per = 7

    def body(*refs):
        ins, outs, done = refs[:n], refs[n:2 * n], refs[2 * n]
        send_sems, recv_sems, local_sems = refs[2 * n + 1:]
        x, y, c = _mesh_pos()
        me, sibling = (x, y, c), (x, y, 1 - c)
        chips = _other_chips(x, y)

        def rows(a, pos):
            return _block_of(outs[a], axes[a], 4 * pos[0] + 2 * pos[1] + pos[2], shards[a].shape[-1])

        def copy(a, k, block, to, src=None):
            return pltpu.make_async_remote_copy(
                src_ref=rows(a, block) if src is None else src, dst_ref=rows(a, block),
                send_sem=send_sems.at[a * per + k], recv_sem=recv_sems.at[a * per + k],
                device_id=to, device_id_type=MESH_IDS)

        mine = [pltpu.make_async_copy(ins[a], rows(a, me), local_sems.at[a]) for a in range(n)]
        for cp in mine:
            cp.start()
        first = []
        for a in range(n):
            first.append(copy(a, 0, me, sibling, src=ins[a]))
            first += [copy(a, 1 + j, me, (*chip, c), src=ins[a]) for j, chip in enumerate(chips)]
        for cp in first:
            cp.start()
        passed = []
        for j, chip in enumerate(chips):
            for a in range(n):
                copy(a, 1 + j, (*chip, c), me).wait_recv()
                fwd = copy(a, 4 + j, (*chip, c), sibling)
                fwd.start()
                passed.append(fwd)
        for a in range(n):
            copy(a, 0, sibling, me).wait_recv()
            for j, chip in enumerate(chips):
                copy(a, 4 + j, (*chip, 1 - c), me).wait_recv()
        for cp in first + passed:
            cp.wait_send()
        for cp in mine:
            cp.wait()
        done[...] = jnp.zeros_like(done)

    def full_shape(s, ax):
        return (N_DEV,) + s.shape if ax == 0 else s.shape[:-1] + (N_DEV * s.shape[-1],)

    any_spec = pl.BlockSpec(memory_space=pl.ANY)
    outs = pl.pallas_call(
        body, name=name,
        in_specs=[any_spec] * n, out_specs=[any_spec] * n + [pl.BlockSpec(memory_space=pltpu.VMEM)],
        out_shape=[jax.ShapeDtypeStruct(full_shape(s, ax), s.dtype) for s, ax in zip(shards, axes)]
        + [jax.ShapeDtypeStruct((SUBLANES, LANES), F32)],
        scratch_shapes=[pltpu.SemaphoreType.DMA((n * per,)), pltpu.SemaphoreType.DMA((n * per,)),
                        pltpu.SemaphoreType.DMA((n,))],
    )(*shards)
    return outs[:n], outs[n]


def _chip_blocks(x, y):
    return [(x, y)] + _other_chips(x, y)


def _sibling_reduce(gs, axis, name):
    g0, n = gs[0], len(gs)
    rows, cols = (g0.shape[1], g0.shape[2]) if axis == 0 else (g0.shape[0], g0.shape[1] // N_DEV)
    chunk = math.gcd(rows, 64)

    def body(*refs):
        g_refs, own_refs, pay_refs = refs[:n], refs[n:2 * n], refs[2 * n:3 * n]
        send_buf, keep_buf, recv_buf, send_sems, recv_sems, stage_sems, keep_sems = refs[3 * n:]
        x, y, c = _mesh_pos()
        sibling = (x, y, 1 - c)
        chips = _chip_blocks(x, y)
        stage, keep, push = [], [], []
        for a in range(n):
            for j, (px, py) in enumerate(chips):
                s = 4 * a + j
                theirs = _block_of(g_refs[a], axis, 4 * px + 2 * py + (1 - c), cols)
                ours = _block_of(g_refs[a], axis, 4 * px + 2 * py + c, cols)
                stage.append(pltpu.make_async_copy(theirs, send_buf.at[s], stage_sems.at[s]))
                keep.append(pltpu.make_async_copy(ours, keep_buf.at[s], keep_sems.at[s]))
                push.append(pltpu.make_async_remote_copy(
                    src_ref=send_buf.at[s], dst_ref=recv_buf.at[s], send_sem=send_sems.at[s],
                    recv_sem=recv_sems.at[s], device_id=sibling, device_id_type=MESH_IDS))
        for cp in stage + keep:
            cp.start()
        for s in range(4 * n):
            stage[s].wait()
            push[s].start()
        for s in range(4 * n):
            push[s].wait_recv()
            keep[s].wait()
            a, j = divmod(s, 4)
            dst = own_refs[a] if j == 0 else pay_refs[a].at[j - 1]

            def add(r, carry, s=s, dst=dst):
                sl = pl.ds(pl.multiple_of(r * chunk, chunk), chunk)
                dst[sl, :] = (keep_buf[s, sl, :] + recv_buf[s, sl, :]).astype(dst.dtype)
                return carry

            lax.fori_loop(0, rows // chunk, add, 0)
        for cp in push:
            cp.wait_send()

    vmem = pl.BlockSpec(memory_space=pltpu.VMEM)
    buf = pltpu.VMEM((4 * n, rows, cols), F32)
    sems = pltpu.SemaphoreType.DMA((4 * n,))
    outs = pl.pallas_call(
        body, name=name,
        in_specs=[pl.BlockSpec(memory_space=pl.ANY)] * n, out_specs=[vmem] * (2 * n),
        out_shape=[jax.ShapeDtypeStruct((rows, cols), F32)] * n + [jax.ShapeDtypeStruct((3, rows, cols), BF16)] * n,
        scratch_shapes=[buf, buf, buf, sems, sems, sems, sems],
        compiler_params=pltpu.CompilerParams(vmem_limit_bytes=VMEM_LIMIT),
    )(*gs)
    return list(zip(outs[:n], outs[n:]))


_HBM_SPEC = pl.BlockSpec(memory_space=pltpu.HBM)
_SEM_SPEC = pl.BlockSpec(memory_space=pltpu.SEMAPHORE)
_SIDE_EFFECT = pltpu.SideEffectType.DATAFLOW_SIDE_EFFECTING


def _exchange_start(name, srcs, lands, plan, n_copies):
    nb = len(srcs) + len(lands)

    def body(*refs):
        bufs, send_sems, recv_sems, token = refs[:nb], refs[nb], refs[nb + 1], refs[-1]
        for cp in plan(bufs[:len(srcs)], bufs[len(srcs):], send_sems, recv_sems):
            cp.start()
        token[...] = jnp.zeros_like(token)

    arrays = list(srcs) + list(lands)
    outs = pl.pallas_call(
        body, name=name,
        out_shape=(pltpu.SemaphoreType.DMA((n_copies,)), pltpu.SemaphoreType.DMA((n_copies,)),
                   *[pltpu.HBM(a.shape, a.dtype) for a in arrays], jax.ShapeDtypeStruct((SUBLANES, LANES), F32)),
        in_specs=[_HBM_SPEC] * nb,
        out_specs=(_SEM_SPEC, _SEM_SPEC, *[_HBM_SPEC] * nb, pl.BlockSpec(memory_space=pltpu.VMEM)),
        input_output_aliases={k: 2 + k for k in range(nb)},
        compiler_params=pltpu.CompilerParams(has_side_effects=_SIDE_EFFECT),
    )(*[pltpu.with_memory_space_constraint(a, pltpu.HBM) for a in arrays])
    return outs[0], outs[1], outs[2:2 + len(srcs)], outs[2 + len(srcs):2 + nb], outs[-1]


def _exchange_wait(name, send_sems, recv_sems, srcs, lands, plan, after):
    nb = len(srcs) + len(lands)
    after = list(after)

    def body(*refs):
        bufs, send_ref, recv_ref = refs[:nb], refs[nb], refs[nb + 1]
        for cp in plan(bufs[:len(srcs)], bufs[len(srcs):], send_ref, recv_ref):
            cp.wait_send()
            cp.wait_recv()

    arrays = list(srcs) + list(lands)
    outs = pl.pallas_call(
        body, name=name,
        out_shape=tuple(pltpu.HBM(a.shape, a.dtype) for a in arrays),
        in_specs=[_HBM_SPEC] * nb + [_SEM_SPEC, _SEM_SPEC] + [pl.BlockSpec(memory_space=pl.ANY)] * len(after),
        out_specs=tuple([_HBM_SPEC] * nb),
        input_output_aliases={k: k for k in range(nb)},
        compiler_params=pltpu.CompilerParams(has_side_effects=_SIDE_EFFECT),
    )(*arrays, send_sems, recv_sems, *after)
    return outs[len(srcs):]


def _gather_plan(axes, sizes):
    def plan(src_refs, land_refs, send_sems, recv_sems):
        x, y, c = _mesh_pos()
        copies = []
        for a, (src, land) in enumerate(zip(src_refs, land_refs)):
            mine = _block_of(land, axes[a], 4 * x + 2 * y + c, sizes[a])
            for k in range(1, N_DEV):
                peer = (1 - x if k & 4 else x, 1 - y if k & 2 else y, 1 - c if k & 1 else c)
                idx = a * (N_DEV - 1) + k - 1
                copies.append(pltpu.make_async_remote_copy(
                    src_ref=src, dst_ref=mine, send_sem=send_sems.at[idx], recv_sem=recv_sems.at[idx],
                    device_id=peer, device_id_type=MESH_IDS))
        return copies
    return plan


def _chip_plan(src_refs, land_refs, send_sems, recv_sems):
    x, y, c = _mesh_pos()
    copies = []
    for a, (src, land) in enumerate(zip(src_refs, land_refs)):
        for j, chip in enumerate(_other_chips(x, y)):
            copies.append(pltpu.make_async_remote_copy(
                src_ref=src.at[j], dst_ref=land.at[j], send_sem=send_sems.at[3 * a + j],
                recv_sem=recv_sems.at[3 * a + j], device_id=(*chip, c), device_id_type=MESH_IDS))
    return copies


def _own_block_placed(shard, axis, me):
    if axis == 0:
        full = lax.empty((N_DEV,) + shard.shape, shard.dtype)
        return lax.dynamic_update_slice(full, shard[None], (me,) + (0,) * shard.ndim)
    full = lax.empty(shard.shape[:-1] + (N_DEV * shard.shape[-1],), shard.dtype)
    return lax.dynamic_update_slice(full, shard, (0,) * (shard.ndim - 1) + (me * shard.shape[-1],))


def _local_step(x, target, mod, win, late_weights, p, grads_ready=None):
    shift1, scale1, gate1, shift2, scale2, gate2 = (mod[k] for k in range(6))

    def after_token(v, token):
        return v if token is None else v + token[0:1, 0:1]
    wa, wx = p["lru_w_a"].astype(BF16), p["lru_w_x"].astype(BF16)
    mask = jnp.tril(jnp.ones((HD, HD), F32))
    wm = (p["sgu_w_s"] * mask).astype(BF16)
    wmt = jnp.swapaxes(wm, 1, 2)
    bst = jnp.transpose(p["sgu_b_s"])

    h1, z = _norm_proj(x, p["norm_mix_g"], scale1, shift1, win, "mix_proj")
    hstate, ya_pre = _rnn_fwd(z, p["rnn_conv_w"], p["rnn_conv_b"], wa, p["lru_b_a"], wx, p["lru_b_x"], p["lru_lambda"])
    yb_pre = _sgu_fwd(z, p["sgu_ln_g"], p["sgu_ln_b"], wm, bst)
    wba, wbb, wout = late_weights("merge", [ya_pre, yb_pre])
    x2, ya, yb, merged, o1 = _merge_fwd(ya_pre, yb_pre, z, x, gate1, wba, wbb, wout)
    wup = late_weights("ffn_up", [x2])
    h2, up, ff, fa, fv = _ffn_proj_mid(x2, p["norm_ffn_g"], scale2, shift2, wup, p["ffn_conv_w"], p["ffn_conv_b"])
    wd = late_weights("ffn_down", [ff])
    dx3, loss, d_gfin, d_gate2 = _ffn_out_loss(ff, wd, x2, target, gate2, p["norm_final_g"])

    dact, dval, d_wd, dcb_a, dcb_v = _ffn_down_bwd(dx3, gate2, ff, fa, fv, wd)
    dup, dx2, do1, d_cwf, d_shift2, d_scale2, d_gffn, d_gate1 = _ffn_up_bwd(
        dact, dval, up, p["ffn_conv_w"], wup, x2, dx3, p["norm_ffn_g"], scale2, o1, gate1)
    d_wup = _xt_y(h2, dup, "w_up_grad")
    token = grads_ready("ffn", {"w_up": d_wup, "w_down": d_wd}) if grads_ready else None

    dya, dyb, dz, d_wout, d_win = _out_bwd(do1, wout, merged, ya, yb, z, h1)
    dz, d_win, d_wba, d_cw, d_cb, d_wa, d_ba, d_wx, d_bx, d_lam = _rnn_bwd(
        dya, ya_pre, wba, h1, z, hstate, dz, d_win, p["rnn_conv_w"], p["rnn_conv_b"], wa, p["lru_b_a"], wx,
        p["lru_b_x"], after_token(p["lru_lambda"], token))
    dz, d_win, d_wbb, d_ws, d_bst, d_lng, d_lnb = _sgu_bwd(
        dyb, yb_pre, wbb, h1, z, dz, d_win, p["sgu_ln_g"], p["sgu_ln_b"], wm, wmt, bst, mask)
    mixer = {"w_in": d_win, "w_out": d_wout, "w_branch_a": d_wba, "w_branch_b": d_wbb}
    token = grads_ready("mixer", mixer) if grads_ready else None
    grad_x, d_shift1, d_scale1, d_gmix = _in_bwd(dz, win, x, dx2, after_token(p["norm_mix_g"], token), scale1)

    small = {
        "norm_mix_g": d_gmix, "rnn_conv_w": d_cw, "rnn_conv_b": d_cb, "lru_w_a": d_wa, "lru_b_a": d_ba,
        "lru_w_x": d_wx, "lru_b_x": d_bx, "lru_lambda": d_lam, "sgu_ln_g": d_lng, "sgu_ln_b": d_lnb,
        "sgu_w_s": d_ws, "sgu_b_s": jnp.transpose(d_bst), "norm_ffn_g": d_gffn,
        "ffn_conv_w": d_cwf, "ffn_conv_b": jnp.concatenate([dcb_a, dcb_v], axis=1),
        "norm_final_g": d_gfin,
    }
    dmod = jnp.stack([d_shift1, d_scale1, d_gate1, d_shift2, d_scale2, d_gate2])
    big = {"w_in": d_win, "w_up": d_wup, "w_branch_a": d_wba, "w_branch_b": d_wbb, "w_out": d_wout, "w_down": d_wd}
    return loss, grad_x, big, small, dmod


REPLICATED = ["b_ada", "norm_mix_g", "rnn_conv_b", "lru_w_a", "lru_b_a", "lru_w_x", "lru_b_x", "lru_lambda",
              "sgu_ln_g", "sgu_ln_b", "sgu_w_s", "sgu_b_s", "norm_ffn_g", "ffn_conv_b", "norm_final_g"]
COL_SHARDED = ["rnn_conv_w", "ffn_conv_w"]
SMALL_NAMES = REPLICATED + COL_SHARDED
BIG_NAMES = ["w_in", "w_up", "w_branch_a", "w_branch_b", "w_out", "w_down"]
BIG_AXES = [1, 1, 0, 0, 0, 0]
WEIGHTS = ["w_ada", "b_ada", "norm_mix_g", "w_in", "rnn_conv_w", "rnn_conv_b", "lru_w_a", "lru_b_a", "lru_w_x",
           "lru_b_x", "lru_lambda", "sgu_ln_g", "sgu_ln_b", "sgu_w_s", "sgu_b_s", "w_branch_a", "w_branch_b",
           "w_out", "norm_ffn_g", "w_up", "ffn_conv_w", "ffn_conv_b", "w_down", "norm_final_g"]
LANES = 128


def _pack_rows(shape):
    return math.prod(shape) // LANES


def _pack(arrays):
    return jnp.concatenate([a.reshape(-1, LANES) for a in arrays], axis=0)


def _unpack(packed, shapes):
    out, r0 = [], 0
    for s in shapes:
        nrow = math.prod(s) // LANES
        out.append(packed[r0:r0 + nrow].reshape(s))
        r0 += nrow
    return out


def kernel(x, c, w_ada, b_ada, norm_mix_g, w_in, rnn_conv_w, rnn_conv_b, lru_w_a, lru_b_a, lru_w_x, lru_b_x, lru_lambda, sgu_ln_g, sgu_ln_b, sgu_w_s, sgu_b_s, w_branch_a, w_branch_b, w_out, norm_ffn_g, w_up, ffn_conv_w, ffn_conv_b, w_down, norm_final_g, loss_target, m_w_ada, m_b_ada, m_norm_mix_g, m_w_in, m_rnn_conv_w, m_rnn_conv_b, m_lru_w_a, m_lru_b_a, m_lru_w_x, m_lru_b_x, m_lru_lambda, m_sgu_ln_g, m_sgu_ln_b, m_sgu_w_s, m_sgu_b_s, m_w_branch_a, m_w_branch_b, m_w_out, m_norm_ffn_g, m_w_up, m_ffn_conv_w, m_ffn_conv_b, m_w_down, m_norm_final_g, v_w_ada, v_b_ada, v_norm_mix_g, v_w_in, v_rnn_conv_w, v_rnn_conv_b, v_lru_w_a, v_lru_b_a, v_lru_w_x, v_lru_b_x, v_lru_lambda, v_sgu_ln_g, v_sgu_ln_b, v_sgu_w_s, v_sgu_b_s, v_w_branch_a, v_w_branch_b, v_w_out, v_norm_ffn_g, v_w_up, v_ffn_conv_w, v_ffn_conv_b, v_w_down, v_norm_final_g):
    given = dict(locals())
    me = 4 * lax.axis_index("x") + 2 * lax.axis_index("y") + lax.axis_index("c")
    ada_cols = w_ada.shape[2]
    conv_cols = {"rnn_conv_w": rnn_conv_w.shape[2], "ffn_conv_w": ffn_conv_w.shape[2]}

    (win, c_all, cw_rnn, cw_ffn), _ = _all_gather(
        [w_in[0].astype(BF16), c.reshape(1, 1, D), rnn_conv_w[0], ffn_conv_w[0]], [1, 0, 1, 1], "gather_first")
    c_all = c_all.reshape(N_DEV, D)

    b_cols = lax.dynamic_slice_in_dim(b_ada, me * ada_cols, ada_cols, axis=1)
    (mod_all,), mod_done = _all_gather(
        [_mod_cols(c_all, w_ada[0], b_cols).reshape(1, N_DEV, ada_cols)], [0], "gather_mod")
    mod_all = mod_all.reshape(N_DEV, N_DEV, ada_cols)
    mod_mine = lax.dynamic_index_in_dim(mod_all, me, axis=1, keepdims=False).reshape(6, 1, D)

    late_groups = {"merge": (["w_branch_a", "w_branch_b", "w_out"], [0, 0, 0]), "ffn_up": (["w_up"], [1]),
                   "ffn_down": (["w_down"], [0])}
    in_flight, started = {}, mod_done[0:1, 0:1]
    for stage, (names, axes) in late_groups.items():
        shards = [(given[n][0] + started).astype(BF16) for n in names]
        plan = _gather_plan(axes, [s.shape[-1] for s in shards])
        send, recv, srcs, lands, token = _exchange_start(
            "gather_start_" + stage, shards, [_own_block_placed(s, ax, me) for s, ax in zip(shards, axes)], plan,
            len(shards) * (N_DEV - 1))
        in_flight[stage] = (send, recv, srcs, lands, plan)
        started = started + token[0:1, 0:1]

    def late_weights(stage, after):
        send, recv, srcs, lands, plan = in_flight[stage]
        full = _exchange_wait("gather_wait_" + stage, send, recv, srcs, lands, plan, after)
        full = [w.reshape(-1, D) if ax == 0 else w for w, ax in zip(full, late_groups[stage][1])]
        return full if len(full) > 1 else full[0]

    mod_mine = mod_mine + started

    reducing = {}

    def grads_ready(stage, grads):
        names = [n for n in BIG_NAMES if n in grads]
        blocked = {}
        for n in names:
            ax = BIG_AXES[BIG_NAMES.index(n)]
            g = grads[n] if ax == 1 else grads[n].reshape(N_DEV, grads[n].shape[0] // N_DEV, grads[n].shape[1])
            blocked.setdefault((ax, g.shape), []).append((n, g))
        sums = {}
        for (ax, _), group in blocked.items():
            reduced = _sibling_reduce([g for _, g in group], ax, "reduce_sibling_" + "_".join(n for n, _ in group))
            sums.update({n: r for (n, _), r in zip(group, reduced)})
        sums = [sums[n] for n in names]
        pays = [pay for _, pay in sums]
        send, recv, srcs, lands, tok = _exchange_start(
            "reduce_start_" + stage, pays, [lax.empty(p_.shape, p_.dtype) for p_ in pays], _chip_plan, 3 * len(pays))
        reducing[stage] = (names, [own for own, _ in sums], send, recv, srcs, lands)
        return tok

    p = {n: given[n][0] for n in REPLICATED if n not in ("b_ada", "norm_final_g")}
    p = {n: (a.reshape(1, -1) if a.ndim == 1 else a) for n, a in p.items()}
    p["rnn_conv_w"], p["ffn_conv_w"] = cw_rnn, cw_ffn
    p["norm_final_g"] = norm_final_g.reshape(1, D)
    loss, grad_x, _, small, dmod = _local_step(x[0], loss_target[0], mod_mine, win, late_weights, p, grads_ready)

    small["b_ada"] = dmod.reshape(1, 6 * D)
    rows_of = {n: _pack_rows(small[n].shape) for n in SMALL_NAMES}
    start_of = {n: sum(rows_of[q] for q in SMALL_NAMES[:k]) for k, n in enumerate(SMALL_NAMES)}
    pack = _pack([small[n] for n in SMALL_NAMES])
    (packs,), _ = _all_gather([pack[None]], [0], "gather_small")
    packs = packs.reshape(N_DEV, pack.shape[0], LANES)

    out = {}
    for stage, (names, owns, send, recv, srcs, lands) in reducing.items():
        landed = _exchange_wait("reduce_wait_" + stage, send, recv, srcs, lands, _chip_plan, [packs])
        for n, own, got in zip(names, owns, landed):
            out[n] = _adamw(given[n][0], given["m_" + n][0], given["v_" + n][0], [own, got], "adamw_" + n)

    dmod_all = packs[:, :rows_of["b_ada"]].reshape(N_DEV, 6 * D)
    dmod_cols = lax.dynamic_slice_in_dim(dmod_all, me * ada_cols, ada_cols, axis=1)
    out["w_ada"] = _adamw(w_ada[0], m_w_ada[0], v_w_ada[0], [_ada_grad(c_all, dmod_cols)], "adamw_w_ada")

    rep_rows = sum(rows_of[n] for n in REPLICATED)
    res = _adamw(*[_pack([given[pre + n] for n in REPLICATED]) for pre in ("", "m_", "v_")],
                 [packs[:, :rep_rows]], "adamw_small")
    unpacked = [_unpack(r, [given[n].shape for n in REPLICATED]) for r in res]
    for k, n in enumerate(REPLICATED):
        out[n] = tuple(u[k] for u in unpacked)

    for n in COL_SHARDED:
        full = packs[:, start_of[n]:start_of[n] + rows_of[n]].reshape(N_DEV, small[n].shape[0], small[n].shape[1])
        mine = lax.dynamic_slice_in_dim(full, me * conv_cols[n], conv_cols[n], axis=2)
        out[n] = _adamw(given[n][0], given["m_" + n][0], given["v_" + n][0], [mine], "adamw_" + n)

    total = lax.psum(loss[0, 0], ("x", "y", "c"))
    results = [total, grad_x[None]]
    for kind in range(4):
        results += [out[n][kind].reshape(given[n].shape) for n in WEIGHTS]
    return tuple(results)
```

```python
import math

import jax
import jax.numpy as jnp
from jax import lax
from jax.experimental import pallas as pl
from jax.experimental.pallas import tpu as pltpu

F32 = jnp.float32
BF16 = jnp.bfloat16
MESH_IDS = pl.DeviceIdType.MESH

D = 1024
NH = 8
HD = 128
NCOL_IN = 6 * D
DFF = 3 * D
N_DEV = 8
EPS = 1e-6
LRU_C = 8.0
ADAM_LR, ADAM_B1, ADAM_B2, ADAM_EPS, ADAM_WD, ADAM_STEP = 0.001, 0.9, 0.999, 1e-08, 0.01, 10

SUBLANES = 8
HALO = 16
VMEM_LIMIT = 56 * 1024 * 1024
GELU_K = math.sqrt(2.0 / math.pi)
GELU_C = 0.044715


def _cparams(n_axes):
    return pltpu.CompilerParams(dimension_semantics=("arbitrary",) * n_axes, vmem_limit_bytes=VMEM_LIMIT)


def _const_spec(shape, single_buffer=False):
    nd = len(shape)
    if single_buffer:
        return pl.BlockSpec(shape, lambda *_: (0,) * nd, pipeline_mode=pl.Buffered(1))
    return pl.BlockSpec(shape, lambda *_: (0,) * nd)


def _tile_big(t):
    return min(512, t)


def _tile_seq(t):
    return min(256, t)


def _row_tile(rows):
    if rows <= 512:
        return rows
    return next(tr for tr in range(512, 0, -SUBLANES) if rows % tr == 0)


def _gelu_t(x):
    t = jnp.tanh(GELU_K * (x + GELU_C * (x * x * x)))
    return 0.5 * x * (1.0 + t), t


def _gelu_grad(x, t):
    return 0.5 * (1.0 + t) + 0.5 * x * (1.0 - t * t) * (GELU_K * (1.0 + 3.0 * GELU_C * x * x))


def _sigmoid(x):
    return 1.0 / (1.0 + jnp.exp(-x))


def _log_sigmoid(x):
    return -(jnp.maximum(-x, 0.0) + jnp.log1p(jnp.exp(-jnp.abs(x))))


def _row_iota(cols):
    return lax.broadcasted_iota(jnp.int32, (SUBLANES, cols), 0)


def _shift_down(x, k, prev8):
    if k == 0:
        return x
    r = pltpu.roll(x, k, 0)
    p = pltpu.roll(prev8, k, 0)
    head = jnp.where(_row_iota(x.shape[1]) < k, p, r[:SUBLANES])
    return jnp.concatenate([head, r[SUBLANES:]], axis=0)


def _shift_up(x, k, next8):
    if k == 0:
        return x
    n = x.shape[0]
    r = pltpu.roll(x, n - k, 0)
    q = pltpu.roll(next8, SUBLANES - k, 0)
    tail = jnp.where(_row_iota(x.shape[1]) >= SUBLANES - k, q, r[n - SUBLANES:])
    return jnp.concatenate([r[:n - SUBLANES], tail], axis=0)


def _heads_nn(x_bf, w_ref):
    return jnp.concatenate(
        [jnp.dot(x_bf[:, h * HD:(h + 1) * HD], w_ref[h], preferred_element_type=F32) for h in range(NH)], axis=1)


def _heads_nt(x_bf, w_ref):
    return jnp.concatenate(
        [lax.dot_general(x_bf[:, h * HD:(h + 1) * HD], w_ref[h], (((1,), (1,)), ((), ())), preferred_element_type=F32)
         for h in range(NH)], axis=1)


def _dot_nt(a, b):
    return lax.dot_general(a, b, (((1,), (1,)), ((), ())), preferred_element_type=F32)


def _dot_tn(a, b):
    return lax.dot_general(a, b, (((0,), (0,)), ((), ())), preferred_element_type=F32)


def _colsum(x):
    return jnp.sum(x, axis=0, keepdims=True)


def _prev_halo_map(tm, col):
    return lambda i, *_: (jnp.maximum(i * (tm // HALO) - 1, 0), col)


def _norm_proj(x, g, scale, shift, w, name):
    t, n = x.shape[0], w.shape[1]
    tm = _tile_big(t)

    def body(x_ref, g_ref, sc_ref, sh_ref, w_ref, h_ref, z_ref):
        xv = x_ref[...]
        r = lax.rsqrt(jnp.mean(xv * xv, axis=-1, keepdims=True) + EPS)
        hb = ((xv * r * g_ref[...]) * (1.0 + sc_ref[...]) + sh_ref[...]).astype(BF16)
        h_ref[...] = hb
        for c0 in range(0, n, D):
            z_ref[:, c0:c0 + D] = jnp.dot(hb, w_ref[:, c0:c0 + D], preferred_element_type=F32).astype(BF16)

    vec = _const_spec((1, D))
    return pl.pallas_call(
        body, name=name, grid=(t // tm,),
        in_specs=[pl.BlockSpec((tm, D), lambda i: (i, 0)), vec, vec, vec, _const_spec((D, n), True)],
        out_specs=[pl.BlockSpec((tm, D), lambda i: (i, 0)), pl.BlockSpec((tm, n), lambda i: (i, 0))],
        out_shape=[jax.ShapeDtypeStruct((t, D), BF16), jax.ShapeDtypeStruct((t, n), BF16)],
        compiler_params=_cparams(1),
    )(x, g, scale, shift, w)


def _lru_gates(xc, wa_ref, ba, wx_ref, bx, ls):
    xb = xc.astype(BF16)
    ra = _sigmoid(_heads_nn(xb, wa_ref) + ba)
    ia = _sigmoid(_heads_nn(xb, wx_ref) + bx)
    la = LRU_C * ra * ls
    a = jnp.exp(la)
    mult = jnp.sqrt(-jnp.tanh(la) * (1.0 + a * a))
    return ra, ia, a, mult


def _conv4(xr, prev8, cw_ref, cb):
    return (cb + cw_ref[3:4, :] * xr + cw_ref[2:3, :] * _shift_down(xr, 1, prev8)
            + cw_ref[1:2, :] * _shift_down(xr, 2, prev8) + cw_ref[0:1, :] * _shift_down(xr, 3, prev8))


def _rnn_fwd(z, cw, cb, wa, ba, wx, bx, lam):
    t = z.shape[0]
    tm = _tile_seq(t)
    ngrp = tm // SUBLANES

    def body(xr_ref, xp_ref, gr_ref, cw_ref, cb_ref, wa_ref, ba_ref, wx_ref, bx_ref, lam_ref,
             h_ref, ya_ref, xc_ref, ra_ref, ia_ref, gg_ref, hg_ref, carry_ref, a_scr, u_scr):
        i = pl.program_id(0)

        @pl.when(i == 0)
        def _():
            carry_ref[...] = jnp.zeros_like(carry_ref)

        xr = xr_ref[...].astype(F32)
        prev8 = jnp.where(i == 0, 0.0, xp_ref[...].astype(F32)[HALO - SUBLANES:])
        xc = _conv4(xr, prev8, cw_ref, cb_ref[...])
        ra, ia, a, mult = _lru_gates(xc, wa_ref, ba_ref[...], wx_ref, bx_ref[...], _log_sigmoid(lam_ref[...]))
        xc_ref[...] = xc.astype(BF16)
        ra_ref[...] = ra.astype(BF16)
        ia_ref[...] = ia.astype(BF16)
        a_scr[...] = a
        u_scr[...] = mult * (ia * xc)
        row = _row_iota(D)

        def grp(j, carry):
            r0 = pl.multiple_of(j * SUBLANES, SUBLANES)
            av = a_scr[pl.ds(r0, SUBLANES), :]
            uv = u_scr[pl.ds(r0, SUBLANES), :]
            for d in (1, 2, 4):
                m = row >= d
                uv = jnp.where(m, av * pltpu.roll(uv, d, 0) + uv, uv)
                av = jnp.where(m, av * pltpu.roll(av, d, 0), av)
            hv = uv + av * carry
            h_ref[pl.ds(r0, SUBLANES), :] = hv
            return hv[SUBLANES - 1:SUBLANES, :]

        carry_ref[0:1, :] = lax.fori_loop(0, ngrp, grp, carry_ref[0:1, :])
        grv = gr_ref[...].astype(F32)
        gg, tg = _gelu_t(grv)
        hv = h_ref[...]
        ya_ref[...] = (hv * gg).astype(BF16)
        gg_ref[...] = gg.astype(BF16)
        hg_ref[...] = (hv * _gelu_grad(grv, tg)).astype(BF16)

    vec = _const_spec((1, D))
    wspec = _const_spec((NH, HD, HD))
    tile = pl.BlockSpec((tm, D), lambda i: (i, 0))
    bshape = jax.ShapeDtypeStruct((t, D), BF16)
    return pl.pallas_call(
        body, name="rnn_fwd", grid=(t // tm,),
        in_specs=[tile, pl.BlockSpec((HALO, D), _prev_halo_map(tm, 0)),
                  pl.BlockSpec((tm, D), lambda i: (i, 1)), _const_spec((4, D)), vec, wspec, vec, wspec, vec, vec],
        out_specs=[tile] * 7,
        out_shape=[jax.ShapeDtypeStruct((t, D), F32)] + [bshape] * 6,
        scratch_shapes=[pltpu.VMEM((SUBLANES, D), F32), pltpu.VMEM((tm, D), F32), pltpu.VMEM((tm, D), F32)],
        compiler_params=_cparams(1),
    )(z, z, z, cw, cb, wa, ba, wx, bx, lam)


def _sgu_core(zu, zv, lng, lnb, wm_ref, bst_ref):
    gu, tu = _gelu_t(zu)
    gv, tv = _gelu_t(zv)
    mu = jnp.mean(gv, axis=-1, keepdims=True)
    cen = gv - mu
    rstd = lax.rsqrt(jnp.mean(cen * cen, axis=-1, keepdims=True) + EPS)
    vhat = cen * rstd
    vln = vhat * lng + lnb
    vb = vln.astype(BF16)
    rows = []
    for b0 in range(0, zu.shape[0], HD):
        rows.append(jnp.concatenate(
            [jnp.dot(wm_ref[g], vb[b0:b0 + HD, g * HD:(g + 1) * HD], preferred_element_type=F32)
             + bst_ref[:, g:g + 1] for g in range(NH)], axis=1))
    mixed = jnp.concatenate(rows, axis=0) if len(rows) > 1 else rows[0]
    return gu, tu, tv, rstd, vhat, vb, mixed


def _sgu_fwd(z, lng, lnb, wm, bst):
    t = z.shape[0]
    tm = _tile_seq(t)

    def body(zu_ref, zv_ref, lng_ref, lnb_ref, wm_ref, bst_ref, yb_ref):
        gu, _, _, _, _, _, mixed = _sgu_core(zu_ref[...].astype(F32), zv_ref[...].astype(F32),
                                             lng_ref[...], lnb_ref[...], wm_ref, bst_ref)
        yb_ref[...] = (gu * mixed).astype(BF16)

    vec = _const_spec((1, D))
    return pl.pallas_call(
        body, name="sgu_fwd", grid=(t // tm,),
        in_specs=[pl.BlockSpec((tm, D), lambda i: (i, 2)), pl.BlockSpec((tm, D), lambda i: (i, 3)), vec, vec,
                  _const_spec((NH, HD, HD)), _const_spec((HD, NH))],
        out_specs=pl.BlockSpec((tm, D), lambda i: (i, 0)),
        out_shape=jax.ShapeDtypeStruct((t, D), BF16),
        compiler_params=_cparams(1),
    )(z, z, lng, lnb, wm, bst)


def _merge_fwd(ya_pre, yb_pre, z, x, gate1, wba, wbb, wout):
    t = x.shape[0]
    tm = _tile_big(t)

    def body(yap_ref, ybp_ref, ga_ref, gb_ref, x_ref, g1_ref, wba_ref, wbb_ref, wo_ref,
             x2_ref, ya_ref, yb_ref, mg_ref, o1_ref):
        ya = jnp.dot(yap_ref[...], wba_ref[...], preferred_element_type=F32)
        yb = jnp.dot(ybp_ref[...], wbb_ref[...], preferred_element_type=F32)
        merged = _sigmoid(ga_ref[...].astype(F32)) * ya + _sigmoid(gb_ref[...].astype(F32)) * yb
        mb = merged.astype(BF16)
        o1 = jnp.dot(mb, wo_ref[...], preferred_element_type=F32)
        x2_ref[...] = x_ref[...] + g1_ref[...] * o1
        ya_ref[...] = ya.astype(BF16)
        yb_ref[...] = yb.astype(BF16)
        mg_ref[...] = mb
        o1_ref[...] = o1.astype(BF16)

    tile = pl.BlockSpec((tm, D), lambda i: (i, 0))
    wspec = _const_spec((D, D))
    bshape = jax.ShapeDtypeStruct((t, D), BF16)
    return pl.pallas_call(
        body, name="merge_fwd", grid=(t // tm,),
        in_specs=[tile, tile, pl.BlockSpec((tm, D), lambda i: (i, 4)), pl.BlockSpec((tm, D), lambda i: (i, 5)),
                  tile, _const_spec((1, D)), wspec, wspec, wspec],
        out_specs=[tile] * 5,
        out_shape=[jax.ShapeDtypeStruct((t, D), F32), bshape, bshape, bshape, bshape],
        compiler_params=_cparams(1),
    )(ya_pre, yb_pre, z, z, x, gate1, wba, wbb, wout)


def _conv3(u, prev8, cw_ref, cb):
    return cb + cw_ref[2:3, :] * u + cw_ref[1:2, :] * _shift_down(u, 1, prev8) + cw_ref[0:1, :] * _shift_down(u, 2, prev8)


def _ffn_proj_mid(x2, g, scale, shift, w, cw, cb):
    t = x2.shape[0]
    tm = _tile_seq(t)
    nc = DFF // D

    def body(x_ref, g_ref, sc_ref, sh_ref, w_ref, cw_ref, cb_ref, h_ref, up_ref, ff_ref, fa_ref, fv_ref, prev_ref):
        @pl.when(pl.program_id(0) == 0)
        def _():
            prev_ref[...] = jnp.zeros_like(prev_ref)

        xv = x_ref[...]
        r = lax.rsqrt(jnp.mean(xv * xv, axis=-1, keepdims=True) + EPS)
        hb = ((xv * r * g_ref[...]) * (1.0 + sc_ref[...]) + sh_ref[...]).astype(BF16)
        h_ref[...] = hb
        for c in range(nc):
            halves = []
            for c0 in (c * D, DFF + c * D):
                u = jnp.dot(hb, w_ref[:, c0:c0 + D], preferred_element_type=F32)
                up_ref[:, c0:c0 + D] = u.astype(BF16)
                halves.append(_conv3(u, prev_ref[:, c0:c0 + D], cw_ref[:, c0:c0 + D], cb_ref[:, c0:c0 + D]))
                prev_ref[:, c0:c0 + D] = u[tm - SUBLANES:]
            act, val = halves
            ga, ta = _gelu_t(act)
            cols = slice(c * D, (c + 1) * D)
            ff_ref[:, cols] = (ga * val).astype(BF16)
            fa_ref[:, cols] = (val * _gelu_grad(act, ta)).astype(BF16)
            fv_ref[:, cols] = ga.astype(BF16)

    vec = _const_spec((1, D))
    n = 2 * DFF
    half = pl.BlockSpec((tm, DFF), lambda i: (i, 0))
    hshape = jax.ShapeDtypeStruct((t, DFF), BF16)
    return pl.pallas_call(
        body, name="ffn_proj_mid", grid=(t // tm,),
        in_specs=[pl.BlockSpec((tm, D), lambda i: (i, 0)), vec, vec, vec, _const_spec((D, n), True),
                  _const_spec((3, n)), _const_spec((1, n))],
        out_specs=[pl.BlockSpec((tm, D), lambda i: (i, 0)), pl.BlockSpec((tm, n), lambda i: (i, 0)), half, half, half],
        out_shape=[jax.ShapeDtypeStruct((t, D), BF16), jax.ShapeDtypeStruct((t, n), BF16), hshape, hshape, hshape],
        scratch_shapes=[pltpu.VMEM((SUBLANES, n), F32)],
        compiler_params=_cparams(1),
    )(x2, g, scale, shift, w, cw, cb)


def _ffn_out_loss(ff, wd, x2, target, gate2, gfin):
    t = x2.shape[0]
    tm = _tile_big(t)

    def body(ff_ref, wd_ref, x2_ref, tg_ref, g2_ref, gf_ref, dx3_ref, loss_ref, dgf_ref, dg2_ref):
        @pl.when(pl.program_id(0) == 0)
        def _():
            loss_ref[...] = jnp.zeros_like(loss_ref)
            dgf_ref[...] = jnp.zeros_like(dgf_ref)
            dg2_ref[...] = jnp.zeros_like(dg2_ref)

        o2 = jnp.dot(ff_ref[...], wd_ref[...], preferred_element_type=F32)
        x3 = x2_ref[...] + g2_ref[...] * o2
        r = lax.rsqrt(jnp.mean(x3 * x3, axis=-1, keepdims=True) + EPS)
        xhat = x3 * r
        err = xhat * gf_ref[...] - tg_ref[...]
        loss_ref[...] += 0.5 * jnp.sum(jnp.mean(err * err, axis=-1, keepdims=True), axis=0, keepdims=True)
        dy = err * (1.0 / D)
        dgf_ref[...] += _colsum(dy * xhat)
        dxh = dy * gf_ref[...]
        dx3 = r * (dxh - xhat * jnp.mean(dxh * xhat, axis=-1, keepdims=True))
        dx3_ref[...] = dx3
        dg2_ref[...] += _colsum(dx3 * o2)

    tile = pl.BlockSpec((tm, D), lambda i: (i, 0))
    vec = _const_spec((1, D))
    return pl.pallas_call(
        body, name="ffn_out_loss", grid=(t // tm,),
        in_specs=[pl.BlockSpec((tm, DFF), lambda i: (i, 0)), _const_spec((DFF, D), True), tile, tile, vec, vec],
        out_specs=[tile, _const_spec((1, 1)), vec, vec],
        out_shape=[jax.ShapeDtypeStruct((t, D), F32), jax.ShapeDtypeStruct((1, 1), F32),
                   jax.ShapeDtypeStruct((1, D), F32), jax.ShapeDtypeStruct((1, D), F32)],
        compiler_params=_cparams(1),
    )(ff, wd, x2, target, gate2, gfin)


def _ffn_down_bwd(dx3, gate2, ff, fa, fv, wd):
    t = dx3.shape[0]
    tm = _tile_big(t)
    nc = DFF // D

    def body(dx3_ref, g2_ref, ff_ref, fa_ref, fv_ref, wd_ref, da_ref, dv_ref, dwd_ref, dcba_ref, dcbv_ref):
        @pl.when(pl.program_id(1) == 0)
        def _():
            for r in (dwd_ref, dcba_ref, dcbv_ref):
                r[...] = jnp.zeros_like(r)

        do2 = (dx3_ref[...] * g2_ref[...]).astype(BF16)
        dwd_ref[...] += _dot_tn(ff_ref[...], do2)
        dff = _dot_nt(do2, wd_ref[...])
        dact = dff * fa_ref[...].astype(F32)
        dval = dff * fv_ref[...].astype(F32)
        da_ref[...] = dact.astype(BF16)
        dv_ref[...] = dval.astype(BF16)
        dcba_ref[...] += _colsum(dact)
        dcbv_ref[...] += _colsum(dval)

    blk = pl.BlockSpec((tm, D), lambda c, i: (i, c))
    vec = pl.BlockSpec((1, D), lambda c, i: (0, c))
    return pl.pallas_call(
        body, name="ffn_down_bwd", grid=(nc, t // tm),
        in_specs=[pl.BlockSpec((tm, D), lambda c, i: (i, 0)), pl.BlockSpec((1, D), lambda c, i: (0, 0)),
                  blk, blk, blk, pl.BlockSpec((D, D), lambda c, i: (c, 0))],
        out_specs=[blk, blk, pl.BlockSpec((D, D), lambda c, i: (c, 0)), vec, vec],
        out_shape=[jax.ShapeDtypeStruct((t, DFF), BF16), jax.ShapeDtypeStruct((t, DFF), BF16),
                   jax.ShapeDtypeStruct((DFF, D), F32),
                   jax.ShapeDtypeStruct((1, DFF), F32), jax.ShapeDtypeStruct((1, DFF), F32)],
        compiler_params=_cparams(2),
    )(dx3, gate2, ff, fa, fv, wd)


def _modnorm_bwd(dh, xv, g, scale):
    r = lax.rsqrt(jnp.mean(xv * xv, axis=-1, keepdims=True) + EPS)
    xhat = xv * r
    dxn = dh * (1.0 + scale)
    dxh = dxn * g
    dx = r * (dxh - xhat * jnp.mean(dxh * xhat, axis=-1, keepdims=True))
    return dx, _colsum(dh), _colsum(dh * (xhat * g)), _colsum(dxn * xhat)


def _ffn_up_bwd(dact, dval, up, cw, wup, x2, dx3, gffn, scale2, o1, gate1):
    t = x2.shape[0]
    tm = _tile_seq(t)
    nt = t // tm
    nc = DFF // D

    def body(da_ref, dan_ref, dv_ref, dvn_ref, up_ref, cw_ref, w_ref, x2_ref, dx3_ref, g_ref, sc_ref, o1_ref, g1_ref,
             dup_ref, dx2_ref, do1_ref, dcw_ref, dsh_ref, dsc_ref, dg_ref, dg1_ref):
        i = pl.program_id(0)

        @pl.when(i == 0)
        def _():
            for r in (dcw_ref, dsh_ref, dsc_ref, dg_ref, dg1_ref):
                r[...] = jnp.zeros_like(r)

        last = i == nt - 1
        dh = jnp.zeros((tm, D), F32)
        for half, (d_ref, dn_ref) in enumerate(((da_ref, dan_ref), (dv_ref, dvn_ref))):
            nxt = jnp.where(last, 0.0, dn_ref[...].astype(F32)[:SUBLANES])
            for c in range(nc):
                c0 = half * DFF + c * D
                dv = d_ref[:, c * D:(c + 1) * D].astype(F32)
                nx = nxt[:, c * D:(c + 1) * D]
                taps = (_shift_up(dv, 2, nx), _shift_up(dv, 1, nx), dv)
                dup = (cw_ref[2:3, c0:c0 + D] * taps[2] + cw_ref[1:2, c0:c0 + D] * taps[1]
                       + cw_ref[0:1, c0:c0 + D] * taps[0]).astype(BF16)
                upv = up_ref[:, c0:c0 + D].astype(F32)
                for k in range(3):
                    dcw_ref[k:k + 1, c0:c0 + D] += _colsum(taps[k] * upv)
                dup_ref[:, c0:c0 + D] = dup
                dh = dh + _dot_nt(dup, w_ref[:, c0:c0 + D])
        dxn, dsh, dsc, dg = _modnorm_bwd(dh, x2_ref[...], g_ref[...], sc_ref[...])
        dx2 = dx3_ref[...] + dxn
        dx2_ref[...] = dx2
        do1_ref[...] = (dx2 * g1_ref[...]).astype(BF16)
        dsh_ref[...] += dsh
        dsc_ref[...] += dsc
        dg_ref[...] += dg
        dg1_ref[...] += _colsum(dx2 * o1_ref[...].astype(F32))

    tile = pl.BlockSpec((tm, D), lambda i: (i, 0))
    wide = pl.BlockSpec((tm, DFF), lambda i: (i, 0))
    nxt = pl.BlockSpec((HALO, DFF), lambda i: (jnp.minimum((i + 1) * (tm // HALO), t // HALO - 1), 0))
    vec = _const_spec((1, D))
    vshape = jax.ShapeDtypeStruct((1, D), F32)
    return pl.pallas_call(
        body, name="ffn_up_bwd", grid=(nt,),
        in_specs=[wide, nxt, wide, nxt, pl.BlockSpec((tm, 2 * DFF), lambda i: (i, 0)),
                  _const_spec((3, 2 * DFF)), _const_spec((D, 2 * DFF), True),
                  tile, tile, vec, vec, tile, vec],
        out_specs=[pl.BlockSpec((tm, 2 * DFF), lambda i: (i, 0)), tile, tile, _const_spec((3, 2 * DFF)),
                   vec, vec, vec, vec],
        out_shape=[jax.ShapeDtypeStruct((t, 2 * DFF), BF16), jax.ShapeDtypeStruct((t, D), F32),
                   jax.ShapeDtypeStruct((t, D), BF16), jax.ShapeDtypeStruct((3, 2 * DFF), F32),
                   vshape, vshape, vshape, vshape],
        compiler_params=_cparams(1),
    )(dact, dact, dval, dval, up, cw, wup, x2, dx3, gffn, scale2, o1, gate1)


def _xt_y(a, b, name):
    t, k = a.shape
    n = b.shape[1]
    tm = min(1024, t)
    bn = 768 if n % 768 == 0 else D

    def body(a_ref, b_ref, o_ref):
        @pl.when(pl.program_id(1) == 0)
        def _():
            o_ref[...] = jnp.zeros_like(o_ref)

        o_ref[...] += _dot_tn(a_ref[...], b_ref[...])

    return pl.pallas_call(
        body, name=name, grid=(n // bn, t // tm),
        in_specs=[pl.BlockSpec((tm, k), lambda j, i: (i, 0)), pl.BlockSpec((tm, bn), lambda j, i: (i, j))],
        out_specs=pl.BlockSpec((k, bn), lambda j, i: (0, j)),
        out_shape=jax.ShapeDtypeStruct((k, n), F32),
        compiler_params=_cparams(2),
    )(a, b)


def _acc_spec(shape, index):
    return pl.BlockSpec(shape, lambda *_: index, pipeline_mode=pl.Buffered(1))


def _out_bwd(do1, wout, merged, ya, yb, z, h1):
    t = do1.shape[0]
    tm = _tile_big(t)

    def body(do1_ref, wo_ref, mg_ref, ya_ref, yb_ref, ga_ref, gb_ref, h1_ref,
             dya_ref, dyb_ref, dz_ref, dwo_ref, dwin_ref):
        @pl.when(pl.program_id(0) == 0)
        def _():
            dwo_ref[...] = jnp.zeros_like(dwo_ref)
            dwin_ref[...] = jnp.zeros_like(dwin_ref)

        do1v = do1_ref[...]
        dwo_ref[...] += _dot_tn(mg_ref[...], do1v)
        dm = _dot_nt(do1v, wo_ref[...])
        sa = _sigmoid(ga_ref[...].astype(F32))
        sb = _sigmoid(gb_ref[...].astype(F32))
        dya_ref[...] = (dm * sa).astype(BF16)
        dyb_ref[...] = (dm * sb).astype(BF16)
        dga = (dm * ya_ref[...].astype(F32) * sa * (1.0 - sa)).astype(BF16)
        dgb = (dm * yb_ref[...].astype(F32) * sb * (1.0 - sb)).astype(BF16)
        dz_ref[:, 0:D] = dga
        dz_ref[:, D:2 * D] = dgb
        h1v = h1_ref[...]
        dwin_ref[:, 0:D] += _dot_tn(h1v, dga)
        dwin_ref[:, D:2 * D] += _dot_tn(h1v, dgb)

    tile = pl.BlockSpec((tm, D), lambda i: (i, 0))
    bshape = jax.ShapeDtypeStruct((t, D), BF16)
    return pl.pallas_call(
        body, name="out_bwd", grid=(t // tm,),
        in_specs=[tile, _const_spec((D, D), True), tile, tile, tile,
                  pl.BlockSpec((tm, D), lambda i: (i, 4)), pl.BlockSpec((tm, D), lambda i: (i, 5)), tile],
        out_specs=[tile, tile, pl.BlockSpec((tm, 2 * D), lambda i: (i, 2)), _acc_spec((D, D), (0, 0)),
                   _acc_spec((D, 2 * D), (0, 2))],
        out_shape=[bshape, bshape, jax.ShapeDtypeStruct((t, NCOL_IN), BF16), jax.ShapeDtypeStruct((D, D), F32),
                   jax.ShapeDtypeStruct((D, NCOL_IN), F32)],
        compiler_params=_cparams(1),
    )(do1, wout, merged, ya, yb, z, z, h1)


def _rnn_bwd(dya, ya_pre, wba, h1, z, saved, h, dz, dwin, cw, wa, wx, lam):
    t = z.shape[0]
    tm = _tile_seq(t)
    nt = t // tm
    ngrp = tm // SUBLANES
    hpt = tm // HALO

    def body(dya_ref, yap_ref, wba_ref, h1_ref, xr_ref, xc_ref, ra_ref, ia_ref, gg_ref, hg_ref, h_ref, hp_ref,
             dz_any, dwin_any, cw_ref, wa_ref, wx_ref, lam_ref,
             dz_ref, dwin_ref, dwba_ref, dcw_ref, dcb_ref, dwa_ref, dba_ref, dwx_ref, dbx_ref, dlam_ref,
             a_first, g_first, dxc_first, b_scr, d_scr, g_scr):
        del dz_any, dwin_any
        i = pl.program_id(0)

        @pl.when(i == 0)
        def _():
            for r in (dwin_ref, dwba_ref, dcw_ref, dcb_ref, dwa_ref, dba_ref, dwx_ref, dbx_ref, dlam_ref,
                      a_first, g_first, dxc_first):
                r[...] = jnp.zeros_like(r)

        dya_v = dya_ref[...]
        dwba_ref[...] += _dot_tn(yap_ref[...], dya_v)
        dyap_v = _dot_nt(dya_v, wba_ref[...])
        h1v = h1_ref[...]

        first_tile = i == nt - 1
        xc = xc_ref[...].astype(F32)
        ra = ra_ref[...].astype(F32)
        ia = ia_ref[...].astype(F32)
        lam_v = lam_ref[...]
        ls = _log_sigmoid(lam_v)
        la = LRU_C * ra * ls
        a = jnp.exp(la)
        mult = jnp.sqrt(-jnp.tanh(la) * (1.0 + a * a))
        hprev8 = jnp.where(first_tile, 0.0, hp_ref[...][HALO - SUBLANES:])
        h_prev = _shift_down(h_ref[...], 1, hprev8)
        dgr = (dyap_v * hg_ref[...].astype(F32)).astype(BF16)
        dz_ref[:, D:2 * D] = dgr
        dwin_ref[:, D:2 * D] += _dot_tn(h1v, dgr)

        b_scr[...] = _shift_up(a, 1, a_first[...])
        d_scr[...] = dyap_v * gg_ref[...].astype(F32)
        row = _row_iota(D)

        def grp(jj, carry):
            r0 = pl.multiple_of((ngrp - 1 - jj) * SUBLANES, SUBLANES)
            bv = b_scr[pl.ds(r0, SUBLANES), :]
            dv = d_scr[pl.ds(r0, SUBLANES), :]
            for d in (1, 2, 4):
                m = row < SUBLANES - d
                dv = jnp.where(m, dv + bv * pltpu.roll(dv, SUBLANES - d, 0), dv)
                bv = jnp.where(m, bv * pltpu.roll(bv, SUBLANES - d, 0), bv)
            gv = dv + bv * carry
            g_scr[pl.ds(r0, SUBLANES), :] = gv
            return gv[0:1, :]

        lax.fori_loop(0, ngrp, grp, g_first[0:1, :])
        g = g_scr[...]
        a_first[...] = a[:SUBLANES]
        g_first[...] = g[:SUBLANES]

        da = g * h_prev
        gx = g * xc
        dmult = gx * ia
        dia = gx * mult
        dxc = g * (mult * ia)
        dla = da * a - dmult * (a * a) / mult
        dra = dla * (LRU_C * ls)
        dlam_ref[...] += _colsum(dla * ra) * (LRU_C * _sigmoid(-lam_v))
        dpa = dra * ra * (1.0 - ra)
        dpx = dia * ia * (1.0 - ia)
        dba_ref[...] += _colsum(dpa)
        dbx_ref[...] += _colsum(dpx)
        dpab = dpa.astype(BF16)
        dpxb = dpx.astype(BF16)
        xcb = xc_ref[...]
        for hd in range(NH):
            sl = slice(hd * HD, (hd + 1) * HD)
            dwa_ref[hd] += _dot_tn(xcb[:, sl], dpab[:, sl])
            dwx_ref[hd] += _dot_tn(xcb[:, sl], dpxb[:, sl])
        dxc = dxc + _heads_nt(dpab, wa_ref) + _heads_nt(dpxb, wx_ref)

        nxt = dxc_first[...]
        taps = (_shift_up(dxc, 3, nxt), _shift_up(dxc, 2, nxt), _shift_up(dxc, 1, nxt), dxc)
        dxr = cw_ref[0:1, :] * taps[0]
        for k in range(1, 4):
            dxr = dxr + cw_ref[k:k + 1, :] * taps[k]
        dxrb = dxr.astype(BF16)
        dz_ref[:, 0:D] = dxrb
        dwin_ref[:, 0:D] += _dot_tn(h1v, dxrb)
        dxc_first[...] = dxc[:SUBLANES]
        dcb_ref[...] += _colsum(dxc)
        xr = xr_ref[...].astype(F32)
        for k in range(4):
            dcw_ref[k:k + 1, :] += _colsum(taps[k] * xr)

    def rev(col):
        return lambda i: (nt - 1 - i, col)

    vec = _const_spec((1, D))
    wspec = _const_spec((NH, HD, HD))
    vshape = jax.ShapeDtypeStruct((1, D), F32)
    wshape = jax.ShapeDtypeStruct((NH, HD, HD), F32)
    any_spec = pl.BlockSpec(memory_space=pl.ANY)
    tile = pl.BlockSpec((tm, D), rev(0))
    outs = pl.pallas_call(
        body, name="rnn_bwd", grid=(nt,),
        in_specs=[tile, tile, _const_spec((D, D), True), tile, tile, tile, tile, tile, tile, tile, tile,
                  pl.BlockSpec((HALO, D), lambda i: (jnp.maximum((nt - 1 - i) * hpt - 1, 0), 0)),
                  any_spec, any_spec, _const_spec((4, D)), wspec, wspec, vec],
        out_specs=[pl.BlockSpec((tm, 2 * D), rev(0)), _acc_spec((D, 2 * D), (0, 0)), _acc_spec((D, D), (0, 0)),
                   _const_spec((4, D)), vec, wspec, vec, wspec, vec, vec],
        out_shape=[jax.ShapeDtypeStruct((t, NCOL_IN), BF16), jax.ShapeDtypeStruct((D, NCOL_IN), F32),
                   jax.ShapeDtypeStruct((D, D), F32), jax.ShapeDtypeStruct((4, D), F32), vshape,
                   wshape, vshape, wshape, vshape, vshape],
        scratch_shapes=[pltpu.VMEM((SUBLANES, D), F32), pltpu.VMEM((SUBLANES, D), F32), pltpu.VMEM((SUBLANES, D), F32),
                        pltpu.VMEM((tm, D), F32), pltpu.VMEM((tm, D), F32), pltpu.VMEM((tm, D), F32)],
        input_output_aliases={12: 0, 13: 1},
        compiler_params=_cparams(1),
    )(dya, ya_pre, wba, h1, z, *saved, h, h, dz, dwin, cw, wa, wx, lam)
    return outs


def _sgu_bwd(dyb, yb_pre, wbb, h1, z, dz, dwin, lng, lnb, wm, wmt, bst, mask):
    t = z.shape[0]
    tm = _tile_big(t)

    def body(dyb_ref, ybp_ref, wbb_ref, h1_ref, zu_ref, zv_ref, dz_any, dwin_any,
             lng_ref, lnb_ref, wm_ref, wmt_ref, bst_ref, mask_ref,
             dz_ref, dwin_ref, dwbb_ref, dws_ref, dbst_ref, dlng_ref, dlnb_ref):
        del dz_any, dwin_any

        @pl.when(pl.program_id(0) == 0)
        def _():
            for r in (dwin_ref, dwbb_ref, dws_ref, dbst_ref, dlng_ref, dlnb_ref):
                r[...] = jnp.zeros_like(r)

        zu = zu_ref[...].astype(F32)
        zv = zv_ref[...].astype(F32)
        lng_v = lng_ref[...]
        gu, tu, tv, rstd, vhat, vb, mixed = _sgu_core(zu, zv, lng_v, lnb_ref[...], wm_ref, bst_ref)
        dyb_v = dyb_ref[...]
        dwbb_ref[...] += _dot_tn(ybp_ref[...], dyb_v)
        dyb = _dot_nt(dyb_v, wbb_ref[...])
        h1v = h1_ref[...]
        dzu = (dyb * mixed * _gelu_grad(zu, tu)).astype(BF16)
        dz_ref[:, 0:D] = dzu
        dwin_ref[:, 0:D] += _dot_tn(h1v, dzu)
        dmix = dyb * gu
        dmb = dmix.astype(BF16)
        rows = []
        lane = lax.broadcasted_iota(jnp.int32, (HD, NH), 1)
        dbst = jnp.zeros((HD, NH), F32)
        for b0 in range(0, tm, HD):
            cols = []
            for g in range(NH):
                sl = slice(g * HD, (g + 1) * HD)
                dmg = dmb[b0:b0 + HD, sl]
                dws_ref[g] += _dot_nt(dmg, vb[b0:b0 + HD, sl]) * mask_ref[...]
                cols.append(jnp.dot(wmt_ref[g], dmg, preferred_element_type=F32))
                dbst = dbst + jnp.where(lane == g, jnp.sum(dmix[b0:b0 + HD, sl], axis=1, keepdims=True), 0.0)
            rows.append(jnp.concatenate(cols, axis=1))
        dbst_ref[...] += dbst
        dvln = jnp.concatenate(rows, axis=0) if len(rows) > 1 else rows[0]
        dlng_ref[...] += _colsum(dvln * vhat)
        dlnb_ref[...] += _colsum(dvln)
        dvh = dvln * lng_v
        dgv = rstd * (dvh - jnp.mean(dvh, axis=-1, keepdims=True)
                      - vhat * jnp.mean(dvh * vhat, axis=-1, keepdims=True))
        dzv = (dgv * _gelu_grad(zv, tv)).astype(BF16)
        dz_ref[:, D:2 * D] = dzv
        dwin_ref[:, D:2 * D] += _dot_tn(h1v, dzv)

    vec = _const_spec((1, D))
    wspec = _const_spec((NH, HD, HD))
    vshape = jax.ShapeDtypeStruct((1, D), F32)
    tile = pl.BlockSpec((tm, D), lambda i: (i, 0))
    any_spec = pl.BlockSpec(memory_space=pl.ANY)
    return pl.pallas_call(
        body, name="sgu_bwd", grid=(t // tm,),
        in_specs=[tile, tile, _const_spec((D, D), True), tile,
                  pl.BlockSpec((tm, D), lambda i: (i, 2)), pl.BlockSpec((tm, D), lambda i: (i, 3)), any_spec, any_spec,
                  vec, vec, wspec, wspec, _const_spec((HD, NH)), _const_spec((HD, HD))],
        out_specs=[pl.BlockSpec((tm, 2 * D), lambda i: (i, 1)), _acc_spec((D, 2 * D), (0, 1)), _acc_spec((D, D), (0, 0)),
                   wspec, _const_spec((HD, NH)), vec, vec],
        out_shape=[jax.ShapeDtypeStruct((t, NCOL_IN), BF16), jax.ShapeDtypeStruct((D, NCOL_IN), F32),
                   jax.ShapeDtypeStruct((D, D), F32), jax.ShapeDtypeStruct((NH, HD, HD), F32),
                   jax.ShapeDtypeStruct((HD, NH), F32), vshape, vshape],
        input_output_aliases={6: 0, 7: 1},
        compiler_params=_cparams(1),
    )(dyb, yb_pre, wbb, h1, z, z, dz, dwin, lng, lnb, wm, wmt, bst, mask)


def _in_bwd(dz, win, x, dx2, g, scale1):
    t = x.shape[0]
    tm = _tile_big(t)

    def body(dz_ref, w_ref, x_ref, dx2_ref, g_ref, sc_ref, dx_ref, dsh_ref, dsc_ref, dg_ref):
        @pl.when(pl.program_id(0) == 0)
        def _():
            for r in (dsh_ref, dsc_ref, dg_ref):
                r[...] = jnp.zeros_like(r)

        dh = jnp.zeros((tm, D), F32)
        for c0 in range(0, NCOL_IN, D):
            dh = dh + _dot_nt(dz_ref[:, c0:c0 + D], w_ref[:, c0:c0 + D])
        dxn, dsh, dsc, dg = _modnorm_bwd(dh, x_ref[...], g_ref[...], sc_ref[...])
        dx_ref[...] = dx2_ref[...] + dxn
        dsh_ref[...] += dsh
        dsc_ref[...] += dsc
        dg_ref[...] += dg

    tile = pl.BlockSpec((tm, D), lambda i: (i, 0))
    vec = _const_spec((1, D))
    vshape = jax.ShapeDtypeStruct((1, D), F32)
    return pl.pallas_call(
        body, name="in_bwd", grid=(t // tm,),
        in_specs=[pl.BlockSpec((tm, NCOL_IN), lambda i: (i, 0)), _const_spec((D, NCOL_IN), True), tile, tile, vec, vec],
        out_specs=[tile, vec, vec, vec],
        out_shape=[jax.ShapeDtypeStruct((t, D), F32), vshape, vshape, vshape],
        compiler_params=_cparams(1),
    )(dz, win, x, dx2, g, scale1)


def _mod_cols(c_all, w_ada, b_cols):
    nb, cols = c_all.shape[0], w_ada.shape[1]

    def body(c_ref, w_ref, b_ref, o_ref):
        cv = c_ref[...]
        ca = (cv * _sigmoid(cv)).astype(BF16)
        o_ref[...] = jnp.dot(ca, w_ref[...].astype(BF16), preferred_element_type=F32) + b_ref[...]

    return pl.pallas_call(body, name="mod_cols", out_shape=jax.ShapeDtypeStruct((nb, cols), F32))(c_all, w_ada, b_cols)


def _ada_grad(c_all, dmod_cols):
    cols = dmod_cols.shape[1]

    def body(c_ref, d_ref, o_ref):
        cv = c_ref[...]
        ca = (cv * _sigmoid(cv)).astype(BF16)
        o_ref[...] = _dot_tn(ca, d_ref[...].astype(BF16))

    return pl.pallas_call(body, name="ada_grad", out_shape=jax.ShapeDtypeStruct((D, cols), F32))(c_all, dmod_cols)


def _adamw(w, m, v, parts, name):
    rows, cols = w.shape
    tr = _row_tile(rows)
    stacked = [p.ndim == 3 for p in parts]
    bc1 = 1.0 - ADAM_B1 ** ADAM_STEP
    bc2 = 1.0 - ADAM_B2 ** ADAM_STEP

    def body(*refs):
        w_ref, m_ref, v_ref = refs[:3]
        p_refs = refs[3:3 + len(parts)]
        g_ref, d_ref, mo_ref, vo_ref = refs[3 + len(parts):]
        g = None
        for p_ref, st in zip(p_refs, stacked):
            terms = [p_ref[k].astype(F32) for k in range(p_ref.shape[0])] if st else [p_ref[...].astype(F32)]
            for term in terms:
                g = term if g is None else g + term
        mn = ADAM_B1 * m_ref[...] + (1.0 - ADAM_B1) * g
        vn = ADAM_B2 * v_ref[...] + (1.0 - ADAM_B2) * (g * g)
        g_ref[...] = g
        mo_ref[...] = mn
        vo_ref[...] = vn
        d_ref[...] = -ADAM_LR * ((mn / bc1) / (jnp.sqrt(vn / bc2) + ADAM_EPS) + ADAM_WD * w_ref[...])

    tile = pl.BlockSpec((tr, cols), lambda i: (i, 0))
    p_specs = [pl.BlockSpec((p.shape[0], tr, cols), lambda i: (0, i, 0)) if st else tile for p, st in zip(parts, stacked)]
    shp = jax.ShapeDtypeStruct((rows, cols), F32)
    return pl.pallas_call(
        body, name=name, grid=(rows // tr,),
        in_specs=[tile, tile, tile] + p_specs, out_specs=[tile] * 4, out_shape=[shp] * 4,
        compiler_params=_cparams(1),
    )(w, m, v, *parts)


def _mesh_pos():
    return lax.axis_index("x"), lax.axis_index("y"), lax.axis_index("c")


def _other_chips(x, y):
    return [(1 - x, y), (x, 1 - y), (1 - x, 1 - y)]


def _block_of(ref, axis, index, size):
    if axis == 0:
        return ref.at[index]
    return ref.at[:, pl.ds(pl.multiple_of(index * size, 128), size)]


def _all_gather(shards, axes, name):
    n = len(shards)
    per = 7

    def body(*refs):
        ins, outs, done = refs[:n], refs[n:2 * n], refs[2 * n]
        send_sems, recv_sems, local_sems = refs[2 * n + 1:]
        x, y, c = _mesh_pos()
        me, sibling = (x, y, c), (x, y, 1 - c)
        chips = _other_chips(x, y)

        def rows(a, pos):
            return _block_of(outs[a], axes[a], 4 * pos[0] + 2 * pos[1] + pos[2], shards[a].shape[-1])

        def copy(a, k, block, to, src=None):
            return pltpu.make_async_remote_copy(
                src_ref=rows(a, block) if src is None else src, dst_ref=rows(a, block),
                send_sem=send_sems.at[a * per + k], recv_sem=recv_sems.at[a * per + k],
                device_id=to, device_id_type=MESH_IDS)

        mine = [pltpu.make_async_copy(ins[a], rows(a, me), local_sems.at[a]) for a in range(n)]
        for cp in mine:
            cp.start()
        first = []
        for a in range(n):
            first.append(copy(a, 0, me, sibling, src=ins[a]))
            first += [copy(a, 1 + j, me, (*chip, c), src=ins[a]) for j, chip in enumerate(chips)]
        for cp in first:
            cp.start()
        passed = []
        for j, chip in enumerate(chips):
            for a in range(n):
                copy(a, 1 + j, (*chip, c), me).wait_recv()
                fwd = copy(a, 4 + j, (*chip, c), sibling)
                fwd.start()
                passed.append(fwd)
        for a in range(n):
            copy(a, 0, sibling, me).wait_recv()
            for j, chip in enumerate(chips):
                copy(a, 4 + j, (*chip, 1 - c), me).wait_recv()
        for cp in first + passed:
            cp.wait_send()
        for cp in mine:
            cp.wait()
        done[...] = jnp.zeros_like(done)

    def full_shape(s, ax):
        return (N_DEV,) + s.shape if ax == 0 else s.shape[:-1] + (N_DEV * s.shape[-1],)

    any_spec = pl.BlockSpec(memory_space=pl.ANY)
    outs = pl.pallas_call(
        body, name=name,
        in_specs=[any_spec] * n, out_specs=[any_spec] * n + [pl.BlockSpec(memory_space=pltpu.VMEM)],
        out_shape=[jax.ShapeDtypeStruct(full_shape(s, ax), s.dtype) for s, ax in zip(shards, axes)]
        + [jax.ShapeDtypeStruct((SUBLANES, LANES), F32)],
        scratch_shapes=[pltpu.SemaphoreType.DMA((n * per,)), pltpu.SemaphoreType.DMA((n * per,)),
                        pltpu.SemaphoreType.DMA((n,))],
    )(*shards)
    return outs[:n], outs[n]


def _chip_blocks(x, y):
    return [(x, y)] + _other_chips(x, y)


def _sibling_reduce(gs, axis, name):
    g0, n = gs[0], len(gs)
    rows, cols = (g0.shape[1], g0.shape[2]) if axis == 0 else (g0.shape[0], g0.shape[1] // N_DEV)
    chunk = math.gcd(rows, 64)

    def body(*refs):
        g_refs, own_refs, pay_refs = refs[:n], refs[n:2 * n], refs[2 * n:3 * n]
        send_buf, keep_buf, recv_buf, send_sems, recv_sems, stage_sems, keep_sems = refs[3 * n:]
        x, y, c = _mesh_pos()
        sibling = (x, y, 1 - c)
        chips = _chip_blocks(x, y)
        stage, keep, push = [], [], []
        for a in range(n):
            for j, (px, py) in enumerate(chips):
                s = 4 * a + j
                theirs = _block_of(g_refs[a], axis, 4 * px + 2 * py + (1 - c), cols)
                ours = _block_of(g_refs[a], axis, 4 * px + 2 * py + c, cols)
                stage.append(pltpu.make_async_copy(theirs, send_buf.at[s], stage_sems.at[s]))
                keep.append(pltpu.make_async_copy(ours, keep_buf.at[s], keep_sems.at[s]))
                push.append(pltpu.make_async_remote_copy(
                    src_ref=send_buf.at[s], dst_ref=recv_buf.at[s], send_sem=send_sems.at[s],
                    recv_sem=recv_sems.at[s], device_id=sibling, device_id_type=MESH_IDS))
        for cp in stage + keep:
            cp.start()
        for s in range(4 * n):
            stage[s].wait()
            push[s].start()
        for s in range(4 * n):
            push[s].wait_recv()
            keep[s].wait()
            a, j = divmod(s, 4)
            dst = own_refs[a] if j == 0 else pay_refs[a].at[j - 1]

            def add(r, carry, s=s, dst=dst):
                sl = pl.ds(pl.multiple_of(r * chunk, chunk), chunk)
                dst[sl, :] = (keep_buf[s, sl, :] + recv_buf[s, sl, :]).astype(dst.dtype)
                return carry

            lax.fori_loop(0, rows // chunk, add, 0)
        for cp in push:
            cp.wait_send()

    vmem = pl.BlockSpec(memory_space=pltpu.VMEM)
    buf = pltpu.VMEM((4 * n, rows, cols), F32)
    sems = pltpu.SemaphoreType.DMA((4 * n,))
    outs = pl.pallas_call(
        body, name=name,
        in_specs=[pl.BlockSpec(memory_space=pl.ANY)] * n, out_specs=[vmem] * (2 * n),
        out_shape=[jax.ShapeDtypeStruct((rows, cols), F32)] * n + [jax.ShapeDtypeStruct((3, rows, cols), BF16)] * n,
        scratch_shapes=[buf, buf, buf, sems, sems, sems, sems],
        compiler_params=pltpu.CompilerParams(vmem_limit_bytes=VMEM_LIMIT),
    )(*gs)
    return list(zip(outs[:n], outs[n:]))


_HBM_SPEC = pl.BlockSpec(memory_space=pltpu.HBM)
_SEM_SPEC = pl.BlockSpec(memory_space=pltpu.SEMAPHORE)
_SIDE_EFFECT = pltpu.SideEffectType.DATAFLOW_SIDE_EFFECTING


def _exchange_start(name, srcs, lands, plan, n_copies):
    nb = len(srcs) + len(lands)

    def body(*refs):
        bufs, send_sems, recv_sems, token = refs[:nb], refs[nb], refs[nb + 1], refs[-1]
        for cp in plan(bufs[:len(srcs)], bufs[len(srcs):], send_sems, recv_sems):
            cp.start()
        token[...] = jnp.zeros_like(token)

    arrays = list(srcs) + list(lands)
    outs = pl.pallas_call(
        body, name=name,
        out_shape=(pltpu.SemaphoreType.DMA((n_copies,)), pltpu.SemaphoreType.DMA((n_copies,)),
                   *[pltpu.HBM(a.shape, a.dtype) for a in arrays], jax.ShapeDtypeStruct((SUBLANES, LANES), F32)),
        in_specs=[_HBM_SPEC] * nb,
        out_specs=(_SEM_SPEC, _SEM_SPEC, *[_HBM_SPEC] * nb, pl.BlockSpec(memory_space=pltpu.VMEM)),
        input_output_aliases={k: 2 + k for k in range(nb)},
        compiler_params=pltpu.CompilerParams(has_side_effects=_SIDE_EFFECT),
    )(*[pltpu.with_memory_space_constraint(a, pltpu.HBM) for a in arrays])
    return outs[0], outs[1], outs[2:2 + len(srcs)], outs[2 + len(srcs):2 + nb], outs[-1]


def _exchange_wait(name, send_sems, recv_sems, srcs, lands, plan, after):
    nb = len(srcs) + len(lands)
    after = list(after)

    def body(*refs):
        bufs, send_ref, recv_ref = refs[:nb], refs[nb], refs[nb + 1]
        for cp in plan(bufs[:len(srcs)], bufs[len(srcs):], send_ref, recv_ref):
            cp.wait_send()
            cp.wait_recv()

    arrays = list(srcs) + list(lands)
    outs = pl.pallas_call(
        body, name=name,
        out_shape=tuple(pltpu.HBM(a.shape, a.dtype) for a in arrays),
        in_specs=[_HBM_SPEC] * nb + [_SEM_SPEC, _SEM_SPEC] + [pl.BlockSpec(memory_space=pl.ANY)] * len(after),
        out_specs=tuple([_HBM_SPEC] * nb),
        input_output_aliases={k: k for k in range(nb)},
        compiler_params=pltpu.CompilerParams(has_side_effects=_SIDE_EFFECT),
    )(*arrays, send_sems, recv_sems, *after)
    return outs[len(srcs):]


def _gather_plan(axes, sizes):
    def plan(src_refs, land_refs, send_sems, recv_sems):
        x, y, c = _mesh_pos()
        copies = []
        for a, (src, land) in enumerate(zip(src_refs, land_refs)):
            mine = _block_of(land, axes[a], 4 * x + 2 * y + c, sizes[a])
            for k in range(1, N_DEV):
                peer = (1 - x if k & 4 else x, 1 - y if k & 2 else y, 1 - c if k & 1 else c)
                idx = a * (N_DEV - 1) + k - 1
                copies.append(pltpu.make_async_remote_copy(
                    src_ref=src, dst_ref=mine, send_sem=send_sems.at[idx], recv_sem=recv_sems.at[idx],
                    device_id=peer, device_id_type=MESH_IDS))
        return copies
    return plan


def _chip_plan(src_refs, land_refs, send_sems, recv_sems):
    x, y, c = _mesh_pos()
    copies = []
    for a, (src, land) in enumerate(zip(src_refs, land_refs)):
        for j, chip in enumerate(_other_chips(x, y)):
            copies.append(pltpu.make_async_remote_copy(
                src_ref=src.at[j], dst_ref=land.at[j], send_sem=send_sems.at[3 * a + j],
                recv_sem=recv_sems.at[3 * a + j], device_id=(*chip, c), device_id_type=MESH_IDS))
    return copies


def _own_block_placed(shard, axis, me):
    if axis == 0:
        full = lax.empty((N_DEV,) + shard.shape, shard.dtype)
        return lax.dynamic_update_slice(full, shard[None], (me,) + (0,) * shard.ndim)
    full = lax.empty(shard.shape[:-1] + (N_DEV * shard.shape[-1],), shard.dtype)
    return lax.dynamic_update_slice(full, shard, (0,) * (shard.ndim - 1) + (me * shard.shape[-1],))


def _local_step(x, target, mod, win, late_weights, p, grads_ready=None):
    shift1, scale1, gate1, shift2, scale2, gate2 = (mod[k] for k in range(6))

    def after_token(v, token):
        return v if token is None else v + token[0:1, 0:1]
    wa, wx = p["lru_w_a"].astype(BF16), p["lru_w_x"].astype(BF16)
    mask = jnp.tril(jnp.ones((HD, HD), F32))
    wm = (p["sgu_w_s"] * mask).astype(BF16)
    wmt = jnp.swapaxes(wm, 1, 2)
    bst = jnp.transpose(p["sgu_b_s"])

    h1, z = _norm_proj(x, p["norm_mix_g"], scale1, shift1, win, "mix_proj")
    hstate, ya_pre, *rnn_saved = _rnn_fwd(
        z, p["rnn_conv_w"], p["rnn_conv_b"], wa, p["lru_b_a"], wx, p["lru_b_x"], p["lru_lambda"])
    yb_pre = _sgu_fwd(z, p["sgu_ln_g"], p["sgu_ln_b"], wm, bst)
    wba, wbb, wout = late_weights("merge", [ya_pre, yb_pre])
    x2, ya, yb, merged, o1 = _merge_fwd(ya_pre, yb_pre, z, x, gate1, wba, wbb, wout)
    wup = late_weights("ffn_up", [x2])
    h2, up, ff, fa, fv = _ffn_proj_mid(x2, p["norm_ffn_g"], scale2, shift2, wup, p["ffn_conv_w"], p["ffn_conv_b"])
    wd = late_weights("ffn_down", [ff])
    dx3, loss, d_gfin, d_gate2 = _ffn_out_loss(ff, wd, x2, target, gate2, p["norm_final_g"])

    dact, dval, d_wd, dcb_a, dcb_v = _ffn_down_bwd(dx3, gate2, ff, fa, fv, wd)
    dup, dx2, do1, d_cwf, d_shift2, d_scale2, d_gffn, d_gate1 = _ffn_up_bwd(
        dact, dval, up, p["ffn_conv_w"], wup, x2, dx3, p["norm_ffn_g"], scale2, o1, gate1)
    d_wup = _xt_y(h2, dup, "w_up_grad")
    ready = grads_ready if grads_ready else (lambda stage, big, small: None)
    token = ready("ffn", {"w_up": d_wup, "w_down": d_wd}, {})

    dya, dyb, dz, d_wout, d_win = _out_bwd(do1, wout, merged, ya, yb, z, h1)
    dz, d_win, d_wba, d_cw, d_cb, d_wa, d_ba, d_wx, d_bx, d_lam = _rnn_bwd(
        dya, ya_pre, wba, h1, z, rnn_saved, hstate, dz, d_win, p["rnn_conv_w"], wa, wx,
        after_token(p["lru_lambda"], token))
    small = {
        "rnn_conv_w": d_cw, "rnn_conv_b": d_cb, "lru_w_a": d_wa, "lru_b_a": d_ba, "lru_w_x": d_wx, "lru_b_x": d_bx,
        "lru_lambda": d_lam, "norm_ffn_g": d_gffn, "ffn_conv_w": d_cwf,
        "ffn_conv_b": jnp.concatenate([dcb_a, dcb_v], axis=1), "norm_final_g": d_gfin,
    }
    token = ready("rnn", {}, small)
    dz, d_win, d_wbb, d_ws, d_bst, d_lng, d_lnb = _sgu_bwd(
        dyb, yb_pre, wbb, h1, z, dz, d_win, p["sgu_ln_g"], after_token(p["sgu_ln_b"], token), wm, wmt, bst, mask)
    sgu_small = {"sgu_ln_g": d_lng, "sgu_ln_b": d_lnb, "sgu_w_s": d_ws, "sgu_b_s": jnp.transpose(d_bst)}
    mixer = {"w_in": d_win, "w_out": d_wout, "w_branch_a": d_wba, "w_branch_b": d_wbb}
    token = ready("mixer", mixer, sgu_small)
    grad_x, d_shift1, d_scale1, d_gmix = _in_bwd(dz, win, x, dx2, after_token(p["norm_mix_g"], token), scale1)

    small.update(sgu_small)
    small["norm_mix_g"] = d_gmix
    dmod = jnp.stack([d_shift1, d_scale1, d_gate1, d_shift2, d_scale2, d_gate2])
    big = {"w_in": d_win, "w_up": d_wup, "w_branch_a": d_wba, "w_branch_b": d_wbb, "w_out": d_wout, "w_down": d_wd}
    return loss, grad_x, big, small, dmod


LAST_REP = ["b_ada", "norm_mix_g"]
EARLY_REP = ["rnn_conv_b", "lru_w_a", "lru_b_a", "lru_w_x", "lru_b_x", "lru_lambda", "norm_ffn_g", "ffn_conv_b",
             "norm_final_g"]
MID_REP = ["sgu_ln_g", "sgu_ln_b", "sgu_w_s", "sgu_b_s"]
COL_SHARDED = ["rnn_conv_w", "ffn_conv_w"]
SMALL_GROUPS = {"rnn": EARLY_REP + COL_SHARDED, "mixer": MID_REP, "last": LAST_REP}
REPLICATED = LAST_REP + EARLY_REP + MID_REP
SMALL_NAMES = REPLICATED + COL_SHARDED
BIG_NAMES = ["w_in", "w_up", "w_branch_a", "w_branch_b", "w_out", "w_down"]
BIG_AXES = [1, 1, 0, 0, 0, 0]
WEIGHTS = ["w_ada", "b_ada", "norm_mix_g", "w_in", "rnn_conv_w", "rnn_conv_b", "lru_w_a", "lru_b_a", "lru_w_x",
           "lru_b_x", "lru_lambda", "sgu_ln_g", "sgu_ln_b", "sgu_w_s", "sgu_b_s", "w_branch_a", "w_branch_b",
           "w_out", "norm_ffn_g", "w_up", "ffn_conv_w", "ffn_conv_b", "w_down", "norm_final_g"]
LANES = 128


def _pack_rows(shape):
    return math.prod(shape) // LANES


def _pack(arrays):
    return jnp.concatenate([a.reshape(-1, LANES) for a in arrays], axis=0)


def _unpack(packed, shapes):
    out, r0 = [], 0
    for s in shapes:
        nrow = math.prod(s) // LANES
        out.append(packed[r0:r0 + nrow].reshape(s))
        r0 += nrow
    return out


def kernel(x, c, w_ada, b_ada, norm_mix_g, w_in, rnn_conv_w, rnn_conv_b, lru_w_a, lru_b_a, lru_w_x, lru_b_x, lru_lambda, sgu_ln_g, sgu_ln_b, sgu_w_s, sgu_b_s, w_branch_a, w_branch_b, w_out, norm_ffn_g, w_up, ffn_conv_w, ffn_conv_b, w_down, norm_final_g, loss_target, m_w_ada, m_b_ada, m_norm_mix_g, m_w_in, m_rnn_conv_w, m_rnn_conv_b, m_lru_w_a, m_lru_b_a, m_lru_w_x, m_lru_b_x, m_lru_lambda, m_sgu_ln_g, m_sgu_ln_b, m_sgu_w_s, m_sgu_b_s, m_w_branch_a, m_w_branch_b, m_w_out, m_norm_ffn_g, m_w_up, m_ffn_conv_w, m_ffn_conv_b, m_w_down, m_norm_final_g, v_w_ada, v_b_ada, v_norm_mix_g, v_w_in, v_rnn_conv_w, v_rnn_conv_b, v_lru_w_a, v_lru_b_a, v_lru_w_x, v_lru_b_x, v_lru_lambda, v_sgu_ln_g, v_sgu_ln_b, v_sgu_w_s, v_sgu_b_s, v_w_branch_a, v_w_branch_b, v_w_out, v_norm_ffn_g, v_w_up, v_ffn_conv_w, v_ffn_conv_b, v_w_down, v_norm_final_g):
    given = dict(locals())
    me = 4 * lax.axis_index("x") + 2 * lax.axis_index("y") + lax.axis_index("c")
    ada_cols = w_ada.shape[2]
    conv_cols = {"rnn_conv_w": rnn_conv_w.shape[2], "ffn_conv_w": ffn_conv_w.shape[2]}

    (win, c_all, cw_rnn, cw_ffn), _ = _all_gather(
        [w_in[0].astype(BF16), c.reshape(1, 1, D), rnn_conv_w[0], ffn_conv_w[0]], [1, 0, 1, 1], "gather_first")
    c_all = c_all.reshape(N_DEV, D)

    b_cols = lax.dynamic_slice_in_dim(b_ada, me * ada_cols, ada_cols, axis=1)
    (mod_all,), mod_done = _all_gather(
        [_mod_cols(c_all, w_ada[0], b_cols).reshape(1, N_DEV, ada_cols)], [0], "gather_mod")
    mod_all = mod_all.reshape(N_DEV, N_DEV, ada_cols)
    mod_mine = lax.dynamic_index_in_dim(mod_all, me, axis=1, keepdims=False).reshape(6, 1, D)

    late_groups = {"merge": (["w_branch_a", "w_branch_b", "w_out"], [0, 0, 0]), "ffn_up": (["w_up"], [1]),
                   "ffn_down": (["w_down"], [0])}
    in_flight, started = {}, mod_done[0:1, 0:1]
    for stage, (names, axes) in late_groups.items():
        shards = [(given[n][0] + started).astype(BF16) for n in names]
        plan = _gather_plan(axes, [s.shape[-1] for s in shards])
        send, recv, srcs, lands, token = _exchange_start(
            "gather_start_" + stage, shards, [_own_block_placed(s, ax, me) for s, ax in zip(shards, axes)], plan,
            len(shards) * (N_DEV - 1))
        in_flight[stage] = (send, recv, srcs, lands, plan)
        started = started + token[0:1, 0:1]

    def late_weights(stage, after):
        send, recv, srcs, lands, plan = in_flight[stage]
        full = _exchange_wait("gather_wait_" + stage, send, recv, srcs, lands, plan, after)
        full = [w.reshape(-1, D) if ax == 0 else w for w, ax in zip(full, late_groups[stage][1])]
        return full if len(full) > 1 else full[0]

    mod_mine = mod_mine + started

    reducing, packing = {}, {}

    def start_pack(stage, small):
        pack = _pack([small[n] for n in SMALL_GROUPS[stage]])[None]
        plan = _gather_plan([0], [LANES])
        send, recv, srcs, lands, tok = _exchange_start(
            "small_start_" + stage, [pack], [_own_block_placed(pack, 0, me)], plan, N_DEV - 1)
        packing[stage] = (send, recv, srcs, lands, plan)
        return tok

    def grads_ready(stage, grads, small):
        tokens = [start_pack(stage, small)] if small else []
        if grads:
            tokens.append(start_reduce(stage, grads))
        return sum(tokens[1:], tokens[0])

    def start_reduce(stage, grads):
        names = [n for n in BIG_NAMES if n in grads]
        blocked = {}
        for n in names:
            ax = BIG_AXES[BIG_NAMES.index(n)]
            g = grads[n] if ax == 1 else grads[n].reshape(N_DEV, grads[n].shape[0] // N_DEV, grads[n].shape[1])
            blocked.setdefault((ax, g.shape), []).append((n, g))
        sums = {}
        for (ax, _), group in blocked.items():
            reduced = _sibling_reduce([g for _, g in group], ax, "reduce_sibling_" + "_".join(n for n, _ in group))
            sums.update({n: r for (n, _), r in zip(group, reduced)})
        sums = [sums[n] for n in names]
        pays = [pay for _, pay in sums]
        send, recv, srcs, lands, tok = _exchange_start(
            "reduce_start_" + stage, pays, [lax.empty(p_.shape, p_.dtype) for p_ in pays], _chip_plan, 3 * len(pays))
        reducing[stage] = (names, [own for own, _ in sums], send, recv, srcs, lands)
        return tok

    p = {n: given[n][0] for n in REPLICATED if n not in ("b_ada", "norm_final_g")}
    p = {n: (a.reshape(1, -1) if a.ndim == 1 else a) for n, a in p.items()}
    p["rnn_conv_w"], p["ffn_conv_w"] = cw_rnn, cw_ffn
    p["norm_final_g"] = norm_final_g.reshape(1, D)
    loss, grad_x, _, small, dmod = _local_step(x[0], loss_target[0], mod_mine, win, late_weights, p, grads_ready)

    small["b_ada"] = dmod.reshape(1, 6 * D)
    rows_of = {n: _pack_rows(small[n].shape) for n in SMALL_NAMES}
    start_of = {n: sum(rows_of[q] for q in SMALL_NAMES[:k]) for k, n in enumerate(SMALL_NAMES)}
    (last,), _ = _all_gather([_pack([small[n] for n in LAST_REP])[None]], [0], "gather_small")
    gathered = {"last": last}
    for stage, (send, recv, srcs, lands, plan) in packing.items():
        (gathered[stage],) = _exchange_wait("small_wait_" + stage, send, recv, srcs, lands, plan, [grad_x])
    gathered = {k: v.reshape(N_DEV, -1, LANES) for k, v in gathered.items()}
    early_rows = sum(rows_of[n] for n in EARLY_REP)
    packs = jnp.concatenate([gathered["last"], gathered["rnn"][:, :early_rows], gathered["mixer"],
                             gathered["rnn"][:, early_rows:]], axis=1)

    out = {}
    for stage, (names, owns, send, recv, srcs, lands) in reducing.items():
        landed = _exchange_wait("reduce_wait_" + stage, send, recv, srcs, lands, _chip_plan, [packs])
        for n, own, got in zip(names, owns, landed):
            out[n] = _adamw(given[n][0], given["m_" + n][0], given["v_" + n][0], [own, got], "adamw_" + n)

    dmod_all = packs[:, :rows_of["b_ada"]].reshape(N_DEV, 6 * D)
    dmod_cols = lax.dynamic_slice_in_dim(dmod_all, me * ada_cols, ada_cols, axis=1)
    out["w_ada"] = _adamw(w_ada[0], m_w_ada[0], v_w_ada[0], [_ada_grad(c_all, dmod_cols)], "adamw_w_ada")

    rep_rows = sum(rows_of[n] for n in REPLICATED)
    res = _adamw(*[_pack([given[pre + n] for n in REPLICATED]) for pre in ("", "m_", "v_")],
                 [packs[:, :rep_rows]], "adamw_small")
    unpacked = [_unpack(r, [given[n].shape for n in REPLICATED]) for r in res]
    for k, n in enumerate(REPLICATED):
        out[n] = tuple(u[k] for u in unpacked)

    for n in COL_SHARDED:
        full = packs[:, start_of[n]:start_of[n] + rows_of[n]].reshape(N_DEV, small[n].shape[0], small[n].shape[1])
        mine = lax.dynamic_slice_in_dim(full, me * conv_cols[n], conv_cols[n], axis=2)
        out[n] = _adamw(given[n][0], given["m_" + n][0], given["v_" + n][0], [mine], "adamw_" + n)

    total = lax.psum(loss[0, 0], ("x", "y", "c"))
    results = [total, grad_x[None]]
    for kind in range(4):
        results += [out[n][kind].reshape(given[n].shape) for n in WEIGHTS]
    return tuple(results)
```

```python
import math

import jax
import jax.numpy as jnp
from jax import lax
from jax.experimental import pallas as pl
from jax.experimental.pallas import tpu as pltpu

F32 = jnp.float32
BF16 = jnp.bfloat16
MESH_IDS = pl.DeviceIdType.MESH

D = 1024
NH = 8
HD = 128
NCOL_IN = 6 * D
DFF = 3 * D
N_DEV = 8
EPS = 1e-6
LRU_C = 8.0
ADAM_LR, ADAM_B1, ADAM_B2, ADAM_EPS, ADAM_WD, ADAM_STEP = 0.001, 0.9, 0.999, 1e-08, 0.01, 10

SUBLANES = 8
HALO = 16
VMEM_LIMIT = 56 * 1024 * 1024
GELU_K = math.sqrt(2.0 / math.pi)
GELU_C = 0.044715


def _cparams(n_axes):
    return pltpu.CompilerParams(dimension_semantics=("arbitrary",) * n_axes, vmem_limit_bytes=VMEM_LIMIT)


def _const_spec(shape, single_buffer=False):
    nd = len(shape)
    if single_buffer:
        return pl.BlockSpec(shape, lambda *_: (0,) * nd, pipeline_mode=pl.Buffered(1))
    return pl.BlockSpec(shape, lambda *_: (0,) * nd)


def _tile_big(t):
    return min(512, t)


def _tile_seq(t):
    return min(256, t)


def _row_tile(rows, cols):
    cap = max(SUBLANES, (2 * 1024 * 1024) // (4 * cols) // SUBLANES * SUBLANES)
    if rows <= cap:
        return rows
    return next(tr for tr in range(cap, 0, -SUBLANES) if rows % tr == 0)


def _gelu_t(x):
    x2 = x * x
    t = jnp.tanh(x * (GELU_K + (GELU_K * GELU_C) * x2))
    hx = 0.5 * x
    return hx + hx * t, (x2, hx, t)


def _gelu_grad(shared):
    x2, hx, t = shared
    return (0.5 + 0.5 * t) + (hx * (1.0 - t * t)) * (GELU_K + (3.0 * GELU_K * GELU_C) * x2)


def _sigmoid(x):
    return 1.0 / (1.0 + jnp.exp(-x))


def _log_sigmoid(x):
    return -(jnp.maximum(-x, 0.0) + jnp.log1p(jnp.exp(-jnp.abs(x))))


def _row_iota(cols):
    return lax.broadcasted_iota(jnp.int32, (SUBLANES, cols), 0)


def _shift_down(x, k, prev8):
    if k == 0:
        return x
    r = pltpu.roll(x, k, 0)
    p = pltpu.roll(prev8, k, 0)
    head = jnp.where(_row_iota(x.shape[1]) < k, p, r[:SUBLANES])
    return jnp.concatenate([head, r[SUBLANES:]], axis=0)


def _shift_up(x, k, next8):
    if k == 0:
        return x
    n = x.shape[0]
    r = pltpu.roll(x, n - k, 0)
    q = pltpu.roll(next8, SUBLANES - k, 0)
    tail = jnp.where(_row_iota(x.shape[1]) >= SUBLANES - k, q, r[n - SUBLANES:])
    return jnp.concatenate([r[:n - SUBLANES], tail], axis=0)


def _heads_nn(x_bf, w_ref):
    return jnp.concatenate(
        [jnp.dot(x_bf[:, h * HD:(h + 1) * HD], w_ref[h], preferred_element_type=F32) for h in range(NH)], axis=1)


def _heads_nt(x_bf, w_ref):
    return jnp.concatenate(
        [lax.dot_general(x_bf[:, h * HD:(h + 1) * HD], w_ref[h], (((1,), (1,)), ((), ())), preferred_element_type=F32)
         for h in range(NH)], axis=1)


def _dot_nt(a, b):
    return lax.dot_general(a, b, (((1,), (1,)), ((), ())), preferred_element_type=F32)


def _dot_tn(a, b):
    return lax.dot_general(a, b, (((0,), (0,)), ((), ())), preferred_element_type=F32)


def _colsum(x):
    return jnp.sum(x, axis=0, keepdims=True)


def _prev_halo_map(tm, col):
    return lambda i, *_: (jnp.maximum(i * (tm // HALO) - 1, 0), col)


def _norm_proj(x, g, scale, shift, w, name):
    t, n = x.shape[0], w.shape[1]
    tm = _tile_big(t)

    def body(x_ref, g_ref, sc_ref, sh_ref, w_ref, h_ref, z_ref):
        xv = x_ref[...]
        r = lax.rsqrt(jnp.mean(xv * xv, axis=-1, keepdims=True) + EPS)
        hb = ((xv * r * g_ref[...]) * (1.0 + sc_ref[...]) + sh_ref[...]).astype(BF16)
        h_ref[...] = hb
        for c0 in range(0, n, D):
            z_ref[:, c0:c0 + D] = jnp.dot(hb, w_ref[:, c0:c0 + D], preferred_element_type=F32).astype(BF16)

    vec = _const_spec((1, D))
    return pl.pallas_call(
        body, name=name, grid=(t // tm,),
        in_specs=[pl.BlockSpec((tm, D), lambda i: (i, 0)), vec, vec, vec, _const_spec((D, n), True)],
        out_specs=[pl.BlockSpec((tm, D), lambda i: (i, 0)), pl.BlockSpec((tm, n), lambda i: (i, 0))],
        out_shape=[jax.ShapeDtypeStruct((t, D), BF16), jax.ShapeDtypeStruct((t, n), BF16)],
        compiler_params=_cparams(1),
    )(x, g, scale, shift, w)


def _lru_gates(xc, wa_ref, ba, wx_ref, bx, ls):
    xb = xc.astype(BF16)
    ra = _sigmoid(_heads_nn(xb, wa_ref) + ba)
    ia = _sigmoid(_heads_nn(xb, wx_ref) + bx)
    la = LRU_C * ra * ls
    a = jnp.exp(la)
    mult = jnp.sqrt(-jnp.tanh(la) * (1.0 + a * a))
    return ra, ia, a, mult


def _conv4(xr, prev8, cw_ref, cb):
    return (cb + cw_ref[3:4, :] * xr + cw_ref[2:3, :] * _shift_down(xr, 1, prev8)
            + cw_ref[1:2, :] * _shift_down(xr, 2, prev8) + cw_ref[0:1, :] * _shift_down(xr, 3, prev8))


def _rnn_fwd(z, cw, cb, wa, ba, wx, bx, lam):
    t = z.shape[0]
    tm = _tile_seq(t)
    ngrp = tm // SUBLANES

    def body(xr_ref, xp_ref, gr_ref, cw_ref, cb_ref, wa_ref, ba_ref, wx_ref, bx_ref, lam_ref,
             h_ref, ya_ref, xc_ref, ra_ref, ia_ref, gg_ref, hg_ref, carry_ref, a_scr, u_scr):
        i = pl.program_id(0)

        @pl.when(i == 0)
        def _():
            carry_ref[...] = jnp.zeros_like(carry_ref)

        xr = xr_ref[...].astype(F32)
        prev8 = jnp.where(i == 0, 0.0, xp_ref[...].astype(F32)[HALO - SUBLANES:])
        xc = _conv4(xr, prev8, cw_ref, cb_ref[...])
        ra, ia, a, mult = _lru_gates(xc, wa_ref, ba_ref[...], wx_ref, bx_ref[...], _log_sigmoid(lam_ref[...]))
        xc_ref[...] = xc.astype(BF16)
        ra_ref[...] = ra.astype(BF16)
        ia_ref[...] = ia.astype(BF16)
        a_scr[...] = a
        u_scr[...] = mult * (ia * xc)
        row = _row_iota(D)

        def grp(j, carry):
            r0 = pl.multiple_of(j * SUBLANES, SUBLANES)
            av = a_scr[pl.ds(r0, SUBLANES), :]
            uv = u_scr[pl.ds(r0, SUBLANES), :]
            for d in (1, 2, 4):
                m = row >= d
                uv = jnp.where(m, av * pltpu.roll(uv, d, 0) + uv, uv)
                av = jnp.where(m, av * pltpu.roll(av, d, 0), av)
            hv = uv + av * carry
            h_ref[pl.ds(r0, SUBLANES), :] = hv
            return hv[SUBLANES - 1:SUBLANES, :]

        carry_ref[0:1, :] = lax.fori_loop(0, ngrp, grp, carry_ref[0:1, :])
        grv = gr_ref[...].astype(F32)
        gg, tg = _gelu_t(grv)
        hv = h_ref[...]
        ya_ref[...] = (hv * gg).astype(BF16)
        gg_ref[...] = gg.astype(BF16)
        hg_ref[...] = (hv * _gelu_grad(tg)).astype(BF16)

    vec = _const_spec((1, D))
    wspec = _const_spec((NH, HD, HD))
    tile = pl.BlockSpec((tm, D), lambda i: (i, 0))
    bshape = jax.ShapeDtypeStruct((t, D), BF16)
    return pl.pallas_call(
        body, name="rnn_fwd", grid=(t // tm,),
        in_specs=[tile, pl.BlockSpec((HALO, D), _prev_halo_map(tm, 0)),
                  pl.BlockSpec((tm, D), lambda i: (i, 1)), _const_spec((4, D)), vec, wspec, vec, wspec, vec, vec],
        out_specs=[tile] * 7,
        out_shape=[jax.ShapeDtypeStruct((t, D), F32)] + [bshape] * 6,
        scratch_shapes=[pltpu.VMEM((SUBLANES, D), F32), pltpu.VMEM((tm, D), F32), pltpu.VMEM((tm, D), F32)],
        compiler_params=_cparams(1),
    )(z, z, z, cw, cb, wa, ba, wx, bx, lam)


def _sgu_core(zu, zv, lng, lnb, wm_ref, bst_ref):
    gu, tu = _gelu_t(zu)
    gv, tv = _gelu_t(zv)
    mu = jnp.mean(gv, axis=-1, keepdims=True)
    cen = gv - mu
    rstd = lax.rsqrt(jnp.mean(cen * cen, axis=-1, keepdims=True) + EPS)
    vhat = cen * rstd
    vln = vhat * lng + lnb
    vb = vln.astype(BF16)
    rows = []
    for b0 in range(0, zu.shape[0], HD):
        rows.append(jnp.concatenate(
            [jnp.dot(wm_ref[g], vb[b0:b0 + HD, g * HD:(g + 1) * HD], preferred_element_type=F32)
             + bst_ref[:, g:g + 1] for g in range(NH)], axis=1))
    mixed = jnp.concatenate(rows, axis=0) if len(rows) > 1 else rows[0]
    return gu, tu, tv, rstd, vhat, vb, mixed


def _sgu_fwd(z, lng, lnb, wm, bst):
    t = z.shape[0]
    tm = _tile_seq(t)

    def body(zu_ref, zv_ref, lng_ref, lnb_ref, wm_ref, bst_ref, yb_ref):
        gu, _, _, _, _, _, mixed = _sgu_core(zu_ref[...].astype(F32), zv_ref[...].astype(F32),
                                             lng_ref[...], lnb_ref[...], wm_ref, bst_ref)
        yb_ref[...] = (gu * mixed).astype(BF16)

    vec = _const_spec((1, D))
    return pl.pallas_call(
        body, name="sgu_fwd", grid=(t // tm,),
        in_specs=[pl.BlockSpec((tm, D), lambda i: (i, 2)), pl.BlockSpec((tm, D), lambda i: (i, 3)), vec, vec,
                  _const_spec((NH, HD, HD)), _const_spec((HD, NH))],
        out_specs=pl.BlockSpec((tm, D), lambda i: (i, 0)),
        out_shape=jax.ShapeDtypeStruct((t, D), BF16),
        compiler_params=_cparams(1),
    )(z, z, lng, lnb, wm, bst)


def _merge_fwd(ya_pre, yb_pre, z, x, gate1, wba, wbb, wout):
    t = x.shape[0]
    tm = _tile_big(t)

    def body(yap_ref, ybp_ref, ga_ref, gb_ref, x_ref, g1_ref, wba_ref, wbb_ref, wo_ref,
             x2_ref, ya_ref, yb_ref, mg_ref, o1_ref):
        ya = jnp.dot(yap_ref[...], wba_ref[...], preferred_element_type=F32)
        yb = jnp.dot(ybp_ref[...], wbb_ref[...], preferred_element_type=F32)
        merged = _sigmoid(ga_ref[...].astype(F32)) * ya + _sigmoid(gb_ref[...].astype(F32)) * yb
        mb = merged.astype(BF16)
        o1 = jnp.dot(mb, wo_ref[...], preferred_element_type=F32)
        x2_ref[...] = x_ref[...] + g1_ref[...] * o1
        ya_ref[...] = ya.astype(BF16)
        yb_ref[...] = yb.astype(BF16)
        mg_ref[...] = mb
        o1_ref[...] = o1.astype(BF16)

    tile = pl.BlockSpec((tm, D), lambda i: (i, 0))
    wspec = _const_spec((D, D))
    bshape = jax.ShapeDtypeStruct((t, D), BF16)
    return pl.pallas_call(
        body, name="merge_fwd", grid=(t // tm,),
        in_specs=[tile, tile, pl.BlockSpec((tm, D), lambda i: (i, 4)), pl.BlockSpec((tm, D), lambda i: (i, 5)),
                  tile, _const_spec((1, D)), wspec, wspec, wspec],
        out_specs=[tile] * 5,
        out_shape=[jax.ShapeDtypeStruct((t, D), F32), bshape, bshape, bshape, bshape],
        compiler_params=_cparams(1),
    )(ya_pre, yb_pre, z, z, x, gate1, wba, wbb, wout)


def _conv3(u, prev8, cw_ref, cb):
    return cb + cw_ref[2:3, :] * u + cw_ref[1:2, :] * _shift_down(u, 1, prev8) + cw_ref[0:1, :] * _shift_down(u, 2, prev8)


def _ffn_proj_mid(x2, g, scale, shift, w, cw, cb):
    t = x2.shape[0]
    tm = _tile_big(t)
    nc = DFF // D

    def body(x_ref, g_ref, sc_ref, sh_ref, wa_ref, wv_ref, cwa_ref, cwv_ref, cba_ref, cbv_ref,
             h_ref, upa_ref, upv_ref, ff_ref, fa_ref, fv_ref, hb_scr, prev_ref):
        i, c = pl.program_id(0), pl.program_id(1)

        @pl.when(i == 0)
        def _():
            prev_ref[c] = jnp.zeros((2, SUBLANES, D), F32)

        @pl.when(c == 0)
        def _():
            xv = x_ref[...]
            r = lax.rsqrt(jnp.mean(xv * xv, axis=-1, keepdims=True) + EPS)
            hb_scr[...] = ((xv * r * g_ref[...]) * (1.0 + sc_ref[...]) + sh_ref[...]).astype(BF16)
            h_ref[...] = hb_scr[...]

        hb = hb_scr[...]
        halves = []
        for s, (w_ref, up_ref, cw_ref, cb_ref) in enumerate(((wa_ref, upa_ref, cwa_ref, cba_ref),
                                                             (wv_ref, upv_ref, cwv_ref, cbv_ref))):
            u = jnp.dot(hb, w_ref[...], preferred_element_type=F32)
            up_ref[...] = u.astype(BF16)
            halves.append(_conv3(u, prev_ref[c, s], cw_ref, cb_ref[...]))
            prev_ref[c, s] = u[tm - SUBLANES:]
        act, val = halves
        ga, ta = _gelu_t(act)
        ff_ref[...] = (ga * val).astype(BF16)
        fa_ref[...] = (val * _gelu_grad(ta)).astype(BF16)
        fv_ref[...] = ga.astype(BF16)

    def cols(rows, off):
        return pl.BlockSpec((rows, D), lambda i, c: (0, off + c))

    vec = pl.BlockSpec((1, D), lambda i, c: (0, 0))
    row_tile = pl.BlockSpec((tm, D), lambda i, c: (i, 0))
    chunk = pl.BlockSpec((tm, D), lambda i, c: (i, c))
    hshape = jax.ShapeDtypeStruct((t, DFF), BF16)
    return pl.pallas_call(
        body, name="ffn_proj_mid", grid=(t // tm, nc),
        in_specs=[row_tile, vec, vec, vec, cols(D, 0), cols(D, nc), cols(3, 0), cols(3, nc), cols(1, 0), cols(1, nc)],
        out_specs=[row_tile, chunk, chunk, chunk, chunk, chunk],
        out_shape=[jax.ShapeDtypeStruct((t, D), BF16), hshape, hshape, hshape, hshape, hshape],
        scratch_shapes=[pltpu.VMEM((tm, D), BF16), pltpu.VMEM((nc, 2, SUBLANES, D), F32)],
        compiler_params=_cparams(2),
    )(x2, g, scale, shift, w, w, cw, cw, cb, cb)


def _ffn_out_loss(ff, wd, x2, target, gate2, gfin):
    t = x2.shape[0]
    tm = _tile_big(t)

    def body(ff_ref, wd_ref, x2_ref, tg_ref, g2_ref, gf_ref, dx3_ref, loss_ref, dgf_ref, dg2_ref):
        @pl.when(pl.program_id(0) == 0)
        def _():
            loss_ref[...] = jnp.zeros_like(loss_ref)
            dgf_ref[...] = jnp.zeros_like(dgf_ref)
            dg2_ref[...] = jnp.zeros_like(dg2_ref)

        o2 = jnp.dot(ff_ref[...], wd_ref[...], preferred_element_type=F32)
        x3 = x2_ref[...] + g2_ref[...] * o2
        r = lax.rsqrt(jnp.mean(x3 * x3, axis=-1, keepdims=True) + EPS)
        xhat = x3 * r
        err = xhat * gf_ref[...] - tg_ref[...]
        loss_ref[...] += 0.5 * jnp.sum(jnp.mean(err * err, axis=-1, keepdims=True), axis=0, keepdims=True)
        dy = err * (1.0 / D)
        dgf_ref[...] += _colsum(dy * xhat)
        dxh = dy * gf_ref[...]
        dx3 = r * (dxh - xhat * jnp.mean(dxh * xhat, axis=-1, keepdims=True))
        dx3_ref[...] = dx3
        dg2_ref[...] += _colsum(dx3 * o2)

    tile = pl.BlockSpec((tm, D), lambda i: (i, 0))
    vec = _const_spec((1, D))
    return pl.pallas_call(
        body, name="ffn_out_loss", grid=(t // tm,),
        in_specs=[pl.BlockSpec((tm, DFF), lambda i: (i, 0)), _const_spec((DFF, D), True), tile, tile, vec, vec],
        out_specs=[tile, _const_spec((1, 1)), vec, vec],
        out_shape=[jax.ShapeDtypeStruct((t, D), F32), jax.ShapeDtypeStruct((1, 1), F32),
                   jax.ShapeDtypeStruct((1, D), F32), jax.ShapeDtypeStruct((1, D), F32)],
        compiler_params=_cparams(1),
    )(ff, wd, x2, target, gate2, gfin)


def _ffn_down_bwd(dx3, gate2, ff, fa, fv, wd):
    t = dx3.shape[0]
    tm = _tile_big(t)
    nc = DFF // D

    def body(dx3_ref, g2_ref, ff_ref, fa_ref, fv_ref, wd_ref, da_ref, dv_ref, dwd_ref, dcba_ref, dcbv_ref):
        @pl.when(pl.program_id(1) == 0)
        def _():
            for r in (dwd_ref, dcba_ref, dcbv_ref):
                r[...] = jnp.zeros_like(r)

        do2 = (dx3_ref[...] * g2_ref[...]).astype(BF16)
        dwd_ref[...] += _dot_tn(ff_ref[...], do2)
        dff = _dot_nt(do2, wd_ref[...])
        dact = dff * fa_ref[...].astype(F32)
        dval = dff * fv_ref[...].astype(F32)
        da_ref[...] = dact.astype(BF16)
        dv_ref[...] = dval.astype(BF16)
        dcba_ref[...] += _colsum(dact)
        dcbv_ref[...] += _colsum(dval)

    blk = pl.BlockSpec((tm, D), lambda c, i: (i, c))
    vec = pl.BlockSpec((1, D), lambda c, i: (0, c))
    return pl.pallas_call(
        body, name="ffn_down_bwd", grid=(nc, t // tm),
        in_specs=[pl.BlockSpec((tm, D), lambda c, i: (i, 0)), pl.BlockSpec((1, D), lambda c, i: (0, 0)),
                  blk, blk, blk, pl.BlockSpec((D, D), lambda c, i: (c, 0))],
        out_specs=[blk, blk, pl.BlockSpec((D, D), lambda c, i: (c, 0)), vec, vec],
        out_shape=[jax.ShapeDtypeStruct((t, DFF), BF16), jax.ShapeDtypeStruct((t, DFF), BF16),
                   jax.ShapeDtypeStruct((DFF, D), F32),
                   jax.ShapeDtypeStruct((1, DFF), F32), jax.ShapeDtypeStruct((1, DFF), F32)],
        compiler_params=_cparams(2),
    )(dx3, gate2, ff, fa, fv, wd)


def _modnorm_bwd(dh, xv, g, scale):
    r = lax.rsqrt(jnp.mean(xv * xv, axis=-1, keepdims=True) + EPS)
    xhat = xv * r
    dxn = dh * (1.0 + scale)
    dxh = dxn * g
    dx = r * (dxh - xhat * jnp.mean(dxh * xhat, axis=-1, keepdims=True))
    return dx, _colsum(dh), _colsum(dh * (xhat * g)), _colsum(dxn * xhat)


def _ffn_up_bwd(dact, dval, up_a, up_v, cw, wup, x2, dx3, gffn, scale2, o1, gate1):
    t = x2.shape[0]
    tm = _tile_seq(t)
    nt = t // tm
    nc = DFF // D

    def body(da_ref, dan_ref, dv_ref, dvn_ref, ua_ref, uv_ref, cw_ref, w_ref, x2_ref, dx3_ref, g_ref, sc_ref, o1_ref, g1_ref,
             dup_ref, dx2_ref, do1_ref, dcw_ref, dsh_ref, dsc_ref, dg_ref, dg1_ref):
        i = pl.program_id(0)

        @pl.when(i == 0)
        def _():
            for r in (dcw_ref, dsh_ref, dsc_ref, dg_ref, dg1_ref):
                r[...] = jnp.zeros_like(r)

        last = i == nt - 1
        dh = jnp.zeros((tm, D), F32)
        for half, (d_ref, dn_ref, u_ref) in enumerate(((da_ref, dan_ref, ua_ref), (dv_ref, dvn_ref, uv_ref))):
            nxt = jnp.where(last, 0.0, dn_ref[...].astype(F32)[:SUBLANES])
            for c in range(nc):
                c0 = half * DFF + c * D
                dv = d_ref[:, c * D:(c + 1) * D].astype(F32)
                nx = nxt[:, c * D:(c + 1) * D]
                taps = (_shift_up(dv, 2, nx), _shift_up(dv, 1, nx), dv)
                dup = (cw_ref[2:3, c0:c0 + D] * taps[2] + cw_ref[1:2, c0:c0 + D] * taps[1]
                       + cw_ref[0:1, c0:c0 + D] * taps[0]).astype(BF16)
                upv = u_ref[:, c * D:(c + 1) * D].astype(F32)
                for k in range(3):
                    dcw_ref[k:k + 1, c0:c0 + D] += _colsum(taps[k] * upv)
                dup_ref[:, c0:c0 + D] = dup
                dh = dh + _dot_nt(dup, w_ref[:, c0:c0 + D])
        dxn, dsh, dsc, dg = _modnorm_bwd(dh, x2_ref[...], g_ref[...], sc_ref[...])
        dx2 = dx3_ref[...] + dxn
        dx2_ref[...] = dx2
        do1_ref[...] = (dx2 * g1_ref[...]).astype(BF16)
        dsh_ref[...] += dsh
        dsc_ref[...] += dsc
        dg_ref[...] += dg
        dg1_ref[...] += _colsum(dx2 * o1_ref[...].astype(F32))

    tile = pl.BlockSpec((tm, D), lambda i: (i, 0))
    wide = pl.BlockSpec((tm, DFF), lambda i: (i, 0))
    nxt = pl.BlockSpec((HALO, DFF), lambda i: (jnp.minimum((i + 1) * (tm // HALO), t // HALO - 1), 0))
    vec = _const_spec((1, D))
    vshape = jax.ShapeDtypeStruct((1, D), F32)
    return pl.pallas_call(
        body, name="ffn_up_bwd", grid=(nt,),
        in_specs=[wide, nxt, wide, nxt, wide, wide,
                  _const_spec((3, 2 * DFF)), _const_spec((D, 2 * DFF), True),
                  tile, tile, vec, vec, tile, vec],
        out_specs=[pl.BlockSpec((tm, 2 * DFF), lambda i: (i, 0)), tile, tile, _const_spec((3, 2 * DFF)),
                   vec, vec, vec, vec],
        out_shape=[jax.ShapeDtypeStruct((t, 2 * DFF), BF16), jax.ShapeDtypeStruct((t, D), F32),
                   jax.ShapeDtypeStruct((t, D), BF16), jax.ShapeDtypeStruct((3, 2 * DFF), F32),
                   vshape, vshape, vshape, vshape],
        compiler_params=_cparams(1),
    )(dact, dact, dval, dval, up_a, up_v, cw, wup, x2, dx3, gffn, scale2, o1, gate1)


def _xt_y(a, b, name):
    t, k = a.shape
    n = b.shape[1]
    tm = min(1024, t)
    bn = 768 if n % 768 == 0 else D

    def body(a_ref, b_ref, o_ref):
        @pl.when(pl.program_id(1) == 0)
        def _():
            o_ref[...] = jnp.zeros_like(o_ref)

        o_ref[...] += _dot_tn(a_ref[...], b_ref[...])

    return pl.pallas_call(
        body, name=name, grid=(n // bn, t // tm),
        in_specs=[pl.BlockSpec((tm, k), lambda j, i: (i, 0)), pl.BlockSpec((tm, bn), lambda j, i: (i, j))],
        out_specs=pl.BlockSpec((k, bn), lambda j, i: (0, j)),
        out_shape=jax.ShapeDtypeStruct((k, n), F32),
        compiler_params=_cparams(2),
    )(a, b)


def _acc_spec(shape, index):
    return pl.BlockSpec(shape, lambda *_: index, pipeline_mode=pl.Buffered(1))


def _out_bwd(do1, wout, merged, ya, yb, z, h1):
    t = do1.shape[0]
    tm = _tile_big(t)

    def body(do1_ref, wo_ref, mg_ref, ya_ref, yb_ref, ga_ref, gb_ref, h1_ref,
             dya_ref, dyb_ref, dz_ref, dwo_ref, dwin_ref):
        @pl.when(pl.program_id(0) == 0)
        def _():
            dwo_ref[...] = jnp.zeros_like(dwo_ref)
            dwin_ref[...] = jnp.zeros_like(dwin_ref)

        do1v = do1_ref[...]
        dwo_ref[...] += _dot_tn(mg_ref[...], do1v)
        dm = _dot_nt(do1v, wo_ref[...])
        sa = _sigmoid(ga_ref[...].astype(F32))
        sb = _sigmoid(gb_ref[...].astype(F32))
        dya_ref[...] = (dm * sa).astype(BF16)
        dyb_ref[...] = (dm * sb).astype(BF16)
        dga = (dm * ya_ref[...].astype(F32) * sa * (1.0 - sa)).astype(BF16)
        dgb = (dm * yb_ref[...].astype(F32) * sb * (1.0 - sb)).astype(BF16)
        dz_ref[:, 0:D] = dga
        dz_ref[:, D:2 * D] = dgb
        h1v = h1_ref[...]
        dwin_ref[:, 0:D] += _dot_tn(h1v, dga)
        dwin_ref[:, D:2 * D] += _dot_tn(h1v, dgb)

    tile = pl.BlockSpec((tm, D), lambda i: (i, 0))
    bshape = jax.ShapeDtypeStruct((t, D), BF16)
    return pl.pallas_call(
        body, name="out_bwd", grid=(t // tm,),
        in_specs=[tile, _const_spec((D, D), True), tile, tile, tile,
                  pl.BlockSpec((tm, D), lambda i: (i, 4)), pl.BlockSpec((tm, D), lambda i: (i, 5)), tile],
        out_specs=[tile, tile, pl.BlockSpec((tm, 2 * D), lambda i: (i, 2)), _acc_spec((D, D), (0, 0)),
                   _acc_spec((D, 2 * D), (0, 2))],
        out_shape=[bshape, bshape, jax.ShapeDtypeStruct((t, NCOL_IN), BF16), jax.ShapeDtypeStruct((D, D), F32),
                   jax.ShapeDtypeStruct((D, NCOL_IN), F32)],
        compiler_params=_cparams(1),
    )(do1, wout, merged, ya, yb, z, z, h1)


def _rnn_bwd(dya, ya_pre, wba, h1, z, saved, h, dz, dwin, cw, wa, wx, lam):
    t = z.shape[0]
    tm = _tile_seq(t)
    nt = t // tm
    ngrp = tm // SUBLANES
    hpt = tm // HALO

    def body(dya_ref, yap_ref, wba_ref, h1_ref, xr_ref, xc_ref, ra_ref, ia_ref, gg_ref, hg_ref, h_ref, hp_ref,
             dz_any, dwin_any, cw_ref, wa_ref, wx_ref, lam_ref,
             dz_ref, dwin_ref, dwba_ref, dcw_ref, dcb_ref, dwa_ref, dba_ref, dwx_ref, dbx_ref, dlam_ref,
             a_first, g_first, dxc_first, b_scr, d_scr, g_scr):
        del dz_any, dwin_any
        i = pl.program_id(0)

        @pl.when(i == 0)
        def _():
            for r in (dwin_ref, dwba_ref, dcw_ref, dcb_ref, dwa_ref, dba_ref, dwx_ref, dbx_ref, dlam_ref,
                      a_first, g_first, dxc_first):
                r[...] = jnp.zeros_like(r)

        dya_v = dya_ref[...]
        dwba_ref[...] += _dot_tn(yap_ref[...], dya_v)
        dyap_v = _dot_nt(dya_v, wba_ref[...])
        h1v = h1_ref[...]

        first_tile = i == nt - 1
        xc = xc_ref[...].astype(F32)
        ra = ra_ref[...].astype(F32)
        ia = ia_ref[...].astype(F32)
        lam_v = lam_ref[...]
        ls = _log_sigmoid(lam_v)
        la = LRU_C * ra * ls
        a = jnp.exp(la)
        mult = jnp.sqrt(-jnp.tanh(la) * (1.0 + a * a))
        hprev8 = jnp.where(first_tile, 0.0, hp_ref[...][HALO - SUBLANES:])
        h_prev = _shift_down(h_ref[...], 1, hprev8)
        dgr = (dyap_v * hg_ref[...].astype(F32)).astype(BF16)
        dz_ref[:, D:2 * D] = dgr
        dwin_ref[:, D:2 * D] += _dot_tn(h1v, dgr)

        b_scr[...] = _shift_up(a, 1, a_first[...])
        d_scr[...] = dyap_v * gg_ref[...].astype(F32)
        row = _row_iota(D)

        def grp(jj, carry):
            r0 = pl.multiple_of((ngrp - 1 - jj) * SUBLANES, SUBLANES)
            bv = b_scr[pl.ds(r0, SUBLANES), :]
            dv = d_scr[pl.ds(r0, SUBLANES), :]
            for d in (1, 2, 4):
                m = row < SUBLANES - d
                dv = jnp.where(m, dv + bv * pltpu.roll(dv, SUBLANES - d, 0), dv)
                bv = jnp.where(m, bv * pltpu.roll(bv, SUBLANES - d, 0), bv)
            gv = dv + bv * carry
            g_scr[pl.ds(r0, SUBLANES), :] = gv
            return gv[0:1, :]

        lax.fori_loop(0, ngrp, grp, g_first[0:1, :])
        g = g_scr[...]
        a_first[...] = a[:SUBLANES]
        g_first[...] = g[:SUBLANES]

        da = g * h_prev
        gx = g * xc
        dmult = gx * ia
        dia = gx * mult
        dxc = g * (mult * ia)
        dla = da * a - dmult * (a * a) / mult
        dra = dla * (LRU_C * ls)
        dlam_ref[...] += _colsum(dla * ra) * (LRU_C * _sigmoid(-lam_v))
        dpa = dra * ra * (1.0 - ra)
        dpx = dia * ia * (1.0 - ia)
        dba_ref[...] += _colsum(dpa)
        dbx_ref[...] += _colsum(dpx)
        dpab = dpa.astype(BF16)
        dpxb = dpx.astype(BF16)
        xcb = xc_ref[...]
        for hd in range(NH):
            sl = slice(hd * HD, (hd + 1) * HD)
            dwa_ref[hd] += _dot_tn(xcb[:, sl], dpab[:, sl])
            dwx_ref[hd] += _dot_tn(xcb[:, sl], dpxb[:, sl])
        dxc = dxc + _heads_nt(dpab, wa_ref) + _heads_nt(dpxb, wx_ref)

        nxt = dxc_first[...]
        taps = (_shift_up(dxc, 3, nxt), _shift_up(dxc, 2, nxt), _shift_up(dxc, 1, nxt), dxc)
        dxr = cw_ref[0:1, :] * taps[0]
        for k in range(1, 4):
            dxr = dxr + cw_ref[k:k + 1, :] * taps[k]
        dxrb = dxr.astype(BF16)
        dz_ref[:, 0:D] = dxrb
        dwin_ref[:, 0:D] += _dot_tn(h1v, dxrb)
        dxc_first[...] = dxc[:SUBLANES]
        dcb_ref[...] += _colsum(dxc)
        xr = xr_ref[...].astype(F32)
        for k in range(4):
            dcw_ref[k:k + 1, :] += _colsum(taps[k] * xr)

    def rev(col):
        return lambda i: (nt - 1 - i, col)

    vec = _const_spec((1, D))
    wspec = _const_spec((NH, HD, HD))
    vshape = jax.ShapeDtypeStruct((1, D), F32)
    wshape = jax.ShapeDtypeStruct((NH, HD, HD), F32)
    any_spec = pl.BlockSpec(memory_space=pl.ANY)
    tile = pl.BlockSpec((tm, D), rev(0))
    outs = pl.pallas_call(
        body, name="rnn_bwd", grid=(nt,),
        in_specs=[tile, tile, _const_spec((D, D), True), tile, tile, tile, tile, tile, tile, tile, tile,
                  pl.BlockSpec((HALO, D), lambda i: (jnp.maximum((nt - 1 - i) * hpt - 1, 0), 0)),
                  any_spec, any_spec, _const_spec((4, D)), wspec, wspec, vec],
        out_specs=[pl.BlockSpec((tm, 2 * D), rev(0)), _acc_spec((D, 2 * D), (0, 0)), _acc_spec((D, D), (0, 0)),
                   _const_spec((4, D)), vec, wspec, vec, wspec, vec, vec],
        out_shape=[jax.ShapeDtypeStruct((t, NCOL_IN), BF16), jax.ShapeDtypeStruct((D, NCOL_IN), F32),
                   jax.ShapeDtypeStruct((D, D), F32), jax.ShapeDtypeStruct((4, D), F32), vshape,
                   wshape, vshape, wshape, vshape, vshape],
        scratch_shapes=[pltpu.VMEM((SUBLANES, D), F32), pltpu.VMEM((SUBLANES, D), F32), pltpu.VMEM((SUBLANES, D), F32),
                        pltpu.VMEM((tm, D), F32), pltpu.VMEM((tm, D), F32), pltpu.VMEM((tm, D), F32)],
        input_output_aliases={12: 0, 13: 1},
        compiler_params=_cparams(1),
    )(dya, ya_pre, wba, h1, z, *saved, h, h, dz, dwin, cw, wa, wx, lam)
    return outs


def _sgu_bwd(dyb, yb_pre, wbb, h1, z, dz, dwin, lng, lnb, wm, wmt, bst, mask):
    t = z.shape[0]
    tm = _tile_big(t)

    def body(dyb_ref, ybp_ref, wbb_ref, h1_ref, zu_ref, zv_ref, dz_any, dwin_any,
             lng_ref, lnb_ref, wm_ref, wmt_ref, bst_ref, mask_ref,
             dz_ref, dwin_ref, dwbb_ref, dws_ref, dbst_ref, dlng_ref, dlnb_ref):
        del dz_any, dwin_any

        @pl.when(pl.program_id(0) == 0)
        def _():
            for r in (dwin_ref, dwbb_ref, dws_ref, dbst_ref, dlng_ref, dlnb_ref):
                r[...] = jnp.zeros_like(r)

        zu = zu_ref[...].astype(F32)
        zv = zv_ref[...].astype(F32)
        lng_v = lng_ref[...]
        gu, tu, tv, rstd, vhat, vb, mixed = _sgu_core(zu, zv, lng_v, lnb_ref[...], wm_ref, bst_ref)
        dyb_v = dyb_ref[...]
        dwbb_ref[...] += _dot_tn(ybp_ref[...], dyb_v)
        dyb = _dot_nt(dyb_v, wbb_ref[...])
        h1v = h1_ref[...]
        dzu = (dyb * mixed * _gelu_grad(tu)).astype(BF16)
        dz_ref[:, 0:D] = dzu
        dwin_ref[:, 0:D] += _dot_tn(h1v, dzu)
        dmix = dyb * gu
        dmb = dmix.astype(BF16)
        rows = []
        lane = lax.broadcasted_iota(jnp.int32, (HD, NH), 1)
        dbst = jnp.zeros((HD, NH), F32)
        for b0 in range(0, tm, HD):
            cols = []
            for g in range(NH):
                sl = slice(g * HD, (g + 1) * HD)
                dmg = dmb[b0:b0 + HD, sl]
                dws_ref[g] += _dot_nt(dmg, vb[b0:b0 + HD, sl]) * mask_ref[...]
                cols.append(jnp.dot(wmt_ref[g], dmg, preferred_element_type=F32))
                dbst = dbst + jnp.where(lane == g, jnp.sum(dmix[b0:b0 + HD, sl], axis=1, keepdims=True), 0.0)
            rows.append(jnp.concatenate(cols, axis=1))
        dbst_ref[...] += dbst
        dvln = jnp.concatenate(rows, axis=0) if len(rows) > 1 else rows[0]
        dlng_ref[...] += _colsum(dvln * vhat)
        dlnb_ref[...] += _colsum(dvln)
        dvh = dvln * lng_v
        dgv = rstd * (dvh - jnp.mean(dvh, axis=-1, keepdims=True)
                      - vhat * jnp.mean(dvh * vhat, axis=-1, keepdims=True))
        dzv = (dgv * _gelu_grad(tv)).astype(BF16)
        dz_ref[:, D:2 * D] = dzv
        dwin_ref[:, D:2 * D] += _dot_tn(h1v, dzv)

    vec = _const_spec((1, D))
    wspec = _const_spec((NH, HD, HD))
    vshape = jax.ShapeDtypeStruct((1, D), F32)
    tile = pl.BlockSpec((tm, D), lambda i: (i, 0))
    any_spec = pl.BlockSpec(memory_space=pl.ANY)
    return pl.pallas_call(
        body, name="sgu_bwd", grid=(t // tm,),
        in_specs=[tile, tile, _const_spec((D, D), True), tile,
                  pl.BlockSpec((tm, D), lambda i: (i, 2)), pl.BlockSpec((tm, D), lambda i: (i, 3)), any_spec, any_spec,
                  vec, vec, wspec, wspec, _const_spec((HD, NH)), _const_spec((HD, HD))],
        out_specs=[pl.BlockSpec((tm, 2 * D), lambda i: (i, 1)), _acc_spec((D, 2 * D), (0, 1)), _acc_spec((D, D), (0, 0)),
                   wspec, _const_spec((HD, NH)), vec, vec],
        out_shape=[jax.ShapeDtypeStruct((t, NCOL_IN), BF16), jax.ShapeDtypeStruct((D, NCOL_IN), F32),
                   jax.ShapeDtypeStruct((D, D), F32), jax.ShapeDtypeStruct((NH, HD, HD), F32),
                   jax.ShapeDtypeStruct((HD, NH), F32), vshape, vshape],
        input_output_aliases={6: 0, 7: 1},
        compiler_params=_cparams(1),
    )(dyb, yb_pre, wbb, h1, z, z, dz, dwin, lng, lnb, wm, wmt, bst, mask)


def _in_bwd(dz, win, x, dx2, g, scale1):
    t = x.shape[0]
    tm = _tile_big(t)

    def body(dz_ref, w_ref, x_ref, dx2_ref, g_ref, sc_ref, dx_ref, dsh_ref, dsc_ref, dg_ref):
        @pl.when(pl.program_id(0) == 0)
        def _():
            for r in (dsh_ref, dsc_ref, dg_ref):
                r[...] = jnp.zeros_like(r)

        dh = jnp.zeros((tm, D), F32)
        for c0 in range(0, NCOL_IN, D):
            dh = dh + _dot_nt(dz_ref[:, c0:c0 + D], w_ref[:, c0:c0 + D])
        dxn, dsh, dsc, dg = _modnorm_bwd(dh, x_ref[...], g_ref[...], sc_ref[...])
        dx_ref[...] = dx2_ref[...] + dxn
        dsh_ref[...] += dsh
        dsc_ref[...] += dsc
        dg_ref[...] += dg

    tile = pl.BlockSpec((tm, D), lambda i: (i, 0))
    vec = _const_spec((1, D))
    vshape = jax.ShapeDtypeStruct((1, D), F32)
    return pl.pallas_call(
        body, name="in_bwd", grid=(t // tm,),
        in_specs=[pl.BlockSpec((tm, NCOL_IN), lambda i: (i, 0)), _const_spec((D, NCOL_IN), True), tile, tile, vec, vec],
        out_specs=[tile, vec, vec, vec],
        out_shape=[jax.ShapeDtypeStruct((t, D), F32), vshape, vshape, vshape],
        compiler_params=_cparams(1),
    )(dz, win, x, dx2, g, scale1)


def _mod_cols(c_all, w_ada, b_cols):
    nb, cols = c_all.shape[0], w_ada.shape[1]

    def body(c_ref, w_ref, b_ref, o_ref):
        cv = c_ref[...]
        ca = (cv * _sigmoid(cv)).astype(BF16)
        o_ref[...] = jnp.dot(ca, w_ref[...].astype(BF16), preferred_element_type=F32) + b_ref[...]

    return pl.pallas_call(body, name="mod_cols", out_shape=jax.ShapeDtypeStruct((nb, cols), F32))(c_all, w_ada, b_cols)


def _ada_grad(c_all, dmod_cols):
    cols = dmod_cols.shape[1]

    def body(c_ref, d_ref, o_ref):
        cv = c_ref[...]
        ca = (cv * _sigmoid(cv)).astype(BF16)
        o_ref[...] = _dot_tn(ca, d_ref[...].astype(BF16))

    return pl.pallas_call(body, name="ada_grad", out_shape=jax.ShapeDtypeStruct((D, cols), F32))(c_all, dmod_cols)


def _adamw(w, m, v, parts, name):
    rows, cols = w.shape
    tr = _row_tile(rows, cols)
    stacked = [p.ndim == 3 for p in parts]
    bc1 = 1.0 - ADAM_B1 ** ADAM_STEP
    bc2 = 1.0 - ADAM_B2 ** ADAM_STEP

    def body(*refs):
        w_ref, m_ref, v_ref = refs[:3]
        p_refs = refs[3:3 + len(parts)]
        g_ref, d_ref, mo_ref, vo_ref = refs[3 + len(parts):]
        g = None
        for p_ref, st in zip(p_refs, stacked):
            terms = [p_ref[k].astype(F32) for k in range(p_ref.shape[0])] if st else [p_ref[...].astype(F32)]
            for term in terms:
                g = term if g is None else g + term
        mn = ADAM_B1 * m_ref[...] + (1.0 - ADAM_B1) * g
        vn = ADAM_B2 * v_ref[...] + (1.0 - ADAM_B2) * (g * g)
        g_ref[...] = g
        mo_ref[...] = mn
        vo_ref[...] = vn
        d_ref[...] = -ADAM_LR * ((mn / bc1) / (jnp.sqrt(vn / bc2) + ADAM_EPS) + ADAM_WD * w_ref[...])

    tile = pl.BlockSpec((tr, cols), lambda i: (i, 0))
    p_specs = [pl.BlockSpec((p.shape[0], tr, cols), lambda i: (0, i, 0)) if st else tile for p, st in zip(parts, stacked)]
    shp = jax.ShapeDtypeStruct((rows, cols), F32)
    return pl.pallas_call(
        body, name=name, grid=(rows // tr,),
        in_specs=[tile, tile, tile] + p_specs, out_specs=[tile] * 4, out_shape=[shp] * 4,
        compiler_params=_cparams(1),
    )(w, m, v, *parts)


def _mesh_pos():
    return lax.axis_index("x"), lax.axis_index("y"), lax.axis_index("c")


def _other_chips(x, y):
    return [(1 - x, y), (x, 1 - y), (1 - x, 1 - y)]


def _block_of(ref, axis, index, size):
    if axis == 0:
        return ref.at[index]
    return ref.at[:, pl.ds(pl.multiple_of(index * size, 128), size)]


def _all_gather(shards, axes, name):
    n = len(shards)
    per = 7

    def body(*refs):
        ins, outs, done = refs[:n], refs[n:2 * n], refs[2 * n]
        send_sems, recv_sems, local_sems = refs[2 * n + 1:]
        x, y, c = _mesh_pos()
        me, sibling = (x, y, c), (x, y, 1 - c)
        chips = _other_chips(x, y)

        def rows(a, pos):
            return _block_of(outs[a], axes[a], 4 * pos[0] + 2 * pos[1] + pos[2], shards[a].shape[-1])

        def copy(a, k, block, to, src=None):
            return pltpu.make_async_remote_copy(
                src_ref=rows(a, block) if src is None else src, dst_ref=rows(a, block),
                send_sem=send_sems.at[a * per + k], recv_sem=recv_sems.at[a * per + k],
                device_id=to, device_id_type=MESH_IDS)

        mine = [pltpu.make_async_copy(ins[a], rows(a, me), local_sems.at[a]) for a in range(n)]
        for cp in mine:
            cp.start()
        first = []
        for a in range(n):
            first.append(copy(a, 0, me, sibling, src=ins[a]))
            first += [copy(a, 1 + j, me, (*chip, c), src=ins[a]) for j, chip in enumerate(chips)]
        for cp in first:
            cp.start()
        passed = []
        for j, chip in enumerate(chips):
            for a in range(n):
                copy(a, 1 + j, (*chip, c), me).wait_recv()
                fwd = copy(a, 4 + j, (*chip, c), sibling)
                fwd.start()
                passed.append(fwd)
        for a in range(n):
            copy(a, 0, sibling, me).wait_recv()
            for j, chip in enumerate(chips):
                copy(a, 4 + j, (*chip, 1 - c), me).wait_recv()
        for cp in first + passed:
            cp.wait_send()
        for cp in mine:
            cp.wait()
        done[...] = jnp.zeros_like(done)

    def full_shape(s, ax):
        return (N_DEV,) + s.shape if ax == 0 else s.shape[:-1] + (N_DEV * s.shape[-1],)

    any_spec = pl.BlockSpec(memory_space=pl.ANY)
    outs = pl.pallas_call(
        body, name=name,
        in_specs=[any_spec] * n, out_specs=[any_spec] * n + [pl.BlockSpec(memory_space=pltpu.VMEM)],
        out_shape=[jax.ShapeDtypeStruct(full_shape(s, ax), s.dtype) for s, ax in zip(shards, axes)]
        + [jax.ShapeDtypeStruct((SUBLANES, LANES), F32)],
        scratch_shapes=[pltpu.SemaphoreType.DMA((n * per,)), pltpu.SemaphoreType.DMA((n * per,)),
                        pltpu.SemaphoreType.DMA((n,))],
    )(*shards)
    return outs[:n], outs[n]


def _chip_blocks(x, y):
    return [(x, y)] + _other_chips(x, y)


def _sibling_reduce(gs, axis, name):
    g0, n = gs[0], len(gs)
    rows, cols = (g0.shape[1], g0.shape[2]) if axis == 0 else (g0.shape[0], g0.shape[1] // N_DEV)
    chunk = math.gcd(rows, 64)

    def body(*refs):
        g_refs, own_refs, pay_refs = refs[:n], refs[n:2 * n], refs[2 * n:3 * n]
        send_buf, keep_buf, recv_buf, send_sems, recv_sems, stage_sems, keep_sems = refs[3 * n:]
        x, y, c = _mesh_pos()
        sibling = (x, y, 1 - c)
        chips = _chip_blocks(x, y)
        stage, keep, push = [], [], []
        for a in range(n):
            for j, (px, py) in enumerate(chips):
                s = 4 * a + j
                theirs = _block_of(g_refs[a], axis, 4 * px + 2 * py + (1 - c), cols)
                ours = _block_of(g_refs[a], axis, 4 * px + 2 * py + c, cols)
                stage.append(pltpu.make_async_copy(theirs, send_buf.at[s], stage_sems.at[s]))
                keep.append(pltpu.make_async_copy(ours, keep_buf.at[s], keep_sems.at[s]))
                push.append(pltpu.make_async_remote_copy(
                    src_ref=send_buf.at[s], dst_ref=recv_buf.at[s], send_sem=send_sems.at[s],
                    recv_sem=recv_sems.at[s], device_id=sibling, device_id_type=MESH_IDS))
        for cp in stage + keep:
            cp.start()
        for s in range(4 * n):
            stage[s].wait()
            push[s].start()
        for s in range(4 * n):
            push[s].wait_recv()
            keep[s].wait()
            a, j = divmod(s, 4)
            dst = own_refs[a] if j == 0 else pay_refs[a].at[j - 1]

            def add(r, carry, s=s, dst=dst):
                sl = pl.ds(pl.multiple_of(r * chunk, chunk), chunk)
                dst[sl, :] = (keep_buf[s, sl, :] + recv_buf[s, sl, :]).astype(dst.dtype)
                return carry

            lax.fori_loop(0, rows // chunk, add, 0)
        for cp in push:
            cp.wait_send()

    vmem = pl.BlockSpec(memory_space=pltpu.VMEM)
    buf = pltpu.VMEM((4 * n, rows, cols), F32)
    sems = pltpu.SemaphoreType.DMA((4 * n,))
    outs = pl.pallas_call(
        body, name=name,
        in_specs=[pl.BlockSpec(memory_space=pl.ANY)] * n, out_specs=[vmem] * (2 * n),
        out_shape=[jax.ShapeDtypeStruct((rows, cols), F32)] * n + [jax.ShapeDtypeStruct((3, rows, cols), BF16)] * n,
        scratch_shapes=[buf, buf, buf, sems, sems, sems, sems],
        compiler_params=pltpu.CompilerParams(vmem_limit_bytes=VMEM_LIMIT),
    )(*gs)
    return list(zip(outs[:n], outs[n:]))


_HBM_SPEC = pl.BlockSpec(memory_space=pltpu.HBM)
_SEM_SPEC = pl.BlockSpec(memory_space=pltpu.SEMAPHORE)
_SIDE_EFFECT = pltpu.SideEffectType.DATAFLOW_SIDE_EFFECTING


def _exchange_start(name, srcs, lands, plan, n_copies):
    nb = len(srcs) + len(lands)

    def body(*refs):
        bufs, send_sems, recv_sems, token = refs[:nb], refs[nb], refs[nb + 1], refs[-1]
        for cp in plan(bufs[:len(srcs)], bufs[len(srcs):], send_sems, recv_sems):
            cp.start()
        token[...] = jnp.zeros_like(token)

    arrays = list(srcs) + list(lands)
    outs = pl.pallas_call(
        body, name=name,
        out_shape=(pltpu.SemaphoreType.DMA((n_copies,)), pltpu.SemaphoreType.DMA((n_copies,)),
                   *[pltpu.HBM(a.shape, a.dtype) for a in arrays], jax.ShapeDtypeStruct((SUBLANES, LANES), F32)),
        in_specs=[_HBM_SPEC] * nb,
        out_specs=(_SEM_SPEC, _SEM_SPEC, *[_HBM_SPEC] * nb, pl.BlockSpec(memory_space=pltpu.VMEM)),
        input_output_aliases={k: 2 + k for k in range(nb)},
        compiler_params=pltpu.CompilerParams(has_side_effects=_SIDE_EFFECT),
    )(*[pltpu.with_memory_space_constraint(a, pltpu.HBM) for a in arrays])
    return outs[0], outs[1], outs[2:2 + len(srcs)], outs[2 + len(srcs):2 + nb], outs[-1]


def _exchange_wait(name, send_sems, recv_sems, srcs, lands, plan, after):
    nb = len(srcs) + len(lands)
    after = list(after)

    def body(*refs):
        bufs, send_ref, recv_ref = refs[:nb], refs[nb], refs[nb + 1]
        for cp in plan(bufs[:len(srcs)], bufs[len(srcs):], send_ref, recv_ref):
            cp.wait_send()
            cp.wait_recv()

    arrays = list(srcs) + list(lands)
    outs = pl.pallas_call(
        body, name=name,
        out_shape=tuple(pltpu.HBM(a.shape, a.dtype) for a in arrays),
        in_specs=[_HBM_SPEC] * nb + [_SEM_SPEC, _SEM_SPEC] + [pl.BlockSpec(memory_space=pl.ANY)] * len(after),
        out_specs=tuple([_HBM_SPEC] * nb),
        input_output_aliases={k: k for k in range(nb)},
        compiler_params=pltpu.CompilerParams(has_side_effects=_SIDE_EFFECT),
    )(*arrays, send_sems, recv_sems, *after)
    return outs[len(srcs):]


def _gather_plan(axes, sizes):
    def plan(src_refs, land_refs, send_sems, recv_sems):
        x, y, c = _mesh_pos()
        copies = []
        for a, (src, land) in enumerate(zip(src_refs, land_refs)):
            mine = _block_of(land, axes[a], 4 * x + 2 * y + c, sizes[a])
            for k in range(1, N_DEV):
                peer = (1 - x if k & 4 else x, 1 - y if k & 2 else y, 1 - c if k & 1 else c)
                idx = a * (N_DEV - 1) + k - 1
                copies.append(pltpu.make_async_remote_copy(
                    src_ref=src, dst_ref=mine, send_sem=send_sems.at[idx], recv_sem=recv_sems.at[idx],
                    device_id=peer, device_id_type=MESH_IDS))
        return copies
    return plan


def _chip_plan(src_refs, land_refs, send_sems, recv_sems):
    x, y, c = _mesh_pos()
    copies = []
    for a, (src, land) in enumerate(zip(src_refs, land_refs)):
        for j, chip in enumerate(_other_chips(x, y)):
            copies.append(pltpu.make_async_remote_copy(
                src_ref=src.at[j], dst_ref=land.at[j], send_sem=send_sems.at[3 * a + j],
                recv_sem=recv_sems.at[3 * a + j], device_id=(*chip, c), device_id_type=MESH_IDS))
    return copies


def _own_block_placed(shard, axis, me):
    if axis == 0:
        full = lax.empty((N_DEV,) + shard.shape, shard.dtype)
        return lax.dynamic_update_slice(full, shard[None], (me,) + (0,) * shard.ndim)
    rows, cols = shard.shape

    def body(me_ref, s_ref, o_ref):
        del me_ref
        o_ref[...] = s_ref[...]

    return pl.pallas_call(
        body, name="place_own_columns",
        grid_spec=pltpu.PrefetchScalarGridSpec(
            num_scalar_prefetch=1, grid=(1,),
            in_specs=[pl.BlockSpec((rows, cols), lambda i, me_ref: (0, 0))],
            out_specs=pl.BlockSpec((rows, cols), lambda i, me_ref: (0, me_ref[0]))),
        out_shape=jax.ShapeDtypeStruct((rows, N_DEV * cols), shard.dtype),
    )(jnp.reshape(me, (1,)).astype(jnp.int32), shard)


def _local_step(x, target, mod, win, late_weights, p, grads_ready=None):
    shift1, scale1, gate1, shift2, scale2, gate2 = (mod[k] for k in range(6))

    def after_token(v, token):
        return v if token is None else v + token[0:1, 0:1]
    wa, wx = p["lru_w_a"].astype(BF16), p["lru_w_x"].astype(BF16)
    mask = jnp.tril(jnp.ones((HD, HD), F32))
    wm = (p["sgu_w_s"] * mask).astype(BF16)
    wmt = jnp.swapaxes(wm, 1, 2)
    bst = jnp.transpose(p["sgu_b_s"])

    h1, z = _norm_proj(x, p["norm_mix_g"], scale1, shift1, win, "mix_proj")
    hstate, ya_pre, *rnn_saved = _rnn_fwd(
        z, p["rnn_conv_w"], p["rnn_conv_b"], wa, p["lru_b_a"], wx, p["lru_b_x"], p["lru_lambda"])
    yb_pre = _sgu_fwd(z, p["sgu_ln_g"], p["sgu_ln_b"], wm, bst)
    wba, wbb, wout = late_weights("merge", [ya_pre, yb_pre])
    x2, ya, yb, merged, o1 = _merge_fwd(ya_pre, yb_pre, z, x, gate1, wba, wbb, wout)
    wup = late_weights("ffn_up", [x2])
    h2, up_a, up_v, ff, fa, fv = _ffn_proj_mid(
        x2, p["norm_ffn_g"], scale2, shift2, wup, p["ffn_conv_w"], p["ffn_conv_b"])
    wd = late_weights("ffn_down", [ff])
    dx3, loss, d_gfin, d_gate2 = _ffn_out_loss(ff, wd, x2, target, gate2, p["norm_final_g"])

    dact, dval, d_wd, dcb_a, dcb_v = _ffn_down_bwd(dx3, gate2, ff, fa, fv, wd)
    dup, dx2, do1, d_cwf, d_shift2, d_scale2, d_gffn, d_gate1 = _ffn_up_bwd(
        dact, dval, up_a, up_v, p["ffn_conv_w"], wup, x2, dx3, p["norm_ffn_g"], scale2, o1, gate1)
    d_wup = _xt_y(h2, dup, "w_up_grad")
    ready = grads_ready if grads_ready else (lambda stage, big, small: None)
    token = ready("ffn", {"w_up": d_wup, "w_down": d_wd}, {})

    dya, dyb, dz, d_wout, d_win = _out_bwd(do1, wout, merged, ya, yb, z, h1)
    dz, d_win, d_wba, d_cw, d_cb, d_wa, d_ba, d_wx, d_bx, d_lam = _rnn_bwd(
        dya, ya_pre, wba, h1, z, rnn_saved, hstate, dz, d_win, p["rnn_conv_w"], wa, wx,
        after_token(p["lru_lambda"], token))
    small = {
        "rnn_conv_w": d_cw, "rnn_conv_b": d_cb, "lru_w_a": d_wa, "lru_b_a": d_ba, "lru_w_x": d_wx, "lru_b_x": d_bx,
        "lru_lambda": d_lam, "norm_ffn_g": d_gffn, "ffn_conv_w": d_cwf,
        "ffn_conv_b": jnp.concatenate([dcb_a, dcb_v], axis=1), "norm_final_g": d_gfin,
    }
    token = ready("rnn", {}, small)
    dz, d_win, d_wbb, d_ws, d_bst, d_lng, d_lnb = _sgu_bwd(
        dyb, yb_pre, wbb, h1, z, dz, d_win, p["sgu_ln_g"], after_token(p["sgu_ln_b"], token), wm, wmt, bst, mask)
    sgu_small = {"sgu_ln_g": d_lng, "sgu_ln_b": d_lnb, "sgu_w_s": d_ws, "sgu_b_s": jnp.transpose(d_bst)}
    mixer = {"w_in": d_win, "w_out": d_wout, "w_branch_a": d_wba, "w_branch_b": d_wbb}
    token = ready("mixer", mixer, sgu_small)
    grad_x, d_shift1, d_scale1, d_gmix = _in_bwd(dz, win, x, dx2, after_token(p["norm_mix_g"], token), scale1)

    small.update(sgu_small)
    small["norm_mix_g"] = d_gmix
    dmod = jnp.stack([d_shift1, d_scale1, d_gate1, d_shift2, d_scale2, d_gate2])
    big = {"w_in": d_win, "w_up": d_wup, "w_branch_a": d_wba, "w_branch_b": d_wbb, "w_out": d_wout, "w_down": d_wd}
    return loss, grad_x, big, small, dmod


LAST_REP = ["b_ada", "norm_mix_g"]
EARLY_REP = ["rnn_conv_b", "lru_w_a", "lru_b_a", "lru_w_x", "lru_b_x", "lru_lambda", "norm_ffn_g", "ffn_conv_b",
             "norm_final_g"]
MID_REP = ["sgu_ln_g", "sgu_ln_b", "sgu_w_s", "sgu_b_s"]
COL_SHARDED = ["rnn_conv_w", "ffn_conv_w"]
SMALL_GROUPS = {"rnn": EARLY_REP + COL_SHARDED, "mixer": MID_REP, "last": LAST_REP}
REPLICATED = LAST_REP + EARLY_REP + MID_REP
SMALL_NAMES = REPLICATED + COL_SHARDED
BIG_NAMES = ["w_in", "w_up", "w_branch_a", "w_branch_b", "w_out", "w_down"]
BIG_AXES = [1, 1, 0, 0, 0, 0]
WEIGHTS = ["w_ada", "b_ada", "norm_mix_g", "w_in", "rnn_conv_w", "rnn_conv_b", "lru_w_a", "lru_b_a", "lru_w_x",
           "lru_b_x", "lru_lambda", "sgu_ln_g", "sgu_ln_b", "sgu_w_s", "sgu_b_s", "w_branch_a", "w_branch_b",
           "w_out", "norm_ffn_g", "w_up", "ffn_conv_w", "ffn_conv_b", "w_down", "norm_final_g"]
LANES = 128


def _pack_rows(shape):
    return math.prod(shape) // LANES


def _pack(arrays):
    return jnp.concatenate([a.reshape(-1, LANES) for a in arrays], axis=0)


def _unpack(packed, shapes):
    out, r0 = [], 0
    for s in shapes:
        nrow = math.prod(s) // LANES
        out.append(packed[r0:r0 + nrow].reshape(s))
        r0 += nrow
    return out


def kernel(x, c, w_ada, b_ada, norm_mix_g, w_in, rnn_conv_w, rnn_conv_b, lru_w_a, lru_b_a, lru_w_x, lru_b_x, lru_lambda, sgu_ln_g, sgu_ln_b, sgu_w_s, sgu_b_s, w_branch_a, w_branch_b, w_out, norm_ffn_g, w_up, ffn_conv_w, ffn_conv_b, w_down, norm_final_g, loss_target, m_w_ada, m_b_ada, m_norm_mix_g, m_w_in, m_rnn_conv_w, m_rnn_conv_b, m_lru_w_a, m_lru_b_a, m_lru_w_x, m_lru_b_x, m_lru_lambda, m_sgu_ln_g, m_sgu_ln_b, m_sgu_w_s, m_sgu_b_s, m_w_branch_a, m_w_branch_b, m_w_out, m_norm_ffn_g, m_w_up, m_ffn_conv_w, m_ffn_conv_b, m_w_down, m_norm_final_g, v_w_ada, v_b_ada, v_norm_mix_g, v_w_in, v_rnn_conv_w, v_rnn_conv_b, v_lru_w_a, v_lru_b_a, v_lru_w_x, v_lru_b_x, v_lru_lambda, v_sgu_ln_g, v_sgu_ln_b, v_sgu_w_s, v_sgu_b_s, v_w_branch_a, v_w_branch_b, v_w_out, v_norm_ffn_g, v_w_up, v_ffn_conv_w, v_ffn_conv_b, v_w_down, v_norm_final_g):
    given = dict(locals())
    me = 4 * lax.axis_index("x") + 2 * lax.axis_index("y") + lax.axis_index("c")
    ada_cols = w_ada.shape[2]
    conv_cols = {"rnn_conv_w": rnn_conv_w.shape[2], "ffn_conv_w": ffn_conv_w.shape[2]}

    (win, c_all, cw_rnn, cw_ffn), _ = _all_gather(
        [w_in[0].astype(BF16), c.reshape(1, 1, D), rnn_conv_w[0], ffn_conv_w[0]], [1, 0, 1, 1], "gather_first")
    c_all = c_all.reshape(N_DEV, D)

    b_cols = lax.dynamic_slice_in_dim(b_ada, me * ada_cols, ada_cols, axis=1)
    (mod_all,), mod_done = _all_gather(
        [_mod_cols(c_all, w_ada[0], b_cols).reshape(1, N_DEV, ada_cols)], [0], "gather_mod")
    mod_all = mod_all.reshape(N_DEV, N_DEV, ada_cols)
    mod_mine = lax.dynamic_index_in_dim(mod_all, me, axis=1, keepdims=False).reshape(6, 1, D)

    late_groups = {"merge": (["w_branch_a", "w_branch_b", "w_out"], [0, 0, 0]), "ffn_up": (["w_up"], [1]),
                   "ffn_down": (["w_down"], [0])}
    in_flight, started = {}, mod_done[0:1, 0:1]
    for stage, (names, axes) in late_groups.items():
        shards = [(given[n][0] + started).astype(BF16) for n in names]
        plan = _gather_plan(axes, [s.shape[-1] for s in shards])
        send, recv, srcs, lands, token = _exchange_start(
            "gather_start_" + stage, shards, [_own_block_placed(s, ax, me) for s, ax in zip(shards, axes)], plan,
            len(shards) * (N_DEV - 1))
        in_flight[stage] = (send, recv, srcs, lands, plan)
        started = started + token[0:1, 0:1]

    def late_weights(stage, after):
        send, recv, srcs, lands, plan = in_flight[stage]
        full = _exchange_wait("gather_wait_" + stage, send, recv, srcs, lands, plan, after)
        full = [w.reshape(-1, D) if ax == 0 else w for w, ax in zip(full, late_groups[stage][1])]
        return full if len(full) > 1 else full[0]

    mod_mine = mod_mine + started

    reducing, packing = {}, {}

    def start_pack(stage, small):
        pack = _pack([small[n] for n in SMALL_GROUPS[stage]])[None]
        plan = _gather_plan([0], [LANES])
        send, recv, srcs, lands, tok = _exchange_start(
            "small_start_" + stage, [pack], [_own_block_placed(pack, 0, me)], plan, N_DEV - 1)
        packing[stage] = (send, recv, srcs, lands, plan)
        return tok

    def grads_ready(stage, grads, small):
        tokens = [start_pack(stage, small)] if small else []
        if grads:
            tokens.append(start_reduce(stage, grads))
        return sum(tokens[1:], tokens[0])

    def start_reduce(stage, grads):
        names = [n for n in BIG_NAMES if n in grads]
        blocked = {}
        for n in names:
            ax = BIG_AXES[BIG_NAMES.index(n)]
            g = grads[n] if ax == 1 else grads[n].reshape(N_DEV, grads[n].shape[0] // N_DEV, grads[n].shape[1])
            blocked.setdefault((ax, g.shape), []).append((n, g))
        sums = {}
        for (ax, _), group in blocked.items():
            reduced = _sibling_reduce([g for _, g in group], ax, "reduce_sibling_" + "_".join(n for n, _ in group))
            sums.update({n: r for (n, _), r in zip(group, reduced)})
        sums = [sums[n] for n in names]
        pays = [pay for _, pay in sums]
        send, recv, srcs, lands, tok = _exchange_start(
            "reduce_start_" + stage, pays, [lax.empty(p_.shape, p_.dtype) for p_ in pays], _chip_plan, 3 * len(pays))
        reducing[stage] = (names, [own for own, _ in sums], send, recv, srcs, lands)
        return tok

    p = {n: given[n][0] for n in REPLICATED if n not in ("b_ada", "norm_final_g")}
    p = {n: (a.reshape(1, -1) if a.ndim == 1 else a) for n, a in p.items()}
    p["rnn_conv_w"], p["ffn_conv_w"] = cw_rnn, cw_ffn
    p["norm_final_g"] = norm_final_g.reshape(1, D)
    loss, grad_x, _, small, dmod = _local_step(x[0], loss_target[0], mod_mine, win, late_weights, p, grads_ready)

    small["b_ada"] = dmod.reshape(1, 6 * D)
    rows_of = {n: _pack_rows(small[n].shape) for n in SMALL_NAMES}
    (last,), _ = _all_gather([_pack([small[n] for n in LAST_REP])[None]], [0], "gather_small")
    gathered = {"last": last}
    for stage, (send, recv, srcs, lands, plan) in packing.items():
        (gathered[stage],) = _exchange_wait("small_wait_" + stage, send, recv, srcs, lands, plan, [grad_x])
    gathered = {k: v.reshape(N_DEV, -1, LANES) for k, v in gathered.items()}

    out = {}
    for stage, (names, owns, send, recv, srcs, lands) in reducing.items():
        landed = _exchange_wait("reduce_wait_" + stage, send, recv, srcs, lands, _chip_plan, [last])
        for n, own, got in zip(names, owns, landed):
            out[n] = _adamw(given[n][0], given["m_" + n][0], given["v_" + n][0], [own, got], "adamw_" + n)

    dmod_all = gathered["last"][:, :rows_of["b_ada"]].reshape(N_DEV, 6 * D)
    dmod_cols = lax.dynamic_slice_in_dim(dmod_all, me * ada_cols, ada_cols, axis=1)
    out["w_ada"] = _adamw(w_ada[0], m_w_ada[0], v_w_ada[0], [_ada_grad(c_all, dmod_cols)], "adamw_w_ada")

    for stage, names in (("last", LAST_REP), ("rnn", EARLY_REP), ("mixer", MID_REP)):
        res = _adamw(*[_pack([given[pre + n] for n in names]) for pre in ("", "m_", "v_")],
                     [gathered[stage]], "adamw_small_" + stage)
        unpacked = [_unpack(r, [given[n].shape for n in names]) for r in res]
        for k, n in enumerate(names):
            out[n] = tuple(u[k] for u in unpacked)

    row0 = sum(rows_of[n] for n in EARLY_REP)
    for n in COL_SHARDED:
        full = gathered["rnn"][:, row0:row0 + rows_of[n]].reshape(N_DEV, small[n].shape[0], small[n].shape[1])
        mine = lax.dynamic_slice_in_dim(full, me * conv_cols[n], conv_cols[n], axis=2)
        out[n] = _adamw(given[n][0], given["m_" + n][0], given["v_" + n][0], [mine], "adamw_" + n)
        row0 += rows_of[n]

    total = lax.psum(loss[0, 0], ("x", "y", "c"))
    results = [total, grad_x[None]]
    for kind in range(4):
        results += [out[n][kind].reshape(given[n].shape) for n in WEIGHTS]
    return tuple(results)
```

```python
import math

import jax
import jax.numpy as jnp
from jax import lax
from jax.experimental import pallas as pl
from jax.experimental.pallas import tpu as pltpu

F32 = jnp.float32
BF16 = jnp.bfloat16
MESH_IDS = pl.DeviceIdType.MESH

D = 1024
NH = 8
HD = 128
NCOL_IN = 6 * D
DFF = 3 * D
N_DEV = 8
EPS = 1e-6
LRU_C = 8.0
ADAM_LR, ADAM_B1, ADAM_B2, ADAM_EPS, ADAM_WD, ADAM_STEP = 0.001, 0.9, 0.999, 1e-08, 0.01, 10

SUBLANES = 8
HALO = 16
VMEM_LIMIT = 56 * 1024 * 1024
GELU_K = math.sqrt(2.0 / math.pi)
GELU_C = 0.044715


def _cparams(n_axes):
    return pltpu.CompilerParams(dimension_semantics=("arbitrary",) * n_axes, vmem_limit_bytes=VMEM_LIMIT)


def _const_spec(shape, single_buffer=False):
    nd = len(shape)
    if single_buffer:
        return pl.BlockSpec(shape, lambda *_: (0,) * nd, pipeline_mode=pl.Buffered(1))
    return pl.BlockSpec(shape, lambda *_: (0,) * nd)


def _tile_big(t):
    return min(512, t)


def _tile_seq(t):
    return min(256, t)


def _row_tile(rows, cols):
    cap = max(SUBLANES, (2 * 1024 * 1024) // (4 * cols) // SUBLANES * SUBLANES)
    if rows <= cap:
        return rows
    return next(tr for tr in range(cap, 0, -SUBLANES) if rows % tr == 0)


def _gelu_t(x):
    x2 = x * x
    t = jnp.tanh(x * (GELU_K + (GELU_K * GELU_C) * x2))
    hx = 0.5 * x
    return hx + hx * t, (x2, hx, t)


def _gelu_grad(shared):
    x2, hx, t = shared
    return (0.5 + 0.5 * t) + (hx * (1.0 - t * t)) * (GELU_K + (3.0 * GELU_K * GELU_C) * x2)


def _sigmoid(x):
    return 1.0 / (1.0 + jnp.exp(-x))


def _log_sigmoid(x):
    return -(jnp.maximum(-x, 0.0) + jnp.log1p(jnp.exp(-jnp.abs(x))))


def _row_iota(cols):
    return lax.broadcasted_iota(jnp.int32, (SUBLANES, cols), 0)


def _shift_down(x, k, prev8):
    if k == 0:
        return x
    r = pltpu.roll(x, k, 0)
    p = pltpu.roll(prev8, k, 0)
    head = jnp.where(_row_iota(x.shape[1]) < k, p, r[:SUBLANES])
    return jnp.concatenate([head, r[SUBLANES:]], axis=0)


def _shift_up(x, k, next8):
    if k == 0:
        return x
    n = x.shape[0]
    r = pltpu.roll(x, n - k, 0)
    q = pltpu.roll(next8, SUBLANES - k, 0)
    tail = jnp.where(_row_iota(x.shape[1]) >= SUBLANES - k, q, r[n - SUBLANES:])
    return jnp.concatenate([r[:n - SUBLANES], tail], axis=0)


def _heads_nn(x_bf, w_ref):
    return jnp.concatenate(
        [jnp.dot(x_bf[:, h * HD:(h + 1) * HD], w_ref[h], preferred_element_type=F32) for h in range(NH)], axis=1)


def _heads_nt(x_bf, w_ref):
    return jnp.concatenate(
        [lax.dot_general(x_bf[:, h * HD:(h + 1) * HD], w_ref[h], (((1,), (1,)), ((), ())), preferred_element_type=F32)
         for h in range(NH)], axis=1)


def _dot_nt(a, b):
    return lax.dot_general(a, b, (((1,), (1,)), ((), ())), preferred_element_type=F32)


def _dot_tn(a, b):
    return lax.dot_general(a, b, (((0,), (0,)), ((), ())), preferred_element_type=F32)


def _colsum(x):
    return jnp.sum(x, axis=0, keepdims=True)


def _prev_halo_map(tm, col):
    return lambda i, *_: (jnp.maximum(i * (tm // HALO) - 1, 0), col)


def _norm_proj(x, g, scale, shift, w, name):
    t, n = x.shape[0], w.shape[1]
    tm = _tile_big(t)

    def body(x_ref, g_ref, sc_ref, sh_ref, w_ref, h_ref, z_ref):
        xv = x_ref[...]
        r = lax.rsqrt(jnp.mean(xv * xv, axis=-1, keepdims=True) + EPS)
        hb = ((xv * r * g_ref[...]) * (1.0 + sc_ref[...]) + sh_ref[...]).astype(BF16)
        h_ref[...] = hb
        for c0 in range(0, n, D):
            z_ref[:, c0:c0 + D] = jnp.dot(hb, w_ref[:, c0:c0 + D], preferred_element_type=F32).astype(BF16)

    vec = _const_spec((1, D))
    return pl.pallas_call(
        body, name=name, grid=(t // tm,),
        in_specs=[pl.BlockSpec((tm, D), lambda i: (i, 0)), vec, vec, vec, _const_spec((D, n), True)],
        out_specs=[pl.BlockSpec((tm, D), lambda i: (i, 0)), pl.BlockSpec((tm, n), lambda i: (i, 0))],
        out_shape=[jax.ShapeDtypeStruct((t, D), BF16), jax.ShapeDtypeStruct((t, n), BF16)],
        compiler_params=_cparams(1),
    )(x, g, scale, shift, w)


def _lru_gates(xc, wa_ref, ba, wx_ref, bx, ls):
    xb = xc.astype(BF16)
    ra = _sigmoid(_heads_nn(xb, wa_ref) + ba)
    ia = _sigmoid(_heads_nn(xb, wx_ref) + bx)
    la = LRU_C * ra * ls
    a = jnp.exp(la)
    mult = jnp.sqrt(-jnp.tanh(la) * (1.0 + a * a))
    return ra, ia, a, mult


def _conv4(xr, prev8, cw_ref, cb):
    return (cb + cw_ref[3:4, :] * xr + cw_ref[2:3, :] * _shift_down(xr, 1, prev8)
            + cw_ref[1:2, :] * _shift_down(xr, 2, prev8) + cw_ref[0:1, :] * _shift_down(xr, 3, prev8))


def _rnn_fwd(z, cw, cb, wa, ba, wx, bx, lam):
    t = z.shape[0]
    tm = _tile_seq(t)
    ngrp = tm // SUBLANES

    def body(xr_ref, xp_ref, gr_ref, cw_ref, cb_ref, wa_ref, ba_ref, wx_ref, bx_ref, lam_ref,
             h_ref, ya_ref, xc_ref, ra_ref, ia_ref, gg_ref, hg_ref, carry_ref, a_scr, u_scr):
        i = pl.program_id(0)

        @pl.when(i == 0)
        def _():
            carry_ref[...] = jnp.zeros_like(carry_ref)

        xr = xr_ref[...].astype(F32)
        prev8 = jnp.where(i == 0, 0.0, xp_ref[...].astype(F32)[HALO - SUBLANES:])
        xc = _conv4(xr, prev8, cw_ref, cb_ref[...])
        ra, ia, a, mult = _lru_gates(xc, wa_ref, ba_ref[...], wx_ref, bx_ref[...], _log_sigmoid(lam_ref[...]))
        xc_ref[...] = xc.astype(BF16)
        ra_ref[...] = ra.astype(BF16)
        ia_ref[...] = ia.astype(BF16)
        a_scr[...] = a
        u_scr[...] = mult * (ia * xc)
        row = _row_iota(D)

        def grp(j, carry):
            r0 = pl.multiple_of(j * SUBLANES, SUBLANES)
            av = a_scr[pl.ds(r0, SUBLANES), :]
            uv = u_scr[pl.ds(r0, SUBLANES), :]
            for d in (1, 2, 4):
                m = row >= d
                uv = jnp.where(m, av * pltpu.roll(uv, d, 0) + uv, uv)
                av = jnp.where(m, av * pltpu.roll(av, d, 0), av)
            hv = uv + av * carry
            h_ref[pl.ds(r0, SUBLANES), :] = hv
            return hv[SUBLANES - 1:SUBLANES, :]

        carry_ref[0:1, :] = lax.fori_loop(0, ngrp, grp, carry_ref[0:1, :])
        grv = gr_ref[...].astype(F32)
        gg, tg = _gelu_t(grv)
        hv = h_ref[...]
        ya_ref[...] = (hv * gg).astype(BF16)
        gg_ref[...] = gg.astype(BF16)
        hg_ref[...] = (hv * _gelu_grad(tg)).astype(BF16)

    vec = _const_spec((1, D))
    wspec = _const_spec((NH, HD, HD))
    tile = pl.BlockSpec((tm, D), lambda i: (i, 0))
    bshape = jax.ShapeDtypeStruct((t, D), BF16)
    return pl.pallas_call(
        body, name="rnn_fwd", grid=(t // tm,),
        in_specs=[tile, pl.BlockSpec((HALO, D), _prev_halo_map(tm, 0)),
                  pl.BlockSpec((tm, D), lambda i: (i, 1)), _const_spec((4, D)), vec, wspec, vec, wspec, vec, vec],
        out_specs=[tile] * 7,
        out_shape=[jax.ShapeDtypeStruct((t, D), F32)] + [bshape] * 6,
        scratch_shapes=[pltpu.VMEM((SUBLANES, D), F32), pltpu.VMEM((tm, D), F32), pltpu.VMEM((tm, D), F32)],
        compiler_params=_cparams(1),
    )(z, z, z, cw, cb, wa, ba, wx, bx, lam)


def _sgu_fwd(z, lng, lnb, wm, bst):
    t = z.shape[0]
    tm = _tile_seq(t)

    def body(zu_ref, zv_ref, lng_ref, lnb_ref, wm_ref, bst_ref, yb_ref, gu_ref, mg_ref, vh_ref, gpv_ref, rstd_ref):
        gu, su = _gelu_t(zu_ref[...].astype(F32))
        gv, sv = _gelu_t(zv_ref[...].astype(F32))
        mu = jnp.mean(gv, axis=-1, keepdims=True)
        cen = gv - mu
        rstd = lax.rsqrt(jnp.mean(cen * cen, axis=-1, keepdims=True) + EPS)
        vhat = cen * rstd
        vb = (vhat * lng_ref[...] + lnb_ref[...]).astype(BF16)
        rows = []
        for b0 in range(0, tm, HD):
            rows.append(jnp.concatenate(
                [jnp.dot(wm_ref[g], vb[b0:b0 + HD, g * HD:(g + 1) * HD], preferred_element_type=F32)
                 + bst_ref[:, g:g + 1] for g in range(NH)], axis=1))
        mixed = jnp.concatenate(rows, axis=0) if len(rows) > 1 else rows[0]
        yb_ref[...] = (gu * mixed).astype(BF16)
        gu_ref[...] = gu.astype(BF16)
        mg_ref[...] = (mixed * _gelu_grad(su)).astype(BF16)
        vh_ref[...] = vhat.astype(BF16)
        gpv_ref[...] = _gelu_grad(sv).astype(BF16)
        rstd_ref[...] = rstd

    vec = _const_spec((1, D))
    tile = pl.BlockSpec((tm, D), lambda i: (i, 0))
    bshape = jax.ShapeDtypeStruct((t, D), BF16)
    return pl.pallas_call(
        body, name="sgu_fwd", grid=(t // tm,),
        in_specs=[pl.BlockSpec((tm, D), lambda i: (i, 2)), pl.BlockSpec((tm, D), lambda i: (i, 3)), vec, vec,
                  _const_spec((NH, HD, HD)), _const_spec((HD, NH))],
        out_specs=[tile] * 5 + [pl.BlockSpec((tm, 1), lambda i: (i, 0))],
        out_shape=[bshape] * 5 + [jax.ShapeDtypeStruct((t, 1), F32)],
        compiler_params=_cparams(1),
    )(z, z, lng, lnb, wm, bst)


def _merge_fwd(ya_pre, yb_pre, z, x, gate1, wba, wbb, wout):
    t = x.shape[0]
    tm = _tile_big(t)

    def body(yap_ref, ybp_ref, ga_ref, gb_ref, x_ref, g1_ref, wba_ref, wbb_ref, wo_ref,
             x2_ref, ya_ref, yb_ref, mg_ref, o1_ref):
        ya = jnp.dot(yap_ref[...], wba_ref[...], preferred_element_type=F32)
        yb = jnp.dot(ybp_ref[...], wbb_ref[...], preferred_element_type=F32)
        merged = _sigmoid(ga_ref[...].astype(F32)) * ya + _sigmoid(gb_ref[...].astype(F32)) * yb
        mb = merged.astype(BF16)
        o1 = jnp.dot(mb, wo_ref[...], preferred_element_type=F32)
        x2_ref[...] = x_ref[...] + g1_ref[...] * o1
        ya_ref[...] = ya.astype(BF16)
        yb_ref[...] = yb.astype(BF16)
        mg_ref[...] = mb
        o1_ref[...] = o1.astype(BF16)

    tile = pl.BlockSpec((tm, D), lambda i: (i, 0))
    wspec = _const_spec((D, D))
    bshape = jax.ShapeDtypeStruct((t, D), BF16)
    return pl.pallas_call(
        body, name="merge_fwd", grid=(t // tm,),
        in_specs=[tile, tile, pl.BlockSpec((tm, D), lambda i: (i, 4)), pl.BlockSpec((tm, D), lambda i: (i, 5)),
                  tile, _const_spec((1, D)), wspec, wspec, wspec],
        out_specs=[tile] * 5,
        out_shape=[jax.ShapeDtypeStruct((t, D), F32), bshape, bshape, bshape, bshape],
        compiler_params=_cparams(1),
    )(ya_pre, yb_pre, z, z, x, gate1, wba, wbb, wout)


def _conv3(u, prev8, cw_ref, cb):
    return cb + cw_ref[2:3, :] * u + cw_ref[1:2, :] * _shift_down(u, 1, prev8) + cw_ref[0:1, :] * _shift_down(u, 2, prev8)


def _ffn_proj_mid(x2, g, scale, shift, w, cw, cb):
    t = x2.shape[0]
    tm = _tile_big(t)
    nc = DFF // D

    def body(x_ref, g_ref, sc_ref, sh_ref, wa_ref, wv_ref, cwa_ref, cwv_ref, cba_ref, cbv_ref,
             h_ref, upa_ref, upv_ref, ff_ref, fa_ref, fv_ref, hb_scr, prev_ref):
        i, c = pl.program_id(0), pl.program_id(1)

        @pl.when(i == 0)
        def _():
            prev_ref[c] = jnp.zeros((2, SUBLANES, D), F32)

        @pl.when(c == 0)
        def _():
            xv = x_ref[...]
            r = lax.rsqrt(jnp.mean(xv * xv, axis=-1, keepdims=True) + EPS)
            hb_scr[...] = ((xv * r * g_ref[...]) * (1.0 + sc_ref[...]) + sh_ref[...]).astype(BF16)
            h_ref[...] = hb_scr[...]

        hb = hb_scr[...]
        halves = []
        for s, (w_ref, up_ref, cw_ref, cb_ref) in enumerate(((wa_ref, upa_ref, cwa_ref, cba_ref),
                                                             (wv_ref, upv_ref, cwv_ref, cbv_ref))):
            u = jnp.dot(hb, w_ref[...], preferred_element_type=F32)
            up_ref[...] = u.astype(BF16)
            halves.append(_conv3(u, prev_ref[c, s], cw_ref, cb_ref[...]))
            prev_ref[c, s] = u[tm - SUBLANES:]
        act, val = halves
        ga, ta = _gelu_t(act)
        ff_ref[...] = (ga * val).astype(BF16)
        fa_ref[...] = (val * _gelu_grad(ta)).astype(BF16)
        fv_ref[...] = ga.astype(BF16)

    def cols(rows, off):
        return pl.BlockSpec((rows, D), lambda i, c: (0, off + c))

    vec = pl.BlockSpec((1, D), lambda i, c: (0, 0))
    row_tile = pl.BlockSpec((tm, D), lambda i, c: (i, 0))
    chunk = pl.BlockSpec((tm, D), lambda i, c: (i, c))
    hshape = jax.ShapeDtypeStruct((t, DFF), BF16)
    return pl.pallas_call(
        body, name="ffn_proj_mid", grid=(t // tm, nc),
        in_specs=[row_tile, vec, vec, vec, cols(D, 0), cols(D, nc), cols(3, 0), cols(3, nc), cols(1, 0), cols(1, nc)],
        out_specs=[row_tile, chunk, chunk, chunk, chunk, chunk],
        out_shape=[jax.ShapeDtypeStruct((t, D), BF16), hshape, hshape, hshape, hshape, hshape],
        scratch_shapes=[pltpu.VMEM((tm, D), BF16), pltpu.VMEM((nc, 2, SUBLANES, D), F32)],
        compiler_params=_cparams(2),
    )(x2, g, scale, shift, w, w, cw, cw, cb, cb)


def _ffn_out_loss(ff, wd, x2, target, gate2, gfin):
    t = x2.shape[0]
    tm = _tile_big(t)

    def body(ff_ref, wd_ref, x2_ref, tg_ref, g2_ref, gf_ref, dx3_ref, loss_ref, dgf_ref, dg2_ref):
        @pl.when(pl.program_id(0) == 0)
        def _():
            loss_ref[...] = jnp.zeros_like(loss_ref)
            dgf_ref[...] = jnp.zeros_like(dgf_ref)
            dg2_ref[...] = jnp.zeros_like(dg2_ref)

        o2 = jnp.dot(ff_ref[...], wd_ref[...], preferred_element_type=F32)
        x3 = x2_ref[...] + g2_ref[...] * o2
        r = lax.rsqrt(jnp.mean(x3 * x3, axis=-1, keepdims=True) + EPS)
        xhat = x3 * r
        err = xhat * gf_ref[...] - tg_ref[...]
        loss_ref[...] += 0.5 * jnp.sum(jnp.mean(err * err, axis=-1, keepdims=True), axis=0, keepdims=True)
        dy = err * (1.0 / D)
        dgf_ref[...] += _colsum(dy * xhat)
        dxh = dy * gf_ref[...]
        dx3 = r * (dxh - xhat * jnp.mean(dxh * xhat, axis=-1, keepdims=True))
        dx3_ref[...] = dx3
        dg2_ref[...] += _colsum(dx3 * o2)

    tile = pl.BlockSpec((tm, D), lambda i: (i, 0))
    vec = _const_spec((1, D))
    return pl.pallas_call(
        body, name="ffn_out_loss", grid=(t // tm,),
        in_specs=[pl.BlockSpec((tm, DFF), lambda i: (i, 0)), _const_spec((DFF, D), True), tile, tile, vec, vec],
        out_specs=[tile, _const_spec((1, 1)), vec, vec],
        out_shape=[jax.ShapeDtypeStruct((t, D), F32), jax.ShapeDtypeStruct((1, 1), F32),
                   jax.ShapeDtypeStruct((1, D), F32), jax.ShapeDtypeStruct((1, D), F32)],
        compiler_params=_cparams(1),
    )(ff, wd, x2, target, gate2, gfin)


def _ffn_down_bwd(dx3, gate2, ff, fa, fv, wd):
    t = dx3.shape[0]
    tm = _tile_big(t)
    nc = DFF // D

    def body(dx3_ref, g2_ref, ff_ref, fa_ref, fv_ref, wd_ref, da_ref, dv_ref, dwd_ref, dcba_ref, dcbv_ref):
        @pl.when(pl.program_id(1) == 0)
        def _():
            for r in (dwd_ref, dcba_ref, dcbv_ref):
                r[...] = jnp.zeros_like(r)

        do2 = (dx3_ref[...] * g2_ref[...]).astype(BF16)
        dwd_ref[...] += _dot_tn(ff_ref[...], do2)
        dff = _dot_nt(do2, wd_ref[...])
        dact = dff * fa_ref[...].astype(F32)
        dval = dff * fv_ref[...].astype(F32)
        da_ref[...] = dact.astype(BF16)
        dv_ref[...] = dval.astype(BF16)
        dcba_ref[...] += _colsum(dact)
        dcbv_ref[...] += _colsum(dval)

    blk = pl.BlockSpec((tm, D), lambda c, i: (i, c))
    vec = pl.BlockSpec((1, D), lambda c, i: (0, c))
    return pl.pallas_call(
        body, name="ffn_down_bwd", grid=(nc, t // tm),
        in_specs=[pl.BlockSpec((tm, D), lambda c, i: (i, 0)), pl.BlockSpec((1, D), lambda c, i: (0, 0)),
                  blk, blk, blk, pl.BlockSpec((D, D), lambda c, i: (c, 0))],
        out_specs=[blk, blk, pl.BlockSpec((D, D), lambda c, i: (c, 0)), vec, vec],
        out_shape=[jax.ShapeDtypeStruct((t, DFF), BF16), jax.ShapeDtypeStruct((t, DFF), BF16),
                   jax.ShapeDtypeStruct((DFF, D), F32),
                   jax.ShapeDtypeStruct((1, DFF), F32), jax.ShapeDtypeStruct((1, DFF), F32)],
        compiler_params=_cparams(2),
    )(dx3, gate2, ff, fa, fv, wd)


def _modnorm_bwd(dh, xv, g, scale):
    r = lax.rsqrt(jnp.mean(xv * xv, axis=-1, keepdims=True) + EPS)
    xhat = xv * r
    dxn = dh * (1.0 + scale)
    dxh = dxn * g
    dx = r * (dxh - xhat * jnp.mean(dxh * xhat, axis=-1, keepdims=True))
    return dx, _colsum(dh), _colsum(dh * (xhat * g)), _colsum(dxn * xhat)


def _ffn_up_bwd(dact, dval, up_a, up_v, cw, wup, x2, dx3, gffn, scale2, o1, gate1):
    t = x2.shape[0]
    tm = _tile_seq(t)
    nt = t // tm
    nc = DFF // D

    def body(da_ref, dan_ref, dv_ref, dvn_ref, ua_ref, uv_ref, cw_ref, w_ref, x2_ref, dx3_ref, g_ref, sc_ref, o1_ref, g1_ref,
             dup_ref, dx2_ref, do1_ref, dcw_ref, dsh_ref, dsc_ref, dg_ref, dg1_ref):
        i = pl.program_id(0)

        @pl.when(i == 0)
        def _():
            for r in (dcw_ref, dsh_ref, dsc_ref, dg_ref, dg1_ref):
                r[...] = jnp.zeros_like(r)

        last = i == nt - 1
        dh = jnp.zeros((tm, D), F32)
        for half, (d_ref, dn_ref, u_ref) in enumerate(((da_ref, dan_ref, ua_ref), (dv_ref, dvn_ref, uv_ref))):
            nxt = jnp.where(last, 0.0, dn_ref[...].astype(F32)[:SUBLANES])
            for c in range(nc):
                c0 = half * DFF + c * D
                dv = d_ref[:, c * D:(c + 1) * D].astype(F32)
                nx = nxt[:, c * D:(c + 1) * D]
                taps = (_shift_up(dv, 2, nx), _shift_up(dv, 1, nx), dv)
                dup = (cw_ref[2:3, c0:c0 + D] * taps[2] + cw_ref[1:2, c0:c0 + D] * taps[1]
                       + cw_ref[0:1, c0:c0 + D] * taps[0]).astype(BF16)
                upv = u_ref[:, c * D:(c + 1) * D].astype(F32)
                for k in range(3):
                    dcw_ref[k:k + 1, c0:c0 + D] += _colsum(taps[k] * upv)
                dup_ref[:, c0:c0 + D] = dup
                dh = dh + _dot_nt(dup, w_ref[:, c0:c0 + D])
        dxn, dsh, dsc, dg = _modnorm_bwd(dh, x2_ref[...], g_ref[...], sc_ref[...])
        dx2 = dx3_ref[...] + dxn
        dx2_ref[...] = dx2
        do1_ref[...] = (dx2 * g1_ref[...]).astype(BF16)
        dsh_ref[...] += dsh
        dsc_ref[...] += dsc
        dg_ref[...] += dg
        dg1_ref[...] += _colsum(dx2 * o1_ref[...].astype(F32))

    tile = pl.BlockSpec((tm, D), lambda i: (i, 0))
    wide = pl.BlockSpec((tm, DFF), lambda i: (i, 0))
    nxt = pl.BlockSpec((HALO, DFF), lambda i: (jnp.minimum((i + 1) * (tm // HALO), t // HALO - 1), 0))
    vec = _const_spec((1, D))
    vshape = jax.ShapeDtypeStruct((1, D), F32)
    return pl.pallas_call(
        body, name="ffn_up_bwd", grid=(nt,),
        in_specs=[wide, nxt, wide, nxt, wide, wide,
                  _const_spec((3, 2 * DFF)), _const_spec((D, 2 * DFF), True),
                  tile, tile, vec, vec, tile, vec],
        out_specs=[pl.BlockSpec((tm, 2 * DFF), lambda i: (i, 0)), tile, tile, _const_spec((3, 2 * DFF)),
                   vec, vec, vec, vec],
        out_shape=[jax.ShapeDtypeStruct((t, 2 * DFF), BF16), jax.ShapeDtypeStruct((t, D), F32),
                   jax.ShapeDtypeStruct((t, D), BF16), jax.ShapeDtypeStruct((3, 2 * DFF), F32),
                   vshape, vshape, vshape, vshape],
        compiler_params=_cparams(1),
    )(dact, dact, dval, dval, up_a, up_v, cw, wup, x2, dx3, gffn, scale2, o1, gate1)


def _xt_y(a, b, name):
    t, k = a.shape
    n = b.shape[1]
    tm = min(1024, t)
    bn = 768 if n % 768 == 0 else D

    def body(a_ref, b_ref, o_ref):
        @pl.when(pl.program_id(1) == 0)
        def _():
            o_ref[...] = jnp.zeros_like(o_ref)

        o_ref[...] += _dot_tn(a_ref[...], b_ref[...])

    return pl.pallas_call(
        body, name=name, grid=(n // bn, t // tm),
        in_specs=[pl.BlockSpec((tm, k), lambda j, i: (i, 0)), pl.BlockSpec((tm, bn), lambda j, i: (i, j))],
        out_specs=pl.BlockSpec((k, bn), lambda j, i: (0, j)),
        out_shape=jax.ShapeDtypeStruct((k, n), F32),
        compiler_params=_cparams(2),
    )(a, b)


def _acc_spec(shape, index):
    return pl.BlockSpec(shape, lambda *_: index, pipeline_mode=pl.Buffered(1))


def _out_bwd(do1, wout, merged, ya, yb, z, h1):
    t = do1.shape[0]
    tm = _tile_big(t)

    def body(do1_ref, wo_ref, mg_ref, ya_ref, yb_ref, ga_ref, gb_ref, h1_ref,
             dya_ref, dyb_ref, dz_ref, dwo_ref, dwin_ref):
        @pl.when(pl.program_id(0) == 0)
        def _():
            dwo_ref[...] = jnp.zeros_like(dwo_ref)
            dwin_ref[...] = jnp.zeros_like(dwin_ref)

        do1v = do1_ref[...]
        dwo_ref[...] += _dot_tn(mg_ref[...], do1v)
        dm = _dot_nt(do1v, wo_ref[...])
        sa = _sigmoid(ga_ref[...].astype(F32))
        sb = _sigmoid(gb_ref[...].astype(F32))
        dya_ref[...] = (dm * sa).astype(BF16)
        dyb_ref[...] = (dm * sb).astype(BF16)
        dga = (dm * ya_ref[...].astype(F32) * sa * (1.0 - sa)).astype(BF16)
        dgb = (dm * yb_ref[...].astype(F32) * sb * (1.0 - sb)).astype(BF16)
        dz_ref[:, 0:D] = dga
        dz_ref[:, D:2 * D] = dgb
        h1v = h1_ref[...]
        dwin_ref[:, 0:D] += _dot_tn(h1v, dga)
        dwin_ref[:, D:2 * D] += _dot_tn(h1v, dgb)

    tile = pl.BlockSpec((tm, D), lambda i: (i, 0))
    bshape = jax.ShapeDtypeStruct((t, D), BF16)
    return pl.pallas_call(
        body, name="out_bwd", grid=(t // tm,),
        in_specs=[tile, _const_spec((D, D), True), tile, tile, tile,
                  pl.BlockSpec((tm, D), lambda i: (i, 4)), pl.BlockSpec((tm, D), lambda i: (i, 5)), tile],
        out_specs=[tile, tile, pl.BlockSpec((tm, 2 * D), lambda i: (i, 2)), _acc_spec((D, D), (0, 0)),
                   _acc_spec((D, 2 * D), (0, 2))],
        out_shape=[bshape, bshape, jax.ShapeDtypeStruct((t, NCOL_IN), BF16), jax.ShapeDtypeStruct((D, D), F32),
                   jax.ShapeDtypeStruct((D, NCOL_IN), F32)],
        compiler_params=_cparams(1),
    )(do1, wout, merged, ya, yb, z, z, h1)


def _rnn_bwd(dya, ya_pre, wba, h1, z, saved, h, dz, dwin, cw, wa, wx, lam):
    t = z.shape[0]
    tm = _tile_seq(t)
    nt = t // tm
    ngrp = tm // SUBLANES
    hpt = tm // HALO

    def body(dya_ref, yap_ref, wba_ref, h1_ref, xr_ref, xc_ref, ra_ref, ia_ref, gg_ref, hg_ref, h_ref, hp_ref,
             dz_any, dwin_any, cw_ref, wa_ref, wx_ref, lam_ref,
             dz_ref, dwin_ref, dwba_ref, dcw_ref, dcb_ref, dwa_ref, dba_ref, dwx_ref, dbx_ref, dlam_ref,
             a_first, g_first, dxc_first, b_scr, d_scr, g_scr):
        del dz_any, dwin_any
        i = pl.program_id(0)

        @pl.when(i == 0)
        def _():
            for r in (dwin_ref, dwba_ref, dcw_ref, dcb_ref, dwa_ref, dba_ref, dwx_ref, dbx_ref, dlam_ref,
                      a_first, g_first, dxc_first):
                r[...] = jnp.zeros_like(r)

        dya_v = dya_ref[...]
        dwba_ref[...] += _dot_tn(yap_ref[...], dya_v)
        dyap_v = _dot_nt(dya_v, wba_ref[...])
        h1v = h1_ref[...]

        first_tile = i == nt - 1
        xc = xc_ref[...].astype(F32)
        ra = ra_ref[...].astype(F32)
        ia = ia_ref[...].astype(F32)
        lam_v = lam_ref[...]
        ls = _log_sigmoid(lam_v)
        la = LRU_C * ra * ls
        a = jnp.exp(la)
        mult = jnp.sqrt(-jnp.tanh(la) * (1.0 + a * a))
        hprev8 = jnp.where(first_tile, 0.0, hp_ref[...][HALO - SUBLANES:])
        h_prev = _shift_down(h_ref[...], 1, hprev8)
        dgr = (dyap_v * hg_ref[...].astype(F32)).astype(BF16)
        dz_ref[:, D:2 * D] = dgr
        dwin_ref[:, D:2 * D] += _dot_tn(h1v, dgr)

        b_scr[...] = _shift_up(a, 1, a_first[...])
        d_scr[...] = dyap_v * gg_ref[...].astype(F32)
        row = _row_iota(D)

        def grp(jj, carry):
            r0 = pl.multiple_of((ngrp - 1 - jj) * SUBLANES, SUBLANES)
            bv = b_scr[pl.ds(r0, SUBLANES), :]
            dv = d_scr[pl.ds(r0, SUBLANES), :]
            for d in (1, 2, 4):
                m = row < SUBLANES - d
                dv = jnp.where(m, dv + bv * pltpu.roll(dv, SUBLANES - d, 0), dv)
                bv = jnp.where(m, bv * pltpu.roll(bv, SUBLANES - d, 0), bv)
            gv = dv + bv * carry
            g_scr[pl.ds(r0, SUBLANES), :] = gv
            return gv[0:1, :]

        lax.fori_loop(0, ngrp, grp, g_first[0:1, :])
        g = g_scr[...]
        a_first[...] = a[:SUBLANES]
        g_first[...] = g[:SUBLANES]

        da = g * h_prev
        gx = g * xc
        dmult = gx * ia
        dia = gx * mult
        dxc = g * (mult * ia)
        dla = da * a - dmult * (a * a) / mult
        dra = dla * (LRU_C * ls)
        dlam_ref[...] += _colsum(dla * ra) * (LRU_C * _sigmoid(-lam_v))
        dpa = dra * ra * (1.0 - ra)
        dpx = dia * ia * (1.0 - ia)
        dba_ref[...] += _colsum(dpa)
        dbx_ref[...] += _colsum(dpx)
        dpab = dpa.astype(BF16)
        dpxb = dpx.astype(BF16)
        xcb = xc_ref[...]
        for hd in range(NH):
            sl = slice(hd * HD, (hd + 1) * HD)
            dwa_ref[hd] += _dot_tn(xcb[:, sl], dpab[:, sl])
            dwx_ref[hd] += _dot_tn(xcb[:, sl], dpxb[:, sl])
        dxc = dxc + _heads_nt(dpab, wa_ref) + _heads_nt(dpxb, wx_ref)

        nxt = dxc_first[...]
        taps = (_shift_up(dxc, 3, nxt), _shift_up(dxc, 2, nxt), _shift_up(dxc, 1, nxt), dxc)
        dxr = cw_ref[0:1, :] * taps[0]
        for k in range(1, 4):
            dxr = dxr + cw_ref[k:k + 1, :] * taps[k]
        dxrb = dxr.astype(BF16)
        dz_ref[:, 0:D] = dxrb
        dwin_ref[:, 0:D] += _dot_tn(h1v, dxrb)
        dxc_first[...] = dxc[:SUBLANES]
        dcb_ref[...] += _colsum(dxc)
        xr = xr_ref[...].astype(F32)
        for k in range(4):
            dcw_ref[k:k + 1, :] += _colsum(taps[k] * xr)

    def rev(col):
        return lambda i: (nt - 1 - i, col)

    vec = _const_spec((1, D))
    wspec = _const_spec((NH, HD, HD))
    vshape = jax.ShapeDtypeStruct((1, D), F32)
    wshape = jax.ShapeDtypeStruct((NH, HD, HD), F32)
    any_spec = pl.BlockSpec(memory_space=pl.ANY)
    tile = pl.BlockSpec((tm, D), rev(0))
    outs = pl.pallas_call(
        body, name="rnn_bwd", grid=(nt,),
        in_specs=[tile, tile, _const_spec((D, D), True), tile, tile, tile, tile, tile, tile, tile, tile,
                  pl.BlockSpec((HALO, D), lambda i: (jnp.maximum((nt - 1 - i) * hpt - 1, 0), 0)),
                  any_spec, any_spec, _const_spec((4, D)), wspec, wspec, vec],
        out_specs=[pl.BlockSpec((tm, 2 * D), rev(0)), _acc_spec((D, 2 * D), (0, 0)), _acc_spec((D, D), (0, 0)),
                   _const_spec((4, D)), vec, wspec, vec, wspec, vec, vec],
        out_shape=[jax.ShapeDtypeStruct((t, NCOL_IN), BF16), jax.ShapeDtypeStruct((D, NCOL_IN), F32),
                   jax.ShapeDtypeStruct((D, D), F32), jax.ShapeDtypeStruct((4, D), F32), vshape,
                   wshape, vshape, wshape, vshape, vshape],
        scratch_shapes=[pltpu.VMEM((SUBLANES, D), F32), pltpu.VMEM((SUBLANES, D), F32), pltpu.VMEM((SUBLANES, D), F32),
                        pltpu.VMEM((tm, D), F32), pltpu.VMEM((tm, D), F32), pltpu.VMEM((tm, D), F32)],
        input_output_aliases={12: 0, 13: 1},
        compiler_params=_cparams(1),
    )(dya, ya_pre, wba, h1, z, *saved, h, h, dz, dwin, cw, wa, wx, lam)
    return outs


def _sgu_bwd(dyb, yb_pre, wbb, h1, saved, dz, dwin, lng, lnb, wmt, mask):
    t = dyb.shape[0]
    tm = _tile_big(t)

    def body(dyb_ref, ybp_ref, wbb_ref, h1_ref, gu_ref, mg_ref, vh_ref, gpv_ref, rstd_ref, dz_any, dwin_any,
             lng_ref, lnb_ref, wmt_ref, mask_ref,
             dz_ref, dwin_ref, dwbb_ref, dws_ref, dbst_ref, dlng_ref, dlnb_ref):
        del dz_any, dwin_any

        @pl.when(pl.program_id(0) == 0)
        def _():
            for r in (dwin_ref, dwbb_ref, dws_ref, dbst_ref, dlng_ref, dlnb_ref):
                r[...] = jnp.zeros_like(r)

        lng_v = lng_ref[...]
        vhat = vh_ref[...].astype(F32)
        vb = (vhat * lng_v + lnb_ref[...]).astype(BF16)
        rstd = rstd_ref[...]
        dyb_v = dyb_ref[...]
        dwbb_ref[...] += _dot_tn(ybp_ref[...], dyb_v)
        dyb = _dot_nt(dyb_v, wbb_ref[...])
        h1v = h1_ref[...]
        dzu = (dyb * mg_ref[...].astype(F32)).astype(BF16)
        dz_ref[:, 0:D] = dzu
        dwin_ref[:, 0:D] += _dot_tn(h1v, dzu)
        dmix = dyb * gu_ref[...].astype(F32)
        dmb = dmix.astype(BF16)
        rows = []
        lane = lax.broadcasted_iota(jnp.int32, (HD, NH), 1)
        dbst = jnp.zeros((HD, NH), F32)
        for b0 in range(0, tm, HD):
            cols = []
            for g in range(NH):
                sl = slice(g * HD, (g + 1) * HD)
                dmg = dmb[b0:b0 + HD, sl]
                dws_ref[g] += _dot_nt(dmg, vb[b0:b0 + HD, sl]) * mask_ref[...]
                cols.append(jnp.dot(wmt_ref[g], dmg, preferred_element_type=F32))
                dbst = dbst + jnp.where(lane == g, jnp.sum(dmix[b0:b0 + HD, sl], axis=1, keepdims=True), 0.0)
            rows.append(jnp.concatenate(cols, axis=1))
        dbst_ref[...] += dbst
        dvln = jnp.concatenate(rows, axis=0) if len(rows) > 1 else rows[0]
        dlng_ref[...] += _colsum(dvln * vhat)
        dlnb_ref[...] += _colsum(dvln)
        dvh = dvln * lng_v
        dgv = rstd * (dvh - jnp.mean(dvh, axis=-1, keepdims=True)
                      - vhat * jnp.mean(dvh * vhat, axis=-1, keepdims=True))
        dzv = (dgv * gpv_ref[...].astype(F32)).astype(BF16)
        dz_ref[:, D:2 * D] = dzv
        dwin_ref[:, D:2 * D] += _dot_tn(h1v, dzv)

    vec = _const_spec((1, D))
    wspec = _const_spec((NH, HD, HD))
    vshape = jax.ShapeDtypeStruct((1, D), F32)
    tile = pl.BlockSpec((tm, D), lambda i: (i, 0))
    any_spec = pl.BlockSpec(memory_space=pl.ANY)
    return pl.pallas_call(
        body, name="sgu_bwd", grid=(t // tm,),
        in_specs=[tile, tile, _const_spec((D, D), True), tile, tile, tile, tile, tile,
                  pl.BlockSpec((tm, 1), lambda i: (i, 0)), any_spec, any_spec,
                  vec, vec, wspec, _const_spec((HD, HD))],
        out_specs=[pl.BlockSpec((tm, 2 * D), lambda i: (i, 1)), _acc_spec((D, 2 * D), (0, 1)), _acc_spec((D, D), (0, 0)),
                   wspec, _const_spec((HD, NH)), vec, vec],
        out_shape=[jax.ShapeDtypeStruct((t, NCOL_IN), BF16), jax.ShapeDtypeStruct((D, NCOL_IN), F32),
                   jax.ShapeDtypeStruct((D, D), F32), jax.ShapeDtypeStruct((NH, HD, HD), F32),
                   jax.ShapeDtypeStruct((HD, NH), F32), vshape, vshape],
        input_output_aliases={9: 0, 10: 1},
        compiler_params=_cparams(1),
    )(dyb, yb_pre, wbb, h1, *saved, dz, dwin, lng, lnb, wmt, mask)


def _in_bwd(dz, win, x, dx2, g, scale1):
    t = x.shape[0]
    tm = _tile_big(t)

    def body(dz_ref, w_ref, x_ref, dx2_ref, g_ref, sc_ref, dx_ref, dsh_ref, dsc_ref, dg_ref):
        @pl.when(pl.program_id(0) == 0)
        def _():
            for r in (dsh_ref, dsc_ref, dg_ref):
                r[...] = jnp.zeros_like(r)

        dh = jnp.zeros((tm, D), F32)
        for c0 in range(0, NCOL_IN, D):
            dh = dh + _dot_nt(dz_ref[:, c0:c0 + D], w_ref[:, c0:c0 + D])
        dxn, dsh, dsc, dg = _modnorm_bwd(dh, x_ref[...], g_ref[...], sc_ref[...])
        dx_ref[...] = dx2_ref[...] + dxn
        dsh_ref[...] += dsh
        dsc_ref[...] += dsc
        dg_ref[...] += dg

    tile = pl.BlockSpec((tm, D), lambda i: (i, 0))
    vec = _const_spec((1, D))
    vshape = jax.ShapeDtypeStruct((1, D), F32)
    return pl.pallas_call(
        body, name="in_bwd", grid=(t // tm,),
        in_specs=[pl.BlockSpec((tm, NCOL_IN), lambda i: (i, 0)), _const_spec((D, NCOL_IN), True), tile, tile, vec, vec],
        out_specs=[tile, vec, vec, vec],
        out_shape=[jax.ShapeDtypeStruct((t, D), F32), vshape, vshape, vshape],
        compiler_params=_cparams(1),
    )(dz, win, x, dx2, g, scale1)


def _mod_cols(c_all, w_ada, b_cols):
    nb, cols = c_all.shape[0], w_ada.shape[1]

    def body(c_ref, w_ref, b_ref, o_ref):
        cv = c_ref[...]
        ca = (cv * _sigmoid(cv)).astype(BF16)
        o_ref[...] = jnp.dot(ca, w_ref[...].astype(BF16), preferred_element_type=F32) + b_ref[...]

    return pl.pallas_call(body, name="mod_cols", out_shape=jax.ShapeDtypeStruct((nb, cols), F32))(c_all, w_ada, b_cols)


def _ada_grad(c_all, dmod_cols):
    cols = dmod_cols.shape[1]

    def body(c_ref, d_ref, o_ref):
        cv = c_ref[...]
        ca = (cv * _sigmoid(cv)).astype(BF16)
        o_ref[...] = _dot_tn(ca, d_ref[...].astype(BF16))

    return pl.pallas_call(body, name="ada_grad", out_shape=jax.ShapeDtypeStruct((D, cols), F32))(c_all, dmod_cols)


def _adamw_update(w, m, v, g):
    bc1 = 1.0 - ADAM_B1 ** ADAM_STEP
    bc2 = 1.0 - ADAM_B2 ** ADAM_STEP
    mn = ADAM_B1 * m + (1.0 - ADAM_B1) * g
    vn = ADAM_B2 * v + (1.0 - ADAM_B2) * (g * g)
    return -ADAM_LR * ((mn / bc1) / (jnp.sqrt(vn / bc2) + ADAM_EPS) + ADAM_WD * w), mn, vn


def _adamw_group(names, ws, ms, vs, packs, name):
    n = len(names)
    starts, r0 = [], 0
    for w in ws:
        starts.append(r0)
        r0 += _pack_rows(w.shape)

    def body(*refs):
        w_refs, m_refs, v_refs, p_ref = refs[:n], refs[n:2 * n], refs[2 * n:3 * n], refs[3 * n]
        outs = refs[3 * n + 1:]
        for k in range(n):
            rows = _pack_rows(ws[k].shape)
            g = None
            for dev in range(N_DEV):
                if ws[k].shape[0] == 1:
                    term = jnp.concatenate(
                        [p_ref[dev, starts[k] + r:starts[k] + r + 1, :] for r in range(rows)], axis=1)
                else:
                    term = p_ref[dev, starts[k]:starts[k] + rows, :]
                g = term if g is None else g + term
            delta, mn, vn = _adamw_update(w_refs[k][...], m_refs[k][...], v_refs[k][...], g)
            for o_ref, val in zip(outs[4 * k:4 * k + 4], (g, delta, mn, vn)):
                o_ref[...] = val

    shapes = [jax.ShapeDtypeStruct(w.shape, F32) for w in ws for _ in range(4)]
    outs = pl.pallas_call(body, name=name, out_shape=shapes,
                          compiler_params=pltpu.CompilerParams(vmem_limit_bytes=VMEM_LIMIT))(*ws, *ms, *vs, packs)
    return {nm: tuple(outs[4 * k:4 * k + 4]) for k, nm in enumerate(names)}


def _adamw(w, m, v, parts, name):
    rows, cols = w.shape
    tr = _row_tile(rows, cols)
    stacked = [p.ndim == 3 for p in parts]

    def body(*refs):
        w_ref, m_ref, v_ref = refs[:3]
        p_refs = refs[3:3 + len(parts)]
        g_ref, d_ref, mo_ref, vo_ref = refs[3 + len(parts):]
        g = None
        for p_ref, st in zip(p_refs, stacked):
            terms = [p_ref[k].astype(F32) for k in range(p_ref.shape[0])] if st else [p_ref[...].astype(F32)]
            for term in terms:
                g = term if g is None else g + term
        delta, mn, vn = _adamw_update(w_ref[...], m_ref[...], v_ref[...], g)
        g_ref[...] = g
        mo_ref[...] = mn
        vo_ref[...] = vn
        d_ref[...] = delta

    tile = pl.BlockSpec((tr, cols), lambda i: (i, 0))
    p_specs = [pl.BlockSpec((p.shape[0], tr, cols), lambda i: (0, i, 0)) if st else tile for p, st in zip(parts, stacked)]
    shp = jax.ShapeDtypeStruct((rows, cols), F32)
    return pl.pallas_call(
        body, name=name, grid=(rows // tr,),
        in_specs=[tile, tile, tile] + p_specs, out_specs=[tile] * 4, out_shape=[shp] * 4,
        compiler_params=_cparams(1),
    )(w, m, v, *parts)


def _mesh_pos():
    return lax.axis_index("x"), lax.axis_index("y"), lax.axis_index("c")


def _other_chips(x, y):
    return [(1 - x, y), (x, 1 - y), (1 - x, 1 - y)]


def _block_of(ref, axis, index, size):
    if axis == 0:
        return ref.at[index]
    return ref.at[:, pl.ds(pl.multiple_of(index * size, 128), size)]


def _all_gather(shards, axes, name):
    n = len(shards)
    per = 7

    def body(*refs):
        ins, outs, done = refs[:n], refs[n:2 * n], refs[2 * n]
        send_sems, recv_sems, local_sems = refs[2 * n + 1:]
        x, y, c = _mesh_pos()
        me, sibling = (x, y, c), (x, y, 1 - c)
        chips = _other_chips(x, y)

        def rows(a, pos):
            return _block_of(outs[a], axes[a], 4 * pos[0] + 2 * pos[1] + pos[2], shards[a].shape[-1])

        def copy(a, k, block, to, src=None):
            return pltpu.make_async_remote_copy(
                src_ref=rows(a, block) if src is None else src, dst_ref=rows(a, block),
                send_sem=send_sems.at[a * per + k], recv_sem=recv_sems.at[a * per + k],
                device_id=to, device_id_type=MESH_IDS)

        mine = [pltpu.make_async_copy(ins[a], rows(a, me), local_sems.at[a]) for a in range(n)]
        for cp in mine:
            cp.start()
        first = []
        for a in range(n):
            first.append(copy(a, 0, me, sibling, src=ins[a]))
            first += [copy(a, 1 + j, me, (*chip, c), src=ins[a]) for j, chip in enumerate(chips)]
        for cp in first:
            cp.start()
        passed = []
        for j, chip in enumerate(chips):
            for a in range(n):
                copy(a, 1 + j, (*chip, c), me).wait_recv()
                fwd = copy(a, 4 + j, (*chip, c), sibling)
                fwd.start()
                passed.append(fwd)
        for a in range(n):
            copy(a, 0, sibling, me).wait_recv()
            for j, chip in enumerate(chips):
                copy(a, 4 + j, (*chip, 1 - c), me).wait_recv()
        for cp in first + passed:
            cp.wait_send()
        for cp in mine:
            cp.wait()
        done[...] = jnp.zeros_like(done)

    def full_shape(s, ax):
        return (N_DEV,) + s.shape if ax == 0 else s.shape[:-1] + (N_DEV * s.shape[-1],)

    any_spec = pl.BlockSpec(memory_space=pl.ANY)
    outs = pl.pallas_call(
        body, name=name,
        in_specs=[any_spec] * n, out_specs=[any_spec] * n + [pl.BlockSpec(memory_space=pltpu.VMEM)],
        out_shape=[jax.ShapeDtypeStruct(full_shape(s, ax), s.dtype) for s, ax in zip(shards, axes)]
        + [jax.ShapeDtypeStruct((SUBLANES, LANES), F32)],
        scratch_shapes=[pltpu.SemaphoreType.DMA((n * per,)), pltpu.SemaphoreType.DMA((n * per,)),
                        pltpu.SemaphoreType.DMA((n,))],
    )(*shards)
    return outs[:n], outs[n]


def _chip_blocks(x, y):
    return [(x, y)] + _other_chips(x, y)


def _sibling_reduce(gs, axis, name):
    g0, n = gs[0], len(gs)
    rows, cols = (g0.shape[1], g0.shape[2]) if axis == 0 else (g0.shape[0], g0.shape[1] // N_DEV)
    chunk = math.gcd(rows, 64)

    def body(*refs):
        g_refs, own_refs, pay_refs = refs[:n], refs[n:2 * n], refs[2 * n:3 * n]
        send_buf, keep_buf, recv_buf, send_sems, recv_sems, stage_sems, keep_sems = refs[3 * n:]
        x, y, c = _mesh_pos()
        sibling = (x, y, 1 - c)
        chips = _chip_blocks(x, y)
        stage, keep, push = [], [], []
        for a in range(n):
            for j, (px, py) in enumerate(chips):
                s = 4 * a + j
                theirs = _block_of(g_refs[a], axis, 4 * px + 2 * py + (1 - c), cols)
                ours = _block_of(g_refs[a], axis, 4 * px + 2 * py + c, cols)
                stage.append(pltpu.make_async_copy(theirs, send_buf.at[s], stage_sems.at[s]))
                keep.append(pltpu.make_async_copy(ours, keep_buf.at[s], keep_sems.at[s]))
                push.append(pltpu.make_async_remote_copy(
                    src_ref=send_buf.at[s], dst_ref=recv_buf.at[s], send_sem=send_sems.at[s],
                    recv_sem=recv_sems.at[s], device_id=sibling, device_id_type=MESH_IDS))
        for cp in stage + keep:
            cp.start()
        for s in range(4 * n):
            stage[s].wait()
            push[s].start()
        for s in range(4 * n):
            push[s].wait_recv()
            keep[s].wait()
            a, j = divmod(s, 4)
            dst = own_refs[a] if j == 0 else pay_refs[a].at[j - 1]

            def add(r, carry, s=s, dst=dst):
                sl = pl.ds(pl.multiple_of(r * chunk, chunk), chunk)
                dst[sl, :] = (keep_buf[s, sl, :] + recv_buf[s, sl, :]).astype(dst.dtype)
                return carry

            lax.fori_loop(0, rows // chunk, add, 0)
        for cp in push:
            cp.wait_send()

    vmem = pl.BlockSpec(memory_space=pltpu.VMEM)
    buf = pltpu.VMEM((4 * n, rows, cols), F32)
    sems = pltpu.SemaphoreType.DMA((4 * n,))
    outs = pl.pallas_call(
        body, name=name,
        in_specs=[pl.BlockSpec(memory_space=pl.ANY)] * n, out_specs=[vmem] * (2 * n),
        out_shape=[jax.ShapeDtypeStruct((rows, cols), F32)] * n + [jax.ShapeDtypeStruct((3, rows, cols), BF16)] * n,
        scratch_shapes=[buf, buf, buf, sems, sems, sems, sems],
        compiler_params=pltpu.CompilerParams(vmem_limit_bytes=VMEM_LIMIT),
    )(*gs)
    return list(zip(outs[:n], outs[n:]))


_HBM_SPEC = pl.BlockSpec(memory_space=pltpu.HBM)
_SEM_SPEC = pl.BlockSpec(memory_space=pltpu.SEMAPHORE)
_SIDE_EFFECT = pltpu.SideEffectType.DATAFLOW_SIDE_EFFECTING


def _exchange_start(name, srcs, lands, plan, n_copies):
    nb = len(srcs) + len(lands)

    def body(*refs):
        bufs, send_sems, recv_sems, token = refs[:nb], refs[nb], refs[nb + 1], refs[-1]
        for cp in plan(bufs[:len(srcs)], bufs[len(srcs):], send_sems, recv_sems):
            cp.start()
        token[...] = jnp.zeros_like(token)

    arrays = list(srcs) + list(lands)
    outs = pl.pallas_call(
        body, name=name,
        out_shape=(pltpu.SemaphoreType.DMA((n_copies,)), pltpu.SemaphoreType.DMA((n_copies,)),
                   *[pltpu.HBM(a.shape, a.dtype) for a in arrays], jax.ShapeDtypeStruct((SUBLANES, LANES), F32)),
        in_specs=[_HBM_SPEC] * nb,
        out_specs=(_SEM_SPEC, _SEM_SPEC, *[_HBM_SPEC] * nb, pl.BlockSpec(memory_space=pltpu.VMEM)),
        input_output_aliases={k: 2 + k for k in range(nb)},
        compiler_params=pltpu.CompilerParams(has_side_effects=_SIDE_EFFECT),
    )(*[pltpu.with_memory_space_constraint(a, pltpu.HBM) for a in arrays])
    return outs[0], outs[1], outs[2:2 + len(srcs)], outs[2 + len(srcs):2 + nb], outs[-1]


def _exchange_wait(name, send_sems, recv_sems, srcs, lands, plan, after):
    nb = len(srcs) + len(lands)
    after = list(after)

    def body(*refs):
        bufs, send_ref, recv_ref = refs[:nb], refs[nb], refs[nb + 1]
        for cp in plan(bufs[:len(srcs)], bufs[len(srcs):], send_ref, recv_ref):
            cp.wait_send()
            cp.wait_recv()

    arrays = list(srcs) + list(lands)
    outs = pl.pallas_call(
        body, name=name,
        out_shape=tuple(pltpu.HBM(a.shape, a.dtype) for a in arrays),
        in_specs=[_HBM_SPEC] * nb + [_SEM_SPEC, _SEM_SPEC] + [pl.BlockSpec(memory_space=pl.ANY)] * len(after),
        out_specs=tuple([_HBM_SPEC] * nb),
        input_output_aliases={k: k for k in range(nb)},
        compiler_params=pltpu.CompilerParams(has_side_effects=_SIDE_EFFECT),
    )(*arrays, send_sems, recv_sems, *after)
    return outs[len(srcs):]


def _gather_plan(axes, sizes):
    def plan(src_refs, land_refs, send_sems, recv_sems):
        x, y, c = _mesh_pos()
        copies = []
        for a, (src, land) in enumerate(zip(src_refs, land_refs)):
            mine = _block_of(land, axes[a], 4 * x + 2 * y + c, sizes[a])
            for k in range(1, N_DEV):
                peer = (1 - x if k & 4 else x, 1 - y if k & 2 else y, 1 - c if k & 1 else c)
                idx = a * (N_DEV - 1) + k - 1
                copies.append(pltpu.make_async_remote_copy(
                    src_ref=src, dst_ref=mine, send_sem=send_sems.at[idx], recv_sem=recv_sems.at[idx],
                    device_id=peer, device_id_type=MESH_IDS))
        return copies
    return plan


def _chip_plan(src_refs, land_refs, send_sems, recv_sems):
    x, y, c = _mesh_pos()
    copies = []
    for a, (src, land) in enumerate(zip(src_refs, land_refs)):
        for j, chip in enumerate(_other_chips(x, y)):
            copies.append(pltpu.make_async_remote_copy(
                src_ref=src.at[j], dst_ref=land.at[j], send_sem=send_sems.at[3 * a + j],
                recv_sem=recv_sems.at[3 * a + j], device_id=(*chip, c), device_id_type=MESH_IDS))
    return copies


def _own_block_placed(shard, axis, me):
    if axis == 0:
        full = lax.empty((N_DEV,) + shard.shape, shard.dtype)
        return lax.dynamic_update_slice(full, shard[None], (me,) + (0,) * shard.ndim)
    rows, cols = shard.shape

    def body(me_ref, s_ref, o_ref):
        del me_ref
        o_ref[...] = s_ref[...]

    return pl.pallas_call(
        body, name="place_own_columns",
        grid_spec=pltpu.PrefetchScalarGridSpec(
            num_scalar_prefetch=1, grid=(1,),
            in_specs=[pl.BlockSpec((rows, cols), lambda i, me_ref: (0, 0))],
            out_specs=pl.BlockSpec((rows, cols), lambda i, me_ref: (0, me_ref[0]))),
        out_shape=jax.ShapeDtypeStruct((rows, N_DEV * cols), shard.dtype),
    )(jnp.reshape(me, (1,)).astype(jnp.int32), shard)


def _local_step(x, target, mod, win, late_weights, p, grads_ready=None):
    shift1, scale1, gate1, shift2, scale2, gate2 = (mod[k] for k in range(6))

    def after_token(v, token):
        return v if token is None else v + token[0:1, 0:1]
    wa, wx = p["lru_w_a"].astype(BF16), p["lru_w_x"].astype(BF16)
    mask = jnp.tril(jnp.ones((HD, HD), F32))
    wm = (p["sgu_w_s"] * mask).astype(BF16)
    wmt = jnp.swapaxes(wm, 1, 2)
    bst = jnp.transpose(p["sgu_b_s"])

    h1, z = _norm_proj(x, p["norm_mix_g"], scale1, shift1, win, "mix_proj")
    hstate, ya_pre, *rnn_saved = _rnn_fwd(
        z, p["rnn_conv_w"], p["rnn_conv_b"], wa, p["lru_b_a"], wx, p["lru_b_x"], p["lru_lambda"])
    yb_pre, *sgu_saved = _sgu_fwd(z, p["sgu_ln_g"], p["sgu_ln_b"], wm, bst)
    wba, wbb, wout = late_weights("merge", [ya_pre, yb_pre])
    x2, ya, yb, merged, o1 = _merge_fwd(ya_pre, yb_pre, z, x, gate1, wba, wbb, wout)
    wup = late_weights("ffn_up", [x2])
    h2, up_a, up_v, ff, fa, fv = _ffn_proj_mid(
        x2, p["norm_ffn_g"], scale2, shift2, wup, p["ffn_conv_w"], p["ffn_conv_b"])
    wd = late_weights("ffn_down", [ff])
    dx3, loss, d_gfin, d_gate2 = _ffn_out_loss(ff, wd, x2, target, gate2, p["norm_final_g"])

    dact, dval, d_wd, dcb_a, dcb_v = _ffn_down_bwd(dx3, gate2, ff, fa, fv, wd)
    dup, dx2, do1, d_cwf, d_shift2, d_scale2, d_gffn, d_gate1 = _ffn_up_bwd(
        dact, dval, up_a, up_v, p["ffn_conv_w"], wup, x2, dx3, p["norm_ffn_g"], scale2, o1, gate1)
    d_wup = _xt_y(h2, dup, "w_up_grad")
    ready = grads_ready if grads_ready else (lambda stage, big, small: None)
    token = ready("ffn", {"w_up": d_wup, "w_down": d_wd}, {})

    dya, dyb, dz, d_wout, d_win = _out_bwd(do1, wout, merged, ya, yb, z, h1)
    dz, d_win, d_wba, d_cw, d_cb, d_wa, d_ba, d_wx, d_bx, d_lam = _rnn_bwd(
        dya, ya_pre, wba, h1, z, rnn_saved, hstate, dz, d_win, p["rnn_conv_w"], wa, wx,
        after_token(p["lru_lambda"], token))
    small = {
        "rnn_conv_w": d_cw, "rnn_conv_b": d_cb, "lru_w_a": d_wa, "lru_b_a": d_ba, "lru_w_x": d_wx, "lru_b_x": d_bx,
        "lru_lambda": d_lam, "norm_ffn_g": d_gffn, "ffn_conv_w": d_cwf,
        "ffn_conv_b": jnp.concatenate([dcb_a, dcb_v], axis=1), "norm_final_g": d_gfin,
    }
    token = ready("rnn", {}, small)
    dz, d_win, d_wbb, d_ws, d_bst, d_lng, d_lnb = _sgu_bwd(
        dyb, yb_pre, wbb, h1, sgu_saved, dz, d_win, p["sgu_ln_g"], after_token(p["sgu_ln_b"], token), wmt, mask)
    sgu_small = {"sgu_ln_g": d_lng, "sgu_ln_b": d_lnb, "sgu_w_s": d_ws, "sgu_b_s": jnp.transpose(d_bst)}
    mixer = {"w_in": d_win, "w_out": d_wout, "w_branch_a": d_wba, "w_branch_b": d_wbb}
    token = ready("mixer", mixer, sgu_small)
    grad_x, d_shift1, d_scale1, d_gmix = _in_bwd(dz, win, x, dx2, after_token(p["norm_mix_g"], token), scale1)

    small.update(sgu_small)
    small["norm_mix_g"] = d_gmix
    dmod = jnp.stack([d_shift1, d_scale1, d_gate1, d_shift2, d_scale2, d_gate2])
    big = {"w_in": d_win, "w_up": d_wup, "w_branch_a": d_wba, "w_branch_b": d_wbb, "w_out": d_wout, "w_down": d_wd}
    return loss, grad_x, big, small, dmod


LAST_REP = ["b_ada", "norm_mix_g"]
EARLY_REP = ["rnn_conv_b", "lru_w_a", "lru_b_a", "lru_w_x", "lru_b_x", "lru_lambda", "norm_ffn_g", "ffn_conv_b",
             "norm_final_g"]
MID_REP = ["sgu_ln_g", "sgu_ln_b", "sgu_w_s", "sgu_b_s"]
COL_SHARDED = ["rnn_conv_w", "ffn_conv_w"]
SMALL_GROUPS = {"rnn": EARLY_REP + COL_SHARDED, "mixer": MID_REP, "last": LAST_REP}
REPLICATED = LAST_REP + EARLY_REP + MID_REP
SMALL_NAMES = REPLICATED + COL_SHARDED
BIG_NAMES = ["w_in", "w_up", "w_branch_a", "w_branch_b", "w_out", "w_down"]
BIG_AXES = [1, 1, 0, 0, 0, 0]
WEIGHTS = ["w_ada", "b_ada", "norm_mix_g", "w_in", "rnn_conv_w", "rnn_conv_b", "lru_w_a", "lru_b_a", "lru_w_x",
           "lru_b_x", "lru_lambda", "sgu_ln_g", "sgu_ln_b", "sgu_w_s", "sgu_b_s", "w_branch_a", "w_branch_b",
           "w_out", "norm_ffn_g", "w_up", "ffn_conv_w", "ffn_conv_b", "w_down", "norm_final_g"]
LANES = 128


def _pack_rows(shape):
    return math.prod(shape) // LANES


def _pack(arrays):
    return jnp.concatenate([a.reshape(-1, LANES) for a in arrays], axis=0)


def kernel(x, c, w_ada, b_ada, norm_mix_g, w_in, rnn_conv_w, rnn_conv_b, lru_w_a, lru_b_a, lru_w_x, lru_b_x, lru_lambda, sgu_ln_g, sgu_ln_b, sgu_w_s, sgu_b_s, w_branch_a, w_branch_b, w_out, norm_ffn_g, w_up, ffn_conv_w, ffn_conv_b, w_down, norm_final_g, loss_target, m_w_ada, m_b_ada, m_norm_mix_g, m_w_in, m_rnn_conv_w, m_rnn_conv_b, m_lru_w_a, m_lru_b_a, m_lru_w_x, m_lru_b_x, m_lru_lambda, m_sgu_ln_g, m_sgu_ln_b, m_sgu_w_s, m_sgu_b_s, m_w_branch_a, m_w_branch_b, m_w_out, m_norm_ffn_g, m_w_up, m_ffn_conv_w, m_ffn_conv_b, m_w_down, m_norm_final_g, v_w_ada, v_b_ada, v_norm_mix_g, v_w_in, v_rnn_conv_w, v_rnn_conv_b, v_lru_w_a, v_lru_b_a, v_lru_w_x, v_lru_b_x, v_lru_lambda, v_sgu_ln_g, v_sgu_ln_b, v_sgu_w_s, v_sgu_b_s, v_w_branch_a, v_w_branch_b, v_w_out, v_norm_ffn_g, v_w_up, v_ffn_conv_w, v_ffn_conv_b, v_w_down, v_norm_final_g):
    given = dict(locals())
    me = 4 * lax.axis_index("x") + 2 * lax.axis_index("y") + lax.axis_index("c")
    ada_cols = w_ada.shape[2]
    conv_cols = {"rnn_conv_w": rnn_conv_w.shape[2], "ffn_conv_w": ffn_conv_w.shape[2]}

    (win, c_all, cw_rnn, cw_ffn), _ = _all_gather(
        [w_in[0].astype(BF16), c.reshape(1, 1, D), rnn_conv_w[0], ffn_conv_w[0]], [1, 0, 1, 1], "gather_first")
    c_all = c_all.reshape(N_DEV, D)

    b_cols = lax.dynamic_slice_in_dim(b_ada, me * ada_cols, ada_cols, axis=1)
    (mod_all,), mod_done = _all_gather(
        [_mod_cols(c_all, w_ada[0], b_cols).reshape(1, N_DEV, ada_cols)], [0], "gather_mod")
    mod_all = mod_all.reshape(N_DEV, N_DEV, ada_cols)
    mod_mine = lax.dynamic_index_in_dim(mod_all, me, axis=1, keepdims=False).reshape(6, 1, D)

    late_groups = {"merge": (["w_branch_a", "w_branch_b", "w_out"], [0, 0, 0]), "ffn_up": (["w_up"], [1]),
                   "ffn_down": (["w_down"], [0])}
    in_flight, started = {}, mod_done[0:1, 0:1]
    for stage, (names, axes) in late_groups.items():
        shards = [(given[n][0] + started).astype(BF16) for n in names]
        plan = _gather_plan(axes, [s.shape[-1] for s in shards])
        send, recv, srcs, lands, token = _exchange_start(
            "gather_start_" + stage, shards, [_own_block_placed(s, ax, me) for s, ax in zip(shards, axes)], plan,
            len(shards) * (N_DEV - 1))
        in_flight[stage] = (send, recv, srcs, lands, plan)
        started = started + token[0:1, 0:1]

    def late_weights(stage, after):
        send, recv, srcs, lands, plan = in_flight[stage]
        full = _exchange_wait("gather_wait_" + stage, send, recv, srcs, lands, plan, after)
        full = [w.reshape(-1, D) if ax == 0 else w for w, ax in zip(full, late_groups[stage][1])]
        return full if len(full) > 1 else full[0]

    mod_mine = mod_mine + started

    reducing, packing = {}, {}

    def start_pack(stage, small):
        pack = _pack([small[n] for n in SMALL_GROUPS[stage]])[None]
        plan = _gather_plan([0], [LANES])
        send, recv, srcs, lands, tok = _exchange_start(
            "small_start_" + stage, [pack], [_own_block_placed(pack, 0, me)], plan, N_DEV - 1)
        packing[stage] = (send, recv, srcs, lands, plan)
        return tok

    def grads_ready(stage, grads, small):
        tokens = [start_pack(stage, small)] if small else []
        if grads:
            tokens.append(start_reduce(stage, grads))
        return sum(tokens[1:], tokens[0])

    def start_reduce(stage, grads):
        names = [n for n in BIG_NAMES if n in grads]
        blocked = {}
        for n in names:
            ax = BIG_AXES[BIG_NAMES.index(n)]
            g = grads[n] if ax == 1 else grads[n].reshape(N_DEV, grads[n].shape[0] // N_DEV, grads[n].shape[1])
            blocked.setdefault((ax, g.shape), []).append((n, g))
        sums = {}
        for (ax, _), group in blocked.items():
            reduced = _sibling_reduce([g for _, g in group], ax, "reduce_sibling_" + "_".join(n for n, _ in group))
            sums.update({n: r for (n, _), r in zip(group, reduced)})
        sums = [sums[n] for n in names]
        pays = [pay for _, pay in sums]
        send, recv, srcs, lands, tok = _exchange_start(
            "reduce_start_" + stage, pays, [lax.empty(p_.shape, p_.dtype) for p_ in pays], _chip_plan, 3 * len(pays))
        reducing[stage] = (names, [own for own, _ in sums], send, recv, srcs, lands)
        return tok

    p = {n: given[n][0] for n in REPLICATED if n not in ("b_ada", "norm_final_g")}
    p = {n: (a.reshape(1, -1) if a.ndim == 1 else a) for n, a in p.items()}
    p["rnn_conv_w"], p["ffn_conv_w"] = cw_rnn, cw_ffn
    p["norm_final_g"] = norm_final_g.reshape(1, D)
    loss, grad_x, _, small, dmod = _local_step(x[0], loss_target[0], mod_mine, win, late_weights, p, grads_ready)

    small["b_ada"] = dmod.reshape(1, 6 * D)
    rows_of = {n: _pack_rows(small[n].shape) for n in SMALL_NAMES}
    (last,), _ = _all_gather([_pack([small[n] for n in LAST_REP])[None]], [0], "gather_small")
    gathered = {"last": last}
    for stage, (send, recv, srcs, lands, plan) in packing.items():
        (gathered[stage],) = _exchange_wait("small_wait_" + stage, send, recv, srcs, lands, plan, [grad_x])
    gathered = {k: v.reshape(N_DEV, -1, LANES) for k, v in gathered.items()}

    out = {}
    for stage, (names, owns, send, recv, srcs, lands) in reducing.items():
        landed = _exchange_wait("reduce_wait_" + stage, send, recv, srcs, lands, _chip_plan, [last])
        for n, own, got in zip(names, owns, landed):
            out[n] = _adamw(given[n][0], given["m_" + n][0], given["v_" + n][0], [own, got], "adamw_" + n)

    dmod_all = gathered["last"][:, :rows_of["b_ada"]].reshape(N_DEV, 6 * D)
    dmod_cols = lax.dynamic_slice_in_dim(dmod_all, me * ada_cols, ada_cols, axis=1)
    out["w_ada"] = _adamw(w_ada[0], m_w_ada[0], v_w_ada[0], [_ada_grad(c_all, dmod_cols)], "adamw_w_ada")

    def rows_form(a):
        return a.reshape(1, -1) if a.size // a.shape[-1] == 1 or a.ndim == 1 else a.reshape(-1, LANES)

    for stage, names in (("last", LAST_REP), ("rnn", EARLY_REP), ("mixer", MID_REP)):
        out.update(_adamw_group(names, *[[rows_form(given[pre + n]) for n in names] for pre in ("", "m_", "v_")],
                                gathered[stage], "adamw_small_" + stage))

    row0 = sum(rows_of[n] for n in EARLY_REP)
    for n in COL_SHARDED:
        full = gathered["rnn"][:, row0:row0 + rows_of[n]].reshape(N_DEV, small[n].shape[0], small[n].shape[1])
        mine = lax.dynamic_slice_in_dim(full, me * conv_cols[n], conv_cols[n], axis=2)
        out[n] = _adamw(given[n][0], given["m_" + n][0], given["v_" + n][0], [mine], "adamw_" + n)
        row0 += rows_of[n]

    total = lax.psum(loss[0, 0], ("x", "y", "c"))
    results = [total, grad_x[None]]
    for kind in range(4):
        results += [out[n][kind].reshape(given[n].shape) for n in WEIGHTS]
    return tuple(results)
```

```python
import math

import jax
import jax.numpy as jnp
from jax import lax
from jax.experimental import pallas as pl
from jax.experimental.pallas import tpu as pltpu

F32 = jnp.float32
BF16 = jnp.bfloat16
MESH_IDS = pl.DeviceIdType.MESH

D = 1024
NH = 8
HD = 128
NCOL_IN = 6 * D
DFF = 3 * D
N_DEV = 8
EPS = 1e-6
LRU_C = 8.0
ADAM_LR, ADAM_B1, ADAM_B2, ADAM_EPS, ADAM_WD, ADAM_STEP = 0.001, 0.9, 0.999, 1e-08, 0.01, 10

SUBLANES = 8
LANES = 128
HALO = 16
VMEM_LIMIT = 56 * 1024 * 1024
GELU_K = math.sqrt(2.0 / math.pi)
GELU_C = 0.044715


def _cparams(n_axes):
    return pltpu.CompilerParams(dimension_semantics=("arbitrary",) * n_axes, vmem_limit_bytes=VMEM_LIMIT)


def _const_spec(shape, single_buffer=False):
    nd = len(shape)
    if single_buffer:
        return pl.BlockSpec(shape, lambda *_: (0,) * nd, pipeline_mode=pl.Buffered(1))
    return pl.BlockSpec(shape, lambda *_: (0,) * nd)


def _tile_big(t):
    return min(512, t)


def _tile_seq(t):
    return min(256, t)


def _row_tile(rows, cols):
    cap = max(SUBLANES, (2 * 1024 * 1024) // (4 * cols) // SUBLANES * SUBLANES)
    if rows <= cap:
        return rows
    return next(tr for tr in range(cap, 0, -SUBLANES) if rows % tr == 0)


def _gelu_t(x):
    x2 = x * x
    t = jnp.tanh(x * (GELU_K + (GELU_K * GELU_C) * x2))
    hx = 0.5 * x
    return hx + hx * t, (x2, hx, t)


def _gelu_grad(shared):
    x2, hx, t = shared
    return (0.5 + 0.5 * t) + (hx * (1.0 - t * t)) * (GELU_K + (3.0 * GELU_K * GELU_C) * x2)


def _sigmoid(x):
    return 1.0 / (1.0 + jnp.exp(-x))


def _log_sigmoid(x):
    return -(jnp.maximum(-x, 0.0) + jnp.log1p(jnp.exp(-jnp.abs(x))))


def _row_iota(cols):
    return lax.broadcasted_iota(jnp.int32, (SUBLANES, cols), 0)


def _shift_down(x, k, prev8):
    if k == 0:
        return x
    r = pltpu.roll(x, k, 0)
    p = pltpu.roll(prev8, k, 0)
    head = jnp.where(_row_iota(x.shape[1]) < k, p, r[:SUBLANES])
    return jnp.concatenate([head, r[SUBLANES:]], axis=0)


def _shift_up(x, k, next8):
    if k == 0:
        return x
    n = x.shape[0]
    r = pltpu.roll(x, n - k, 0)
    q = pltpu.roll(next8, SUBLANES - k, 0)
    tail = jnp.where(_row_iota(x.shape[1]) >= SUBLANES - k, q, r[n - SUBLANES:])
    return jnp.concatenate([r[:n - SUBLANES], tail], axis=0)


def _heads_nn(x_bf, w_ref):
    return jnp.concatenate(
        [jnp.dot(x_bf[:, h * HD:(h + 1) * HD], w_ref[h], preferred_element_type=F32) for h in range(NH)], axis=1)


def _heads_nt(x_bf, w_ref):
    return jnp.concatenate(
        [lax.dot_general(x_bf[:, h * HD:(h + 1) * HD], w_ref[h], (((1,), (1,)), ((), ())), preferred_element_type=F32)
         for h in range(NH)], axis=1)


def _dot_nt(a, b):
    return lax.dot_general(a, b, (((1,), (1,)), ((), ())), preferred_element_type=F32)


def _dot_tn(a, b):
    return lax.dot_general(a, b, (((0,), (0,)), ((), ())), preferred_element_type=F32)


def _colsum(x):
    return jnp.sum(x, axis=0, keepdims=True)


def _prev_halo_map(tm, col):
    return lambda i, *_: (jnp.maximum(i * (tm // HALO) - 1, 0), col)


def _norm_proj(x, g, scale, shift, w, name):
    t, n = x.shape[0], w.shape[1]
    tm = _tile_big(t)

    def body(x_ref, g_ref, sc_ref, sh_ref, w_ref, h_ref, z_ref):
        xv = x_ref[...]
        r = lax.rsqrt(jnp.mean(xv * xv, axis=-1, keepdims=True) + EPS)
        hb = ((xv * r * g_ref[...]) * (1.0 + sc_ref[...]) + sh_ref[...]).astype(BF16)
        h_ref[...] = hb
        for c0 in range(0, n, D):
            z_ref[:, c0:c0 + D] = jnp.dot(hb, w_ref[:, c0:c0 + D], preferred_element_type=F32).astype(BF16)

    vec = _const_spec((1, D))
    return pl.pallas_call(
        body, name=name, grid=(t // tm,),
        in_specs=[pl.BlockSpec((tm, D), lambda i: (i, 0)), vec, vec, vec, _const_spec((D, n), True)],
        out_specs=[pl.BlockSpec((tm, D), lambda i: (i, 0)), pl.BlockSpec((tm, n), lambda i: (i, 0))],
        out_shape=[jax.ShapeDtypeStruct((t, D), BF16), jax.ShapeDtypeStruct((t, n), BF16)],
        compiler_params=_cparams(1),
    )(x, g, scale, shift, w)


def _lru_gates(xc, wa_ref, ba, wx_ref, bx, ls):
    xb = xc.astype(BF16)
    ra = _sigmoid(_heads_nn(xb, wa_ref) + ba)
    ia = _sigmoid(_heads_nn(xb, wx_ref) + bx)
    la = LRU_C * ra * ls
    a = jnp.exp(la)
    mult = jnp.sqrt(-jnp.tanh(la) * (1.0 + a * a))
    return ra, ia, a, mult


def _conv4(xr, prev8, cw_ref, cb):
    return (cb + cw_ref[3:4, :] * xr + cw_ref[2:3, :] * _shift_down(xr, 1, prev8)
            + cw_ref[1:2, :] * _shift_down(xr, 2, prev8) + cw_ref[0:1, :] * _shift_down(xr, 3, prev8))


def _rnn_fwd(z, cw, cb, wa, ba, wx, bx, lam):
    t = z.shape[0]
    tm = _tile_seq(t)
    ngrp = tm // SUBLANES

    def body(xr_ref, xp_ref, gr_ref, cw_ref, cb_ref, wa_ref, ba_ref, wx_ref, bx_ref, lam_ref,
             h_ref, ya_ref, xc_ref, ra_ref, ia_ref, gg_ref, hg_ref, carry_ref, a_scr, u_scr):
        i = pl.program_id(0)

        @pl.when(i == 0)
        def _():
            carry_ref[...] = jnp.zeros_like(carry_ref)

        xr = xr_ref[...].astype(F32)
        prev8 = jnp.where(i == 0, 0.0, xp_ref[...].astype(F32)[HALO - SUBLANES:])
        xc = _conv4(xr, prev8, cw_ref, cb_ref[...])
        ra, ia, a, mult = _lru_gates(xc, wa_ref, ba_ref[...], wx_ref, bx_ref[...], _log_sigmoid(lam_ref[...]))
        xc_ref[...] = xc.astype(BF16)
        ra_ref[...] = ra.astype(BF16)
        ia_ref[...] = ia.astype(BF16)
        a_scr[...] = a
        u_scr[...] = mult * (ia * xc)
        row = _row_iota(D)

        def grp(j, carry):
            r0 = pl.multiple_of(j * SUBLANES, SUBLANES)
            av = a_scr[pl.ds(r0, SUBLANES), :]
            uv = u_scr[pl.ds(r0, SUBLANES), :]
            for d in (1, 2, 4):
                m = row >= d
                uv = jnp.where(m, av * pltpu.roll(uv, d, 0) + uv, uv)
                av = jnp.where(m, av * pltpu.roll(av, d, 0), av)
            hv = uv + av * carry
            h_ref[pl.ds(r0, SUBLANES), :] = hv
            return hv[SUBLANES - 1:SUBLANES, :]

        carry_ref[0:1, :] = lax.fori_loop(0, ngrp, grp, carry_ref[0:1, :])
        grv = gr_ref[...].astype(F32)
        gg, tg = _gelu_t(grv)
        hv = h_ref[...]
        ya_ref[...] = (hv * gg).astype(BF16)
        gg_ref[...] = gg.astype(BF16)
        hg_ref[...] = (hv * _gelu_grad(tg)).astype(BF16)

    vec = _const_spec((1, D))
    wspec = _const_spec((NH, HD, HD))
    tile = pl.BlockSpec((tm, D), lambda i: (i, 0))
    bshape = jax.ShapeDtypeStruct((t, D), BF16)
    return pl.pallas_call(
        body, name="rnn_fwd", grid=(t // tm,),
        in_specs=[tile, pl.BlockSpec((HALO, D), _prev_halo_map(tm, 0)),
                  pl.BlockSpec((tm, D), lambda i: (i, 1)), _const_spec((4, D)), vec, wspec, vec, wspec, vec, vec],
        out_specs=[tile] * 7,
        out_shape=[jax.ShapeDtypeStruct((t, D), F32)] + [bshape] * 6,
        scratch_shapes=[pltpu.VMEM((SUBLANES, D), F32), pltpu.VMEM((tm, D), F32), pltpu.VMEM((tm, D), F32)],
        compiler_params=_cparams(1),
    )(z, z, z, cw, cb, wa, ba, wx, bx, lam)


def _sgu_fwd(z, lng, lnb, wm, bst):
    t = z.shape[0]
    tm = _tile_seq(t)

    def body(zu_ref, zv_ref, lng_ref, lnb_ref, wm_ref, bst_ref, yb_ref, gu_ref, mg_ref, vh_ref, gpv_ref, rstd_ref):
        gu, su = _gelu_t(zu_ref[...].astype(F32))
        gv, sv = _gelu_t(zv_ref[...].astype(F32))
        mu = jnp.mean(gv, axis=-1, keepdims=True)
        cen = gv - mu
        rstd = lax.rsqrt(jnp.mean(cen * cen, axis=-1, keepdims=True) + EPS)
        vhat = cen * rstd
        vb = (vhat * lng_ref[...] + lnb_ref[...]).astype(BF16)
        rows = []
        for b0 in range(0, tm, HD):
            rows.append(jnp.concatenate(
                [jnp.dot(wm_ref[g], vb[b0:b0 + HD, g * HD:(g + 1) * HD], preferred_element_type=F32)
                 + bst_ref[:, g:g + 1] for g in range(NH)], axis=1))
        mixed = jnp.concatenate(rows, axis=0) if len(rows) > 1 else rows[0]
        yb_ref[...] = (gu * mixed).astype(BF16)
        gu_ref[...] = gu.astype(BF16)
        mg_ref[...] = (mixed * _gelu_grad(su)).astype(BF16)
        vh_ref[...] = vhat.astype(BF16)
        gpv_ref[...] = _gelu_grad(sv).astype(BF16)
        rstd_ref[...] = rstd

    vec = _const_spec((1, D))
    tile = pl.BlockSpec((tm, D), lambda i: (i, 0))
    bshape = jax.ShapeDtypeStruct((t, D), BF16)
    return pl.pallas_call(
        body, name="sgu_fwd", grid=(t // tm,),
        in_specs=[pl.BlockSpec((tm, D), lambda i: (i, 2)), pl.BlockSpec((tm, D), lambda i: (i, 3)), vec, vec,
                  _const_spec((NH, HD, HD)), _const_spec((HD, NH))],
        out_specs=[tile] * 5 + [pl.BlockSpec((tm, 1), lambda i: (i, 0))],
        out_shape=[bshape] * 5 + [jax.ShapeDtypeStruct((t, 1), F32)],
        compiler_params=_cparams(1),
    )(z, z, lng, lnb, wm, bst)


def _merge_fwd(ya_pre, yb_pre, z, x, gate1, wba, wbb, wout):
    t = x.shape[0]
    tm = _tile_big(t)

    def body(yap_ref, ybp_ref, ga_ref, gb_ref, x_ref, g1_ref, wba_ref, wbb_ref, wo_ref,
             x2_ref, ya_ref, yb_ref, mg_ref, o1_ref):
        ya = jnp.dot(yap_ref[...], wba_ref[...], preferred_element_type=F32)
        yb = jnp.dot(ybp_ref[...], wbb_ref[...], preferred_element_type=F32)
        merged = _sigmoid(ga_ref[...].astype(F32)) * ya + _sigmoid(gb_ref[...].astype(F32)) * yb
        mb = merged.astype(BF16)
        o1 = jnp.dot(mb, wo_ref[...], preferred_element_type=F32)
        x2_ref[...] = x_ref[...] + g1_ref[...] * o1
        ya_ref[...] = ya.astype(BF16)
        yb_ref[...] = yb.astype(BF16)
        mg_ref[...] = mb
        o1_ref[...] = o1.astype(BF16)

    tile = pl.BlockSpec((tm, D), lambda i: (i, 0))
    wspec = _const_spec((D, D))
    bshape = jax.ShapeDtypeStruct((t, D), BF16)
    return pl.pallas_call(
        body, name="merge_fwd", grid=(t // tm,),
        in_specs=[tile, tile, pl.BlockSpec((tm, D), lambda i: (i, 4)), pl.BlockSpec((tm, D), lambda i: (i, 5)),
                  tile, _const_spec((1, D)), wspec, wspec, wspec],
        out_specs=[tile] * 5,
        out_shape=[jax.ShapeDtypeStruct((t, D), F32), bshape, bshape, bshape, bshape],
        compiler_params=_cparams(1),
    )(ya_pre, yb_pre, z, z, x, gate1, wba, wbb, wout)


def _conv3(u, prev8, cw_ref, cb):
    return cb + cw_ref[2:3, :] * u + cw_ref[1:2, :] * _shift_down(u, 1, prev8) + cw_ref[0:1, :] * _shift_down(u, 2, prev8)


def _ffn_proj_mid(x2, g, scale, shift, w, cw, cb):
    t = x2.shape[0]
    tm = _tile_big(t)
    nc = DFF // D

    def body(x_ref, g_ref, sc_ref, sh_ref, wa_ref, wv_ref, cwa_ref, cwv_ref, cba_ref, cbv_ref,
             h_ref, upa_ref, upv_ref, ff_ref, fa_ref, fv_ref, hb_scr, prev_ref):
        i, c = pl.program_id(0), pl.program_id(1)

        @pl.when(i == 0)
        def _():
            prev_ref[c] = jnp.zeros((2, SUBLANES, D), F32)

        @pl.when(c == 0)
        def _():
            xv = x_ref[...]
            r = lax.rsqrt(jnp.mean(xv * xv, axis=-1, keepdims=True) + EPS)
            hb_scr[...] = ((xv * r * g_ref[...]) * (1.0 + sc_ref[...]) + sh_ref[...]).astype(BF16)
            h_ref[...] = hb_scr[...]

        hb = hb_scr[...]
        halves = []
        for s, (w_ref, up_ref, cw_ref, cb_ref) in enumerate(((wa_ref, upa_ref, cwa_ref, cba_ref),
                                                             (wv_ref, upv_ref, cwv_ref, cbv_ref))):
            u = jnp.dot(hb, w_ref[...], preferred_element_type=F32)
            up_ref[...] = u.astype(BF16)
            halves.append(_conv3(u, prev_ref[c, s], cw_ref, cb_ref[...]))
            prev_ref[c, s] = u[tm - SUBLANES:]
        act, val = halves
        ga, ta = _gelu_t(act)
        ff_ref[...] = (ga * val).astype(BF16)
        fa_ref[...] = (val * _gelu_grad(ta)).astype(BF16)
        fv_ref[...] = ga.astype(BF16)

    def cols(rows, off):
        return pl.BlockSpec((rows, D), lambda i, c: (0, off + c))

    vec = pl.BlockSpec((1, D), lambda i, c: (0, 0))
    row_tile = pl.BlockSpec((tm, D), lambda i, c: (i, 0))
    chunk = pl.BlockSpec((tm, D), lambda i, c: (i, c))
    hshape = jax.ShapeDtypeStruct((t, DFF), BF16)
    return pl.pallas_call(
        body, name="ffn_proj_mid", grid=(t // tm, nc),
        in_specs=[row_tile, vec, vec, vec, cols(D, 0), cols(D, nc), cols(3, 0), cols(3, nc), cols(1, 0), cols(1, nc)],
        out_specs=[row_tile, chunk, chunk, chunk, chunk, chunk],
        out_shape=[jax.ShapeDtypeStruct((t, D), BF16), hshape, hshape, hshape, hshape, hshape],
        scratch_shapes=[pltpu.VMEM((tm, D), BF16), pltpu.VMEM((nc, 2, SUBLANES, D), F32)],
        compiler_params=_cparams(2),
    )(x2, g, scale, shift, w, w, cw, cw, cb, cb)


def _ffn_out_loss(ff, wd, x2, target, gate2, gfin):
    t = x2.shape[0]
    tm = _tile_big(t)

    def body(ff_ref, wd_ref, x2_ref, tg_ref, g2_ref, gf_ref, dx3_ref, loss_ref, dgf_ref, dg2_ref):
        @pl.when(pl.program_id(0) == 0)
        def _():
            loss_ref[...] = jnp.zeros_like(loss_ref)
            dgf_ref[...] = jnp.zeros_like(dgf_ref)
            dg2_ref[...] = jnp.zeros_like(dg2_ref)

        o2 = jnp.dot(ff_ref[...], wd_ref[...], preferred_element_type=F32)
        x3 = x2_ref[...] + g2_ref[...] * o2
        r = lax.rsqrt(jnp.mean(x3 * x3, axis=-1, keepdims=True) + EPS)
        xhat = x3 * r
        err = xhat * gf_ref[...] - tg_ref[...]
        loss_ref[...] += 0.5 * jnp.sum(jnp.mean(err * err, axis=-1, keepdims=True), axis=0, keepdims=True)
        dy = err * (1.0 / D)
        dgf_ref[...] += _colsum(dy * xhat)
        dxh = dy * gf_ref[...]
        dx3 = r * (dxh - xhat * jnp.mean(dxh * xhat, axis=-1, keepdims=True))
        dx3_ref[...] = dx3
        dg2_ref[...] += _colsum(dx3 * o2)

    tile = pl.BlockSpec((tm, D), lambda i: (i, 0))
    vec = _const_spec((1, D))
    return pl.pallas_call(
        body, name="ffn_out_loss", grid=(t // tm,),
        in_specs=[pl.BlockSpec((tm, DFF), lambda i: (i, 0)), _const_spec((DFF, D), True), tile, tile, vec, vec],
        out_specs=[tile, _const_spec((1, 1)), vec, vec],
        out_shape=[jax.ShapeDtypeStruct((t, D), F32), jax.ShapeDtypeStruct((1, 1), F32),
                   jax.ShapeDtypeStruct((1, D), F32), jax.ShapeDtypeStruct((1, D), F32)],
        compiler_params=_cparams(1),
    )(ff, wd, x2, target, gate2, gfin)


def _ffn_down_bwd(dx3, gate2, ff, fa, fv, wd):
    t = dx3.shape[0]
    tm = min(1024, t)
    nc = DFF // D

    def body(dx3_ref, g2_ref, ff_ref, fa_ref, fv_ref, wd_ref, da_ref, dv_ref, dwd_ref, dcba_ref, dcbv_ref):
        @pl.when(pl.program_id(1) == 0)
        def _():
            for r in (dwd_ref, dcba_ref, dcbv_ref):
                r[...] = jnp.zeros_like(r)

        do2 = (dx3_ref[...] * g2_ref[...]).astype(BF16)
        dwd_ref[...] += _dot_tn(ff_ref[...], do2)
        dff = _dot_nt(do2, wd_ref[...])
        dact = dff * fa_ref[...].astype(F32)
        dval = dff * fv_ref[...].astype(F32)
        da_ref[...] = dact.astype(BF16)
        dv_ref[...] = dval.astype(BF16)
        dcba_ref[...] += _colsum(dact)
        dcbv_ref[...] += _colsum(dval)

    blk = pl.BlockSpec((tm, D), lambda c, i: (i, c))
    vec = pl.BlockSpec((1, D), lambda c, i: (0, c))
    return pl.pallas_call(
        body, name="ffn_down_bwd", grid=(nc, t // tm),
        in_specs=[pl.BlockSpec((tm, D), lambda c, i: (i, 0)), pl.BlockSpec((1, D), lambda c, i: (0, 0)),
                  blk, blk, blk, pl.BlockSpec((D, D), lambda c, i: (c, 0))],
        out_specs=[blk, blk, pl.BlockSpec((D, D), lambda c, i: (c, 0)), vec, vec],
        out_shape=[jax.ShapeDtypeStruct((t, DFF), BF16), jax.ShapeDtypeStruct((t, DFF), BF16),
                   jax.ShapeDtypeStruct((DFF, D), F32),
                   jax.ShapeDtypeStruct((1, DFF), F32), jax.ShapeDtypeStruct((1, DFF), F32)],
        compiler_params=_cparams(2),
    )(dx3, gate2, ff, fa, fv, wd)


def _modnorm_bwd(dh, xv, g, scale):
    r = lax.rsqrt(jnp.mean(xv * xv, axis=-1, keepdims=True) + EPS)
    xhat = xv * r
    dxn = dh * (1.0 + scale)
    dxh = dxn * g
    dx = r * (dxh - xhat * jnp.mean(dxh * xhat, axis=-1, keepdims=True))
    return dx, _colsum(dh), _colsum(dh * (xhat * g)), _colsum(dxn * xhat)


def _ffn_up_bwd(dact, dval, up_a, up_v, cw, wup, x2, dx3, gffn, scale2, o1, gate1):
    t = x2.shape[0]
    tm = _tile_seq(t)
    nt = t // tm
    nc = DFF // D

    def body(da_ref, dan_ref, dv_ref, dvn_ref, ua_ref, uv_ref, cw_ref, w_ref, x2_ref, dx3_ref, g_ref, sc_ref, o1_ref, g1_ref,
             dup_ref, dx2_ref, do1_ref, dcw_ref, dsh_ref, dsc_ref, dg_ref, dg1_ref):
        i = pl.program_id(0)

        @pl.when(i == 0)
        def _():
            for r in (dcw_ref, dsh_ref, dsc_ref, dg_ref, dg1_ref):
                r[...] = jnp.zeros_like(r)

        last = i == nt - 1
        dh = jnp.zeros((tm, D), F32)
        for half, (d_ref, dn_ref, u_ref) in enumerate(((da_ref, dan_ref, ua_ref), (dv_ref, dvn_ref, uv_ref))):
            nxt = jnp.where(last, 0.0, dn_ref[...].astype(F32)[:SUBLANES])
            for c in range(nc):
                c0 = half * DFF + c * D
                dv = d_ref[:, c * D:(c + 1) * D].astype(F32)
                nx = nxt[:, c * D:(c + 1) * D]
                taps = (_shift_up(dv, 2, nx), _shift_up(dv, 1, nx), dv)
                dup = (cw_ref[2:3, c0:c0 + D] * taps[2] + cw_ref[1:2, c0:c0 + D] * taps[1]
                       + cw_ref[0:1, c0:c0 + D] * taps[0]).astype(BF16)
                upv = u_ref[:, c * D:(c + 1) * D].astype(F32)
                for k in range(3):
                    dcw_ref[k:k + 1, c0:c0 + D] += _colsum(taps[k] * upv)
                dup_ref[:, c0:c0 + D] = dup
                dh = dh + _dot_nt(dup, w_ref[:, c0:c0 + D])
        dxn, dsh, dsc, dg = _modnorm_bwd(dh, x2_ref[...], g_ref[...], sc_ref[...])
        dx2 = dx3_ref[...] + dxn
        dx2_ref[...] = dx2
        do1_ref[...] = (dx2 * g1_ref[...]).astype(BF16)
        dsh_ref[...] += dsh
        dsc_ref[...] += dsc
        dg_ref[...] += dg
        dg1_ref[...] += _colsum(dx2 * o1_ref[...].astype(F32))

    tile = pl.BlockSpec((tm, D), lambda i: (i, 0))
    wide = pl.BlockSpec((tm, DFF), lambda i: (i, 0))
    nxt = pl.BlockSpec((HALO, DFF), lambda i: (jnp.minimum((i + 1) * (tm // HALO), t // HALO - 1), 0))
    vec = _const_spec((1, D))
    vshape = jax.ShapeDtypeStruct((1, D), F32)
    return pl.pallas_call(
        body, name="ffn_up_bwd", grid=(nt,),
        in_specs=[wide, nxt, wide, nxt, wide, wide,
                  _const_spec((3, 2 * DFF)), _const_spec((D, 2 * DFF), True),
                  tile, tile, vec, vec, tile, vec],
        out_specs=[pl.BlockSpec((tm, 2 * DFF), lambda i: (i, 0)), tile, tile, _const_spec((3, 2 * DFF)),
                   vec, vec, vec, vec],
        out_shape=[jax.ShapeDtypeStruct((t, 2 * DFF), BF16), jax.ShapeDtypeStruct((t, D), F32),
                   jax.ShapeDtypeStruct((t, D), BF16), jax.ShapeDtypeStruct((3, 2 * DFF), F32),
                   vshape, vshape, vshape, vshape],
        compiler_params=_cparams(1),
    )(dact, dact, dval, dval, up_a, up_v, cw, wup, x2, dx3, gffn, scale2, o1, gate1)


def _xt_y(a, b, name):
    t, k = a.shape
    n = b.shape[1]
    tm = min(1024, t)
    bn = 1536 if n % 1536 == 0 else D

    def body(a_ref, b_ref, o_ref):
        @pl.when(pl.program_id(1) == 0)
        def _():
            o_ref[...] = jnp.zeros_like(o_ref)

        o_ref[...] += _dot_tn(a_ref[...], b_ref[...])

    return pl.pallas_call(
        body, name=name, grid=(n // bn, t // tm),
        in_specs=[pl.BlockSpec((tm, k), lambda j, i: (i, 0)), pl.BlockSpec((tm, bn), lambda j, i: (i, j))],
        out_specs=pl.BlockSpec((k, bn), lambda j, i: (0, j)),
        out_shape=jax.ShapeDtypeStruct((k, n), F32),
        compiler_params=_cparams(2),
    )(a, b)


def _acc_spec(shape, index):
    return pl.BlockSpec(shape, lambda *_: index, pipeline_mode=pl.Buffered(1))


def _out_bwd(do1, wout, merged, ya, yb, z, h1):
    t = do1.shape[0]
    tm = _tile_big(t)

    def body(do1_ref, wo_ref, mg_ref, ya_ref, yb_ref, ga_ref, gb_ref, h1_ref,
             dya_ref, dyb_ref, dz_ref, dwo_ref, dwin_ref):
        @pl.when(pl.program_id(0) == 0)
        def _():
            dwo_ref[...] = jnp.zeros_like(dwo_ref)
            dwin_ref[...] = jnp.zeros_like(dwin_ref)

        do1v = do1_ref[...]
        dwo_ref[...] += _dot_tn(mg_ref[...], do1v)
        dm = _dot_nt(do1v, wo_ref[...])
        sa = _sigmoid(ga_ref[...].astype(F32))
        sb = _sigmoid(gb_ref[...].astype(F32))
        dya_ref[...] = (dm * sa).astype(BF16)
        dyb_ref[...] = (dm * sb).astype(BF16)
        dga = (dm * ya_ref[...].astype(F32) * sa * (1.0 - sa)).astype(BF16)
        dgb = (dm * yb_ref[...].astype(F32) * sb * (1.0 - sb)).astype(BF16)
        dz_ref[:, 0:D] = dga
        dz_ref[:, D:2 * D] = dgb
        h1v = h1_ref[...]
        dwin_ref[:, 0:D] += _dot_tn(h1v, dga)
        dwin_ref[:, D:2 * D] += _dot_tn(h1v, dgb)

    tile = pl.BlockSpec((tm, D), lambda i: (i, 0))
    bshape = jax.ShapeDtypeStruct((t, D), BF16)
    return pl.pallas_call(
        body, name="out_bwd", grid=(t // tm,),
        in_specs=[tile, _const_spec((D, D), True), tile, tile, tile,
                  pl.BlockSpec((tm, D), lambda i: (i, 4)), pl.BlockSpec((tm, D), lambda i: (i, 5)), tile],
        out_specs=[tile, tile, pl.BlockSpec((tm, 2 * D), lambda i: (i, 2)), _acc_spec((D, D), (0, 0)),
                   _acc_spec((D, 2 * D), (0, 2))],
        out_shape=[bshape, bshape, jax.ShapeDtypeStruct((t, NCOL_IN), BF16), jax.ShapeDtypeStruct((D, D), F32),
                   jax.ShapeDtypeStruct((D, NCOL_IN), F32)],
        compiler_params=_cparams(1),
    )(do1, wout, merged, ya, yb, z, z, h1)


def _rnn_bwd(dya, ya_pre, wba, h1, z, saved, h, dz, dwin, cw, wa, wx, lam):
    t = z.shape[0]
    tm = _tile_seq(t)
    nt = t // tm
    ngrp = tm // SUBLANES
    hpt = tm // HALO

    def body(dya_ref, yap_ref, wba_ref, h1_ref, xr_ref, xc_ref, ra_ref, ia_ref, gg_ref, hg_ref, h_ref, hp_ref,
             dz_any, dwin_any, cw_ref, wa_ref, wx_ref, lam_ref,
             dz_ref, dwin_ref, dwba_ref, dcw_ref, dcb_ref, dwa_ref, dba_ref, dwx_ref, dbx_ref, dlam_ref,
             a_first, g_first, dxc_first, b_scr, d_scr, g_scr):
        del dz_any, dwin_any
        i = pl.program_id(0)

        @pl.when(i == 0)
        def _():
            for r in (dwin_ref, dwba_ref, dcw_ref, dcb_ref, dwa_ref, dba_ref, dwx_ref, dbx_ref, dlam_ref,
                      a_first, g_first, dxc_first):
                r[...] = jnp.zeros_like(r)

        dya_v = dya_ref[...]
        dwba_ref[...] += _dot_tn(yap_ref[...], dya_v)
        dyap_v = _dot_nt(dya_v, wba_ref[...])
        h1v = h1_ref[...]

        first_tile = i == nt - 1
        xc = xc_ref[...].astype(F32)
        ra = ra_ref[...].astype(F32)
        ia = ia_ref[...].astype(F32)
        lam_v = lam_ref[...]
        ls = _log_sigmoid(lam_v)
        la = LRU_C * ra * ls
        a = jnp.exp(la)
        mult = jnp.sqrt(-jnp.tanh(la) * (1.0 + a * a))
        hprev8 = jnp.where(first_tile, 0.0, hp_ref[...][HALO - SUBLANES:])
        h_prev = _shift_down(h_ref[...], 1, hprev8)
        dgr = (dyap_v * hg_ref[...].astype(F32)).astype(BF16)
        dz_ref[:, D:2 * D] = dgr
        dwin_ref[:, D:2 * D] += _dot_tn(h1v, dgr)

        b_scr[...] = _shift_up(a, 1, a_first[...])
        d_scr[...] = dyap_v * gg_ref[...].astype(F32)
        row = _row_iota(D)

        def grp(jj, carry):
            r0 = pl.multiple_of((ngrp - 1 - jj) * SUBLANES, SUBLANES)
            bv = b_scr[pl.ds(r0, SUBLANES), :]
            dv = d_scr[pl.ds(r0, SUBLANES), :]
            for d in (1, 2, 4):
                m = row < SUBLANES - d
                dv = jnp.where(m, dv + bv * pltpu.roll(dv, SUBLANES - d, 0), dv)
                bv = jnp.where(m, bv * pltpu.roll(bv, SUBLANES - d, 0), bv)
            gv = dv + bv * carry
            g_scr[pl.ds(r0, SUBLANES), :] = gv
            return gv[0:1, :]

        lax.fori_loop(0, ngrp, grp, g_first[0:1, :])
        g = g_scr[...]
        a_first[...] = a[:SUBLANES]
        g_first[...] = g[:SUBLANES]

        da = g * h_prev
        gx = g * xc
        dmult = gx * ia
        dia = gx * mult
        dxc = g * (mult * ia)
        dla = da * a - dmult * (a * a) / mult
        dra = dla * (LRU_C * ls)
        dlam_ref[...] += _colsum(dla * ra) * (LRU_C * _sigmoid(-lam_v))
        dpa = dra * ra * (1.0 - ra)
        dpx = dia * ia * (1.0 - ia)
        dba_ref[...] += _colsum(dpa)
        dbx_ref[...] += _colsum(dpx)
        dpab = dpa.astype(BF16)
        dpxb = dpx.astype(BF16)
        xcb = xc_ref[...]
        for hd in range(NH):
            sl = slice(hd * HD, (hd + 1) * HD)
            dwa_ref[hd] += _dot_tn(xcb[:, sl], dpab[:, sl])
            dwx_ref[hd] += _dot_tn(xcb[:, sl], dpxb[:, sl])
        dxc = dxc + _heads_nt(dpab, wa_ref) + _heads_nt(dpxb, wx_ref)

        nxt = dxc_first[...]
        taps = (_shift_up(dxc, 3, nxt), _shift_up(dxc, 2, nxt), _shift_up(dxc, 1, nxt), dxc)
        dxr = cw_ref[0:1, :] * taps[0]
        for k in range(1, 4):
            dxr = dxr + cw_ref[k:k + 1, :] * taps[k]
        dxrb = dxr.astype(BF16)
        dz_ref[:, 0:D] = dxrb
        dwin_ref[:, 0:D] += _dot_tn(h1v, dxrb)
        dxc_first[...] = dxc[:SUBLANES]
        dcb_ref[...] += _colsum(dxc)
        xr = xr_ref[...].astype(F32)
        for k in range(4):
            dcw_ref[k:k + 1, :] += _colsum(taps[k] * xr)

    def rev(col):
        return lambda i: (nt - 1 - i, col)

    vec = _const_spec((1, D))
    wspec = _const_spec((NH, HD, HD))
    vshape = jax.ShapeDtypeStruct((1, D), F32)
    wshape = jax.ShapeDtypeStruct((NH, HD, HD), F32)
    any_spec = pl.BlockSpec(memory_space=pl.ANY)
    tile = pl.BlockSpec((tm, D), rev(0))
    outs = pl.pallas_call(
        body, name="rnn_bwd", grid=(nt,),
        in_specs=[tile, tile, _const_spec((D, D), True), tile, tile, tile, tile, tile, tile, tile, tile,
                  pl.BlockSpec((HALO, D), lambda i: (jnp.maximum((nt - 1 - i) * hpt - 1, 0), 0)),
                  any_spec, any_spec, _const_spec((4, D)), wspec, wspec, vec],
        out_specs=[pl.BlockSpec((tm, 2 * D), rev(0)), _acc_spec((D, 2 * D), (0, 0)), _acc_spec((D, D), (0, 0)),
                   _const_spec((4, D)), vec, wspec, vec, wspec, vec, vec],
        out_shape=[jax.ShapeDtypeStruct((t, NCOL_IN), BF16), jax.ShapeDtypeStruct((D, NCOL_IN), F32),
                   jax.ShapeDtypeStruct((D, D), F32), jax.ShapeDtypeStruct((4, D), F32), vshape,
                   wshape, vshape, wshape, vshape, vshape],
        scratch_shapes=[pltpu.VMEM((SUBLANES, D), F32), pltpu.VMEM((SUBLANES, D), F32), pltpu.VMEM((SUBLANES, D), F32),
                        pltpu.VMEM((tm, D), F32), pltpu.VMEM((tm, D), F32), pltpu.VMEM((tm, D), F32)],
        input_output_aliases={12: 0, 13: 1},
        compiler_params=_cparams(1),
    )(dya, ya_pre, wba, h1, z, *saved, h, h, dz, dwin, cw, wa, wx, lam)
    return outs


def _sgu_bwd(dyb, yb_pre, wbb, h1, saved, dz, dwin, lng, lnb, wmt, mask):
    t = dyb.shape[0]
    tm = _tile_big(t)

    def body(dyb_ref, ybp_ref, wbb_ref, h1_ref, gu_ref, mg_ref, vh_ref, gpv_ref, rstd_ref, dz_any, dwin_any,
             lng_ref, lnb_ref, wmt_ref, mask_ref,
             dz_ref, dwin_ref, dwbb_ref, dws_ref, dbst_ref, dlng_ref, dlnb_ref):
        del dz_any, dwin_any

        @pl.when(pl.program_id(0) == 0)
        def _():
            for r in (dwin_ref, dwbb_ref, dws_ref, dbst_ref, dlng_ref, dlnb_ref):
                r[...] = jnp.zeros_like(r)

        lng_v = lng_ref[...]
        vhat = vh_ref[...].astype(F32)
        vb = (vhat * lng_v + lnb_ref[...]).astype(BF16)
        rstd = rstd_ref[...]
        dyb_v = dyb_ref[...]
        dwbb_ref[...] += _dot_tn(ybp_ref[...], dyb_v)
        dyb = _dot_nt(dyb_v, wbb_ref[...])
        h1v = h1_ref[...]
        dzu = (dyb * mg_ref[...].astype(F32)).astype(BF16)
        dz_ref[:, 0:D] = dzu
        dwin_ref[:, 0:D] += _dot_tn(h1v, dzu)
        dmix = dyb * gu_ref[...].astype(F32)
        dmb = dmix.astype(BF16)
        rows = []
        lane = lax.broadcasted_iota(jnp.int32, (HD, NH), 1)
        dbst = jnp.zeros((HD, NH), F32)
        for b0 in range(0, tm, HD):
            cols = []
            for g in range(NH):
                sl = slice(g * HD, (g + 1) * HD)
                dmg = dmb[b0:b0 + HD, sl]
                dws_ref[g] += _dot_nt(dmg, vb[b0:b0 + HD, sl]) * mask_ref[...]
                cols.append(jnp.dot(wmt_ref[g], dmg, preferred_element_type=F32))
                dbst = dbst + jnp.where(lane == g, jnp.sum(dmix[b0:b0 + HD, sl], axis=1, keepdims=True), 0.0)
            rows.append(jnp.concatenate(cols, axis=1))
        dbst_ref[...] += dbst
        dvln = jnp.concatenate(rows, axis=0) if len(rows) > 1 else rows[0]
        dlng_ref[...] += _colsum(dvln * vhat)
        dlnb_ref[...] += _colsum(dvln)
        dvh = dvln * lng_v
        dgv = rstd * (dvh - jnp.mean(dvh, axis=-1, keepdims=True)
                      - vhat * jnp.mean(dvh * vhat, axis=-1, keepdims=True))
        dzv = (dgv * gpv_ref[...].astype(F32)).astype(BF16)
        dz_ref[:, D:2 * D] = dzv
        dwin_ref[:, D:2 * D] += _dot_tn(h1v, dzv)

    vec = _const_spec((1, D))
    wspec = _const_spec((NH, HD, HD))
    vshape = jax.ShapeDtypeStruct((1, D), F32)
    tile = pl.BlockSpec((tm, D), lambda i: (i, 0))
    any_spec = pl.BlockSpec(memory_space=pl.ANY)
    return pl.pallas_call(
        body, name="sgu_bwd", grid=(t // tm,),
        in_specs=[tile, tile, _const_spec((D, D), True), tile, tile, tile, tile, tile,
                  pl.BlockSpec((tm, 1), lambda i: (i, 0)), any_spec, any_spec,
                  vec, vec, wspec, _const_spec((HD, HD))],
        out_specs=[pl.BlockSpec((tm, 2 * D), lambda i: (i, 1)), _acc_spec((D, 2 * D), (0, 1)), _acc_spec((D, D), (0, 0)),
                   wspec, _const_spec((HD, NH)), vec, vec],
        out_shape=[jax.ShapeDtypeStruct((t, NCOL_IN), BF16), jax.ShapeDtypeStruct((D, NCOL_IN), F32),
                   jax.ShapeDtypeStruct((D, D), F32), jax.ShapeDtypeStruct((NH, HD, HD), F32),
                   jax.ShapeDtypeStruct((HD, NH), F32), vshape, vshape],
        input_output_aliases={9: 0, 10: 1},
        compiler_params=_cparams(1),
    )(dyb, yb_pre, wbb, h1, *saved, dz, dwin, lng, lnb, wmt, mask)


def _in_bwd(dz, win, x, dx2, g, scale1):
    t = x.shape[0]
    tm = _tile_big(t)

    def body(dz_ref, w_ref, x_ref, dx2_ref, g_ref, sc_ref, dx_ref, dsh_ref, dsc_ref, dg_ref):
        @pl.when(pl.program_id(0) == 0)
        def _():
            for r in (dsh_ref, dsc_ref, dg_ref):
                r[...] = jnp.zeros_like(r)

        dh = jnp.zeros((tm, D), F32)
        for c0 in range(0, NCOL_IN, D):
            dh = dh + _dot_nt(dz_ref[:, c0:c0 + D], w_ref[:, c0:c0 + D])
        dxn, dsh, dsc, dg = _modnorm_bwd(dh, x_ref[...], g_ref[...], sc_ref[...])
        dx_ref[...] = dx2_ref[...] + dxn
        dsh_ref[...] += dsh
        dsc_ref[...] += dsc
        dg_ref[...] += dg

    tile = pl.BlockSpec((tm, D), lambda i: (i, 0))
    vec = _const_spec((1, D))
    vshape = jax.ShapeDtypeStruct((1, D), F32)
    return pl.pallas_call(
        body, name="in_bwd", grid=(t // tm,),
        in_specs=[pl.BlockSpec((tm, NCOL_IN), lambda i: (i, 0)), _const_spec((D, NCOL_IN), True), tile, tile, vec, vec],
        out_specs=[tile, vec, vec, vec],
        out_shape=[jax.ShapeDtypeStruct((t, D), F32), vshape, vshape, vshape],
        compiler_params=_cparams(1),
    )(dz, win, x, dx2, g, scale1)


def _mod_cols(c_all, w_ada, b_cols):
    nb, cols = c_all.shape[0], w_ada.shape[1]

    def body(c_ref, w_ref, b_ref, o_ref):
        cv = c_ref[...]
        ca = (cv * _sigmoid(cv)).astype(BF16)
        o_ref[...] = jnp.dot(ca, w_ref[...].astype(BF16), preferred_element_type=F32) + b_ref[...]

    return pl.pallas_call(body, name="mod_cols", out_shape=jax.ShapeDtypeStruct((nb, cols), F32))(c_all, w_ada, b_cols)


def _ada_grad(c_all, dmod_cols):
    cols = dmod_cols.shape[1]

    def body(c_ref, d_ref, o_ref):
        cv = c_ref[...]
        ca = (cv * _sigmoid(cv)).astype(BF16)
        o_ref[...] = _dot_tn(ca, d_ref[...].astype(BF16))

    return pl.pallas_call(body, name="ada_grad", out_shape=jax.ShapeDtypeStruct((D, cols), F32))(c_all, dmod_cols)


def _adamw_update(w, m, v, g):
    bc1 = 1.0 - ADAM_B1 ** ADAM_STEP
    bc2 = 1.0 - ADAM_B2 ** ADAM_STEP
    mn = ADAM_B1 * m + (1.0 - ADAM_B1) * g
    vn = ADAM_B2 * v + (1.0 - ADAM_B2) * (g * g)
    return -ADAM_LR * ((mn / bc1) / (jnp.sqrt(vn / bc2) + ADAM_EPS) + ADAM_WD * w), mn, vn


def _adamw_group(names, ws, ms, vs, packs, name):
    n = len(names)
    starts, r0 = [], 0
    for w in ws:
        starts.append(r0)
        r0 += _pack_rows(w.shape)

    def body(*refs):
        w_refs, m_refs, v_refs, p_ref = refs[:n], refs[n:2 * n], refs[2 * n:3 * n], refs[3 * n]
        outs = refs[3 * n + 1:]
        for k in range(n):
            rows = _pack_rows(ws[k].shape)
            g = None
            for dev in range(N_DEV):
                if ws[k].shape[0] == 1:
                    term = jnp.concatenate(
                        [p_ref[dev, starts[k] + r:starts[k] + r + 1, :] for r in range(rows)], axis=1)
                else:
                    term = p_ref[dev, starts[k]:starts[k] + rows, :]
                g = term if g is None else g + term
            delta, mn, vn = _adamw_update(w_refs[k][...], m_refs[k][...], v_refs[k][...], g)
            for o_ref, val in zip(outs[4 * k:4 * k + 4], (g, delta, mn, vn)):
                o_ref[...] = val

    shapes = [jax.ShapeDtypeStruct(w.shape, F32) for w in ws for _ in range(4)]
    outs = pl.pallas_call(body, name=name, out_shape=shapes,
                          compiler_params=pltpu.CompilerParams(vmem_limit_bytes=VMEM_LIMIT))(*ws, *ms, *vs, packs)
    return {nm: tuple(outs[4 * k:4 * k + 4]) for k, nm in enumerate(names)}


def _adamw(w, m, v, parts, name):
    rows, cols = w.shape
    tr = _row_tile(rows, cols)
    stacked = [p.ndim == 3 for p in parts]

    def body(*refs):
        w_ref, m_ref, v_ref = refs[:3]
        p_refs = refs[3:3 + len(parts)]
        g_ref, d_ref, mo_ref, vo_ref = refs[3 + len(parts):]
        g = None
        for p_ref, st in zip(p_refs, stacked):
            terms = [p_ref[k].astype(F32) for k in range(p_ref.shape[0])] if st else [p_ref[...].astype(F32)]
            for term in terms:
                g = term if g is None else g + term
        delta, mn, vn = _adamw_update(w_ref[...], m_ref[...], v_ref[...], g)
        g_ref[...] = g
        mo_ref[...] = mn
        vo_ref[...] = vn
        d_ref[...] = delta

    tile = pl.BlockSpec((tr, cols), lambda i: (i, 0))
    p_specs = [pl.BlockSpec((p.shape[0], tr, cols), lambda i: (0, i, 0)) if st else tile for p, st in zip(parts, stacked)]
    shp = jax.ShapeDtypeStruct((rows, cols), F32)
    return pl.pallas_call(
        body, name=name, grid=(rows // tr,),
        in_specs=[tile, tile, tile] + p_specs, out_specs=[tile] * 4, out_shape=[shp] * 4,
        compiler_params=_cparams(1),
    )(w, m, v, *parts)


def _mesh_pos():
    return lax.axis_index("x"), lax.axis_index("y"), lax.axis_index("c")


def _other_chips(x, y):
    return [(1 - x, y), (x, 1 - y), (1 - x, 1 - y)]


def _block_of(ref, axis, index, size):
    if axis == 0:
        return ref.at[index]
    return ref.at[:, pl.ds(pl.multiple_of(index * size, 128), size)]


def _all_gather(shards, axes, name):
    n = len(shards)
    per = 7

    def body(*refs):
        ins, outs, done = refs[:n], refs[n:2 * n], refs[2 * n]
        send_sems, recv_sems, local_sems = refs[2 * n + 1:]
        x, y, c = _mesh_pos()
        me, sibling = (x, y, c), (x, y, 1 - c)
        chips = _other_chips(x, y)

        def rows(a, pos):
            return _block_of(outs[a], axes[a], 4 * pos[0] + 2 * pos[1] + pos[2], shards[a].shape[-1])

        def copy(a, k, block, to, src=None):
            return pltpu.make_async_remote_copy(
                src_ref=rows(a, block) if src is None else src, dst_ref=rows(a, block),
                send_sem=send_sems.at[a * per + k], recv_sem=recv_sems.at[a * per + k],
                device_id=to, device_id_type=MESH_IDS)

        mine = [pltpu.make_async_copy(ins[a], rows(a, me), local_sems.at[a]) for a in range(n)]
        for cp in mine:
            cp.start()
        first = []
        for a in range(n):
            first.append(copy(a, 0, me, sibling, src=ins[a]))
            first += [copy(a, 1 + j, me, (*chip, c), src=ins[a]) for j, chip in enumerate(chips)]
        for cp in first:
            cp.start()
        passed = []
        for j, chip in enumerate(chips):
            for a in range(n):
                copy(a, 1 + j, (*chip, c), me).wait_recv()
                fwd = copy(a, 4 + j, (*chip, c), sibling)
                fwd.start()
                passed.append(fwd)
        for a in range(n):
            copy(a, 0, sibling, me).wait_recv()
            for j, chip in enumerate(chips):
                copy(a, 4 + j, (*chip, 1 - c), me).wait_recv()
        for cp in first + passed:
            cp.wait_send()
        for cp in mine:
            cp.wait()
        done[...] = jnp.zeros_like(done)

    def full_shape(s, ax):
        return (N_DEV,) + s.shape if ax == 0 else s.shape[:-1] + (N_DEV * s.shape[-1],)

    any_spec = pl.BlockSpec(memory_space=pl.ANY)
    outs = pl.pallas_call(
        body, name=name,
        in_specs=[any_spec] * n, out_specs=[any_spec] * n + [pl.BlockSpec(memory_space=pltpu.VMEM)],
        out_shape=[jax.ShapeDtypeStruct(full_shape(s, ax), s.dtype) for s, ax in zip(shards, axes)]
        + [jax.ShapeDtypeStruct((SUBLANES, LANES), F32)],
        scratch_shapes=[pltpu.SemaphoreType.DMA((n * per,)), pltpu.SemaphoreType.DMA((n * per,)),
                        pltpu.SemaphoreType.DMA((n,))],
    )(*shards)
    return outs[:n], outs[n]


def _chip_blocks(x, y):
    return [(x, y)] + _other_chips(x, y)


def _sibling_reduce(gs, axis, name):
    g0, n = gs[0], len(gs)
    rows, cols = (g0.shape[1], g0.shape[2]) if axis == 0 else (g0.shape[0], g0.shape[1] // N_DEV)
    chunk = math.gcd(rows, 64)

    def body(*refs):
        g_refs, own_refs, pay_refs = refs[:n], refs[n:2 * n], refs[2 * n:3 * n]
        send_buf, keep_buf, recv_buf, send_sems, recv_sems, stage_sems, keep_sems = refs[3 * n:]
        x, y, c = _mesh_pos()
        sibling = (x, y, 1 - c)
        chips = _chip_blocks(x, y)
        stage, keep, push = [], [], []
        for a in range(n):
            for j, (px, py) in enumerate(chips):
                s = 4 * a + j
                theirs = _block_of(g_refs[a], axis, 4 * px + 2 * py + (1 - c), cols)
                ours = _block_of(g_refs[a], axis, 4 * px + 2 * py + c, cols)
                stage.append(pltpu.make_async_copy(theirs, send_buf.at[s], stage_sems.at[s]))
                keep.append(pltpu.make_async_copy(ours, keep_buf.at[s], keep_sems.at[s]))
                push.append(pltpu.make_async_remote_copy(
                    src_ref=send_buf.at[s], dst_ref=recv_buf.at[s], send_sem=send_sems.at[s],
                    recv_sem=recv_sems.at[s], device_id=sibling, device_id_type=MESH_IDS))
        for cp in stage + keep:
            cp.start()
        for s in range(4 * n):
            stage[s].wait()
            push[s].start()
        for s in range(4 * n):
            push[s].wait_recv()
            keep[s].wait()
            a, j = divmod(s, 4)
            dst = own_refs[a] if j == 0 else pay_refs[a].at[j - 1]

            def add(r, carry, s=s, dst=dst):
                sl = pl.ds(pl.multiple_of(r * chunk, chunk), chunk)
                dst[sl, :] = (keep_buf[s, sl, :] + recv_buf[s, sl, :]).astype(dst.dtype)
                return carry

            lax.fori_loop(0, rows // chunk, add, 0)
        for cp in push:
            cp.wait_send()

    vmem = pl.BlockSpec(memory_space=pltpu.VMEM)
    buf = pltpu.VMEM((4 * n, rows, cols), F32)
    sems = pltpu.SemaphoreType.DMA((4 * n,))
    outs = pl.pallas_call(
        body, name=name,
        in_specs=[pl.BlockSpec(memory_space=pl.ANY)] * n, out_specs=[vmem] * (2 * n),
        out_shape=[jax.ShapeDtypeStruct((rows, cols), F32)] * n + [jax.ShapeDtypeStruct((3, rows, cols), BF16)] * n,
        scratch_shapes=[buf, buf, buf, sems, sems, sems, sems],
        compiler_params=pltpu.CompilerParams(vmem_limit_bytes=VMEM_LIMIT),
    )(*gs)
    return list(zip(outs[:n], outs[n:]))


_HBM_SPEC = pl.BlockSpec(memory_space=pltpu.HBM)
_SEM_SPEC = pl.BlockSpec(memory_space=pltpu.SEMAPHORE)
_SIDE_EFFECT = pltpu.SideEffectType.DATAFLOW_SIDE_EFFECTING


def _exchange_start(name, srcs, lands, plan, n_copies):
    nb = len(srcs) + len(lands)

    def body(*refs):
        bufs, send_sems, recv_sems, token = refs[:nb], refs[nb], refs[nb + 1], refs[-1]
        for cp in plan(bufs[:len(srcs)], bufs[len(srcs):], send_sems, recv_sems):
            cp.start()
        token[...] = jnp.zeros_like(token)

    arrays = list(srcs) + list(lands)
    outs = pl.pallas_call(
        body, name=name,
        out_shape=(pltpu.SemaphoreType.DMA((n_copies,)), pltpu.SemaphoreType.DMA((n_copies,)),
                   *[pltpu.HBM(a.shape, a.dtype) for a in arrays], jax.ShapeDtypeStruct((SUBLANES, LANES), F32)),
        in_specs=[_HBM_SPEC] * nb,
        out_specs=(_SEM_SPEC, _SEM_SPEC, *[_HBM_SPEC] * nb, pl.BlockSpec(memory_space=pltpu.VMEM)),
        input_output_aliases={k: 2 + k for k in range(nb)},
        compiler_params=pltpu.CompilerParams(has_side_effects=_SIDE_EFFECT),
    )(*[pltpu.with_memory_space_constraint(a, pltpu.HBM) for a in arrays])
    return outs[0], outs[1], outs[2:2 + len(srcs)], outs[2 + len(srcs):2 + nb], outs[-1]


def _exchange_wait(name, send_sems, recv_sems, srcs, lands, plan, after):
    nb = len(srcs) + len(lands)
    after = list(after)

    def body(*refs):
        bufs, send_ref, recv_ref = refs[:nb], refs[nb], refs[nb + 1]
        for cp in plan(bufs[:len(srcs)], bufs[len(srcs):], send_ref, recv_ref):
            cp.wait_send()
            cp.wait_recv()

    arrays = list(srcs) + list(lands)
    outs = pl.pallas_call(
        body, name=name,
        out_shape=tuple(pltpu.HBM(a.shape, a.dtype) for a in arrays),
        in_specs=[_HBM_SPEC] * nb + [_SEM_SPEC, _SEM_SPEC] + [pl.BlockSpec(memory_space=pl.ANY)] * len(after),
        out_specs=tuple([_HBM_SPEC] * nb),
        input_output_aliases={k: k for k in range(nb)},
        compiler_params=pltpu.CompilerParams(has_side_effects=_SIDE_EFFECT),
    )(*arrays, send_sems, recv_sems, *after)
    return outs[len(srcs):]


def _gather_plan(axes, sizes):
    def plan(src_refs, land_refs, send_sems, recv_sems):
        x, y, c = _mesh_pos()
        copies = []
        for a, (src, land) in enumerate(zip(src_refs, land_refs)):
            mine = _block_of(land, axes[a], 4 * x + 2 * y + c, sizes[a])
            for k in range(1, N_DEV):
                peer = (1 - x if k & 4 else x, 1 - y if k & 2 else y, 1 - c if k & 1 else c)
                idx = a * (N_DEV - 1) + k - 1
                copies.append(pltpu.make_async_remote_copy(
                    src_ref=src, dst_ref=mine, send_sem=send_sems.at[idx], recv_sem=recv_sems.at[idx],
                    device_id=peer, device_id_type=MESH_IDS))
        return copies
    return plan


def _chip_plan(src_refs, land_refs, send_sems, recv_sems):
    x, y, c = _mesh_pos()
    copies = []
    for a, (src, land) in enumerate(zip(src_refs, land_refs)):
        for j, chip in enumerate(_other_chips(x, y)):
            copies.append(pltpu.make_async_remote_copy(
                src_ref=src.at[j], dst_ref=land.at[j], send_sem=send_sems.at[3 * a + j],
                recv_sem=recv_sems.at[3 * a + j], device_id=(*chip, c), device_id_type=MESH_IDS))
    return copies


def _own_block_placed(shard, axis, me):
    if axis == 0:
        full = lax.empty((N_DEV,) + shard.shape, shard.dtype)
        return lax.dynamic_update_slice(full, shard[None], (me,) + (0,) * shard.ndim)
    rows, cols = shard.shape

    def body(me_ref, s_ref, o_ref):
        del me_ref
        o_ref[...] = s_ref[...]

    return pl.pallas_call(
        body, name="place_own_columns",
        grid_spec=pltpu.PrefetchScalarGridSpec(
            num_scalar_prefetch=1, grid=(1,),
            in_specs=[pl.BlockSpec((rows, cols), lambda i, me_ref: (0, 0))],
            out_specs=pl.BlockSpec((rows, cols), lambda i, me_ref: (0, me_ref[0]))),
        out_shape=jax.ShapeDtypeStruct((rows, N_DEV * cols), shard.dtype),
    )(jnp.reshape(me, (1,)).astype(jnp.int32), shard)


def _local_step(x, target, mod, win, late_weights, p, grads_ready=None):
    shift1, scale1, gate1, shift2, scale2, gate2 = (mod[k] for k in range(6))

    def after_token(v, token):
        return v if token is None else v + token[0:1, 0:1]
    wa, wx = p["lru_w_a"].astype(BF16), p["lru_w_x"].astype(BF16)
    mask = jnp.tril(jnp.ones((HD, HD), F32))
    wm = (p["sgu_w_s"] * mask).astype(BF16)
    wmt = jnp.swapaxes(wm, 1, 2)
    bst = jnp.transpose(p["sgu_b_s"])

    h1, z = _norm_proj(x, p["norm_mix_g"], scale1, shift1, win, "mix_proj")
    hstate, ya_pre, *rnn_saved = _rnn_fwd(
        z, p["rnn_conv_w"], p["rnn_conv_b"], wa, p["lru_b_a"], wx, p["lru_b_x"], p["lru_lambda"])
    yb_pre, *sgu_saved = _sgu_fwd(z, p["sgu_ln_g"], p["sgu_ln_b"], wm, bst)
    wba, wbb, wout = late_weights("merge", [ya_pre, yb_pre])
    x2, ya, yb, merged, o1 = _merge_fwd(ya_pre, yb_pre, z, x, gate1, wba, wbb, wout)
    wup = late_weights("ffn_up", [x2])
    h2, up_a, up_v, ff, fa, fv = _ffn_proj_mid(
        x2, p["norm_ffn_g"], scale2, shift2, wup, p["ffn_conv_w"], p["ffn_conv_b"])
    wd = late_weights("ffn_down", [ff])
    dx3, loss, d_gfin, d_gate2 = _ffn_out_loss(ff, wd, x2, target, gate2, p["norm_final_g"])

    dact, dval, d_wd, dcb_a, dcb_v = _ffn_down_bwd(dx3, gate2, ff, fa, fv, wd)
    dup, dx2, do1, d_cwf, d_shift2, d_scale2, d_gffn, d_gate1 = _ffn_up_bwd(
        dact, dval, up_a, up_v, p["ffn_conv_w"], wup, x2, dx3, p["norm_ffn_g"], scale2, o1, gate1)
    d_wup = _xt_y(h2, dup, "w_up_grad")
    ready = grads_ready if grads_ready else (lambda stage, big, small: None)
    token = ready("ffn", {"w_up": d_wup, "w_down": d_wd}, {})

    dya, dyb, dz, d_wout, d_win = _out_bwd(do1, wout, merged, ya, yb, z, h1)
    dz, d_win, d_wba, d_cw, d_cb, d_wa, d_ba, d_wx, d_bx, d_lam = _rnn_bwd(
        dya, ya_pre, wba, h1, z, rnn_saved, hstate, dz, d_win, p["rnn_conv_w"], wa, wx,
        after_token(p["lru_lambda"], token))
    small = {
        "rnn_conv_w": d_cw, "rnn_conv_b": d_cb, "lru_w_a": d_wa, "lru_b_a": d_ba, "lru_w_x": d_wx, "lru_b_x": d_bx,
        "lru_lambda": d_lam, "norm_ffn_g": d_gffn, "ffn_conv_w": d_cwf,
        "ffn_conv_b": jnp.concatenate([dcb_a, dcb_v], axis=1), "norm_final_g": d_gfin,
    }
    token = ready("rnn", {}, small)
    dz, d_win, d_wbb, d_ws, d_bst, d_lng, d_lnb = _sgu_bwd(
        dyb, yb_pre, wbb, h1, sgu_saved, dz, d_win, p["sgu_ln_g"], after_token(p["sgu_ln_b"], token), wmt, mask)
    sgu_small = {"sgu_ln_g": d_lng, "sgu_ln_b": d_lnb, "sgu_w_s": d_ws, "sgu_b_s": jnp.transpose(d_bst)}
    mixer = {"w_in": d_win, "w_out": d_wout, "w_branch_a": d_wba, "w_branch_b": d_wbb}
    token = ready("mixer", mixer, sgu_small)
    grad_x, d_shift1, d_scale1, d_gmix = _in_bwd(dz, win, x, dx2, after_token(p["norm_mix_g"], token), scale1)

    small.update(sgu_small)
    small["norm_mix_g"] = d_gmix
    dmod = jnp.stack([d_shift1, d_scale1, d_gate1, d_shift2, d_scale2, d_gate2])
    big = {"w_in": d_win, "w_up": d_wup, "w_branch_a": d_wba, "w_branch_b": d_wbb, "w_out": d_wout, "w_down": d_wd}
    return loss, grad_x, big, small, dmod


LAST_REP = ["b_ada", "norm_mix_g"]
EARLY_REP = ["rnn_conv_b", "lru_w_a", "lru_b_a", "lru_w_x", "lru_b_x", "lru_lambda", "norm_ffn_g", "ffn_conv_b",
             "norm_final_g"]
MID_REP = ["sgu_ln_g", "sgu_ln_b", "sgu_w_s", "sgu_b_s"]
COL_SHARDED = ["rnn_conv_w", "ffn_conv_w"]
SMALL_GROUPS = {"rnn": EARLY_REP + COL_SHARDED, "mixer": MID_REP, "last": LAST_REP}
REPLICATED = LAST_REP + EARLY_REP + MID_REP
SMALL_NAMES = REPLICATED + COL_SHARDED
BIG_NAMES = ["w_in", "w_up", "w_branch_a", "w_branch_b", "w_out", "w_down"]
BIG_AXES = [1, 1, 0, 0, 0, 0]
WEIGHTS = ["w_ada", "b_ada", "norm_mix_g", "w_in", "rnn_conv_w", "rnn_conv_b", "lru_w_a", "lru_b_a", "lru_w_x",
           "lru_b_x", "lru_lambda", "sgu_ln_g", "sgu_ln_b", "sgu_w_s", "sgu_b_s", "w_branch_a", "w_branch_b",
           "w_out", "norm_ffn_g", "w_up", "ffn_conv_w", "ffn_conv_b", "w_down", "norm_final_g"]


def _pack_rows(shape):
    return math.prod(shape) // LANES


def _pack(arrays):
    return jnp.concatenate([a.reshape(-1, LANES) for a in arrays], axis=0)


def kernel(x, c, w_ada, b_ada, norm_mix_g, w_in, rnn_conv_w, rnn_conv_b, lru_w_a, lru_b_a, lru_w_x, lru_b_x, lru_lambda, sgu_ln_g, sgu_ln_b, sgu_w_s, sgu_b_s, w_branch_a, w_branch_b, w_out, norm_ffn_g, w_up, ffn_conv_w, ffn_conv_b, w_down, norm_final_g, loss_target, m_w_ada, m_b_ada, m_norm_mix_g, m_w_in, m_rnn_conv_w, m_rnn_conv_b, m_lru_w_a, m_lru_b_a, m_lru_w_x, m_lru_b_x, m_lru_lambda, m_sgu_ln_g, m_sgu_ln_b, m_sgu_w_s, m_sgu_b_s, m_w_branch_a, m_w_branch_b, m_w_out, m_norm_ffn_g, m_w_up, m_ffn_conv_w, m_ffn_conv_b, m_w_down, m_norm_final_g, v_w_ada, v_b_ada, v_norm_mix_g, v_w_in, v_rnn_conv_w, v_rnn_conv_b, v_lru_w_a, v_lru_b_a, v_lru_w_x, v_lru_b_x, v_lru_lambda, v_sgu_ln_g, v_sgu_ln_b, v_sgu_w_s, v_sgu_b_s, v_w_branch_a, v_w_branch_b, v_w_out, v_norm_ffn_g, v_w_up, v_ffn_conv_w, v_ffn_conv_b, v_w_down, v_norm_final_g):
    given = dict(locals())
    me = 4 * lax.axis_index("x") + 2 * lax.axis_index("y") + lax.axis_index("c")
    ada_cols = w_ada.shape[2]
    conv_cols = {"rnn_conv_w": rnn_conv_w.shape[2], "ffn_conv_w": ffn_conv_w.shape[2]}

    (win, c_all, cw_rnn, cw_ffn), _ = _all_gather(
        [w_in[0].astype(BF16), c.reshape(1, 1, D), rnn_conv_w[0], ffn_conv_w[0]], [1, 0, 1, 1], "gather_first")
    c_all = c_all.reshape(N_DEV, D)

    b_cols = lax.dynamic_slice_in_dim(b_ada, me * ada_cols, ada_cols, axis=1)
    (mod_all,), mod_done = _all_gather(
        [_mod_cols(c_all, w_ada[0], b_cols).reshape(1, N_DEV, ada_cols)], [0], "gather_mod")
    mod_all = mod_all.reshape(N_DEV, N_DEV, ada_cols)
    mod_mine = lax.dynamic_index_in_dim(mod_all, me, axis=1, keepdims=False).reshape(6, 1, D)

    late_groups = {"merge": (["w_branch_a", "w_branch_b", "w_out"], [0, 0, 0]), "ffn_up": (["w_up"], [1]),
                   "ffn_down": (["w_down"], [0])}
    in_flight, started = {}, mod_done[0:1, 0:1]
    for stage, (names, axes) in late_groups.items():
        shards = [(given[n][0] + started).astype(BF16) for n in names]
        plan = _gather_plan(axes, [s.shape[-1] for s in shards])
        send, recv, srcs, lands, token = _exchange_start(
            "gather_start_" + stage, shards, [_own_block_placed(s, ax, me) for s, ax in zip(shards, axes)], plan,
            len(shards) * (N_DEV - 1))
        in_flight[stage] = (send, recv, srcs, lands, plan)
        started = started + token[0:1, 0:1]

    def late_weights(stage, after):
        send, recv, srcs, lands, plan = in_flight[stage]
        full = _exchange_wait("gather_wait_" + stage, send, recv, srcs, lands, plan, after)
        full = [w.reshape(-1, D) if ax == 0 else w for w, ax in zip(full, late_groups[stage][1])]
        return full if len(full) > 1 else full[0]

    mod_mine = mod_mine + started

    reducing, packing = {}, {}

    def start_pack(stage, small):
        pack = _pack([small[n] for n in SMALL_GROUPS[stage]])[None]
        plan = _gather_plan([0], [LANES])
        send, recv, srcs, lands, tok = _exchange_start(
            "small_start_" + stage, [pack], [_own_block_placed(pack, 0, me)], plan, N_DEV - 1)
        packing[stage] = (send, recv, srcs, lands, plan)
        return tok

    def grads_ready(stage, grads, small):
        tokens = [start_pack(stage, small)] if small else []
        if grads:
            tokens.append(start_reduce(stage, grads))
        return sum(tokens[1:], tokens[0])

    def start_reduce(stage, grads):
        names = [n for n in BIG_NAMES if n in grads]
        blocked = {}
        for n in names:
            ax = BIG_AXES[BIG_NAMES.index(n)]
            g = grads[n] if ax == 1 else grads[n].reshape(N_DEV, grads[n].shape[0] // N_DEV, grads[n].shape[1])
            blocked.setdefault((ax, g.shape), []).append((n, g))
        sums = {}
        for (ax, _), group in blocked.items():
            reduced = _sibling_reduce([g for _, g in group], ax, "reduce_sibling_" + "_".join(n for n, _ in group))
            sums.update({n: r for (n, _), r in zip(group, reduced)})
        sums = [sums[n] for n in names]
        pays = [pay for _, pay in sums]
        send, recv, srcs, lands, tok = _exchange_start(
            "reduce_start_" + stage, pays, [lax.empty(p_.shape, p_.dtype) for p_ in pays], _chip_plan, 3 * len(pays))
        reducing[stage] = (names, [own for own, _ in sums], send, recv, srcs, lands)
        return tok

    p = {n: given[n][0] for n in REPLICATED if n not in ("b_ada", "norm_final_g")}
    p = {n: (a.reshape(1, -1) if a.ndim == 1 else a) for n, a in p.items()}
    p["rnn_conv_w"], p["ffn_conv_w"] = cw_rnn, cw_ffn
    p["norm_final_g"] = norm_final_g.reshape(1, D)
    loss, grad_x, _, small, dmod = _local_step(x[0], loss_target[0], mod_mine, win, late_weights, p, grads_ready)

    small["b_ada"] = dmod.reshape(1, 6 * D)
    rows_of = {n: _pack_rows(small[n].shape) for n in SMALL_NAMES}
    (last,), _ = _all_gather([_pack([small[n] for n in LAST_REP])[None]], [0], "gather_small")
    gathered = {"last": last}
    for stage, (send, recv, srcs, lands, plan) in packing.items():
        (gathered[stage],) = _exchange_wait("small_wait_" + stage, send, recv, srcs, lands, plan, [grad_x])
    gathered = {k: v.reshape(N_DEV, -1, LANES) for k, v in gathered.items()}

    out = {}
    for stage, (names, owns, send, recv, srcs, lands) in reducing.items():
        landed = _exchange_wait("reduce_wait_" + stage, send, recv, srcs, lands, _chip_plan, [last])
        for n, own, got in zip(names, owns, landed):
            out[n] = _adamw(given[n][0], given["m_" + n][0], given["v_" + n][0], [own, got], "adamw_" + n)

    dmod_all = gathered["last"][:, :rows_of["b_ada"]].reshape(N_DEV, 6 * D)
    dmod_cols = lax.dynamic_slice_in_dim(dmod_all, me * ada_cols, ada_cols, axis=1)
    out["w_ada"] = _adamw(w_ada[0], m_w_ada[0], v_w_ada[0], [_ada_grad(c_all, dmod_cols)], "adamw_w_ada")

    def rows_form(a):
        return a.reshape(1, -1) if a.size // a.shape[-1] == 1 or a.ndim == 1 else a.reshape(-1, LANES)

    for stage, names in (("last", LAST_REP), ("rnn", EARLY_REP), ("mixer", MID_REP)):
        out.update(_adamw_group(names, *[[rows_form(given[pre + n]) for n in names] for pre in ("", "m_", "v_")],
                                gathered[stage], "adamw_small_" + stage))

    row0 = sum(rows_of[n] for n in EARLY_REP)
    for n in COL_SHARDED:
        full = gathered["rnn"][:, row0:row0 + rows_of[n]].reshape(N_DEV, small[n].shape[0], small[n].shape[1])
        mine = lax.dynamic_slice_in_dim(full, me * conv_cols[n], conv_cols[n], axis=2)
        out[n] = _adamw(given[n][0], given["m_" + n][0], given["v_" + n][0], [mine], "adamw_" + n)
        row0 += rows_of[n]

    total = lax.psum(loss[0, 0], ("x", "y", "c"))
    results = [total, grad_x[None]]
    for kind in range(4):
        results += [out[n][kind].reshape(given[n].shape) for n in WEIGHTS]
    return tuple(results)
```

```python
import math

import jax
import jax.numpy as jnp
from jax import lax
from jax.experimental import pallas as pl
from jax.experimental.pallas import tpu as pltpu

F32 = jnp.float32
BF16 = jnp.bfloat16
MESH_IDS = pl.DeviceIdType.MESH

D = 1024
NH = 8
HD = 128
NCOL_IN = 6 * D
DFF = 3 * D
N_DEV = 8
EPS = 1e-6
LRU_C = 8.0
ADAM_LR, ADAM_B1, ADAM_B2, ADAM_EPS, ADAM_WD, ADAM_STEP = 0.001, 0.9, 0.999, 1e-08, 0.01, 10

SUBLANES = 8
LANES = 128
HALO = 16
VMEM_LIMIT = 56 * 1024 * 1024
GELU_K = math.sqrt(2.0 / math.pi)
GELU_C = 0.044715


def _cparams(n_axes):
    return pltpu.CompilerParams(dimension_semantics=("arbitrary",) * n_axes, vmem_limit_bytes=VMEM_LIMIT)


def _const_spec(shape, single_buffer=False):
    nd = len(shape)
    if single_buffer:
        return pl.BlockSpec(shape, lambda *_: (0,) * nd, pipeline_mode=pl.Buffered(1))
    return pl.BlockSpec(shape, lambda *_: (0,) * nd)


def _tile_big(t):
    return min(512, t)


def _tile_seq(t):
    return min(256, t)


def _row_tile(rows, cols):
    cap = max(SUBLANES, (2 * 1024 * 1024) // (4 * cols) // SUBLANES * SUBLANES)
    if rows <= cap:
        return rows
    return next(tr for tr in range(cap, 0, -SUBLANES) if rows % tr == 0)


def _gelu_t(x):
    x2 = x * x
    t = jnp.tanh(x * (GELU_K + (GELU_K * GELU_C) * x2))
    hx = 0.5 * x
    return hx + hx * t, (x2, hx, t)


def _gelu_grad(shared):
    x2, hx, t = shared
    return (0.5 + 0.5 * t) + (hx * (1.0 - t * t)) * (GELU_K + (3.0 * GELU_K * GELU_C) * x2)


def _sigmoid(x):
    return 1.0 / (1.0 + jnp.exp(-x))


def _log_sigmoid(x):
    return -(jnp.maximum(-x, 0.0) + jnp.log1p(jnp.exp(-jnp.abs(x))))


def _row_iota(cols):
    return lax.broadcasted_iota(jnp.int32, (SUBLANES, cols), 0)


def _shift_down(x, k, prev8):
    if k == 0:
        return x
    r = pltpu.roll(x, k, 0)
    p = pltpu.roll(prev8, k, 0)
    head = jnp.where(_row_iota(x.shape[1]) < k, p, r[:SUBLANES])
    return jnp.concatenate([head, r[SUBLANES:]], axis=0)


def _shift_up(x, k, next8):
    if k == 0:
        return x
    n = x.shape[0]
    r = pltpu.roll(x, n - k, 0)
    q = pltpu.roll(next8, SUBLANES - k, 0)
    tail = jnp.where(_row_iota(x.shape[1]) >= SUBLANES - k, q, r[n - SUBLANES:])
    return jnp.concatenate([r[:n - SUBLANES], tail], axis=0)


def _heads_nn(x_bf, w_ref):
    return jnp.concatenate(
        [jnp.dot(x_bf[:, h * HD:(h + 1) * HD], w_ref[h], preferred_element_type=F32) for h in range(NH)], axis=1)


def _heads_nt(x_bf, w_ref):
    return jnp.concatenate(
        [lax.dot_general(x_bf[:, h * HD:(h + 1) * HD], w_ref[h], (((1,), (1,)), ((), ())), preferred_element_type=F32)
         for h in range(NH)], axis=1)


def _dot_nt(a, b):
    return lax.dot_general(a, b, (((1,), (1,)), ((), ())), preferred_element_type=F32)


def _dot_tn(a, b):
    return lax.dot_general(a, b, (((0,), (0,)), ((), ())), preferred_element_type=F32)


def _colsum(x):
    return jnp.sum(x, axis=0, keepdims=True)


def _prev_halo_map(tm, col):
    return lambda i, *_: (jnp.maximum(i * (tm // HALO) - 1, 0), col)


def _norm_proj(x, g, scale, shift, w, name):
    t, n = x.shape[0], w.shape[1]
    tm = _tile_big(t)

    def body(x_ref, g_ref, sc_ref, sh_ref, w_ref, h_ref, z_ref):
        xv = x_ref[...]
        r = lax.rsqrt(jnp.mean(xv * xv, axis=-1, keepdims=True) + EPS)
        hb = ((xv * r * g_ref[...]) * (1.0 + sc_ref[...]) + sh_ref[...]).astype(BF16)
        h_ref[...] = hb
        for c0 in range(0, n, D):
            z_ref[:, c0:c0 + D] = jnp.dot(hb, w_ref[:, c0:c0 + D], preferred_element_type=F32).astype(BF16)

    vec = _const_spec((1, D))
    return pl.pallas_call(
        body, name=name, grid=(t // tm,),
        in_specs=[pl.BlockSpec((tm, D), lambda i: (i, 0)), vec, vec, vec, _const_spec((D, n), True)],
        out_specs=[pl.BlockSpec((tm, D), lambda i: (i, 0)), pl.BlockSpec((tm, n), lambda i: (i, 0))],
        out_shape=[jax.ShapeDtypeStruct((t, D), BF16), jax.ShapeDtypeStruct((t, n), BF16)],
        compiler_params=_cparams(1),
    )(x, g, scale, shift, w)


def _lru_gates(xc, wa_ref, ba, wx_ref, bx, ls):
    xb = xc.astype(BF16)
    ra = _sigmoid(_heads_nn(xb, wa_ref) + ba)
    ia = _sigmoid(_heads_nn(xb, wx_ref) + bx)
    la = LRU_C * ra * ls
    a = jnp.exp(la)
    mult = jnp.sqrt(-jnp.tanh(la) * (1.0 + a * a))
    return ra, ia, a, mult


def _conv4(xr, prev8, cw_ref, cb):
    return (cb + cw_ref[3:4, :] * xr + cw_ref[2:3, :] * _shift_down(xr, 1, prev8)
            + cw_ref[1:2, :] * _shift_down(xr, 2, prev8) + cw_ref[0:1, :] * _shift_down(xr, 3, prev8))


def _rnn_fwd(z, cw, cb, wa, ba, wx, bx, lam):
    t = z.shape[0]
    tm = _tile_seq(t)
    ngrp = tm // SUBLANES

    def body(xr_ref, xp_ref, gr_ref, cw_ref, cb_ref, wa_ref, ba_ref, wx_ref, bx_ref, lam_ref,
             h_ref, ya_ref, xc_ref, ra_ref, ia_ref, gg_ref, hg_ref, carry_ref, a_scr, u_scr):
        i = pl.program_id(0)

        @pl.when(i == 0)
        def _():
            carry_ref[...] = jnp.zeros_like(carry_ref)

        xr = xr_ref[...].astype(F32)
        prev8 = jnp.where(i == 0, 0.0, xp_ref[...].astype(F32)[HALO - SUBLANES:])
        xc = _conv4(xr, prev8, cw_ref, cb_ref[...])
        ra, ia, a, mult = _lru_gates(xc, wa_ref, ba_ref[...], wx_ref, bx_ref[...], _log_sigmoid(lam_ref[...]))
        xc_ref[...] = xc.astype(BF16)
        ra_ref[...] = ra.astype(BF16)
        ia_ref[...] = ia.astype(BF16)
        a_scr[...] = a
        u_scr[...] = mult * (ia * xc)
        row = _row_iota(D)

        def grp(j, carry):
            r0 = pl.multiple_of(j * SUBLANES, SUBLANES)
            av = a_scr[pl.ds(r0, SUBLANES), :]
            uv = u_scr[pl.ds(r0, SUBLANES), :]
            for d in (1, 2, 4):
                m = row >= d
                uv = jnp.where(m, av * pltpu.roll(uv, d, 0) + uv, uv)
                av = jnp.where(m, av * pltpu.roll(av, d, 0), av)
            hv = uv + av * carry
            h_ref[pl.ds(r0, SUBLANES), :] = hv
            return hv[SUBLANES - 1:SUBLANES, :]

        carry_ref[0:1, :] = lax.fori_loop(0, ngrp, grp, carry_ref[0:1, :])
        grv = gr_ref[...].astype(F32)
        gg, tg = _gelu_t(grv)
        hv = h_ref[...]
        ya_ref[...] = (hv * gg).astype(BF16)
        gg_ref[...] = gg.astype(BF16)
        hg_ref[...] = (hv * _gelu_grad(tg)).astype(BF16)

    vec = _const_spec((1, D))
    wspec = _const_spec((NH, HD, HD))
    tile = pl.BlockSpec((tm, D), lambda i: (i, 0))
    bshape = jax.ShapeDtypeStruct((t, D), BF16)
    return pl.pallas_call(
        body, name="rnn_fwd", grid=(t // tm,),
        in_specs=[tile, pl.BlockSpec((HALO, D), _prev_halo_map(tm, 0)),
                  pl.BlockSpec((tm, D), lambda i: (i, 1)), _const_spec((4, D)), vec, wspec, vec, wspec, vec, vec],
        out_specs=[tile] * 7,
        out_shape=[jax.ShapeDtypeStruct((t, D), F32)] + [bshape] * 6,
        scratch_shapes=[pltpu.VMEM((SUBLANES, D), F32), pltpu.VMEM((tm, D), F32), pltpu.VMEM((tm, D), F32)],
        compiler_params=_cparams(1),
    )(z, z, z, cw, cb, wa, ba, wx, bx, lam)


def _sgu_fwd(z, lng, lnb, wm, bst):
    t = z.shape[0]
    tm = _tile_seq(t)

    def body(zu_ref, zv_ref, lng_ref, lnb_ref, wm_ref, bst_ref, yb_ref, gu_ref, mg_ref, vh_ref, gpv_ref, rstd_ref):
        gu, su = _gelu_t(zu_ref[...].astype(F32))
        gv, sv = _gelu_t(zv_ref[...].astype(F32))
        mu = jnp.mean(gv, axis=-1, keepdims=True)
        cen = gv - mu
        rstd = lax.rsqrt(jnp.mean(cen * cen, axis=-1, keepdims=True) + EPS)
        vhat = cen * rstd
        vb = (vhat * lng_ref[...] + lnb_ref[...]).astype(BF16)
        rows = []
        for b0 in range(0, tm, HD):
            rows.append(jnp.concatenate(
                [jnp.dot(wm_ref[g], vb[b0:b0 + HD, g * HD:(g + 1) * HD], preferred_element_type=F32)
                 + bst_ref[:, g:g + 1] for g in range(NH)], axis=1))
        mixed = jnp.concatenate(rows, axis=0) if len(rows) > 1 else rows[0]
        yb_ref[...] = (gu * mixed).astype(BF16)
        gu_ref[...] = gu.astype(BF16)
        mg_ref[...] = (mixed * _gelu_grad(su)).astype(BF16)
        vh_ref[...] = vhat.astype(BF16)
        gpv_ref[...] = _gelu_grad(sv).astype(BF16)
        rstd_ref[...] = rstd

    vec = _const_spec((1, D))
    tile = pl.BlockSpec((tm, D), lambda i: (i, 0))
    bshape = jax.ShapeDtypeStruct((t, D), BF16)
    return pl.pallas_call(
        body, name="sgu_fwd", grid=(t // tm,),
        in_specs=[pl.BlockSpec((tm, D), lambda i: (i, 2)), pl.BlockSpec((tm, D), lambda i: (i, 3)), vec, vec,
                  _const_spec((NH, HD, HD)), _const_spec((HD, NH))],
        out_specs=[tile] * 5 + [pl.BlockSpec((tm, 1), lambda i: (i, 0))],
        out_shape=[bshape] * 5 + [jax.ShapeDtypeStruct((t, 1), F32)],
        compiler_params=_cparams(1),
    )(z, z, lng, lnb, wm, bst)


def _merge_fwd(ya_pre, yb_pre, z, x, gate1, wba, wbb, wout):
    t = x.shape[0]
    tm = _tile_big(t)

    def body(yap_ref, ybp_ref, ga_ref, gb_ref, x_ref, g1_ref, wba_ref, wbb_ref, wo_ref,
             x2_ref, ya_ref, yb_ref, mg_ref, o1_ref):
        ya = jnp.dot(yap_ref[...], wba_ref[...], preferred_element_type=F32)
        yb = jnp.dot(ybp_ref[...], wbb_ref[...], preferred_element_type=F32)
        merged = _sigmoid(ga_ref[...].astype(F32)) * ya + _sigmoid(gb_ref[...].astype(F32)) * yb
        mb = merged.astype(BF16)
        o1 = jnp.dot(mb, wo_ref[...], preferred_element_type=F32)
        x2_ref[...] = x_ref[...] + g1_ref[...] * o1
        ya_ref[...] = ya.astype(BF16)
        yb_ref[...] = yb.astype(BF16)
        mg_ref[...] = mb
        o1_ref[...] = o1.astype(BF16)

    tile = pl.BlockSpec((tm, D), lambda i: (i, 0))
    wspec = _const_spec((D, D))
    bshape = jax.ShapeDtypeStruct((t, D), BF16)
    return pl.pallas_call(
        body, name="merge_fwd", grid=(t // tm,),
        in_specs=[tile, tile, pl.BlockSpec((tm, D), lambda i: (i, 4)), pl.BlockSpec((tm, D), lambda i: (i, 5)),
                  tile, _const_spec((1, D)), wspec, wspec, wspec],
        out_specs=[tile] * 5,
        out_shape=[jax.ShapeDtypeStruct((t, D), F32), bshape, bshape, bshape, bshape],
        compiler_params=_cparams(1),
    )(ya_pre, yb_pre, z, z, x, gate1, wba, wbb, wout)


def _conv3(u, prev8, cw_ref, cb):
    return cb + cw_ref[2:3, :] * u + cw_ref[1:2, :] * _shift_down(u, 1, prev8) + cw_ref[0:1, :] * _shift_down(u, 2, prev8)


def _ffn_proj_mid(x2, g, scale, shift, w, cw, cb):
    t = x2.shape[0]
    tm = _tile_big(t)
    nc = DFF // D

    def body(x_ref, g_ref, sc_ref, sh_ref, wa_ref, wv_ref, cwa_ref, cwv_ref, cba_ref, cbv_ref,
             h_ref, upa_ref, upv_ref, ff_ref, fa_ref, fv_ref, hb_scr, prev_ref):
        i, c = pl.program_id(0), pl.program_id(1)

        @pl.when(i == 0)
        def _():
            prev_ref[c] = jnp.zeros((2, SUBLANES, D), F32)

        @pl.when(c == 0)
        def _():
            xv = x_ref[...]
            r = lax.rsqrt(jnp.mean(xv * xv, axis=-1, keepdims=True) + EPS)
            hb_scr[...] = ((xv * r * g_ref[...]) * (1.0 + sc_ref[...]) + sh_ref[...]).astype(BF16)
            h_ref[...] = hb_scr[...]

        hb = hb_scr[...]
        halves = []
        for s, (w_ref, up_ref, cw_ref, cb_ref) in enumerate(((wa_ref, upa_ref, cwa_ref, cba_ref),
                                                             (wv_ref, upv_ref, cwv_ref, cbv_ref))):
            u = jnp.dot(hb, w_ref[...], preferred_element_type=F32)
            up_ref[...] = u.astype(BF16)
            halves.append(_conv3(u, prev_ref[c, s], cw_ref, cb_ref[...]))
            prev_ref[c, s] = u[tm - SUBLANES:]
        act, val = halves
        ga, ta = _gelu_t(act)
        ff_ref[...] = (ga * val).astype(BF16)
        fa_ref[...] = (val * _gelu_grad(ta)).astype(BF16)
        fv_ref[...] = ga.astype(BF16)

    def cols(rows, off):
        return pl.BlockSpec((rows, D), lambda i, c: (0, off + c))

    vec = pl.BlockSpec((1, D), lambda i, c: (0, 0))
    row_tile = pl.BlockSpec((tm, D), lambda i, c: (i, 0))
    chunk = pl.BlockSpec((tm, D), lambda i, c: (i, c))
    hshape = jax.ShapeDtypeStruct((t, DFF), BF16)
    return pl.pallas_call(
        body, name="ffn_proj_mid", grid=(t // tm, nc),
        in_specs=[row_tile, vec, vec, vec, cols(D, 0), cols(D, nc), cols(3, 0), cols(3, nc), cols(1, 0), cols(1, nc)],
        out_specs=[row_tile, chunk, chunk, chunk, chunk, chunk],
        out_shape=[jax.ShapeDtypeStruct((t, D), BF16), hshape, hshape, hshape, hshape, hshape],
        scratch_shapes=[pltpu.VMEM((tm, D), BF16), pltpu.VMEM((nc, 2, SUBLANES, D), F32)],
        compiler_params=_cparams(2),
    )(x2, g, scale, shift, w, w, cw, cw, cb, cb)


def _ffn_out_loss(ff, wd, x2, target, gate2, gfin):
    t = x2.shape[0]
    tm = _tile_big(t)

    def body(ff_ref, wd_ref, x2_ref, tg_ref, g2_ref, gf_ref, dx3_ref, loss_ref, dgf_ref, dg2_ref):
        @pl.when(pl.program_id(0) == 0)
        def _():
            loss_ref[...] = jnp.zeros_like(loss_ref)
            dgf_ref[...] = jnp.zeros_like(dgf_ref)
            dg2_ref[...] = jnp.zeros_like(dg2_ref)

        o2 = jnp.dot(ff_ref[...], wd_ref[...], preferred_element_type=F32)
        x3 = x2_ref[...] + g2_ref[...] * o2
        r = lax.rsqrt(jnp.mean(x3 * x3, axis=-1, keepdims=True) + EPS)
        xhat = x3 * r
        err = xhat * gf_ref[...] - tg_ref[...]
        loss_ref[...] += 0.5 * jnp.sum(jnp.mean(err * err, axis=-1, keepdims=True), axis=0, keepdims=True)
        dy = err * (1.0 / D)
        dgf_ref[...] += _colsum(dy * xhat)
        dxh = dy * gf_ref[...]
        dx3 = r * (dxh - xhat * jnp.mean(dxh * xhat, axis=-1, keepdims=True))
        dx3_ref[...] = dx3
        dg2_ref[...] += _colsum(dx3 * o2)

    tile = pl.BlockSpec((tm, D), lambda i: (i, 0))
    vec = _const_spec((1, D))
    return pl.pallas_call(
        body, name="ffn_out_loss", grid=(t // tm,),
        in_specs=[pl.BlockSpec((tm, DFF), lambda i: (i, 0)), _const_spec((DFF, D), True), tile, tile, vec, vec],
        out_specs=[tile, _const_spec((SUBLANES, LANES)), vec, vec],
        out_shape=[jax.ShapeDtypeStruct((t, D), F32), jax.ShapeDtypeStruct((SUBLANES, LANES), F32),
                   jax.ShapeDtypeStruct((1, D), F32), jax.ShapeDtypeStruct((1, D), F32)],
        compiler_params=_cparams(1),
    )(ff, wd, x2, target, gate2, gfin)


def _ffn_down_bwd(dx3, gate2, ff, fa, fv, wd):
    t = dx3.shape[0]
    tm = min(1024, t)
    nc = DFF // D

    def body(dx3_ref, g2_ref, ff_ref, fa_ref, fv_ref, wd_ref, da_ref, dv_ref, dwd_ref, dcba_ref, dcbv_ref):
        @pl.when(pl.program_id(1) == 0)
        def _():
            for r in (dwd_ref, dcba_ref, dcbv_ref):
                r[...] = jnp.zeros_like(r)

        do2 = (dx3_ref[...] * g2_ref[...]).astype(BF16)
        dwd_ref[...] += _dot_tn(ff_ref[...], do2)
        dff = _dot_nt(do2, wd_ref[...])
        dact = dff * fa_ref[...].astype(F32)
        dval = dff * fv_ref[...].astype(F32)
        da_ref[...] = dact.astype(BF16)
        dv_ref[...] = dval.astype(BF16)
        dcba_ref[...] += _colsum(dact)
        dcbv_ref[...] += _colsum(dval)

    blk = pl.BlockSpec((tm, D), lambda c, i: (i, c))
    vec = pl.BlockSpec((1, D), lambda c, i: (0, c))
    return pl.pallas_call(
        body, name="ffn_down_bwd", grid=(nc, t // tm),
        in_specs=[pl.BlockSpec((tm, D), lambda c, i: (i, 0)), pl.BlockSpec((1, D), lambda c, i: (0, 0)),
                  blk, blk, blk, pl.BlockSpec((D, D), lambda c, i: (c, 0))],
        out_specs=[blk, blk, pl.BlockSpec((D, D), lambda c, i: (c, 0)), vec, vec],
        out_shape=[jax.ShapeDtypeStruct((t, DFF), BF16), jax.ShapeDtypeStruct((t, DFF), BF16),
                   jax.ShapeDtypeStruct((DFF, D), F32),
                   jax.ShapeDtypeStruct((1, DFF), F32), jax.ShapeDtypeStruct((1, DFF), F32)],
        compiler_params=_cparams(2),
    )(dx3, gate2, ff, fa, fv, wd)


def _modnorm_bwd(dh, xv, g, scale):
    r = lax.rsqrt(jnp.mean(xv * xv, axis=-1, keepdims=True) + EPS)
    xhat = xv * r
    dxn = dh * (1.0 + scale)
    dxh = dxn * g
    dx = r * (dxh - xhat * jnp.mean(dxh * xhat, axis=-1, keepdims=True))
    return dx, _colsum(dh), _colsum(dh * (xhat * g)), _colsum(dxn * xhat)


def _ffn_up_bwd(dact, dval, up_a, up_v, cw, wup, x2, dx3, gffn, scale2, o1, gate1):
    t = x2.shape[0]
    tm = _tile_seq(t)
    nt = t // tm
    nc = DFF // D

    def body(da_ref, dan_ref, dv_ref, dvn_ref, ua_ref, uv_ref, cw_ref, w_ref, x2_ref, dx3_ref, g_ref, sc_ref, o1_ref, g1_ref,
             dup_ref, dx2_ref, do1_ref, dcw_ref, dsh_ref, dsc_ref, dg_ref, dg1_ref):
        i = pl.program_id(0)

        @pl.when(i == 0)
        def _():
            for r in (dcw_ref, dsh_ref, dsc_ref, dg_ref, dg1_ref):
                r[...] = jnp.zeros_like(r)

        last = i == nt - 1
        dh = jnp.zeros((tm, D), F32)
        for half, (d_ref, dn_ref, u_ref) in enumerate(((da_ref, dan_ref, ua_ref), (dv_ref, dvn_ref, uv_ref))):
            nxt = jnp.where(last, 0.0, dn_ref[...].astype(F32)[:SUBLANES])
            for c in range(nc):
                c0 = half * DFF + c * D
                dv = d_ref[:, c * D:(c + 1) * D].astype(F32)
                nx = nxt[:, c * D:(c + 1) * D]
                taps = (_shift_up(dv, 2, nx), _shift_up(dv, 1, nx), dv)
                dup = (cw_ref[2:3, c0:c0 + D] * taps[2] + cw_ref[1:2, c0:c0 + D] * taps[1]
                       + cw_ref[0:1, c0:c0 + D] * taps[0]).astype(BF16)
                upv = u_ref[:, c * D:(c + 1) * D].astype(F32)
                for k in range(3):
                    dcw_ref[k:k + 1, c0:c0 + D] += _colsum(taps[k] * upv)
                dup_ref[:, c0:c0 + D] = dup
                dh = dh + _dot_nt(dup, w_ref[:, c0:c0 + D])
        dxn, dsh, dsc, dg = _modnorm_bwd(dh, x2_ref[...], g_ref[...], sc_ref[...])
        dx2 = dx3_ref[...] + dxn
        dx2_ref[...] = dx2
        do1_ref[...] = (dx2 * g1_ref[...]).astype(BF16)
        dsh_ref[...] += dsh
        dsc_ref[...] += dsc
        dg_ref[...] += dg
        dg1_ref[...] += _colsum(dx2 * o1_ref[...].astype(F32))

    tile = pl.BlockSpec((tm, D), lambda i: (i, 0))
    wide = pl.BlockSpec((tm, DFF), lambda i: (i, 0))
    nxt = pl.BlockSpec((HALO, DFF), lambda i: (jnp.minimum((i + 1) * (tm // HALO), t // HALO - 1), 0))
    vec = _const_spec((1, D))
    vshape = jax.ShapeDtypeStruct((1, D), F32)
    return pl.pallas_call(
        body, name="ffn_up_bwd", grid=(nt,),
        in_specs=[wide, nxt, wide, nxt, wide, wide,
                  _const_spec((3, 2 * DFF)), _const_spec((D, 2 * DFF), True),
                  tile, tile, vec, vec, tile, vec],
        out_specs=[pl.BlockSpec((tm, 2 * DFF), lambda i: (i, 0)), tile, tile, _const_spec((3, 2 * DFF)),
                   vec, vec, vec, vec],
        out_shape=[jax.ShapeDtypeStruct((t, 2 * DFF), BF16), jax.ShapeDtypeStruct((t, D), F32),
                   jax.ShapeDtypeStruct((t, D), BF16), jax.ShapeDtypeStruct((3, 2 * DFF), F32),
                   vshape, vshape, vshape, vshape],
        compiler_params=_cparams(1),
    )(dact, dact, dval, dval, up_a, up_v, cw, wup, x2, dx3, gffn, scale2, o1, gate1)


def _xt_y(a, b, name):
    t, k = a.shape
    n = b.shape[1]
    tm = min(1024, t)
    bn = 1536 if n % 1536 == 0 else D

    def body(a_ref, b_ref, o_ref):
        @pl.when(pl.program_id(1) == 0)
        def _():
            o_ref[...] = jnp.zeros_like(o_ref)

        o_ref[...] += _dot_tn(a_ref[...], b_ref[...])

    return pl.pallas_call(
        body, name=name, grid=(n // bn, t // tm),
        in_specs=[pl.BlockSpec((tm, k), lambda j, i: (i, 0)), pl.BlockSpec((tm, bn), lambda j, i: (i, j))],
        out_specs=pl.BlockSpec((k, bn), lambda j, i: (0, j)),
        out_shape=jax.ShapeDtypeStruct((k, n), F32),
        compiler_params=_cparams(2),
    )(a, b)


def _acc_spec(shape, index):
    return pl.BlockSpec(shape, lambda *_: index, pipeline_mode=pl.Buffered(1))


def _out_bwd(do1, wout, merged, ya, yb, z, h1):
    t = do1.shape[0]
    tm = _tile_big(t)

    def body(do1_ref, wo_ref, mg_ref, ya_ref, yb_ref, ga_ref, gb_ref, h1_ref,
             dya_ref, dyb_ref, dz_ref, dwo_ref, dwin_ref):
        @pl.when(pl.program_id(0) == 0)
        def _():
            dwo_ref[...] = jnp.zeros_like(dwo_ref)
            dwin_ref[...] = jnp.zeros_like(dwin_ref)

        do1v = do1_ref[...]
        dwo_ref[...] += _dot_tn(mg_ref[...], do1v)
        dm = _dot_nt(do1v, wo_ref[...])
        sa = _sigmoid(ga_ref[...].astype(F32))
        sb = _sigmoid(gb_ref[...].astype(F32))
        dya_ref[...] = (dm * sa).astype(BF16)
        dyb_ref[...] = (dm * sb).astype(BF16)
        dga = (dm * ya_ref[...].astype(F32) * sa * (1.0 - sa)).astype(BF16)
        dgb = (dm * yb_ref[...].astype(F32) * sb * (1.0 - sb)).astype(BF16)
        dz_ref[:, 0:D] = dga
        dz_ref[:, D:2 * D] = dgb
        h1v = h1_ref[...]
        dwin_ref[:, 0:D] += _dot_tn(h1v, dga)
        dwin_ref[:, D:2 * D] += _dot_tn(h1v, dgb)

    tile = pl.BlockSpec((tm, D), lambda i: (i, 0))
    bshape = jax.ShapeDtypeStruct((t, D), BF16)
    return pl.pallas_call(
        body, name="out_bwd", grid=(t // tm,),
        in_specs=[tile, _const_spec((D, D), True), tile, tile, tile,
                  pl.BlockSpec((tm, D), lambda i: (i, 4)), pl.BlockSpec((tm, D), lambda i: (i, 5)), tile],
        out_specs=[tile, tile, pl.BlockSpec((tm, 2 * D), lambda i: (i, 2)), _acc_spec((D, D), (0, 0)),
                   _acc_spec((D, 2 * D), (0, 2))],
        out_shape=[bshape, bshape, jax.ShapeDtypeStruct((t, NCOL_IN), BF16), jax.ShapeDtypeStruct((D, D), F32),
                   jax.ShapeDtypeStruct((D, NCOL_IN), F32)],
        compiler_params=_cparams(1),
    )(do1, wout, merged, ya, yb, z, z, h1)


def _rnn_bwd(dya, ya_pre, wba, h1, z, saved, h, dz, dwin, cw, wa, wx, lam):
    t = z.shape[0]
    tm = _tile_seq(t)
    nt = t // tm
    ngrp = tm // SUBLANES
    hpt = tm // HALO

    def body(dya_ref, yap_ref, wba_ref, h1_ref, xr_ref, xc_ref, ra_ref, ia_ref, gg_ref, hg_ref, h_ref, hp_ref,
             dz_any, dwin_any, cw_ref, wa_ref, wx_ref, lam_ref,
             dz_ref, dwin_ref, dwba_ref, dcw_ref, dcb_ref, dwa_ref, dba_ref, dwx_ref, dbx_ref, dlam_ref,
             a_first, g_first, dxc_first, b_scr, d_scr, g_scr):
        del dz_any, dwin_any
        i = pl.program_id(0)

        @pl.when(i == 0)
        def _():
            for r in (dwin_ref, dwba_ref, dcw_ref, dcb_ref, dwa_ref, dba_ref, dwx_ref, dbx_ref, dlam_ref,
                      a_first, g_first, dxc_first):
                r[...] = jnp.zeros_like(r)

        dya_v = dya_ref[...]
        dwba_ref[...] += _dot_tn(yap_ref[...], dya_v)
        dyap_v = _dot_nt(dya_v, wba_ref[...])
        h1v = h1_ref[...]

        first_tile = i == nt - 1
        xc = xc_ref[...].astype(F32)
        ra = ra_ref[...].astype(F32)
        ia = ia_ref[...].astype(F32)
        lam_v = lam_ref[...]
        ls = _log_sigmoid(lam_v)
        la = LRU_C * ra * ls
        a = jnp.exp(la)
        mult = jnp.sqrt(-jnp.tanh(la) * (1.0 + a * a))
        hprev8 = jnp.where(first_tile, 0.0, hp_ref[...][HALO - SUBLANES:])
        h_prev = _shift_down(h_ref[...], 1, hprev8)
        dgr = (dyap_v * hg_ref[...].astype(F32)).astype(BF16)
        dz_ref[:, D:2 * D] = dgr
        dwin_ref[:, D:2 * D] += _dot_tn(h1v, dgr)

        b_scr[...] = _shift_up(a, 1, a_first[...])
        d_scr[...] = dyap_v * gg_ref[...].astype(F32)
        row = _row_iota(D)

        def grp(jj, carry):
            r0 = pl.multiple_of((ngrp - 1 - jj) * SUBLANES, SUBLANES)
            bv = b_scr[pl.ds(r0, SUBLANES), :]
            dv = d_scr[pl.ds(r0, SUBLANES), :]
            for d in (1, 2, 4):
                m = row < SUBLANES - d
                dv = jnp.where(m, dv + bv * pltpu.roll(dv, SUBLANES - d, 0), dv)
                bv = jnp.where(m, bv * pltpu.roll(bv, SUBLANES - d, 0), bv)
            gv = dv + bv * carry
            g_scr[pl.ds(r0, SUBLANES), :] = gv
            return gv[0:1, :]

        lax.fori_loop(0, ngrp, grp, g_first[0:1, :])
        g = g_scr[...]
        a_first[...] = a[:SUBLANES]
        g_first[...] = g[:SUBLANES]

        da = g * h_prev
        gx = g * xc
        dmult = gx * ia
        dia = gx * mult
        dxc = g * (mult * ia)
        dla = da * a - dmult * (a * a) / mult
        dra = dla * (LRU_C * ls)
        dlam_ref[...] += _colsum(dla * ra) * (LRU_C * _sigmoid(-lam_v))
        dpa = dra * ra * (1.0 - ra)
        dpx = dia * ia * (1.0 - ia)
        dba_ref[...] += _colsum(dpa)
        dbx_ref[...] += _colsum(dpx)
        dpab = dpa.astype(BF16)
        dpxb = dpx.astype(BF16)
        xcb = xc_ref[...]
        for hd in range(NH):
            sl = slice(hd * HD, (hd + 1) * HD)
            dwa_ref[hd] += _dot_tn(xcb[:, sl], dpab[:, sl])
            dwx_ref[hd] += _dot_tn(xcb[:, sl], dpxb[:, sl])
        dxc = dxc + _heads_nt(dpab, wa_ref) + _heads_nt(dpxb, wx_ref)

        nxt = dxc_first[...]
        taps = (_shift_up(dxc, 3, nxt), _shift_up(dxc, 2, nxt), _shift_up(dxc, 1, nxt), dxc)
        dxr = cw_ref[0:1, :] * taps[0]
        for k in range(1, 4):
            dxr = dxr + cw_ref[k:k + 1, :] * taps[k]
        dxrb = dxr.astype(BF16)
        dz_ref[:, 0:D] = dxrb
        dwin_ref[:, 0:D] += _dot_tn(h1v, dxrb)
        dxc_first[...] = dxc[:SUBLANES]
        dcb_ref[...] += _colsum(dxc)
        xr = xr_ref[...].astype(F32)
        for k in range(4):
            dcw_ref[k:k + 1, :] += _colsum(taps[k] * xr)

    def rev(col):
        return lambda i: (nt - 1 - i, col)

    vec = _const_spec((1, D))
    wspec = _const_spec((NH, HD, HD))
    vshape = jax.ShapeDtypeStruct((1, D), F32)
    wshape = jax.ShapeDtypeStruct((NH, HD, HD), F32)
    any_spec = pl.BlockSpec(memory_space=pl.ANY)
    tile = pl.BlockSpec((tm, D), rev(0))
    outs = pl.pallas_call(
        body, name="rnn_bwd", grid=(nt,),
        in_specs=[tile, tile, _const_spec((D, D), True), tile, tile, tile, tile, tile, tile, tile, tile,
                  pl.BlockSpec((HALO, D), lambda i: (jnp.maximum((nt - 1 - i) * hpt - 1, 0), 0)),
                  any_spec, any_spec, _const_spec((4, D)), wspec, wspec, vec],
        out_specs=[pl.BlockSpec((tm, 2 * D), rev(0)), _acc_spec((D, 2 * D), (0, 0)), _acc_spec((D, D), (0, 0)),
                   _const_spec((4, D)), vec, wspec, vec, wspec, vec, vec],
        out_shape=[jax.ShapeDtypeStruct((t, NCOL_IN), BF16), jax.ShapeDtypeStruct((D, NCOL_IN), F32),
                   jax.ShapeDtypeStruct((D, D), F32), jax.ShapeDtypeStruct((4, D), F32), vshape,
                   wshape, vshape, wshape, vshape, vshape],
        scratch_shapes=[pltpu.VMEM((SUBLANES, D), F32), pltpu.VMEM((SUBLANES, D), F32), pltpu.VMEM((SUBLANES, D), F32),
                        pltpu.VMEM((tm, D), F32), pltpu.VMEM((tm, D), F32), pltpu.VMEM((tm, D), F32)],
        input_output_aliases={12: 0, 13: 1},
        compiler_params=_cparams(1),
    )(dya, ya_pre, wba, h1, z, *saved, h, h, dz, dwin, cw, wa, wx, lam)
    return outs


def _sgu_bwd(dyb, yb_pre, wbb, h1, saved, dz, dwin, lng, lnb, wmt, mask):
    t = dyb.shape[0]
    tm = _tile_big(t)

    def body(dyb_ref, ybp_ref, wbb_ref, h1_ref, gu_ref, mg_ref, vh_ref, gpv_ref, rstd_ref, dz_any, dwin_any,
             lng_ref, lnb_ref, wmt_ref, mask_ref,
             dz_ref, dwin_ref, dwbb_ref, dws_ref, dbst_ref, dlng_ref, dlnb_ref):
        del dz_any, dwin_any

        @pl.when(pl.program_id(0) == 0)
        def _():
            for r in (dwin_ref, dwbb_ref, dws_ref, dbst_ref, dlng_ref, dlnb_ref):
                r[...] = jnp.zeros_like(r)

        lng_v = lng_ref[...]
        vhat = vh_ref[...].astype(F32)
        vb = (vhat * lng_v + lnb_ref[...]).astype(BF16)
        rstd = rstd_ref[...]
        dyb_v = dyb_ref[...]
        dwbb_ref[...] += _dot_tn(ybp_ref[...], dyb_v)
        dyb = _dot_nt(dyb_v, wbb_ref[...])
        h1v = h1_ref[...]
        dzu = (dyb * mg_ref[...].astype(F32)).astype(BF16)
        dz_ref[:, 0:D] = dzu
        dwin_ref[:, 0:D] += _dot_tn(h1v, dzu)
        dmix = dyb * gu_ref[...].astype(F32)
        dmb = dmix.astype(BF16)
        rows = []
        lane = lax.broadcasted_iota(jnp.int32, (HD, NH), 1)
        dbst = jnp.zeros((HD, NH), F32)
        for b0 in range(0, tm, HD):
            cols = []
            for g in range(NH):
                sl = slice(g * HD, (g + 1) * HD)
                dmg = dmb[b0:b0 + HD, sl]
                dws_ref[g] += _dot_nt(dmg, vb[b0:b0 + HD, sl]) * mask_ref[...]
                cols.append(jnp.dot(wmt_ref[g], dmg, preferred_element_type=F32))
                dbst = dbst + jnp.where(lane == g, jnp.sum(dmix[b0:b0 + HD, sl], axis=1, keepdims=True), 0.0)
            rows.append(jnp.concatenate(cols, axis=1))
        dbst_ref[...] += dbst
        dvln = jnp.concatenate(rows, axis=0) if len(rows) > 1 else rows[0]
        dlng_ref[...] += _colsum(dvln * vhat)
        dlnb_ref[...] += _colsum(dvln)
        dvh = dvln * lng_v
        dgv = rstd * (dvh - jnp.mean(dvh, axis=-1, keepdims=True)
                      - vhat * jnp.mean(dvh * vhat, axis=-1, keepdims=True))
        dzv = (dgv * gpv_ref[...].astype(F32)).astype(BF16)
        dz_ref[:, D:2 * D] = dzv
        dwin_ref[:, D:2 * D] += _dot_tn(h1v, dzv)

    vec = _const_spec((1, D))
    wspec = _const_spec((NH, HD, HD))
    vshape = jax.ShapeDtypeStruct((1, D), F32)
    tile = pl.BlockSpec((tm, D), lambda i: (i, 0))
    any_spec = pl.BlockSpec(memory_space=pl.ANY)
    return pl.pallas_call(
        body, name="sgu_bwd", grid=(t // tm,),
        in_specs=[tile, tile, _const_spec((D, D), True), tile, tile, tile, tile, tile,
                  pl.BlockSpec((tm, 1), lambda i: (i, 0)), any_spec, any_spec,
                  vec, vec, wspec, _const_spec((HD, HD))],
        out_specs=[pl.BlockSpec((tm, 2 * D), lambda i: (i, 1)), _acc_spec((D, 2 * D), (0, 1)), _acc_spec((D, D), (0, 0)),
                   wspec, _const_spec((HD, NH)), vec, vec],
        out_shape=[jax.ShapeDtypeStruct((t, NCOL_IN), BF16), jax.ShapeDtypeStruct((D, NCOL_IN), F32),
                   jax.ShapeDtypeStruct((D, D), F32), jax.ShapeDtypeStruct((NH, HD, HD), F32),
                   jax.ShapeDtypeStruct((HD, NH), F32), vshape, vshape],
        input_output_aliases={9: 0, 10: 1},
        compiler_params=_cparams(1),
    )(dyb, yb_pre, wbb, h1, *saved, dz, dwin, lng, lnb, wmt, mask)


def _in_bwd(dz, win, x, dx2, g, scale1):
    t = x.shape[0]
    tm = _tile_big(t)

    def body(dz_ref, w_ref, x_ref, dx2_ref, g_ref, sc_ref, dx_ref, dsh_ref, dsc_ref, dg_ref):
        @pl.when(pl.program_id(0) == 0)
        def _():
            for r in (dsh_ref, dsc_ref, dg_ref):
                r[...] = jnp.zeros_like(r)

        dh = jnp.zeros((tm, D), F32)
        for c0 in range(0, NCOL_IN, D):
            dh = dh + _dot_nt(dz_ref[:, c0:c0 + D], w_ref[:, c0:c0 + D])
        dxn, dsh, dsc, dg = _modnorm_bwd(dh, x_ref[...], g_ref[...], sc_ref[...])
        dx_ref[...] = dx2_ref[...] + dxn
        dsh_ref[...] += dsh
        dsc_ref[...] += dsc
        dg_ref[...] += dg

    tile = pl.BlockSpec((tm, D), lambda i: (i, 0))
    vec = _const_spec((1, D))
    vshape = jax.ShapeDtypeStruct((1, D), F32)
    return pl.pallas_call(
        body, name="in_bwd", grid=(t // tm,),
        in_specs=[pl.BlockSpec((tm, NCOL_IN), lambda i: (i, 0)), _const_spec((D, NCOL_IN), True), tile, tile, vec, vec],
        out_specs=[tile, vec, vec, vec],
        out_shape=[jax.ShapeDtypeStruct((t, D), F32), vshape, vshape, vshape],
        compiler_params=_cparams(1),
    )(dz, win, x, dx2, g, scale1)


def _mod_cols(c_all, w_ada, b_cols):
    nb, cols = c_all.shape[0], w_ada.shape[1]

    def body(c_ref, w_ref, b_ref, o_ref):
        cv = c_ref[...]
        ca = (cv * _sigmoid(cv)).astype(BF16)
        o_ref[...] = jnp.dot(ca, w_ref[...].astype(BF16), preferred_element_type=F32) + b_ref[...]

    return pl.pallas_call(body, name="mod_cols", out_shape=jax.ShapeDtypeStruct((nb, cols), F32))(c_all, w_ada, b_cols)


def _ada_grad(c_all, dmod_cols):
    cols = dmod_cols.shape[1]

    def body(c_ref, d_ref, o_ref):
        cv = c_ref[...]
        ca = (cv * _sigmoid(cv)).astype(BF16)
        o_ref[...] = _dot_tn(ca, d_ref[...].astype(BF16))

    return pl.pallas_call(body, name="ada_grad", out_shape=jax.ShapeDtypeStruct((D, cols), F32))(c_all, dmod_cols)


def _adamw_update(w, m, v, g):
    bc1 = 1.0 - ADAM_B1 ** ADAM_STEP
    bc2 = 1.0 - ADAM_B2 ** ADAM_STEP
    mn = ADAM_B1 * m + (1.0 - ADAM_B1) * g
    vn = ADAM_B2 * v + (1.0 - ADAM_B2) * (g * g)
    return -ADAM_LR * ((mn / bc1) / (jnp.sqrt(vn / bc2) + ADAM_EPS) + ADAM_WD * w), mn, vn


def _adamw_group(names, ws, ms, vs, packs, name):
    n = len(names)
    starts, r0 = [], 0
    for w in ws:
        starts.append(r0)
        r0 += _pack_rows(w.shape)

    def body(*refs):
        w_refs, m_refs, v_refs, p_ref = refs[:n], refs[n:2 * n], refs[2 * n:3 * n], refs[3 * n]
        outs = refs[3 * n + 1:]
        for k in range(n):
            rows = _pack_rows(ws[k].shape)
            g = None
            for dev in range(N_DEV):
                if ws[k].shape[0] == 1:
                    term = jnp.concatenate(
                        [p_ref[dev, starts[k] + r:starts[k] + r + 1, :] for r in range(rows)], axis=1)
                else:
                    term = p_ref[dev, starts[k]:starts[k] + rows, :]
                g = term if g is None else g + term
            delta, mn, vn = _adamw_update(w_refs[k][...], m_refs[k][...], v_refs[k][...], g)
            for o_ref, val in zip(outs[4 * k:4 * k + 4], (g, delta, mn, vn)):
                o_ref[...] = val

    shapes = [jax.ShapeDtypeStruct(w.shape, F32) for w in ws for _ in range(4)]
    outs = pl.pallas_call(body, name=name, out_shape=shapes,
                          compiler_params=pltpu.CompilerParams(vmem_limit_bytes=VMEM_LIMIT))(*ws, *ms, *vs, packs)
    return {nm: tuple(outs[4 * k:4 * k + 4]) for k, nm in enumerate(names)}


def _adamw(w, m, v, parts, name):
    rows, cols = w.shape
    tr = _row_tile(rows, cols)
    stacked = [p.ndim == 3 for p in parts]

    def body(*refs):
        w_ref, m_ref, v_ref = refs[:3]
        p_refs = refs[3:3 + len(parts)]
        g_ref, d_ref, mo_ref, vo_ref = refs[3 + len(parts):]
        g = None
        for p_ref, st in zip(p_refs, stacked):
            terms = [p_ref[k].astype(F32) for k in range(p_ref.shape[0])] if st else [p_ref[...].astype(F32)]
            for term in terms:
                g = term if g is None else g + term
        delta, mn, vn = _adamw_update(w_ref[...], m_ref[...], v_ref[...], g)
        g_ref[...] = g
        mo_ref[...] = mn
        vo_ref[...] = vn
        d_ref[...] = delta

    tile = pl.BlockSpec((tr, cols), lambda i: (i, 0))
    p_specs = [pl.BlockSpec((p.shape[0], tr, cols), lambda i: (0, i, 0)) if st else tile for p, st in zip(parts, stacked)]
    shp = jax.ShapeDtypeStruct((rows, cols), F32)
    return pl.pallas_call(
        body, name=name, grid=(rows // tr,),
        in_specs=[tile, tile, tile] + p_specs, out_specs=[tile] * 4, out_shape=[shp] * 4,
        compiler_params=_cparams(1),
    )(w, m, v, *parts)


def _mesh_pos():
    return lax.axis_index("x"), lax.axis_index("y"), lax.axis_index("c")


def _other_chips(x, y):
    return [(1 - x, y), (x, 1 - y), (1 - x, 1 - y)]


def _block_of(ref, axis, index, size):
    if axis == 0:
        return ref.at[index]
    return ref.at[:, pl.ds(pl.multiple_of(index * size, 128), size)]


def _all_gather(shards, axes, name):
    n = len(shards)
    per = 7

    def body(*refs):
        ins, outs, done = refs[:n], refs[n:2 * n], refs[2 * n]
        send_sems, recv_sems, local_sems = refs[2 * n + 1:]
        x, y, c = _mesh_pos()
        me, sibling = (x, y, c), (x, y, 1 - c)
        chips = _other_chips(x, y)

        def rows(a, pos):
            return _block_of(outs[a], axes[a], 4 * pos[0] + 2 * pos[1] + pos[2], shards[a].shape[-1])

        def copy(a, k, block, to, src=None):
            return pltpu.make_async_remote_copy(
                src_ref=rows(a, block) if src is None else src, dst_ref=rows(a, block),
                send_sem=send_sems.at[a * per + k], recv_sem=recv_sems.at[a * per + k],
                device_id=to, device_id_type=MESH_IDS)

        mine = [pltpu.make_async_copy(ins[a], rows(a, me), local_sems.at[a]) for a in range(n)]
        for cp in mine:
            cp.start()
        first = []
        for a in range(n):
            first.append(copy(a, 0, me, sibling, src=ins[a]))
            first += [copy(a, 1 + j, me, (*chip, c), src=ins[a]) for j, chip in enumerate(chips)]
        for cp in first:
            cp.start()
        passed = []
        for j, chip in enumerate(chips):
            for a in range(n):
                copy(a, 1 + j, (*chip, c), me).wait_recv()
                fwd = copy(a, 4 + j, (*chip, c), sibling)
                fwd.start()
                passed.append(fwd)
        for a in range(n):
            copy(a, 0, sibling, me).wait_recv()
            for j, chip in enumerate(chips):
                copy(a, 4 + j, (*chip, 1 - c), me).wait_recv()
        for cp in first + passed:
            cp.wait_send()
        for cp in mine:
            cp.wait()
        done[...] = jnp.zeros_like(done)

    def full_shape(s, ax):
        return (N_DEV,) + s.shape if ax == 0 else s.shape[:-1] + (N_DEV * s.shape[-1],)

    any_spec = pl.BlockSpec(memory_space=pl.ANY)
    outs = pl.pallas_call(
        body, name=name,
        in_specs=[any_spec] * n, out_specs=[any_spec] * n + [pl.BlockSpec(memory_space=pltpu.VMEM)],
        out_shape=[jax.ShapeDtypeStruct(full_shape(s, ax), s.dtype) for s, ax in zip(shards, axes)]
        + [jax.ShapeDtypeStruct((SUBLANES, LANES), F32)],
        scratch_shapes=[pltpu.SemaphoreType.DMA((n * per,)), pltpu.SemaphoreType.DMA((n * per,)),
                        pltpu.SemaphoreType.DMA((n,))],
    )(*shards)
    return outs[:n], outs[n]


def _chip_blocks(x, y):
    return [(x, y)] + _other_chips(x, y)


def _sibling_reduce(gs, axis, name):
    g0, n = gs[0], len(gs)
    rows, cols = (g0.shape[1], g0.shape[2]) if axis == 0 else (g0.shape[0], g0.shape[1] // N_DEV)
    chunk = math.gcd(rows, 64)

    def body(*refs):
        g_refs, own_refs, pay_refs = refs[:n], refs[n:2 * n], refs[2 * n:3 * n]
        send_buf, keep_buf, recv_buf, send_sems, recv_sems, stage_sems, keep_sems = refs[3 * n:]
        x, y, c = _mesh_pos()
        sibling = (x, y, 1 - c)
        chips = _chip_blocks(x, y)
        stage, keep, push = [], [], []
        for a in range(n):
            for j, (px, py) in enumerate(chips):
                s = 4 * a + j
                theirs = _block_of(g_refs[a], axis, 4 * px + 2 * py + (1 - c), cols)
                ours = _block_of(g_refs[a], axis, 4 * px + 2 * py + c, cols)
                stage.append(pltpu.make_async_copy(theirs, send_buf.at[s], stage_sems.at[s]))
                keep.append(pltpu.make_async_copy(ours, keep_buf.at[s], keep_sems.at[s]))
                push.append(pltpu.make_async_remote_copy(
                    src_ref=send_buf.at[s], dst_ref=recv_buf.at[s], send_sem=send_sems.at[s],
                    recv_sem=recv_sems.at[s], device_id=sibling, device_id_type=MESH_IDS))
        for cp in stage + keep:
            cp.start()
        for s in range(4 * n):
            stage[s].wait()
            push[s].start()
        for s in range(4 * n):
            push[s].wait_recv()
            keep[s].wait()
            a, j = divmod(s, 4)
            dst = own_refs[a] if j == 0 else pay_refs[a].at[j - 1]

            def add(r, carry, s=s, dst=dst):
                sl = pl.ds(pl.multiple_of(r * chunk, chunk), chunk)
                dst[sl, :] = (keep_buf[s, sl, :] + recv_buf[s, sl, :]).astype(dst.dtype)
                return carry

            lax.fori_loop(0, rows // chunk, add, 0)
        for cp in push:
            cp.wait_send()

    vmem = pl.BlockSpec(memory_space=pltpu.VMEM)
    buf = pltpu.VMEM((4 * n, rows, cols), F32)
    sems = pltpu.SemaphoreType.DMA((4 * n,))
    outs = pl.pallas_call(
        body, name=name,
        in_specs=[pl.BlockSpec(memory_space=pl.ANY)] * n, out_specs=[vmem] * (2 * n),
        out_shape=[jax.ShapeDtypeStruct((rows, cols), F32)] * n + [jax.ShapeDtypeStruct((3, rows, cols), BF16)] * n,
        scratch_shapes=[buf, buf, buf, sems, sems, sems, sems],
        compiler_params=pltpu.CompilerParams(vmem_limit_bytes=VMEM_LIMIT),
    )(*gs)
    return list(zip(outs[:n], outs[n:]))


_HBM_SPEC = pl.BlockSpec(memory_space=pltpu.HBM)
_SEM_SPEC = pl.BlockSpec(memory_space=pltpu.SEMAPHORE)
_SIDE_EFFECT = pltpu.SideEffectType.DATAFLOW_SIDE_EFFECTING


def _exchange_start(name, srcs, lands, plan, n_copies):
    nb = len(srcs) + len(lands)

    def body(*refs):
        bufs, send_sems, recv_sems, token = refs[:nb], refs[nb], refs[nb + 1], refs[-1]
        for cp in plan(bufs[:len(srcs)], bufs[len(srcs):], send_sems, recv_sems):
            cp.start()
        token[...] = jnp.zeros_like(token)

    arrays = list(srcs) + list(lands)
    outs = pl.pallas_call(
        body, name=name,
        out_shape=(pltpu.SemaphoreType.DMA((n_copies,)), pltpu.SemaphoreType.DMA((n_copies,)),
                   *[pltpu.HBM(a.shape, a.dtype) for a in arrays], jax.ShapeDtypeStruct((SUBLANES, LANES), F32)),
        in_specs=[_HBM_SPEC] * nb,
        out_specs=(_SEM_SPEC, _SEM_SPEC, *[_HBM_SPEC] * nb, pl.BlockSpec(memory_space=pltpu.VMEM)),
        input_output_aliases={k: 2 + k for k in range(nb)},
        compiler_params=pltpu.CompilerParams(has_side_effects=_SIDE_EFFECT),
    )(*[pltpu.with_memory_space_constraint(a, pltpu.HBM) for a in arrays])
    return outs[0], outs[1], outs[2:2 + len(srcs)], outs[2 + len(srcs):2 + nb], outs[-1]


def _exchange_wait(name, send_sems, recv_sems, srcs, lands, plan, after):
    nb = len(srcs) + len(lands)
    after = list(after)

    def body(*refs):
        bufs, send_ref, recv_ref = refs[:nb], refs[nb], refs[nb + 1]
        for cp in plan(bufs[:len(srcs)], bufs[len(srcs):], send_ref, recv_ref):
            cp.wait_send()
            cp.wait_recv()

    arrays = list(srcs) + list(lands)
    outs = pl.pallas_call(
        body, name=name,
        out_shape=tuple(pltpu.HBM(a.shape, a.dtype) for a in arrays),
        in_specs=[_HBM_SPEC] * nb + [_SEM_SPEC, _SEM_SPEC] + [pl.BlockSpec(memory_space=pl.ANY)] * len(after),
        out_specs=tuple([_HBM_SPEC] * nb),
        input_output_aliases={k: k for k in range(nb)},
        compiler_params=pltpu.CompilerParams(has_side_effects=_SIDE_EFFECT),
    )(*arrays, send_sems, recv_sems, *after)
    return outs[len(srcs):]


def _gather_plan(axes, sizes):
    def plan(src_refs, land_refs, send_sems, recv_sems):
        x, y, c = _mesh_pos()
        copies = []
        for a, (src, land) in enumerate(zip(src_refs, land_refs)):
            mine = _block_of(land, axes[a], 4 * x + 2 * y + c, sizes[a])
            for k in range(1, N_DEV):
                peer = (1 - x if k & 4 else x, 1 - y if k & 2 else y, 1 - c if k & 1 else c)
                idx = a * (N_DEV - 1) + k - 1
                copies.append(pltpu.make_async_remote_copy(
                    src_ref=src, dst_ref=mine, send_sem=send_sems.at[idx], recv_sem=recv_sems.at[idx],
                    device_id=peer, device_id_type=MESH_IDS))
        return copies
    return plan


def _chip_plan(src_refs, land_refs, send_sems, recv_sems):
    x, y, c = _mesh_pos()
    copies = []
    for a, (src, land) in enumerate(zip(src_refs, land_refs)):
        for j, chip in enumerate(_other_chips(x, y)):
            copies.append(pltpu.make_async_remote_copy(
                src_ref=src.at[j], dst_ref=land.at[j], send_sem=send_sems.at[3 * a + j],
                recv_sem=recv_sems.at[3 * a + j], device_id=(*chip, c), device_id_type=MESH_IDS))
    return copies


def _own_block_placed(shard, axis, me):
    if axis == 0:
        full = lax.empty((N_DEV,) + shard.shape, shard.dtype)
        return lax.dynamic_update_slice(full, shard[None], (me,) + (0,) * shard.ndim)
    rows, cols = shard.shape

    def body(me_ref, s_ref, o_ref):
        del me_ref
        o_ref[...] = s_ref[...]

    return pl.pallas_call(
        body, name="place_own_columns",
        grid_spec=pltpu.PrefetchScalarGridSpec(
            num_scalar_prefetch=1, grid=(1,),
            in_specs=[pl.BlockSpec((rows, cols), lambda i, me_ref: (0, 0))],
            out_specs=pl.BlockSpec((rows, cols), lambda i, me_ref: (0, me_ref[0]))),
        out_shape=jax.ShapeDtypeStruct((rows, N_DEV * cols), shard.dtype),
    )(jnp.reshape(me, (1,)).astype(jnp.int32), shard)


def _local_step(x, target, mod, win, late_weights, p, grads_ready=None):
    shift1, scale1, gate1, shift2, scale2, gate2 = (mod[k] for k in range(6))

    def after_token(v, token):
        return v if token is None else v + token[0:1, 0:1]
    wa, wx = p["lru_w_a"].astype(BF16), p["lru_w_x"].astype(BF16)
    mask = jnp.tril(jnp.ones((HD, HD), F32))
    wm = (p["sgu_w_s"] * mask).astype(BF16)
    wmt = jnp.swapaxes(wm, 1, 2)
    bst = jnp.transpose(p["sgu_b_s"])

    h1, z = _norm_proj(x, p["norm_mix_g"], scale1, shift1, win, "mix_proj")
    hstate, ya_pre, *rnn_saved = _rnn_fwd(
        z, p["rnn_conv_w"], p["rnn_conv_b"], wa, p["lru_b_a"], wx, p["lru_b_x"], p["lru_lambda"])
    yb_pre, *sgu_saved = _sgu_fwd(z, p["sgu_ln_g"], p["sgu_ln_b"], wm, bst)
    wba, wbb, wout = late_weights("merge", [ya_pre, yb_pre])
    x2, ya, yb, merged, o1 = _merge_fwd(ya_pre, yb_pre, z, x, gate1, wba, wbb, wout)
    wup = late_weights("ffn_up", [x2])
    h2, up_a, up_v, ff, fa, fv = _ffn_proj_mid(
        x2, p["norm_ffn_g"], scale2, shift2, wup, p["ffn_conv_w"], p["ffn_conv_b"])
    wd = late_weights("ffn_down", [ff])
    dx3, loss, d_gfin, d_gate2 = _ffn_out_loss(ff, wd, x2, target, gate2, p["norm_final_g"])

    dact, dval, d_wd, dcb_a, dcb_v = _ffn_down_bwd(dx3, gate2, ff, fa, fv, wd)
    dup, dx2, do1, d_cwf, d_shift2, d_scale2, d_gffn, d_gate1 = _ffn_up_bwd(
        dact, dval, up_a, up_v, p["ffn_conv_w"], wup, x2, dx3, p["norm_ffn_g"], scale2, o1, gate1)
    d_wup = _xt_y(h2, dup, "w_up_grad")
    ready = grads_ready if grads_ready else (lambda stage, big, small: None)
    token = ready("ffn", {"w_up": d_wup, "w_down": d_wd}, {})

    dya, dyb, dz, d_wout, d_win = _out_bwd(do1, wout, merged, ya, yb, z, h1)
    dz, d_win, d_wba, d_cw, d_cb, d_wa, d_ba, d_wx, d_bx, d_lam = _rnn_bwd(
        dya, ya_pre, wba, h1, z, rnn_saved, hstate, dz, d_win, p["rnn_conv_w"], wa, wx,
        after_token(p["lru_lambda"], token))
    small = {
        "rnn_conv_w": d_cw, "rnn_conv_b": d_cb, "lru_w_a": d_wa, "lru_b_a": d_ba, "lru_w_x": d_wx, "lru_b_x": d_bx,
        "lru_lambda": d_lam, "norm_ffn_g": d_gffn, "ffn_conv_w": d_cwf,
        "ffn_conv_b": jnp.concatenate([dcb_a, dcb_v], axis=1), "norm_final_g": d_gfin,
    }
    token = ready("rnn", {}, small)
    dz, d_win, d_wbb, d_ws, d_bst, d_lng, d_lnb = _sgu_bwd(
        dyb, yb_pre, wbb, h1, sgu_saved, dz, d_win, p["sgu_ln_g"], after_token(p["sgu_ln_b"], token), wmt, mask)
    sgu_small = {"sgu_ln_g": d_lng, "sgu_ln_b": d_lnb, "sgu_w_s": d_ws, "sgu_b_s": jnp.transpose(d_bst)}
    mixer = {"w_in": d_win, "w_out": d_wout, "w_branch_a": d_wba, "w_branch_b": d_wbb}
    token = ready("mixer", mixer, sgu_small)
    grad_x, d_shift1, d_scale1, d_gmix = _in_bwd(dz, win, x, dx2, after_token(p["norm_mix_g"], token), scale1)

    small.update(sgu_small)
    small["norm_mix_g"] = d_gmix
    dmod = jnp.stack([d_shift1, d_scale1, d_gate1, d_shift2, d_scale2, d_gate2])
    big = {"w_in": d_win, "w_up": d_wup, "w_branch_a": d_wba, "w_branch_b": d_wbb, "w_out": d_wout, "w_down": d_wd}
    return loss, grad_x, big, small, dmod


LAST_REP = ["b_ada", "norm_mix_g"]
EARLY_REP = ["rnn_conv_b", "lru_w_a", "lru_b_a", "lru_w_x", "lru_b_x", "lru_lambda", "norm_ffn_g", "ffn_conv_b",
             "norm_final_g"]
MID_REP = ["sgu_ln_g", "sgu_ln_b", "sgu_w_s", "sgu_b_s"]
COL_SHARDED = ["rnn_conv_w", "ffn_conv_w"]
SMALL_GROUPS = {"rnn": EARLY_REP + COL_SHARDED, "mixer": MID_REP, "last": LAST_REP}
REPLICATED = LAST_REP + EARLY_REP + MID_REP
SMALL_NAMES = REPLICATED + COL_SHARDED
BIG_NAMES = ["w_in", "w_up", "w_branch_a", "w_branch_b", "w_out", "w_down"]
BIG_AXES = [1, 1, 0, 0, 0, 0]
WEIGHTS = ["w_ada", "b_ada", "norm_mix_g", "w_in", "rnn_conv_w", "rnn_conv_b", "lru_w_a", "lru_b_a", "lru_w_x",
           "lru_b_x", "lru_lambda", "sgu_ln_g", "sgu_ln_b", "sgu_w_s", "sgu_b_s", "w_branch_a", "w_branch_b",
           "w_out", "norm_ffn_g", "w_up", "ffn_conv_w", "ffn_conv_b", "w_down", "norm_final_g"]


def _pack_rows(shape):
    return math.prod(shape) // LANES


def _pack(arrays):
    return jnp.concatenate([a.reshape(-1, LANES) for a in arrays], axis=0)


def kernel(x, c, w_ada, b_ada, norm_mix_g, w_in, rnn_conv_w, rnn_conv_b, lru_w_a, lru_b_a, lru_w_x, lru_b_x, lru_lambda, sgu_ln_g, sgu_ln_b, sgu_w_s, sgu_b_s, w_branch_a, w_branch_b, w_out, norm_ffn_g, w_up, ffn_conv_w, ffn_conv_b, w_down, norm_final_g, loss_target, m_w_ada, m_b_ada, m_norm_mix_g, m_w_in, m_rnn_conv_w, m_rnn_conv_b, m_lru_w_a, m_lru_b_a, m_lru_w_x, m_lru_b_x, m_lru_lambda, m_sgu_ln_g, m_sgu_ln_b, m_sgu_w_s, m_sgu_b_s, m_w_branch_a, m_w_branch_b, m_w_out, m_norm_ffn_g, m_w_up, m_ffn_conv_w, m_ffn_conv_b, m_w_down, m_norm_final_g, v_w_ada, v_b_ada, v_norm_mix_g, v_w_in, v_rnn_conv_w, v_rnn_conv_b, v_lru_w_a, v_lru_b_a, v_lru_w_x, v_lru_b_x, v_lru_lambda, v_sgu_ln_g, v_sgu_ln_b, v_sgu_w_s, v_sgu_b_s, v_w_branch_a, v_w_branch_b, v_w_out, v_norm_ffn_g, v_w_up, v_ffn_conv_w, v_ffn_conv_b, v_w_down, v_norm_final_g):
    given = dict(locals())
    me = 4 * lax.axis_index("x") + 2 * lax.axis_index("y") + lax.axis_index("c")
    ada_cols = w_ada.shape[2]
    conv_cols = {"rnn_conv_w": rnn_conv_w.shape[2], "ffn_conv_w": ffn_conv_w.shape[2]}

    (win, c_all, cw_rnn, cw_ffn), _ = _all_gather(
        [w_in[0].astype(BF16), c.reshape(1, 1, D), rnn_conv_w[0], ffn_conv_w[0]], [1, 0, 1, 1], "gather_first")
    c_all = c_all.reshape(N_DEV, D)

    b_cols = lax.dynamic_slice_in_dim(b_ada, me * ada_cols, ada_cols, axis=1)
    (mod_all,), mod_done = _all_gather(
        [_mod_cols(c_all, w_ada[0], b_cols).reshape(1, N_DEV, ada_cols)], [0], "gather_mod")
    mod_all = mod_all.reshape(N_DEV, N_DEV, ada_cols)
    mod_mine = lax.dynamic_index_in_dim(mod_all, me, axis=1, keepdims=False).reshape(6, 1, D)

    late_groups = {"merge": (["w_branch_a", "w_branch_b", "w_out"], [0, 0, 0]), "ffn_up": (["w_up"], [1]),
                   "ffn_down": (["w_down"], [0])}
    in_flight, started = {}, mod_done[0:1, 0:1]
    for stage, (names, axes) in late_groups.items():
        shards = [(given[n][0] + started).astype(BF16) for n in names]
        plan = _gather_plan(axes, [s.shape[-1] for s in shards])
        send, recv, srcs, lands, token = _exchange_start(
            "gather_start_" + stage, shards, [_own_block_placed(s, ax, me) for s, ax in zip(shards, axes)], plan,
            len(shards) * (N_DEV - 1))
        in_flight[stage] = (send, recv, srcs, lands, plan)
        started = started + token[0:1, 0:1]

    def late_weights(stage, after):
        send, recv, srcs, lands, plan = in_flight[stage]
        full = _exchange_wait("gather_wait_" + stage, send, recv, srcs, lands, plan, after)
        full = [w.reshape(-1, D) if ax == 0 else w for w, ax in zip(full, late_groups[stage][1])]
        return full if len(full) > 1 else full[0]

    mod_mine = mod_mine + started

    reducing, packing = {}, {}

    def start_pack(stage, small):
        pack = _pack([small[n] for n in SMALL_GROUPS[stage]])[None]
        plan = _gather_plan([0], [LANES])
        send, recv, srcs, lands, tok = _exchange_start(
            "small_start_" + stage, [pack], [_own_block_placed(pack, 0, me)], plan, N_DEV - 1)
        packing[stage] = (send, recv, srcs, lands, plan)
        return tok

    def grads_ready(stage, grads, small):
        tokens = [start_pack(stage, small)] if small else []
        if grads:
            tokens.append(start_reduce(stage, grads))
        return sum(tokens[1:], tokens[0])

    def start_reduce(stage, grads):
        names = [n for n in BIG_NAMES if n in grads]
        blocked = {}
        for n in names:
            ax = BIG_AXES[BIG_NAMES.index(n)]
            g = grads[n] if ax == 1 else grads[n].reshape(N_DEV, grads[n].shape[0] // N_DEV, grads[n].shape[1])
            blocked.setdefault((ax, g.shape), []).append((n, g))
        sums = {}
        for (ax, _), group in blocked.items():
            reduced = _sibling_reduce([g for _, g in group], ax, "reduce_sibling_" + "_".join(n for n, _ in group))
            sums.update({n: r for (n, _), r in zip(group, reduced)})
        sums = [sums[n] for n in names]
        pays = [pay for _, pay in sums]
        send, recv, srcs, lands, tok = _exchange_start(
            "reduce_start_" + stage, pays, [lax.empty(p_.shape, p_.dtype) for p_ in pays], _chip_plan, 3 * len(pays))
        reducing[stage] = (names, [own for own, _ in sums], send, recv, srcs, lands)
        return tok

    p = {n: given[n][0] for n in REPLICATED if n not in ("b_ada", "norm_final_g")}
    p = {n: (a.reshape(1, -1) if a.ndim == 1 else a) for n, a in p.items()}
    p["rnn_conv_w"], p["ffn_conv_w"] = cw_rnn, cw_ffn
    p["norm_final_g"] = norm_final_g.reshape(1, D)
    loss, grad_x, _, small, dmod = _local_step(x[0], loss_target[0], mod_mine, win, late_weights, p, grads_ready)

    small["b_ada"] = dmod.reshape(1, 6 * D)
    rows_of = {n: _pack_rows(small[n].shape) for n in SMALL_NAMES}
    (last,), _ = _all_gather([_pack([small[n] for n in LAST_REP] + [loss])[None]], [0], "gather_small")
    gathered = {"last": last}
    for stage, (send, recv, srcs, lands, plan) in packing.items():
        (gathered[stage],) = _exchange_wait("small_wait_" + stage, send, recv, srcs, lands, plan, [grad_x])
    gathered = {k: v.reshape(N_DEV, -1, LANES) for k, v in gathered.items()}

    out = {}
    for stage, (names, owns, send, recv, srcs, lands) in reducing.items():
        landed = _exchange_wait("reduce_wait_" + stage, send, recv, srcs, lands, _chip_plan, [last])
        for n, own, got in zip(names, owns, landed):
            out[n] = _adamw(given[n][0], given["m_" + n][0], given["v_" + n][0], [own, got], "adamw_" + n)

    dmod_all = gathered["last"][:, :rows_of["b_ada"]].reshape(N_DEV, 6 * D)
    dmod_cols = lax.dynamic_slice_in_dim(dmod_all, me * ada_cols, ada_cols, axis=1)
    out["w_ada"] = _adamw(w_ada[0], m_w_ada[0], v_w_ada[0], [_ada_grad(c_all, dmod_cols)], "adamw_w_ada")

    def rows_form(a):
        return a.reshape(1, -1) if a.size // a.shape[-1] == 1 or a.ndim == 1 else a.reshape(-1, LANES)

    for stage, names in (("last", LAST_REP), ("rnn", EARLY_REP), ("mixer", MID_REP)):
        out.update(_adamw_group(names, *[[rows_form(given[pre + n]) for n in names] for pre in ("", "m_", "v_")],
                                gathered[stage], "adamw_small_" + stage))

    row0 = sum(rows_of[n] for n in EARLY_REP)
    for n in COL_SHARDED:
        full = gathered["rnn"][:, row0:row0 + rows_of[n]].reshape(N_DEV, small[n].shape[0], small[n].shape[1])
        mine = lax.dynamic_slice_in_dim(full, me * conv_cols[n], conv_cols[n], axis=2)
        out[n] = _adamw(given[n][0], given["m_" + n][0], given["v_" + n][0], [mine], "adamw_" + n)
        row0 += rows_of[n]

    loss_row = sum(rows_of[n] for n in LAST_REP)
    total = gathered["last"][0, loss_row, 0]
    for dev in range(1, N_DEV):
        total = total + gathered["last"][dev, loss_row, 0]
    results = [total, grad_x[None]]
    for kind in range(4):
        results += [out[n][kind].reshape(given[n].shape) for n in WEIGHTS]
    return tuple(results)
```

```python
import math

import jax
import jax.numpy as jnp
from jax import lax
from jax.experimental import pallas as pl
from jax.experimental.pallas import tpu as pltpu

F32 = jnp.float32
BF16 = jnp.bfloat16
MESH_IDS = pl.DeviceIdType.MESH

D = 1024
NH = 8
HD = 128
NCOL_IN = 6 * D
DFF = 3 * D
N_DEV = 8
EPS = 1e-6
LRU_C = 8.0
ADAM_LR, ADAM_B1, ADAM_B2, ADAM_EPS, ADAM_WD, ADAM_STEP = 0.001, 0.9, 0.999, 1e-08, 0.01, 10

SUBLANES = 8
LANES = 128
HALO = 16
VMEM_LIMIT = 56 * 1024 * 1024
GELU_K = math.sqrt(2.0 / math.pi)
GELU_C = 0.044715


def _cparams(n_axes):
    return pltpu.CompilerParams(dimension_semantics=("arbitrary",) * n_axes, vmem_limit_bytes=VMEM_LIMIT)


def _const_spec(shape, single_buffer=False):
    nd = len(shape)
    if single_buffer:
        return pl.BlockSpec(shape, lambda *_: (0,) * nd, pipeline_mode=pl.Buffered(1))
    return pl.BlockSpec(shape, lambda *_: (0,) * nd)


def _tile_big(t):
    return min(512, t)


def _tile_seq(t):
    return min(256, t)


def _row_tile(rows, cols):
    cap = max(SUBLANES, (2 * 1024 * 1024) // (4 * cols) // SUBLANES * SUBLANES)
    if rows <= cap:
        return rows
    return next(tr for tr in range(cap, 0, -SUBLANES) if rows % tr == 0)


def _gelu_t(x):
    x2 = x * x
    t = jnp.tanh(x * (GELU_K + (GELU_K * GELU_C) * x2))
    hx = 0.5 * x
    return hx + hx * t, (x2, hx, t)


def _gelu_grad(shared):
    x2, hx, t = shared
    return (0.5 + 0.5 * t) + (hx * (1.0 - t * t)) * (GELU_K + (3.0 * GELU_K * GELU_C) * x2)


def _sigmoid(x):
    return 1.0 / (1.0 + jnp.exp(-x))


def _log_sigmoid(x):
    return -(jnp.maximum(-x, 0.0) + jnp.log1p(jnp.exp(-jnp.abs(x))))


def _row_iota(cols):
    return lax.broadcasted_iota(jnp.int32, (SUBLANES, cols), 0)


def _shift_down(x, k, prev8):
    if k == 0:
        return x
    r = pltpu.roll(x, k, 0)
    p = pltpu.roll(prev8, k, 0)
    head = jnp.where(_row_iota(x.shape[1]) < k, p, r[:SUBLANES])
    return jnp.concatenate([head, r[SUBLANES:]], axis=0)


def _shift_up(x, k, next8):
    if k == 0:
        return x
    n = x.shape[0]
    r = pltpu.roll(x, n - k, 0)
    q = pltpu.roll(next8, SUBLANES - k, 0)
    tail = jnp.where(_row_iota(x.shape[1]) >= SUBLANES - k, q, r[n - SUBLANES:])
    return jnp.concatenate([r[:n - SUBLANES], tail], axis=0)


def _heads_nn(x_bf, w_ref):
    return jnp.concatenate(
        [jnp.dot(x_bf[:, h * HD:(h + 1) * HD], w_ref[h], preferred_element_type=F32) for h in range(NH)], axis=1)


def _heads_nt(x_bf, w_ref):
    return jnp.concatenate(
        [lax.dot_general(x_bf[:, h * HD:(h + 1) * HD], w_ref[h], (((1,), (1,)), ((), ())), preferred_element_type=F32)
         for h in range(NH)], axis=1)


def _dot_nt(a, b):
    return lax.dot_general(a, b, (((1,), (1,)), ((), ())), preferred_element_type=F32)


def _dot_tn(a, b):
    return lax.dot_general(a, b, (((0,), (0,)), ((), ())), preferred_element_type=F32)


def _colsum(x):
    return jnp.sum(x, axis=0, keepdims=True)


def _prev_halo_map(tm, col):
    return lambda i, *_: (jnp.maximum(i * (tm // HALO) - 1, 0), col)


def _norm_proj(x, g, scale, shift, w, name):
    t, n = x.shape[0], w.shape[1]
    tm = _tile_big(t)

    def body(x_ref, g_ref, sc_ref, sh_ref, w_ref, h_ref, z_ref):
        xv = x_ref[...]
        r = lax.rsqrt(jnp.mean(xv * xv, axis=-1, keepdims=True) + EPS)
        hb = ((xv * r * g_ref[...]) * (1.0 + sc_ref[...]) + sh_ref[...]).astype(BF16)
        h_ref[...] = hb
        for c0 in range(0, n, D):
            z_ref[:, c0:c0 + D] = jnp.dot(hb, w_ref[:, c0:c0 + D], preferred_element_type=F32).astype(BF16)

    vec = _const_spec((1, D))
    return pl.pallas_call(
        body, name=name, grid=(t // tm,),
        in_specs=[pl.BlockSpec((tm, D), lambda i: (i, 0)), vec, vec, vec, _const_spec((D, n), True)],
        out_specs=[pl.BlockSpec((tm, D), lambda i: (i, 0)), pl.BlockSpec((tm, n), lambda i: (i, 0))],
        out_shape=[jax.ShapeDtypeStruct((t, D), BF16), jax.ShapeDtypeStruct((t, n), BF16)],
        compiler_params=_cparams(1),
    )(x, g, scale, shift, w)


def _lru_gates(xc, wa_ref, ba, wx_ref, bx, ls):
    xb = xc.astype(BF16)
    ra = _sigmoid(_heads_nn(xb, wa_ref) + ba)
    ia = _sigmoid(_heads_nn(xb, wx_ref) + bx)
    la = LRU_C * ra * ls
    a = jnp.exp(la)
    mult = jnp.sqrt(-jnp.tanh(la) * (1.0 + a * a))
    return ra, ia, a, mult


def _conv4(xr, prev8, cw_ref, cb):
    return (cb + cw_ref[3:4, :] * xr + cw_ref[2:3, :] * _shift_down(xr, 1, prev8)
            + cw_ref[1:2, :] * _shift_down(xr, 2, prev8) + cw_ref[0:1, :] * _shift_down(xr, 3, prev8))


def _rnn_fwd(z, cw, cb, wa, ba, wx, bx, lam):
    t = z.shape[0]
    tm = _tile_seq(t)
    ngrp = tm // SUBLANES

    def body(xr_ref, xp_ref, gr_ref, cw_ref, cb_ref, wa_ref, ba_ref, wx_ref, bx_ref, lam_ref,
             h_ref, ya_ref, xc_ref, ra_ref, ia_ref, gg_ref, hg_ref, carry_ref, a_scr, u_scr):
        i = pl.program_id(0)

        @pl.when(i == 0)
        def _():
            carry_ref[...] = jnp.zeros_like(carry_ref)

        xr = xr_ref[...].astype(F32)
        prev8 = jnp.where(i == 0, 0.0, xp_ref[...].astype(F32)[HALO - SUBLANES:])
        xc = _conv4(xr, prev8, cw_ref, cb_ref[...])
        ra, ia, a, mult = _lru_gates(xc, wa_ref, ba_ref[...], wx_ref, bx_ref[...], _log_sigmoid(lam_ref[...]))
        xc_ref[...] = xc.astype(BF16)
        ra_ref[...] = ra.astype(BF16)
        ia_ref[...] = ia.astype(BF16)
        a_scr[...] = a
        u_scr[...] = mult * (ia * xc)
        row = _row_iota(D)

        def grp(j, carry):
            r0 = pl.multiple_of(j * SUBLANES, SUBLANES)
            av = a_scr[pl.ds(r0, SUBLANES), :]
            uv = u_scr[pl.ds(r0, SUBLANES), :]
            for d in (1, 2, 4):
                m = row >= d
                uv = jnp.where(m, av * pltpu.roll(uv, d, 0) + uv, uv)
                av = jnp.where(m, av * pltpu.roll(av, d, 0), av)
            hv = uv + av * carry
            h_ref[pl.ds(r0, SUBLANES), :] = hv
            return hv[SUBLANES - 1:SUBLANES, :]

        carry_ref[0:1, :] = lax.fori_loop(0, ngrp, grp, carry_ref[0:1, :])
        grv = gr_ref[...].astype(F32)
        gg, tg = _gelu_t(grv)
        hv = h_ref[...]
        ya_ref[...] = (hv * gg).astype(BF16)
        gg_ref[...] = gg.astype(BF16)
        hg_ref[...] = (hv * _gelu_grad(tg)).astype(BF16)

    vec = _const_spec((1, D))
    wspec = _const_spec((NH, HD, HD))
    tile = pl.BlockSpec((tm, D), lambda i: (i, 0))
    bshape = jax.ShapeDtypeStruct((t, D), BF16)
    return pl.pallas_call(
        body, name="rnn_fwd", grid=(t // tm,),
        in_specs=[tile, pl.BlockSpec((HALO, D), _prev_halo_map(tm, 0)),
                  pl.BlockSpec((tm, D), lambda i: (i, 1)), _const_spec((4, D)), vec, wspec, vec, wspec, vec, vec],
        out_specs=[tile] * 7,
        out_shape=[jax.ShapeDtypeStruct((t, D), F32)] + [bshape] * 6,
        scratch_shapes=[pltpu.VMEM((SUBLANES, D), F32), pltpu.VMEM((tm, D), F32), pltpu.VMEM((tm, D), F32)],
        compiler_params=_cparams(1),
    )(z, z, z, cw, cb, wa, ba, wx, bx, lam)


def _sgu_fwd(z, lng, lnb, wm, bst):
    t = z.shape[0]
    tm = _tile_seq(t)

    def body(zu_ref, zv_ref, lng_ref, lnb_ref, wm_ref, bst_ref, yb_ref, gu_ref, mg_ref, vh_ref, gpv_ref, rstd_ref):
        gu, su = _gelu_t(zu_ref[...].astype(F32))
        gv, sv = _gelu_t(zv_ref[...].astype(F32))
        mu = jnp.mean(gv, axis=-1, keepdims=True)
        cen = gv - mu
        rstd = lax.rsqrt(jnp.mean(cen * cen, axis=-1, keepdims=True) + EPS)
        vhat = cen * rstd
        vb = (vhat * lng_ref[...] + lnb_ref[...]).astype(BF16)
        rows = []
        for b0 in range(0, tm, HD):
            rows.append(jnp.concatenate(
                [jnp.dot(wm_ref[g], vb[b0:b0 + HD, g * HD:(g + 1) * HD], preferred_element_type=F32)
                 + bst_ref[:, g:g + 1] for g in range(NH)], axis=1))
        mixed = jnp.concatenate(rows, axis=0) if len(rows) > 1 else rows[0]
        yb_ref[...] = (gu * mixed).astype(BF16)
        gu_ref[...] = gu.astype(BF16)
        mg_ref[...] = (mixed * _gelu_grad(su)).astype(BF16)
        vh_ref[...] = vhat.astype(BF16)
        gpv_ref[...] = _gelu_grad(sv).astype(BF16)
        rstd_ref[...] = rstd

    vec = _const_spec((1, D))
    tile = pl.BlockSpec((tm, D), lambda i: (i, 0))
    bshape = jax.ShapeDtypeStruct((t, D), BF16)
    return pl.pallas_call(
        body, name="sgu_fwd", grid=(t // tm,),
        in_specs=[pl.BlockSpec((tm, D), lambda i: (i, 2)), pl.BlockSpec((tm, D), lambda i: (i, 3)), vec, vec,
                  _const_spec((NH, HD, HD)), _const_spec((HD, NH))],
        out_specs=[tile] * 5 + [pl.BlockSpec((tm, 1), lambda i: (i, 0))],
        out_shape=[bshape] * 5 + [jax.ShapeDtypeStruct((t, 1), F32)],
        compiler_params=_cparams(1),
    )(z, z, lng, lnb, wm, bst)


def _merge_fwd(ya_pre, yb_pre, z, x, gate1, wba, wbb, wout):
    t = x.shape[0]
    tm = _tile_big(t)

    def body(yap_ref, ybp_ref, ga_ref, gb_ref, x_ref, g1_ref, wba_ref, wbb_ref, wo_ref,
             x2_ref, ya_ref, yb_ref, mg_ref, o1_ref):
        ya = jnp.dot(yap_ref[...], wba_ref[...], preferred_element_type=F32)
        yb = jnp.dot(ybp_ref[...], wbb_ref[...], preferred_element_type=F32)
        merged = _sigmoid(ga_ref[...].astype(F32)) * ya + _sigmoid(gb_ref[...].astype(F32)) * yb
        mb = merged.astype(BF16)
        o1 = jnp.dot(mb, wo_ref[...], preferred_element_type=F32)
        x2_ref[...] = x_ref[...] + g1_ref[...] * o1
        ya_ref[...] = ya.astype(BF16)
        yb_ref[...] = yb.astype(BF16)
        mg_ref[...] = mb
        o1_ref[...] = o1.astype(BF16)

    tile = pl.BlockSpec((tm, D), lambda i: (i, 0))
    wspec = _const_spec((D, D))
    bshape = jax.ShapeDtypeStruct((t, D), BF16)
    return pl.pallas_call(
        body, name="merge_fwd", grid=(t // tm,),
        in_specs=[tile, tile, pl.BlockSpec((tm, D), lambda i: (i, 4)), pl.BlockSpec((tm, D), lambda i: (i, 5)),
                  tile, _const_spec((1, D)), wspec, wspec, wspec],
        out_specs=[tile] * 5,
        out_shape=[jax.ShapeDtypeStruct((t, D), F32), bshape, bshape, bshape, bshape],
        compiler_params=_cparams(1),
    )(ya_pre, yb_pre, z, z, x, gate1, wba, wbb, wout)


def _conv3(u, prev8, cw_ref, cb):
    return cb + cw_ref[2:3, :] * u + cw_ref[1:2, :] * _shift_down(u, 1, prev8) + cw_ref[0:1, :] * _shift_down(u, 2, prev8)


def _ffn_proj_mid(x2, g, scale, shift, w, cw, cb):
    t = x2.shape[0]
    tm = _tile_big(t)
    nc = DFF // D

    def body(x_ref, g_ref, sc_ref, sh_ref, wa_ref, wv_ref, cwa_ref, cwv_ref, cba_ref, cbv_ref,
             h_ref, upa_ref, upv_ref, ff_ref, fa_ref, fv_ref, hb_scr, prev_ref):
        i, c = pl.program_id(0), pl.program_id(1)

        @pl.when(i == 0)
        def _():
            prev_ref[c] = jnp.zeros((2, SUBLANES, D), F32)

        @pl.when(c == 0)
        def _():
            xv = x_ref[...]
            r = lax.rsqrt(jnp.mean(xv * xv, axis=-1, keepdims=True) + EPS)
            hb_scr[...] = ((xv * r * g_ref[...]) * (1.0 + sc_ref[...]) + sh_ref[...]).astype(BF16)
            h_ref[...] = hb_scr[...]

        hb = hb_scr[...]
        halves = []
        for s, (w_ref, up_ref, cw_ref, cb_ref) in enumerate(((wa_ref, upa_ref, cwa_ref, cba_ref),
                                                             (wv_ref, upv_ref, cwv_ref, cbv_ref))):
            u = jnp.dot(hb, w_ref[...], preferred_element_type=F32)
            up_ref[...] = u.astype(BF16)
            halves.append(_conv3(u, prev_ref[c, s], cw_ref, cb_ref[...]))
            prev_ref[c, s] = u[tm - SUBLANES:]
        act, val = halves
        ga, ta = _gelu_t(act)
        ff_ref[...] = (ga * val).astype(BF16)
        fa_ref[...] = (val * _gelu_grad(ta)).astype(BF16)
        fv_ref[...] = ga.astype(BF16)

    def cols(rows, off):
        return pl.BlockSpec((rows, D), lambda i, c: (0, off + c))

    vec = pl.BlockSpec((1, D), lambda i, c: (0, 0))
    row_tile = pl.BlockSpec((tm, D), lambda i, c: (i, 0))
    chunk = pl.BlockSpec((tm, D), lambda i, c: (i, c))
    hshape = jax.ShapeDtypeStruct((t, DFF), BF16)
    return pl.pallas_call(
        body, name="ffn_proj_mid", grid=(t // tm, nc),
        in_specs=[row_tile, vec, vec, vec, cols(D, 0), cols(D, nc), cols(3, 0), cols(3, nc), cols(1, 0), cols(1, nc)],
        out_specs=[row_tile, chunk, chunk, chunk, chunk, chunk],
        out_shape=[jax.ShapeDtypeStruct((t, D), BF16), hshape, hshape, hshape, hshape, hshape],
        scratch_shapes=[pltpu.VMEM((tm, D), BF16), pltpu.VMEM((nc, 2, SUBLANES, D), F32)],
        compiler_params=_cparams(2),
    )(x2, g, scale, shift, w, w, cw, cw, cb, cb)


def _ffn_out_loss(ff, wd, x2, target, gate2, gfin):
    t = x2.shape[0]
    tm = _tile_big(t)

    def body(ff_ref, wd_ref, x2_ref, tg_ref, g2_ref, gf_ref, dx3_ref, loss_ref, dgf_ref, dg2_ref):
        @pl.when(pl.program_id(0) == 0)
        def _():
            loss_ref[...] = jnp.zeros_like(loss_ref)
            dgf_ref[...] = jnp.zeros_like(dgf_ref)
            dg2_ref[...] = jnp.zeros_like(dg2_ref)

        o2 = jnp.dot(ff_ref[...], wd_ref[...], preferred_element_type=F32)
        x3 = x2_ref[...] + g2_ref[...] * o2
        r = lax.rsqrt(jnp.mean(x3 * x3, axis=-1, keepdims=True) + EPS)
        xhat = x3 * r
        err = xhat * gf_ref[...] - tg_ref[...]
        loss_ref[...] += 0.5 * jnp.sum(jnp.mean(err * err, axis=-1, keepdims=True), axis=0, keepdims=True)
        dy = err * (1.0 / D)
        dgf_ref[...] += _colsum(dy * xhat)
        dxh = dy * gf_ref[...]
        dx3 = r * (dxh - xhat * jnp.mean(dxh * xhat, axis=-1, keepdims=True))
        dx3_ref[...] = dx3
        dg2_ref[...] += _colsum(dx3 * o2)

    tile = pl.BlockSpec((tm, D), lambda i: (i, 0))
    vec = _const_spec((1, D))
    return pl.pallas_call(
        body, name="ffn_out_loss", grid=(t // tm,),
        in_specs=[pl.BlockSpec((tm, DFF), lambda i: (i, 0)), _const_spec((DFF, D), True), tile, tile, vec, vec],
        out_specs=[tile, _const_spec((1, 1)), vec, vec],
        out_shape=[jax.ShapeDtypeStruct((t, D), F32), jax.ShapeDtypeStruct((1, 1), F32),
                   jax.ShapeDtypeStruct((1, D), F32), jax.ShapeDtypeStruct((1, D), F32)],
        compiler_params=_cparams(1),
    )(ff, wd, x2, target, gate2, gfin)


def _ffn_down_bwd(dx3, gate2, ff, fa, fv, wd):
    t = dx3.shape[0]
    tm = min(1024, t)
    nc = DFF // D

    def body(dx3_ref, g2_ref, ff_ref, fa_ref, fv_ref, wd_ref, da_ref, dv_ref, dwd_ref, dcba_ref, dcbv_ref):
        @pl.when(pl.program_id(1) == 0)
        def _():
            for r in (dwd_ref, dcba_ref, dcbv_ref):
                r[...] = jnp.zeros_like(r)

        do2 = (dx3_ref[...] * g2_ref[...]).astype(BF16)
        dwd_ref[...] += _dot_tn(ff_ref[...], do2)
        dff = _dot_nt(do2, wd_ref[...])
        dact = dff * fa_ref[...].astype(F32)
        dval = dff * fv_ref[...].astype(F32)
        da_ref[...] = dact.astype(BF16)
        dv_ref[...] = dval.astype(BF16)
        dcba_ref[...] += _colsum(dact)
        dcbv_ref[...] += _colsum(dval)

    blk = pl.BlockSpec((tm, D), lambda c, i: (i, c))
    vec = pl.BlockSpec((1, D), lambda c, i: (0, c))
    return pl.pallas_call(
        body, name="ffn_down_bwd", grid=(nc, t // tm),
        in_specs=[pl.BlockSpec((tm, D), lambda c, i: (i, 0)), pl.BlockSpec((1, D), lambda c, i: (0, 0)),
                  blk, blk, blk, pl.BlockSpec((D, D), lambda c, i: (c, 0))],
        out_specs=[blk, blk, pl.BlockSpec((D, D), lambda c, i: (c, 0)), vec, vec],
        out_shape=[jax.ShapeDtypeStruct((t, DFF), BF16), jax.ShapeDtypeStruct((t, DFF), BF16),
                   jax.ShapeDtypeStruct((DFF, D), F32),
                   jax.ShapeDtypeStruct((1, DFF), F32), jax.ShapeDtypeStruct((1, DFF), F32)],
        compiler_params=_cparams(2),
    )(dx3, gate2, ff, fa, fv, wd)


def _modnorm_bwd(dh, xv, g, scale):
    r = lax.rsqrt(jnp.mean(xv * xv, axis=-1, keepdims=True) + EPS)
    xhat = xv * r
    dxn = dh * (1.0 + scale)
    dxh = dxn * g
    dx = r * (dxh - xhat * jnp.mean(dxh * xhat, axis=-1, keepdims=True))
    return dx, _colsum(dh), _colsum(dh * (xhat * g)), _colsum(dxn * xhat)


def _ffn_up_bwd(dact, dval, up_a, up_v, cw, wup, x2, dx3, gffn, scale2, o1, gate1):
    t = x2.shape[0]
    tm = _tile_seq(t)
    nt = t // tm
    nc = DFF // D

    def body(da_ref, dan_ref, dv_ref, dvn_ref, ua_ref, uv_ref, cw_ref, w_ref, x2_ref, dx3_ref, g_ref, sc_ref, o1_ref, g1_ref,
             dup_ref, dx2_ref, do1_ref, dcw_ref, dsh_ref, dsc_ref, dg_ref, dg1_ref):
        i = pl.program_id(0)

        @pl.when(i == 0)
        def _():
            for r in (dcw_ref, dsh_ref, dsc_ref, dg_ref, dg1_ref):
                r[...] = jnp.zeros_like(r)

        last = i == nt - 1
        dh = jnp.zeros((tm, D), F32)
        for half, (d_ref, dn_ref, u_ref) in enumerate(((da_ref, dan_ref, ua_ref), (dv_ref, dvn_ref, uv_ref))):
            nxt = jnp.where(last, 0.0, dn_ref[...].astype(F32)[:SUBLANES])
            for c in range(nc):
                c0 = half * DFF + c * D
                dv = d_ref[:, c * D:(c + 1) * D].astype(F32)
                nx = nxt[:, c * D:(c + 1) * D]
                taps = (_shift_up(dv, 2, nx), _shift_up(dv, 1, nx), dv)
                dup = (cw_ref[2:3, c0:c0 + D] * taps[2] + cw_ref[1:2, c0:c0 + D] * taps[1]
                       + cw_ref[0:1, c0:c0 + D] * taps[0]).astype(BF16)
                upv = u_ref[:, c * D:(c + 1) * D].astype(F32)
                for k in range(3):
                    dcw_ref[k:k + 1, c0:c0 + D] += _colsum(taps[k] * upv)
                dup_ref[:, c0:c0 + D] = dup
                dh = dh + _dot_nt(dup, w_ref[:, c0:c0 + D])
        dxn, dsh, dsc, dg = _modnorm_bwd(dh, x2_ref[...], g_ref[...], sc_ref[...])
        dx2 = dx3_ref[...] + dxn
        dx2_ref[...] = dx2
        do1_ref[...] = (dx2 * g1_ref[...]).astype(BF16)
        dsh_ref[...] += dsh
        dsc_ref[...] += dsc
        dg_ref[...] += dg
        dg1_ref[...] += _colsum(dx2 * o1_ref[...].astype(F32))

    tile = pl.BlockSpec((tm, D), lambda i: (i, 0))
    wide = pl.BlockSpec((tm, DFF), lambda i: (i, 0))
    nxt = pl.BlockSpec((HALO, DFF), lambda i: (jnp.minimum((i + 1) * (tm // HALO), t // HALO - 1), 0))
    vec = _const_spec((1, D))
    vshape = jax.ShapeDtypeStruct((1, D), F32)
    return pl.pallas_call(
        body, name="ffn_up_bwd", grid=(nt,),
        in_specs=[wide, nxt, wide, nxt, wide, wide,
                  _const_spec((3, 2 * DFF)), _const_spec((D, 2 * DFF), True),
                  tile, tile, vec, vec, tile, vec],
        out_specs=[pl.BlockSpec((tm, 2 * DFF), lambda i: (i, 0)), tile, tile, _const_spec((3, 2 * DFF)),
                   vec, vec, vec, vec],
        out_shape=[jax.ShapeDtypeStruct((t, 2 * DFF), BF16), jax.ShapeDtypeStruct((t, D), F32),
                   jax.ShapeDtypeStruct((t, D), BF16), jax.ShapeDtypeStruct((3, 2 * DFF), F32),
                   vshape, vshape, vshape, vshape],
        compiler_params=_cparams(1),
    )(dact, dact, dval, dval, up_a, up_v, cw, wup, x2, dx3, gffn, scale2, o1, gate1)


def _xt_y(a, b, name):
    t, k = a.shape
    n = b.shape[1]
    tm = min(1024, t)
    bn = 1536 if n % 1536 == 0 else D

    def body(a_ref, b_ref, o_ref):
        @pl.when(pl.program_id(1) == 0)
        def _():
            o_ref[...] = jnp.zeros_like(o_ref)

        o_ref[...] += _dot_tn(a_ref[...], b_ref[...])

    return pl.pallas_call(
        body, name=name, grid=(n // bn, t // tm),
        in_specs=[pl.BlockSpec((tm, k), lambda j, i: (i, 0)), pl.BlockSpec((tm, bn), lambda j, i: (i, j))],
        out_specs=pl.BlockSpec((k, bn), lambda j, i: (0, j)),
        out_shape=jax.ShapeDtypeStruct((k, n), F32),
        compiler_params=_cparams(2),
    )(a, b)


def _acc_spec(shape, index):
    return pl.BlockSpec(shape, lambda *_: index, pipeline_mode=pl.Buffered(1))


def _out_bwd(do1, wout, merged, ya, yb, z, h1):
    t = do1.shape[0]
    tm = _tile_big(t)

    def body(do1_ref, wo_ref, mg_ref, ya_ref, yb_ref, ga_ref, gb_ref, h1_ref,
             dya_ref, dyb_ref, dz_ref, dwo_ref, dwin_ref):
        @pl.when(pl.program_id(0) == 0)
        def _():
            dwo_ref[...] = jnp.zeros_like(dwo_ref)
            dwin_ref[...] = jnp.zeros_like(dwin_ref)

        do1v = do1_ref[...]
        dwo_ref[...] += _dot_tn(mg_ref[...], do1v)
        dm = _dot_nt(do1v, wo_ref[...])
        sa = _sigmoid(ga_ref[...].astype(F32))
        sb = _sigmoid(gb_ref[...].astype(F32))
        dya_ref[...] = (dm * sa).astype(BF16)
        dyb_ref[...] = (dm * sb).astype(BF16)
        dga = (dm * ya_ref[...].astype(F32) * sa * (1.0 - sa)).astype(BF16)
        dgb = (dm * yb_ref[...].astype(F32) * sb * (1.0 - sb)).astype(BF16)
        dz_ref[:, 0:D] = dga
        dz_ref[:, D:2 * D] = dgb
        h1v = h1_ref[...]
        dwin_ref[:, 0:D] += _dot_tn(h1v, dga)
        dwin_ref[:, D:2 * D] += _dot_tn(h1v, dgb)

    tile = pl.BlockSpec((tm, D), lambda i: (i, 0))
    bshape = jax.ShapeDtypeStruct((t, D), BF16)
    return pl.pallas_call(
        body, name="out_bwd", grid=(t // tm,),
        in_specs=[tile, _const_spec((D, D), True), tile, tile, tile,
                  pl.BlockSpec((tm, D), lambda i: (i, 4)), pl.BlockSpec((tm, D), lambda i: (i, 5)), tile],
        out_specs=[tile, tile, pl.BlockSpec((tm, 2 * D), lambda i: (i, 2)), _acc_spec((D, D), (0, 0)),
                   _acc_spec((D, 2 * D), (0, 2))],
        out_shape=[bshape, bshape, jax.ShapeDtypeStruct((t, NCOL_IN), BF16), jax.ShapeDtypeStruct((D, D), F32),
                   jax.ShapeDtypeStruct((D, NCOL_IN), F32)],
        compiler_params=_cparams(1),
    )(do1, wout, merged, ya, yb, z, z, h1)


def _rnn_bwd(dya, ya_pre, wba, h1, z, saved, h, dz, dwin, cw, wa, wx, lam):
    t = z.shape[0]
    tm = _tile_seq(t)
    nt = t // tm
    ngrp = tm // SUBLANES
    hpt = tm // HALO

    def body(dya_ref, yap_ref, wba_ref, h1_ref, xr_ref, xc_ref, ra_ref, ia_ref, gg_ref, hg_ref, h_ref, hp_ref,
             dz_any, dwin_any, cw_ref, wa_ref, wx_ref, lam_ref,
             dz_ref, dwin_ref, dwba_ref, dcw_ref, dcb_ref, dwa_ref, dba_ref, dwx_ref, dbx_ref, dlam_ref,
             a_first, g_first, dxc_first, b_scr, d_scr, g_scr):
        del dz_any, dwin_any
        i = pl.program_id(0)

        @pl.when(i == 0)
        def _():
            for r in (dwin_ref, dwba_ref, dcw_ref, dcb_ref, dwa_ref, dba_ref, dwx_ref, dbx_ref, dlam_ref,
                      a_first, g_first, dxc_first):
                r[...] = jnp.zeros_like(r)

        dya_v = dya_ref[...]
        dwba_ref[...] += _dot_tn(yap_ref[...], dya_v)
        dyap_v = _dot_nt(dya_v, wba_ref[...])
        h1v = h1_ref[...]

        first_tile = i == nt - 1
        xc = xc_ref[...].astype(F32)
        ra = ra_ref[...].astype(F32)
        ia = ia_ref[...].astype(F32)
        lam_v = lam_ref[...]
        ls = _log_sigmoid(lam_v)
        la = LRU_C * ra * ls
        a = jnp.exp(la)
        mult = jnp.sqrt(-jnp.tanh(la) * (1.0 + a * a))
        hprev8 = jnp.where(first_tile, 0.0, hp_ref[...][HALO - SUBLANES:])
        h_prev = _shift_down(h_ref[...], 1, hprev8)
        dgr = (dyap_v * hg_ref[...].astype(F32)).astype(BF16)
        dz_ref[:, D:2 * D] = dgr
        dwin_ref[:, D:2 * D] += _dot_tn(h1v, dgr)

        b_scr[...] = _shift_up(a, 1, a_first[...])
        d_scr[...] = dyap_v * gg_ref[...].astype(F32)
        row = _row_iota(D)

        def grp(jj, carry):
            r0 = pl.multiple_of((ngrp - 1 - jj) * SUBLANES, SUBLANES)
            bv = b_scr[pl.ds(r0, SUBLANES), :]
            dv = d_scr[pl.ds(r0, SUBLANES), :]
            for d in (1, 2, 4):
                m = row < SUBLANES - d
                dv = jnp.where(m, dv + bv * pltpu.roll(dv, SUBLANES - d, 0), dv)
                bv = jnp.where(m, bv * pltpu.roll(bv, SUBLANES - d, 0), bv)
            gv = dv + bv * carry
            g_scr[pl.ds(r0, SUBLANES), :] = gv
            return gv[0:1, :]

        lax.fori_loop(0, ngrp, grp, g_first[0:1, :])
        g = g_scr[...]
        a_first[...] = a[:SUBLANES]
        g_first[...] = g[:SUBLANES]

        da = g * h_prev
        gx = g * xc
        dmult = gx * ia
        dia = gx * mult
        dxc = g * (mult * ia)
        dla = da * a - dmult * (a * a) / mult
        dra = dla * (LRU_C * ls)
        dlam_ref[...] += _colsum(dla * ra) * (LRU_C * _sigmoid(-lam_v))
        dpa = dra * ra * (1.0 - ra)
        dpx = dia * ia * (1.0 - ia)
        dba_ref[...] += _colsum(dpa)
        dbx_ref[...] += _colsum(dpx)
        dpab = dpa.astype(BF16)
        dpxb = dpx.astype(BF16)
        xcb = xc_ref[...]
        for hd in range(NH):
            sl = slice(hd * HD, (hd + 1) * HD)
            dwa_ref[hd] += _dot_tn(xcb[:, sl], dpab[:, sl])
            dwx_ref[hd] += _dot_tn(xcb[:, sl], dpxb[:, sl])
        dxc = dxc + _heads_nt(dpab, wa_ref) + _heads_nt(dpxb, wx_ref)

        nxt = dxc_first[...]
        taps = (_shift_up(dxc, 3, nxt), _shift_up(dxc, 2, nxt), _shift_up(dxc, 1, nxt), dxc)
        dxr = cw_ref[0:1, :] * taps[0]
        for k in range(1, 4):
            dxr = dxr + cw_ref[k:k + 1, :] * taps[k]
        dxrb = dxr.astype(BF16)
        dz_ref[:, 0:D] = dxrb
        dwin_ref[:, 0:D] += _dot_tn(h1v, dxrb)
        dxc_first[...] = dxc[:SUBLANES]
        dcb_ref[...] += _colsum(dxc)
        xr = xr_ref[...].astype(F32)
        for k in range(4):
            dcw_ref[k:k + 1, :] += _colsum(taps[k] * xr)

    def rev(col):
        return lambda i: (nt - 1 - i, col)

    vec = _const_spec((1, D))
    wspec = _const_spec((NH, HD, HD))
    vshape = jax.ShapeDtypeStruct((1, D), F32)
    wshape = jax.ShapeDtypeStruct((NH, HD, HD), F32)
    any_spec = pl.BlockSpec(memory_space=pl.ANY)
    tile = pl.BlockSpec((tm, D), rev(0))
    outs = pl.pallas_call(
        body, name="rnn_bwd", grid=(nt,),
        in_specs=[tile, tile, _const_spec((D, D), True), tile, tile, tile, tile, tile, tile, tile, tile,
                  pl.BlockSpec((HALO, D), lambda i: (jnp.maximum((nt - 1 - i) * hpt - 1, 0), 0)),
                  any_spec, any_spec, _const_spec((4, D)), wspec, wspec, vec],
        out_specs=[pl.BlockSpec((tm, 2 * D), rev(0)), _acc_spec((D, 2 * D), (0, 0)), _acc_spec((D, D), (0, 0)),
                   _const_spec((4, D)), vec, wspec, vec, wspec, vec, vec],
        out_shape=[jax.ShapeDtypeStruct((t, NCOL_IN), BF16), jax.ShapeDtypeStruct((D, NCOL_IN), F32),
                   jax.ShapeDtypeStruct((D, D), F32), jax.ShapeDtypeStruct((4, D), F32), vshape,
                   wshape, vshape, wshape, vshape, vshape],
        scratch_shapes=[pltpu.VMEM((SUBLANES, D), F32), pltpu.VMEM((SUBLANES, D), F32), pltpu.VMEM((SUBLANES, D), F32),
                        pltpu.VMEM((tm, D), F32), pltpu.VMEM((tm, D), F32), pltpu.VMEM((tm, D), F32)],
        input_output_aliases={12: 0, 13: 1},
        compiler_params=_cparams(1),
    )(dya, ya_pre, wba, h1, z, *saved, h, h, dz, dwin, cw, wa, wx, lam)
    return outs


def _sgu_bwd(dyb, yb_pre, wbb, h1, saved, dz, dwin, lng, lnb, wmt, mask):
    t = dyb.shape[0]
    tm = _tile_big(t)

    def body(dyb_ref, ybp_ref, wbb_ref, h1_ref, gu_ref, mg_ref, vh_ref, gpv_ref, rstd_ref, dz_any, dwin_any,
             lng_ref, lnb_ref, wmt_ref, mask_ref,
             dz_ref, dwin_ref, dwbb_ref, dws_ref, dbst_ref, dlng_ref, dlnb_ref):
        del dz_any, dwin_any

        @pl.when(pl.program_id(0) == 0)
        def _():
            for r in (dwin_ref, dwbb_ref, dws_ref, dbst_ref, dlng_ref, dlnb_ref):
                r[...] = jnp.zeros_like(r)

        lng_v = lng_ref[...]
        vhat = vh_ref[...].astype(F32)
        vb = (vhat * lng_v + lnb_ref[...]).astype(BF16)
        rstd = rstd_ref[...]
        dyb_v = dyb_ref[...]
        dwbb_ref[...] += _dot_tn(ybp_ref[...], dyb_v)
        dyb = _dot_nt(dyb_v, wbb_ref[...])
        h1v = h1_ref[...]
        dzu = (dyb * mg_ref[...].astype(F32)).astype(BF16)
        dz_ref[:, 0:D] = dzu
        dwin_ref[:, 0:D] += _dot_tn(h1v, dzu)
        dmix = dyb * gu_ref[...].astype(F32)
        dmb = dmix.astype(BF16)
        rows = []
        lane = lax.broadcasted_iota(jnp.int32, (HD, NH), 1)
        dbst = jnp.zeros((HD, NH), F32)
        for b0 in range(0, tm, HD):
            cols = []
            for g in range(NH):
                sl = slice(g * HD, (g + 1) * HD)
                dmg = dmb[b0:b0 + HD, sl]
                dws_ref[g] += _dot_nt(dmg, vb[b0:b0 + HD, sl]) * mask_ref[...]
                cols.append(jnp.dot(wmt_ref[g], dmg, preferred_element_type=F32))
                dbst = dbst + jnp.where(lane == g, jnp.sum(dmix[b0:b0 + HD, sl], axis=1, keepdims=True), 0.0)
            rows.append(jnp.concatenate(cols, axis=1))
        dbst_ref[...] += dbst
        dvln = jnp.concatenate(rows, axis=0) if len(rows) > 1 else rows[0]
        dlng_ref[...] += _colsum(dvln * vhat)
        dlnb_ref[...] += _colsum(dvln)
        dvh = dvln * lng_v
        dgv = rstd * (dvh - jnp.mean(dvh, axis=-1, keepdims=True)
                      - vhat * jnp.mean(dvh * vhat, axis=-1, keepdims=True))
        dzv = (dgv * gpv_ref[...].astype(F32)).astype(BF16)
        dz_ref[:, D:2 * D] = dzv
        dwin_ref[:, D:2 * D] += _dot_tn(h1v, dzv)

    vec = _const_spec((1, D))
    wspec = _const_spec((NH, HD, HD))
    vshape = jax.ShapeDtypeStruct((1, D), F32)
    tile = pl.BlockSpec((tm, D), lambda i: (i, 0))
    any_spec = pl.BlockSpec(memory_space=pl.ANY)
    return pl.pallas_call(
        body, name="sgu_bwd", grid=(t // tm,),
        in_specs=[tile, tile, _const_spec((D, D), True), tile, tile, tile, tile, tile,
                  pl.BlockSpec((tm, 1), lambda i: (i, 0)), any_spec, any_spec,
                  vec, vec, wspec, _const_spec((HD, HD))],
        out_specs=[pl.BlockSpec((tm, 2 * D), lambda i: (i, 1)), _acc_spec((D, 2 * D), (0, 1)), _acc_spec((D, D), (0, 0)),
                   wspec, _const_spec((HD, NH)), vec, vec],
        out_shape=[jax.ShapeDtypeStruct((t, NCOL_IN), BF16), jax.ShapeDtypeStruct((D, NCOL_IN), F32),
                   jax.ShapeDtypeStruct((D, D), F32), jax.ShapeDtypeStruct((NH, HD, HD), F32),
                   jax.ShapeDtypeStruct((HD, NH), F32), vshape, vshape],
        input_output_aliases={9: 0, 10: 1},
        compiler_params=_cparams(1),
    )(dyb, yb_pre, wbb, h1, *saved, dz, dwin, lng, lnb, wmt, mask)


def _in_bwd(dz, win, x, dx2, g, scale1):
    t = x.shape[0]
    tm = _tile_big(t)

    def body(dz_ref, w_ref, x_ref, dx2_ref, g_ref, sc_ref, dx_ref, dsh_ref, dsc_ref, dg_ref):
        @pl.when(pl.program_id(0) == 0)
        def _():
            for r in (dsh_ref, dsc_ref, dg_ref):
                r[...] = jnp.zeros_like(r)

        dh = jnp.zeros((tm, D), F32)
        for c0 in range(0, NCOL_IN, D):
            dh = dh + _dot_nt(dz_ref[:, c0:c0 + D], w_ref[:, c0:c0 + D])
        dxn, dsh, dsc, dg = _modnorm_bwd(dh, x_ref[...], g_ref[...], sc_ref[...])
        dx_ref[...] = dx2_ref[...] + dxn
        dsh_ref[...] += dsh
        dsc_ref[...] += dsc
        dg_ref[...] += dg

    tile = pl.BlockSpec((tm, D), lambda i: (i, 0))
    vec = _const_spec((1, D))
    vshape = jax.ShapeDtypeStruct((1, D), F32)
    return pl.pallas_call(
        body, name="in_bwd", grid=(t // tm,),
        in_specs=[pl.BlockSpec((tm, NCOL_IN), lambda i: (i, 0)), _const_spec((D, NCOL_IN), True), tile, tile, vec, vec],
        out_specs=[tile, vec, vec, vec],
        out_shape=[jax.ShapeDtypeStruct((t, D), F32), vshape, vshape, vshape],
        compiler_params=_cparams(1),
    )(dz, win, x, dx2, g, scale1)


def _ada_grad(c_all, dmod_cols):
    cols = dmod_cols.shape[1]

    def body(c_ref, d_ref, o_ref):
        cv = c_ref[...]
        ca = (cv * _sigmoid(cv)).astype(BF16)
        o_ref[...] = _dot_tn(ca, d_ref[...].astype(BF16))

    return pl.pallas_call(body, name="ada_grad", out_shape=jax.ShapeDtypeStruct((D, cols), F32))(c_all, dmod_cols)


def _adamw_update(w, m, v, g):
    bc1 = 1.0 - ADAM_B1 ** ADAM_STEP
    bc2 = 1.0 - ADAM_B2 ** ADAM_STEP
    mn = ADAM_B1 * m + (1.0 - ADAM_B1) * g
    vn = ADAM_B2 * v + (1.0 - ADAM_B2) * (g * g)
    return -ADAM_LR * ((mn / bc1) / (jnp.sqrt(vn / bc2) + ADAM_EPS) + ADAM_WD * w), mn, vn


def _adamw_group(names, ws, ms, vs, packs, name):
    n = len(names)
    starts, r0 = [], 0
    for w in ws:
        starts.append(r0)
        r0 += _pack_rows(w.shape)

    def body(*refs):
        w_refs, m_refs, v_refs, p_ref = refs[:n], refs[n:2 * n], refs[2 * n:3 * n], refs[3 * n]
        outs = refs[3 * n + 1:]
        for k in range(n):
            rows = _pack_rows(ws[k].shape)
            g = None
            for dev in range(N_DEV):
                if ws[k].shape[0] == 1:
                    term = jnp.concatenate(
                        [p_ref[dev, starts[k] + r:starts[k] + r + 1, :] for r in range(rows)], axis=1)
                else:
                    term = p_ref[dev, starts[k]:starts[k] + rows, :]
                g = term if g is None else g + term
            delta, mn, vn = _adamw_update(w_refs[k][...], m_refs[k][...], v_refs[k][...], g)
            for o_ref, val in zip(outs[4 * k:4 * k + 4], (g, delta, mn, vn)):
                o_ref[...] = val

    shapes = [jax.ShapeDtypeStruct(w.shape, F32) for w in ws for _ in range(4)]
    outs = pl.pallas_call(body, name=name, out_shape=shapes,
                          compiler_params=pltpu.CompilerParams(vmem_limit_bytes=VMEM_LIMIT))(*ws, *ms, *vs, packs)
    return {nm: tuple(outs[4 * k:4 * k + 4]) for k, nm in enumerate(names)}


def _adamw(w, m, v, parts, name):
    rows, cols = w.shape
    tr = _row_tile(rows, cols)
    stacked = [p.ndim == 3 for p in parts]

    def body(*refs):
        w_ref, m_ref, v_ref = refs[:3]
        p_refs = refs[3:3 + len(parts)]
        g_ref, d_ref, mo_ref, vo_ref = refs[3 + len(parts):]
        g = None
        for p_ref, st in zip(p_refs, stacked):
            terms = [p_ref[k].astype(F32) for k in range(p_ref.shape[0])] if st else [p_ref[...].astype(F32)]
            for term in terms:
                g = term if g is None else g + term
        delta, mn, vn = _adamw_update(w_ref[...], m_ref[...], v_ref[...], g)
        g_ref[...] = g
        mo_ref[...] = mn
        vo_ref[...] = vn
        d_ref[...] = delta

    tile = pl.BlockSpec((tr, cols), lambda i: (i, 0))
    p_specs = [pl.BlockSpec((p.shape[0], tr, cols), lambda i: (0, i, 0)) if st else tile for p, st in zip(parts, stacked)]
    shp = jax.ShapeDtypeStruct((rows, cols), F32)
    return pl.pallas_call(
        body, name=name, grid=(rows // tr,),
        in_specs=[tile, tile, tile] + p_specs, out_specs=[tile] * 4, out_shape=[shp] * 4,
        compiler_params=_cparams(1),
    )(w, m, v, *parts)


def _mesh_pos():
    return lax.axis_index("x"), lax.axis_index("y"), lax.axis_index("c")


def _other_chips(x, y):
    return [(1 - x, y), (x, 1 - y), (1 - x, 1 - y)]


def _block_of(ref, axis, index, size):
    if axis == 0:
        return ref.at[index]
    return ref.at[:, pl.ds(pl.multiple_of(index * size, 128), size)]


def _all_gather(shards, axes, name, ada=None):
    n = len(shards)
    per = 7
    extra = 3 if ada is not None else 0

    def ada_exchange(c_ref, wada_ref, b_ref, c_all_ref, mod_ref, modc_scr, small_send, small_recv, x, y, c):
        def peer(k):
            return (1 - x if k & 4 else x, 1 - y if k & 2 else y, 1 - c if k & 1 else c)

        def index(pos):
            return 4 * pos[0] + 2 * pos[1] + pos[2]

        me = index((x, y, c))

        def to_all(src_of, dst, base):
            copies = [pltpu.make_async_remote_copy(
                src_ref=src_of(peer(k)), dst_ref=dst.at[me], send_sem=small_send.at[base + k - 1],
                recv_sem=small_recv.at[base + k - 1], device_id=peer(k), device_id_type=MESH_IDS)
                for k in range(1, N_DEV)]
            for cp in copies:
                cp.start()
            for cp in copies:
                cp.wait_recv()
            for cp in copies:
                cp.wait_send()

        c_all_ref[me] = c_ref[...]
        to_all(lambda _: c_ref, c_all_ref, 0)
        cv = c_all_ref[:, 0, :]
        ca = (cv * _sigmoid(cv)).astype(BF16)
        modc = jnp.dot(ca, wada_ref[...].astype(BF16), preferred_element_type=F32) + b_ref[...]
        for d in range(N_DEV):
            modc_scr[d] = modc[d:d + 1, :]
        mod_ref[me] = modc_scr[me]
        to_all(lambda pos: modc_scr.at[index(pos)], mod_ref, N_DEV - 1)

    def body(*refs):
        n_in = n + extra
        ins, outs, done = refs[:n], refs[n_in:n_in + n], refs[n_in + n]
        s0 = n_in + n + 1 + (2 if ada is not None else 0)
        send_sems, recv_sems, local_sems = refs[s0:s0 + 3]
        x, y, c = _mesh_pos()
        me, sibling = (x, y, c), (x, y, 1 - c)
        chips = _other_chips(x, y)

        def rows(a, pos):
            return _block_of(outs[a], axes[a], 4 * pos[0] + 2 * pos[1] + pos[2], shards[a].shape[-1])

        def copy(a, k, block, to, src=None):
            return pltpu.make_async_remote_copy(
                src_ref=rows(a, block) if src is None else src, dst_ref=rows(a, block),
                send_sem=send_sems.at[a * per + k], recv_sem=recv_sems.at[a * per + k],
                device_id=to, device_id_type=MESH_IDS)

        mine = [pltpu.make_async_copy(ins[a], rows(a, me), local_sems.at[a]) for a in range(n)]
        for cp in mine:
            cp.start()
        first = []
        for a in range(n):
            first.append(copy(a, 0, me, sibling, src=ins[a]))
            first += [copy(a, 1 + j, me, (*chip, c), src=ins[a]) for j, chip in enumerate(chips)]
        for cp in first:
            cp.start()
        if ada is not None:
            ada_exchange(*refs[n:n_in], *refs[n_in + n + 1:n_in + n + 3], *refs[s0 + 3:s0 + 6], x, y, c)
        passed = []
        for j, chip in enumerate(chips):
            for a in range(n):
                copy(a, 1 + j, (*chip, c), me).wait_recv()
                fwd = copy(a, 4 + j, (*chip, c), sibling)
                fwd.start()
                passed.append(fwd)
        for a in range(n):
            copy(a, 0, sibling, me).wait_recv()
            for j, chip in enumerate(chips):
                copy(a, 4 + j, (*chip, 1 - c), me).wait_recv()
        for cp in first + passed:
            cp.wait_send()
        for cp in mine:
            cp.wait()
        done[...] = jnp.zeros_like(done)

    def full_shape(s, ax):
        return (N_DEV,) + s.shape if ax == 0 else s.shape[:-1] + (N_DEV * s.shape[-1],)

    any_spec = pl.BlockSpec(memory_space=pl.ANY)
    vmem = pl.BlockSpec(memory_space=pltpu.VMEM)
    out_shape = [jax.ShapeDtypeStruct(full_shape(s, ax), s.dtype) for s, ax in zip(shards, axes)]
    out_shape.append(jax.ShapeDtypeStruct((SUBLANES, LANES), F32))
    scratch = [pltpu.SemaphoreType.DMA((n * per,)), pltpu.SemaphoreType.DMA((n * per,)), pltpu.SemaphoreType.DMA((n,))]
    if ada is not None:
        cols = ada[1].shape[1]
        out_shape += [jax.ShapeDtypeStruct((N_DEV, 1, D), F32), jax.ShapeDtypeStruct((N_DEV, 1, cols), F32)]
        scratch += [pltpu.VMEM((N_DEV, 1, cols), F32), pltpu.SemaphoreType.DMA((2 * (N_DEV - 1),)),
                    pltpu.SemaphoreType.DMA((2 * (N_DEV - 1),))]
    outs = pl.pallas_call(
        body, name=name,
        in_specs=[any_spec] * n + [vmem] * extra, out_specs=[any_spec] * n + [vmem] * (1 + (2 if extra else 0)),
        out_shape=out_shape, scratch_shapes=scratch,
    )(*shards, *(ada or ()))
    return (outs[:n], outs[n]) if ada is None else (outs[:n], outs[n], outs[n + 1], outs[n + 2])


def _chip_blocks(x, y):
    return [(x, y)] + _other_chips(x, y)


def _sibling_reduce(gs, axis, name):
    g0, n = gs[0], len(gs)
    rows, cols = (g0.shape[1], g0.shape[2]) if axis == 0 else (g0.shape[0], g0.shape[1] // N_DEV)
    chunk = math.gcd(rows, 64)

    def body(*refs):
        g_refs, own_refs, pay_refs = refs[:n], refs[n:2 * n], refs[2 * n:3 * n]
        send_buf, keep_buf, recv_buf, send_sems, recv_sems, stage_sems, keep_sems = refs[3 * n:]
        x, y, c = _mesh_pos()
        sibling = (x, y, 1 - c)
        chips = _chip_blocks(x, y)
        stage, keep, push = [], [], []
        for a in range(n):
            for j, (px, py) in enumerate(chips):
                s = 4 * a + j
                theirs = _block_of(g_refs[a], axis, 4 * px + 2 * py + (1 - c), cols)
                ours = _block_of(g_refs[a], axis, 4 * px + 2 * py + c, cols)
                stage.append(pltpu.make_async_copy(theirs, send_buf.at[s], stage_sems.at[s]))
                keep.append(pltpu.make_async_copy(ours, keep_buf.at[s], keep_sems.at[s]))
                push.append(pltpu.make_async_remote_copy(
                    src_ref=send_buf.at[s], dst_ref=recv_buf.at[s], send_sem=send_sems.at[s],
                    recv_sem=recv_sems.at[s], device_id=sibling, device_id_type=MESH_IDS))
        for cp in stage + keep:
            cp.start()
        for s in range(4 * n):
            stage[s].wait()
            push[s].start()
        for s in range(4 * n):
            push[s].wait_recv()
            keep[s].wait()
            a, j = divmod(s, 4)
            dst = own_refs[a] if j == 0 else pay_refs[a].at[j - 1]

            def add(r, carry, s=s, dst=dst):
                sl = pl.ds(pl.multiple_of(r * chunk, chunk), chunk)
                dst[sl, :] = (keep_buf[s, sl, :] + recv_buf[s, sl, :]).astype(dst.dtype)
                return carry

            lax.fori_loop(0, rows // chunk, add, 0)
        for cp in push:
            cp.wait_send()

    vmem = pl.BlockSpec(memory_space=pltpu.VMEM)
    buf = pltpu.VMEM((4 * n, rows, cols), F32)
    sems = pltpu.SemaphoreType.DMA((4 * n,))
    outs = pl.pallas_call(
        body, name=name,
        in_specs=[pl.BlockSpec(memory_space=pl.ANY)] * n, out_specs=[vmem] * (2 * n),
        out_shape=[jax.ShapeDtypeStruct((rows, cols), F32)] * n + [jax.ShapeDtypeStruct((3, rows, cols), BF16)] * n,
        scratch_shapes=[buf, buf, buf, sems, sems, sems, sems],
        compiler_params=pltpu.CompilerParams(vmem_limit_bytes=VMEM_LIMIT),
    )(*gs)
    return list(zip(outs[:n], outs[n:]))


_HBM_SPEC = pl.BlockSpec(memory_space=pltpu.HBM)
_SEM_SPEC = pl.BlockSpec(memory_space=pltpu.SEMAPHORE)
_SIDE_EFFECT = pltpu.SideEffectType.DATAFLOW_SIDE_EFFECTING


def _exchange_start(name, srcs, lands, plan, n_copies):
    nb = len(srcs) + len(lands)

    def body(*refs):
        bufs, send_sems, recv_sems, token = refs[:nb], refs[nb], refs[nb + 1], refs[-1]
        for cp in plan(bufs[:len(srcs)], bufs[len(srcs):], send_sems, recv_sems):
            cp.start()
        token[...] = jnp.zeros_like(token)

    arrays = list(srcs) + list(lands)
    outs = pl.pallas_call(
        body, name=name,
        out_shape=(pltpu.SemaphoreType.DMA((n_copies,)), pltpu.SemaphoreType.DMA((n_copies,)),
                   *[pltpu.HBM(a.shape, a.dtype) for a in arrays], jax.ShapeDtypeStruct((SUBLANES, LANES), F32)),
        in_specs=[_HBM_SPEC] * nb,
        out_specs=(_SEM_SPEC, _SEM_SPEC, *[_HBM_SPEC] * nb, pl.BlockSpec(memory_space=pltpu.VMEM)),
        input_output_aliases={k: 2 + k for k in range(nb)},
        compiler_params=pltpu.CompilerParams(has_side_effects=_SIDE_EFFECT),
    )(*[pltpu.with_memory_space_constraint(a, pltpu.HBM) for a in arrays])
    return outs[0], outs[1], outs[2:2 + len(srcs)], outs[2 + len(srcs):2 + nb], outs[-1]


def _exchange_wait(name, send_sems, recv_sems, srcs, lands, plan, after):
    nb = len(srcs) + len(lands)
    after = list(after)

    def body(*refs):
        bufs, send_ref, recv_ref = refs[:nb], refs[nb], refs[nb + 1]
        for cp in plan(bufs[:len(srcs)], bufs[len(srcs):], send_ref, recv_ref):
            cp.wait_send()
            cp.wait_recv()

    arrays = list(srcs) + list(lands)
    outs = pl.pallas_call(
        body, name=name,
        out_shape=tuple(pltpu.HBM(a.shape, a.dtype) for a in arrays),
        in_specs=[_HBM_SPEC] * nb + [_SEM_SPEC, _SEM_SPEC] + [pl.BlockSpec(memory_space=pl.ANY)] * len(after),
        out_specs=tuple([_HBM_SPEC] * nb),
        input_output_aliases={k: k for k in range(nb)},
        compiler_params=pltpu.CompilerParams(has_side_effects=_SIDE_EFFECT),
    )(*arrays, send_sems, recv_sems, *after)
    return outs[len(srcs):]


def _gather_plan(axes, sizes):
    def plan(src_refs, land_refs, send_sems, recv_sems):
        x, y, c = _mesh_pos()
        copies = []
        for a, (src, land) in enumerate(zip(src_refs, land_refs)):
            mine = _block_of(land, axes[a], 4 * x + 2 * y + c, sizes[a])
            for k in range(1, N_DEV):
                peer = (1 - x if k & 4 else x, 1 - y if k & 2 else y, 1 - c if k & 1 else c)
                idx = a * (N_DEV - 1) + k - 1
                copies.append(pltpu.make_async_remote_copy(
                    src_ref=src, dst_ref=mine, send_sem=send_sems.at[idx], recv_sem=recv_sems.at[idx],
                    device_id=peer, device_id_type=MESH_IDS))
        return copies
    return plan


def _chip_plan(src_refs, land_refs, send_sems, recv_sems):
    x, y, c = _mesh_pos()
    copies = []
    for a, (src, land) in enumerate(zip(src_refs, land_refs)):
        for j, chip in enumerate(_other_chips(x, y)):
            copies.append(pltpu.make_async_remote_copy(
                src_ref=src.at[j], dst_ref=land.at[j], send_sem=send_sems.at[3 * a + j],
                recv_sem=recv_sems.at[3 * a + j], device_id=(*chip, c), device_id_type=MESH_IDS))
    return copies


def _own_block_placed(shard, axis, me):
    if axis == 0:
        full = lax.empty((N_DEV,) + shard.shape, shard.dtype)
        return lax.dynamic_update_slice(full, shard[None], (me,) + (0,) * shard.ndim)
    rows, cols = shard.shape

    def body(me_ref, s_ref, o_ref):
        del me_ref
        o_ref[...] = s_ref[...]

    return pl.pallas_call(
        body, name="place_own_columns",
        grid_spec=pltpu.PrefetchScalarGridSpec(
            num_scalar_prefetch=1, grid=(1,),
            in_specs=[pl.BlockSpec((rows, cols), lambda i, me_ref: (0, 0))],
            out_specs=pl.BlockSpec((rows, cols), lambda i, me_ref: (0, me_ref[0]))),
        out_shape=jax.ShapeDtypeStruct((rows, N_DEV * cols), shard.dtype),
    )(jnp.reshape(me, (1,)).astype(jnp.int32), shard)


def _local_step(x, target, mod, win, late_weights, p, grads_ready=None):
    shift1, scale1, gate1, shift2, scale2, gate2 = (mod[k] for k in range(6))

    def after_token(v, token):
        return v if token is None else v + token[0:1, 0:1]
    wa, wx = p["lru_w_a"].astype(BF16), p["lru_w_x"].astype(BF16)
    mask = jnp.tril(jnp.ones((HD, HD), F32))
    wm = (p["sgu_w_s"] * mask).astype(BF16)
    wmt = jnp.swapaxes(wm, 1, 2)
    bst = jnp.transpose(p["sgu_b_s"])

    h1, z = _norm_proj(x, p["norm_mix_g"], scale1, shift1, win, "mix_proj")
    hstate, ya_pre, *rnn_saved = _rnn_fwd(
        z, p["rnn_conv_w"], p["rnn_conv_b"], wa, p["lru_b_a"], wx, p["lru_b_x"], p["lru_lambda"])
    yb_pre, *sgu_saved = _sgu_fwd(z, p["sgu_ln_g"], p["sgu_ln_b"], wm, bst)
    wba, wbb, wout = late_weights("merge", [ya_pre, yb_pre])
    x2, ya, yb, merged, o1 = _merge_fwd(ya_pre, yb_pre, z, x, gate1, wba, wbb, wout)
    wup = late_weights("ffn_up", [x2])
    h2, up_a, up_v, ff, fa, fv = _ffn_proj_mid(
        x2, p["norm_ffn_g"], scale2, shift2, wup, p["ffn_conv_w"], p["ffn_conv_b"])
    wd = late_weights("ffn_down", [ff])
    dx3, loss, d_gfin, d_gate2 = _ffn_out_loss(ff, wd, x2, target, gate2, p["norm_final_g"])

    dact, dval, d_wd, dcb_a, dcb_v = _ffn_down_bwd(dx3, gate2, ff, fa, fv, wd)
    dup, dx2, do1, d_cwf, d_shift2, d_scale2, d_gffn, d_gate1 = _ffn_up_bwd(
        dact, dval, up_a, up_v, p["ffn_conv_w"], wup, x2, dx3, p["norm_ffn_g"], scale2, o1, gate1)
    d_wup = _xt_y(h2, dup, "w_up_grad")
    ready = grads_ready if grads_ready else (lambda stage, big, small: None)
    token = ready("ffn", {"w_up": d_wup, "w_down": d_wd}, {})

    dya, dyb, dz, d_wout, d_win = _out_bwd(do1, wout, merged, ya, yb, z, h1)
    dz, d_win, d_wba, d_cw, d_cb, d_wa, d_ba, d_wx, d_bx, d_lam = _rnn_bwd(
        dya, ya_pre, wba, h1, z, rnn_saved, hstate, dz, d_win, p["rnn_conv_w"], wa, wx,
        after_token(p["lru_lambda"], token))
    small = {
        "rnn_conv_w": d_cw, "rnn_conv_b": d_cb, "lru_w_a": d_wa, "lru_b_a": d_ba, "lru_w_x": d_wx, "lru_b_x": d_bx,
        "lru_lambda": d_lam, "norm_ffn_g": d_gffn, "ffn_conv_w": d_cwf,
        "ffn_conv_b": jnp.concatenate([dcb_a, dcb_v], axis=1), "norm_final_g": d_gfin,
    }
    token = ready("rnn", {}, small)
    dz, d_win, d_wbb, d_ws, d_bst, d_lng, d_lnb = _sgu_bwd(
        dyb, yb_pre, wbb, h1, sgu_saved, dz, d_win, p["sgu_ln_g"], after_token(p["sgu_ln_b"], token), wmt, mask)
    sgu_small = {"sgu_ln_g": d_lng, "sgu_ln_b": d_lnb, "sgu_w_s": d_ws, "sgu_b_s": jnp.transpose(d_bst)}
    mixer = {"w_in": d_win, "w_out": d_wout, "w_branch_a": d_wba, "w_branch_b": d_wbb}
    token = ready("mixer", mixer, sgu_small)
    grad_x, d_shift1, d_scale1, d_gmix = _in_bwd(dz, win, x, dx2, after_token(p["norm_mix_g"], token), scale1)

    small.update(sgu_small)
    small["norm_mix_g"] = d_gmix
    dmod = jnp.stack([d_shift1, d_scale1, d_gate1, d_shift2, d_scale2, d_gate2])
    big = {"w_in": d_win, "w_up": d_wup, "w_branch_a": d_wba, "w_branch_b": d_wbb, "w_out": d_wout, "w_down": d_wd}
    return loss, grad_x, big, small, dmod


LAST_REP = ["b_ada", "norm_mix_g"]
EARLY_REP = ["rnn_conv_b", "lru_w_a", "lru_b_a", "lru_w_x", "lru_b_x", "lru_lambda", "norm_ffn_g", "ffn_conv_b",
             "norm_final_g"]
MID_REP = ["sgu_ln_g", "sgu_ln_b", "sgu_w_s", "sgu_b_s"]
COL_SHARDED = ["rnn_conv_w", "ffn_conv_w"]
SMALL_GROUPS = {"rnn": EARLY_REP + COL_SHARDED, "mixer": MID_REP, "last": LAST_REP}
REPLICATED = LAST_REP + EARLY_REP + MID_REP
SMALL_NAMES = REPLICATED + COL_SHARDED
BIG_NAMES = ["w_in", "w_up", "w_branch_a", "w_branch_b", "w_out", "w_down"]
BIG_AXES = [1, 1, 0, 0, 0, 0]
WEIGHTS = ["w_ada", "b_ada", "norm_mix_g", "w_in", "rnn_conv_w", "rnn_conv_b", "lru_w_a", "lru_b_a", "lru_w_x",
           "lru_b_x", "lru_lambda", "sgu_ln_g", "sgu_ln_b", "sgu_w_s", "sgu_b_s", "w_branch_a", "w_branch_b",
           "w_out", "norm_ffn_g", "w_up", "ffn_conv_w", "ffn_conv_b", "w_down", "norm_final_g"]


def _pack_rows(shape):
    return math.prod(shape) // LANES


def _pack(arrays):
    return jnp.concatenate([a.reshape(-1, LANES) for a in arrays], axis=0)


def kernel(x, c, w_ada, b_ada, norm_mix_g, w_in, rnn_conv_w, rnn_conv_b, lru_w_a, lru_b_a, lru_w_x, lru_b_x, lru_lambda, sgu_ln_g, sgu_ln_b, sgu_w_s, sgu_b_s, w_branch_a, w_branch_b, w_out, norm_ffn_g, w_up, ffn_conv_w, ffn_conv_b, w_down, norm_final_g, loss_target, m_w_ada, m_b_ada, m_norm_mix_g, m_w_in, m_rnn_conv_w, m_rnn_conv_b, m_lru_w_a, m_lru_b_a, m_lru_w_x, m_lru_b_x, m_lru_lambda, m_sgu_ln_g, m_sgu_ln_b, m_sgu_w_s, m_sgu_b_s, m_w_branch_a, m_w_branch_b, m_w_out, m_norm_ffn_g, m_w_up, m_ffn_conv_w, m_ffn_conv_b, m_w_down, m_norm_final_g, v_w_ada, v_b_ada, v_norm_mix_g, v_w_in, v_rnn_conv_w, v_rnn_conv_b, v_lru_w_a, v_lru_b_a, v_lru_w_x, v_lru_b_x, v_lru_lambda, v_sgu_ln_g, v_sgu_ln_b, v_sgu_w_s, v_sgu_b_s, v_w_branch_a, v_w_branch_b, v_w_out, v_norm_ffn_g, v_w_up, v_ffn_conv_w, v_ffn_conv_b, v_w_down, v_norm_final_g):
    given = dict(locals())
    me = 4 * lax.axis_index("x") + 2 * lax.axis_index("y") + lax.axis_index("c")
    ada_cols = w_ada.shape[2]
    conv_cols = {"rnn_conv_w": rnn_conv_w.shape[2], "ffn_conv_w": ffn_conv_w.shape[2]}

    b_cols = lax.dynamic_slice_in_dim(b_ada, me * ada_cols, ada_cols, axis=1)
    (win, cw_rnn, cw_ffn), first_done, c_all, mod_mine = _all_gather(
        [w_in[0].astype(BF16), rnn_conv_w[0], ffn_conv_w[0]], [1, 1, 1], "gather_first", ada=(c, w_ada[0], b_cols))
    c_all = c_all.reshape(N_DEV, D)
    mod_mine = mod_mine.reshape(6, 1, D)

    late_groups = {"merge": (["w_branch_a", "w_branch_b", "w_out"], [0, 0, 0]), "ffn_up": (["w_up"], [1]),
                   "ffn_down": (["w_down"], [0])}
    in_flight, started = {}, first_done[0:1, 0:1]
    for stage, (names, axes) in late_groups.items():
        shards = [(given[n][0] + started).astype(BF16) for n in names]
        plan = _gather_plan(axes, [s.shape[-1] for s in shards])
        send, recv, srcs, lands, token = _exchange_start(
            "gather_start_" + stage, shards, [_own_block_placed(s, ax, me) for s, ax in zip(shards, axes)], plan,
            len(shards) * (N_DEV - 1))
        in_flight[stage] = (send, recv, srcs, lands, plan)
        started = started + token[0:1, 0:1]

    def late_weights(stage, after):
        send, recv, srcs, lands, plan = in_flight[stage]
        full = _exchange_wait("gather_wait_" + stage, send, recv, srcs, lands, plan, after)
        full = [w.reshape(-1, D) if ax == 0 else w for w, ax in zip(full, late_groups[stage][1])]
        return full if len(full) > 1 else full[0]

    mod_mine = mod_mine + started

    reducing, packing = {}, {}

    def start_pack(stage, small):
        pack = _pack([small[n] for n in SMALL_GROUPS[stage]])[None]
        plan = _gather_plan([0], [LANES])
        send, recv, srcs, lands, tok = _exchange_start(
            "small_start_" + stage, [pack], [_own_block_placed(pack, 0, me)], plan, N_DEV - 1)
        packing[stage] = (send, recv, srcs, lands, plan)
        return tok

    def grads_ready(stage, grads, small):
        tokens = [start_pack(stage, small)] if small else []
        if grads:
            tokens.append(start_reduce(stage, grads))
        return sum(tokens[1:], tokens[0])

    def start_reduce(stage, grads):
        names = [n for n in BIG_NAMES if n in grads]
        blocked = {}
        for n in names:
            ax = BIG_AXES[BIG_NAMES.index(n)]
            g = grads[n] if ax == 1 else grads[n].reshape(N_DEV, grads[n].shape[0] // N_DEV, grads[n].shape[1])
            blocked.setdefault((ax, g.shape), []).append((n, g))
        sums = {}
        for (ax, _), group in blocked.items():
            reduced = _sibling_reduce([g for _, g in group], ax, "reduce_sibling_" + "_".join(n for n, _ in group))
            sums.update({n: r for (n, _), r in zip(group, reduced)})
        sums = [sums[n] for n in names]
        pays = [pay for _, pay in sums]
        send, recv, srcs, lands, tok = _exchange_start(
            "reduce_start_" + stage, pays, [lax.empty(p_.shape, p_.dtype) for p_ in pays], _chip_plan, 3 * len(pays))
        reducing[stage] = (names, [own for own, _ in sums], send, recv, srcs, lands)
        return tok

    p = {n: given[n][0] for n in REPLICATED if n not in ("b_ada", "norm_final_g")}
    p = {n: (a.reshape(1, -1) if a.ndim == 1 else a) for n, a in p.items()}
    p["rnn_conv_w"], p["ffn_conv_w"] = cw_rnn, cw_ffn
    p["norm_final_g"] = norm_final_g.reshape(1, D)
    loss, grad_x, _, small, dmod = _local_step(x[0], loss_target[0], mod_mine, win, late_weights, p, grads_ready)

    small["b_ada"] = dmod.reshape(1, 6 * D)
    rows_of = {n: _pack_rows(small[n].shape) for n in SMALL_NAMES}
    (last,), _ = _all_gather([_pack([small[n] for n in LAST_REP])[None]], [0], "gather_small")
    gathered = {"last": last}
    for stage, (send, recv, srcs, lands, plan) in packing.items():
        (gathered[stage],) = _exchange_wait("small_wait_" + stage, send, recv, srcs, lands, plan, [grad_x])
    gathered = {k: v.reshape(N_DEV, -1, LANES) for k, v in gathered.items()}

    out = {}
    for stage, (names, owns, send, recv, srcs, lands) in reducing.items():
        landed = _exchange_wait("reduce_wait_" + stage, send, recv, srcs, lands, _chip_plan, [last])
        for n, own, got in zip(names, owns, landed):
            out[n] = _adamw(given[n][0], given["m_" + n][0], given["v_" + n][0], [own, got], "adamw_" + n)

    dmod_all = gathered["last"][:, :rows_of["b_ada"]].reshape(N_DEV, 6 * D)
    dmod_cols = lax.dynamic_slice_in_dim(dmod_all, me * ada_cols, ada_cols, axis=1)
    out["w_ada"] = _adamw(w_ada[0], m_w_ada[0], v_w_ada[0], [_ada_grad(c_all, dmod_cols)], "adamw_w_ada")

    def rows_form(a):
        return a.reshape(1, -1) if a.size // a.shape[-1] == 1 or a.ndim == 1 else a.reshape(-1, LANES)

    for stage, names in (("last", LAST_REP), ("rnn", EARLY_REP), ("mixer", MID_REP)):
        out.update(_adamw_group(names, *[[rows_form(given[pre + n]) for n in names] for pre in ("", "m_", "v_")],
                                gathered[stage], "adamw_small_" + stage))

    row0 = sum(rows_of[n] for n in EARLY_REP)
    for n in COL_SHARDED:
        full = gathered["rnn"][:, row0:row0 + rows_of[n]].reshape(N_DEV, small[n].shape[0], small[n].shape[1])
        mine = lax.dynamic_slice_in_dim(full, me * conv_cols[n], conv_cols[n], axis=2)
        out[n] = _adamw(given[n][0], given["m_" + n][0], given["v_" + n][0], [mine], "adamw_" + n)
        row0 += rows_of[n]

    total = lax.psum(loss[0, 0], ("x", "y", "c"))
    results = [total, grad_x[None]]
    for kind in range(4):
        results += [out[n][kind].reshape(given[n].shape) for n in WEIGHTS]
    return tuple(results)
```

```python
import math

import jax
import jax.numpy as jnp
from jax import lax
from jax.experimental import pallas as pl
from jax.experimental.pallas import tpu as pltpu

F32 = jnp.float32
BF16 = jnp.bfloat16
MESH_IDS = pl.DeviceIdType.MESH

D = 1024
NH = 8
HD = 128
NCOL_IN = 6 * D
DFF = 3 * D
N_DEV = 8
EPS = 1e-6
LRU_C = 8.0
ADAM_LR, ADAM_B1, ADAM_B2, ADAM_EPS, ADAM_WD, ADAM_STEP = 0.001, 0.9, 0.999, 1e-08, 0.01, 10

SUBLANES = 8
LANES = 128
HALO = 16
VMEM_LIMIT = 56 * 1024 * 1024
GELU_K = math.sqrt(2.0 / math.pi)
GELU_C = 0.044715


def _cparams(n_axes):
    return pltpu.CompilerParams(dimension_semantics=("arbitrary",) * n_axes, vmem_limit_bytes=VMEM_LIMIT)


def _const_spec(shape, single_buffer=False):
    nd = len(shape)
    if single_buffer:
        return pl.BlockSpec(shape, lambda *_: (0,) * nd, pipeline_mode=pl.Buffered(1))
    return pl.BlockSpec(shape, lambda *_: (0,) * nd)


def _tile_big(t):
    return min(512, t)


def _tile_seq(t):
    return min(256, t)


def _row_tile(rows, cols):
    cap = max(SUBLANES, (2 * 1024 * 1024) // (4 * cols) // SUBLANES * SUBLANES)
    if rows <= cap:
        return rows
    return next(tr for tr in range(cap, 0, -SUBLANES) if rows % tr == 0)


def _gelu_t(x):
    x2 = x * x
    t = jnp.tanh(x * (GELU_K + (GELU_K * GELU_C) * x2))
    hx = 0.5 * x
    return hx + hx * t, (x2, hx, t)


def _gelu_grad(shared):
    x2, hx, t = shared
    return (0.5 + 0.5 * t) + (hx * (1.0 - t * t)) * (GELU_K + (3.0 * GELU_K * GELU_C) * x2)


def _sigmoid(x):
    return 1.0 / (1.0 + jnp.exp(-x))


def _log_sigmoid(x):
    return -(jnp.maximum(-x, 0.0) + jnp.log1p(jnp.exp(-jnp.abs(x))))


def _row_iota(cols):
    return lax.broadcasted_iota(jnp.int32, (SUBLANES, cols), 0)


def _shift_down(x, k, prev8):
    if k == 0:
        return x
    r = pltpu.roll(x, k, 0)
    p = pltpu.roll(prev8, k, 0)
    head = jnp.where(_row_iota(x.shape[1]) < k, p, r[:SUBLANES])
    return jnp.concatenate([head, r[SUBLANES:]], axis=0)


def _shift_up(x, k, next8):
    if k == 0:
        return x
    n = x.shape[0]
    r = pltpu.roll(x, n - k, 0)
    q = pltpu.roll(next8, SUBLANES - k, 0)
    tail = jnp.where(_row_iota(x.shape[1]) >= SUBLANES - k, q, r[n - SUBLANES:])
    return jnp.concatenate([r[:n - SUBLANES], tail], axis=0)


def _heads_nn(x_bf, w_ref):
    return jnp.concatenate(
        [jnp.dot(x_bf[:, h * HD:(h + 1) * HD], w_ref[h], preferred_element_type=F32) for h in range(NH)], axis=1)


def _heads_nt(x_bf, w_ref):
    return jnp.concatenate(
        [lax.dot_general(x_bf[:, h * HD:(h + 1) * HD], w_ref[h], (((1,), (1,)), ((), ())), preferred_element_type=F32)
         for h in range(NH)], axis=1)


def _dot_nt(a, b):
    return lax.dot_general(a, b, (((1,), (1,)), ((), ())), preferred_element_type=F32)


def _dot_tn(a, b):
    return lax.dot_general(a, b, (((0,), (0,)), ((), ())), preferred_element_type=F32)


def _colsum(x):
    return jnp.sum(x, axis=0, keepdims=True)


def _prev_halo_map(tm, col):
    return lambda i, *_: (jnp.maximum(i * (tm // HALO) - 1, 0), col)


def _norm_proj(x, g, scale, shift, w, name):
    t, n = x.shape[0], w.shape[1]
    tm = _tile_big(t)

    def body(x_ref, g_ref, sc_ref, sh_ref, w_ref, h_ref, z_ref):
        xv = x_ref[...]
        r = lax.rsqrt(jnp.mean(xv * xv, axis=-1, keepdims=True) + EPS)
        hb = ((xv * r * g_ref[...]) * (1.0 + sc_ref[...]) + sh_ref[...]).astype(BF16)
        h_ref[...] = hb
        for c0 in range(0, n, D):
            z_ref[:, c0:c0 + D] = jnp.dot(hb, w_ref[:, c0:c0 + D], preferred_element_type=F32).astype(BF16)

    vec = _const_spec((1, D))
    return pl.pallas_call(
        body, name=name, grid=(t // tm,),
        in_specs=[pl.BlockSpec((tm, D), lambda i: (i, 0)), vec, vec, vec, _const_spec((D, n), True)],
        out_specs=[pl.BlockSpec((tm, D), lambda i: (i, 0)), pl.BlockSpec((tm, n), lambda i: (i, 0))],
        out_shape=[jax.ShapeDtypeStruct((t, D), BF16), jax.ShapeDtypeStruct((t, n), BF16)],
        compiler_params=_cparams(1),
    )(x, g, scale, shift, w)


def _lru_gates(xc, wa_ref, ba, wx_ref, bx, ls):
    xb = xc.astype(BF16)
    ra = _sigmoid(_heads_nn(xb, wa_ref) + ba)
    ia = _sigmoid(_heads_nn(xb, wx_ref) + bx)
    la = LRU_C * ra * ls
    a = jnp.exp(la)
    mult = jnp.sqrt(-jnp.tanh(la) * (1.0 + a * a))
    return ra, ia, a, mult


def _conv4(xr, prev8, cw_ref, cb):
    return (cb + cw_ref[3:4, :] * xr + cw_ref[2:3, :] * _shift_down(xr, 1, prev8)
            + cw_ref[1:2, :] * _shift_down(xr, 2, prev8) + cw_ref[0:1, :] * _shift_down(xr, 3, prev8))


def _rnn_fwd(z, cw, cb, wa, ba, wx, bx, lam):
    t = z.shape[0]
    tm = _tile_seq(t)
    ngrp = tm // SUBLANES

    def body(xr_ref, xp_ref, gr_ref, cw_ref, cb_ref, wa_ref, ba_ref, wx_ref, bx_ref, lam_ref,
             h_ref, ya_ref, xc_ref, ra_ref, ia_ref, gg_ref, hg_ref, carry_ref, a_scr, u_scr):
        i = pl.program_id(0)

        @pl.when(i == 0)
        def _():
            carry_ref[...] = jnp.zeros_like(carry_ref)

        xr = xr_ref[...].astype(F32)
        prev8 = jnp.where(i == 0, 0.0, xp_ref[...].astype(F32)[HALO - SUBLANES:])
        xc = _conv4(xr, prev8, cw_ref, cb_ref[...])
        ra, ia, a, mult = _lru_gates(xc, wa_ref, ba_ref[...], wx_ref, bx_ref[...], _log_sigmoid(lam_ref[...]))
        xc_ref[...] = xc.astype(BF16)
        ra_ref[...] = ra.astype(BF16)
        ia_ref[...] = ia.astype(BF16)
        a_scr[...] = a
        u_scr[...] = mult * (ia * xc)
        row = _row_iota(D)

        def grp(j, carry):
            r0 = pl.multiple_of(j * SUBLANES, SUBLANES)
            av = a_scr[pl.ds(r0, SUBLANES), :]
            uv = u_scr[pl.ds(r0, SUBLANES), :]
            for d in (1, 2, 4):
                m = row >= d
                uv = jnp.where(m, av * pltpu.roll(uv, d, 0) + uv, uv)
                av = jnp.where(m, av * pltpu.roll(av, d, 0), av)
            hv = uv + av * carry
            h_ref[pl.ds(r0, SUBLANES), :] = hv
            return hv[SUBLANES - 1:SUBLANES, :]

        carry_ref[0:1, :] = lax.fori_loop(0, ngrp, grp, carry_ref[0:1, :])
        grv = gr_ref[...].astype(F32)
        gg, tg = _gelu_t(grv)
        hv = h_ref[...]
        ya_ref[...] = (hv * gg).astype(BF16)
        gg_ref[...] = gg.astype(BF16)
        hg_ref[...] = (hv * _gelu_grad(tg)).astype(BF16)

    vec = _const_spec((1, D))
    wspec = _const_spec((NH, HD, HD))
    tile = pl.BlockSpec((tm, D), lambda i: (i, 0))
    bshape = jax.ShapeDtypeStruct((t, D), BF16)
    return pl.pallas_call(
        body, name="rnn_fwd", grid=(t // tm,),
        in_specs=[tile, pl.BlockSpec((HALO, D), _prev_halo_map(tm, 0)),
                  pl.BlockSpec((tm, D), lambda i: (i, 1)), _const_spec((4, D)), vec, wspec, vec, wspec, vec, vec],
        out_specs=[tile] * 7,
        out_shape=[jax.ShapeDtypeStruct((t, D), F32)] + [bshape] * 6,
        scratch_shapes=[pltpu.VMEM((SUBLANES, D), F32), pltpu.VMEM((tm, D), F32), pltpu.VMEM((tm, D), F32)],
        compiler_params=_cparams(1),
    )(z, z, z, cw, cb, wa, ba, wx, bx, lam)


def _sgu_fwd(z, lng, lnb, wm, bst):
    t = z.shape[0]
    tm = _tile_seq(t)

    def body(zu_ref, zv_ref, lng_ref, lnb_ref, wm_ref, bst_ref, yb_ref, gu_ref, mg_ref, vh_ref, gpv_ref, rstd_ref):
        gu, su = _gelu_t(zu_ref[...].astype(F32))
        gv, sv = _gelu_t(zv_ref[...].astype(F32))
        mu = jnp.mean(gv, axis=-1, keepdims=True)
        cen = gv - mu
        rstd = lax.rsqrt(jnp.mean(cen * cen, axis=-1, keepdims=True) + EPS)
        vhat = cen * rstd
        vb = (vhat * lng_ref[...] + lnb_ref[...]).astype(BF16)
        rows = []
        for b0 in range(0, tm, HD):
            rows.append(jnp.concatenate(
                [jnp.dot(wm_ref[g], vb[b0:b0 + HD, g * HD:(g + 1) * HD], preferred_element_type=F32)
                 + bst_ref[:, g:g + 1] for g in range(NH)], axis=1))
        mixed = jnp.concatenate(rows, axis=0) if len(rows) > 1 else rows[0]
        yb_ref[...] = (gu * mixed).astype(BF16)
        gu_ref[...] = gu.astype(BF16)
        mg_ref[...] = (mixed * _gelu_grad(su)).astype(BF16)
        vh_ref[...] = vhat.astype(BF16)
        gpv_ref[...] = _gelu_grad(sv).astype(BF16)
        rstd_ref[...] = rstd

    vec = _const_spec((1, D))
    tile = pl.BlockSpec((tm, D), lambda i: (i, 0))
    bshape = jax.ShapeDtypeStruct((t, D), BF16)
    return pl.pallas_call(
        body, name="sgu_fwd", grid=(t // tm,),
        in_specs=[pl.BlockSpec((tm, D), lambda i: (i, 2)), pl.BlockSpec((tm, D), lambda i: (i, 3)), vec, vec,
                  _const_spec((NH, HD, HD)), _const_spec((HD, NH))],
        out_specs=[tile] * 5 + [pl.BlockSpec((tm, 1), lambda i: (i, 0))],
        out_shape=[bshape] * 5 + [jax.ShapeDtypeStruct((t, 1), F32)],
        compiler_params=_cparams(1),
    )(z, z, lng, lnb, wm, bst)


def _merge_fwd(ya_pre, yb_pre, z, x, gate1, wba, wbb, wout):
    t = x.shape[0]
    tm = _tile_big(t)

    def body(yap_ref, ybp_ref, ga_ref, gb_ref, x_ref, g1_ref, wba_ref, wbb_ref, wo_ref,
             x2_ref, ya_ref, yb_ref, o1_ref):
        ya = jnp.dot(yap_ref[...], wba_ref[...], preferred_element_type=F32)
        yb = jnp.dot(ybp_ref[...], wbb_ref[...], preferred_element_type=F32)
        merged = _sigmoid(ga_ref[...].astype(F32)) * ya + _sigmoid(gb_ref[...].astype(F32)) * yb
        o1 = jnp.dot(merged.astype(BF16), wo_ref[...], preferred_element_type=F32)
        x2_ref[...] = x_ref[...] + g1_ref[...] * o1
        ya_ref[...] = ya.astype(BF16)
        yb_ref[...] = yb.astype(BF16)
        o1_ref[...] = o1.astype(BF16)

    tile = pl.BlockSpec((tm, D), lambda i: (i, 0))
    wspec = _const_spec((D, D))
    bshape = jax.ShapeDtypeStruct((t, D), BF16)
    return pl.pallas_call(
        body, name="merge_fwd", grid=(t // tm,),
        in_specs=[tile, tile, pl.BlockSpec((tm, D), lambda i: (i, 4)), pl.BlockSpec((tm, D), lambda i: (i, 5)),
                  tile, _const_spec((1, D)), wspec, wspec, wspec],
        out_specs=[tile] * 4,
        out_shape=[jax.ShapeDtypeStruct((t, D), F32), bshape, bshape, bshape],
        compiler_params=_cparams(1),
    )(ya_pre, yb_pre, z, z, x, gate1, wba, wbb, wout)


def _conv3(u, prev8, cw_ref, cb):
    return cb + cw_ref[2:3, :] * u + cw_ref[1:2, :] * _shift_down(u, 1, prev8) + cw_ref[0:1, :] * _shift_down(u, 2, prev8)


def _ffn_proj_mid(x2, g, scale, shift, w, cw, cb):
    t = x2.shape[0]
    tm = _tile_big(t)
    nc = DFF // D

    def body(x_ref, g_ref, sc_ref, sh_ref, wa_ref, wv_ref, cwa_ref, cwv_ref, cba_ref, cbv_ref,
             h_ref, upa_ref, upv_ref, ff_ref, fa_ref, fv_ref, hb_scr, prev_ref):
        i, c = pl.program_id(0), pl.program_id(1)

        @pl.when(i == 0)
        def _():
            prev_ref[c] = jnp.zeros((2, SUBLANES, D), F32)

        @pl.when(c == 0)
        def _():
            xv = x_ref[...]
            r = lax.rsqrt(jnp.mean(xv * xv, axis=-1, keepdims=True) + EPS)
            hb_scr[...] = ((xv * r * g_ref[...]) * (1.0 + sc_ref[...]) + sh_ref[...]).astype(BF16)
            h_ref[...] = hb_scr[...]

        hb = hb_scr[...]
        halves = []
        for s, (w_ref, up_ref, cw_ref, cb_ref) in enumerate(((wa_ref, upa_ref, cwa_ref, cba_ref),
                                                             (wv_ref, upv_ref, cwv_ref, cbv_ref))):
            u = jnp.dot(hb, w_ref[...], preferred_element_type=F32)
            up_ref[...] = u.astype(BF16)
            halves.append(_conv3(u, prev_ref[c, s], cw_ref, cb_ref[...]))
            prev_ref[c, s] = u[tm - SUBLANES:]
        act, val = halves
        ga, ta = _gelu_t(act)
        ff_ref[...] = (ga * val).astype(BF16)
        fa_ref[...] = (val * _gelu_grad(ta)).astype(BF16)
        fv_ref[...] = ga.astype(BF16)

    def cols(rows, off):
        return pl.BlockSpec((rows, D), lambda i, c: (0, off + c))

    vec = pl.BlockSpec((1, D), lambda i, c: (0, 0))
    row_tile = pl.BlockSpec((tm, D), lambda i, c: (i, 0))
    chunk = pl.BlockSpec((tm, D), lambda i, c: (i, c))
    hshape = jax.ShapeDtypeStruct((t, DFF), BF16)
    return pl.pallas_call(
        body, name="ffn_proj_mid", grid=(t // tm, nc),
        in_specs=[row_tile, vec, vec, vec, cols(D, 0), cols(D, nc), cols(3, 0), cols(3, nc), cols(1, 0), cols(1, nc)],
        out_specs=[row_tile, chunk, chunk, chunk, chunk, chunk],
        out_shape=[jax.ShapeDtypeStruct((t, D), BF16), hshape, hshape, hshape, hshape, hshape],
        scratch_shapes=[pltpu.VMEM((tm, D), BF16), pltpu.VMEM((nc, 2, SUBLANES, D), F32)],
        compiler_params=_cparams(2),
    )(x2, g, scale, shift, w, w, cw, cw, cb, cb)


def _ffn_out_loss(ff, wd, x2, target, gate2, gfin):
    t = x2.shape[0]
    tm = _tile_big(t)

    def body(ff_ref, wd_ref, x2_ref, tg_ref, g2_ref, gf_ref, dx3_ref, loss_ref, dgf_ref, dg2_ref):
        @pl.when(pl.program_id(0) == 0)
        def _():
            loss_ref[...] = jnp.zeros_like(loss_ref)
            dgf_ref[...] = jnp.zeros_like(dgf_ref)
            dg2_ref[...] = jnp.zeros_like(dg2_ref)

        o2 = jnp.dot(ff_ref[...], wd_ref[...], preferred_element_type=F32)
        x3 = x2_ref[...] + g2_ref[...] * o2
        r = lax.rsqrt(jnp.mean(x3 * x3, axis=-1, keepdims=True) + EPS)
        xhat = x3 * r
        err = xhat * gf_ref[...] - tg_ref[...]
        loss_ref[...] += 0.5 * jnp.sum(jnp.mean(err * err, axis=-1, keepdims=True), axis=0, keepdims=True)
        dy = err * (1.0 / D)
        dgf_ref[...] += _colsum(dy * xhat)
        dxh = dy * gf_ref[...]
        dx3 = r * (dxh - xhat * jnp.mean(dxh * xhat, axis=-1, keepdims=True))
        dx3_ref[...] = dx3
        dg2_ref[...] += _colsum(dx3 * o2)

    tile = pl.BlockSpec((tm, D), lambda i: (i, 0))
    vec = _const_spec((1, D))
    return pl.pallas_call(
        body, name="ffn_out_loss", grid=(t // tm,),
        in_specs=[pl.BlockSpec((tm, DFF), lambda i: (i, 0)), _const_spec((DFF, D), True), tile, tile, vec, vec],
        out_specs=[tile, _const_spec((1, 1)), vec, vec],
        out_shape=[jax.ShapeDtypeStruct((t, D), F32), jax.ShapeDtypeStruct((1, 1), F32),
                   jax.ShapeDtypeStruct((1, D), F32), jax.ShapeDtypeStruct((1, D), F32)],
        compiler_params=_cparams(1),
    )(ff, wd, x2, target, gate2, gfin)


def _ffn_down_bwd(dx3, gate2, ff, fa, fv, wd):
    t = dx3.shape[0]
    tm = min(1024, t)
    nc = DFF // D

    def body(dx3_ref, g2_ref, ff_ref, fa_ref, fv_ref, wd_ref, da_ref, dv_ref, dwd_ref, dcba_ref, dcbv_ref):
        @pl.when(pl.program_id(1) == 0)
        def _():
            for r in (dwd_ref, dcba_ref, dcbv_ref):
                r[...] = jnp.zeros_like(r)

        do2 = (dx3_ref[...] * g2_ref[...]).astype(BF16)
        dwd_ref[...] += _dot_tn(ff_ref[...], do2)
        dff = _dot_nt(do2, wd_ref[...])
        dact = dff * fa_ref[...].astype(F32)
        dval = dff * fv_ref[...].astype(F32)
        da_ref[...] = dact.astype(BF16)
        dv_ref[...] = dval.astype(BF16)
        dcba_ref[...] += _colsum(dact)
        dcbv_ref[...] += _colsum(dval)

    blk = pl.BlockSpec((tm, D), lambda c, i: (i, c))
    vec = pl.BlockSpec((1, D), lambda c, i: (0, c))
    return pl.pallas_call(
        body, name="ffn_down_bwd", grid=(nc, t // tm),
        in_specs=[pl.BlockSpec((tm, D), lambda c, i: (i, 0)), pl.BlockSpec((1, D), lambda c, i: (0, 0)),
                  blk, blk, blk, pl.BlockSpec((D, D), lambda c, i: (c, 0))],
        out_specs=[blk, blk, pl.BlockSpec((D, D), lambda c, i: (c, 0)), vec, vec],
        out_shape=[jax.ShapeDtypeStruct((t, DFF), BF16), jax.ShapeDtypeStruct((t, DFF), BF16),
                   jax.ShapeDtypeStruct((DFF, D), F32),
                   jax.ShapeDtypeStruct((1, DFF), F32), jax.ShapeDtypeStruct((1, DFF), F32)],
        compiler_params=_cparams(2),
    )(dx3, gate2, ff, fa, fv, wd)


def _modnorm_bwd(dh, xv, g, scale):
    r = lax.rsqrt(jnp.mean(xv * xv, axis=-1, keepdims=True) + EPS)
    xhat = xv * r
    dxn = dh * (1.0 + scale)
    dxh = dxn * g
    dx = r * (dxh - xhat * jnp.mean(dxh * xhat, axis=-1, keepdims=True))
    return dx, _colsum(dh), _colsum(dh * (xhat * g)), _colsum(dxn * xhat)


def _ffn_up_bwd(dact, dval, up_a, up_v, cw, wup, x2, dx3, gffn, scale2, o1, gate1):
    t = x2.shape[0]
    tm = _tile_seq(t)
    nt = t // tm
    nc = DFF // D

    def body(da_ref, dan_ref, dv_ref, dvn_ref, ua_ref, uv_ref, cw_ref, w_ref, x2_ref, dx3_ref, g_ref, sc_ref, o1_ref, g1_ref,
             dup_ref, dx2_ref, do1_ref, dcw_ref, dsh_ref, dsc_ref, dg_ref, dg1_ref):
        i = pl.program_id(0)

        @pl.when(i == 0)
        def _():
            for r in (dcw_ref, dsh_ref, dsc_ref, dg_ref, dg1_ref):
                r[...] = jnp.zeros_like(r)

        last = i == nt - 1
        dh = jnp.zeros((tm, D), F32)
        for half, (d_ref, dn_ref, u_ref) in enumerate(((da_ref, dan_ref, ua_ref), (dv_ref, dvn_ref, uv_ref))):
            nxt = jnp.where(last, 0.0, dn_ref[...].astype(F32)[:SUBLANES])
            for c in range(nc):
                c0 = half * DFF + c * D
                dv = d_ref[:, c * D:(c + 1) * D].astype(F32)
                nx = nxt[:, c * D:(c + 1) * D]
                taps = (_shift_up(dv, 2, nx), _shift_up(dv, 1, nx), dv)
                dup = (cw_ref[2:3, c0:c0 + D] * taps[2] + cw_ref[1:2, c0:c0 + D] * taps[1]
                       + cw_ref[0:1, c0:c0 + D] * taps[0]).astype(BF16)
                upv = u_ref[:, c * D:(c + 1) * D].astype(F32)
                for k in range(3):
                    dcw_ref[k:k + 1, c0:c0 + D] += _colsum(taps[k] * upv)
                dup_ref[:, c0:c0 + D] = dup
                dh = dh + _dot_nt(dup, w_ref[:, c0:c0 + D])
        dxn, dsh, dsc, dg = _modnorm_bwd(dh, x2_ref[...], g_ref[...], sc_ref[...])
        dx2 = dx3_ref[...] + dxn
        dx2_ref[...] = dx2
        do1_ref[...] = (dx2 * g1_ref[...]).astype(BF16)
        dsh_ref[...] += dsh
        dsc_ref[...] += dsc
        dg_ref[...] += dg
        dg1_ref[...] += _colsum(dx2 * o1_ref[...].astype(F32))

    tile = pl.BlockSpec((tm, D), lambda i: (i, 0))
    wide = pl.BlockSpec((tm, DFF), lambda i: (i, 0))
    nxt = pl.BlockSpec((HALO, DFF), lambda i: (jnp.minimum((i + 1) * (tm // HALO), t // HALO - 1), 0))
    vec = _const_spec((1, D))
    vshape = jax.ShapeDtypeStruct((1, D), F32)
    return pl.pallas_call(
        body, name="ffn_up_bwd", grid=(nt,),
        in_specs=[wide, nxt, wide, nxt, wide, wide,
                  _const_spec((3, 2 * DFF)), _const_spec((D, 2 * DFF), True),
                  tile, tile, vec, vec, tile, vec],
        out_specs=[pl.BlockSpec((tm, 2 * DFF), lambda i: (i, 0)), tile, tile, _const_spec((3, 2 * DFF)),
                   vec, vec, vec, vec],
        out_shape=[jax.ShapeDtypeStruct((t, 2 * DFF), BF16), jax.ShapeDtypeStruct((t, D), F32),
                   jax.ShapeDtypeStruct((t, D), BF16), jax.ShapeDtypeStruct((3, 2 * DFF), F32),
                   vshape, vshape, vshape, vshape],
        compiler_params=_cparams(1),
    )(dact, dact, dval, dval, up_a, up_v, cw, wup, x2, dx3, gffn, scale2, o1, gate1)


def _xt_y(a, b, name):
    t, k = a.shape
    n = b.shape[1]
    tm = min(1024, t)
    bn = 1536 if n % 1536 == 0 else D

    def body(a_ref, b_ref, o_ref):
        @pl.when(pl.program_id(1) == 0)
        def _():
            o_ref[...] = jnp.zeros_like(o_ref)

        o_ref[...] += _dot_tn(a_ref[...], b_ref[...])

    return pl.pallas_call(
        body, name=name, grid=(n // bn, t // tm),
        in_specs=[pl.BlockSpec((tm, k), lambda j, i: (i, 0)), pl.BlockSpec((tm, bn), lambda j, i: (i, j))],
        out_specs=pl.BlockSpec((k, bn), lambda j, i: (0, j)),
        out_shape=jax.ShapeDtypeStruct((k, n), F32),
        compiler_params=_cparams(2),
    )(a, b)


def _acc_spec(shape, index):
    return pl.BlockSpec(shape, lambda *_: index, pipeline_mode=pl.Buffered(1))


def _out_bwd(do1, wout, ya, yb, z, h1):
    t = do1.shape[0]
    tm = _tile_big(t)

    def body(do1_ref, wo_ref, ya_ref, yb_ref, ga_ref, gb_ref, h1_ref,
             dya_ref, dyb_ref, dz_ref, dwo_ref, dwin_ref):
        @pl.when(pl.program_id(0) == 0)
        def _():
            dwo_ref[...] = jnp.zeros_like(dwo_ref)
            dwin_ref[...] = jnp.zeros_like(dwin_ref)

        do1v = do1_ref[...]
        sa = _sigmoid(ga_ref[...].astype(F32))
        sb = _sigmoid(gb_ref[...].astype(F32))
        ya = ya_ref[...].astype(F32)
        yb = yb_ref[...].astype(F32)
        dwo_ref[...] += _dot_tn((sa * ya + sb * yb).astype(BF16), do1v)
        dm = _dot_nt(do1v, wo_ref[...])
        dya_ref[...] = (dm * sa).astype(BF16)
        dyb_ref[...] = (dm * sb).astype(BF16)
        dga = (dm * ya * sa * (1.0 - sa)).astype(BF16)
        dgb = (dm * yb * sb * (1.0 - sb)).astype(BF16)
        dz_ref[:, 0:D] = dga
        dz_ref[:, D:2 * D] = dgb
        h1v = h1_ref[...]
        dwin_ref[:, 0:D] += _dot_tn(h1v, dga)
        dwin_ref[:, D:2 * D] += _dot_tn(h1v, dgb)

    tile = pl.BlockSpec((tm, D), lambda i: (i, 0))
    bshape = jax.ShapeDtypeStruct((t, D), BF16)
    return pl.pallas_call(
        body, name="out_bwd", grid=(t // tm,),
        in_specs=[tile, _const_spec((D, D), True), tile, tile,
                  pl.BlockSpec((tm, D), lambda i: (i, 4)), pl.BlockSpec((tm, D), lambda i: (i, 5)), tile],
        out_specs=[tile, tile, pl.BlockSpec((tm, 2 * D), lambda i: (i, 2)), _acc_spec((D, D), (0, 0)),
                   _acc_spec((D, 2 * D), (0, 2))],
        out_shape=[bshape, bshape, jax.ShapeDtypeStruct((t, NCOL_IN), BF16), jax.ShapeDtypeStruct((D, D), F32),
                   jax.ShapeDtypeStruct((D, NCOL_IN), F32)],
        compiler_params=_cparams(1),
    )(do1, wout, ya, yb, z, z, h1)


def _rnn_bwd(dya, ya_pre, wba, h1, z, saved, h, dz, dwin, cw, wa, wx, lam):
    t = z.shape[0]
    tm = _tile_seq(t)
    nt = t // tm
    ngrp = tm // SUBLANES
    hpt = tm // HALO

    def body(dya_ref, yap_ref, wba_ref, h1_ref, xr_ref, xc_ref, ra_ref, ia_ref, gg_ref, hg_ref, h_ref, hp_ref,
             dz_any, dwin_any, cw_ref, wa_ref, wx_ref, lam_ref,
             dz_ref, dwin_ref, dwba_ref, dcw_ref, dcb_ref, dwa_ref, dba_ref, dwx_ref, dbx_ref, dlam_ref,
             a_first, g_first, dxc_first, b_scr, d_scr, g_scr):
        del dz_any, dwin_any
        i = pl.program_id(0)

        @pl.when(i == 0)
        def _():
            for r in (dwin_ref, dwba_ref, dcw_ref, dcb_ref, dwa_ref, dba_ref, dwx_ref, dbx_ref, dlam_ref,
                      a_first, g_first, dxc_first):
                r[...] = jnp.zeros_like(r)

        dya_v = dya_ref[...]
        dwba_ref[...] += _dot_tn(yap_ref[...], dya_v)
        dyap_v = _dot_nt(dya_v, wba_ref[...])
        h1v = h1_ref[...]

        first_tile = i == nt - 1
        xc = xc_ref[...].astype(F32)
        ra = ra_ref[...].astype(F32)
        ia = ia_ref[...].astype(F32)
        lam_v = lam_ref[...]
        ls = _log_sigmoid(lam_v)
        la = LRU_C * ra * ls
        a = jnp.exp(la)
        mult = jnp.sqrt(-jnp.tanh(la) * (1.0 + a * a))
        hprev8 = jnp.where(first_tile, 0.0, hp_ref[...][HALO - SUBLANES:])
        h_prev = _shift_down(h_ref[...], 1, hprev8)
        dgr = (dyap_v * hg_ref[...].astype(F32)).astype(BF16)
        dz_ref[:, D:2 * D] = dgr
        dwin_ref[:, D:2 * D] += _dot_tn(h1v, dgr)

        b_scr[...] = _shift_up(a, 1, a_first[...])
        d_scr[...] = dyap_v * gg_ref[...].astype(F32)
        row = _row_iota(D)

        def grp(jj, carry):
            r0 = pl.multiple_of((ngrp - 1 - jj) * SUBLANES, SUBLANES)
            bv = b_scr[pl.ds(r0, SUBLANES), :]
            dv = d_scr[pl.ds(r0, SUBLANES), :]
            for d in (1, 2, 4):
                m = row < SUBLANES - d
                dv = jnp.where(m, dv + bv * pltpu.roll(dv, SUBLANES - d, 0), dv)
                bv = jnp.where(m, bv * pltpu.roll(bv, SUBLANES - d, 0), bv)
            gv = dv + bv * carry
            g_scr[pl.ds(r0, SUBLANES), :] = gv
            return gv[0:1, :]

        lax.fori_loop(0, ngrp, grp, g_first[0:1, :])
        g = g_scr[...]
        a_first[...] = a[:SUBLANES]
        g_first[...] = g[:SUBLANES]

        da = g * h_prev
        gx = g * xc
        dmult = gx * ia
        dia = gx * mult
        dxc = g * (mult * ia)
        dla = da * a - dmult * (a * a) / mult
        dra = dla * (LRU_C * ls)
        dlam_ref[...] += _colsum(dla * ra) * (LRU_C * _sigmoid(-lam_v))
        dpa = dra * ra * (1.0 - ra)
        dpx = dia * ia * (1.0 - ia)
        dba_ref[...] += _colsum(dpa)
        dbx_ref[...] += _colsum(dpx)
        dpab = dpa.astype(BF16)
        dpxb = dpx.astype(BF16)
        xcb = xc_ref[...]
        for hd in range(NH):
            sl = slice(hd * HD, (hd + 1) * HD)
            dwa_ref[hd] += _dot_tn(xcb[:, sl], dpab[:, sl])
            dwx_ref[hd] += _dot_tn(xcb[:, sl], dpxb[:, sl])
        dxc = dxc + _heads_nt(dpab, wa_ref) + _heads_nt(dpxb, wx_ref)

        nxt = dxc_first[...]
        taps = (_shift_up(dxc, 3, nxt), _shift_up(dxc, 2, nxt), _shift_up(dxc, 1, nxt), dxc)
        dxr = cw_ref[0:1, :] * taps[0]
        for k in range(1, 4):
            dxr = dxr + cw_ref[k:k + 1, :] * taps[k]
        dxrb = dxr.astype(BF16)
        dz_ref[:, 0:D] = dxrb
        dwin_ref[:, 0:D] += _dot_tn(h1v, dxrb)
        dxc_first[...] = dxc[:SUBLANES]
        dcb_ref[...] += _colsum(dxc)
        xr = xr_ref[...].astype(F32)
        for k in range(4):
            dcw_ref[k:k + 1, :] += _colsum(taps[k] * xr)

    def rev(col):
        return lambda i: (nt - 1 - i, col)

    vec = _const_spec((1, D))
    wspec = _const_spec((NH, HD, HD))
    vshape = jax.ShapeDtypeStruct((1, D), F32)
    wshape = jax.ShapeDtypeStruct((NH, HD, HD), F32)
    any_spec = pl.BlockSpec(memory_space=pl.ANY)
    tile = pl.BlockSpec((tm, D), rev(0))
    outs = pl.pallas_call(
        body, name="rnn_bwd", grid=(nt,),
        in_specs=[tile, tile, _const_spec((D, D), True), tile, tile, tile, tile, tile, tile, tile, tile,
                  pl.BlockSpec((HALO, D), lambda i: (jnp.maximum((nt - 1 - i) * hpt - 1, 0), 0)),
                  any_spec, any_spec, _const_spec((4, D)), wspec, wspec, vec],
        out_specs=[pl.BlockSpec((tm, 2 * D), rev(0)), _acc_spec((D, 2 * D), (0, 0)), _acc_spec((D, D), (0, 0)),
                   _const_spec((4, D)), vec, wspec, vec, wspec, vec, vec],
        out_shape=[jax.ShapeDtypeStruct((t, NCOL_IN), BF16), jax.ShapeDtypeStruct((D, NCOL_IN), F32),
                   jax.ShapeDtypeStruct((D, D), F32), jax.ShapeDtypeStruct((4, D), F32), vshape,
                   wshape, vshape, wshape, vshape, vshape],
        scratch_shapes=[pltpu.VMEM((SUBLANES, D), F32), pltpu.VMEM((SUBLANES, D), F32), pltpu.VMEM((SUBLANES, D), F32),
                        pltpu.VMEM((tm, D), F32), pltpu.VMEM((tm, D), F32), pltpu.VMEM((tm, D), F32)],
        input_output_aliases={12: 0, 13: 1},
        compiler_params=_cparams(1),
    )(dya, ya_pre, wba, h1, z, *saved, h, h, dz, dwin, cw, wa, wx, lam)
    return outs


def _sgu_bwd(dyb, yb_pre, wbb, h1, saved, dz, dwin, lng, lnb, wmt, mask):
    t = dyb.shape[0]
    tm = _tile_big(t)

    def body(dyb_ref, ybp_ref, wbb_ref, h1_ref, gu_ref, mg_ref, vh_ref, gpv_ref, rstd_ref, dz_any, dwin_any,
             lng_ref, lnb_ref, wmt_ref, mask_ref,
             dz_ref, dwin_ref, dwbb_ref, dws_ref, dbst_ref, dlng_ref, dlnb_ref):
        del dz_any, dwin_any

        @pl.when(pl.program_id(0) == 0)
        def _():
            for r in (dwin_ref, dwbb_ref, dws_ref, dbst_ref, dlng_ref, dlnb_ref):
                r[...] = jnp.zeros_like(r)

        lng_v = lng_ref[...]
        vhat = vh_ref[...].astype(F32)
        vb = (vhat * lng_v + lnb_ref[...]).astype(BF16)
        rstd = rstd_ref[...]
        dyb_v = dyb_ref[...]
        dwbb_ref[...] += _dot_tn(ybp_ref[...], dyb_v)
        dyb = _dot_nt(dyb_v, wbb_ref[...])
        h1v = h1_ref[...]
        dzu = (dyb * mg_ref[...].astype(F32)).astype(BF16)
        dz_ref[:, 0:D] = dzu
        dwin_ref[:, 0:D] += _dot_tn(h1v, dzu)
        dmix = dyb * gu_ref[...].astype(F32)
        dmb = dmix.astype(BF16)
        rows = []
        lane = lax.broadcasted_iota(jnp.int32, (HD, NH), 1)
        dbst = jnp.zeros((HD, NH), F32)
        for b0 in range(0, tm, HD):
            cols = []
            for g in range(NH):
                sl = slice(g * HD, (g + 1) * HD)
                dmg = dmb[b0:b0 + HD, sl]
                dws_ref[g] += _dot_nt(dmg, vb[b0:b0 + HD, sl]) * mask_ref[...]
                cols.append(jnp.dot(wmt_ref[g], dmg, preferred_element_type=F32))
                dbst = dbst + jnp.where(lane == g, jnp.sum(dmix[b0:b0 + HD, sl], axis=1, keepdims=True), 0.0)
            rows.append(jnp.concatenate(cols, axis=1))
        dbst_ref[...] += dbst
        dvln = jnp.concatenate(rows, axis=0) if len(rows) > 1 else rows[0]
        dlng_ref[...] += _colsum(dvln * vhat)
        dlnb_ref[...] += _colsum(dvln)
        dvh = dvln * lng_v
        dgv = rstd * (dvh - jnp.mean(dvh, axis=-1, keepdims=True)
                      - vhat * jnp.mean(dvh * vhat, axis=-1, keepdims=True))
        dzv = (dgv * gpv_ref[...].astype(F32)).astype(BF16)
        dz_ref[:, D:2 * D] = dzv
        dwin_ref[:, D:2 * D] += _dot_tn(h1v, dzv)

    vec = _const_spec((1, D))
    wspec = _const_spec((NH, HD, HD))
    vshape = jax.ShapeDtypeStruct((1, D), F32)
    tile = pl.BlockSpec((tm, D), lambda i: (i, 0))
    any_spec = pl.BlockSpec(memory_space=pl.ANY)
    return pl.pallas_call(
        body, name="sgu_bwd", grid=(t // tm,),
        in_specs=[tile, tile, _const_spec((D, D), True), tile, tile, tile, tile, tile,
                  pl.BlockSpec((tm, 1), lambda i: (i, 0)), any_spec, any_spec,
                  vec, vec, wspec, _const_spec((HD, HD))],
        out_specs=[pl.BlockSpec((tm, 2 * D), lambda i: (i, 1)), _acc_spec((D, 2 * D), (0, 1)), _acc_spec((D, D), (0, 0)),
                   wspec, _const_spec((HD, NH)), vec, vec],
        out_shape=[jax.ShapeDtypeStruct((t, NCOL_IN), BF16), jax.ShapeDtypeStruct((D, NCOL_IN), F32),
                   jax.ShapeDtypeStruct((D, D), F32), jax.ShapeDtypeStruct((NH, HD, HD), F32),
                   jax.ShapeDtypeStruct((HD, NH), F32), vshape, vshape],
        input_output_aliases={9: 0, 10: 1},
        compiler_params=_cparams(1),
    )(dyb, yb_pre, wbb, h1, *saved, dz, dwin, lng, lnb, wmt, mask)


def _in_bwd(dz, win, x, dx2, g, scale1):
    t = x.shape[0]
    tm = _tile_big(t)

    def body(dz_ref, w_ref, x_ref, dx2_ref, g_ref, sc_ref, dx_ref, dsh_ref, dsc_ref, dg_ref):
        @pl.when(pl.program_id(0) == 0)
        def _():
            for r in (dsh_ref, dsc_ref, dg_ref):
                r[...] = jnp.zeros_like(r)

        dh = jnp.zeros((tm, D), F32)
        for c0 in range(0, NCOL_IN, D):
            dh = dh + _dot_nt(dz_ref[:, c0:c0 + D], w_ref[:, c0:c0 + D])
        dxn, dsh, dsc, dg = _modnorm_bwd(dh, x_ref[...], g_ref[...], sc_ref[...])
        dx_ref[...] = dx2_ref[...] + dxn
        dsh_ref[...] += dsh
        dsc_ref[...] += dsc
        dg_ref[...] += dg

    tile = pl.BlockSpec((tm, D), lambda i: (i, 0))
    vec = _const_spec((1, D))
    vshape = jax.ShapeDtypeStruct((1, D), F32)
    return pl.pallas_call(
        body, name="in_bwd", grid=(t // tm,),
        in_specs=[pl.BlockSpec((tm, NCOL_IN), lambda i: (i, 0)), _const_spec((D, NCOL_IN), True), tile, tile, vec, vec],
        out_specs=[tile, vec, vec, vec],
        out_shape=[jax.ShapeDtypeStruct((t, D), F32), vshape, vshape, vshape],
        compiler_params=_cparams(1),
    )(dz, win, x, dx2, g, scale1)


def _mod_cols(c_all, w_ada, b_cols):
    nb, cols = c_all.shape[0], w_ada.shape[1]

    def body(c_ref, w_ref, b_ref, o_ref):
        cv = c_ref[...]
        ca = (cv * _sigmoid(cv)).astype(BF16)
        o_ref[...] = jnp.dot(ca, w_ref[...].astype(BF16), preferred_element_type=F32) + b_ref[...]

    return pl.pallas_call(body, name="mod_cols", out_shape=jax.ShapeDtypeStruct((nb, cols), F32))(c_all, w_ada, b_cols)


def _ada_grad(c_all, dmod_cols):
    cols = dmod_cols.shape[1]

    def body(c_ref, d_ref, o_ref):
        cv = c_ref[...]
        ca = (cv * _sigmoid(cv)).astype(BF16)
        o_ref[...] = _dot_tn(ca, d_ref[...].astype(BF16))

    return pl.pallas_call(body, name="ada_grad", out_shape=jax.ShapeDtypeStruct((D, cols), F32))(c_all, dmod_cols)


def _adamw_update(w, m, v, g):
    bc1 = 1.0 - ADAM_B1 ** ADAM_STEP
    bc2 = 1.0 - ADAM_B2 ** ADAM_STEP
    mn = ADAM_B1 * m + (1.0 - ADAM_B1) * g
    vn = ADAM_B2 * v + (1.0 - ADAM_B2) * (g * g)
    return -ADAM_LR * ((mn / bc1) / (jnp.sqrt(vn / bc2) + ADAM_EPS) + ADAM_WD * w), mn, vn


def _adamw_group(names, ws, ms, vs, packs, name):
    n = len(names)
    starts, r0 = [], 0
    for w in ws:
        starts.append(r0)
        r0 += _pack_rows(w.shape)

    def body(*refs):
        w_refs, m_refs, v_refs, p_ref = refs[:n], refs[n:2 * n], refs[2 * n:3 * n], refs[3 * n]
        outs = refs[3 * n + 1:]
        for k in range(n):
            rows = _pack_rows(ws[k].shape)
            g = None
            for dev in range(N_DEV):
                if ws[k].shape[0] == 1:
                    term = jnp.concatenate(
                        [p_ref[dev, starts[k] + r:starts[k] + r + 1, :] for r in range(rows)], axis=1)
                else:
                    term = p_ref[dev, starts[k]:starts[k] + rows, :]
                g = term if g is None else g + term
            delta, mn, vn = _adamw_update(w_refs[k][...], m_refs[k][...], v_refs[k][...], g)
            for o_ref, val in zip(outs[4 * k:4 * k + 4], (g, delta, mn, vn)):
                o_ref[...] = val

    shapes = [jax.ShapeDtypeStruct(w.shape, F32) for w in ws for _ in range(4)]
    outs = pl.pallas_call(body, name=name, out_shape=shapes,
                          compiler_params=pltpu.CompilerParams(vmem_limit_bytes=VMEM_LIMIT))(*ws, *ms, *vs, packs)
    return {nm: tuple(outs[4 * k:4 * k + 4]) for k, nm in enumerate(names)}


def _adamw(w, m, v, parts, name):
    rows, cols = w.shape
    tr = _row_tile(rows, cols)
    stacked = [p.ndim == 3 for p in parts]

    def body(*refs):
        w_ref, m_ref, v_ref = refs[:3]
        p_refs = refs[3:3 + len(parts)]
        g_ref, d_ref, mo_ref, vo_ref = refs[3 + len(parts):]
        g = None
        for p_ref, st in zip(p_refs, stacked):
            terms = [p_ref[k].astype(F32) for k in range(p_ref.shape[0])] if st else [p_ref[...].astype(F32)]
            for term in terms:
                g = term if g is None else g + term
        delta, mn, vn = _adamw_update(w_ref[...], m_ref[...], v_ref[...], g)
        g_ref[...] = g
        mo_ref[...] = mn
        vo_ref[...] = vn
        d_ref[...] = delta

    tile = pl.BlockSpec((tr, cols), lambda i: (i, 0))
    p_specs = [pl.BlockSpec((p.shape[0], tr, cols), lambda i: (0, i, 0)) if st else tile for p, st in zip(parts, stacked)]
    shp = jax.ShapeDtypeStruct((rows, cols), F32)
    return pl.pallas_call(
        body, name=name, grid=(rows // tr,),
        in_specs=[tile, tile, tile] + p_specs, out_specs=[tile] * 4, out_shape=[shp] * 4,
        compiler_params=_cparams(1),
    )(w, m, v, *parts)


def _mesh_pos():
    return lax.axis_index("x"), lax.axis_index("y"), lax.axis_index("c")


def _other_chips(x, y):
    return [(1 - x, y), (x, 1 - y), (1 - x, 1 - y)]


def _block_of(ref, axis, index, size):
    if axis == 0:
        return ref.at[index]
    return ref.at[:, pl.ds(pl.multiple_of(index * size, 128), size)]


def _all_gather(shards, axes, name):
    n = len(shards)
    per = 7

    def body(*refs):
        ins, outs, done = refs[:n], refs[n:2 * n], refs[2 * n]
        send_sems, recv_sems, local_sems = refs[2 * n + 1:]
        x, y, c = _mesh_pos()
        me, sibling = (x, y, c), (x, y, 1 - c)
        chips = _other_chips(x, y)

        def rows(a, pos):
            return _block_of(outs[a], axes[a], 4 * pos[0] + 2 * pos[1] + pos[2], shards[a].shape[-1])

        def copy(a, k, block, to, src=None):
            return pltpu.make_async_remote_copy(
                src_ref=rows(a, block) if src is None else src, dst_ref=rows(a, block),
                send_sem=send_sems.at[a * per + k], recv_sem=recv_sems.at[a * per + k],
                device_id=to, device_id_type=MESH_IDS)

        mine = [pltpu.make_async_copy(ins[a], rows(a, me), local_sems.at[a]) for a in range(n)]
        for cp in mine:
            cp.start()
        first = []
        for a in range(n):
            first.append(copy(a, 0, me, sibling, src=ins[a]))
            first += [copy(a, 1 + j, me, (*chip, c), src=ins[a]) for j, chip in enumerate(chips)]
        for cp in first:
            cp.start()
        passed = []
        for j, chip in enumerate(chips):
            for a in range(n):
                copy(a, 1 + j, (*chip, c), me).wait_recv()
                fwd = copy(a, 4 + j, (*chip, c), sibling)
                fwd.start()
                passed.append(fwd)
        for a in range(n):
            copy(a, 0, sibling, me).wait_recv()
            for j, chip in enumerate(chips):
                copy(a, 4 + j, (*chip, 1 - c), me).wait_recv()
        for cp in first + passed:
            cp.wait_send()
        for cp in mine:
            cp.wait()
        done[...] = jnp.zeros_like(done)

    def full_shape(s, ax):
        return (N_DEV,) + s.shape if ax == 0 else s.shape[:-1] + (N_DEV * s.shape[-1],)

    any_spec = pl.BlockSpec(memory_space=pl.ANY)
    outs = pl.pallas_call(
        body, name=name,
        in_specs=[any_spec] * n, out_specs=[any_spec] * n + [pl.BlockSpec(memory_space=pltpu.VMEM)],
        out_shape=[jax.ShapeDtypeStruct(full_shape(s, ax), s.dtype) for s, ax in zip(shards, axes)]
        + [jax.ShapeDtypeStruct((SUBLANES, LANES), F32)],
        scratch_shapes=[pltpu.SemaphoreType.DMA((n * per,)), pltpu.SemaphoreType.DMA((n * per,)),
                        pltpu.SemaphoreType.DMA((n,))],
    )(*shards)
    return outs[:n], outs[n]


def _chip_blocks(x, y):
    return [(x, y)] + _other_chips(x, y)


def _sibling_reduce(gs, axis, name):
    g0, n = gs[0], len(gs)
    rows, cols = (g0.shape[1], g0.shape[2]) if axis == 0 else (g0.shape[0], g0.shape[1] // N_DEV)
    chunk = math.gcd(rows, 64)

    def body(*refs):
        g_refs, own_refs, pay_refs = refs[:n], refs[n:2 * n], refs[2 * n:3 * n]
        send_buf, keep_buf, recv_buf, pay_buf, send_sems, recv_sems, stage_sems, keep_sems, out_sems = refs[3 * n:]
        x, y, c = _mesh_pos()
        sibling = (x, y, 1 - c)
        chips = _chip_blocks(x, y)
        stage, keep, push = [], [], []
        for a in range(n):
            for j, (px, py) in enumerate(chips):
                s = 4 * a + j
                theirs = _block_of(g_refs[a], axis, 4 * px + 2 * py + (1 - c), cols)
                ours = _block_of(g_refs[a], axis, 4 * px + 2 * py + c, cols)
                stage.append(pltpu.make_async_copy(theirs, send_buf.at[s], stage_sems.at[s]))
                keep.append(pltpu.make_async_copy(ours, keep_buf.at[s], keep_sems.at[s]))
                push.append(pltpu.make_async_remote_copy(
                    src_ref=send_buf.at[s], dst_ref=recv_buf.at[s], send_sem=send_sems.at[s],
                    recv_sem=recv_sems.at[s], device_id=sibling, device_id_type=MESH_IDS))
        for cp in stage + keep:
            cp.start()
        for s in range(4 * n):
            stage[s].wait()
            push[s].start()
        written = []
        for s in range(4 * n):
            push[s].wait_recv()
            keep[s].wait()
            a, j = divmod(s, 4)
            res = keep_buf.at[s] if j == 0 else pay_buf.at[3 * a + j - 1]

            def add(r, carry, s=s, res=res):
                sl = pl.ds(pl.multiple_of(r * chunk, chunk), chunk)
                res[sl, :] = (keep_buf[s, sl, :] + recv_buf[s, sl, :]).astype(res.dtype)
                return carry

            lax.fori_loop(0, rows // chunk, add, 0)
            out = pltpu.make_async_copy(res, own_refs[a] if j == 0 else pay_refs[a].at[j - 1], out_sems.at[s])
            out.start()
            written.append(out)
        for cp in push:
            cp.wait_send()
        for cp in written:
            cp.wait()

    any_spec = pl.BlockSpec(memory_space=pl.ANY)
    buf = pltpu.VMEM((4 * n, rows, cols), F32)
    sems = pltpu.SemaphoreType.DMA((4 * n,))
    outs = pl.pallas_call(
        body, name=name,
        in_specs=[any_spec] * n, out_specs=[any_spec] * (2 * n),
        out_shape=[jax.ShapeDtypeStruct((rows, cols), F32)] * n + [jax.ShapeDtypeStruct((3, rows, cols), BF16)] * n,
        scratch_shapes=[buf, buf, buf, pltpu.VMEM((3 * n, rows, cols), BF16), sems, sems, sems, sems, sems],
        compiler_params=pltpu.CompilerParams(vmem_limit_bytes=VMEM_LIMIT),
    )(*gs)
    return list(zip(outs[:n], outs[n:]))


_HBM_SPEC = pl.BlockSpec(memory_space=pltpu.HBM)
_SEM_SPEC = pl.BlockSpec(memory_space=pltpu.SEMAPHORE)
_SIDE_EFFECT = pltpu.SideEffectType.DATAFLOW_SIDE_EFFECTING


def _exchange_start(name, srcs, lands, plan, n_copies):
    nb = len(srcs) + len(lands)

    def body(*refs):
        bufs, send_sems, recv_sems, token = refs[:nb], refs[nb], refs[nb + 1], refs[-1]
        for cp in plan(bufs[:len(srcs)], bufs[len(srcs):], send_sems, recv_sems):
            cp.start()
        token[...] = jnp.zeros_like(token)

    arrays = list(srcs) + list(lands)
    outs = pl.pallas_call(
        body, name=name,
        out_shape=(pltpu.SemaphoreType.DMA((n_copies,)), pltpu.SemaphoreType.DMA((n_copies,)),
                   *[pltpu.HBM(a.shape, a.dtype) for a in arrays], jax.ShapeDtypeStruct((SUBLANES, LANES), F32)),
        in_specs=[_HBM_SPEC] * nb,
        out_specs=(_SEM_SPEC, _SEM_SPEC, *[_HBM_SPEC] * nb, pl.BlockSpec(memory_space=pltpu.VMEM)),
        input_output_aliases={k: 2 + k for k in range(nb)},
        compiler_params=pltpu.CompilerParams(has_side_effects=_SIDE_EFFECT),
    )(*[pltpu.with_memory_space_constraint(a, pltpu.HBM) for a in arrays])
    return outs[0], outs[1], outs[2:2 + len(srcs)], outs[2 + len(srcs):2 + nb], outs[-1]


def _exchange_wait(name, send_sems, recv_sems, srcs, lands, plan, after):
    nb = len(srcs) + len(lands)
    after = list(after)

    def body(*refs):
        bufs, send_ref, recv_ref = refs[:nb], refs[nb], refs[nb + 1]
        for cp in plan(bufs[:len(srcs)], bufs[len(srcs):], send_ref, recv_ref):
            cp.wait_send()
            cp.wait_recv()

    arrays = list(srcs) + list(lands)
    outs = pl.pallas_call(
        body, name=name,
        out_shape=tuple(pltpu.HBM(a.shape, a.dtype) for a in arrays),
        in_specs=[_HBM_SPEC] * nb + [_SEM_SPEC, _SEM_SPEC] + [pl.BlockSpec(memory_space=pl.ANY)] * len(after),
        out_specs=tuple([_HBM_SPEC] * nb),
        input_output_aliases={k: k for k in range(nb)},
        compiler_params=pltpu.CompilerParams(has_side_effects=_SIDE_EFFECT),
    )(*arrays, send_sems, recv_sems, *after)
    return outs[len(srcs):]


def _gather_plan(axes, sizes):
    def plan(src_refs, land_refs, send_sems, recv_sems):
        x, y, c = _mesh_pos()
        copies = []
        for a, (src, land) in enumerate(zip(src_refs, land_refs)):
            mine = _block_of(land, axes[a], 4 * x + 2 * y + c, sizes[a])
            for k in range(1, N_DEV):
                peer = (1 - x if k & 4 else x, 1 - y if k & 2 else y, 1 - c if k & 1 else c)
                idx = a * (N_DEV - 1) + k - 1
                copies.append(pltpu.make_async_remote_copy(
                    src_ref=src, dst_ref=mine, send_sem=send_sems.at[idx], recv_sem=recv_sems.at[idx],
                    device_id=peer, device_id_type=MESH_IDS))
        return copies
    return plan


def _chip_plan(src_refs, land_refs, send_sems, recv_sems):
    x, y, c = _mesh_pos()
    copies = []
    for a, (src, land) in enumerate(zip(src_refs, land_refs)):
        for j, chip in enumerate(_other_chips(x, y)):
            copies.append(pltpu.make_async_remote_copy(
                src_ref=src.at[j], dst_ref=land.at[j], send_sem=send_sems.at[3 * a + j],
                recv_sem=recv_sems.at[3 * a + j], device_id=(*chip, c), device_id_type=MESH_IDS))
    return copies


def _own_block_placed(shard, axis, me):
    if axis == 0:
        full = lax.empty((N_DEV,) + shard.shape, shard.dtype)
        return lax.dynamic_update_slice(full, shard[None], (me,) + (0,) * shard.ndim)
    rows, cols = shard.shape

    def body(me_ref, s_ref, o_ref):
        del me_ref
        o_ref[...] = s_ref[...]

    return pl.pallas_call(
        body, name="place_own_columns",
        grid_spec=pltpu.PrefetchScalarGridSpec(
            num_scalar_prefetch=1, grid=(1,),
            in_specs=[pl.BlockSpec((rows, cols), lambda i, me_ref: (0, 0))],
            out_specs=pl.BlockSpec((rows, cols), lambda i, me_ref: (0, me_ref[0]))),
        out_shape=jax.ShapeDtypeStruct((rows, N_DEV * cols), shard.dtype),
    )(jnp.reshape(me, (1,)).astype(jnp.int32), shard)


def _local_step(x, target, mod, win, late_weights, p, grads_ready=None):
    shift1, scale1, gate1, shift2, scale2, gate2 = (mod[k] for k in range(6))

    def after_token(v, token):
        return v if token is None else v + token[0:1, 0:1]
    wa, wx = p["lru_w_a"].astype(BF16), p["lru_w_x"].astype(BF16)
    mask = jnp.tril(jnp.ones((HD, HD), F32))
    wm = (p["sgu_w_s"] * mask).astype(BF16)
    wmt = jnp.swapaxes(wm, 1, 2)
    bst = jnp.transpose(p["sgu_b_s"])

    h1, z = _norm_proj(x, p["norm_mix_g"], scale1, shift1, win, "mix_proj")
    hstate, ya_pre, *rnn_saved = _rnn_fwd(
        z, p["rnn_conv_w"], p["rnn_conv_b"], wa, p["lru_b_a"], wx, p["lru_b_x"], p["lru_lambda"])
    yb_pre, *sgu_saved = _sgu_fwd(z, p["sgu_ln_g"], p["sgu_ln_b"], wm, bst)
    wba, wbb, wout = late_weights("merge", [ya_pre, yb_pre])
    x2, ya, yb, o1 = _merge_fwd(ya_pre, yb_pre, z, x, gate1, wba, wbb, wout)
    wup = late_weights("ffn_up", [x2])
    h2, up_a, up_v, ff, fa, fv = _ffn_proj_mid(
        x2, p["norm_ffn_g"], scale2, shift2, wup, p["ffn_conv_w"], p["ffn_conv_b"])
    wd = late_weights("ffn_down", [ff])
    dx3, loss, d_gfin, d_gate2 = _ffn_out_loss(ff, wd, x2, target, gate2, p["norm_final_g"])

    dact, dval, d_wd, dcb_a, dcb_v = _ffn_down_bwd(dx3, gate2, ff, fa, fv, wd)
    dup, dx2, do1, d_cwf, d_shift2, d_scale2, d_gffn, d_gate1 = _ffn_up_bwd(
        dact, dval, up_a, up_v, p["ffn_conv_w"], wup, x2, dx3, p["norm_ffn_g"], scale2, o1, gate1)
    d_wup = _xt_y(h2, dup, "w_up_grad")
    ready = grads_ready if grads_ready else (lambda stage, big, small: None)
    token = ready("ffn", {"w_up": d_wup, "w_down": d_wd}, {})

    dya, dyb, dz, d_wout, d_win = _out_bwd(do1, wout, ya, yb, z, h1)
    dz, d_win, d_wba, d_cw, d_cb, d_wa, d_ba, d_wx, d_bx, d_lam = _rnn_bwd(
        dya, ya_pre, wba, h1, z, rnn_saved, hstate, dz, d_win, p["rnn_conv_w"], wa, wx,
        after_token(p["lru_lambda"], token))
    small = {
        "rnn_conv_w": d_cw, "rnn_conv_b": d_cb, "lru_w_a": d_wa, "lru_b_a": d_ba, "lru_w_x": d_wx, "lru_b_x": d_bx,
        "lru_lambda": d_lam, "norm_ffn_g": d_gffn, "ffn_conv_w": d_cwf,
        "ffn_conv_b": jnp.concatenate([dcb_a, dcb_v], axis=1), "norm_final_g": d_gfin,
    }
    token = ready("rnn", {}, small)
    dz, d_win, d_wbb, d_ws, d_bst, d_lng, d_lnb = _sgu_bwd(
        dyb, yb_pre, wbb, h1, sgu_saved, dz, d_win, p["sgu_ln_g"], after_token(p["sgu_ln_b"], token), wmt, mask)
    sgu_small = {"sgu_ln_g": d_lng, "sgu_ln_b": d_lnb, "sgu_w_s": d_ws, "sgu_b_s": jnp.transpose(d_bst)}
    mixer = {"w_in": d_win, "w_out": d_wout, "w_branch_a": d_wba, "w_branch_b": d_wbb}
    token = ready("mixer", mixer, sgu_small)
    grad_x, d_shift1, d_scale1, d_gmix = _in_bwd(dz, win, x, dx2, after_token(p["norm_mix_g"], token), scale1)

    small.update(sgu_small)
    small["norm_mix_g"] = d_gmix
    dmod = jnp.stack([d_shift1, d_scale1, d_gate1, d_shift2, d_scale2, d_gate2])
    big = {"w_in": d_win, "w_up": d_wup, "w_branch_a": d_wba, "w_branch_b": d_wbb, "w_out": d_wout, "w_down": d_wd}
    return loss, grad_x, big, small, dmod


LAST_REP = ["b_ada", "norm_mix_g"]
EARLY_REP = ["rnn_conv_b", "lru_w_a", "lru_b_a", "lru_w_x", "lru_b_x", "lru_lambda", "norm_ffn_g", "ffn_conv_b",
             "norm_final_g"]
MID_REP = ["sgu_ln_g", "sgu_ln_b", "sgu_w_s", "sgu_b_s"]
COL_SHARDED = ["rnn_conv_w", "ffn_conv_w"]
SMALL_GROUPS = {"rnn": EARLY_REP + COL_SHARDED, "mixer": MID_REP, "last": LAST_REP}
REPLICATED = LAST_REP + EARLY_REP + MID_REP
SMALL_NAMES = REPLICATED + COL_SHARDED
BIG_NAMES = ["w_in", "w_up", "w_branch_a", "w_branch_b", "w_out", "w_down"]
BIG_AXES = [1, 1, 0, 0, 0, 0]
WEIGHTS = ["w_ada", "b_ada", "norm_mix_g", "w_in", "rnn_conv_w", "rnn_conv_b", "lru_w_a", "lru_b_a", "lru_w_x",
           "lru_b_x", "lru_lambda", "sgu_ln_g", "sgu_ln_b", "sgu_w_s", "sgu_b_s", "w_branch_a", "w_branch_b",
           "w_out", "norm_ffn_g", "w_up", "ffn_conv_w", "ffn_conv_b", "w_down", "norm_final_g"]


def _pack_rows(shape):
    return math.prod(shape) // LANES


def _pack(arrays):
    return jnp.concatenate([a.reshape(-1, LANES) for a in arrays], axis=0)


def kernel(x, c, w_ada, b_ada, norm_mix_g, w_in, rnn_conv_w, rnn_conv_b, lru_w_a, lru_b_a, lru_w_x, lru_b_x, lru_lambda, sgu_ln_g, sgu_ln_b, sgu_w_s, sgu_b_s, w_branch_a, w_branch_b, w_out, norm_ffn_g, w_up, ffn_conv_w, ffn_conv_b, w_down, norm_final_g, loss_target, m_w_ada, m_b_ada, m_norm_mix_g, m_w_in, m_rnn_conv_w, m_rnn_conv_b, m_lru_w_a, m_lru_b_a, m_lru_w_x, m_lru_b_x, m_lru_lambda, m_sgu_ln_g, m_sgu_ln_b, m_sgu_w_s, m_sgu_b_s, m_w_branch_a, m_w_branch_b, m_w_out, m_norm_ffn_g, m_w_up, m_ffn_conv_w, m_ffn_conv_b, m_w_down, m_norm_final_g, v_w_ada, v_b_ada, v_norm_mix_g, v_w_in, v_rnn_conv_w, v_rnn_conv_b, v_lru_w_a, v_lru_b_a, v_lru_w_x, v_lru_b_x, v_lru_lambda, v_sgu_ln_g, v_sgu_ln_b, v_sgu_w_s, v_sgu_b_s, v_w_branch_a, v_w_branch_b, v_w_out, v_norm_ffn_g, v_w_up, v_ffn_conv_w, v_ffn_conv_b, v_w_down, v_norm_final_g):
    given = dict(locals())
    me = 4 * lax.axis_index("x") + 2 * lax.axis_index("y") + lax.axis_index("c")
    ada_cols = w_ada.shape[2]
    conv_cols = {"rnn_conv_w": rnn_conv_w.shape[2], "ffn_conv_w": ffn_conv_w.shape[2]}

    (win, c_all, cw_rnn, cw_ffn), _ = _all_gather(
        [w_in[0].astype(BF16), c.reshape(1, 1, D), rnn_conv_w[0], ffn_conv_w[0]], [1, 0, 1, 1], "gather_first")
    c_all = c_all.reshape(N_DEV, D)

    b_cols = lax.dynamic_slice_in_dim(b_ada, me * ada_cols, ada_cols, axis=1)
    (mod_all,), mod_done = _all_gather(
        [_mod_cols(c_all, w_ada[0], b_cols).reshape(1, N_DEV, ada_cols)], [0], "gather_mod")
    mod_all = mod_all.reshape(N_DEV, N_DEV, ada_cols)
    mod_mine = lax.dynamic_index_in_dim(mod_all, me, axis=1, keepdims=False).reshape(6, 1, D)

    late_groups = {"merge": (["w_branch_a", "w_branch_b", "w_out"], [0, 0, 0]), "ffn_up": (["w_up"], [1]),
                   "ffn_down": (["w_down"], [0])}
    in_flight, started = {}, mod_done[0:1, 0:1]
    for stage, (names, axes) in late_groups.items():
        shards = [(given[n][0] + started).astype(BF16) for n in names]
        plan = _gather_plan(axes, [s.shape[-1] for s in shards])
        send, recv, srcs, lands, token = _exchange_start(
            "gather_start_" + stage, shards, [_own_block_placed(s, ax, me) for s, ax in zip(shards, axes)], plan,
            len(shards) * (N_DEV - 1))
        in_flight[stage] = (send, recv, srcs, lands, plan)
        started = started + token[0:1, 0:1]

    def late_weights(stage, after):
        send, recv, srcs, lands, plan = in_flight[stage]
        full = _exchange_wait("gather_wait_" + stage, send, recv, srcs, lands, plan, after)
        full = [w.reshape(-1, D) if ax == 0 else w for w, ax in zip(full, late_groups[stage][1])]
        return full if len(full) > 1 else full[0]

    mod_mine = mod_mine + started

    reducing, packing = {}, {}

    def start_pack(stage, small):
        pack = _pack([small[n] for n in SMALL_GROUPS[stage]])[None]
        plan = _gather_plan([0], [LANES])
        send, recv, srcs, lands, tok = _exchange_start(
            "small_start_" + stage, [pack], [_own_block_placed(pack, 0, me)], plan, N_DEV - 1)
        packing[stage] = (send, recv, srcs, lands, plan)
        return tok

    def grads_ready(stage, grads, small):
        tokens = [start_pack(stage, small)] if small else []
        if grads:
            tokens.append(start_reduce(stage, grads))
        return sum(tokens[1:], tokens[0])

    def start_reduce(stage, grads):
        names = [n for n in BIG_NAMES if n in grads]
        blocked = {}
        for n in names:
            ax = BIG_AXES[BIG_NAMES.index(n)]
            g = grads[n] if ax == 1 else grads[n].reshape(N_DEV, grads[n].shape[0] // N_DEV, grads[n].shape[1])
            blocked.setdefault((ax, g.shape), []).append((n, g))
        sums = {}
        for (ax, _), group in blocked.items():
            reduced = _sibling_reduce([g for _, g in group], ax, "reduce_sibling_" + "_".join(n for n, _ in group))
            sums.update({n: r for (n, _), r in zip(group, reduced)})
        sums = [sums[n] for n in names]
        pays = [pay for _, pay in sums]
        send, recv, srcs, lands, tok = _exchange_start(
            "reduce_start_" + stage, pays, [lax.empty(p_.shape, p_.dtype) for p_ in pays], _chip_plan, 3 * len(pays))
        reducing[stage] = (names, [own for own, _ in sums], send, recv, srcs, lands)
        return tok

    p = {n: given[n][0] for n in REPLICATED if n not in ("b_ada", "norm_final_g")}
    p = {n: (a.reshape(1, -1) if a.ndim == 1 else a) for n, a in p.items()}
    p["rnn_conv_w"], p["ffn_conv_w"] = cw_rnn, cw_ffn
    p["norm_final_g"] = norm_final_g.reshape(1, D)
    loss, grad_x, _, small, dmod = _local_step(x[0], loss_target[0], mod_mine, win, late_weights, p, grads_ready)

    small["b_ada"] = dmod.reshape(1, 6 * D)
    rows_of = {n: _pack_rows(small[n].shape) for n in SMALL_NAMES}
    (last,), _ = _all_gather([_pack([small[n] for n in LAST_REP])[None]], [0], "gather_small")
    gathered = {"last": last}
    for stage, (send, recv, srcs, lands, plan) in packing.items():
        (gathered[stage],) = _exchange_wait("small_wait_" + stage, send, recv, srcs, lands, plan, [grad_x])
    gathered = {k: v.reshape(N_DEV, -1, LANES) for k, v in gathered.items()}

    out = {}
    for stage, (names, owns, send, recv, srcs, lands) in reducing.items():
        landed = _exchange_wait("reduce_wait_" + stage, send, recv, srcs, lands, _chip_plan, [last])
        for n, own, got in zip(names, owns, landed):
            out[n] = _adamw(given[n][0], given["m_" + n][0], given["v_" + n][0], [own, got], "adamw_" + n)

    dmod_all = gathered["last"][:, :rows_of["b_ada"]].reshape(N_DEV, 6 * D)
    dmod_cols = lax.dynamic_slice_in_dim(dmod_all, me * ada_cols, ada_cols, axis=1)
    out["w_ada"] = _adamw(w_ada[0], m_w_ada[0], v_w_ada[0], [_ada_grad(c_all, dmod_cols)], "adamw_w_ada")

    def rows_form(a):
        return a.reshape(1, -1) if a.size // a.shape[-1] == 1 or a.ndim == 1 else a.reshape(-1, LANES)

    for stage, names in (("last", LAST_REP), ("rnn", EARLY_REP), ("mixer", MID_REP)):
        out.update(_adamw_group(names, *[[rows_form(given[pre + n]) for n in names] for pre in ("", "m_", "v_")],
                                gathered[stage], "adamw_small_" + stage))

    row0 = sum(rows_of[n] for n in EARLY_REP)
    for n in COL_SHARDED:
        full = gathered["rnn"][:, row0:row0 + rows_of[n]].reshape(N_DEV, small[n].shape[0], small[n].shape[1])
        mine = lax.dynamic_slice_in_dim(full, me * conv_cols[n], conv_cols[n], axis=2)
        out[n] = _adamw(given[n][0], given["m_" + n][0], given["v_" + n][0], [mine], "adamw_" + n)
        row0 += rows_of[n]

    total = lax.psum(loss[0, 0], ("x", "y", "c"))
    results = [total, grad_x[None]]
    for kind in range(4):
        results += [out[n][kind].reshape(given[n].shape) for n in WEIGHTS]
    return tuple(results)
```

```python
import math

import jax
import jax.numpy as jnp
from jax import lax
from jax.experimental import pallas as pl
from jax.experimental.pallas import tpu as pltpu

F32 = jnp.float32
BF16 = jnp.bfloat16
MESH_IDS = pl.DeviceIdType.MESH

D = 1024
NH = 8
HD = 128
NCOL_IN = 6 * D
DFF = 3 * D
N_DEV = 8
EPS = 1e-6
LRU_C = 8.0
ADAM_LR, ADAM_B1, ADAM_B2, ADAM_EPS, ADAM_WD, ADAM_STEP = 0.001, 0.9, 0.999, 1e-08, 0.01, 10

SUBLANES = 8
LANES = 128
HALO = 16
VMEM_LIMIT = 56 * 1024 * 1024
GELU_K = math.sqrt(2.0 / math.pi)
GELU_C = 0.044715


def _cparams(n_axes):
    return pltpu.CompilerParams(dimension_semantics=("arbitrary",) * n_axes, vmem_limit_bytes=VMEM_LIMIT)


def _const_spec(shape, single_buffer=False):
    nd = len(shape)
    if single_buffer:
        return pl.BlockSpec(shape, lambda *_: (0,) * nd, pipeline_mode=pl.Buffered(1))
    return pl.BlockSpec(shape, lambda *_: (0,) * nd)


def _vec_operand(v):
    if isinstance(v, tuple):
        stack, k = v
        return stack, pl.BlockSpec((None, 1, D), lambda *_: (k, 0, 0))
    return v, _const_spec((1, D))


def _tile_big(t):
    return min(512, t)


def _tile_seq(t):
    return min(256, t)


def _row_tile(rows, cols):
    cap = max(SUBLANES, (2 * 1024 * 1024) // (4 * cols) // SUBLANES * SUBLANES)
    if rows <= cap:
        return rows
    return next(tr for tr in range(cap, 0, -SUBLANES) if rows % tr == 0)


def _gelu_t(x):
    x2 = x * x
    t = jnp.tanh(x * (GELU_K + (GELU_K * GELU_C) * x2))
    hx = 0.5 * x
    return hx + hx * t, (x2, hx, t)


def _gelu_grad(shared):
    x2, hx, t = shared
    return (0.5 + 0.5 * t) + (hx * (1.0 - t * t)) * (GELU_K + (3.0 * GELU_K * GELU_C) * x2)


def _sigmoid(x):
    return 1.0 / (1.0 + jnp.exp(-x))


def _log_sigmoid(x):
    return -(jnp.maximum(-x, 0.0) + jnp.log1p(jnp.exp(-jnp.abs(x))))


def _row_iota(cols):
    return lax.broadcasted_iota(jnp.int32, (SUBLANES, cols), 0)


def _shift_down(x, k, prev8):
    if k == 0:
        return x
    r = pltpu.roll(x, k, 0)
    p = pltpu.roll(prev8, k, 0)
    head = jnp.where(_row_iota(x.shape[1]) < k, p, r[:SUBLANES])
    return jnp.concatenate([head, r[SUBLANES:]], axis=0)


def _shift_up(x, k, next8):
    if k == 0:
        return x
    n = x.shape[0]
    r = pltpu.roll(x, n - k, 0)
    q = pltpu.roll(next8, SUBLANES - k, 0)
    tail = jnp.where(_row_iota(x.shape[1]) >= SUBLANES - k, q, r[n - SUBLANES:])
    return jnp.concatenate([r[:n - SUBLANES], tail], axis=0)


def _heads_nn(x_bf, w_ref):
    return jnp.concatenate(
        [jnp.dot(x_bf[:, h * HD:(h + 1) * HD], w_ref[h], preferred_element_type=F32) for h in range(NH)], axis=1)


def _heads_nt(x_bf, w_ref):
    return jnp.concatenate(
        [lax.dot_general(x_bf[:, h * HD:(h + 1) * HD], w_ref[h], (((1,), (1,)), ((), ())), preferred_element_type=F32)
         for h in range(NH)], axis=1)


def _dot_nt(a, b):
    return lax.dot_general(a, b, (((1,), (1,)), ((), ())), preferred_element_type=F32)


def _dot_tn(a, b):
    return lax.dot_general(a, b, (((0,), (0,)), ((), ())), preferred_element_type=F32)


def _colsum(x):
    return jnp.sum(x, axis=0, keepdims=True)


def _prev_halo_map(tm, col):
    return lambda i, *_: (jnp.maximum(i * (tm // HALO) - 1, 0), col)


def _norm_proj(x, g, scale, shift, w, name):
    t, n = x.shape[0], w.shape[1]
    tm = _tile_big(t)

    def body(x_ref, g_ref, sc_ref, sh_ref, w_ref, h_ref, z_ref):
        xv = x_ref[...]
        r = lax.rsqrt(jnp.mean(xv * xv, axis=-1, keepdims=True) + EPS)
        hb = ((xv * r * g_ref[...]) * (1.0 + sc_ref[...]) + sh_ref[...]).astype(BF16)
        h_ref[...] = hb
        for c0 in range(0, n, D):
            z_ref[:, c0:c0 + D] = jnp.dot(hb, w_ref[:, c0:c0 + D], preferred_element_type=F32).astype(BF16)

    vec = _const_spec((1, D))
    (scale, sc_spec), (shift, sh_spec) = _vec_operand(scale), _vec_operand(shift)
    return pl.pallas_call(
        body, name=name, grid=(t // tm,),
        in_specs=[pl.BlockSpec((tm, D), lambda i: (i, 0)), vec, sc_spec, sh_spec, _const_spec((D, n), True)],
        out_specs=[pl.BlockSpec((tm, D), lambda i: (i, 0)), pl.BlockSpec((tm, n), lambda i: (i, 0))],
        out_shape=[jax.ShapeDtypeStruct((t, D), BF16), jax.ShapeDtypeStruct((t, n), BF16)],
        compiler_params=_cparams(1),
    )(x, g, scale, shift, w)


def _lru_gates(xc, wa_ref, ba, wx_ref, bx, ls):
    xb = xc.astype(BF16)
    ra = _sigmoid(_heads_nn(xb, wa_ref) + ba)
    ia = _sigmoid(_heads_nn(xb, wx_ref) + bx)
    la = LRU_C * ra * ls
    a = jnp.exp(la)
    mult = jnp.sqrt(-jnp.tanh(la) * (1.0 + a * a))
    return ra, ia, a, mult


def _conv4(xr, prev8, cw_ref, cb):
    return (cb + cw_ref[3:4, :] * xr + cw_ref[2:3, :] * _shift_down(xr, 1, prev8)
            + cw_ref[1:2, :] * _shift_down(xr, 2, prev8) + cw_ref[0:1, :] * _shift_down(xr, 3, prev8))


def _rnn_fwd(z, cw, cb, wa, ba, wx, bx, lam):
    t = z.shape[0]
    tm = _tile_seq(t)
    ngrp = tm // SUBLANES

    def body(xr_ref, xp_ref, gr_ref, cw_ref, cb_ref, wa_ref, ba_ref, wx_ref, bx_ref, lam_ref,
             h_ref, ya_ref, xc_ref, ra_ref, ia_ref, gg_ref, hg_ref, carry_ref, a_scr, u_scr):
        i = pl.program_id(0)

        @pl.when(i == 0)
        def _():
            carry_ref[...] = jnp.zeros_like(carry_ref)

        xr = xr_ref[...].astype(F32)
        prev8 = jnp.where(i == 0, 0.0, xp_ref[...].astype(F32)[HALO - SUBLANES:])
        xc = _conv4(xr, prev8, cw_ref, cb_ref[...])
        ra, ia, a, mult = _lru_gates(xc, wa_ref, ba_ref[...], wx_ref, bx_ref[...], _log_sigmoid(lam_ref[...]))
        xc_ref[...] = xc.astype(BF16)
        ra_ref[...] = ra.astype(BF16)
        ia_ref[...] = ia.astype(BF16)
        a_scr[...] = a
        u_scr[...] = mult * (ia * xc)
        row = _row_iota(D)

        def grp(j, carry):
            r0 = pl.multiple_of(j * SUBLANES, SUBLANES)
            av = a_scr[pl.ds(r0, SUBLANES), :]
            uv = u_scr[pl.ds(r0, SUBLANES), :]
            for d in (1, 2, 4):
                m = row >= d
                uv = jnp.where(m, av * pltpu.roll(uv, d, 0) + uv, uv)
                av = jnp.where(m, av * pltpu.roll(av, d, 0), av)
            hv = uv + av * carry
            h_ref[pl.ds(r0, SUBLANES), :] = hv
            return hv[SUBLANES - 1:SUBLANES, :]

        carry_ref[0:1, :] = lax.fori_loop(0, ngrp, grp, carry_ref[0:1, :])
        grv = gr_ref[...].astype(F32)
        gg, tg = _gelu_t(grv)
        hv = h_ref[...]
        ya_ref[...] = (hv * gg).astype(BF16)
        gg_ref[...] = gg.astype(BF16)
        hg_ref[...] = (hv * _gelu_grad(tg)).astype(BF16)

    vec = _const_spec((1, D))
    wspec = _const_spec((NH, HD, HD))
    tile = pl.BlockSpec((tm, D), lambda i: (i, 0))
    bshape = jax.ShapeDtypeStruct((t, D), BF16)
    return pl.pallas_call(
        body, name="rnn_fwd", grid=(t // tm,),
        in_specs=[tile, pl.BlockSpec((HALO, D), _prev_halo_map(tm, 0)),
                  pl.BlockSpec((tm, D), lambda i: (i, 1)), _const_spec((4, D)), vec, wspec, vec, wspec, vec, vec],
        out_specs=[tile] * 7,
        out_shape=[jax.ShapeDtypeStruct((t, D), F32)] + [bshape] * 6,
        scratch_shapes=[pltpu.VMEM((SUBLANES, D), F32), pltpu.VMEM((tm, D), F32), pltpu.VMEM((tm, D), F32)],
        compiler_params=_cparams(1),
    )(z, z, z, cw, cb, wa, ba, wx, bx, lam)


def _sgu_fwd(z, lng, lnb, wm, bst):
    t = z.shape[0]
    tm = _tile_seq(t)

    def body(zu_ref, zv_ref, lng_ref, lnb_ref, wm_ref, bst_ref, yb_ref, gu_ref, mg_ref, vh_ref, gpv_ref, rstd_ref):
        gu, su = _gelu_t(zu_ref[...].astype(F32))
        gv, sv = _gelu_t(zv_ref[...].astype(F32))
        mu = jnp.mean(gv, axis=-1, keepdims=True)
        cen = gv - mu
        rstd = lax.rsqrt(jnp.mean(cen * cen, axis=-1, keepdims=True) + EPS)
        vhat = cen * rstd
        vb = (vhat * lng_ref[...] + lnb_ref[...]).astype(BF16)
        rows = []
        for b0 in range(0, tm, HD):
            rows.append(jnp.concatenate(
                [jnp.dot(wm_ref[g], vb[b0:b0 + HD, g * HD:(g + 1) * HD], preferred_element_type=F32)
                 + bst_ref[:, g:g + 1] for g in range(NH)], axis=1))
        mixed = jnp.concatenate(rows, axis=0) if len(rows) > 1 else rows[0]
        yb_ref[...] = (gu * mixed).astype(BF16)
        gu_ref[...] = gu.astype(BF16)
        mg_ref[...] = (mixed * _gelu_grad(su)).astype(BF16)
        vh_ref[...] = vhat.astype(BF16)
        gpv_ref[...] = _gelu_grad(sv).astype(BF16)
        rstd_ref[...] = rstd

    vec = _const_spec((1, D))
    tile = pl.BlockSpec((tm, D), lambda i: (i, 0))
    bshape = jax.ShapeDtypeStruct((t, D), BF16)
    return pl.pallas_call(
        body, name="sgu_fwd", grid=(t // tm,),
        in_specs=[pl.BlockSpec((tm, D), lambda i: (i, 2)), pl.BlockSpec((tm, D), lambda i: (i, 3)), vec, vec,
                  _const_spec((NH, HD, HD)), _const_spec((HD, NH))],
        out_specs=[tile] * 5 + [pl.BlockSpec((tm, 1), lambda i: (i, 0))],
        out_shape=[bshape] * 5 + [jax.ShapeDtypeStruct((t, 1), F32)],
        compiler_params=_cparams(1),
    )(z, z, lng, lnb, wm, bst)


def _merge_fwd(ya_pre, yb_pre, z, x, gate1, wba, wbb, wout):
    t = x.shape[0]
    tm = _tile_big(t)

    def body(yap_ref, ybp_ref, ga_ref, gb_ref, x_ref, g1_ref, wba_ref, wbb_ref, wo_ref,
             x2_ref, ya_ref, yb_ref, mg_ref, o1_ref):
        ya = jnp.dot(yap_ref[...], wba_ref[...], preferred_element_type=F32)
        yb = jnp.dot(ybp_ref[...], wbb_ref[...], preferred_element_type=F32)
        merged = _sigmoid(ga_ref[...].astype(F32)) * ya + _sigmoid(gb_ref[...].astype(F32)) * yb
        mb = merged.astype(BF16)
        o1 = jnp.dot(mb, wo_ref[...], preferred_element_type=F32)
        x2_ref[...] = x_ref[...] + g1_ref[...] * o1
        ya_ref[...] = ya.astype(BF16)
        yb_ref[...] = yb.astype(BF16)
        mg_ref[...] = mb
        o1_ref[...] = o1.astype(BF16)

    tile = pl.BlockSpec((tm, D), lambda i: (i, 0))
    wspec = _const_spec((D, D))
    bshape = jax.ShapeDtypeStruct((t, D), BF16)
    gate1, g1_spec = _vec_operand(gate1)
    return pl.pallas_call(
        body, name="merge_fwd", grid=(t // tm,),
        in_specs=[tile, tile, pl.BlockSpec((tm, D), lambda i: (i, 4)), pl.BlockSpec((tm, D), lambda i: (i, 5)),
                  tile, g1_spec, wspec, wspec, wspec],
        out_specs=[tile] * 5,
        out_shape=[jax.ShapeDtypeStruct((t, D), F32), bshape, bshape, bshape, bshape],
        compiler_params=_cparams(1),
    )(ya_pre, yb_pre, z, z, x, gate1, wba, wbb, wout)


def _conv3(u, prev8, cw_ref, cb):
    return cb + cw_ref[2:3, :] * u + cw_ref[1:2, :] * _shift_down(u, 1, prev8) + cw_ref[0:1, :] * _shift_down(u, 2, prev8)


def _ffn_proj_mid(x2, g, scale, shift, w, cw, cb):
    t = x2.shape[0]
    tm = _tile_big(t)
    nc = DFF // D

    def body(x_ref, g_ref, sc_ref, sh_ref, wa_ref, wv_ref, cwa_ref, cwv_ref, cba_ref, cbv_ref,
             h_ref, upa_ref, upv_ref, ff_ref, fa_ref, fv_ref, hb_scr, prev_ref):
        i, c = pl.program_id(0), pl.program_id(1)

        @pl.when(i == 0)
        def _():
            prev_ref[c] = jnp.zeros((2, SUBLANES, D), F32)

        @pl.when(c == 0)
        def _():
            xv = x_ref[...]
            r = lax.rsqrt(jnp.mean(xv * xv, axis=-1, keepdims=True) + EPS)
            hb_scr[...] = ((xv * r * g_ref[...]) * (1.0 + sc_ref[...]) + sh_ref[...]).astype(BF16)
            h_ref[...] = hb_scr[...]

        hb = hb_scr[...]
        halves = []
        for s, (w_ref, up_ref, cw_ref, cb_ref) in enumerate(((wa_ref, upa_ref, cwa_ref, cba_ref),
                                                             (wv_ref, upv_ref, cwv_ref, cbv_ref))):
            u = jnp.dot(hb, w_ref[...], preferred_element_type=F32)
            up_ref[...] = u.astype(BF16)
            halves.append(_conv3(u, prev_ref[c, s], cw_ref, cb_ref[...]))
            prev_ref[c, s] = u[tm - SUBLANES:]
        act, val = halves
        ga, ta = _gelu_t(act)
        ff_ref[...] = (ga * val).astype(BF16)
        fa_ref[...] = (val * _gelu_grad(ta)).astype(BF16)
        fv_ref[...] = ga.astype(BF16)

    def cols(rows, off):
        return pl.BlockSpec((rows, D), lambda i, c: (0, off + c))

    vec = pl.BlockSpec((1, D), lambda i, c: (0, 0))
    row_tile = pl.BlockSpec((tm, D), lambda i, c: (i, 0))
    chunk = pl.BlockSpec((tm, D), lambda i, c: (i, c))
    hshape = jax.ShapeDtypeStruct((t, DFF), BF16)
    (scale, sc_spec), (shift, sh_spec) = _vec_operand(scale), _vec_operand(shift)
    return pl.pallas_call(
        body, name="ffn_proj_mid", grid=(t // tm, nc),
        in_specs=[row_tile, vec, sc_spec, sh_spec, cols(D, 0), cols(D, nc), cols(3, 0), cols(3, nc), cols(1, 0), cols(1, nc)],
        out_specs=[row_tile, chunk, chunk, chunk, chunk, chunk],
        out_shape=[jax.ShapeDtypeStruct((t, D), BF16), hshape, hshape, hshape, hshape, hshape],
        scratch_shapes=[pltpu.VMEM((tm, D), BF16), pltpu.VMEM((nc, 2, SUBLANES, D), F32)],
        compiler_params=_cparams(2),
    )(x2, g, scale, shift, w, w, cw, cw, cb, cb)


def _ffn_out_loss(ff, wd, x2, target, gate2, gfin):
    t = x2.shape[0]
    tm = _tile_big(t)

    def body(ff_ref, wd_ref, x2_ref, tg_ref, g2_ref, gf_ref, dx3_ref, loss_ref, dgf_ref, dg2_ref):
        @pl.when(pl.program_id(0) == 0)
        def _():
            loss_ref[...] = jnp.zeros_like(loss_ref)
            dgf_ref[...] = jnp.zeros_like(dgf_ref)
            dg2_ref[...] = jnp.zeros_like(dg2_ref)

        o2 = jnp.dot(ff_ref[...], wd_ref[...], preferred_element_type=F32)
        x3 = x2_ref[...] + g2_ref[...] * o2
        r = lax.rsqrt(jnp.mean(x3 * x3, axis=-1, keepdims=True) + EPS)
        xhat = x3 * r
        err = xhat * gf_ref[...] - tg_ref[...]
        loss_ref[...] += 0.5 * jnp.sum(jnp.mean(err * err, axis=-1, keepdims=True), axis=0, keepdims=True)
        dy = err * (1.0 / D)
        dgf_ref[...] += _colsum(dy * xhat)
        dxh = dy * gf_ref[...]
        dx3 = r * (dxh - xhat * jnp.mean(dxh * xhat, axis=-1, keepdims=True))
        dx3_ref[...] = dx3
        dg2_ref[...] += _colsum(dx3 * o2)

    tile = pl.BlockSpec((tm, D), lambda i: (i, 0))
    vec = _const_spec((1, D))
    gate2, g2_spec = _vec_operand(gate2)
    return pl.pallas_call(
        body, name="ffn_out_loss", grid=(t // tm,),
        in_specs=[pl.BlockSpec((tm, DFF), lambda i: (i, 0)), _const_spec((DFF, D), True), tile, tile, g2_spec, vec],
        out_specs=[tile, _const_spec((1, 1)), vec, vec],
        out_shape=[jax.ShapeDtypeStruct((t, D), F32), jax.ShapeDtypeStruct((1, 1), F32),
                   jax.ShapeDtypeStruct((1, D), F32), jax.ShapeDtypeStruct((1, D), F32)],
        compiler_params=_cparams(1),
    )(ff, wd, x2, target, gate2, gfin)


def _ffn_down_bwd(dx3, gate2, ff, fa, fv, wd):
    t = dx3.shape[0]
    tm = min(1024, t)
    nc = DFF // D

    def body(dx3_ref, g2_ref, ff_ref, fa_ref, fv_ref, wd_ref, da_ref, dv_ref, dwd_ref, dcba_ref, dcbv_ref):
        @pl.when(pl.program_id(1) == 0)
        def _():
            for r in (dwd_ref, dcba_ref, dcbv_ref):
                r[...] = jnp.zeros_like(r)

        do2 = (dx3_ref[...] * g2_ref[...]).astype(BF16)
        dwd_ref[...] += _dot_tn(ff_ref[...], do2)
        dff = _dot_nt(do2, wd_ref[...])
        dact = dff * fa_ref[...].astype(F32)
        dval = dff * fv_ref[...].astype(F32)
        da_ref[...] = dact.astype(BF16)
        dv_ref[...] = dval.astype(BF16)
        dcba_ref[...] += _colsum(dact)
        dcbv_ref[...] += _colsum(dval)

    blk = pl.BlockSpec((tm, D), lambda c, i: (i, c))
    vec = pl.BlockSpec((1, D), lambda c, i: (0, c))
    gate2, g2_spec = _vec_operand(gate2)
    return pl.pallas_call(
        body, name="ffn_down_bwd", grid=(nc, t // tm),
        in_specs=[pl.BlockSpec((tm, D), lambda c, i: (i, 0)), g2_spec,
                  blk, blk, blk, pl.BlockSpec((D, D), lambda c, i: (c, 0))],
        out_specs=[blk, blk, pl.BlockSpec((D, D), lambda c, i: (c, 0)), vec, vec],
        out_shape=[jax.ShapeDtypeStruct((t, DFF), BF16), jax.ShapeDtypeStruct((t, DFF), BF16),
                   jax.ShapeDtypeStruct((DFF, D), F32),
                   jax.ShapeDtypeStruct((1, DFF), F32), jax.ShapeDtypeStruct((1, DFF), F32)],
        compiler_params=_cparams(2),
    )(dx3, gate2, ff, fa, fv, wd)


def _modnorm_bwd(dh, xv, g, scale):
    r = lax.rsqrt(jnp.mean(xv * xv, axis=-1, keepdims=True) + EPS)
    xhat = xv * r
    dxn = dh * (1.0 + scale)
    dxh = dxn * g
    dx = r * (dxh - xhat * jnp.mean(dxh * xhat, axis=-1, keepdims=True))
    return dx, _colsum(dh), _colsum(dh * (xhat * g)), _colsum(dxn * xhat)


def _ffn_up_bwd(dact, dval, up_a, up_v, cw, wup, x2, dx3, gffn, scale2, o1, gate1):
    t = x2.shape[0]
    tm = _tile_seq(t)
    nt = t // tm
    nc = DFF // D

    def body(da_ref, dan_ref, dv_ref, dvn_ref, ua_ref, uv_ref, cw_ref, w_ref, x2_ref, dx3_ref, g_ref, sc_ref, o1_ref, g1_ref,
             dup_ref, dx2_ref, do1_ref, dcw_ref, dsh_ref, dsc_ref, dg_ref, dg1_ref):
        i = pl.program_id(0)

        @pl.when(i == 0)
        def _():
            for r in (dcw_ref, dsh_ref, dsc_ref, dg_ref, dg1_ref):
                r[...] = jnp.zeros_like(r)

        last = i == nt - 1
        dh = jnp.zeros((tm, D), F32)
        for half, (d_ref, dn_ref, u_ref) in enumerate(((da_ref, dan_ref, ua_ref), (dv_ref, dvn_ref, uv_ref))):
            nxt = jnp.where(last, 0.0, dn_ref[...].astype(F32)[:SUBLANES])
            for c in range(nc):
                c0 = half * DFF + c * D
                dv = d_ref[:, c * D:(c + 1) * D].astype(F32)
                nx = nxt[:, c * D:(c + 1) * D]
                taps = (_shift_up(dv, 2, nx), _shift_up(dv, 1, nx), dv)
                dup = (cw_ref[2:3, c0:c0 + D] * taps[2] + cw_ref[1:2, c0:c0 + D] * taps[1]
                       + cw_ref[0:1, c0:c0 + D] * taps[0]).astype(BF16)
                upv = u_ref[:, c * D:(c + 1) * D].astype(F32)
                for k in range(3):
                    dcw_ref[k:k + 1, c0:c0 + D] += _colsum(taps[k] * upv)
                dup_ref[:, c0:c0 + D] = dup
                dh = dh + _dot_nt(dup, w_ref[:, c0:c0 + D])
        dxn, dsh, dsc, dg = _modnorm_bwd(dh, x2_ref[...], g_ref[...], sc_ref[...])
        dx2 = dx3_ref[...] + dxn
        dx2_ref[...] = dx2
        do1_ref[...] = (dx2 * g1_ref[...]).astype(BF16)
        dsh_ref[...] += dsh
        dsc_ref[...] += dsc
        dg_ref[...] += dg
        dg1_ref[...] += _colsum(dx2 * o1_ref[...].astype(F32))

    tile = pl.BlockSpec((tm, D), lambda i: (i, 0))
    wide = pl.BlockSpec((tm, DFF), lambda i: (i, 0))
    nxt = pl.BlockSpec((HALO, DFF), lambda i: (jnp.minimum((i + 1) * (tm // HALO), t // HALO - 1), 0))
    vec = _const_spec((1, D))
    vshape = jax.ShapeDtypeStruct((1, D), F32)
    (scale2, sc_spec), (gate1, g1_spec) = _vec_operand(scale2), _vec_operand(gate1)
    return pl.pallas_call(
        body, name="ffn_up_bwd", grid=(nt,),
        in_specs=[wide, nxt, wide, nxt, wide, wide,
                  _const_spec((3, 2 * DFF)), _const_spec((D, 2 * DFF), True),
                  tile, tile, vec, sc_spec, tile, g1_spec],
        out_specs=[pl.BlockSpec((tm, 2 * DFF), lambda i: (i, 0)), tile, tile, _const_spec((3, 2 * DFF)),
                   vec, vec, vec, vec],
        out_shape=[jax.ShapeDtypeStruct((t, 2 * DFF), BF16), jax.ShapeDtypeStruct((t, D), F32),
                   jax.ShapeDtypeStruct((t, D), BF16), jax.ShapeDtypeStruct((3, 2 * DFF), F32),
                   vshape, vshape, vshape, vshape],
        compiler_params=_cparams(1),
    )(dact, dact, dval, dval, up_a, up_v, cw, wup, x2, dx3, gffn, scale2, o1, gate1)


def _xt_y(a, b, name):
    t, k = a.shape
    n = b.shape[1]
    tm = min(1024, t)
    bn = 1536 if n % 1536 == 0 else D

    def body(a_ref, b_ref, o_ref):
        @pl.when(pl.program_id(1) == 0)
        def _():
            o_ref[...] = jnp.zeros_like(o_ref)

        o_ref[...] += _dot_tn(a_ref[...], b_ref[...])

    return pl.pallas_call(
        body, name=name, grid=(n // bn, t // tm),
        in_specs=[pl.BlockSpec((tm, k), lambda j, i: (i, 0)), pl.BlockSpec((tm, bn), lambda j, i: (i, j))],
        out_specs=pl.BlockSpec((k, bn), lambda j, i: (0, j)),
        out_shape=jax.ShapeDtypeStruct((k, n), F32),
        compiler_params=_cparams(2),
    )(a, b)


def _acc_spec(shape, index):
    return pl.BlockSpec(shape, lambda *_: index, pipeline_mode=pl.Buffered(1))


def _out_bwd(do1, wout, merged, ya, yb, z, h1):
    t = do1.shape[0]
    tm = _tile_big(t)

    def body(do1_ref, wo_ref, mg_ref, ya_ref, yb_ref, ga_ref, gb_ref, h1_ref,
             dya_ref, dyb_ref, dz_ref, dwo_ref, dwin_ref):
        @pl.when(pl.program_id(0) == 0)
        def _():
            dwo_ref[...] = jnp.zeros_like(dwo_ref)
            dwin_ref[...] = jnp.zeros_like(dwin_ref)

        do1v = do1_ref[...]
        dwo_ref[...] += _dot_tn(mg_ref[...], do1v)
        dm = _dot_nt(do1v, wo_ref[...])
        sa = _sigmoid(ga_ref[...].astype(F32))
        sb = _sigmoid(gb_ref[...].astype(F32))
        dya_ref[...] = (dm * sa).astype(BF16)
        dyb_ref[...] = (dm * sb).astype(BF16)
        dga = (dm * ya_ref[...].astype(F32) * sa * (1.0 - sa)).astype(BF16)
        dgb = (dm * yb_ref[...].astype(F32) * sb * (1.0 - sb)).astype(BF16)
        dz_ref[:, 0:D] = dga
        dz_ref[:, D:2 * D] = dgb
        h1v = h1_ref[...]
        dwin_ref[:, 0:D] += _dot_tn(h1v, dga)
        dwin_ref[:, D:2 * D] += _dot_tn(h1v, dgb)

    tile = pl.BlockSpec((tm, D), lambda i: (i, 0))
    bshape = jax.ShapeDtypeStruct((t, D), BF16)
    return pl.pallas_call(
        body, name="out_bwd", grid=(t // tm,),
        in_specs=[tile, _const_spec((D, D), True), tile, tile, tile,
                  pl.BlockSpec((tm, D), lambda i: (i, 4)), pl.BlockSpec((tm, D), lambda i: (i, 5)), tile],
        out_specs=[tile, tile, pl.BlockSpec((tm, 2 * D), lambda i: (i, 2)), _acc_spec((D, D), (0, 0)),
                   _acc_spec((D, 2 * D), (0, 2))],
        out_shape=[bshape, bshape, jax.ShapeDtypeStruct((t, NCOL_IN), BF16), jax.ShapeDtypeStruct((D, D), F32),
                   jax.ShapeDtypeStruct((D, NCOL_IN), F32)],
        compiler_params=_cparams(1),
    )(do1, wout, merged, ya, yb, z, z, h1)


def _rnn_bwd(dya, ya_pre, wba, h1, z, saved, h, dz, dwin, cw, wa, wx, lam):
    t = z.shape[0]
    tm = _tile_seq(t)
    nt = t // tm
    ngrp = tm // SUBLANES
    hpt = tm // HALO

    def body(dya_ref, yap_ref, wba_ref, h1_ref, xr_ref, xc_ref, ra_ref, ia_ref, gg_ref, hg_ref, h_ref, hp_ref,
             dz_any, dwin_any, cw_ref, wa_ref, wx_ref, lam_ref,
             dz_ref, dwin_ref, dwba_ref, dcw_ref, dcb_ref, dwa_ref, dba_ref, dwx_ref, dbx_ref, dlam_ref,
             a_first, g_first, dxc_first, b_scr, d_scr, g_scr):
        del dz_any, dwin_any
        i = pl.program_id(0)

        @pl.when(i == 0)
        def _():
            for r in (dwin_ref, dwba_ref, dcw_ref, dcb_ref, dwa_ref, dba_ref, dwx_ref, dbx_ref, dlam_ref,
                      a_first, g_first, dxc_first):
                r[...] = jnp.zeros_like(r)

        dya_v = dya_ref[...]
        dwba_ref[...] += _dot_tn(yap_ref[...], dya_v)
        dyap_v = _dot_nt(dya_v, wba_ref[...])
        h1v = h1_ref[...]

        first_tile = i == nt - 1
        xc = xc_ref[...].astype(F32)
        ra = ra_ref[...].astype(F32)
        ia = ia_ref[...].astype(F32)
        lam_v = lam_ref[...]
        ls = _log_sigmoid(lam_v)
        la = LRU_C * ra * ls
        a = jnp.exp(la)
        mult = jnp.sqrt(-jnp.tanh(la) * (1.0 + a * a))
        hprev8 = jnp.where(first_tile, 0.0, hp_ref[...][HALO - SUBLANES:])
        h_prev = _shift_down(h_ref[...], 1, hprev8)
        dgr = (dyap_v * hg_ref[...].astype(F32)).astype(BF16)
        dz_ref[:, D:2 * D] = dgr
        dwin_ref[:, D:2 * D] += _dot_tn(h1v, dgr)

        b_scr[...] = _shift_up(a, 1, a_first[...])
        d_scr[...] = dyap_v * gg_ref[...].astype(F32)
        row = _row_iota(D)

        def grp(jj, carry):
            r0 = pl.multiple_of((ngrp - 1 - jj) * SUBLANES, SUBLANES)
            bv = b_scr[pl.ds(r0, SUBLANES), :]
            dv = d_scr[pl.ds(r0, SUBLANES), :]
            for d in (1, 2, 4):
                m = row < SUBLANES - d
                dv = jnp.where(m, dv + bv * pltpu.roll(dv, SUBLANES - d, 0), dv)
                bv = jnp.where(m, bv * pltpu.roll(bv, SUBLANES - d, 0), bv)
            gv = dv + bv * carry
            g_scr[pl.ds(r0, SUBLANES), :] = gv
            return gv[0:1, :]

        lax.fori_loop(0, ngrp, grp, g_first[0:1, :])
        g = g_scr[...]
        a_first[...] = a[:SUBLANES]
        g_first[...] = g[:SUBLANES]

        da = g * h_prev
        gx = g * xc
        dmult = gx * ia
        dia = gx * mult
        dxc = g * (mult * ia)
        dla = da * a - dmult * (a * a) / mult
        dra = dla * (LRU_C * ls)
        dlam_ref[...] += _colsum(dla * ra) * (LRU_C * _sigmoid(-lam_v))
        dpa = dra * ra * (1.0 - ra)
        dpx = dia * ia * (1.0 - ia)
        dba_ref[...] += _colsum(dpa)
        dbx_ref[...] += _colsum(dpx)
        dpab = dpa.astype(BF16)
        dpxb = dpx.astype(BF16)
        xcb = xc_ref[...]
        for hd in range(NH):
            sl = slice(hd * HD, (hd + 1) * HD)
            dwa_ref[hd] += _dot_tn(xcb[:, sl], dpab[:, sl])
            dwx_ref[hd] += _dot_tn(xcb[:, sl], dpxb[:, sl])
        dxc = dxc + _heads_nt(dpab, wa_ref) + _heads_nt(dpxb, wx_ref)

        nxt = dxc_first[...]
        taps = (_shift_up(dxc, 3, nxt), _shift_up(dxc, 2, nxt), _shift_up(dxc, 1, nxt), dxc)
        dxr = cw_ref[0:1, :] * taps[0]
        for k in range(1, 4):
            dxr = dxr + cw_ref[k:k + 1, :] * taps[k]
        dxrb = dxr.astype(BF16)
        dz_ref[:, 0:D] = dxrb
        dwin_ref[:, 0:D] += _dot_tn(h1v, dxrb)
        dxc_first[...] = dxc[:SUBLANES]
        dcb_ref[...] += _colsum(dxc)
        xr = xr_ref[...].astype(F32)
        for k in range(4):
            dcw_ref[k:k + 1, :] += _colsum(taps[k] * xr)

    def rev(col):
        return lambda i: (nt - 1 - i, col)

    vec = _const_spec((1, D))
    wspec = _const_spec((NH, HD, HD))
    vshape = jax.ShapeDtypeStruct((1, D), F32)
    wshape = jax.ShapeDtypeStruct((NH, HD, HD), F32)
    any_spec = pl.BlockSpec(memory_space=pl.ANY)
    tile = pl.BlockSpec((tm, D), rev(0))
    outs = pl.pallas_call(
        body, name="rnn_bwd", grid=(nt,),
        in_specs=[tile, tile, _const_spec((D, D), True), tile, tile, tile, tile, tile, tile, tile, tile,
                  pl.BlockSpec((HALO, D), lambda i: (jnp.maximum((nt - 1 - i) * hpt - 1, 0), 0)),
                  any_spec, any_spec, _const_spec((4, D)), wspec, wspec, vec],
        out_specs=[pl.BlockSpec((tm, 2 * D), rev(0)), _acc_spec((D, 2 * D), (0, 0)), _acc_spec((D, D), (0, 0)),
                   _const_spec((4, D)), vec, wspec, vec, wspec, vec, vec],
        out_shape=[jax.ShapeDtypeStruct((t, NCOL_IN), BF16), jax.ShapeDtypeStruct((D, NCOL_IN), F32),
                   jax.ShapeDtypeStruct((D, D), F32), jax.ShapeDtypeStruct((4, D), F32), vshape,
                   wshape, vshape, wshape, vshape, vshape],
        scratch_shapes=[pltpu.VMEM((SUBLANES, D), F32), pltpu.VMEM((SUBLANES, D), F32), pltpu.VMEM((SUBLANES, D), F32),
                        pltpu.VMEM((tm, D), F32), pltpu.VMEM((tm, D), F32), pltpu.VMEM((tm, D), F32)],
        input_output_aliases={12: 0, 13: 1},
        compiler_params=_cparams(1),
    )(dya, ya_pre, wba, h1, z, *saved, h, h, dz, dwin, cw, wa, wx, lam)
    return outs


def _sgu_bwd(dyb, yb_pre, wbb, h1, saved, dz, dwin, lng, lnb, wmt, mask):
    t = dyb.shape[0]
    tm = _tile_big(t)

    def body(dyb_ref, ybp_ref, wbb_ref, h1_ref, gu_ref, mg_ref, vh_ref, gpv_ref, rstd_ref, dz_any, dwin_any,
             lng_ref, lnb_ref, wmt_ref, mask_ref,
             dz_ref, dwin_ref, dwbb_ref, dws_ref, dbst_ref, dlng_ref, dlnb_ref):
        del dz_any, dwin_any

        @pl.when(pl.program_id(0) == 0)
        def _():
            for r in (dwin_ref, dwbb_ref, dws_ref, dbst_ref, dlng_ref, dlnb_ref):
                r[...] = jnp.zeros_like(r)

        lng_v = lng_ref[...]
        vhat = vh_ref[...].astype(F32)
        vb = (vhat * lng_v + lnb_ref[...]).astype(BF16)
        rstd = rstd_ref[...]
        dyb_v = dyb_ref[...]
        dwbb_ref[...] += _dot_tn(ybp_ref[...], dyb_v)
        dyb = _dot_nt(dyb_v, wbb_ref[...])
        h1v = h1_ref[...]
        dzu = (dyb * mg_ref[...].astype(F32)).astype(BF16)
        dz_ref[:, 0:D] = dzu
        dwin_ref[:, 0:D] += _dot_tn(h1v, dzu)
        dmix = dyb * gu_ref[...].astype(F32)
        dmb = dmix.astype(BF16)
        rows = []
        lane = lax.broadcasted_iota(jnp.int32, (HD, NH), 1)
        dbst = jnp.zeros((HD, NH), F32)
        for b0 in range(0, tm, HD):
            cols = []
            for g in range(NH):
                sl = slice(g * HD, (g + 1) * HD)
                dmg = dmb[b0:b0 + HD, sl]
                dws_ref[g] += _dot_nt(dmg, vb[b0:b0 + HD, sl]) * mask_ref[...]
                cols.append(jnp.dot(wmt_ref[g], dmg, preferred_element_type=F32))
                dbst = dbst + jnp.where(lane == g, jnp.sum(dmix[b0:b0 + HD, sl], axis=1, keepdims=True), 0.0)
            rows.append(jnp.concatenate(cols, axis=1))
        dbst_ref[...] += dbst
        dvln = jnp.concatenate(rows, axis=0) if len(rows) > 1 else rows[0]
        dlng_ref[...] += _colsum(dvln * vhat)
        dlnb_ref[...] += _colsum(dvln)
        dvh = dvln * lng_v
        dgv = rstd * (dvh - jnp.mean(dvh, axis=-1, keepdims=True)
                      - vhat * jnp.mean(dvh * vhat, axis=-1, keepdims=True))
        dzv = (dgv * gpv_ref[...].astype(F32)).astype(BF16)
        dz_ref[:, D:2 * D] = dzv
        dwin_ref[:, D:2 * D] += _dot_tn(h1v, dzv)

    vec = _const_spec((1, D))
    wspec = _const_spec((NH, HD, HD))
    vshape = jax.ShapeDtypeStruct((1, D), F32)
    tile = pl.BlockSpec((tm, D), lambda i: (i, 0))
    any_spec = pl.BlockSpec(memory_space=pl.ANY)
    return pl.pallas_call(
        body, name="sgu_bwd", grid=(t // tm,),
        in_specs=[tile, tile, _const_spec((D, D), True), tile, tile, tile, tile, tile,
                  pl.BlockSpec((tm, 1), lambda i: (i, 0)), any_spec, any_spec,
                  vec, vec, wspec, _const_spec((HD, HD))],
        out_specs=[pl.BlockSpec((tm, 2 * D), lambda i: (i, 1)), _acc_spec((D, 2 * D), (0, 1)), _acc_spec((D, D), (0, 0)),
                   wspec, _const_spec((HD, NH)), vec, vec],
        out_shape=[jax.ShapeDtypeStruct((t, NCOL_IN), BF16), jax.ShapeDtypeStruct((D, NCOL_IN), F32),
                   jax.ShapeDtypeStruct((D, D), F32), jax.ShapeDtypeStruct((NH, HD, HD), F32),
                   jax.ShapeDtypeStruct((HD, NH), F32), vshape, vshape],
        input_output_aliases={9: 0, 10: 1},
        compiler_params=_cparams(1),
    )(dyb, yb_pre, wbb, h1, *saved, dz, dwin, lng, lnb, wmt, mask)


def _in_bwd(dz, win, x, dx2, g, scale1):
    t = x.shape[0]
    tm = _tile_big(t)

    def body(dz_ref, w_ref, x_ref, dx2_ref, g_ref, sc_ref, dx_ref, dsh_ref, dsc_ref, dg_ref):
        @pl.when(pl.program_id(0) == 0)
        def _():
            for r in (dsh_ref, dsc_ref, dg_ref):
                r[...] = jnp.zeros_like(r)

        dh = jnp.zeros((tm, D), F32)
        for c0 in range(0, NCOL_IN, D):
            dh = dh + _dot_nt(dz_ref[:, c0:c0 + D], w_ref[:, c0:c0 + D])
        dxn, dsh, dsc, dg = _modnorm_bwd(dh, x_ref[...], g_ref[...], sc_ref[...])
        dx_ref[...] = dx2_ref[...] + dxn
        dsh_ref[...] += dsh
        dsc_ref[...] += dsc
        dg_ref[...] += dg

    tile = pl.BlockSpec((tm, D), lambda i: (i, 0))
    vec = _const_spec((1, D))
    vshape = jax.ShapeDtypeStruct((1, D), F32)
    scale1, sc_spec = _vec_operand(scale1)
    return pl.pallas_call(
        body, name="in_bwd", grid=(t // tm,),
        in_specs=[pl.BlockSpec((tm, NCOL_IN), lambda i: (i, 0)), _const_spec((D, NCOL_IN), True), tile, tile, vec,
                  sc_spec],
        out_specs=[tile, vec, vec, vec],
        out_shape=[jax.ShapeDtypeStruct((t, D), F32), vshape, vshape, vshape],
        compiler_params=_cparams(1),
    )(dz, win, x, dx2, g, scale1)


def _mod_cols(c_all, w_ada, b_cols):
    nb, cols = c_all.shape[0], w_ada.shape[1]

    def body(c_ref, w_ref, b_ref, o_ref):
        cv = c_ref[...]
        ca = (cv * _sigmoid(cv)).astype(BF16)
        o_ref[...] = jnp.dot(ca, w_ref[...].astype(BF16), preferred_element_type=F32) + b_ref[...]

    return pl.pallas_call(body, name="mod_cols", out_shape=jax.ShapeDtypeStruct((nb, cols), F32))(c_all, w_ada, b_cols)


def _ada_grad(c_all, dmod_cols):
    cols = dmod_cols.shape[1]

    def body(c_ref, d_ref, o_ref):
        cv = c_ref[...]
        ca = (cv * _sigmoid(cv)).astype(BF16)
        o_ref[...] = _dot_tn(ca, d_ref[...].astype(BF16))

    return pl.pallas_call(body, name="ada_grad", out_shape=jax.ShapeDtypeStruct((D, cols), F32))(c_all, dmod_cols)


def _adamw_update(w, m, v, g):
    bc1 = 1.0 - ADAM_B1 ** ADAM_STEP
    bc2 = 1.0 - ADAM_B2 ** ADAM_STEP
    mn = ADAM_B1 * m + (1.0 - ADAM_B1) * g
    vn = ADAM_B2 * v + (1.0 - ADAM_B2) * (g * g)
    return -ADAM_LR * ((mn / bc1) / (jnp.sqrt(vn / bc2) + ADAM_EPS) + ADAM_WD * w), mn, vn


def _adamw_group(names, ws, ms, vs, packs, name):
    n = len(names)
    starts, r0 = [], 0
    for w in ws:
        starts.append(r0)
        r0 += _pack_rows(w.shape)

    def body(*refs):
        w_refs, m_refs, v_refs, p_ref = refs[:n], refs[n:2 * n], refs[2 * n:3 * n], refs[3 * n]
        outs = refs[3 * n + 1:]
        for k in range(n):
            rows = _pack_rows(ws[k].shape)
            g = None
            for dev in range(N_DEV):
                if ws[k].shape[0] == 1:
                    term = jnp.concatenate(
                        [p_ref[dev, starts[k] + r:starts[k] + r + 1, :] for r in range(rows)], axis=1)
                else:
                    term = p_ref[dev, starts[k]:starts[k] + rows, :]
                g = term if g is None else g + term
            delta, mn, vn = _adamw_update(w_refs[k][...], m_refs[k][...], v_refs[k][...], g)
            for o_ref, val in zip(outs[4 * k:4 * k + 4], (g, delta, mn, vn)):
                o_ref[...] = val

    shapes = [jax.ShapeDtypeStruct(w.shape, F32) for w in ws for _ in range(4)]
    outs = pl.pallas_call(body, name=name, out_shape=shapes,
                          compiler_params=pltpu.CompilerParams(vmem_limit_bytes=VMEM_LIMIT))(*ws, *ms, *vs, packs)
    return {nm: tuple(outs[4 * k:4 * k + 4]) for k, nm in enumerate(names)}


def _adamw(w, m, v, parts, name):
    rows, cols = w.shape
    tr = _row_tile(rows, cols)
    stacked = [p.ndim == 3 for p in parts]

    def body(*refs):
        w_ref, m_ref, v_ref = refs[:3]
        p_refs = refs[3:3 + len(parts)]
        g_ref, d_ref, mo_ref, vo_ref = refs[3 + len(parts):]
        g = None
        for p_ref, st in zip(p_refs, stacked):
            terms = [p_ref[k].astype(F32) for k in range(p_ref.shape[0])] if st else [p_ref[...].astype(F32)]
            for term in terms:
                g = term if g is None else g + term
        delta, mn, vn = _adamw_update(w_ref[...], m_ref[...], v_ref[...], g)
        g_ref[...] = g
        mo_ref[...] = mn
        vo_ref[...] = vn
        d_ref[...] = delta

    tile = pl.BlockSpec((tr, cols), lambda i: (i, 0))
    p_specs = [pl.BlockSpec((p.shape[0], tr, cols), lambda i: (0, i, 0)) if st else tile for p, st in zip(parts, stacked)]
    shp = jax.ShapeDtypeStruct((rows, cols), F32)
    return pl.pallas_call(
        body, name=name, grid=(rows // tr,),
        in_specs=[tile, tile, tile] + p_specs, out_specs=[tile] * 4, out_shape=[shp] * 4,
        compiler_params=_cparams(1),
    )(w, m, v, *parts)


def _mesh_pos():
    return lax.axis_index("x"), lax.axis_index("y"), lax.axis_index("c")


def _other_chips(x, y):
    return [(1 - x, y), (x, 1 - y), (1 - x, 1 - y)]


def _block_of(ref, axis, index, size):
    if axis == 0:
        return ref.at[index]
    return ref.at[:, pl.ds(pl.multiple_of(index * size, 128), size)]


def _all_gather(shards, axes, name):
    n = len(shards)
    per = 7

    def body(*refs):
        ins, outs, done = refs[:n], refs[n:2 * n], refs[2 * n]
        send_sems, recv_sems, local_sems = refs[2 * n + 1:]
        x, y, c = _mesh_pos()
        me, sibling = (x, y, c), (x, y, 1 - c)
        chips = _other_chips(x, y)

        def rows(a, pos):
            return _block_of(outs[a], axes[a], 4 * pos[0] + 2 * pos[1] + pos[2], shards[a].shape[-1])

        def copy(a, k, block, to, src=None):
            return pltpu.make_async_remote_copy(
                src_ref=rows(a, block) if src is None else src, dst_ref=rows(a, block),
                send_sem=send_sems.at[a * per + k], recv_sem=recv_sems.at[a * per + k],
                device_id=to, device_id_type=MESH_IDS)

        mine = [pltpu.make_async_copy(ins[a], rows(a, me), local_sems.at[a]) for a in range(n)]
        for cp in mine:
            cp.start()
        first = []
        for a in range(n):
            first.append(copy(a, 0, me, sibling, src=ins[a]))
            first += [copy(a, 1 + j, me, (*chip, c), src=ins[a]) for j, chip in enumerate(chips)]
        for cp in first:
            cp.start()
        passed = []
        for j, chip in enumerate(chips):
            for a in range(n):
                copy(a, 1 + j, (*chip, c), me).wait_recv()
                fwd = copy(a, 4 + j, (*chip, c), sibling)
                fwd.start()
                passed.append(fwd)
        for a in range(n):
            copy(a, 0, sibling, me).wait_recv()
            for j, chip in enumerate(chips):
                copy(a, 4 + j, (*chip, 1 - c), me).wait_recv()
        for cp in first + passed:
            cp.wait_send()
        for cp in mine:
            cp.wait()
        done[...] = jnp.zeros_like(done)

    def full_shape(s, ax):
        return (N_DEV,) + s.shape if ax == 0 else s.shape[:-1] + (N_DEV * s.shape[-1],)

    any_spec = pl.BlockSpec(memory_space=pl.ANY)
    outs = pl.pallas_call(
        body, name=name,
        in_specs=[any_spec] * n, out_specs=[any_spec] * n + [pl.BlockSpec(memory_space=pltpu.VMEM)],
        out_shape=[jax.ShapeDtypeStruct(full_shape(s, ax), s.dtype) for s, ax in zip(shards, axes)]
        + [jax.ShapeDtypeStruct((SUBLANES, LANES), F32)],
        scratch_shapes=[pltpu.SemaphoreType.DMA((n * per,)), pltpu.SemaphoreType.DMA((n * per,)),
                        pltpu.SemaphoreType.DMA((n,))],
    )(*shards)
    return outs[:n], outs[n]


def _chip_blocks(x, y):
    return [(x, y)] + _other_chips(x, y)


def _sibling_reduce(gs, axis, name):
    g0, n = gs[0], len(gs)
    rows, cols = (g0.shape[1], g0.shape[2]) if axis == 0 else (g0.shape[0], g0.shape[1] // N_DEV)
    chunk = math.gcd(rows, 64)

    def body(*refs):
        g_refs, own_refs, pay_refs = refs[:n], refs[n:2 * n], refs[2 * n:3 * n]
        send_buf, keep_buf, recv_buf, pay_buf, send_sems, recv_sems, stage_sems, keep_sems, out_sems = refs[3 * n:]
        x, y, c = _mesh_pos()
        sibling = (x, y, 1 - c)
        chips = _chip_blocks(x, y)
        stage, keep, push = [], [], []
        for a in range(n):
            for j, (px, py) in enumerate(chips):
                s = 4 * a + j
                theirs = _block_of(g_refs[a], axis, 4 * px + 2 * py + (1 - c), cols)
                ours = _block_of(g_refs[a], axis, 4 * px + 2 * py + c, cols)
                stage.append(pltpu.make_async_copy(theirs, send_buf.at[s], stage_sems.at[s]))
                keep.append(pltpu.make_async_copy(ours, keep_buf.at[s], keep_sems.at[s]))
                push.append(pltpu.make_async_remote_copy(
                    src_ref=send_buf.at[s], dst_ref=recv_buf.at[s], send_sem=send_sems.at[s],
                    recv_sem=recv_sems.at[s], device_id=sibling, device_id_type=MESH_IDS))
        for cp in stage + keep:
            cp.start()
        for s in range(4 * n):
            stage[s].wait()
            push[s].start()
        written = []
        for s in range(4 * n):
            push[s].wait_recv()
            keep[s].wait()
            a, j = divmod(s, 4)
            res = keep_buf.at[s] if j == 0 else pay_buf.at[3 * a + j - 1]

            def add(r, carry, s=s, res=res):
                sl = pl.ds(pl.multiple_of(r * chunk, chunk), chunk)
                res[sl, :] = (keep_buf[s, sl, :] + recv_buf[s, sl, :]).astype(res.dtype)
                return carry

            lax.fori_loop(0, rows // chunk, add, 0)
            out = pltpu.make_async_copy(res, own_refs[a] if j == 0 else pay_refs[a].at[j - 1], out_sems.at[s])
            out.start()
            written.append(out)
        for cp in push:
            cp.wait_send()
        for cp in written:
            cp.wait()

    any_spec = pl.BlockSpec(memory_space=pl.ANY)
    buf = pltpu.VMEM((4 * n, rows, cols), F32)
    sems = pltpu.SemaphoreType.DMA((4 * n,))
    outs = pl.pallas_call(
        body, name=name,
        in_specs=[any_spec] * n, out_specs=[any_spec] * (2 * n),
        out_shape=[jax.ShapeDtypeStruct((rows, cols), F32)] * n + [jax.ShapeDtypeStruct((3, rows, cols), BF16)] * n,
        scratch_shapes=[buf, buf, buf, pltpu.VMEM((3 * n, rows, cols), BF16), sems, sems, sems, sems, sems],
        compiler_params=pltpu.CompilerParams(vmem_limit_bytes=VMEM_LIMIT),
    )(*gs)
    return list(zip(outs[:n], outs[n:]))


_HBM_SPEC = pl.BlockSpec(memory_space=pltpu.HBM)
_SEM_SPEC = pl.BlockSpec(memory_space=pltpu.SEMAPHORE)
_SIDE_EFFECT = pltpu.SideEffectType.DATAFLOW_SIDE_EFFECTING


def _exchange_start(name, srcs, lands, plan, n_copies):
    nb = len(srcs) + len(lands)

    def body(*refs):
        bufs, send_sems, recv_sems, token = refs[:nb], refs[nb], refs[nb + 1], refs[-1]
        for cp in plan(bufs[:len(srcs)], bufs[len(srcs):], send_sems, recv_sems):
            cp.start()
        token[...] = jnp.zeros_like(token)

    arrays = list(srcs) + list(lands)
    outs = pl.pallas_call(
        body, name=name,
        out_shape=(pltpu.SemaphoreType.DMA((n_copies,)), pltpu.SemaphoreType.DMA((n_copies,)),
                   *[pltpu.HBM(a.shape, a.dtype) for a in arrays], jax.ShapeDtypeStruct((SUBLANES, LANES), F32)),
        in_specs=[_HBM_SPEC] * nb,
        out_specs=(_SEM_SPEC, _SEM_SPEC, *[_HBM_SPEC] * nb, pl.BlockSpec(memory_space=pltpu.VMEM)),
        input_output_aliases={k: 2 + k for k in range(nb)},
        compiler_params=pltpu.CompilerParams(has_side_effects=_SIDE_EFFECT),
    )(*[pltpu.with_memory_space_constraint(a, pltpu.HBM) for a in arrays])
    return outs[0], outs[1], outs[2:2 + len(srcs)], outs[2 + len(srcs):2 + nb], outs[-1]


def _exchange_wait(name, send_sems, recv_sems, srcs, lands, plan, after):
    nb = len(srcs) + len(lands)
    after = list(after)

    def body(*refs):
        bufs, send_ref, recv_ref = refs[:nb], refs[nb], refs[nb + 1]
        for cp in plan(bufs[:len(srcs)], bufs[len(srcs):], send_ref, recv_ref):
            cp.wait_send()
            cp.wait_recv()

    arrays = list(srcs) + list(lands)
    outs = pl.pallas_call(
        body, name=name,
        out_shape=tuple(pltpu.HBM(a.shape, a.dtype) for a in arrays),
        in_specs=[_HBM_SPEC] * nb + [_SEM_SPEC, _SEM_SPEC] + [pl.BlockSpec(memory_space=pl.ANY)] * len(after),
        out_specs=tuple([_HBM_SPEC] * nb),
        input_output_aliases={k: k for k in range(nb)},
        compiler_params=pltpu.CompilerParams(has_side_effects=_SIDE_EFFECT),
    )(*arrays, send_sems, recv_sems, *after)
    return outs[len(srcs):]


def _gather_plan(axes, sizes):
    def plan(src_refs, land_refs, send_sems, recv_sems):
        x, y, c = _mesh_pos()
        copies = []
        for a, (src, land) in enumerate(zip(src_refs, land_refs)):
            mine = _block_of(land, axes[a], 4 * x + 2 * y + c, sizes[a])
            for k in range(1, N_DEV):
                peer = (1 - x if k & 4 else x, 1 - y if k & 2 else y, 1 - c if k & 1 else c)
                idx = a * (N_DEV - 1) + k - 1
                copies.append(pltpu.make_async_remote_copy(
                    src_ref=src, dst_ref=mine, send_sem=send_sems.at[idx], recv_sem=recv_sems.at[idx],
                    device_id=peer, device_id_type=MESH_IDS))
        return copies
    return plan


def _chip_plan(src_refs, land_refs, send_sems, recv_sems):
    x, y, c = _mesh_pos()
    copies = []
    for a, (src, land) in enumerate(zip(src_refs, land_refs)):
        for j, chip in enumerate(_other_chips(x, y)):
            copies.append(pltpu.make_async_remote_copy(
                src_ref=src.at[j], dst_ref=land.at[j], send_sem=send_sems.at[3 * a + j],
                recv_sem=recv_sems.at[3 * a + j], device_id=(*chip, c), device_id_type=MESH_IDS))
    return copies


def _own_block_placed(shard, axis, me):
    if axis == 0:
        full = lax.empty((N_DEV,) + shard.shape, shard.dtype)
        return lax.dynamic_update_slice(full, shard[None], (me,) + (0,) * shard.ndim)
    rows, cols = shard.shape

    def body(me_ref, s_ref, o_ref):
        del me_ref
        o_ref[...] = s_ref[...]

    return pl.pallas_call(
        body, name="place_own_columns",
        grid_spec=pltpu.PrefetchScalarGridSpec(
            num_scalar_prefetch=1, grid=(1,),
            in_specs=[pl.BlockSpec((rows, cols), lambda i, me_ref: (0, 0))],
            out_specs=pl.BlockSpec((rows, cols), lambda i, me_ref: (0, me_ref[0]))),
        out_shape=jax.ShapeDtypeStruct((rows, N_DEV * cols), shard.dtype),
    )(jnp.reshape(me, (1,)).astype(jnp.int32), shard)


def _local_step(x, target, mod, win, late_weights, p, grads_ready=None):
    shift1, scale1, gate1, shift2, scale2, gate2 = ((mod, k) for k in range(6))

    def after_token(v, token):
        return v if token is None else v + token[0:1, 0:1]
    wa, wx = p["lru_w_a"].astype(BF16), p["lru_w_x"].astype(BF16)
    mask = jnp.tril(jnp.ones((HD, HD), F32))
    wm = (p["sgu_w_s"] * mask).astype(BF16)
    wmt = jnp.swapaxes(wm, 1, 2)
    bst = jnp.transpose(p["sgu_b_s"])

    h1, z = _norm_proj(x, p["norm_mix_g"], scale1, shift1, win, "mix_proj")
    hstate, ya_pre, *rnn_saved = _rnn_fwd(
        z, p["rnn_conv_w"], p["rnn_conv_b"], wa, p["lru_b_a"], wx, p["lru_b_x"], p["lru_lambda"])
    yb_pre, *sgu_saved = _sgu_fwd(z, p["sgu_ln_g"], p["sgu_ln_b"], wm, bst)
    wba, wbb, wout = late_weights("merge", [ya_pre, yb_pre])
    x2, ya, yb, merged, o1 = _merge_fwd(ya_pre, yb_pre, z, x, gate1, wba, wbb, wout)
    wup = late_weights("ffn_up", [x2])
    h2, up_a, up_v, ff, fa, fv = _ffn_proj_mid(
        x2, p["norm_ffn_g"], scale2, shift2, wup, p["ffn_conv_w"], p["ffn_conv_b"])
    wd = late_weights("ffn_down", [ff])
    dx3, loss, d_gfin, d_gate2 = _ffn_out_loss(ff, wd, x2, target, gate2, p["norm_final_g"])

    dact, dval, d_wd, dcb_a, dcb_v = _ffn_down_bwd(dx3, gate2, ff, fa, fv, wd)
    dup, dx2, do1, d_cwf, d_shift2, d_scale2, d_gffn, d_gate1 = _ffn_up_bwd(
        dact, dval, up_a, up_v, p["ffn_conv_w"], wup, x2, dx3, p["norm_ffn_g"], scale2, o1, gate1)
    d_wup = _xt_y(h2, dup, "w_up_grad")
    ready = grads_ready if grads_ready else (lambda stage, big, small: None)
    token = ready("ffn", {"w_up": d_wup, "w_down": d_wd}, {})

    dya, dyb, dz, d_wout, d_win = _out_bwd(do1, wout, merged, ya, yb, z, h1)
    dz, d_win, d_wba, d_cw, d_cb, d_wa, d_ba, d_wx, d_bx, d_lam = _rnn_bwd(
        dya, ya_pre, wba, h1, z, rnn_saved, hstate, dz, d_win, p["rnn_conv_w"], wa, wx,
        after_token(p["lru_lambda"], token))
    small = {
        "rnn_conv_w": d_cw, "rnn_conv_b": d_cb, "lru_w_a": d_wa, "lru_b_a": d_ba, "lru_w_x": d_wx, "lru_b_x": d_bx,
        "lru_lambda": d_lam, "norm_ffn_g": d_gffn, "ffn_conv_w": d_cwf,
        "ffn_conv_b": jnp.concatenate([dcb_a, dcb_v], axis=1), "norm_final_g": d_gfin,
    }
    token = ready("rnn", {}, small)
    dz, d_win, d_wbb, d_ws, d_bst, d_lng, d_lnb = _sgu_bwd(
        dyb, yb_pre, wbb, h1, sgu_saved, dz, d_win, p["sgu_ln_g"], after_token(p["sgu_ln_b"], token), wmt, mask)
    sgu_small = {"sgu_ln_g": d_lng, "sgu_ln_b": d_lnb, "sgu_w_s": d_ws, "sgu_b_s": jnp.transpose(d_bst)}
    mixer = {"w_in": d_win, "w_out": d_wout, "w_branch_a": d_wba, "w_branch_b": d_wbb}
    token = ready("mixer", mixer, sgu_small)
    grad_x, d_shift1, d_scale1, d_gmix = _in_bwd(dz, win, x, dx2, after_token(p["norm_mix_g"], token), scale1)

    small.update(sgu_small)
    small["norm_mix_g"] = d_gmix
    dmod = jnp.stack([d_shift1, d_scale1, d_gate1, d_shift2, d_scale2, d_gate2])
    big = {"w_in": d_win, "w_up": d_wup, "w_branch_a": d_wba, "w_branch_b": d_wbb, "w_out": d_wout, "w_down": d_wd}
    return loss, grad_x, big, small, dmod


LAST_REP = ["b_ada", "norm_mix_g"]
EARLY_REP = ["rnn_conv_b", "lru_w_a", "lru_b_a", "lru_w_x", "lru_b_x", "lru_lambda", "norm_ffn_g", "ffn_conv_b",
             "norm_final_g"]
MID_REP = ["sgu_ln_g", "sgu_ln_b", "sgu_w_s", "sgu_b_s"]
COL_SHARDED = ["rnn_conv_w", "ffn_conv_w"]
SMALL_GROUPS = {"rnn": EARLY_REP + COL_SHARDED, "mixer": MID_REP, "last": LAST_REP}
REPLICATED = LAST_REP + EARLY_REP + MID_REP
SMALL_NAMES = REPLICATED + COL_SHARDED
BIG_NAMES = ["w_in", "w_up", "w_branch_a", "w_branch_b", "w_out", "w_down"]
BIG_AXES = [1, 1, 0, 0, 0, 0]
WEIGHTS = ["w_ada", "b_ada", "norm_mix_g", "w_in", "rnn_conv_w", "rnn_conv_b", "lru_w_a", "lru_b_a", "lru_w_x",
           "lru_b_x", "lru_lambda", "sgu_ln_g", "sgu_ln_b", "sgu_w_s", "sgu_b_s", "w_branch_a", "w_branch_b",
           "w_out", "norm_ffn_g", "w_up", "ffn_conv_w", "ffn_conv_b", "w_down", "norm_final_g"]


def _pack_rows(shape):
    return math.prod(shape) // LANES


def _pack(arrays):
    return jnp.concatenate([a.reshape(-1, LANES) for a in arrays], axis=0)


def kernel(x, c, w_ada, b_ada, norm_mix_g, w_in, rnn_conv_w, rnn_conv_b, lru_w_a, lru_b_a, lru_w_x, lru_b_x, lru_lambda, sgu_ln_g, sgu_ln_b, sgu_w_s, sgu_b_s, w_branch_a, w_branch_b, w_out, norm_ffn_g, w_up, ffn_conv_w, ffn_conv_b, w_down, norm_final_g, loss_target, m_w_ada, m_b_ada, m_norm_mix_g, m_w_in, m_rnn_conv_w, m_rnn_conv_b, m_lru_w_a, m_lru_b_a, m_lru_w_x, m_lru_b_x, m_lru_lambda, m_sgu_ln_g, m_sgu_ln_b, m_sgu_w_s, m_sgu_b_s, m_w_branch_a, m_w_branch_b, m_w_out, m_norm_ffn_g, m_w_up, m_ffn_conv_w, m_ffn_conv_b, m_w_down, m_norm_final_g, v_w_ada, v_b_ada, v_norm_mix_g, v_w_in, v_rnn_conv_w, v_rnn_conv_b, v_lru_w_a, v_lru_b_a, v_lru_w_x, v_lru_b_x, v_lru_lambda, v_sgu_ln_g, v_sgu_ln_b, v_sgu_w_s, v_sgu_b_s, v_w_branch_a, v_w_branch_b, v_w_out, v_norm_ffn_g, v_w_up, v_ffn_conv_w, v_ffn_conv_b, v_w_down, v_norm_final_g):
    given = dict(locals())
    me = 4 * lax.axis_index("x") + 2 * lax.axis_index("y") + lax.axis_index("c")
    ada_cols = w_ada.shape[2]
    conv_cols = {"rnn_conv_w": rnn_conv_w.shape[2], "ffn_conv_w": ffn_conv_w.shape[2]}

    (win, c_all, cw_rnn, cw_ffn), _ = _all_gather(
        [w_in[0].astype(BF16), c.reshape(1, 1, D), rnn_conv_w[0], ffn_conv_w[0]], [1, 0, 1, 1], "gather_first")
    c_all = c_all.reshape(N_DEV, D)

    b_cols = lax.dynamic_slice_in_dim(b_ada, me * ada_cols, ada_cols, axis=1)
    (mod_all,), mod_done = _all_gather(
        [_mod_cols(c_all, w_ada[0], b_cols).reshape(1, N_DEV, ada_cols)], [0], "gather_mod")
    mod_all = mod_all.reshape(N_DEV, N_DEV, ada_cols)
    mod_mine = lax.dynamic_index_in_dim(mod_all, me, axis=1, keepdims=False).reshape(6, 1, D)

    late_groups = {"merge": (["w_branch_a", "w_branch_b", "w_out"], [0, 0, 0]), "ffn_up": (["w_up"], [1]),
                   "ffn_down": (["w_down"], [0])}
    in_flight, started = {}, mod_done[0:1, 0:1]
    for stage, (names, axes) in late_groups.items():
        shards = [(given[n][0] + started).astype(BF16) for n in names]
        plan = _gather_plan(axes, [s.shape[-1] for s in shards])
        send, recv, srcs, lands, token = _exchange_start(
            "gather_start_" + stage, shards, [_own_block_placed(s, ax, me) for s, ax in zip(shards, axes)], plan,
            len(shards) * (N_DEV - 1))
        in_flight[stage] = (send, recv, srcs, lands, plan)
        started = started + token[0:1, 0:1]

    def late_weights(stage, after):
        send, recv, srcs, lands, plan = in_flight[stage]
        full = _exchange_wait("gather_wait_" + stage, send, recv, srcs, lands, plan, after)
        full = [w.reshape(-1, D) if ax == 0 else w for w, ax in zip(full, late_groups[stage][1])]
        return full if len(full) > 1 else full[0]

    mod_mine = mod_mine + started

    reducing, packing = {}, {}

    def start_pack(stage, small):
        pack = _pack([small[n] for n in SMALL_GROUPS[stage]])[None]
        plan = _gather_plan([0], [LANES])
        send, recv, srcs, lands, tok = _exchange_start(
            "small_start_" + stage, [pack], [_own_block_placed(pack, 0, me)], plan, N_DEV - 1)
        packing[stage] = (send, recv, srcs, lands, plan)
        return tok

    def grads_ready(stage, grads, small):
        tokens = [start_pack(stage, small)] if small else []
        if grads:
            tokens.append(start_reduce(stage, grads))
        return sum(tokens[1:], tokens[0])

    def start_reduce(stage, grads):
        names = [n for n in BIG_NAMES if n in grads]
        blocked = {}
        for n in names:
            ax = BIG_AXES[BIG_NAMES.index(n)]
            g = grads[n] if ax == 1 else grads[n].reshape(N_DEV, grads[n].shape[0] // N_DEV, grads[n].shape[1])
            blocked.setdefault((ax, g.shape), []).append((n, g))
        sums = {}
        for (ax, _), group in blocked.items():
            reduced = _sibling_reduce([g for _, g in group], ax, "reduce_sibling_" + "_".join(n for n, _ in group))
            sums.update({n: r for (n, _), r in zip(group, reduced)})
        sums = [sums[n] for n in names]
        pays = [pay for _, pay in sums]
        send, recv, srcs, lands, tok = _exchange_start(
            "reduce_start_" + stage, pays, [lax.empty(p_.shape, p_.dtype) for p_ in pays], _chip_plan, 3 * len(pays))
        reducing[stage] = (names, [own for own, _ in sums], send, recv, srcs, lands)
        return tok

    p = {n: given[n][0] for n in REPLICATED if n not in ("b_ada", "norm_final_g")}
    p = {n: (a.reshape(1, -1) if a.ndim == 1 else a) for n, a in p.items()}
    p["rnn_conv_w"], p["ffn_conv_w"] = cw_rnn, cw_ffn
    p["norm_final_g"] = norm_final_g.reshape(1, D)
    loss, grad_x, _, small, dmod = _local_step(x[0], loss_target[0], mod_mine, win, late_weights, p, grads_ready)

    small["b_ada"] = dmod.reshape(1, 6 * D)
    rows_of = {n: _pack_rows(small[n].shape) for n in SMALL_NAMES}
    (last,), _ = _all_gather([_pack([small[n] for n in LAST_REP])[None]], [0], "gather_small")
    gathered = {"last": last}
    for stage, (send, recv, srcs, lands, plan) in packing.items():
        (gathered[stage],) = _exchange_wait("small_wait_" + stage, send, recv, srcs, lands, plan, [grad_x])
    gathered = {k: v.reshape(N_DEV, -1, LANES) for k, v in gathered.items()}

    out = {}
    for stage, (names, owns, send, recv, srcs, lands) in reducing.items():
        landed = _exchange_wait("reduce_wait_" + stage, send, recv, srcs, lands, _chip_plan, [last])
        for n, own, got in zip(names, owns, landed):
            out[n] = _adamw(given[n][0], given["m_" + n][0], given["v_" + n][0], [own, got], "adamw_" + n)

    dmod_all = gathered["last"][:, :rows_of["b_ada"]].reshape(N_DEV, 6 * D)
    dmod_cols = lax.dynamic_slice_in_dim(dmod_all, me * ada_cols, ada_cols, axis=1)
    out["w_ada"] = _adamw(w_ada[0], m_w_ada[0], v_w_ada[0], [_ada_grad(c_all, dmod_cols)], "adamw_w_ada")

    def rows_form(a):
        return a.reshape(1, -1) if a.size // a.shape[-1] == 1 or a.ndim == 1 else a.reshape(-1, LANES)

    for stage, names in (("last", LAST_REP), ("rnn", EARLY_REP), ("mixer", MID_REP)):
        out.update(_adamw_group(names, *[[rows_form(given[pre + n]) for n in names] for pre in ("", "m_", "v_")],
                                gathered[stage], "adamw_small_" + stage))

    row0 = sum(rows_of[n] for n in EARLY_REP)
    for n in COL_SHARDED:
        full = gathered["rnn"][:, row0:row0 + rows_of[n]].reshape(N_DEV, small[n].shape[0], small[n].shape[1])
        mine = lax.dynamic_slice_in_dim(full, me * conv_cols[n], conv_cols[n], axis=2)
        out[n] = _adamw(given[n][0], given["m_" + n][0], given["v_" + n][0], [mine], "adamw_" + n)
        row0 += rows_of[n]

    total = lax.psum(loss[0, 0], ("x", "y", "c"))
    results = [total, grad_x[None]]
    for kind in range(4):
        results += [out[n][kind].reshape(given[n].shape) for n in WEIGHTS]
    return tuple(results)
```

```python
import math

import jax
import jax.numpy as jnp
from jax import lax
from jax.experimental import pallas as pl
from jax.experimental.pallas import tpu as pltpu

F32 = jnp.float32
BF16 = jnp.bfloat16
MESH_IDS = pl.DeviceIdType.MESH

D = 1024
NH = 8
HD = 128
NCOL_IN = 6 * D
DFF = 3 * D
N_DEV = 8
EPS = 1e-6
LRU_C = 8.0
ADAM_LR, ADAM_B1, ADAM_B2, ADAM_EPS, ADAM_WD, ADAM_STEP = 0.001, 0.9, 0.999, 1e-08, 0.01, 10

SUBLANES = 8
LANES = 128
HALO = 16
VMEM_LIMIT = 56 * 1024 * 1024
GELU_K = math.sqrt(2.0 / math.pi)
GELU_C = 0.044715


def _cparams(n_axes):
    return pltpu.CompilerParams(dimension_semantics=("arbitrary",) * n_axes, vmem_limit_bytes=VMEM_LIMIT)


def _const_spec(shape, single_buffer=False):
    nd = len(shape)
    if single_buffer:
        return pl.BlockSpec(shape, lambda *_: (0,) * nd, pipeline_mode=pl.Buffered(1))
    return pl.BlockSpec(shape, lambda *_: (0,) * nd)


def _vec_operand(v):
    if isinstance(v, tuple):
        stack, k = v
        return stack, pl.BlockSpec((None, 1, D), lambda *_: (k, 0, 0))
    return v, _const_spec((1, D))


def _tile_big(t):
    return min(512, t)


def _tile_seq(t):
    return min(256, t)


def _row_tile(rows, cols):
    cap = max(SUBLANES, (2 * 1024 * 1024) // (4 * cols) // SUBLANES * SUBLANES)
    if rows <= cap:
        return rows
    return next(tr for tr in range(cap, 0, -SUBLANES) if rows % tr == 0)


def _gelu_t(x):
    x2 = x * x
    t = jnp.tanh(x * (GELU_K + (GELU_K * GELU_C) * x2))
    hx = 0.5 * x
    return hx + hx * t, (x2, hx, t)


def _gelu_grad(shared):
    x2, hx, t = shared
    return (0.5 + 0.5 * t) + (hx * (1.0 - t * t)) * (GELU_K + (3.0 * GELU_K * GELU_C) * x2)


def _sigmoid(x):
    return 1.0 / (1.0 + jnp.exp(-x))


def _log_sigmoid(x):
    return -(jnp.maximum(-x, 0.0) + jnp.log1p(jnp.exp(-jnp.abs(x))))


def _row_iota(cols):
    return lax.broadcasted_iota(jnp.int32, (SUBLANES, cols), 0)


def _shift_down(x, k, prev8):
    if k == 0:
        return x
    r = pltpu.roll(x, k, 0)
    p = pltpu.roll(prev8, k, 0)
    head = jnp.where(_row_iota(x.shape[1]) < k, p, r[:SUBLANES])
    return jnp.concatenate([head, r[SUBLANES:]], axis=0)


def _shift_up(x, k, next8):
    if k == 0:
        return x
    n = x.shape[0]
    r = pltpu.roll(x, n - k, 0)
    q = pltpu.roll(next8, SUBLANES - k, 0)
    tail = jnp.where(_row_iota(x.shape[1]) >= SUBLANES - k, q, r[n - SUBLANES:])
    return jnp.concatenate([r[:n - SUBLANES], tail], axis=0)


def _heads_nn(x_bf, w_ref):
    return jnp.concatenate(
        [jnp.dot(x_bf[:, h * HD:(h + 1) * HD], w_ref[h], preferred_element_type=F32) for h in range(NH)], axis=1)


def _heads_nt(x_bf, w_ref):
    return jnp.concatenate(
        [lax.dot_general(x_bf[:, h * HD:(h + 1) * HD], w_ref[h], (((1,), (1,)), ((), ())), preferred_element_type=F32)
         for h in range(NH)], axis=1)


def _dot_nt(a, b):
    return lax.dot_general(a, b, (((1,), (1,)), ((), ())), preferred_element_type=F32)


def _dot_tn(a, b):
    return lax.dot_general(a, b, (((0,), (0,)), ((), ())), preferred_element_type=F32)


def _colsum(x):
    return jnp.sum(x, axis=0, keepdims=True)


def _prev_halo_map(tm, col):
    return lambda i, *_: (jnp.maximum(i * (tm // HALO) - 1, 0), col)


def _norm_proj(x, g, scale, shift, w, name):
    t, n = x.shape[0], w.shape[1]
    tm = _tile_big(t)

    def body(x_ref, g_ref, sc_ref, sh_ref, w_ref, h_ref, z_ref):
        xv = x_ref[...]
        r = lax.rsqrt(jnp.mean(xv * xv, axis=-1, keepdims=True) + EPS)
        hb = ((xv * r * g_ref[...]) * (1.0 + sc_ref[...]) + sh_ref[...]).astype(BF16)
        h_ref[...] = hb
        for c0 in range(0, n, D):
            z_ref[:, c0:c0 + D] = jnp.dot(hb, w_ref[:, c0:c0 + D], preferred_element_type=F32).astype(BF16)

    vec = _const_spec((1, D))
    (scale, sc_spec), (shift, sh_spec) = _vec_operand(scale), _vec_operand(shift)
    return pl.pallas_call(
        body, name=name, grid=(t // tm,),
        in_specs=[pl.BlockSpec((tm, D), lambda i: (i, 0)), vec, sc_spec, sh_spec, _const_spec((D, n), True)],
        out_specs=[pl.BlockSpec((tm, D), lambda i: (i, 0)), pl.BlockSpec((tm, n), lambda i: (i, 0))],
        out_shape=[jax.ShapeDtypeStruct((t, D), BF16), jax.ShapeDtypeStruct((t, n), BF16)],
        compiler_params=_cparams(1),
    )(x, g, scale, shift, w)


def _lru_gates(xc, wa_ref, ba, wx_ref, bx, ls):
    xb = xc.astype(BF16)
    ra = _sigmoid(_heads_nn(xb, wa_ref) + ba)
    ia = _sigmoid(_heads_nn(xb, wx_ref) + bx)
    la = LRU_C * ra * ls
    a = jnp.exp(la)
    mult = jnp.sqrt(-jnp.tanh(la) * (1.0 + a * a))
    return ra, ia, a, mult


def _conv4(xr, prev8, cw_ref, cb):
    return (cb + cw_ref[3:4, :] * xr + cw_ref[2:3, :] * _shift_down(xr, 1, prev8)
            + cw_ref[1:2, :] * _shift_down(xr, 2, prev8) + cw_ref[0:1, :] * _shift_down(xr, 3, prev8))


def _rnn_fwd(z, cw, cb, wa, ba, wx, bx, lam):
    t = z.shape[0]
    tm = _tile_seq(t)
    ngrp = tm // SUBLANES

    def body(xr_ref, xp_ref, gr_ref, cw_ref, cb_ref, wa_ref, ba_ref, wx_ref, bx_ref, lam_ref,
             h_ref, ya_ref, xc_ref, ra_ref, ia_ref, gg_ref, hg_ref, carry_ref, a_scr, u_scr):
        i = pl.program_id(0)

        @pl.when(i == 0)
        def _():
            carry_ref[...] = jnp.zeros_like(carry_ref)

        xr = xr_ref[...].astype(F32)
        prev8 = jnp.where(i == 0, 0.0, xp_ref[...].astype(F32)[HALO - SUBLANES:])
        xc = _conv4(xr, prev8, cw_ref, cb_ref[...])
        ra, ia, a, mult = _lru_gates(xc, wa_ref, ba_ref[...], wx_ref, bx_ref[...], _log_sigmoid(lam_ref[...]))
        xc_ref[...] = xc.astype(BF16)
        ra_ref[...] = ra.astype(BF16)
        ia_ref[...] = ia.astype(BF16)
        a_scr[...] = a
        u_scr[...] = mult * (ia * xc)
        row = _row_iota(D)

        def grp(j, carry):
            r0 = pl.multiple_of(j * SUBLANES, SUBLANES)
            av = a_scr[pl.ds(r0, SUBLANES), :]
            uv = u_scr[pl.ds(r0, SUBLANES), :]
            for d in (1, 2, 4):
                m = row >= d
                uv = jnp.where(m, av * pltpu.roll(uv, d, 0) + uv, uv)
                av = jnp.where(m, av * pltpu.roll(av, d, 0), av)
            hv = uv + av * carry
            h_ref[pl.ds(r0, SUBLANES), :] = hv
            return hv[SUBLANES - 1:SUBLANES, :]

        carry_ref[0:1, :] = lax.fori_loop(0, ngrp, grp, carry_ref[0:1, :])
        grv = gr_ref[...].astype(F32)
        gg, tg = _gelu_t(grv)
        hv = h_ref[...]
        ya_ref[...] = (hv * gg).astype(BF16)
        gg_ref[...] = gg.astype(BF16)
        hg_ref[...] = (hv * _gelu_grad(tg)).astype(BF16)

    vec = _const_spec((1, D))
    wspec = _const_spec((NH, HD, HD))
    tile = pl.BlockSpec((tm, D), lambda i: (i, 0))
    bshape = jax.ShapeDtypeStruct((t, D), BF16)
    return pl.pallas_call(
        body, name="rnn_fwd", grid=(t // tm,),
        in_specs=[tile, pl.BlockSpec((HALO, D), _prev_halo_map(tm, 0)),
                  pl.BlockSpec((tm, D), lambda i: (i, 1)), _const_spec((4, D)), vec, wspec, vec, wspec, vec, vec],
        out_specs=[tile] * 7,
        out_shape=[jax.ShapeDtypeStruct((t, D), F32)] + [bshape] * 6,
        scratch_shapes=[pltpu.VMEM((SUBLANES, D), F32), pltpu.VMEM((tm, D), F32), pltpu.VMEM((tm, D), F32)],
        compiler_params=_cparams(1),
    )(z, z, z, cw, cb, wa, ba, wx, bx, lam)


def _sgu_fwd(z, lng, lnb, wm, bst):
    t = z.shape[0]
    tm = _tile_seq(t)

    def body(zu_ref, zv_ref, lng_ref, lnb_ref, wm_ref, bst_ref, yb_ref, gu_ref, mg_ref, vh_ref, gpv_ref, rstd_ref):
        gu, su = _gelu_t(zu_ref[...].astype(F32))
        gv, sv = _gelu_t(zv_ref[...].astype(F32))
        mu = jnp.mean(gv, axis=-1, keepdims=True)
        cen = gv - mu
        rstd = lax.rsqrt(jnp.mean(cen * cen, axis=-1, keepdims=True) + EPS)
        vhat = cen * rstd
        vb = (vhat * lng_ref[...] + lnb_ref[...]).astype(BF16)
        rows = []
        for b0 in range(0, tm, HD):
            rows.append(jnp.concatenate(
                [jnp.dot(wm_ref[g], vb[b0:b0 + HD, g * HD:(g + 1) * HD], preferred_element_type=F32)
                 + bst_ref[:, g:g + 1] for g in range(NH)], axis=1))
        mixed = jnp.concatenate(rows, axis=0) if len(rows) > 1 else rows[0]
        yb_ref[...] = (gu * mixed).astype(BF16)
        gu_ref[...] = gu.astype(BF16)
        mg_ref[...] = (mixed * _gelu_grad(su)).astype(BF16)
        vh_ref[...] = vhat.astype(BF16)
        gpv_ref[...] = _gelu_grad(sv).astype(BF16)
        rstd_ref[...] = rstd

    vec = _const_spec((1, D))
    tile = pl.BlockSpec((tm, D), lambda i: (i, 0))
    bshape = jax.ShapeDtypeStruct((t, D), BF16)
    return pl.pallas_call(
        body, name="sgu_fwd", grid=(t // tm,),
        in_specs=[pl.BlockSpec((tm, D), lambda i: (i, 2)), pl.BlockSpec((tm, D), lambda i: (i, 3)), vec, vec,
                  _const_spec((NH, HD, HD)), _const_spec((HD, NH))],
        out_specs=[tile] * 5 + [pl.BlockSpec((tm, 1), lambda i: (i, 0))],
        out_shape=[bshape] * 5 + [jax.ShapeDtypeStruct((t, 1), F32)],
        compiler_params=_cparams(1),
    )(z, z, lng, lnb, wm, bst)


def _merge_fwd(ya_pre, yb_pre, z, x, gate1, wba, wbb, wout):
    t = x.shape[0]
    tm = _tile_big(t)

    def body(yap_ref, ybp_ref, ga_ref, gb_ref, x_ref, g1_ref, wba_ref, wbb_ref, wo_ref,
             x2_ref, ya_ref, yb_ref, mg_ref, o1_ref):
        ya = jnp.dot(yap_ref[...], wba_ref[...], preferred_element_type=F32)
        yb = jnp.dot(ybp_ref[...], wbb_ref[...], preferred_element_type=F32)
        merged = _sigmoid(ga_ref[...].astype(F32)) * ya + _sigmoid(gb_ref[...].astype(F32)) * yb
        mb = merged.astype(BF16)
        o1 = jnp.dot(mb, wo_ref[...], preferred_element_type=F32)
        x2_ref[...] = x_ref[...] + g1_ref[...] * o1
        ya_ref[...] = ya.astype(BF16)
        yb_ref[...] = yb.astype(BF16)
        mg_ref[...] = mb
        o1_ref[...] = o1.astype(BF16)

    tile = pl.BlockSpec((tm, D), lambda i: (i, 0))
    wspec = _const_spec((D, D))
    bshape = jax.ShapeDtypeStruct((t, D), BF16)
    gate1, g1_spec = _vec_operand(gate1)
    return pl.pallas_call(
        body, name="merge_fwd", grid=(t // tm,),
        in_specs=[tile, tile, pl.BlockSpec((tm, D), lambda i: (i, 4)), pl.BlockSpec((tm, D), lambda i: (i, 5)),
                  tile, g1_spec, wspec, wspec, wspec],
        out_specs=[tile] * 5,
        out_shape=[jax.ShapeDtypeStruct((t, D), F32), bshape, bshape, bshape, bshape],
        compiler_params=_cparams(1),
    )(ya_pre, yb_pre, z, z, x, gate1, wba, wbb, wout)


def _conv3(u, prev8, cw_ref, cb):
    return cb + cw_ref[2:3, :] * u + cw_ref[1:2, :] * _shift_down(u, 1, prev8) + cw_ref[0:1, :] * _shift_down(u, 2, prev8)


def _ffn_proj_mid(x2, g, scale, shift, w, cw, cb):
    t = x2.shape[0]
    tm = _tile_big(t)
    nc = DFF // D

    def body(x_ref, g_ref, sc_ref, sh_ref, wa_ref, wv_ref, cwa_ref, cwv_ref, cba_ref, cbv_ref,
             h_ref, upa_ref, upv_ref, ff_ref, fa_ref, fv_ref, hb_scr, prev_ref):
        i, c = pl.program_id(0), pl.program_id(1)

        @pl.when(i == 0)
        def _():
            prev_ref[c] = jnp.zeros((2, SUBLANES, D), F32)

        @pl.when(c == 0)
        def _():
            xv = x_ref[...]
            r = lax.rsqrt(jnp.mean(xv * xv, axis=-1, keepdims=True) + EPS)
            hb_scr[...] = ((xv * r * g_ref[...]) * (1.0 + sc_ref[...]) + sh_ref[...]).astype(BF16)
            h_ref[...] = hb_scr[...]

        hb = hb_scr[...]
        halves = []
        for s, (w_ref, up_ref, cw_ref, cb_ref) in enumerate(((wa_ref, upa_ref, cwa_ref, cba_ref),
                                                             (wv_ref, upv_ref, cwv_ref, cbv_ref))):
            u = jnp.dot(hb, w_ref[...], preferred_element_type=F32)
            up_ref[...] = u.astype(BF16)
            halves.append(_conv3(u, prev_ref[c, s], cw_ref, cb_ref[...]))
            prev_ref[c, s] = u[tm - SUBLANES:]
        act, val = halves
        ga, ta = _gelu_t(act)
        ff_ref[...] = (ga * val).astype(BF16)
        fa_ref[...] = (val * _gelu_grad(ta)).astype(BF16)
        fv_ref[...] = ga.astype(BF16)

    def cols(rows, off):
        return pl.BlockSpec((rows, D), lambda i, c: (0, off + c))

    vec = pl.BlockSpec((1, D), lambda i, c: (0, 0))
    row_tile = pl.BlockSpec((tm, D), lambda i, c: (i, 0))
    chunk = pl.BlockSpec((tm, D), lambda i, c: (i, c))
    hshape = jax.ShapeDtypeStruct((t, DFF), BF16)
    (scale, sc_spec), (shift, sh_spec) = _vec_operand(scale), _vec_operand(shift)
    return pl.pallas_call(
        body, name="ffn_proj_mid", grid=(t // tm, nc),
        in_specs=[row_tile, vec, sc_spec, sh_spec, cols(D, 0), cols(D, nc), cols(3, 0), cols(3, nc), cols(1, 0), cols(1, nc)],
        out_specs=[row_tile, chunk, chunk, chunk, chunk, chunk],
        out_shape=[jax.ShapeDtypeStruct((t, D), BF16), hshape, hshape, hshape, hshape, hshape],
        scratch_shapes=[pltpu.VMEM((tm, D), BF16), pltpu.VMEM((nc, 2, SUBLANES, D), F32)],
        compiler_params=_cparams(2),
    )(x2, g, scale, shift, w, w, cw, cw, cb, cb)


def _ffn_out_loss(ff, wd, x2, target, gate2, gfin):
    t = x2.shape[0]
    tm = _tile_big(t)

    def body(ff_ref, wd_ref, x2_ref, tg_ref, g2_ref, gf_ref, dx3_ref, loss_ref, dgf_ref, dg2_ref):
        @pl.when(pl.program_id(0) == 0)
        def _():
            loss_ref[...] = jnp.zeros_like(loss_ref)
            dgf_ref[...] = jnp.zeros_like(dgf_ref)
            dg2_ref[...] = jnp.zeros_like(dg2_ref)

        o2 = jnp.dot(ff_ref[...], wd_ref[...], preferred_element_type=F32)
        x3 = x2_ref[...] + g2_ref[...] * o2
        r = lax.rsqrt(jnp.mean(x3 * x3, axis=-1, keepdims=True) + EPS)
        xhat = x3 * r
        err = xhat * gf_ref[...] - tg_ref[...]
        loss_ref[...] += 0.5 * jnp.sum(jnp.mean(err * err, axis=-1, keepdims=True), axis=0, keepdims=True)
        dy = err * (1.0 / D)
        dgf_ref[...] += _colsum(dy * xhat)
        dxh = dy * gf_ref[...]
        dx3 = r * (dxh - xhat * jnp.mean(dxh * xhat, axis=-1, keepdims=True))
        dx3_ref[...] = dx3
        dg2_ref[...] += _colsum(dx3 * o2)

    tile = pl.BlockSpec((tm, D), lambda i: (i, 0))
    vec = _const_spec((1, D))
    gate2, g2_spec = _vec_operand(gate2)
    return pl.pallas_call(
        body, name="ffn_out_loss", grid=(t // tm,),
        in_specs=[pl.BlockSpec((tm, DFF), lambda i: (i, 0)), _const_spec((DFF, D), True), tile, tile, g2_spec, vec],
        out_specs=[tile, _const_spec((1, 1)), vec, vec],
        out_shape=[jax.ShapeDtypeStruct((t, D), F32), jax.ShapeDtypeStruct((1, 1), F32),
                   jax.ShapeDtypeStruct((1, D), F32), jax.ShapeDtypeStruct((1, D), F32)],
        compiler_params=_cparams(1),
    )(ff, wd, x2, target, gate2, gfin)


def _ffn_down_bwd(dx3, gate2, ff, fa, fv, wd):
    t = dx3.shape[0]
    tm = min(1024, t)
    nc = DFF // D

    def body(dx3_ref, g2_ref, ff_ref, fa_ref, fv_ref, wd_ref, da_ref, dv_ref, dwd_ref, dcba_ref, dcbv_ref):
        @pl.when(pl.program_id(1) == 0)
        def _():
            for r in (dwd_ref, dcba_ref, dcbv_ref):
                r[...] = jnp.zeros_like(r)

        do2 = (dx3_ref[...] * g2_ref[...]).astype(BF16)
        dwd_ref[...] += _dot_tn(ff_ref[...], do2)
        dff = _dot_nt(do2, wd_ref[...])
        dact = dff * fa_ref[...].astype(F32)
        dval = dff * fv_ref[...].astype(F32)
        da_ref[...] = dact.astype(BF16)
        dv_ref[...] = dval.astype(BF16)
        dcba_ref[...] += _colsum(dact)
        dcbv_ref[...] += _colsum(dval)

    blk = pl.BlockSpec((tm, D), lambda c, i: (i, c))
    vec = pl.BlockSpec((1, D), lambda c, i: (0, c))
    gate2, g2_spec = _vec_operand(gate2)
    return pl.pallas_call(
        body, name="ffn_down_bwd", grid=(nc, t // tm),
        in_specs=[pl.BlockSpec((tm, D), lambda c, i: (i, 0)), g2_spec,
                  blk, blk, blk, pl.BlockSpec((D, D), lambda c, i: (c, 0))],
        out_specs=[blk, blk, pl.BlockSpec((D, D), lambda c, i: (c, 0)), vec, vec],
        out_shape=[jax.ShapeDtypeStruct((t, DFF), BF16), jax.ShapeDtypeStruct((t, DFF), BF16),
                   jax.ShapeDtypeStruct((DFF, D), F32),
                   jax.ShapeDtypeStruct((1, DFF), F32), jax.ShapeDtypeStruct((1, DFF), F32)],
        compiler_params=_cparams(2),
    )(dx3, gate2, ff, fa, fv, wd)


def _modnorm_bwd(dh, xv, g, scale):
    r = lax.rsqrt(jnp.mean(xv * xv, axis=-1, keepdims=True) + EPS)
    xhat = xv * r
    dxn = dh * (1.0 + scale)
    dxh = dxn * g
    dx = r * (dxh - xhat * jnp.mean(dxh * xhat, axis=-1, keepdims=True))
    return dx, _colsum(dh), _colsum(dh * (xhat * g)), _colsum(dxn * xhat)


def _ffn_up_bwd(dact, dval, up_a, up_v, cw, wup, x2, dx3, gffn, scale2, o1, gate1):
    t = x2.shape[0]
    tm = _tile_seq(t)
    nt = t // tm
    nc = DFF // D

    def body(da_ref, dan_ref, dv_ref, dvn_ref, ua_ref, uv_ref, cw_ref, w_ref, x2_ref, dx3_ref, g_ref, sc_ref, o1_ref, g1_ref,
             dup_ref, dx2_ref, do1_ref, dcw_ref, dsh_ref, dsc_ref, dg_ref, dg1_ref):
        i = pl.program_id(0)

        @pl.when(i == 0)
        def _():
            for r in (dcw_ref, dsh_ref, dsc_ref, dg_ref, dg1_ref):
                r[...] = jnp.zeros_like(r)

        last = i == nt - 1
        dh = jnp.zeros((tm, D), F32)
        for half, (d_ref, dn_ref, u_ref) in enumerate(((da_ref, dan_ref, ua_ref), (dv_ref, dvn_ref, uv_ref))):
            nxt = jnp.where(last, 0.0, dn_ref[...].astype(F32)[:SUBLANES])
            for c in range(nc):
                c0 = half * DFF + c * D
                dv = d_ref[:, c * D:(c + 1) * D].astype(F32)
                nx = nxt[:, c * D:(c + 1) * D]
                taps = (_shift_up(dv, 2, nx), _shift_up(dv, 1, nx), dv)
                dup = (cw_ref[2:3, c0:c0 + D] * taps[2] + cw_ref[1:2, c0:c0 + D] * taps[1]
                       + cw_ref[0:1, c0:c0 + D] * taps[0]).astype(BF16)
                upv = u_ref[:, c * D:(c + 1) * D].astype(F32)
                for k in range(3):
                    dcw_ref[k:k + 1, c0:c0 + D] += _colsum(taps[k] * upv)
                dup_ref[:, c0:c0 + D] = dup
                dh = dh + _dot_nt(dup, w_ref[:, c0:c0 + D])
        dxn, dsh, dsc, dg = _modnorm_bwd(dh, x2_ref[...], g_ref[...], sc_ref[...])
        dx2 = dx3_ref[...] + dxn
        dx2_ref[...] = dx2
        do1_ref[...] = (dx2 * g1_ref[...]).astype(BF16)
        dsh_ref[...] += dsh
        dsc_ref[...] += dsc
        dg_ref[...] += dg
        dg1_ref[...] += _colsum(dx2 * o1_ref[...].astype(F32))

    tile = pl.BlockSpec((tm, D), lambda i: (i, 0))
    wide = pl.BlockSpec((tm, DFF), lambda i: (i, 0))
    nxt = pl.BlockSpec((HALO, DFF), lambda i: (jnp.minimum((i + 1) * (tm // HALO), t // HALO - 1), 0))
    vec = _const_spec((1, D))
    vshape = jax.ShapeDtypeStruct((1, D), F32)
    (scale2, sc_spec), (gate1, g1_spec) = _vec_operand(scale2), _vec_operand(gate1)
    return pl.pallas_call(
        body, name="ffn_up_bwd", grid=(nt,),
        in_specs=[wide, nxt, wide, nxt, wide, wide,
                  _const_spec((3, 2 * DFF)), _const_spec((D, 2 * DFF), True),
                  tile, tile, vec, sc_spec, tile, g1_spec],
        out_specs=[pl.BlockSpec((tm, 2 * DFF), lambda i: (i, 0)), tile, tile, _const_spec((3, 2 * DFF)),
                   vec, vec, vec, vec],
        out_shape=[jax.ShapeDtypeStruct((t, 2 * DFF), BF16), jax.ShapeDtypeStruct((t, D), F32),
                   jax.ShapeDtypeStruct((t, D), BF16), jax.ShapeDtypeStruct((3, 2 * DFF), F32),
                   vshape, vshape, vshape, vshape],
        compiler_params=_cparams(1),
    )(dact, dact, dval, dval, up_a, up_v, cw, wup, x2, dx3, gffn, scale2, o1, gate1)


def _xt_y(a, b, name):
    t, k = a.shape
    n = b.shape[1]
    tm = min(1024, t)
    bn = 1536 if n % 1536 == 0 else D

    def body(a_ref, b_ref, o_ref):
        @pl.when(pl.program_id(1) == 0)
        def _():
            o_ref[...] = jnp.zeros_like(o_ref)

        o_ref[...] += _dot_tn(a_ref[...], b_ref[...])

    return pl.pallas_call(
        body, name=name, grid=(n // bn, t // tm),
        in_specs=[pl.BlockSpec((tm, k), lambda j, i: (i, 0)), pl.BlockSpec((tm, bn), lambda j, i: (i, j))],
        out_specs=pl.BlockSpec((k, bn), lambda j, i: (0, j)),
        out_shape=jax.ShapeDtypeStruct((k, n), F32),
        compiler_params=_cparams(2),
    )(a, b)


def _acc_spec(shape, index):
    return pl.BlockSpec(shape, lambda *_: index, pipeline_mode=pl.Buffered(1))


def _out_bwd(do1, wout, merged, ya, yb, z, h1):
    t = do1.shape[0]
    tm = _tile_big(t)

    def body(do1_ref, wo_ref, mg_ref, ya_ref, yb_ref, ga_ref, gb_ref, h1_ref,
             dya_ref, dyb_ref, dz_ref, dwo_ref, dwin_ref):
        @pl.when(pl.program_id(0) == 0)
        def _():
            dwo_ref[...] = jnp.zeros_like(dwo_ref)
            dwin_ref[...] = jnp.zeros_like(dwin_ref)

        do1v = do1_ref[...]
        dwo_ref[...] += _dot_tn(mg_ref[...], do1v)
        dm = _dot_nt(do1v, wo_ref[...])
        sa = _sigmoid(ga_ref[...].astype(F32))
        sb = _sigmoid(gb_ref[...].astype(F32))
        dya_ref[...] = (dm * sa).astype(BF16)
        dyb_ref[...] = (dm * sb).astype(BF16)
        dga = (dm * ya_ref[...].astype(F32) * sa * (1.0 - sa)).astype(BF16)
        dgb = (dm * yb_ref[...].astype(F32) * sb * (1.0 - sb)).astype(BF16)
        dz_ref[:, 0:D] = dga
        dz_ref[:, D:2 * D] = dgb
        h1v = h1_ref[...]
        dwin_ref[:, 0:D] += _dot_tn(h1v, dga)
        dwin_ref[:, D:2 * D] += _dot_tn(h1v, dgb)

    tile = pl.BlockSpec((tm, D), lambda i: (i, 0))
    bshape = jax.ShapeDtypeStruct((t, D), BF16)
    return pl.pallas_call(
        body, name="out_bwd", grid=(t // tm,),
        in_specs=[tile, _const_spec((D, D), True), tile, tile, tile,
                  pl.BlockSpec((tm, D), lambda i: (i, 4)), pl.BlockSpec((tm, D), lambda i: (i, 5)), tile],
        out_specs=[tile, tile, pl.BlockSpec((tm, 2 * D), lambda i: (i, 2)), _acc_spec((D, D), (0, 0)),
                   _acc_spec((D, 2 * D), (0, 2))],
        out_shape=[bshape, bshape, jax.ShapeDtypeStruct((t, NCOL_IN), BF16), jax.ShapeDtypeStruct((D, D), F32),
                   jax.ShapeDtypeStruct((D, NCOL_IN), F32)],
        compiler_params=_cparams(1),
    )(do1, wout, merged, ya, yb, z, z, h1)


def _rnn_bwd(dya, ya_pre, wba, h1, z, saved, h, dz, dwin, cw, wa, wx, lam):
    t = z.shape[0]
    tm = _tile_seq(t)
    nt = t // tm
    ngrp = tm // SUBLANES
    hpt = tm // HALO

    def body(dya_ref, yap_ref, wba_ref, h1_ref, xr_ref, xc_ref, ra_ref, ia_ref, gg_ref, hg_ref, h_ref, hp_ref,
             dz_any, dwin_any, cw_ref, wa_ref, wx_ref, lam_ref,
             dz_ref, dwin_ref, dwba_ref, dcw_ref, dcb_ref, dwa_ref, dba_ref, dwx_ref, dbx_ref, dlam_ref,
             a_first, g_first, dxc_first, b_scr, d_scr, g_scr):
        del dz_any, dwin_any
        i = pl.program_id(0)

        @pl.when(i == 0)
        def _():
            for r in (dwin_ref, dwba_ref, dcw_ref, dcb_ref, dwa_ref, dba_ref, dwx_ref, dbx_ref, dlam_ref,
                      a_first, g_first, dxc_first):
                r[...] = jnp.zeros_like(r)

        dya_v = dya_ref[...]
        dwba_ref[...] += _dot_tn(yap_ref[...], dya_v)
        dyap_v = _dot_nt(dya_v, wba_ref[...])
        h1v = h1_ref[...]

        first_tile = i == nt - 1
        xc = xc_ref[...].astype(F32)
        ra = ra_ref[...].astype(F32)
        ia = ia_ref[...].astype(F32)
        lam_v = lam_ref[...]
        ls = _log_sigmoid(lam_v)
        la = LRU_C * ra * ls
        a = jnp.exp(la)
        mult = jnp.sqrt(-jnp.tanh(la) * (1.0 + a * a))
        hprev8 = jnp.where(first_tile, 0.0, hp_ref[...][HALO - SUBLANES:])
        h_prev = _shift_down(h_ref[...], 1, hprev8)
        dgr = (dyap_v * hg_ref[...].astype(F32)).astype(BF16)
        dz_ref[:, D:2 * D] = dgr
        dwin_ref[:, D:2 * D] += _dot_tn(h1v, dgr)

        b_scr[...] = _shift_up(a, 1, a_first[...])
        d_scr[...] = dyap_v * gg_ref[...].astype(F32)
        row = _row_iota(D)

        def grp(jj, carry):
            r0 = pl.multiple_of((ngrp - 1 - jj) * SUBLANES, SUBLANES)
            bv = b_scr[pl.ds(r0, SUBLANES), :]
            dv = d_scr[pl.ds(r0, SUBLANES), :]
            for d in (1, 2, 4):
                m = row < SUBLANES - d
                dv = jnp.where(m, dv + bv * pltpu.roll(dv, SUBLANES - d, 0), dv)
                bv = jnp.where(m, bv * pltpu.roll(bv, SUBLANES - d, 0), bv)
            gv = dv + bv * carry
            g_scr[pl.ds(r0, SUBLANES), :] = gv
            return gv[0:1, :]

        lax.fori_loop(0, ngrp, grp, g_first[0:1, :])
        g = g_scr[...]
        a_first[...] = a[:SUBLANES]
        g_first[...] = g[:SUBLANES]

        da = g * h_prev
        gx = g * xc
        dmult = gx * ia
        dia = gx * mult
        dxc = g * (mult * ia)
        dla = da * a - dmult * (a * a) / mult
        dra = dla * (LRU_C * ls)
        dlam_ref[...] += _colsum(dla * ra) * (LRU_C * _sigmoid(-lam_v))
        dpa = dra * ra * (1.0 - ra)
        dpx = dia * ia * (1.0 - ia)
        dba_ref[...] += _colsum(dpa)
        dbx_ref[...] += _colsum(dpx)
        dpab = dpa.astype(BF16)
        dpxb = dpx.astype(BF16)
        xcb = xc_ref[...]
        for hd in range(NH):
            sl = slice(hd * HD, (hd + 1) * HD)
            dwa_ref[hd] += _dot_tn(xcb[:, sl], dpab[:, sl])
            dwx_ref[hd] += _dot_tn(xcb[:, sl], dpxb[:, sl])
        dxc = dxc + _heads_nt(dpab, wa_ref) + _heads_nt(dpxb, wx_ref)

        nxt = dxc_first[...]
        taps = (_shift_up(dxc, 3, nxt), _shift_up(dxc, 2, nxt), _shift_up(dxc, 1, nxt), dxc)
        dxr = cw_ref[0:1, :] * taps[0]
        for k in range(1, 4):
            dxr = dxr + cw_ref[k:k + 1, :] * taps[k]
        dxrb = dxr.astype(BF16)
        dz_ref[:, 0:D] = dxrb
        dwin_ref[:, 0:D] += _dot_tn(h1v, dxrb)
        dxc_first[...] = dxc[:SUBLANES]
        dcb_ref[...] += _colsum(dxc)
        xr = xr_ref[...].astype(F32)
        for k in range(4):
            dcw_ref[k:k + 1, :] += _colsum(taps[k] * xr)

    def rev(col):
        return lambda i: (nt - 1 - i, col)

    vec = _const_spec((1, D))
    wspec = _const_spec((NH, HD, HD))
    vshape = jax.ShapeDtypeStruct((1, D), F32)
    wshape = jax.ShapeDtypeStruct((NH, HD, HD), F32)
    any_spec = pl.BlockSpec(memory_space=pl.ANY)
    tile = pl.BlockSpec((tm, D), rev(0))
    outs = pl.pallas_call(
        body, name="rnn_bwd", grid=(nt,),
        in_specs=[tile, tile, _const_spec((D, D), True), tile, tile, tile, tile, tile, tile, tile, tile,
                  pl.BlockSpec((HALO, D), lambda i: (jnp.maximum((nt - 1 - i) * hpt - 1, 0), 0)),
                  any_spec, any_spec, _const_spec((4, D)), wspec, wspec, vec],
        out_specs=[pl.BlockSpec((tm, 2 * D), rev(0)), _acc_spec((D, 2 * D), (0, 0)), _acc_spec((D, D), (0, 0)),
                   _const_spec((4, D)), vec, wspec, vec, wspec, vec, vec],
        out_shape=[jax.ShapeDtypeStruct((t, NCOL_IN), BF16), jax.ShapeDtypeStruct((D, NCOL_IN), F32),
                   jax.ShapeDtypeStruct((D, D), F32), jax.ShapeDtypeStruct((4, D), F32), vshape,
                   wshape, vshape, wshape, vshape, vshape],
        scratch_shapes=[pltpu.VMEM((SUBLANES, D), F32), pltpu.VMEM((SUBLANES, D), F32), pltpu.VMEM((SUBLANES, D), F32),
                        pltpu.VMEM((tm, D), F32), pltpu.VMEM((tm, D), F32), pltpu.VMEM((tm, D), F32)],
        input_output_aliases={12: 0, 13: 1},
        compiler_params=_cparams(1),
    )(dya, ya_pre, wba, h1, z, *saved, h, h, dz, dwin, cw, wa, wx, lam)
    return outs


def _sgu_bwd(dyb, yb_pre, wbb, h1, saved, dz, dwin, lng, lnb, wmt, mask):
    t = dyb.shape[0]
    tm = _tile_big(t)

    def body(dyb_ref, ybp_ref, wbb_ref, h1_ref, gu_ref, mg_ref, vh_ref, gpv_ref, rstd_ref, dz_any, dwin_any,
             lng_ref, lnb_ref, wmt_ref, mask_ref,
             dz_ref, dwin_ref, dwbb_ref, dws_ref, dbst_ref, dlng_ref, dlnb_ref):
        del dz_any, dwin_any

        @pl.when(pl.program_id(0) == 0)
        def _():
            for r in (dwin_ref, dwbb_ref, dws_ref, dbst_ref, dlng_ref, dlnb_ref):
                r[...] = jnp.zeros_like(r)

        lng_v = lng_ref[...]
        vhat = vh_ref[...].astype(F32)
        vb = (vhat * lng_v + lnb_ref[...]).astype(BF16)
        rstd = rstd_ref[...]
        dyb_v = dyb_ref[...]
        dwbb_ref[...] += _dot_tn(ybp_ref[...], dyb_v)
        dyb = _dot_nt(dyb_v, wbb_ref[...])
        h1v = h1_ref[...]
        dzu = (dyb * mg_ref[...].astype(F32)).astype(BF16)
        dz_ref[:, 0:D] = dzu
        dwin_ref[:, 0:D] += _dot_tn(h1v, dzu)
        dmix = dyb * gu_ref[...].astype(F32)
        dmb = dmix.astype(BF16)
        rows = []
        lane = lax.broadcasted_iota(jnp.int32, (HD, NH), 1)
        dbst = jnp.zeros((HD, NH), F32)
        for b0 in range(0, tm, HD):
            cols = []
            for g in range(NH):
                sl = slice(g * HD, (g + 1) * HD)
                dmg = dmb[b0:b0 + HD, sl]
                dws_ref[g] += _dot_nt(dmg, vb[b0:b0 + HD, sl]) * mask_ref[...]
                cols.append(jnp.dot(wmt_ref[g], dmg, preferred_element_type=F32))
                dbst = dbst + jnp.where(lane == g, jnp.sum(dmix[b0:b0 + HD, sl], axis=1, keepdims=True), 0.0)
            rows.append(jnp.concatenate(cols, axis=1))
        dbst_ref[...] += dbst
        dvln = jnp.concatenate(rows, axis=0) if len(rows) > 1 else rows[0]
        dlng_ref[...] += _colsum(dvln * vhat)
        dlnb_ref[...] += _colsum(dvln)
        dvh = dvln * lng_v
        dgv = rstd * (dvh - jnp.mean(dvh, axis=-1, keepdims=True)
                      - vhat * jnp.mean(dvh * vhat, axis=-1, keepdims=True))
        dzv = (dgv * gpv_ref[...].astype(F32)).astype(BF16)
        dz_ref[:, D:2 * D] = dzv
        dwin_ref[:, D:2 * D] += _dot_tn(h1v, dzv)

    vec = _const_spec((1, D))
    wspec = _const_spec((NH, HD, HD))
    vshape = jax.ShapeDtypeStruct((1, D), F32)
    tile = pl.BlockSpec((tm, D), lambda i: (i, 0))
    any_spec = pl.BlockSpec(memory_space=pl.ANY)
    return pl.pallas_call(
        body, name="sgu_bwd", grid=(t // tm,),
        in_specs=[tile, tile, _const_spec((D, D), True), tile, tile, tile, tile, tile,
                  pl.BlockSpec((tm, 1), lambda i: (i, 0)), any_spec, any_spec,
                  vec, vec, wspec, _const_spec((HD, HD))],
        out_specs=[pl.BlockSpec((tm, 2 * D), lambda i: (i, 1)), _acc_spec((D, 2 * D), (0, 1)), _acc_spec((D, D), (0, 0)),
                   wspec, _const_spec((HD, NH)), vec, vec],
        out_shape=[jax.ShapeDtypeStruct((t, NCOL_IN), BF16), jax.ShapeDtypeStruct((D, NCOL_IN), F32),
                   jax.ShapeDtypeStruct((D, D), F32), jax.ShapeDtypeStruct((NH, HD, HD), F32),
                   jax.ShapeDtypeStruct((HD, NH), F32), vshape, vshape],
        input_output_aliases={9: 0, 10: 1},
        compiler_params=_cparams(1),
    )(dyb, yb_pre, wbb, h1, *saved, dz, dwin, lng, lnb, wmt, mask)


def _in_bwd(dz, win, x, dx2, g, scale1):
    t = x.shape[0]
    tm = _tile_big(t)

    def body(dz_ref, w_ref, x_ref, dx2_ref, g_ref, sc_ref, dx_ref, dsh_ref, dsc_ref, dg_ref):
        @pl.when(pl.program_id(0) == 0)
        def _():
            for r in (dsh_ref, dsc_ref, dg_ref):
                r[...] = jnp.zeros_like(r)

        dh = jnp.zeros((tm, D), F32)
        for c0 in range(0, NCOL_IN, D):
            dh = dh + _dot_nt(dz_ref[:, c0:c0 + D], w_ref[:, c0:c0 + D])
        dxn, dsh, dsc, dg = _modnorm_bwd(dh, x_ref[...], g_ref[...], sc_ref[...])
        dx_ref[...] = dx2_ref[...] + dxn
        dsh_ref[...] += dsh
        dsc_ref[...] += dsc
        dg_ref[...] += dg

    tile = pl.BlockSpec((tm, D), lambda i: (i, 0))
    vec = _const_spec((1, D))
    vshape = jax.ShapeDtypeStruct((1, D), F32)
    scale1, sc_spec = _vec_operand(scale1)
    return pl.pallas_call(
        body, name="in_bwd", grid=(t // tm,),
        in_specs=[pl.BlockSpec((tm, NCOL_IN), lambda i: (i, 0)), _const_spec((D, NCOL_IN), True), tile, tile, vec,
                  sc_spec],
        out_specs=[tile, vec, vec, vec],
        out_shape=[jax.ShapeDtypeStruct((t, D), F32), vshape, vshape, vshape],
        compiler_params=_cparams(1),
    )(dz, win, x, dx2, g, scale1)


def _mod_cols(c_all, w_ada, b_cols):
    nb, cols = c_all.shape[0], w_ada.shape[1]

    def body(c_ref, w_ref, b_ref, o_ref):
        cv = c_ref[...]
        ca = (cv * _sigmoid(cv)).astype(BF16)
        o_ref[...] = jnp.dot(ca, w_ref[...].astype(BF16), preferred_element_type=F32) + b_ref[...]

    return pl.pallas_call(body, name="mod_cols", out_shape=jax.ShapeDtypeStruct((nb, cols), F32))(c_all, w_ada, b_cols)


def _ada_grad(c_all, dmod_cols):
    cols = dmod_cols.shape[1]

    def body(c_ref, d_ref, o_ref):
        cv = c_ref[...]
        ca = (cv * _sigmoid(cv)).astype(BF16)
        o_ref[...] = _dot_tn(ca, d_ref[...].astype(BF16))

    return pl.pallas_call(body, name="ada_grad", out_shape=jax.ShapeDtypeStruct((D, cols), F32))(c_all, dmod_cols)


def _adamw_update(w, m, v, g):
    bc1 = 1.0 - ADAM_B1 ** ADAM_STEP
    bc2 = 1.0 - ADAM_B2 ** ADAM_STEP
    mn = ADAM_B1 * m + (1.0 - ADAM_B1) * g
    vn = ADAM_B2 * v + (1.0 - ADAM_B2) * (g * g)
    return -ADAM_LR * ((mn / bc1) / (jnp.sqrt(vn / bc2) + ADAM_EPS) + ADAM_WD * w), mn, vn


def _adamw_group(names, ws, ms, vs, packs, name):
    n = len(names)
    starts, r0 = [], 0
    for w in ws:
        starts.append(r0)
        r0 += _pack_rows(w.shape)

    def body(*refs):
        w_refs, m_refs, v_refs, p_ref = refs[:n], refs[n:2 * n], refs[2 * n:3 * n], refs[3 * n]
        outs = refs[3 * n + 1:]
        for k in range(n):
            rows = _pack_rows(ws[k].shape)
            g = None
            for dev in range(N_DEV):
                if ws[k].shape[0] == 1:
                    term = jnp.concatenate(
                        [p_ref[dev, starts[k] + r:starts[k] + r + 1, :] for r in range(rows)], axis=1)
                else:
                    term = p_ref[dev, starts[k]:starts[k] + rows, :]
                g = term if g is None else g + term
            delta, mn, vn = _adamw_update(w_refs[k][...], m_refs[k][...], v_refs[k][...], g)
            for o_ref, val in zip(outs[4 * k:4 * k + 4], (g, delta, mn, vn)):
                o_ref[...] = val

    shapes = [jax.ShapeDtypeStruct(w.shape, F32) for w in ws for _ in range(4)]
    outs = pl.pallas_call(body, name=name, out_shape=shapes,
                          compiler_params=pltpu.CompilerParams(vmem_limit_bytes=VMEM_LIMIT))(*ws, *ms, *vs, packs)
    return {nm: tuple(outs[4 * k:4 * k + 4]) for k, nm in enumerate(names)}


def _adamw(w, m, v, parts, name):
    rows, cols = w.shape
    tr = _row_tile(rows, cols)
    stacked = [p.ndim == 3 for p in parts]

    def body(*refs):
        w_ref, m_ref, v_ref = refs[:3]
        p_refs = refs[3:3 + len(parts)]
        g_ref, d_ref, mo_ref, vo_ref = refs[3 + len(parts):]
        g = None
        for p_ref, st in zip(p_refs, stacked):
            terms = [p_ref[k].astype(F32) for k in range(p_ref.shape[0])] if st else [p_ref[...].astype(F32)]
            for term in terms:
                g = term if g is None else g + term
        delta, mn, vn = _adamw_update(w_ref[...], m_ref[...], v_ref[...], g)
        g_ref[...] = g
        mo_ref[...] = mn
        vo_ref[...] = vn
        d_ref[...] = delta

    tile = pl.BlockSpec((tr, cols), lambda i: (i, 0))
    p_specs = [pl.BlockSpec((p.shape[0], tr, cols), lambda i: (0, i, 0)) if st else tile for p, st in zip(parts, stacked)]
    shp = jax.ShapeDtypeStruct((rows, cols), F32)
    return pl.pallas_call(
        body, name=name, grid=(rows // tr,),
        in_specs=[tile, tile, tile] + p_specs, out_specs=[tile] * 4, out_shape=[shp] * 4,
        compiler_params=_cparams(1),
    )(w, m, v, *parts)


def _mesh_pos():
    return lax.axis_index("x"), lax.axis_index("y"), lax.axis_index("c")


def _other_chips(x, y):
    return [(1 - x, y), (x, 1 - y), (1 - x, 1 - y)]


def _block_of(ref, axis, index, size):
    if axis == 0:
        return ref.at[index]
    return ref.at[:, pl.ds(pl.multiple_of(index * size, 128), size)]


def _all_gather(shards, axes, name):
    n = len(shards)
    per = 7

    def body(*refs):
        ins, outs, done = refs[:n], refs[n:2 * n], refs[2 * n]
        send_sems, recv_sems, local_sems = refs[2 * n + 1:]
        x, y, c = _mesh_pos()
        me, sibling = (x, y, c), (x, y, 1 - c)
        chips = _other_chips(x, y)

        def rows(a, pos):
            return _block_of(outs[a], axes[a], 4 * pos[0] + 2 * pos[1] + pos[2], shards[a].shape[-1])

        def copy(a, k, block, to, src=None):
            return pltpu.make_async_remote_copy(
                src_ref=rows(a, block) if src is None else src, dst_ref=rows(a, block),
                send_sem=send_sems.at[a * per + k], recv_sem=recv_sems.at[a * per + k],
                device_id=to, device_id_type=MESH_IDS)

        mine = [pltpu.make_async_copy(ins[a], rows(a, me), local_sems.at[a]) for a in range(n)]
        for cp in mine:
            cp.start()
        first = []
        for a in range(n):
            first.append(copy(a, 0, me, sibling, src=ins[a]))
            first += [copy(a, 1 + j, me, (*chip, c), src=ins[a]) for j, chip in enumerate(chips)]
        for cp in first:
            cp.start()
        passed = []
        for j, chip in enumerate(chips):
            for a in range(n):
                copy(a, 1 + j, (*chip, c), me).wait_recv()
                fwd = copy(a, 4 + j, (*chip, c), sibling)
                fwd.start()
                passed.append(fwd)
        for a in range(n):
            copy(a, 0, sibling, me).wait_recv()
            for j, chip in enumerate(chips):
                copy(a, 4 + j, (*chip, 1 - c), me).wait_recv()
        for cp in first + passed:
            cp.wait_send()
        for cp in mine:
            cp.wait()
        done[...] = jnp.zeros_like(done)

    def full_shape(s, ax):
        return (N_DEV,) + s.shape if ax == 0 else s.shape[:-1] + (N_DEV * s.shape[-1],)

    any_spec = pl.BlockSpec(memory_space=pl.ANY)
    outs = pl.pallas_call(
        body, name=name,
        in_specs=[any_spec] * n, out_specs=[any_spec] * n + [pl.BlockSpec(memory_space=pltpu.VMEM)],
        out_shape=[jax.ShapeDtypeStruct(full_shape(s, ax), s.dtype) for s, ax in zip(shards, axes)]
        + [jax.ShapeDtypeStruct((SUBLANES, LANES), F32)],
        scratch_shapes=[pltpu.SemaphoreType.DMA((n * per,)), pltpu.SemaphoreType.DMA((n * per,)),
                        pltpu.SemaphoreType.DMA((n,))],
    )(*shards)
    return outs[:n], outs[n]


def _chip_blocks(x, y):
    return [(x, y)] + _other_chips(x, y)


def _sibling_reduce(gs, axis, name):
    g0, n = gs[0], len(gs)
    rows, cols = (g0.shape[1], g0.shape[2]) if axis == 0 else (g0.shape[0], g0.shape[1] // N_DEV)
    chunk = math.gcd(rows, 64)

    def body(*refs):
        g_refs, own_refs, pay_refs = refs[:n], refs[n:2 * n], refs[2 * n:3 * n]
        send_buf, keep_buf, recv_buf, pay_buf, send_sems, recv_sems, stage_sems, keep_sems, out_sems = refs[3 * n:]
        x, y, c = _mesh_pos()
        sibling = (x, y, 1 - c)
        chips = _chip_blocks(x, y)
        stage, keep, push = [], [], []
        for a in range(n):
            for j, (px, py) in enumerate(chips):
                s = 4 * a + j
                theirs = _block_of(g_refs[a], axis, 4 * px + 2 * py + (1 - c), cols)
                ours = _block_of(g_refs[a], axis, 4 * px + 2 * py + c, cols)
                stage.append(pltpu.make_async_copy(theirs, send_buf.at[s], stage_sems.at[s]))
                keep.append(pltpu.make_async_copy(ours, keep_buf.at[s], keep_sems.at[s]))
                push.append(pltpu.make_async_remote_copy(
                    src_ref=send_buf.at[s], dst_ref=recv_buf.at[s], send_sem=send_sems.at[s],
                    recv_sem=recv_sems.at[s], device_id=sibling, device_id_type=MESH_IDS))
        for cp in stage + keep:
            cp.start()
        for s in range(4 * n):
            stage[s].wait()
            push[s].start()
        written = []
        for s in range(4 * n):
            push[s].wait_recv()
            keep[s].wait()
            a, j = divmod(s, 4)
            res = keep_buf.at[s] if j == 0 else pay_buf.at[3 * a + j - 1]

            def add(r, carry, s=s, res=res):
                sl = pl.ds(pl.multiple_of(r * chunk, chunk), chunk)
                res[sl, :] = (keep_buf[s, sl, :] + recv_buf[s, sl, :]).astype(res.dtype)
                return carry

            lax.fori_loop(0, rows // chunk, add, 0)
            out = pltpu.make_async_copy(res, own_refs[a] if j == 0 else pay_refs[a].at[j - 1], out_sems.at[s])
            out.start()
            written.append(out)
        for cp in push:
            cp.wait_send()
        for cp in written:
            cp.wait()

    any_spec = pl.BlockSpec(memory_space=pl.ANY)
    buf = pltpu.VMEM((4 * n, rows, cols), F32)
    sems = pltpu.SemaphoreType.DMA((4 * n,))
    outs = pl.pallas_call(
        body, name=name,
        in_specs=[any_spec] * n, out_specs=[any_spec] * (2 * n),
        out_shape=[jax.ShapeDtypeStruct((rows, cols), F32)] * n + [jax.ShapeDtypeStruct((3, rows, cols), BF16)] * n,
        scratch_shapes=[buf, buf, buf, pltpu.VMEM((3 * n, rows, cols), BF16), sems, sems, sems, sems, sems],
        compiler_params=pltpu.CompilerParams(vmem_limit_bytes=VMEM_LIMIT),
    )(*gs)
    return list(zip(outs[:n], outs[n:]))


_HBM_SPEC = pl.BlockSpec(memory_space=pltpu.HBM)
_SEM_SPEC = pl.BlockSpec(memory_space=pltpu.SEMAPHORE)
_SIDE_EFFECT = pltpu.SideEffectType.DATAFLOW_SIDE_EFFECTING


def _exchange_start(name, srcs, lands, plan, n_copies):
    nb = len(srcs) + len(lands)

    def body(*refs):
        bufs, send_sems, recv_sems, token = refs[:nb], refs[nb], refs[nb + 1], refs[-1]
        for cp in plan(bufs[:len(srcs)], bufs[len(srcs):], send_sems, recv_sems):
            cp.start()
        token[...] = jnp.zeros_like(token)

    arrays = list(srcs) + list(lands)
    outs = pl.pallas_call(
        body, name=name,
        out_shape=(pltpu.SemaphoreType.DMA((n_copies,)), pltpu.SemaphoreType.DMA((n_copies,)),
                   *[pltpu.HBM(a.shape, a.dtype) for a in arrays], jax.ShapeDtypeStruct((SUBLANES, LANES), F32)),
        in_specs=[_HBM_SPEC] * nb,
        out_specs=(_SEM_SPEC, _SEM_SPEC, *[_HBM_SPEC] * nb, pl.BlockSpec(memory_space=pltpu.VMEM)),
        input_output_aliases={k: 2 + k for k in range(nb)},
        compiler_params=pltpu.CompilerParams(has_side_effects=_SIDE_EFFECT),
    )(*[pltpu.with_memory_space_constraint(a, pltpu.HBM) for a in arrays])
    return outs[0], outs[1], outs[2:2 + len(srcs)], outs[2 + len(srcs):2 + nb], outs[-1]


def _exchange_wait(name, send_sems, recv_sems, srcs, lands, plan, after):
    nb = len(srcs) + len(lands)
    after = list(after)

    def body(*refs):
        bufs, send_ref, recv_ref = refs[:nb], refs[nb], refs[nb + 1]
        for cp in plan(bufs[:len(srcs)], bufs[len(srcs):], send_ref, recv_ref):
            cp.wait_send()
            cp.wait_recv()

    arrays = list(srcs) + list(lands)
    outs = pl.pallas_call(
        body, name=name,
        out_shape=tuple(pltpu.HBM(a.shape, a.dtype) for a in arrays),
        in_specs=[_HBM_SPEC] * nb + [_SEM_SPEC, _SEM_SPEC] + [pl.BlockSpec(memory_space=pl.ANY)] * len(after),
        out_specs=tuple([_HBM_SPEC] * nb),
        input_output_aliases={k: k for k in range(nb)},
        compiler_params=pltpu.CompilerParams(has_side_effects=_SIDE_EFFECT),
    )(*arrays, send_sems, recv_sems, *after)
    return outs[len(srcs):]


def _gather_plan(axes, sizes):
    def plan(src_refs, land_refs, send_sems, recv_sems):
        x, y, c = _mesh_pos()
        copies = []
        for a, (src, land) in enumerate(zip(src_refs, land_refs)):
            mine = _block_of(land, axes[a], 4 * x + 2 * y + c, sizes[a])
            for k in range(1, N_DEV):
                peer = (1 - x if k & 4 else x, 1 - y if k & 2 else y, 1 - c if k & 1 else c)
                idx = a * (N_DEV - 1) + k - 1
                copies.append(pltpu.make_async_remote_copy(
                    src_ref=src, dst_ref=mine, send_sem=send_sems.at[idx], recv_sem=recv_sems.at[idx],
                    device_id=peer, device_id_type=MESH_IDS))
        return copies
    return plan


def _chip_plan(src_refs, land_refs, send_sems, recv_sems):
    x, y, c = _mesh_pos()
    copies = []
    for a, (src, land) in enumerate(zip(src_refs, land_refs)):
        for j, chip in enumerate(_other_chips(x, y)):
            copies.append(pltpu.make_async_remote_copy(
                src_ref=src.at[j], dst_ref=land.at[j], send_sem=send_sems.at[3 * a + j],
                recv_sem=recv_sems.at[3 * a + j], device_id=(*chip, c), device_id_type=MESH_IDS))
    return copies


def _own_block_placed(shard, axis, me):
    if axis == 0:
        full = lax.empty((N_DEV,) + shard.shape, shard.dtype)
        return lax.dynamic_update_slice(full, shard[None], (me,) + (0,) * shard.ndim)
    rows, cols = shard.shape

    def body(me_ref, s_ref, o_ref):
        del me_ref
        o_ref[...] = s_ref[...]

    return pl.pallas_call(
        body, name="place_own_columns",
        grid_spec=pltpu.PrefetchScalarGridSpec(
            num_scalar_prefetch=1, grid=(1,),
            in_specs=[pl.BlockSpec((rows, cols), lambda i, me_ref: (0, 0))],
            out_specs=pl.BlockSpec((rows, cols), lambda i, me_ref: (0, me_ref[0]))),
        out_shape=jax.ShapeDtypeStruct((rows, N_DEV * cols), shard.dtype),
    )(jnp.reshape(me, (1,)).astype(jnp.int32), shard)


def _local_step(x, target, mod, win, late_weights, p, grads_ready=None):
    shift1, scale1, gate1, shift2, scale2, gate2 = ((mod, k) for k in range(6))

    def after_token(v, token):
        return v if token is None else v + token[0:1, 0:1]
    wa, wx = p["lru_w_a"].astype(BF16), p["lru_w_x"].astype(BF16)
    mask = jnp.tril(jnp.ones((HD, HD), F32))
    wm = (p["sgu_w_s"] * mask).astype(BF16)
    wmt = jnp.swapaxes(wm, 1, 2)
    bst = jnp.transpose(p["sgu_b_s"])

    h1, z = _norm_proj(x, p["norm_mix_g"], scale1, shift1, win, "mix_proj")
    hstate, ya_pre, *rnn_saved = _rnn_fwd(
        z, p["rnn_conv_w"], p["rnn_conv_b"], wa, p["lru_b_a"], wx, p["lru_b_x"], p["lru_lambda"])
    yb_pre, *sgu_saved = _sgu_fwd(z, p["sgu_ln_g"], p["sgu_ln_b"], wm, bst)
    wba, wbb, wout = late_weights("merge", [ya_pre, yb_pre])
    x2, ya, yb, merged, o1 = _merge_fwd(ya_pre, yb_pre, z, x, gate1, wba, wbb, wout)
    wup = late_weights("ffn_up", [x2])
    h2, up_a, up_v, ff, fa, fv = _ffn_proj_mid(
        x2, p["norm_ffn_g"], scale2, shift2, wup, p["ffn_conv_w"], p["ffn_conv_b"])
    wd = late_weights("ffn_down", [ff])
    dx3, loss, d_gfin, d_gate2 = _ffn_out_loss(ff, wd, x2, target, gate2, p["norm_final_g"])

    dact, dval, d_wd, dcb_a, dcb_v = _ffn_down_bwd(dx3, gate2, ff, fa, fv, wd)
    dup, dx2, do1, d_cwf, d_shift2, d_scale2, d_gffn, d_gate1 = _ffn_up_bwd(
        dact, dval, up_a, up_v, p["ffn_conv_w"], wup, x2, dx3, p["norm_ffn_g"], scale2, o1, gate1)
    d_wup = _xt_y(h2, dup, "w_up_grad")
    ready = grads_ready if grads_ready else (lambda stage, big, small: None)
    token = ready("ffn", {"w_up": d_wup, "w_down": d_wd}, {})

    dya, dyb, dz, d_wout, d_win = _out_bwd(do1, wout, merged, ya, yb, z, h1)
    dz, d_win, d_wba, d_cw, d_cb, d_wa, d_ba, d_wx, d_bx, d_lam = _rnn_bwd(
        dya, ya_pre, wba, h1, z, rnn_saved, hstate, dz, d_win, p["rnn_conv_w"], wa, wx,
        after_token(p["lru_lambda"], token))
    small = {
        "rnn_conv_w": d_cw, "rnn_conv_b": d_cb, "lru_w_a": d_wa, "lru_b_a": d_ba, "lru_w_x": d_wx, "lru_b_x": d_bx,
        "lru_lambda": d_lam, "norm_ffn_g": d_gffn, "ffn_conv_w": d_cwf,
        "ffn_conv_b": jnp.concatenate([dcb_a, dcb_v], axis=1), "norm_final_g": d_gfin,
    }
    token = ready("rnn", {}, small)
    dz, d_win, d_wbb, d_ws, d_bst, d_lng, d_lnb = _sgu_bwd(
        dyb, yb_pre, wbb, h1, sgu_saved, dz, d_win, p["sgu_ln_g"], after_token(p["sgu_ln_b"], token), wmt, mask)
    sgu_small = {"sgu_ln_g": d_lng, "sgu_ln_b": d_lnb, "sgu_w_s": d_ws, "sgu_b_s": jnp.transpose(d_bst)}
    mixer = {"w_in": d_win, "w_out": d_wout, "w_branch_a": d_wba, "w_branch_b": d_wbb}
    token = ready("mixer", mixer, sgu_small)
    grad_x, d_shift1, d_scale1, d_gmix = _in_bwd(dz, win, x, dx2, after_token(p["norm_mix_g"], token), scale1)

    small.update(sgu_small)
    small["norm_mix_g"] = d_gmix
    dmod = jnp.stack([d_shift1, d_scale1, d_gate1, d_shift2, d_scale2, d_gate2])
    big = {"w_in": d_win, "w_up": d_wup, "w_branch_a": d_wba, "w_branch_b": d_wbb, "w_out": d_wout, "w_down": d_wd}
    return loss, grad_x, big, small, dmod


LAST_REP = ["b_ada", "norm_mix_g"]
EARLY_REP = ["rnn_conv_b", "lru_w_a", "lru_b_a", "lru_w_x", "lru_b_x", "lru_lambda", "norm_ffn_g", "ffn_conv_b",
             "norm_final_g"]
MID_REP = ["sgu_ln_g", "sgu_ln_b", "sgu_w_s", "sgu_b_s"]
COL_SHARDED = ["rnn_conv_w", "ffn_conv_w"]
SMALL_GROUPS = {"rnn": EARLY_REP + COL_SHARDED, "mixer": MID_REP, "last": LAST_REP}
REPLICATED = LAST_REP + EARLY_REP + MID_REP
SMALL_NAMES = REPLICATED + COL_SHARDED
BIG_NAMES = ["w_in", "w_up", "w_branch_a", "w_branch_b", "w_out", "w_down"]
BIG_AXES = [1, 1, 0, 0, 0, 0]
WEIGHTS = ["w_ada", "b_ada", "norm_mix_g", "w_in", "rnn_conv_w", "rnn_conv_b", "lru_w_a", "lru_b_a", "lru_w_x",
           "lru_b_x", "lru_lambda", "sgu_ln_g", "sgu_ln_b", "sgu_w_s", "sgu_b_s", "w_branch_a", "w_branch_b",
           "w_out", "norm_ffn_g", "w_up", "ffn_conv_w", "ffn_conv_b", "w_down", "norm_final_g"]


def _pack_rows(shape):
    return math.prod(shape) // LANES


def _pack(arrays):
    return jnp.concatenate([a.reshape(-1, LANES) for a in arrays], axis=0)


def kernel(x, c, w_ada, b_ada, norm_mix_g, w_in, rnn_conv_w, rnn_conv_b, lru_w_a, lru_b_a, lru_w_x, lru_b_x, lru_lambda, sgu_ln_g, sgu_ln_b, sgu_w_s, sgu_b_s, w_branch_a, w_branch_b, w_out, norm_ffn_g, w_up, ffn_conv_w, ffn_conv_b, w_down, norm_final_g, loss_target, m_w_ada, m_b_ada, m_norm_mix_g, m_w_in, m_rnn_conv_w, m_rnn_conv_b, m_lru_w_a, m_lru_b_a, m_lru_w_x, m_lru_b_x, m_lru_lambda, m_sgu_ln_g, m_sgu_ln_b, m_sgu_w_s, m_sgu_b_s, m_w_branch_a, m_w_branch_b, m_w_out, m_norm_ffn_g, m_w_up, m_ffn_conv_w, m_ffn_conv_b, m_w_down, m_norm_final_g, v_w_ada, v_b_ada, v_norm_mix_g, v_w_in, v_rnn_conv_w, v_rnn_conv_b, v_lru_w_a, v_lru_b_a, v_lru_w_x, v_lru_b_x, v_lru_lambda, v_sgu_ln_g, v_sgu_ln_b, v_sgu_w_s, v_sgu_b_s, v_w_branch_a, v_w_branch_b, v_w_out, v_norm_ffn_g, v_w_up, v_ffn_conv_w, v_ffn_conv_b, v_w_down, v_norm_final_g):
    given = dict(locals())
    me = 4 * lax.axis_index("x") + 2 * lax.axis_index("y") + lax.axis_index("c")
    ada_cols = w_ada.shape[2]
    conv_cols = {"rnn_conv_w": rnn_conv_w.shape[2], "ffn_conv_w": ffn_conv_w.shape[2]}

    (win, c_all, cw_rnn, cw_ffn), _ = _all_gather(
        [w_in[0].astype(BF16), c.reshape(1, 1, D), rnn_conv_w[0], ffn_conv_w[0]], [1, 0, 1, 1], "gather_first")
    c_all = c_all.reshape(N_DEV, D)

    b_cols = lax.dynamic_slice_in_dim(b_ada, me * ada_cols, ada_cols, axis=1)
    (mod_all,), mod_done = _all_gather(
        [_mod_cols(c_all, w_ada[0], b_cols).reshape(1, N_DEV, ada_cols)], [0], "gather_mod")
    mod_all = mod_all.reshape(N_DEV, N_DEV, ada_cols)
    mod_mine = lax.dynamic_index_in_dim(mod_all, me, axis=1, keepdims=False).reshape(6, 1, D)

    late_groups = {"merge": (["w_branch_a", "w_branch_b", "w_out"], [0, 0, 0]), "ffn_up": (["w_up"], [1]),
                   "ffn_down": (["w_down"], [0])}
    in_flight, started = {}, mod_done[0:1, 0:1]
    for stage, (names, axes) in late_groups.items():
        shards = [(given[n][0] + started).astype(BF16) for n in names]
        plan = _gather_plan(axes, [s.shape[-1] for s in shards])
        send, recv, srcs, lands, token = _exchange_start(
            "gather_start_" + stage, shards, [_own_block_placed(s, ax, me) for s, ax in zip(shards, axes)], plan,
            len(shards) * (N_DEV - 1))
        in_flight[stage] = (send, recv, srcs, lands, plan)
        started = started + token[0:1, 0:1]

    def late_weights(stage, after):
        send, recv, srcs, lands, plan = in_flight[stage]
        full = _exchange_wait("gather_wait_" + stage, send, recv, srcs, lands, plan, after)
        full = [w.reshape(-1, D) if ax == 0 else w for w, ax in zip(full, late_groups[stage][1])]
        return full if len(full) > 1 else full[0]

    mod_mine = mod_mine + started

    reducing, packing = {}, {}

    def start_pack(stage, small):
        pack = _pack([small[n] for n in SMALL_GROUPS[stage]])[None]
        plan = _gather_plan([0], [LANES])
        send, recv, srcs, lands, tok = _exchange_start(
            "small_start_" + stage, [pack], [_own_block_placed(pack, 0, me)], plan, N_DEV - 1)
        packing[stage] = (send, recv, srcs, lands, plan)
        return tok

    def grads_ready(stage, grads, small):
        tokens = [start_pack(stage, small)] if small else []
        if grads:
            tokens.append(start_reduce(stage, grads))
        return sum(tokens[1:], tokens[0])

    def start_reduce(stage, grads):
        names = [n for n in BIG_NAMES if n in grads]
        blocked = {}
        for n in names:
            ax = BIG_AXES[BIG_NAMES.index(n)]
            g = grads[n] if ax == 1 else grads[n].reshape(N_DEV, grads[n].shape[0] // N_DEV, grads[n].shape[1])
            blocked.setdefault((ax, g.shape), []).append((n, g))
        sums = {}
        for (ax, _), group in blocked.items():
            reduced = _sibling_reduce([g for _, g in group], ax, "reduce_sibling_" + "_".join(n for n, _ in group))
            sums.update({n: r for (n, _), r in zip(group, reduced)})
        sums = [sums[n] for n in names]
        pays = [pay for _, pay in sums]
        send, recv, srcs, lands, tok = _exchange_start(
            "reduce_start_" + stage, pays, [lax.empty(p_.shape, p_.dtype) for p_ in pays], _chip_plan, 3 * len(pays))
        reducing[stage] = (names, [own for own, _ in sums], send, recv, srcs, lands)
        return tok

    p = {n: given[n][0] for n in REPLICATED if n not in ("b_ada", "norm_final_g")}
    p = {n: (a.reshape(1, -1) if a.ndim == 1 else a) for n, a in p.items()}
    p["rnn_conv_w"], p["ffn_conv_w"] = cw_rnn, cw_ffn
    p["norm_final_g"] = norm_final_g.reshape(1, D)
    loss, grad_x, _, small, dmod = _local_step(x[0], loss_target[0], mod_mine, win, late_weights, p, grads_ready)

    small["b_ada"] = dmod.reshape(1, 6 * D)
    rows_of = {n: _pack_rows(small[n].shape) for n in SMALL_NAMES}
    last_started = start_pack("last", small)
    gathered = {}
    for stage in ("rnn", "mixer"):
        send, recv, srcs, lands, plan = packing[stage]
        (gathered[stage],) = _exchange_wait("small_wait_" + stage, send, recv, srcs, lands, plan, [grad_x])

    out = {}
    for stage, (names, owns, send, recv, srcs, lands) in reducing.items():
        landed = _exchange_wait("reduce_wait_" + stage, send, recv, srcs, lands, _chip_plan, [last_started])
        for n, own, got in zip(names, owns, landed):
            out[n] = _adamw(given[n][0], given["m_" + n][0], given["v_" + n][0], [own, got], "adamw_" + n)
    send, recv, srcs, lands, plan = packing["last"]
    (gathered["last"],) = _exchange_wait(
        "small_wait_last", send, recv, srcs, lands, plan, [out[n][1] for n in BIG_NAMES])
    gathered = {k: v.reshape(N_DEV, -1, LANES) for k, v in gathered.items()}

    dmod_all = gathered["last"][:, :rows_of["b_ada"]].reshape(N_DEV, 6 * D)
    dmod_cols = lax.dynamic_slice_in_dim(dmod_all, me * ada_cols, ada_cols, axis=1)
    out["w_ada"] = _adamw(w_ada[0], m_w_ada[0], v_w_ada[0], [_ada_grad(c_all, dmod_cols)], "adamw_w_ada")

    def rows_form(a):
        return a.reshape(1, -1) if a.size // a.shape[-1] == 1 or a.ndim == 1 else a.reshape(-1, LANES)

    for stage, names in (("last", LAST_REP), ("rnn", EARLY_REP), ("mixer", MID_REP)):
        out.update(_adamw_group(names, *[[rows_form(given[pre + n]) for n in names] for pre in ("", "m_", "v_")],
                                gathered[stage], "adamw_small_" + stage))

    row0 = sum(rows_of[n] for n in EARLY_REP)
    for n in COL_SHARDED:
        full = gathered["rnn"][:, row0:row0 + rows_of[n]].reshape(N_DEV, small[n].shape[0], small[n].shape[1])
        mine = lax.dynamic_slice_in_dim(full, me * conv_cols[n], conv_cols[n], axis=2)
        out[n] = _adamw(given[n][0], given["m_" + n][0], given["v_" + n][0], [mine], "adamw_" + n)
        row0 += rows_of[n]

    total = lax.psum(loss[0, 0], ("x", "y", "c"))
    results = [total, grad_x[None]]
    for kind in range(4):
        results += [out[n][kind].reshape(given[n].shape) for n in WEIGHTS]
    return tuple(results)
```

```python
import math

import jax
import jax.numpy as jnp
from jax import lax
from jax.experimental import pallas as pl
from jax.experimental.pallas import tpu as pltpu

F32 = jnp.float32
BF16 = jnp.bfloat16
MESH_IDS = pl.DeviceIdType.MESH

D = 1024
NH = 8
HD = 128
NCOL_IN = 6 * D
DFF = 3 * D
N_DEV = 8
EPS = 1e-6
LRU_C = 8.0
ADAM_LR, ADAM_B1, ADAM_B2, ADAM_EPS, ADAM_WD, ADAM_STEP = 0.001, 0.9, 0.999, 1e-08, 0.01, 10

SUBLANES = 8
LANES = 128
HALO = 16
VMEM_LIMIT = 56 * 1024 * 1024
GELU_K = math.sqrt(2.0 / math.pi)
GELU_C = 0.044715


def _cparams(n_axes):
    return pltpu.CompilerParams(dimension_semantics=("arbitrary",) * n_axes, vmem_limit_bytes=VMEM_LIMIT)


def _const_spec(shape, single_buffer=False):
    nd = len(shape)
    if single_buffer:
        return pl.BlockSpec(shape, lambda *_: (0,) * nd, pipeline_mode=pl.Buffered(1))
    return pl.BlockSpec(shape, lambda *_: (0,) * nd)


def _vec_operand(v):
    if isinstance(v, tuple):
        stack, k = v
        return stack, pl.BlockSpec((None, 1, D), lambda *_: (k, 0, 0))
    return v, _const_spec((1, D))


def _tile_big(t):
    return min(512, t)


def _tile_seq(t):
    return min(256, t)


def _row_tile(rows, cols):
    cap = max(SUBLANES, (2 * 1024 * 1024) // (4 * cols) // SUBLANES * SUBLANES)
    if rows <= cap:
        return rows
    return next(tr for tr in range(cap, 0, -SUBLANES) if rows % tr == 0)


def _gelu_t(x):
    x2 = x * x
    t = jnp.tanh(x * (GELU_K + (GELU_K * GELU_C) * x2))
    hx = 0.5 * x
    return hx + hx * t, (x2, hx, t)


def _gelu_grad(shared):
    x2, hx, t = shared
    return (0.5 + 0.5 * t) + (hx * (1.0 - t * t)) * (GELU_K + (3.0 * GELU_K * GELU_C) * x2)


def _sigmoid(x):
    return 1.0 / (1.0 + jnp.exp(-x))


def _log_sigmoid(x):
    return -(jnp.maximum(-x, 0.0) + jnp.log1p(jnp.exp(-jnp.abs(x))))


def _row_iota(cols):
    return lax.broadcasted_iota(jnp.int32, (SUBLANES, cols), 0)


def _shift_down(x, k, prev8):
    if k == 0:
        return x
    r = pltpu.roll(x, k, 0)
    p = pltpu.roll(prev8, k, 0)
    head = jnp.where(_row_iota(x.shape[1]) < k, p, r[:SUBLANES])
    return jnp.concatenate([head, r[SUBLANES:]], axis=0)


def _shift_up(x, k, next8):
    if k == 0:
        return x
    n = x.shape[0]
    r = pltpu.roll(x, n - k, 0)
    q = pltpu.roll(next8, SUBLANES - k, 0)
    tail = jnp.where(_row_iota(x.shape[1]) >= SUBLANES - k, q, r[n - SUBLANES:])
    return jnp.concatenate([r[:n - SUBLANES], tail], axis=0)


def _heads_nn(x_bf, w_ref):
    return jnp.concatenate(
        [jnp.dot(x_bf[:, h * HD:(h + 1) * HD], w_ref[h], preferred_element_type=F32) for h in range(NH)], axis=1)


def _heads_nt(x_bf, w_ref):
    return jnp.concatenate(
        [lax.dot_general(x_bf[:, h * HD:(h + 1) * HD], w_ref[h], (((1,), (1,)), ((), ())), preferred_element_type=F32)
         for h in range(NH)], axis=1)


def _dot_nt(a, b):
    return lax.dot_general(a, b, (((1,), (1,)), ((), ())), preferred_element_type=F32)


def _dot_tn(a, b):
    return lax.dot_general(a, b, (((0,), (0,)), ((), ())), preferred_element_type=F32)


def _colsum(x):
    return jnp.sum(x, axis=0, keepdims=True)


def _prev_halo_map(tm, col):
    return lambda i, *_: (jnp.maximum(i * (tm // HALO) - 1, 0), col)


def _norm_proj(x, g, scale, shift, w, name):
    t, n = x.shape[0], w.shape[1]
    tm = _tile_big(t)

    def body(x_ref, g_ref, sc_ref, sh_ref, w_ref, h_ref, z_ref):
        xv = x_ref[...]
        r = lax.rsqrt(jnp.mean(xv * xv, axis=-1, keepdims=True) + EPS)
        hb = ((xv * r * g_ref[...]) * (1.0 + sc_ref[...]) + sh_ref[...]).astype(BF16)
        h_ref[...] = hb
        for c0 in range(0, n, D):
            z_ref[:, c0:c0 + D] = jnp.dot(hb, w_ref[:, c0:c0 + D], preferred_element_type=F32).astype(BF16)

    vec = _const_spec((1, D))
    (scale, sc_spec), (shift, sh_spec) = _vec_operand(scale), _vec_operand(shift)
    return pl.pallas_call(
        body, name=name, grid=(t // tm,),
        in_specs=[pl.BlockSpec((tm, D), lambda i: (i, 0)), vec, sc_spec, sh_spec, _const_spec((D, n), True)],
        out_specs=[pl.BlockSpec((tm, D), lambda i: (i, 0)), pl.BlockSpec((tm, n), lambda i: (i, 0))],
        out_shape=[jax.ShapeDtypeStruct((t, D), BF16), jax.ShapeDtypeStruct((t, n), BF16)],
        compiler_params=_cparams(1),
    )(x, g, scale, shift, w)


def _lru_gates(xc, wa_ref, ba, wx_ref, bx, ls):
    xb = xc.astype(BF16)
    ra = _sigmoid(_heads_nn(xb, wa_ref) + ba)
    ia = _sigmoid(_heads_nn(xb, wx_ref) + bx)
    la = LRU_C * ra * ls
    a = jnp.exp(la)
    mult = jnp.sqrt(-jnp.tanh(la) * (1.0 + a * a))
    return ra, ia, a, mult


def _conv4(xr, prev8, cw_ref, cb):
    return (cb + cw_ref[3:4, :] * xr + cw_ref[2:3, :] * _shift_down(xr, 1, prev8)
            + cw_ref[1:2, :] * _shift_down(xr, 2, prev8) + cw_ref[0:1, :] * _shift_down(xr, 3, prev8))


def _rnn_fwd(z, cw, cb, wa, ba, wx, bx, lam):
    t = z.shape[0]
    tm = _tile_seq(t)
    ngrp = tm // SUBLANES

    def body(xr_ref, xp_ref, gr_ref, cw_ref, cb_ref, wa_ref, ba_ref, wx_ref, bx_ref, lam_ref,
             h_ref, ya_ref, xc_ref, ra_ref, ia_ref, gg_ref, hg_ref, carry_ref, a_scr, u_scr):
        i = pl.program_id(0)

        @pl.when(i == 0)
        def _():
            carry_ref[...] = jnp.zeros_like(carry_ref)

        xr = xr_ref[...].astype(F32)
        prev8 = jnp.where(i == 0, 0.0, xp_ref[...].astype(F32)[HALO - SUBLANES:])
        xc = _conv4(xr, prev8, cw_ref, cb_ref[...])
        ra, ia, a, mult = _lru_gates(xc, wa_ref, ba_ref[...], wx_ref, bx_ref[...], _log_sigmoid(lam_ref[...]))
        xc_ref[...] = xc.astype(BF16)
        ra_ref[...] = ra.astype(BF16)
        ia_ref[...] = ia.astype(BF16)
        a_scr[...] = a
        u_scr[...] = mult * (ia * xc)
        row = _row_iota(D)

        def grp(j, carry):
            r0 = pl.multiple_of(j * SUBLANES, SUBLANES)
            av = a_scr[pl.ds(r0, SUBLANES), :]
            uv = u_scr[pl.ds(r0, SUBLANES), :]
            for d in (1, 2, 4):
                m = row >= d
                uv = jnp.where(m, av * pltpu.roll(uv, d, 0) + uv, uv)
                av = jnp.where(m, av * pltpu.roll(av, d, 0), av)
            hv = uv + av * carry
            h_ref[pl.ds(r0, SUBLANES), :] = hv
            return hv[SUBLANES - 1:SUBLANES, :]

        carry_ref[0:1, :] = lax.fori_loop(0, ngrp, grp, carry_ref[0:1, :])
        grv = gr_ref[...].astype(F32)
        gg, tg = _gelu_t(grv)
        hv = h_ref[...]
        ya_ref[...] = (hv * gg).astype(BF16)
        gg_ref[...] = gg.astype(BF16)
        hg_ref[...] = (hv * _gelu_grad(tg)).astype(BF16)

    vec = _const_spec((1, D))
    wspec = _const_spec((NH, HD, HD))
    tile = pl.BlockSpec((tm, D), lambda i: (i, 0))
    bshape = jax.ShapeDtypeStruct((t, D), BF16)
    return pl.pallas_call(
        body, name="rnn_fwd", grid=(t // tm,),
        in_specs=[tile, pl.BlockSpec((HALO, D), _prev_halo_map(tm, 0)),
                  pl.BlockSpec((tm, D), lambda i: (i, 1)), _const_spec((4, D)), vec, wspec, vec, wspec, vec, vec],
        out_specs=[tile] * 7,
        out_shape=[jax.ShapeDtypeStruct((t, D), F32)] + [bshape] * 6,
        scratch_shapes=[pltpu.VMEM((SUBLANES, D), F32), pltpu.VMEM((tm, D), F32), pltpu.VMEM((tm, D), F32)],
        compiler_params=_cparams(1),
    )(z, z, z, cw, cb, wa, ba, wx, bx, lam)


def _sgu_fwd(z, lng, lnb, wm, bst):
    t = z.shape[0]
    tm = _tile_seq(t)

    def body(zu_ref, zv_ref, lng_ref, lnb_ref, wm_ref, bst_ref, yb_ref, gu_ref, mg_ref, vh_ref, gpv_ref, rstd_ref):
        gu, su = _gelu_t(zu_ref[...].astype(F32))
        gv, sv = _gelu_t(zv_ref[...].astype(F32))
        mu = jnp.mean(gv, axis=-1, keepdims=True)
        cen = gv - mu
        rstd = lax.rsqrt(jnp.mean(cen * cen, axis=-1, keepdims=True) + EPS)
        vhat = cen * rstd
        vb = (vhat * lng_ref[...] + lnb_ref[...]).astype(BF16)
        rows = []
        for b0 in range(0, tm, HD):
            rows.append(jnp.concatenate(
                [jnp.dot(wm_ref[g], vb[b0:b0 + HD, g * HD:(g + 1) * HD], preferred_element_type=F32)
                 + bst_ref[:, g:g + 1] for g in range(NH)], axis=1))
        mixed = jnp.concatenate(rows, axis=0) if len(rows) > 1 else rows[0]
        yb_ref[...] = (gu * mixed).astype(BF16)
        gu_ref[...] = gu.astype(BF16)
        mg_ref[...] = (mixed * _gelu_grad(su)).astype(BF16)
        vh_ref[...] = vhat.astype(BF16)
        gpv_ref[...] = _gelu_grad(sv).astype(BF16)
        rstd_ref[...] = rstd

    vec = _const_spec((1, D))
    tile = pl.BlockSpec((tm, D), lambda i: (i, 0))
    bshape = jax.ShapeDtypeStruct((t, D), BF16)
    return pl.pallas_call(
        body, name="sgu_fwd", grid=(t // tm,),
        in_specs=[pl.BlockSpec((tm, D), lambda i: (i, 2)), pl.BlockSpec((tm, D), lambda i: (i, 3)), vec, vec,
                  _const_spec((NH, HD, HD)), _const_spec((HD, NH))],
        out_specs=[tile] * 5 + [pl.BlockSpec((tm, 1), lambda i: (i, 0))],
        out_shape=[bshape] * 5 + [jax.ShapeDtypeStruct((t, 1), F32)],
        compiler_params=_cparams(1),
    )(z, z, lng, lnb, wm, bst)


def _merge_fwd(ya_pre, yb_pre, z, x, gate1, wba, wbb, wout):
    t = x.shape[0]
    tm = _tile_big(t)

    def body(yap_ref, ybp_ref, ga_ref, gb_ref, x_ref, g1_ref, wba_ref, wbb_ref, wo_ref,
             x2_ref, ya_ref, yb_ref, mg_ref, o1_ref):
        ya = jnp.dot(yap_ref[...], wba_ref[...], preferred_element_type=F32)
        yb = jnp.dot(ybp_ref[...], wbb_ref[...], preferred_element_type=F32)
        merged = _sigmoid(ga_ref[...].astype(F32)) * ya + _sigmoid(gb_ref[...].astype(F32)) * yb
        mb = merged.astype(BF16)
        o1 = jnp.dot(mb, wo_ref[...], preferred_element_type=F32)
        x2_ref[...] = x_ref[...] + g1_ref[...] * o1
        ya_ref[...] = ya.astype(BF16)
        yb_ref[...] = yb.astype(BF16)
        mg_ref[...] = mb
        o1_ref[...] = o1.astype(BF16)

    tile = pl.BlockSpec((tm, D), lambda i: (i, 0))
    wspec = _const_spec((D, D))
    bshape = jax.ShapeDtypeStruct((t, D), BF16)
    gate1, g1_spec = _vec_operand(gate1)
    return pl.pallas_call(
        body, name="merge_fwd", grid=(t // tm,),
        in_specs=[tile, tile, pl.BlockSpec((tm, D), lambda i: (i, 4)), pl.BlockSpec((tm, D), lambda i: (i, 5)),
                  tile, g1_spec, wspec, wspec, wspec],
        out_specs=[tile] * 5,
        out_shape=[jax.ShapeDtypeStruct((t, D), F32), bshape, bshape, bshape, bshape],
        compiler_params=_cparams(1),
    )(ya_pre, yb_pre, z, z, x, gate1, wba, wbb, wout)


def _conv3(u, prev8, cw_ref, cb):
    return cb + cw_ref[2:3, :] * u + cw_ref[1:2, :] * _shift_down(u, 1, prev8) + cw_ref[0:1, :] * _shift_down(u, 2, prev8)


def _ffn_proj_mid(x2, g, scale, shift, w, cw, cb):
    t = x2.shape[0]
    tm = _tile_big(t)
    nc = DFF // D

    def body(x_ref, g_ref, sc_ref, sh_ref, wa_ref, wv_ref, cwa_ref, cwv_ref, cba_ref, cbv_ref,
             h_ref, upa_ref, upv_ref, ff_ref, fa_ref, fv_ref, hb_scr, prev_ref):
        i, c = pl.program_id(0), pl.program_id(1)

        @pl.when(i == 0)
        def _():
            prev_ref[c] = jnp.zeros((2, SUBLANES, D), F32)

        @pl.when(c == 0)
        def _():
            xv = x_ref[...]
            r = lax.rsqrt(jnp.mean(xv * xv, axis=-1, keepdims=True) + EPS)
            hb_scr[...] = ((xv * r * g_ref[...]) * (1.0 + sc_ref[...]) + sh_ref[...]).astype(BF16)
            h_ref[...] = hb_scr[...]

        hb = hb_scr[...]
        halves = []
        for s, (w_ref, up_ref, cw_ref, cb_ref) in enumerate(((wa_ref, upa_ref, cwa_ref, cba_ref),
                                                             (wv_ref, upv_ref, cwv_ref, cbv_ref))):
            u = jnp.dot(hb, w_ref[...], preferred_element_type=F32)
            up_ref[...] = u.astype(BF16)
            halves.append(_conv3(u, prev_ref[c, s], cw_ref, cb_ref[...]))
            prev_ref[c, s] = u[tm - SUBLANES:]
        act, val = halves
        ga, ta = _gelu_t(act)
        ff_ref[...] = (ga * val).astype(BF16)
        fa_ref[...] = (val * _gelu_grad(ta)).astype(BF16)
        fv_ref[...] = ga.astype(BF16)

    def cols(rows, off):
        return pl.BlockSpec((rows, D), lambda i, c: (0, off + c))

    vec = pl.BlockSpec((1, D), lambda i, c: (0, 0))
    row_tile = pl.BlockSpec((tm, D), lambda i, c: (i, 0))
    chunk = pl.BlockSpec((tm, D), lambda i, c: (i, c))
    hshape = jax.ShapeDtypeStruct((t, DFF), BF16)
    (scale, sc_spec), (shift, sh_spec) = _vec_operand(scale), _vec_operand(shift)
    return pl.pallas_call(
        body, name="ffn_proj_mid", grid=(t // tm, nc),
        in_specs=[row_tile, vec, sc_spec, sh_spec, cols(D, 0), cols(D, nc), cols(3, 0), cols(3, nc), cols(1, 0), cols(1, nc)],
        out_specs=[row_tile, chunk, chunk, chunk, chunk, chunk],
        out_shape=[jax.ShapeDtypeStruct((t, D), BF16), hshape, hshape, hshape, hshape, hshape],
        scratch_shapes=[pltpu.VMEM((tm, D), BF16), pltpu.VMEM((nc, 2, SUBLANES, D), F32)],
        compiler_params=_cparams(2),
    )(x2, g, scale, shift, w, w, cw, cw, cb, cb)


def _ffn_out_loss(ff, wd, x2, target, gate2, gfin):
    t = x2.shape[0]
    tm = _tile_big(t)

    def body(ff_ref, wd_ref, x2_ref, tg_ref, g2_ref, gf_ref, dx3_ref, do2_ref, loss_ref, dgf_ref, dg2_ref):
        @pl.when(pl.program_id(0) == 0)
        def _():
            loss_ref[...] = jnp.zeros_like(loss_ref)
            dgf_ref[...] = jnp.zeros_like(dgf_ref)
            dg2_ref[...] = jnp.zeros_like(dg2_ref)

        o2 = jnp.dot(ff_ref[...], wd_ref[...], preferred_element_type=F32)
        x3 = x2_ref[...] + g2_ref[...] * o2
        r = lax.rsqrt(jnp.mean(x3 * x3, axis=-1, keepdims=True) + EPS)
        xhat = x3 * r
        err = xhat * gf_ref[...] - tg_ref[...]
        loss_ref[...] += 0.5 * jnp.sum(jnp.mean(err * err, axis=-1, keepdims=True), axis=0, keepdims=True)
        dy = err * (1.0 / D)
        dgf_ref[...] += _colsum(dy * xhat)
        dxh = dy * gf_ref[...]
        dx3 = r * (dxh - xhat * jnp.mean(dxh * xhat, axis=-1, keepdims=True))
        dx3_ref[...] = dx3
        do2_ref[...] = (dx3 * g2_ref[...]).astype(BF16)
        dg2_ref[...] += _colsum(dx3 * o2)

    tile = pl.BlockSpec((tm, D), lambda i: (i, 0))
    vec = _const_spec((1, D))
    gate2, g2_spec = _vec_operand(gate2)
    return pl.pallas_call(
        body, name="ffn_out_loss", grid=(t // tm,),
        in_specs=[pl.BlockSpec((tm, DFF), lambda i: (i, 0)), _const_spec((DFF, D), True), tile, tile, g2_spec, vec],
        out_specs=[tile, tile, _const_spec((1, 1)), vec, vec],
        out_shape=[jax.ShapeDtypeStruct((t, D), F32), jax.ShapeDtypeStruct((t, D), BF16),
                   jax.ShapeDtypeStruct((1, 1), F32),
                   jax.ShapeDtypeStruct((1, D), F32), jax.ShapeDtypeStruct((1, D), F32)],
        compiler_params=_cparams(1),
    )(ff, wd, x2, target, gate2, gfin)


def _ffn_down_bwd(do2, ff, fa, fv, wd):
    t = do2.shape[0]
    tm = min(1024, t)
    nc = DFF // D

    def body(do2_ref, ff_ref, fa_ref, fv_ref, wd_ref, da_ref, dv_ref, dwd_ref, dcba_ref, dcbv_ref):
        @pl.when(pl.program_id(1) == 0)
        def _():
            for r in (dwd_ref, dcba_ref, dcbv_ref):
                r[...] = jnp.zeros_like(r)

        do2 = do2_ref[...]
        dwd_ref[...] += _dot_tn(ff_ref[...], do2)
        dff = _dot_nt(do2, wd_ref[...])
        dact = dff * fa_ref[...].astype(F32)
        dval = dff * fv_ref[...].astype(F32)
        da_ref[...] = dact.astype(BF16)
        dv_ref[...] = dval.astype(BF16)
        dcba_ref[...] += _colsum(dact)
        dcbv_ref[...] += _colsum(dval)

    blk = pl.BlockSpec((tm, D), lambda c, i: (i, c))
    vec = pl.BlockSpec((1, D), lambda c, i: (0, c))
    return pl.pallas_call(
        body, name="ffn_down_bwd", grid=(nc, t // tm),
        in_specs=[pl.BlockSpec((tm, D), lambda c, i: (i, 0)),
                  blk, blk, blk, pl.BlockSpec((D, D), lambda c, i: (c, 0))],
        out_specs=[blk, blk, pl.BlockSpec((D, D), lambda c, i: (c, 0)), vec, vec],
        out_shape=[jax.ShapeDtypeStruct((t, DFF), BF16), jax.ShapeDtypeStruct((t, DFF), BF16),
                   jax.ShapeDtypeStruct((DFF, D), F32),
                   jax.ShapeDtypeStruct((1, DFF), F32), jax.ShapeDtypeStruct((1, DFF), F32)],
        compiler_params=_cparams(2),
    )(do2, ff, fa, fv, wd)


def _modnorm_bwd(dh, xv, g, scale):
    r = lax.rsqrt(jnp.mean(xv * xv, axis=-1, keepdims=True) + EPS)
    xhat = xv * r
    dxn = dh * (1.0 + scale)
    dxh = dxn * g
    dx = r * (dxh - xhat * jnp.mean(dxh * xhat, axis=-1, keepdims=True))
    return dx, _colsum(dh), _colsum(dh * (xhat * g)), _colsum(dxn * xhat)


def _ffn_up_bwd(dact, dval, up_a, up_v, cw, wup, x2, dx3, gffn, scale2, o1, gate1):
    t = x2.shape[0]
    tm = _tile_seq(t)
    nt = t // tm
    nc = DFF // D

    def body(da_ref, dan_ref, dv_ref, dvn_ref, ua_ref, uv_ref, cw_ref, w_ref, x2_ref, dx3_ref, g_ref, sc_ref, o1_ref, g1_ref,
             dup_ref, dx2_ref, do1_ref, dcw_ref, dsh_ref, dsc_ref, dg_ref, dg1_ref):
        i = pl.program_id(0)

        @pl.when(i == 0)
        def _():
            for r in (dcw_ref, dsh_ref, dsc_ref, dg_ref, dg1_ref):
                r[...] = jnp.zeros_like(r)

        last = i == nt - 1
        dh = jnp.zeros((tm, D), F32)
        for half, (d_ref, dn_ref, u_ref) in enumerate(((da_ref, dan_ref, ua_ref), (dv_ref, dvn_ref, uv_ref))):
            nxt = jnp.where(last, 0.0, dn_ref[...].astype(F32)[:SUBLANES])
            for c in range(nc):
                c0 = half * DFF + c * D
                dv = d_ref[:, c * D:(c + 1) * D].astype(F32)
                nx = nxt[:, c * D:(c + 1) * D]
                taps = (_shift_up(dv, 2, nx), _shift_up(dv, 1, nx), dv)
                dup = (cw_ref[2:3, c0:c0 + D] * taps[2] + cw_ref[1:2, c0:c0 + D] * taps[1]
                       + cw_ref[0:1, c0:c0 + D] * taps[0]).astype(BF16)
                upv = u_ref[:, c * D:(c + 1) * D].astype(F32)
                for k in range(3):
                    dcw_ref[k:k + 1, c0:c0 + D] += _colsum(taps[k] * upv)
                dup_ref[:, c0:c0 + D] = dup
                dh = dh + _dot_nt(dup, w_ref[:, c0:c0 + D])
        dxn, dsh, dsc, dg = _modnorm_bwd(dh, x2_ref[...], g_ref[...], sc_ref[...])
        dx2 = dx3_ref[...] + dxn
        dx2_ref[...] = dx2
        do1_ref[...] = (dx2 * g1_ref[...]).astype(BF16)
        dsh_ref[...] += dsh
        dsc_ref[...] += dsc
        dg_ref[...] += dg
        dg1_ref[...] += _colsum(dx2 * o1_ref[...].astype(F32))

    tile = pl.BlockSpec((tm, D), lambda i: (i, 0))
    wide = pl.BlockSpec((tm, DFF), lambda i: (i, 0))
    nxt = pl.BlockSpec((HALO, DFF), lambda i: (jnp.minimum((i + 1) * (tm // HALO), t // HALO - 1), 0))
    vec = _const_spec((1, D))
    vshape = jax.ShapeDtypeStruct((1, D), F32)
    (scale2, sc_spec), (gate1, g1_spec) = _vec_operand(scale2), _vec_operand(gate1)
    return pl.pallas_call(
        body, name="ffn_up_bwd", grid=(nt,),
        in_specs=[wide, nxt, wide, nxt, wide, wide,
                  _const_spec((3, 2 * DFF)), _const_spec((D, 2 * DFF), True),
                  tile, tile, vec, sc_spec, tile, g1_spec],
        out_specs=[pl.BlockSpec((tm, 2 * DFF), lambda i: (i, 0)), tile, tile, _const_spec((3, 2 * DFF)),
                   vec, vec, vec, vec],
        out_shape=[jax.ShapeDtypeStruct((t, 2 * DFF), BF16), jax.ShapeDtypeStruct((t, D), F32),
                   jax.ShapeDtypeStruct((t, D), BF16), jax.ShapeDtypeStruct((3, 2 * DFF), F32),
                   vshape, vshape, vshape, vshape],
        compiler_params=_cparams(1),
    )(dact, dact, dval, dval, up_a, up_v, cw, wup, x2, dx3, gffn, scale2, o1, gate1)


def _xt_y(a, b, name):
    t, k = a.shape
    n = b.shape[1]
    tm = min(1024, t)
    bn = 1536 if n % 1536 == 0 else D

    def body(a_ref, b_ref, o_ref):
        @pl.when(pl.program_id(1) == 0)
        def _():
            o_ref[...] = jnp.zeros_like(o_ref)

        o_ref[...] += _dot_tn(a_ref[...], b_ref[...])

    return pl.pallas_call(
        body, name=name, grid=(n // bn, t // tm),
        in_specs=[pl.BlockSpec((tm, k), lambda j, i: (i, 0)), pl.BlockSpec((tm, bn), lambda j, i: (i, j))],
        out_specs=pl.BlockSpec((k, bn), lambda j, i: (0, j)),
        out_shape=jax.ShapeDtypeStruct((k, n), F32),
        compiler_params=_cparams(2),
    )(a, b)


def _acc_spec(shape, index):
    return pl.BlockSpec(shape, lambda *_: index, pipeline_mode=pl.Buffered(1))


def _out_bwd(do1, wout, merged, ya, yb, z, h1):
    t = do1.shape[0]
    tm = _tile_big(t)

    def body(do1_ref, wo_ref, mg_ref, ya_ref, yb_ref, ga_ref, gb_ref, h1_ref,
             dya_ref, dyb_ref, dz_ref, dwo_ref, dwin_ref):
        @pl.when(pl.program_id(0) == 0)
        def _():
            dwo_ref[...] = jnp.zeros_like(dwo_ref)
            dwin_ref[...] = jnp.zeros_like(dwin_ref)

        do1v = do1_ref[...]
        dwo_ref[...] += _dot_tn(mg_ref[...], do1v)
        dm = _dot_nt(do1v, wo_ref[...])
        sa = _sigmoid(ga_ref[...].astype(F32))
        sb = _sigmoid(gb_ref[...].astype(F32))
        dya_ref[...] = (dm * sa).astype(BF16)
        dyb_ref[...] = (dm * sb).astype(BF16)
        dga = (dm * ya_ref[...].astype(F32) * sa * (1.0 - sa)).astype(BF16)
        dgb = (dm * yb_ref[...].astype(F32) * sb * (1.0 - sb)).astype(BF16)
        dz_ref[:, 0:D] = dga
        dz_ref[:, D:2 * D] = dgb
        h1v = h1_ref[...]
        dwin_ref[:, 0:D] += _dot_tn(h1v, dga)
        dwin_ref[:, D:2 * D] += _dot_tn(h1v, dgb)

    tile = pl.BlockSpec((tm, D), lambda i: (i, 0))
    bshape = jax.ShapeDtypeStruct((t, D), BF16)
    return pl.pallas_call(
        body, name="out_bwd", grid=(t // tm,),
        in_specs=[tile, _const_spec((D, D), True), tile, tile, tile,
                  pl.BlockSpec((tm, D), lambda i: (i, 4)), pl.BlockSpec((tm, D), lambda i: (i, 5)), tile],
        out_specs=[tile, tile, pl.BlockSpec((tm, 2 * D), lambda i: (i, 2)), _acc_spec((D, D), (0, 0)),
                   _acc_spec((D, 2 * D), (0, 2))],
        out_shape=[bshape, bshape, jax.ShapeDtypeStruct((t, NCOL_IN), BF16), jax.ShapeDtypeStruct((D, D), F32),
                   jax.ShapeDtypeStruct((D, NCOL_IN), F32)],
        compiler_params=_cparams(1),
    )(do1, wout, merged, ya, yb, z, z, h1)


def _rnn_bwd(dya, ya_pre, wba, h1, z, saved, h, dz, dwin, cw, wa, wx, lam):
    t = z.shape[0]
    tm = _tile_seq(t)
    nt = t // tm
    ngrp = tm // SUBLANES
    hpt = tm // HALO

    def body(dya_ref, yap_ref, wba_ref, h1_ref, xr_ref, xc_ref, ra_ref, ia_ref, gg_ref, hg_ref, h_ref, hp_ref,
             dz_any, dwin_any, cw_ref, wa_ref, wx_ref, lam_ref,
             dz_ref, dwin_ref, dwba_ref, dcw_ref, dcb_ref, dwa_ref, dba_ref, dwx_ref, dbx_ref, dlam_ref,
             a_first, g_first, dxc_first, b_scr, d_scr, g_scr):
        del dz_any, dwin_any
        i = pl.program_id(0)

        @pl.when(i == 0)
        def _():
            for r in (dwin_ref, dwba_ref, dcw_ref, dcb_ref, dwa_ref, dba_ref, dwx_ref, dbx_ref, dlam_ref,
                      a_first, g_first, dxc_first):
                r[...] = jnp.zeros_like(r)

        dya_v = dya_ref[...]
        dwba_ref[...] += _dot_tn(yap_ref[...], dya_v)
        dyap_v = _dot_nt(dya_v, wba_ref[...])
        h1v = h1_ref[...]

        first_tile = i == nt - 1
        xc = xc_ref[...].astype(F32)
        ra = ra_ref[...].astype(F32)
        ia = ia_ref[...].astype(F32)
        lam_v = lam_ref[...]
        ls = _log_sigmoid(lam_v)
        la = LRU_C * ra * ls
        a = jnp.exp(la)
        mult = jnp.sqrt(-jnp.tanh(la) * (1.0 + a * a))
        hprev8 = jnp.where(first_tile, 0.0, hp_ref[...][HALO - SUBLANES:])
        h_prev = _shift_down(h_ref[...], 1, hprev8)
        dgr = (dyap_v * hg_ref[...].astype(F32)).astype(BF16)
        dz_ref[:, D:2 * D] = dgr
        dwin_ref[:, D:2 * D] += _dot_tn(h1v, dgr)

        b_scr[...] = _shift_up(a, 1, a_first[...])
        d_scr[...] = dyap_v * gg_ref[...].astype(F32)
        row = _row_iota(D)

        def grp(jj, carry):
            r0 = pl.multiple_of((ngrp - 1 - jj) * SUBLANES, SUBLANES)
            bv = b_scr[pl.ds(r0, SUBLANES), :]
            dv = d_scr[pl.ds(r0, SUBLANES), :]
            for d in (1, 2, 4):
                m = row < SUBLANES - d
                dv = jnp.where(m, dv + bv * pltpu.roll(dv, SUBLANES - d, 0), dv)
                bv = jnp.where(m, bv * pltpu.roll(bv, SUBLANES - d, 0), bv)
            gv = dv + bv * carry
            g_scr[pl.ds(r0, SUBLANES), :] = gv
            return gv[0:1, :]

        lax.fori_loop(0, ngrp, grp, g_first[0:1, :])
        g = g_scr[...]
        a_first[...] = a[:SUBLANES]
        g_first[...] = g[:SUBLANES]

        da = g * h_prev
        gx = g * xc
        dmult = gx * ia
        dia = gx * mult
        dxc = g * (mult * ia)
        dla = da * a - dmult * (a * a) / mult
        dra = dla * (LRU_C * ls)
        dlam_ref[...] += _colsum(dla * ra) * (LRU_C * _sigmoid(-lam_v))
        dpa = dra * ra * (1.0 - ra)
        dpx = dia * ia * (1.0 - ia)
        dba_ref[...] += _colsum(dpa)
        dbx_ref[...] += _colsum(dpx)
        dpab = dpa.astype(BF16)
        dpxb = dpx.astype(BF16)
        xcb = xc_ref[...]
        for hd in range(NH):
            sl = slice(hd * HD, (hd + 1) * HD)
            dwa_ref[hd] += _dot_tn(xcb[:, sl], dpab[:, sl])
            dwx_ref[hd] += _dot_tn(xcb[:, sl], dpxb[:, sl])
        dxc = dxc + _heads_nt(dpab, wa_ref) + _heads_nt(dpxb, wx_ref)

        nxt = dxc_first[...]
        taps = (_shift_up(dxc, 3, nxt), _shift_up(dxc, 2, nxt), _shift_up(dxc, 1, nxt), dxc)
        dxr = cw_ref[0:1, :] * taps[0]
        for k in range(1, 4):
            dxr = dxr + cw_ref[k:k + 1, :] * taps[k]
        dxrb = dxr.astype(BF16)
        dz_ref[:, 0:D] = dxrb
        dwin_ref[:, 0:D] += _dot_tn(h1v, dxrb)
        dxc_first[...] = dxc[:SUBLANES]
        dcb_ref[...] += _colsum(dxc)
        xr = xr_ref[...].astype(F32)
        for k in range(4):
            dcw_ref[k:k + 1, :] += _colsum(taps[k] * xr)

    def rev(col):
        return lambda i: (nt - 1 - i, col)

    vec = _const_spec((1, D))
    wspec = _const_spec((NH, HD, HD))
    vshape = jax.ShapeDtypeStruct((1, D), F32)
    wshape = jax.ShapeDtypeStruct((NH, HD, HD), F32)
    any_spec = pl.BlockSpec(memory_space=pl.ANY)
    tile = pl.BlockSpec((tm, D), rev(0))
    outs = pl.pallas_call(
        body, name="rnn_bwd", grid=(nt,),
        in_specs=[tile, tile, _const_spec((D, D), True), tile, tile, tile, tile, tile, tile, tile, tile,
                  pl.BlockSpec((HALO, D), lambda i: (jnp.maximum((nt - 1 - i) * hpt - 1, 0), 0)),
                  any_spec, any_spec, _const_spec((4, D)), wspec, wspec, vec],
        out_specs=[pl.BlockSpec((tm, 2 * D), rev(0)), _acc_spec((D, 2 * D), (0, 0)), _acc_spec((D, D), (0, 0)),
                   _const_spec((4, D)), vec, wspec, vec, wspec, vec, vec],
        out_shape=[jax.ShapeDtypeStruct((t, NCOL_IN), BF16), jax.ShapeDtypeStruct((D, NCOL_IN), F32),
                   jax.ShapeDtypeStruct((D, D), F32), jax.ShapeDtypeStruct((4, D), F32), vshape,
                   wshape, vshape, wshape, vshape, vshape],
        scratch_shapes=[pltpu.VMEM((SUBLANES, D), F32), pltpu.VMEM((SUBLANES, D), F32), pltpu.VMEM((SUBLANES, D), F32),
                        pltpu.VMEM((tm, D), F32), pltpu.VMEM((tm, D), F32), pltpu.VMEM((tm, D), F32)],
        input_output_aliases={12: 0, 13: 1},
        compiler_params=_cparams(1),
    )(dya, ya_pre, wba, h1, z, *saved, h, h, dz, dwin, cw, wa, wx, lam)
    return outs


def _sgu_bwd(dyb, yb_pre, wbb, h1, saved, dz, dwin, lng, lnb, wmt, mask):
    t = dyb.shape[0]
    tm = _tile_big(t)

    def body(dyb_ref, ybp_ref, wbb_ref, h1_ref, gu_ref, mg_ref, vh_ref, gpv_ref, rstd_ref, dz_any, dwin_any,
             lng_ref, lnb_ref, wmt_ref, mask_ref,
             dz_ref, dwin_ref, dwbb_ref, dws_ref, dbst_ref, dlng_ref, dlnb_ref):
        del dz_any, dwin_any

        @pl.when(pl.program_id(0) == 0)
        def _():
            for r in (dwin_ref, dwbb_ref, dws_ref, dbst_ref, dlng_ref, dlnb_ref):
                r[...] = jnp.zeros_like(r)

        lng_v = lng_ref[...]
        vhat = vh_ref[...].astype(F32)
        vb = (vhat * lng_v + lnb_ref[...]).astype(BF16)
        rstd = rstd_ref[...]
        dyb_v = dyb_ref[...]
        dwbb_ref[...] += _dot_tn(ybp_ref[...], dyb_v)
        dyb = _dot_nt(dyb_v, wbb_ref[...])
        h1v = h1_ref[...]
        dzu = (dyb * mg_ref[...].astype(F32)).astype(BF16)
        dz_ref[:, 0:D] = dzu
        dwin_ref[:, 0:D] += _dot_tn(h1v, dzu)
        dmix = dyb * gu_ref[...].astype(F32)
        dmb = dmix.astype(BF16)
        rows = []
        lane = lax.broadcasted_iota(jnp.int32, (HD, NH), 1)
        dbst = jnp.zeros((HD, NH), F32)
        for b0 in range(0, tm, HD):
            cols = []
            for g in range(NH):
                sl = slice(g * HD, (g + 1) * HD)
                dmg = dmb[b0:b0 + HD, sl]
                dws_ref[g] += _dot_nt(dmg, vb[b0:b0 + HD, sl]) * mask_ref[...]
                cols.append(jnp.dot(wmt_ref[g], dmg, preferred_element_type=F32))
                dbst = dbst + jnp.where(lane == g, jnp.sum(dmix[b0:b0 + HD, sl], axis=1, keepdims=True), 0.0)
            rows.append(jnp.concatenate(cols, axis=1))
        dbst_ref[...] += dbst
        dvln = jnp.concatenate(rows, axis=0) if len(rows) > 1 else rows[0]
        dlng_ref[...] += _colsum(dvln * vhat)
        dlnb_ref[...] += _colsum(dvln)
        dvh = dvln * lng_v
        dgv = rstd * (dvh - jnp.mean(dvh, axis=-1, keepdims=True)
                      - vhat * jnp.mean(dvh * vhat, axis=-1, keepdims=True))
        dzv = (dgv * gpv_ref[...].astype(F32)).astype(BF16)
        dz_ref[:, D:2 * D] = dzv
        dwin_ref[:, D:2 * D] += _dot_tn(h1v, dzv)

    vec = _const_spec((1, D))
    wspec = _const_spec((NH, HD, HD))
    vshape = jax.ShapeDtypeStruct((1, D), F32)
    tile = pl.BlockSpec((tm, D), lambda i: (i, 0))
    any_spec = pl.BlockSpec(memory_space=pl.ANY)
    return pl.pallas_call(
        body, name="sgu_bwd", grid=(t // tm,),
        in_specs=[tile, tile, _const_spec((D, D), True), tile, tile, tile, tile, tile,
                  pl.BlockSpec((tm, 1), lambda i: (i, 0)), any_spec, any_spec,
                  vec, vec, wspec, _const_spec((HD, HD))],
        out_specs=[pl.BlockSpec((tm, 2 * D), lambda i: (i, 1)), _acc_spec((D, 2 * D), (0, 1)), _acc_spec((D, D), (0, 0)),
                   wspec, _const_spec((HD, NH)), vec, vec],
        out_shape=[jax.ShapeDtypeStruct((t, NCOL_IN), BF16), jax.ShapeDtypeStruct((D, NCOL_IN), F32),
                   jax.ShapeDtypeStruct((D, D), F32), jax.ShapeDtypeStruct((NH, HD, HD), F32),
                   jax.ShapeDtypeStruct((HD, NH), F32), vshape, vshape],
        input_output_aliases={9: 0, 10: 1},
        compiler_params=_cparams(1),
    )(dyb, yb_pre, wbb, h1, *saved, dz, dwin, lng, lnb, wmt, mask)


def _in_bwd(dz, win, x, dx2, g, scale1):
    t = x.shape[0]
    tm = _tile_big(t)

    def body(dz_ref, w_ref, x_ref, dx2_ref, g_ref, sc_ref, dx_ref, dsh_ref, dsc_ref, dg_ref):
        @pl.when(pl.program_id(0) == 0)
        def _():
            for r in (dsh_ref, dsc_ref, dg_ref):
                r[...] = jnp.zeros_like(r)

        dh = jnp.zeros((tm, D), F32)
        for c0 in range(0, NCOL_IN, D):
            dh = dh + _dot_nt(dz_ref[:, c0:c0 + D], w_ref[:, c0:c0 + D])
        dxn, dsh, dsc, dg = _modnorm_bwd(dh, x_ref[...], g_ref[...], sc_ref[...])
        dx_ref[...] = dx2_ref[...] + dxn
        dsh_ref[...] += dsh
        dsc_ref[...] += dsc
        dg_ref[...] += dg

    tile = pl.BlockSpec((tm, D), lambda i: (i, 0))
    vec = _const_spec((1, D))
    vshape = jax.ShapeDtypeStruct((1, D), F32)
    scale1, sc_spec = _vec_operand(scale1)
    return pl.pallas_call(
        body, name="in_bwd", grid=(t // tm,),
        in_specs=[pl.BlockSpec((tm, NCOL_IN), lambda i: (i, 0)), _const_spec((D, NCOL_IN), True), tile, tile, vec,
                  sc_spec],
        out_specs=[tile, vec, vec, vec],
        out_shape=[jax.ShapeDtypeStruct((t, D), F32), vshape, vshape, vshape],
        compiler_params=_cparams(1),
    )(dz, win, x, dx2, g, scale1)


def _mod_cols(c_all, w_ada, b_cols):
    nb, cols = c_all.shape[0], w_ada.shape[1]

    def body(c_ref, w_ref, b_ref, o_ref):
        cv = c_ref[...]
        ca = (cv * _sigmoid(cv)).astype(BF16)
        o_ref[...] = jnp.dot(ca, w_ref[...].astype(BF16), preferred_element_type=F32) + b_ref[...]

    return pl.pallas_call(body, name="mod_cols", out_shape=jax.ShapeDtypeStruct((nb, cols), F32))(c_all, w_ada, b_cols)


def _ada_grad(c_all, dmod_cols):
    cols = dmod_cols.shape[1]

    def body(c_ref, d_ref, o_ref):
        cv = c_ref[...]
        ca = (cv * _sigmoid(cv)).astype(BF16)
        o_ref[...] = _dot_tn(ca, d_ref[...].astype(BF16))

    return pl.pallas_call(body, name="ada_grad", out_shape=jax.ShapeDtypeStruct((D, cols), F32))(c_all, dmod_cols)


def _adamw_update(w, m, v, g):
    bc1 = 1.0 - ADAM_B1 ** ADAM_STEP
    bc2 = 1.0 - ADAM_B2 ** ADAM_STEP
    mn = ADAM_B1 * m + (1.0 - ADAM_B1) * g
    vn = ADAM_B2 * v + (1.0 - ADAM_B2) * (g * g)
    return -ADAM_LR * ((mn / bc1) / (jnp.sqrt(vn / bc2) + ADAM_EPS) + ADAM_WD * w), mn, vn


def _adamw_group(names, ws, ms, vs, packs, name):
    n = len(names)
    starts, r0 = [], 0
    for w in ws:
        starts.append(r0)
        r0 += _pack_rows(w.shape)

    def body(*refs):
        w_refs, m_refs, v_refs, p_ref = refs[:n], refs[n:2 * n], refs[2 * n:3 * n], refs[3 * n]
        outs = refs[3 * n + 1:]
        for k in range(n):
            rows = _pack_rows(ws[k].shape)
            g = None
            for dev in range(N_DEV):
                if ws[k].shape[0] == 1:
                    term = jnp.concatenate(
                        [p_ref[dev, starts[k] + r:starts[k] + r + 1, :] for r in range(rows)], axis=1)
                else:
                    term = p_ref[dev, starts[k]:starts[k] + rows, :]
                g = term if g is None else g + term
            delta, mn, vn = _adamw_update(w_refs[k][...], m_refs[k][...], v_refs[k][...], g)
            for o_ref, val in zip(outs[4 * k:4 * k + 4], (g, delta, mn, vn)):
                o_ref[...] = val

    shapes = [jax.ShapeDtypeStruct(w.shape, F32) for w in ws for _ in range(4)]
    outs = pl.pallas_call(body, name=name, out_shape=shapes,
                          compiler_params=pltpu.CompilerParams(vmem_limit_bytes=VMEM_LIMIT))(*ws, *ms, *vs, packs)
    return {nm: tuple(outs[4 * k:4 * k + 4]) for k, nm in enumerate(names)}


def _adamw(w, m, v, parts, name):
    rows, cols = w.shape
    tr = _row_tile(rows, cols)
    stacked = [p.ndim == 3 for p in parts]

    def body(*refs):
        w_ref, m_ref, v_ref = refs[:3]
        p_refs = refs[3:3 + len(parts)]
        g_ref, d_ref, mo_ref, vo_ref = refs[3 + len(parts):]
        g = None
        for p_ref, st in zip(p_refs, stacked):
            terms = [p_ref[k].astype(F32) for k in range(p_ref.shape[0])] if st else [p_ref[...].astype(F32)]
            for term in terms:
                g = term if g is None else g + term
        delta, mn, vn = _adamw_update(w_ref[...], m_ref[...], v_ref[...], g)
        g_ref[...] = g
        mo_ref[...] = mn
        vo_ref[...] = vn
        d_ref[...] = delta

    tile = pl.BlockSpec((tr, cols), lambda i: (i, 0))
    p_specs = [pl.BlockSpec((p.shape[0], tr, cols), lambda i: (0, i, 0)) if st else tile for p, st in zip(parts, stacked)]
    shp = jax.ShapeDtypeStruct((rows, cols), F32)
    return pl.pallas_call(
        body, name=name, grid=(rows // tr,),
        in_specs=[tile, tile, tile] + p_specs, out_specs=[tile] * 4, out_shape=[shp] * 4,
        compiler_params=_cparams(1),
    )(w, m, v, *parts)


def _mesh_pos():
    return lax.axis_index("x"), lax.axis_index("y"), lax.axis_index("c")


def _other_chips(x, y):
    return [(1 - x, y), (x, 1 - y), (1 - x, 1 - y)]


def _block_of(ref, axis, index, size):
    if axis == 0:
        return ref.at[index]
    return ref.at[:, pl.ds(pl.multiple_of(index * size, 128), size)]


def _all_gather(shards, axes, name):
    n = len(shards)
    per = 7

    def body(*refs):
        ins, outs, done = refs[:n], refs[n:2 * n], refs[2 * n]
        send_sems, recv_sems, local_sems = refs[2 * n + 1:]
        x, y, c = _mesh_pos()
        me, sibling = (x, y, c), (x, y, 1 - c)
        chips = _other_chips(x, y)

        def rows(a, pos):
            return _block_of(outs[a], axes[a], 4 * pos[0] + 2 * pos[1] + pos[2], shards[a].shape[-1])

        def copy(a, k, block, to, src=None):
            return pltpu.make_async_remote_copy(
                src_ref=rows(a, block) if src is None else src, dst_ref=rows(a, block),
                send_sem=send_sems.at[a * per + k], recv_sem=recv_sems.at[a * per + k],
                device_id=to, device_id_type=MESH_IDS)

        mine = [pltpu.make_async_copy(ins[a], rows(a, me), local_sems.at[a]) for a in range(n)]
        for cp in mine:
            cp.start()
        first = []
        for a in range(n):
            first.append(copy(a, 0, me, sibling, src=ins[a]))
            first += [copy(a, 1 + j, me, (*chip, c), src=ins[a]) for j, chip in enumerate(chips)]
        for cp in first:
            cp.start()
        passed = []
        for j, chip in enumerate(chips):
            for a in range(n):
                copy(a, 1 + j, (*chip, c), me).wait_recv()
                fwd = copy(a, 4 + j, (*chip, c), sibling)
                fwd.start()
                passed.append(fwd)
        for a in range(n):
            copy(a, 0, sibling, me).wait_recv()
            for j, chip in enumerate(chips):
                copy(a, 4 + j, (*chip, 1 - c), me).wait_recv()
        for cp in first + passed:
            cp.wait_send()
        for cp in mine:
            cp.wait()
        done[...] = jnp.zeros_like(done)

    def full_shape(s, ax):
        return (N_DEV,) + s.shape if ax == 0 else s.shape[:-1] + (N_DEV * s.shape[-1],)

    any_spec = pl.BlockSpec(memory_space=pl.ANY)
    outs = pl.pallas_call(
        body, name=name,
        in_specs=[any_spec] * n, out_specs=[any_spec] * n + [pl.BlockSpec(memory_space=pltpu.VMEM)],
        out_shape=[jax.ShapeDtypeStruct(full_shape(s, ax), s.dtype) for s, ax in zip(shards, axes)]
        + [jax.ShapeDtypeStruct((SUBLANES, LANES), F32)],
        scratch_shapes=[pltpu.SemaphoreType.DMA((n * per,)), pltpu.SemaphoreType.DMA((n * per,)),
                        pltpu.SemaphoreType.DMA((n,))],
    )(*shards)
    return outs[:n], outs[n]


def _chip_blocks(x, y):
    return [(x, y)] + _other_chips(x, y)


def _sibling_reduce(gs, axis, name):
    g0, n = gs[0], len(gs)
    rows, cols = (g0.shape[1], g0.shape[2]) if axis == 0 else (g0.shape[0], g0.shape[1] // N_DEV)
    chunk = math.gcd(rows, 64)

    def body(*refs):
        g_refs, own_refs, pay_refs = refs[:n], refs[n:2 * n], refs[2 * n:3 * n]
        send_buf, keep_buf, recv_buf, pay_buf, send_sems, recv_sems, stage_sems, keep_sems, out_sems = refs[3 * n:]
        x, y, c = _mesh_pos()
        sibling = (x, y, 1 - c)
        chips = _chip_blocks(x, y)
        stage, keep, push = [], [], []
        for a in range(n):
            for j, (px, py) in enumerate(chips):
                s = 4 * a + j
                theirs = _block_of(g_refs[a], axis, 4 * px + 2 * py + (1 - c), cols)
                ours = _block_of(g_refs[a], axis, 4 * px + 2 * py + c, cols)
                stage.append(pltpu.make_async_copy(theirs, send_buf.at[s], stage_sems.at[s]))
                keep.append(pltpu.make_async_copy(ours, keep_buf.at[s], keep_sems.at[s]))
                push.append(pltpu.make_async_remote_copy(
                    src_ref=send_buf.at[s], dst_ref=recv_buf.at[s], send_sem=send_sems.at[s],
                    recv_sem=recv_sems.at[s], device_id=sibling, device_id_type=MESH_IDS))
        for cp in stage + keep:
            cp.start()
        for s in range(4 * n):
            stage[s].wait()
            push[s].start()
        written = []
        for s in range(4 * n):
            push[s].wait_recv()
            keep[s].wait()
            a, j = divmod(s, 4)
            res = keep_buf.at[s] if j == 0 else pay_buf.at[3 * a + j - 1]

            def add(r, carry, s=s, res=res):
                sl = pl.ds(pl.multiple_of(r * chunk, chunk), chunk)
                res[sl, :] = (keep_buf[s, sl, :] + recv_buf[s, sl, :]).astype(res.dtype)
                return carry

            lax.fori_loop(0, rows // chunk, add, 0)
            out = pltpu.make_async_copy(res, own_refs[a] if j == 0 else pay_refs[a].at[j - 1], out_sems.at[s])
            out.start()
            written.append(out)
        for cp in push:
            cp.wait_send()
        for cp in written:
            cp.wait()

    any_spec = pl.BlockSpec(memory_space=pl.ANY)
    buf = pltpu.VMEM((4 * n, rows, cols), F32)
    sems = pltpu.SemaphoreType.DMA((4 * n,))
    outs = pl.pallas_call(
        body, name=name,
        in_specs=[any_spec] * n, out_specs=[any_spec] * (2 * n),
        out_shape=[jax.ShapeDtypeStruct((rows, cols), F32)] * n + [jax.ShapeDtypeStruct((3, rows, cols), BF16)] * n,
        scratch_shapes=[buf, buf, buf, pltpu.VMEM((3 * n, rows, cols), BF16), sems, sems, sems, sems, sems],
        compiler_params=pltpu.CompilerParams(vmem_limit_bytes=VMEM_LIMIT),
    )(*gs)
    return list(zip(outs[:n], outs[n:]))


_HBM_SPEC = pl.BlockSpec(memory_space=pltpu.HBM)
_SEM_SPEC = pl.BlockSpec(memory_space=pltpu.SEMAPHORE)
_SIDE_EFFECT = pltpu.SideEffectType.DATAFLOW_SIDE_EFFECTING


def _exchange_start(name, srcs, lands, plan, n_copies):
    nb = len(srcs) + len(lands)

    def body(*refs):
        bufs, send_sems, recv_sems, token = refs[:nb], refs[nb], refs[nb + 1], refs[-1]
        for cp in plan(bufs[:len(srcs)], bufs[len(srcs):], send_sems, recv_sems):
            cp.start()
        token[...] = jnp.zeros_like(token)

    arrays = list(srcs) + list(lands)
    outs = pl.pallas_call(
        body, name=name,
        out_shape=(pltpu.SemaphoreType.DMA((n_copies,)), pltpu.SemaphoreType.DMA((n_copies,)),
                   *[pltpu.HBM(a.shape, a.dtype) for a in arrays], jax.ShapeDtypeStruct((SUBLANES, LANES), F32)),
        in_specs=[_HBM_SPEC] * nb,
        out_specs=(_SEM_SPEC, _SEM_SPEC, *[_HBM_SPEC] * nb, pl.BlockSpec(memory_space=pltpu.VMEM)),
        input_output_aliases={k: 2 + k for k in range(nb)},
        compiler_params=pltpu.CompilerParams(has_side_effects=_SIDE_EFFECT),
    )(*[pltpu.with_memory_space_constraint(a, pltpu.HBM) for a in arrays])
    return outs[0], outs[1], outs[2:2 + len(srcs)], outs[2 + len(srcs):2 + nb], outs[-1]


def _exchange_wait(name, send_sems, recv_sems, srcs, lands, plan, after):
    nb = len(srcs) + len(lands)
    after = list(after)

    def body(*refs):
        bufs, send_ref, recv_ref = refs[:nb], refs[nb], refs[nb + 1]
        for cp in plan(bufs[:len(srcs)], bufs[len(srcs):], send_ref, recv_ref):
            cp.wait_send()
            cp.wait_recv()

    arrays = list(srcs) + list(lands)
    outs = pl.pallas_call(
        body, name=name,
        out_shape=tuple(pltpu.HBM(a.shape, a.dtype) for a in arrays),
        in_specs=[_HBM_SPEC] * nb + [_SEM_SPEC, _SEM_SPEC] + [pl.BlockSpec(memory_space=pl.ANY)] * len(after),
        out_specs=tuple([_HBM_SPEC] * nb),
        input_output_aliases={k: k for k in range(nb)},
        compiler_params=pltpu.CompilerParams(has_side_effects=_SIDE_EFFECT),
    )(*arrays, send_sems, recv_sems, *after)
    return outs[len(srcs):]


def _gather_plan(axes, sizes):
    def plan(src_refs, land_refs, send_sems, recv_sems):
        x, y, c = _mesh_pos()
        copies = []
        for a, (src, land) in enumerate(zip(src_refs, land_refs)):
            mine = _block_of(land, axes[a], 4 * x + 2 * y + c, sizes[a])
            for k in range(1, N_DEV):
                peer = (1 - x if k & 4 else x, 1 - y if k & 2 else y, 1 - c if k & 1 else c)
                idx = a * (N_DEV - 1) + k - 1
                copies.append(pltpu.make_async_remote_copy(
                    src_ref=src, dst_ref=mine, send_sem=send_sems.at[idx], recv_sem=recv_sems.at[idx],
                    device_id=peer, device_id_type=MESH_IDS))
        return copies
    return plan


def _chip_plan(src_refs, land_refs, send_sems, recv_sems):
    x, y, c = _mesh_pos()
    copies = []
    for a, (src, land) in enumerate(zip(src_refs, land_refs)):
        for j, chip in enumerate(_other_chips(x, y)):
            copies.append(pltpu.make_async_remote_copy(
                src_ref=src.at[j], dst_ref=land.at[j], send_sem=send_sems.at[3 * a + j],
                recv_sem=recv_sems.at[3 * a + j], device_id=(*chip, c), device_id_type=MESH_IDS))
    return copies


def _own_block_placed(shard, axis, me):
    if axis == 0:
        full = lax.empty((N_DEV,) + shard.shape, shard.dtype)
        return lax.dynamic_update_slice(full, shard[None], (me,) + (0,) * shard.ndim)
    rows, cols = shard.shape

    def body(me_ref, s_ref, o_ref):
        del me_ref
        o_ref[...] = s_ref[...]

    return pl.pallas_call(
        body, name="place_own_columns",
        grid_spec=pltpu.PrefetchScalarGridSpec(
            num_scalar_prefetch=1, grid=(1,),
            in_specs=[pl.BlockSpec((rows, cols), lambda i, me_ref: (0, 0))],
            out_specs=pl.BlockSpec((rows, cols), lambda i, me_ref: (0, me_ref[0]))),
        out_shape=jax.ShapeDtypeStruct((rows, N_DEV * cols), shard.dtype),
    )(jnp.reshape(me, (1,)).astype(jnp.int32), shard)


def _local_step(x, target, mod, win, late_weights, p, grads_ready=None):
    shift1, scale1, gate1, shift2, scale2, gate2 = ((mod, k) for k in range(6))

    def after_token(v, token):
        return v if token is None else v + token[0:1, 0:1]
    wa, wx = p["lru_w_a"].astype(BF16), p["lru_w_x"].astype(BF16)
    mask = jnp.tril(jnp.ones((HD, HD), F32))
    wm = (p["sgu_w_s"] * mask).astype(BF16)
    wmt = jnp.swapaxes(wm, 1, 2)
    bst = jnp.transpose(p["sgu_b_s"])

    h1, z = _norm_proj(x, p["norm_mix_g"], scale1, shift1, win, "mix_proj")
    hstate, ya_pre, *rnn_saved = _rnn_fwd(
        z, p["rnn_conv_w"], p["rnn_conv_b"], wa, p["lru_b_a"], wx, p["lru_b_x"], p["lru_lambda"])
    yb_pre, *sgu_saved = _sgu_fwd(z, p["sgu_ln_g"], p["sgu_ln_b"], wm, bst)
    wba, wbb, wout = late_weights("merge", [ya_pre, yb_pre])
    x2, ya, yb, merged, o1 = _merge_fwd(ya_pre, yb_pre, z, x, gate1, wba, wbb, wout)
    wup = late_weights("ffn_up", [x2])
    h2, up_a, up_v, ff, fa, fv = _ffn_proj_mid(
        x2, p["norm_ffn_g"], scale2, shift2, wup, p["ffn_conv_w"], p["ffn_conv_b"])
    wd = late_weights("ffn_down", [ff])
    dx3, do2, loss, d_gfin, d_gate2 = _ffn_out_loss(ff, wd, x2, target, gate2, p["norm_final_g"])

    dact, dval, d_wd, dcb_a, dcb_v = _ffn_down_bwd(do2, ff, fa, fv, wd)
    dup, dx2, do1, d_cwf, d_shift2, d_scale2, d_gffn, d_gate1 = _ffn_up_bwd(
        dact, dval, up_a, up_v, p["ffn_conv_w"], wup, x2, dx3, p["norm_ffn_g"], scale2, o1, gate1)
    d_wup = _xt_y(h2, dup, "w_up_grad")
    ready = grads_ready if grads_ready else (lambda stage, big, small: None)
    token = ready("ffn", {"w_up": d_wup, "w_down": d_wd}, {})

    dya, dyb, dz, d_wout, d_win = _out_bwd(do1, wout, merged, ya, yb, z, h1)
    dz, d_win, d_wba, d_cw, d_cb, d_wa, d_ba, d_wx, d_bx, d_lam = _rnn_bwd(
        dya, ya_pre, wba, h1, z, rnn_saved, hstate, dz, d_win, p["rnn_conv_w"], wa, wx,
        after_token(p["lru_lambda"], token))
    small = {
        "rnn_conv_w": d_cw, "rnn_conv_b": d_cb, "lru_w_a": d_wa, "lru_b_a": d_ba, "lru_w_x": d_wx, "lru_b_x": d_bx,
        "lru_lambda": d_lam, "norm_ffn_g": d_gffn, "ffn_conv_w": d_cwf,
        "ffn_conv_b": jnp.concatenate([dcb_a, dcb_v], axis=1), "norm_final_g": d_gfin,
    }
    token = ready("rnn", {}, small)
    dz, d_win, d_wbb, d_ws, d_bst, d_lng, d_lnb = _sgu_bwd(
        dyb, yb_pre, wbb, h1, sgu_saved, dz, d_win, p["sgu_ln_g"], after_token(p["sgu_ln_b"], token), wmt, mask)
    sgu_small = {"sgu_ln_g": d_lng, "sgu_ln_b": d_lnb, "sgu_w_s": d_ws, "sgu_b_s": jnp.transpose(d_bst)}
    mixer = {"w_in": d_win, "w_out": d_wout, "w_branch_a": d_wba, "w_branch_b": d_wbb}
    token = ready("mixer", mixer, sgu_small)
    grad_x, d_shift1, d_scale1, d_gmix = _in_bwd(dz, win, x, dx2, after_token(p["norm_mix_g"], token), scale1)

    small.update(sgu_small)
    small["norm_mix_g"] = d_gmix
    dmod = jnp.stack([d_shift1, d_scale1, d_gate1, d_shift2, d_scale2, d_gate2])
    big = {"w_in": d_win, "w_up": d_wup, "w_branch_a": d_wba, "w_branch_b": d_wbb, "w_out": d_wout, "w_down": d_wd}
    return loss, grad_x, big, small, dmod


LAST_REP = ["b_ada", "norm_mix_g"]
EARLY_REP = ["rnn_conv_b", "lru_w_a", "lru_b_a", "lru_w_x", "lru_b_x", "lru_lambda", "norm_ffn_g", "ffn_conv_b",
             "norm_final_g"]
MID_REP = ["sgu_ln_g", "sgu_ln_b", "sgu_w_s", "sgu_b_s"]
COL_SHARDED = ["rnn_conv_w", "ffn_conv_w"]
SMALL_GROUPS = {"rnn": EARLY_REP + COL_SHARDED, "mixer": MID_REP, "last": LAST_REP}
REPLICATED = LAST_REP + EARLY_REP + MID_REP
SMALL_NAMES = REPLICATED + COL_SHARDED
BIG_NAMES = ["w_in", "w_up", "w_branch_a", "w_branch_b", "w_out", "w_down"]
BIG_AXES = [1, 1, 0, 0, 0, 0]
WEIGHTS = ["w_ada", "b_ada", "norm_mix_g", "w_in", "rnn_conv_w", "rnn_conv_b", "lru_w_a", "lru_b_a", "lru_w_x",
           "lru_b_x", "lru_lambda", "sgu_ln_g", "sgu_ln_b", "sgu_w_s", "sgu_b_s", "w_branch_a", "w_branch_b",
           "w_out", "norm_ffn_g", "w_up", "ffn_conv_w", "ffn_conv_b", "w_down", "norm_final_g"]


def _pack_rows(shape):
    return math.prod(shape) // LANES


def _pack(arrays):
    return jnp.concatenate([a.reshape(-1, LANES) for a in arrays], axis=0)


def kernel(x, c, w_ada, b_ada, norm_mix_g, w_in, rnn_conv_w, rnn_conv_b, lru_w_a, lru_b_a, lru_w_x, lru_b_x, lru_lambda, sgu_ln_g, sgu_ln_b, sgu_w_s, sgu_b_s, w_branch_a, w_branch_b, w_out, norm_ffn_g, w_up, ffn_conv_w, ffn_conv_b, w_down, norm_final_g, loss_target, m_w_ada, m_b_ada, m_norm_mix_g, m_w_in, m_rnn_conv_w, m_rnn_conv_b, m_lru_w_a, m_lru_b_a, m_lru_w_x, m_lru_b_x, m_lru_lambda, m_sgu_ln_g, m_sgu_ln_b, m_sgu_w_s, m_sgu_b_s, m_w_branch_a, m_w_branch_b, m_w_out, m_norm_ffn_g, m_w_up, m_ffn_conv_w, m_ffn_conv_b, m_w_down, m_norm_final_g, v_w_ada, v_b_ada, v_norm_mix_g, v_w_in, v_rnn_conv_w, v_rnn_conv_b, v_lru_w_a, v_lru_b_a, v_lru_w_x, v_lru_b_x, v_lru_lambda, v_sgu_ln_g, v_sgu_ln_b, v_sgu_w_s, v_sgu_b_s, v_w_branch_a, v_w_branch_b, v_w_out, v_norm_ffn_g, v_w_up, v_ffn_conv_w, v_ffn_conv_b, v_w_down, v_norm_final_g):
    given = dict(locals())
    me = 4 * lax.axis_index("x") + 2 * lax.axis_index("y") + lax.axis_index("c")
    ada_cols = w_ada.shape[2]
    conv_cols = {"rnn_conv_w": rnn_conv_w.shape[2], "ffn_conv_w": ffn_conv_w.shape[2]}

    (win, c_all, cw_rnn, cw_ffn), _ = _all_gather(
        [w_in[0].astype(BF16), c.reshape(1, 1, D), rnn_conv_w[0], ffn_conv_w[0]], [1, 0, 1, 1], "gather_first")
    c_all = c_all.reshape(N_DEV, D)

    b_cols = lax.dynamic_slice_in_dim(b_ada, me * ada_cols, ada_cols, axis=1)
    (mod_all,), mod_done = _all_gather(
        [_mod_cols(c_all, w_ada[0], b_cols).reshape(1, N_DEV, ada_cols)], [0], "gather_mod")
    mod_all = mod_all.reshape(N_DEV, N_DEV, ada_cols)
    mod_mine = lax.dynamic_index_in_dim(mod_all, me, axis=1, keepdims=False).reshape(6, 1, D)

    late_groups = {"merge": (["w_branch_a", "w_branch_b", "w_out"], [0, 0, 0]), "ffn_up": (["w_up"], [1]),
                   "ffn_down": (["w_down"], [0])}
    in_flight, started = {}, mod_done[0:1, 0:1]
    for stage, (names, axes) in late_groups.items():
        shards = [(given[n][0] + started).astype(BF16) for n in names]
        plan = _gather_plan(axes, [s.shape[-1] for s in shards])
        send, recv, srcs, lands, token = _exchange_start(
            "gather_start_" + stage, shards, [_own_block_placed(s, ax, me) for s, ax in zip(shards, axes)], plan,
            len(shards) * (N_DEV - 1))
        in_flight[stage] = (send, recv, srcs, lands, plan)
        started = started + token[0:1, 0:1]

    def late_weights(stage, after):
        send, recv, srcs, lands, plan = in_flight[stage]
        full = _exchange_wait("gather_wait_" + stage, send, recv, srcs, lands, plan, after)
        full = [w.reshape(-1, D) if ax == 0 else w for w, ax in zip(full, late_groups[stage][1])]
        return full if len(full) > 1 else full[0]

    mod_mine = mod_mine + started

    reducing, packing = {}, {}

    def start_pack(stage, small):
        pack = _pack([small[n] for n in SMALL_GROUPS[stage]])[None]
        plan = _gather_plan([0], [LANES])
        send, recv, srcs, lands, tok = _exchange_start(
            "small_start_" + stage, [pack], [_own_block_placed(pack, 0, me)], plan, N_DEV - 1)
        packing[stage] = (send, recv, srcs, lands, plan)
        return tok

    def grads_ready(stage, grads, small):
        tokens = [start_pack(stage, small)] if small else []
        if grads:
            tokens.append(start_reduce(stage, grads))
        return sum(tokens[1:], tokens[0])

    def start_reduce(stage, grads):
        names = [n for n in BIG_NAMES if n in grads]
        blocked = {}
        for n in names:
            ax = BIG_AXES[BIG_NAMES.index(n)]
            g = grads[n] if ax == 1 else grads[n].reshape(N_DEV, grads[n].shape[0] // N_DEV, grads[n].shape[1])
            blocked.setdefault((ax, g.shape), []).append((n, g))
        sums = {}
        for (ax, _), group in blocked.items():
            reduced = _sibling_reduce([g for _, g in group], ax, "reduce_sibling_" + "_".join(n for n, _ in group))
            sums.update({n: r for (n, _), r in zip(group, reduced)})
        sums = [sums[n] for n in names]
        pays = [pay for _, pay in sums]
        send, recv, srcs, lands, tok = _exchange_start(
            "reduce_start_" + stage, pays, [lax.empty(p_.shape, p_.dtype) for p_ in pays], _chip_plan, 3 * len(pays))
        reducing[stage] = (names, [own for own, _ in sums], send, recv, srcs, lands)
        return tok

    p = {n: given[n][0] for n in REPLICATED if n not in ("b_ada", "norm_final_g")}
    p = {n: (a.reshape(1, -1) if a.ndim == 1 else a) for n, a in p.items()}
    p["rnn_conv_w"], p["ffn_conv_w"] = cw_rnn, cw_ffn
    p["norm_final_g"] = norm_final_g.reshape(1, D)
    loss, grad_x, _, small, dmod = _local_step(x[0], loss_target[0], mod_mine, win, late_weights, p, grads_ready)

    small["b_ada"] = dmod.reshape(1, 6 * D)
    rows_of = {n: _pack_rows(small[n].shape) for n in SMALL_NAMES}
    (last,), _ = _all_gather([_pack([small[n] for n in LAST_REP])[None]], [0], "gather_small")
    gathered = {"last": last}
    for stage, (send, recv, srcs, lands, plan) in packing.items():
        (gathered[stage],) = _exchange_wait("small_wait_" + stage, send, recv, srcs, lands, plan, [grad_x])
    gathered = {k: v.reshape(N_DEV, -1, LANES) for k, v in gathered.items()}

    out = {}
    for stage, (names, owns, send, recv, srcs, lands) in reducing.items():
        landed = _exchange_wait("reduce_wait_" + stage, send, recv, srcs, lands, _chip_plan, [last])
        for n, own, got in zip(names, owns, landed):
            out[n] = _adamw(given[n][0], given["m_" + n][0], given["v_" + n][0], [own, got], "adamw_" + n)

    dmod_all = gathered["last"][:, :rows_of["b_ada"]].reshape(N_DEV, 6 * D)
    dmod_cols = lax.dynamic_slice_in_dim(dmod_all, me * ada_cols, ada_cols, axis=1)
    out["w_ada"] = _adamw(w_ada[0], m_w_ada[0], v_w_ada[0], [_ada_grad(c_all, dmod_cols)], "adamw_w_ada")

    def rows_form(a):
        return a.reshape(1, -1) if a.size // a.shape[-1] == 1 or a.ndim == 1 else a.reshape(-1, LANES)

    for stage, names in (("last", LAST_REP), ("rnn", EARLY_REP), ("mixer", MID_REP)):
        out.update(_adamw_group(names, *[[rows_form(given[pre + n]) for n in names] for pre in ("", "m_", "v_")],
                                gathered[stage], "adamw_small_" + stage))

    row0 = sum(rows_of[n] for n in EARLY_REP)
    for n in COL_SHARDED:
        full = gathered["rnn"][:, row0:row0 + rows_of[n]].reshape(N_DEV, small[n].shape[0], small[n].shape[1])
        mine = lax.dynamic_slice_in_dim(full, me * conv_cols[n], conv_cols[n], axis=2)
        out[n] = _adamw(given[n][0], given["m_" + n][0], given["v_" + n][0], [mine], "adamw_" + n)
        row0 += rows_of[n]

    total = lax.psum(loss[0, 0], ("x", "y", "c"))
    results = [total, grad_x[None]]
    for kind in range(4):
        results += [out[n][kind].reshape(given[n].shape) for n in WEIGHTS]
    return tuple(results)
```

```python
import math

import jax
import jax.numpy as jnp
from jax import lax
from jax.experimental import pallas as pl
from jax.experimental.pallas import tpu as pltpu

F32 = jnp.float32
BF16 = jnp.bfloat16
MESH_IDS = pl.DeviceIdType.MESH

D = 1024
NH = 8
HD = 128
NCOL_IN = 6 * D
DFF = 3 * D
N_DEV = 8
EPS = 1e-6
LRU_C = 8.0
ADAM_LR, ADAM_B1, ADAM_B2, ADAM_EPS, ADAM_WD, ADAM_STEP = 0.001, 0.9, 0.999, 1e-08, 0.01, 10

SUBLANES = 8
LANES = 128
HALO = 16
VMEM_LIMIT = 56 * 1024 * 1024
GELU_K = math.sqrt(2.0 / math.pi)
GELU_C = 0.044715


def _cparams(n_axes):
    return pltpu.CompilerParams(dimension_semantics=("arbitrary",) * n_axes, vmem_limit_bytes=VMEM_LIMIT)


def _const_spec(shape, single_buffer=False):
    nd = len(shape)
    if single_buffer:
        return pl.BlockSpec(shape, lambda *_: (0,) * nd, pipeline_mode=pl.Buffered(1))
    return pl.BlockSpec(shape, lambda *_: (0,) * nd)


def _vec_operand(v):
    if isinstance(v, tuple):
        stack, k = v
        return stack, pl.BlockSpec((None, 1, D), lambda *_: (k, 0, 0))
    return v, _const_spec((1, D))


def _tile_big(t):
    return min(512, t)


def _tile_seq(t):
    return min(256, t)


def _row_tile(rows, cols):
    cap = max(SUBLANES, (2 * 1024 * 1024) // (4 * cols) // SUBLANES * SUBLANES)
    if rows <= cap:
        return rows
    return next(tr for tr in range(cap, 0, -SUBLANES) if rows % tr == 0)


def _gelu_t(x):
    x2 = x * x
    t = jnp.tanh(x * (GELU_K + (GELU_K * GELU_C) * x2))
    hx = 0.5 * x
    return hx + hx * t, (x2, hx, t)


def _gelu_grad(shared):
    x2, hx, t = shared
    return (0.5 + 0.5 * t) + (hx * (1.0 - t * t)) * (GELU_K + (3.0 * GELU_K * GELU_C) * x2)


def _sigmoid(x):
    return 1.0 / (1.0 + jnp.exp(-x))


def _log_sigmoid(x):
    return -(jnp.maximum(-x, 0.0) + jnp.log1p(jnp.exp(-jnp.abs(x))))


def _row_iota(cols):
    return lax.broadcasted_iota(jnp.int32, (SUBLANES, cols), 0)


def _shift_down(x, k, prev8):
    if k == 0:
        return x
    r = pltpu.roll(x, k, 0)
    p = pltpu.roll(prev8, k, 0)
    head = jnp.where(_row_iota(x.shape[1]) < k, p, r[:SUBLANES])
    return jnp.concatenate([head, r[SUBLANES:]], axis=0)


def _shift_up(x, k, next8):
    if k == 0:
        return x
    n = x.shape[0]
    r = pltpu.roll(x, n - k, 0)
    q = pltpu.roll(next8, SUBLANES - k, 0)
    tail = jnp.where(_row_iota(x.shape[1]) >= SUBLANES - k, q, r[n - SUBLANES:])
    return jnp.concatenate([r[:n - SUBLANES], tail], axis=0)


def _heads_nn(x_bf, w_ref):
    return jnp.concatenate(
        [jnp.dot(x_bf[:, h * HD:(h + 1) * HD], w_ref[h], preferred_element_type=F32) for h in range(NH)], axis=1)


def _heads_nt(x_bf, w_ref):
    return jnp.concatenate(
        [lax.dot_general(x_bf[:, h * HD:(h + 1) * HD], w_ref[h], (((1,), (1,)), ((), ())), preferred_element_type=F32)
         for h in range(NH)], axis=1)


def _dot_nt(a, b):
    return lax.dot_general(a, b, (((1,), (1,)), ((), ())), preferred_element_type=F32)


def _dot_tn(a, b):
    return lax.dot_general(a, b, (((0,), (0,)), ((), ())), preferred_element_type=F32)


def _colsum(x):
    return jnp.sum(x, axis=0, keepdims=True)


def _prev_halo_map(tm, col):
    return lambda i, *_: (jnp.maximum(i * (tm // HALO) - 1, 0), col)


def _norm_proj(x, g, scale, shift, w, name):
    t, n = x.shape[0], w.shape[1]
    tm = _tile_big(t)

    def body(x_ref, g_ref, sc_ref, sh_ref, w_ref, h_ref, z_ref):
        xv = x_ref[...]
        r = lax.rsqrt(jnp.mean(xv * xv, axis=-1, keepdims=True) + EPS)
        hb = ((xv * r * g_ref[...]) * (1.0 + sc_ref[...]) + sh_ref[...]).astype(BF16)
        h_ref[...] = hb
        for c0 in range(0, n, D):
            z_ref[:, c0:c0 + D] = jnp.dot(hb, w_ref[:, c0:c0 + D], preferred_element_type=F32).astype(BF16)

    vec = _const_spec((1, D))
    (scale, sc_spec), (shift, sh_spec) = _vec_operand(scale), _vec_operand(shift)
    return pl.pallas_call(
        body, name=name, grid=(t // tm,),
        in_specs=[pl.BlockSpec((tm, D), lambda i: (i, 0)), vec, sc_spec, sh_spec, _const_spec((D, n), True)],
        out_specs=[pl.BlockSpec((tm, D), lambda i: (i, 0)), pl.BlockSpec((tm, n), lambda i: (i, 0))],
        out_shape=[jax.ShapeDtypeStruct((t, D), BF16), jax.ShapeDtypeStruct((t, n), BF16)],
        compiler_params=_cparams(1),
    )(x, g, scale, shift, w)


def _lru_gates(xc, wa_ref, ba, wx_ref, bx, ls):
    xb = xc.astype(BF16)
    ra = _sigmoid(_heads_nn(xb, wa_ref) + ba)
    ia = _sigmoid(_heads_nn(xb, wx_ref) + bx)
    la = LRU_C * ra * ls
    a = jnp.exp(la)
    mult = jnp.sqrt(-jnp.tanh(la) * (1.0 + a * a))
    return ra, ia, a, mult


def _conv4(xr, prev8, cw_ref, cb):
    return (cb + cw_ref[3:4, :] * xr + cw_ref[2:3, :] * _shift_down(xr, 1, prev8)
            + cw_ref[1:2, :] * _shift_down(xr, 2, prev8) + cw_ref[0:1, :] * _shift_down(xr, 3, prev8))


def _rnn_fwd(z, cw, cb, wa, ba, wx, bx, lam):
    t = z.shape[0]
    tm = _tile_seq(t)
    ngrp = tm // SUBLANES

    def body(xr_ref, xp_ref, gr_ref, cw_ref, cb_ref, wa_ref, ba_ref, wx_ref, bx_ref, lam_ref,
             h_ref, ya_ref, xc_ref, ra_ref, ia_ref, gg_ref, hg_ref, carry_ref, a_scr, u_scr):
        i = pl.program_id(0)

        @pl.when(i == 0)
        def _():
            carry_ref[...] = jnp.zeros_like(carry_ref)

        xr = xr_ref[...].astype(F32)
        prev8 = jnp.where(i == 0, 0.0, xp_ref[...].astype(F32)[HALO - SUBLANES:])
        xc = _conv4(xr, prev8, cw_ref, cb_ref[...])
        ra, ia, a, mult = _lru_gates(xc, wa_ref, ba_ref[...], wx_ref, bx_ref[...], _log_sigmoid(lam_ref[...]))
        xc_ref[...] = xc.astype(BF16)
        ra_ref[...] = ra.astype(BF16)
        ia_ref[...] = ia.astype(BF16)
        a_scr[...] = a
        u_scr[...] = mult * (ia * xc)
        row = _row_iota(D)

        def grp(j, carry):
            r0 = pl.multiple_of(j * SUBLANES, SUBLANES)
            av = a_scr[pl.ds(r0, SUBLANES), :]
            uv = u_scr[pl.ds(r0, SUBLANES), :]
            for d in (1, 2, 4):
                m = row >= d
                uv = jnp.where(m, av * pltpu.roll(uv, d, 0) + uv, uv)
                av = jnp.where(m, av * pltpu.roll(av, d, 0), av)
            hv = uv + av * carry
            h_ref[pl.ds(r0, SUBLANES), :] = hv
            return hv[SUBLANES - 1:SUBLANES, :]

        carry_ref[0:1, :] = lax.fori_loop(0, ngrp, grp, carry_ref[0:1, :])
        grv = gr_ref[...].astype(F32)
        gg, tg = _gelu_t(grv)
        hv = h_ref[...]
        ya_ref[...] = (hv * gg).astype(BF16)
        gg_ref[...] = gg.astype(BF16)
        hg_ref[...] = (hv * _gelu_grad(tg)).astype(BF16)

    vec = _const_spec((1, D))
    wspec = _const_spec((NH, HD, HD))
    tile = pl.BlockSpec((tm, D), lambda i: (i, 0))
    bshape = jax.ShapeDtypeStruct((t, D), BF16)
    return pl.pallas_call(
        body, name="rnn_fwd", grid=(t // tm,),
        in_specs=[tile, pl.BlockSpec((HALO, D), _prev_halo_map(tm, 0)),
                  pl.BlockSpec((tm, D), lambda i: (i, 1)), _const_spec((4, D)), vec, wspec, vec, wspec, vec, vec],
        out_specs=[tile] * 7,
        out_shape=[jax.ShapeDtypeStruct((t, D), F32)] + [bshape] * 6,
        scratch_shapes=[pltpu.VMEM((SUBLANES, D), F32), pltpu.VMEM((tm, D), F32), pltpu.VMEM((tm, D), F32)],
        compiler_params=_cparams(1),
    )(z, z, z, cw, cb, wa, ba, wx, bx, lam)


def _sgu_fwd(z, lng, lnb, wm, bst):
    t = z.shape[0]
    tm = _tile_seq(t)

    def body(zu_ref, zv_ref, lng_ref, lnb_ref, wm_ref, bst_ref, yb_ref, gu_ref, mg_ref, vh_ref, gpv_ref, rstd_ref):
        gu, su = _gelu_t(zu_ref[...].astype(F32))
        gv, sv = _gelu_t(zv_ref[...].astype(F32))
        mu = jnp.mean(gv, axis=-1, keepdims=True)
        cen = gv - mu
        rstd = lax.rsqrt(jnp.mean(cen * cen, axis=-1, keepdims=True) + EPS)
        vhat = cen * rstd
        vb = (vhat * lng_ref[...] + lnb_ref[...]).astype(BF16)
        rows = []
        for b0 in range(0, tm, HD):
            rows.append(jnp.concatenate(
                [jnp.dot(wm_ref[g], vb[b0:b0 + HD, g * HD:(g + 1) * HD], preferred_element_type=F32)
                 + bst_ref[:, g:g + 1] for g in range(NH)], axis=1))
        mixed = jnp.concatenate(rows, axis=0) if len(rows) > 1 else rows[0]
        yb_ref[...] = (gu * mixed).astype(BF16)
        gu_ref[...] = gu.astype(BF16)
        mg_ref[...] = (mixed * _gelu_grad(su)).astype(BF16)
        vh_ref[...] = vhat.astype(BF16)
        gpv_ref[...] = _gelu_grad(sv).astype(BF16)
        rstd_ref[...] = rstd

    vec = _const_spec((1, D))
    tile = pl.BlockSpec((tm, D), lambda i: (i, 0))
    bshape = jax.ShapeDtypeStruct((t, D), BF16)
    return pl.pallas_call(
        body, name="sgu_fwd", grid=(t // tm,),
        in_specs=[pl.BlockSpec((tm, D), lambda i: (i, 2)), pl.BlockSpec((tm, D), lambda i: (i, 3)), vec, vec,
                  _const_spec((NH, HD, HD)), _const_spec((HD, NH))],
        out_specs=[tile] * 5 + [pl.BlockSpec((tm, 1), lambda i: (i, 0))],
        out_shape=[bshape] * 5 + [jax.ShapeDtypeStruct((t, 1), F32)],
        compiler_params=_cparams(1),
    )(z, z, lng, lnb, wm, bst)


def _merge_fwd(ya_pre, yb_pre, z, x, gate1, wba, wbb, wout):
    t = x.shape[0]
    tm = _tile_big(t)

    def body(yap_ref, ybp_ref, ga_ref, gb_ref, x_ref, g1_ref, wba_ref, wbb_ref, wo_ref,
             x2_ref, ya_ref, yb_ref, mg_ref, o1_ref):
        ya = jnp.dot(yap_ref[...], wba_ref[...], preferred_element_type=F32)
        yb = jnp.dot(ybp_ref[...], wbb_ref[...], preferred_element_type=F32)
        merged = _sigmoid(ga_ref[...].astype(F32)) * ya + _sigmoid(gb_ref[...].astype(F32)) * yb
        mb = merged.astype(BF16)
        o1 = jnp.dot(mb, wo_ref[...], preferred_element_type=F32)
        x2_ref[...] = x_ref[...] + g1_ref[...] * o1
        ya_ref[...] = ya.astype(BF16)
        yb_ref[...] = yb.astype(BF16)
        mg_ref[...] = mb
        o1_ref[...] = o1.astype(BF16)

    tile = pl.BlockSpec((tm, D), lambda i: (i, 0))
    wspec = _const_spec((D, D))
    bshape = jax.ShapeDtypeStruct((t, D), BF16)
    gate1, g1_spec = _vec_operand(gate1)
    return pl.pallas_call(
        body, name="merge_fwd", grid=(t // tm,),
        in_specs=[tile, tile, pl.BlockSpec((tm, D), lambda i: (i, 4)), pl.BlockSpec((tm, D), lambda i: (i, 5)),
                  tile, g1_spec, wspec, wspec, wspec],
        out_specs=[tile] * 5,
        out_shape=[jax.ShapeDtypeStruct((t, D), F32), bshape, bshape, bshape, bshape],
        compiler_params=_cparams(1),
    )(ya_pre, yb_pre, z, z, x, gate1, wba, wbb, wout)


def _conv3(u, prev8, cw_ref, cb):
    return cb + cw_ref[2:3, :] * u + cw_ref[1:2, :] * _shift_down(u, 1, prev8) + cw_ref[0:1, :] * _shift_down(u, 2, prev8)


def _ffn_proj_mid(x2, g, scale, shift, w, cw, cb):
    t = x2.shape[0]
    tm = _tile_big(t)
    nc = DFF // D

    def body(x_ref, g_ref, sc_ref, sh_ref, wa_ref, wv_ref, cwa_ref, cwv_ref, cba_ref, cbv_ref,
             h_ref, upa_ref, upv_ref, ff_ref, fa_ref, fv_ref, hb_scr, prev_ref):
        i, c = pl.program_id(0), pl.program_id(1)

        @pl.when(i == 0)
        def _():
            prev_ref[c] = jnp.zeros((2, SUBLANES, D), F32)

        @pl.when(c == 0)
        def _():
            xv = x_ref[...]
            r = lax.rsqrt(jnp.mean(xv * xv, axis=-1, keepdims=True) + EPS)
            hb_scr[...] = ((xv * r * g_ref[...]) * (1.0 + sc_ref[...]) + sh_ref[...]).astype(BF16)
            h_ref[...] = hb_scr[...]

        hb = hb_scr[...]
        halves = []
        for s, (w_ref, up_ref, cw_ref, cb_ref) in enumerate(((wa_ref, upa_ref, cwa_ref, cba_ref),
                                                             (wv_ref, upv_ref, cwv_ref, cbv_ref))):
            u = jnp.dot(hb, w_ref[...], preferred_element_type=F32)
            up_ref[...] = u.astype(BF16)
            halves.append(_conv3(u, prev_ref[c, s], cw_ref, cb_ref[...]))
            prev_ref[c, s] = u[tm - SUBLANES:]
        act, val = halves
        ga, ta = _gelu_t(act)
        ff_ref[...] = (ga * val).astype(BF16)
        fa_ref[...] = (val * _gelu_grad(ta)).astype(BF16)
        fv_ref[...] = ga.astype(BF16)

    def cols(rows, off):
        return pl.BlockSpec((rows, D), lambda i, c: (0, off + c))

    vec = pl.BlockSpec((1, D), lambda i, c: (0, 0))
    row_tile = pl.BlockSpec((tm, D), lambda i, c: (i, 0))
    chunk = pl.BlockSpec((tm, D), lambda i, c: (i, c))
    hshape = jax.ShapeDtypeStruct((t, DFF), BF16)
    (scale, sc_spec), (shift, sh_spec) = _vec_operand(scale), _vec_operand(shift)
    return pl.pallas_call(
        body, name="ffn_proj_mid", grid=(t // tm, nc),
        in_specs=[row_tile, vec, sc_spec, sh_spec, cols(D, 0), cols(D, nc), cols(3, 0), cols(3, nc), cols(1, 0), cols(1, nc)],
        out_specs=[row_tile, chunk, chunk, chunk, chunk, chunk],
        out_shape=[jax.ShapeDtypeStruct((t, D), BF16), hshape, hshape, hshape, hshape, hshape],
        scratch_shapes=[pltpu.VMEM((tm, D), BF16), pltpu.VMEM((nc, 2, SUBLANES, D), F32)],
        compiler_params=_cparams(2),
    )(x2, g, scale, shift, w, w, cw, cw, cb, cb)


def _ffn_out_loss(ff, wd, x2, target, gate2, gfin):
    t = x2.shape[0]
    tm = _tile_big(t)

    def body(ff_ref, wd_ref, x2_ref, tg_ref, g2_ref, gf_ref, dx3_ref, do2_ref, loss_ref, dgf_ref, dg2_ref):
        @pl.when(pl.program_id(0) == 0)
        def _():
            loss_ref[...] = jnp.zeros_like(loss_ref)
            dgf_ref[...] = jnp.zeros_like(dgf_ref)
            dg2_ref[...] = jnp.zeros_like(dg2_ref)

        o2 = jnp.dot(ff_ref[...], wd_ref[...], preferred_element_type=F32)
        x3 = x2_ref[...] + g2_ref[...] * o2
        r = lax.rsqrt(jnp.mean(x3 * x3, axis=-1, keepdims=True) + EPS)
        xhat = x3 * r
        err = xhat * gf_ref[...] - tg_ref[...]
        loss_ref[...] += 0.5 * jnp.sum(jnp.mean(err * err, axis=-1, keepdims=True), axis=0, keepdims=True)
        dy = err * (1.0 / D)
        dgf_ref[...] += _colsum(dy * xhat)
        dxh = dy * gf_ref[...]
        dx3 = r * (dxh - xhat * jnp.mean(dxh * xhat, axis=-1, keepdims=True))
        dx3_ref[...] = dx3
        do2_ref[...] = (dx3 * g2_ref[...]).astype(BF16)
        dg2_ref[...] += _colsum(dx3 * o2)

    tile = pl.BlockSpec((tm, D), lambda i: (i, 0))
    vec = _const_spec((1, D))
    gate2, g2_spec = _vec_operand(gate2)
    return pl.pallas_call(
        body, name="ffn_out_loss", grid=(t // tm,),
        in_specs=[pl.BlockSpec((tm, DFF), lambda i: (i, 0)), _const_spec((DFF, D), True), tile, tile, g2_spec, vec],
        out_specs=[tile, tile, _const_spec((1, 1)), vec, vec],
        out_shape=[jax.ShapeDtypeStruct((t, D), F32), jax.ShapeDtypeStruct((t, D), BF16),
                   jax.ShapeDtypeStruct((1, 1), F32),
                   jax.ShapeDtypeStruct((1, D), F32), jax.ShapeDtypeStruct((1, D), F32)],
        compiler_params=_cparams(1),
    )(ff, wd, x2, target, gate2, gfin)


def _ffn_down_bwd(do2, ff, fa, fv, wd):
    t = do2.shape[0]
    tm = min(1024, t)
    nc = DFF // D

    def body(do2_ref, ff_ref, fa_ref, fv_ref, wd_ref, da_ref, dv_ref, dwd_ref, dcba_ref, dcbv_ref):
        c = pl.program_id(1)

        @pl.when(pl.program_id(0) == 0)
        def _():
            for r in (dwd_ref, dcba_ref, dcbv_ref):
                r[c] = jnp.zeros(r.shape[1:], F32)

        do2 = do2_ref[...]
        dwd_ref[c] += _dot_tn(ff_ref[...], do2)
        dff = _dot_nt(do2, wd_ref[c])
        dact = dff * fa_ref[...].astype(F32)
        dval = dff * fv_ref[...].astype(F32)
        da_ref[...] = dact.astype(BF16)
        dv_ref[...] = dval.astype(BF16)
        dcba_ref[c] += _colsum(dact)
        dcbv_ref[c] += _colsum(dval)

    blk = pl.BlockSpec((tm, D), lambda i, c: (i, c))
    vec = _acc_spec((nc, 1, D), (0, 0, 0))
    dact, dval, dwd, dcb_a, dcb_v = pl.pallas_call(
        body, name="ffn_down_bwd", grid=(t // tm, nc),
        in_specs=[pl.BlockSpec((tm, D), lambda i, c: (i, 0)),
                  blk, blk, blk, _const_spec((nc, D, D), True)],
        out_specs=[blk, blk, _acc_spec((nc, D, D), (0, 0, 0)), vec, vec],
        out_shape=[jax.ShapeDtypeStruct((t, DFF), BF16), jax.ShapeDtypeStruct((t, DFF), BF16),
                   jax.ShapeDtypeStruct((nc, D, D), F32),
                   jax.ShapeDtypeStruct((nc, 1, D), F32), jax.ShapeDtypeStruct((nc, 1, D), F32)],
        compiler_params=_cparams(2),
    )(do2, ff, fa, fv, wd.reshape(nc, D, D))
    return dact, dval, dwd.reshape(DFF, D), dcb_a.reshape(1, DFF), dcb_v.reshape(1, DFF)


def _modnorm_bwd(dh, xv, g, scale):
    r = lax.rsqrt(jnp.mean(xv * xv, axis=-1, keepdims=True) + EPS)
    xhat = xv * r
    dxn = dh * (1.0 + scale)
    dxh = dxn * g
    dx = r * (dxh - xhat * jnp.mean(dxh * xhat, axis=-1, keepdims=True))
    return dx, _colsum(dh), _colsum(dh * (xhat * g)), _colsum(dxn * xhat)


def _ffn_up_bwd(dact, dval, up_a, up_v, cw, wup, x2, dx3, gffn, scale2, o1, gate1):
    t = x2.shape[0]
    tm = _tile_seq(t)
    nt = t // tm
    nc = DFF // D

    def body(da_ref, dan_ref, dv_ref, dvn_ref, ua_ref, uv_ref, cw_ref, w_ref, x2_ref, dx3_ref, g_ref, sc_ref, o1_ref, g1_ref,
             dup_ref, dx2_ref, do1_ref, dcw_ref, dsh_ref, dsc_ref, dg_ref, dg1_ref):
        i = pl.program_id(0)

        @pl.when(i == 0)
        def _():
            for r in (dcw_ref, dsh_ref, dsc_ref, dg_ref, dg1_ref):
                r[...] = jnp.zeros_like(r)

        last = i == nt - 1
        dh = jnp.zeros((tm, D), F32)
        for half, (d_ref, dn_ref, u_ref) in enumerate(((da_ref, dan_ref, ua_ref), (dv_ref, dvn_ref, uv_ref))):
            nxt = jnp.where(last, 0.0, dn_ref[...].astype(F32)[:SUBLANES])
            for c in range(nc):
                c0 = half * DFF + c * D
                dv = d_ref[:, c * D:(c + 1) * D].astype(F32)
                nx = nxt[:, c * D:(c + 1) * D]
                taps = (_shift_up(dv, 2, nx), _shift_up(dv, 1, nx), dv)
                dup = (cw_ref[2:3, c0:c0 + D] * taps[2] + cw_ref[1:2, c0:c0 + D] * taps[1]
                       + cw_ref[0:1, c0:c0 + D] * taps[0]).astype(BF16)
                upv = u_ref[:, c * D:(c + 1) * D].astype(F32)
                for k in range(3):
                    dcw_ref[k:k + 1, c0:c0 + D] += _colsum(taps[k] * upv)
                dup_ref[:, c0:c0 + D] = dup
                dh = dh + _dot_nt(dup, w_ref[:, c0:c0 + D])
        dxn, dsh, dsc, dg = _modnorm_bwd(dh, x2_ref[...], g_ref[...], sc_ref[...])
        dx2 = dx3_ref[...] + dxn
        dx2_ref[...] = dx2
        do1_ref[...] = (dx2 * g1_ref[...]).astype(BF16)
        dsh_ref[...] += dsh
        dsc_ref[...] += dsc
        dg_ref[...] += dg
        dg1_ref[...] += _colsum(dx2 * o1_ref[...].astype(F32))

    tile = pl.BlockSpec((tm, D), lambda i: (i, 0))
    wide = pl.BlockSpec((tm, DFF), lambda i: (i, 0))
    nxt = pl.BlockSpec((HALO, DFF), lambda i: (jnp.minimum((i + 1) * (tm // HALO), t // HALO - 1), 0))
    vec = _const_spec((1, D))
    vshape = jax.ShapeDtypeStruct((1, D), F32)
    (scale2, sc_spec), (gate1, g1_spec) = _vec_operand(scale2), _vec_operand(gate1)
    return pl.pallas_call(
        body, name="ffn_up_bwd", grid=(nt,),
        in_specs=[wide, nxt, wide, nxt, wide, wide,
                  _const_spec((3, 2 * DFF)), _const_spec((D, 2 * DFF), True),
                  tile, tile, vec, sc_spec, tile, g1_spec],
        out_specs=[pl.BlockSpec((tm, 2 * DFF), lambda i: (i, 0)), tile, tile, _const_spec((3, 2 * DFF)),
                   vec, vec, vec, vec],
        out_shape=[jax.ShapeDtypeStruct((t, 2 * DFF), BF16), jax.ShapeDtypeStruct((t, D), F32),
                   jax.ShapeDtypeStruct((t, D), BF16), jax.ShapeDtypeStruct((3, 2 * DFF), F32),
                   vshape, vshape, vshape, vshape],
        compiler_params=_cparams(1),
    )(dact, dact, dval, dval, up_a, up_v, cw, wup, x2, dx3, gffn, scale2, o1, gate1)


def _xt_y(a, b, name):
    t, k = a.shape
    n = b.shape[1]
    tm = min(1024, t)
    bn = 1536 if n % 1536 == 0 else D

    def body(a_ref, b_ref, o_ref):
        @pl.when(pl.program_id(1) == 0)
        def _():
            o_ref[...] = jnp.zeros_like(o_ref)

        o_ref[...] += _dot_tn(a_ref[...], b_ref[...])

    return pl.pallas_call(
        body, name=name, grid=(n // bn, t // tm),
        in_specs=[pl.BlockSpec((tm, k), lambda j, i: (i, 0)), pl.BlockSpec((tm, bn), lambda j, i: (i, j))],
        out_specs=pl.BlockSpec((k, bn), lambda j, i: (0, j)),
        out_shape=jax.ShapeDtypeStruct((k, n), F32),
        compiler_params=_cparams(2),
    )(a, b)


def _acc_spec(shape, index):
    return pl.BlockSpec(shape, lambda *_: index, pipeline_mode=pl.Buffered(1))


def _out_bwd(do1, wout, merged, ya, yb, z, h1):
    t = do1.shape[0]
    tm = _tile_big(t)

    def body(do1_ref, wo_ref, mg_ref, ya_ref, yb_ref, ga_ref, gb_ref, h1_ref,
             dya_ref, dyb_ref, dz_ref, dwo_ref, dwin_ref):
        @pl.when(pl.program_id(0) == 0)
        def _():
            dwo_ref[...] = jnp.zeros_like(dwo_ref)
            dwin_ref[...] = jnp.zeros_like(dwin_ref)

        do1v = do1_ref[...]
        dwo_ref[...] += _dot_tn(mg_ref[...], do1v)
        dm = _dot_nt(do1v, wo_ref[...])
        sa = _sigmoid(ga_ref[...].astype(F32))
        sb = _sigmoid(gb_ref[...].astype(F32))
        dya_ref[...] = (dm * sa).astype(BF16)
        dyb_ref[...] = (dm * sb).astype(BF16)
        dga = (dm * ya_ref[...].astype(F32) * sa * (1.0 - sa)).astype(BF16)
        dgb = (dm * yb_ref[...].astype(F32) * sb * (1.0 - sb)).astype(BF16)
        dz_ref[:, 0:D] = dga
        dz_ref[:, D:2 * D] = dgb
        h1v = h1_ref[...]
        dwin_ref[:, 0:D] += _dot_tn(h1v, dga)
        dwin_ref[:, D:2 * D] += _dot_tn(h1v, dgb)

    tile = pl.BlockSpec((tm, D), lambda i: (i, 0))
    bshape = jax.ShapeDtypeStruct((t, D), BF16)
    return pl.pallas_call(
        body, name="out_bwd", grid=(t // tm,),
        in_specs=[tile, _const_spec((D, D), True), tile, tile, tile,
                  pl.BlockSpec((tm, D), lambda i: (i, 4)), pl.BlockSpec((tm, D), lambda i: (i, 5)), tile],
        out_specs=[tile, tile, pl.BlockSpec((tm, 2 * D), lambda i: (i, 2)), _acc_spec((D, D), (0, 0)),
                   _acc_spec((D, 2 * D), (0, 2))],
        out_shape=[bshape, bshape, jax.ShapeDtypeStruct((t, NCOL_IN), BF16), jax.ShapeDtypeStruct((D, D), F32),
                   jax.ShapeDtypeStruct((D, NCOL_IN), F32)],
        compiler_params=_cparams(1),
    )(do1, wout, merged, ya, yb, z, z, h1)


def _rnn_bwd(dya, ya_pre, wba, h1, z, saved, h, dz, dwin, cw, wa, wx, lam):
    t = z.shape[0]
    tm = _tile_seq(t)
    nt = t // tm
    ngrp = tm // SUBLANES
    hpt = tm // HALO

    def body(dya_ref, yap_ref, wba_ref, h1_ref, xr_ref, xc_ref, ra_ref, ia_ref, gg_ref, hg_ref, h_ref, hp_ref,
             dz_any, dwin_any, cw_ref, wa_ref, wx_ref, lam_ref,
             dz_ref, dwin_ref, dwba_ref, dcw_ref, dcb_ref, dwa_ref, dba_ref, dwx_ref, dbx_ref, dlam_ref,
             a_first, g_first, dxc_first, b_scr, d_scr, g_scr):
        del dz_any, dwin_any
        i = pl.program_id(0)

        @pl.when(i == 0)
        def _():
            for r in (dwin_ref, dwba_ref, dcw_ref, dcb_ref, dwa_ref, dba_ref, dwx_ref, dbx_ref, dlam_ref,
                      a_first, g_first, dxc_first):
                r[...] = jnp.zeros_like(r)

        dya_v = dya_ref[...]
        dwba_ref[...] += _dot_tn(yap_ref[...], dya_v)
        dyap_v = _dot_nt(dya_v, wba_ref[...])
        h1v = h1_ref[...]

        first_tile = i == nt - 1
        xc = xc_ref[...].astype(F32)
        ra = ra_ref[...].astype(F32)
        ia = ia_ref[...].astype(F32)
        lam_v = lam_ref[...]
        ls = _log_sigmoid(lam_v)
        la = LRU_C * ra * ls
        a = jnp.exp(la)
        mult = jnp.sqrt(-jnp.tanh(la) * (1.0 + a * a))
        hprev8 = jnp.where(first_tile, 0.0, hp_ref[...][HALO - SUBLANES:])
        h_prev = _shift_down(h_ref[...], 1, hprev8)
        dgr = (dyap_v * hg_ref[...].astype(F32)).astype(BF16)
        dz_ref[:, D:2 * D] = dgr
        dwin_ref[:, D:2 * D] += _dot_tn(h1v, dgr)

        b_scr[...] = _shift_up(a, 1, a_first[...])
        d_scr[...] = dyap_v * gg_ref[...].astype(F32)
        row = _row_iota(D)

        def grp(jj, carry):
            r0 = pl.multiple_of((ngrp - 1 - jj) * SUBLANES, SUBLANES)
            bv = b_scr[pl.ds(r0, SUBLANES), :]
            dv = d_scr[pl.ds(r0, SUBLANES), :]
            for d in (1, 2, 4):
                m = row < SUBLANES - d
                dv = jnp.where(m, dv + bv * pltpu.roll(dv, SUBLANES - d, 0), dv)
                bv = jnp.where(m, bv * pltpu.roll(bv, SUBLANES - d, 0), bv)
            gv = dv + bv * carry
            g_scr[pl.ds(r0, SUBLANES), :] = gv
            return gv[0:1, :]

        lax.fori_loop(0, ngrp, grp, g_first[0:1, :])
        g = g_scr[...]
        a_first[...] = a[:SUBLANES]
        g_first[...] = g[:SUBLANES]

        da = g * h_prev
        gx = g * xc
        dmult = gx * ia
        dia = gx * mult
        dxc = g * (mult * ia)
        dla = da * a - dmult * (a * a) / mult
        dra = dla * (LRU_C * ls)
        dlam_ref[...] += _colsum(dla * ra) * (LRU_C * _sigmoid(-lam_v))
        dpa = dra * ra * (1.0 - ra)
        dpx = dia * ia * (1.0 - ia)
        dba_ref[...] += _colsum(dpa)
        dbx_ref[...] += _colsum(dpx)
        dpab = dpa.astype(BF16)
        dpxb = dpx.astype(BF16)
        xcb = xc_ref[...]
        for hd in range(NH):
            sl = slice(hd * HD, (hd + 1) * HD)
            dwa_ref[hd] += _dot_tn(xcb[:, sl], dpab[:, sl])
            dwx_ref[hd] += _dot_tn(xcb[:, sl], dpxb[:, sl])
        dxc = dxc + _heads_nt(dpab, wa_ref) + _heads_nt(dpxb, wx_ref)

        nxt = dxc_first[...]
        taps = (_shift_up(dxc, 3, nxt), _shift_up(dxc, 2, nxt), _shift_up(dxc, 1, nxt), dxc)
        dxr = cw_ref[0:1, :] * taps[0]
        for k in range(1, 4):
            dxr = dxr + cw_ref[k:k + 1, :] * taps[k]
        dxrb = dxr.astype(BF16)
        dz_ref[:, 0:D] = dxrb
        dwin_ref[:, 0:D] += _dot_tn(h1v, dxrb)
        dxc_first[...] = dxc[:SUBLANES]
        dcb_ref[...] += _colsum(dxc)
        xr = xr_ref[...].astype(F32)
        for k in range(4):
            dcw_ref[k:k + 1, :] += _colsum(taps[k] * xr)

    def rev(col):
        return lambda i: (nt - 1 - i, col)

    vec = _const_spec((1, D))
    wspec = _const_spec((NH, HD, HD))
    vshape = jax.ShapeDtypeStruct((1, D), F32)
    wshape = jax.ShapeDtypeStruct((NH, HD, HD), F32)
    any_spec = pl.BlockSpec(memory_space=pl.ANY)
    tile = pl.BlockSpec((tm, D), rev(0))
    outs = pl.pallas_call(
        body, name="rnn_bwd", grid=(nt,),
        in_specs=[tile, tile, _const_spec((D, D), True), tile, tile, tile, tile, tile, tile, tile, tile,
                  pl.BlockSpec((HALO, D), lambda i: (jnp.maximum((nt - 1 - i) * hpt - 1, 0), 0)),
                  any_spec, any_spec, _const_spec((4, D)), wspec, wspec, vec],
        out_specs=[pl.BlockSpec((tm, 2 * D), rev(0)), _acc_spec((D, 2 * D), (0, 0)), _acc_spec((D, D), (0, 0)),
                   _const_spec((4, D)), vec, wspec, vec, wspec, vec, vec],
        out_shape=[jax.ShapeDtypeStruct((t, NCOL_IN), BF16), jax.ShapeDtypeStruct((D, NCOL_IN), F32),
                   jax.ShapeDtypeStruct((D, D), F32), jax.ShapeDtypeStruct((4, D), F32), vshape,
                   wshape, vshape, wshape, vshape, vshape],
        scratch_shapes=[pltpu.VMEM((SUBLANES, D), F32), pltpu.VMEM((SUBLANES, D), F32), pltpu.VMEM((SUBLANES, D), F32),
                        pltpu.VMEM((tm, D), F32), pltpu.VMEM((tm, D), F32), pltpu.VMEM((tm, D), F32)],
        input_output_aliases={12: 0, 13: 1},
        compiler_params=_cparams(1),
    )(dya, ya_pre, wba, h1, z, *saved, h, h, dz, dwin, cw, wa, wx, lam)
    return outs


def _sgu_bwd(dyb, yb_pre, wbb, h1, saved, dz, dwin, lng, lnb, wmt, mask):
    t = dyb.shape[0]
    tm = _tile_big(t)

    def body(dyb_ref, ybp_ref, wbb_ref, h1_ref, gu_ref, mg_ref, vh_ref, gpv_ref, rstd_ref, dz_any, dwin_any,
             lng_ref, lnb_ref, wmt_ref, mask_ref,
             dz_ref, dwin_ref, dwbb_ref, dws_ref, dbst_ref, dlng_ref, dlnb_ref):
        del dz_any, dwin_any

        @pl.when(pl.program_id(0) == 0)
        def _():
            for r in (dwin_ref, dwbb_ref, dws_ref, dbst_ref, dlng_ref, dlnb_ref):
                r[...] = jnp.zeros_like(r)

        lng_v = lng_ref[...]
        vhat = vh_ref[...].astype(F32)
        vb = (vhat * lng_v + lnb_ref[...]).astype(BF16)
        rstd = rstd_ref[...]
        dyb_v = dyb_ref[...]
        dwbb_ref[...] += _dot_tn(ybp_ref[...], dyb_v)
        dyb = _dot_nt(dyb_v, wbb_ref[...])
        h1v = h1_ref[...]
        dzu = (dyb * mg_ref[...].astype(F32)).astype(BF16)
        dz_ref[:, 0:D] = dzu
        dwin_ref[:, 0:D] += _dot_tn(h1v, dzu)
        dmix = dyb * gu_ref[...].astype(F32)
        dmb = dmix.astype(BF16)
        rows = []
        lane = lax.broadcasted_iota(jnp.int32, (HD, NH), 1)
        dbst = jnp.zeros((HD, NH), F32)
        for b0 in range(0, tm, HD):
            cols = []
            for g in range(NH):
                sl = slice(g * HD, (g + 1) * HD)
                dmg = dmb[b0:b0 + HD, sl]
                dws_ref[g] += _dot_nt(dmg, vb[b0:b0 + HD, sl]) * mask_ref[...]
                cols.append(jnp.dot(wmt_ref[g], dmg, preferred_element_type=F32))
                dbst = dbst + jnp.where(lane == g, jnp.sum(dmix[b0:b0 + HD, sl], axis=1, keepdims=True), 0.0)
            rows.append(jnp.concatenate(cols, axis=1))
        dbst_ref[...] += dbst
        dvln = jnp.concatenate(rows, axis=0) if len(rows) > 1 else rows[0]
        dlng_ref[...] += _colsum(dvln * vhat)
        dlnb_ref[...] += _colsum(dvln)
        dvh = dvln * lng_v
        dgv = rstd * (dvh - jnp.mean(dvh, axis=-1, keepdims=True)
                      - vhat * jnp.mean(dvh * vhat, axis=-1, keepdims=True))
        dzv = (dgv * gpv_ref[...].astype(F32)).astype(BF16)
        dz_ref[:, D:2 * D] = dzv
        dwin_ref[:, D:2 * D] += _dot_tn(h1v, dzv)

    vec = _const_spec((1, D))
    wspec = _const_spec((NH, HD, HD))
    vshape = jax.ShapeDtypeStruct((1, D), F32)
    tile = pl.BlockSpec((tm, D), lambda i: (i, 0))
    any_spec = pl.BlockSpec(memory_space=pl.ANY)
    return pl.pallas_call(
        body, name="sgu_bwd", grid=(t // tm,),
        in_specs=[tile, tile, _const_spec((D, D), True), tile, tile, tile, tile, tile,
                  pl.BlockSpec((tm, 1), lambda i: (i, 0)), any_spec, any_spec,
                  vec, vec, wspec, _const_spec((HD, HD))],
        out_specs=[pl.BlockSpec((tm, 2 * D), lambda i: (i, 1)), _acc_spec((D, 2 * D), (0, 1)), _acc_spec((D, D), (0, 0)),
                   wspec, _const_spec((HD, NH)), vec, vec],
        out_shape=[jax.ShapeDtypeStruct((t, NCOL_IN), BF16), jax.ShapeDtypeStruct((D, NCOL_IN), F32),
                   jax.ShapeDtypeStruct((D, D), F32), jax.ShapeDtypeStruct((NH, HD, HD), F32),
                   jax.ShapeDtypeStruct((HD, NH), F32), vshape, vshape],
        input_output_aliases={9: 0, 10: 1},
        compiler_params=_cparams(1),
    )(dyb, yb_pre, wbb, h1, *saved, dz, dwin, lng, lnb, wmt, mask)


def _in_bwd(dz, win, x, dx2, g, scale1):
    t = x.shape[0]
    tm = _tile_big(t)

    def body(dz_ref, w_ref, x_ref, dx2_ref, g_ref, sc_ref, dx_ref, dsh_ref, dsc_ref, dg_ref):
        @pl.when(pl.program_id(0) == 0)
        def _():
            for r in (dsh_ref, dsc_ref, dg_ref):
                r[...] = jnp.zeros_like(r)

        dh = jnp.zeros((tm, D), F32)
        for c0 in range(0, NCOL_IN, D):
            dh = dh + _dot_nt(dz_ref[:, c0:c0 + D], w_ref[:, c0:c0 + D])
        dxn, dsh, dsc, dg = _modnorm_bwd(dh, x_ref[...], g_ref[...], sc_ref[...])
        dx_ref[...] = dx2_ref[...] + dxn
        dsh_ref[...] += dsh
        dsc_ref[...] += dsc
        dg_ref[...] += dg

    tile = pl.BlockSpec((tm, D), lambda i: (i, 0))
    vec = _const_spec((1, D))
    vshape = jax.ShapeDtypeStruct((1, D), F32)
    scale1, sc_spec = _vec_operand(scale1)
    return pl.pallas_call(
        body, name="in_bwd", grid=(t // tm,),
        in_specs=[pl.BlockSpec((tm, NCOL_IN), lambda i: (i, 0)), _const_spec((D, NCOL_IN), True), tile, tile, vec,
                  sc_spec],
        out_specs=[tile, vec, vec, vec],
        out_shape=[jax.ShapeDtypeStruct((t, D), F32), vshape, vshape, vshape],
        compiler_params=_cparams(1),
    )(dz, win, x, dx2, g, scale1)


def _mod_cols(c_all, w_ada, b_cols):
    nb, cols = c_all.shape[0], w_ada.shape[1]

    def body(c_ref, w_ref, b_ref, o_ref):
        cv = c_ref[...]
        ca = (cv * _sigmoid(cv)).astype(BF16)
        o_ref[...] = jnp.dot(ca, w_ref[...].astype(BF16), preferred_element_type=F32) + b_ref[...]

    return pl.pallas_call(body, name="mod_cols", out_shape=jax.ShapeDtypeStruct((nb, cols), F32))(c_all, w_ada, b_cols)


def _ada_grad(c_all, dmod_cols):
    cols = dmod_cols.shape[1]

    def body(c_ref, d_ref, o_ref):
        cv = c_ref[...]
        ca = (cv * _sigmoid(cv)).astype(BF16)
        o_ref[...] = _dot_tn(ca, d_ref[...].astype(BF16))

    return pl.pallas_call(body, name="ada_grad", out_shape=jax.ShapeDtypeStruct((D, cols), F32))(c_all, dmod_cols)


def _adamw_update(w, m, v, g):
    bc1 = 1.0 - ADAM_B1 ** ADAM_STEP
    bc2 = 1.0 - ADAM_B2 ** ADAM_STEP
    mn = ADAM_B1 * m + (1.0 - ADAM_B1) * g
    vn = ADAM_B2 * v + (1.0 - ADAM_B2) * (g * g)
    return -ADAM_LR * ((mn / bc1) / (jnp.sqrt(vn / bc2) + ADAM_EPS) + ADAM_WD * w), mn, vn


def _adamw_group(names, ws, ms, vs, packs, name):
    n = len(names)
    starts, r0 = [], 0
    for w in ws:
        starts.append(r0)
        r0 += _pack_rows(w.shape)

    def body(*refs):
        w_refs, m_refs, v_refs, p_ref = refs[:n], refs[n:2 * n], refs[2 * n:3 * n], refs[3 * n]
        outs = refs[3 * n + 1:]
        for k in range(n):
            rows = _pack_rows(ws[k].shape)
            g = None
            for dev in range(N_DEV):
                if ws[k].shape[0] == 1:
                    term = jnp.concatenate(
                        [p_ref[dev, starts[k] + r:starts[k] + r + 1, :] for r in range(rows)], axis=1)
                else:
                    term = p_ref[dev, starts[k]:starts[k] + rows, :]
                g = term if g is None else g + term
            delta, mn, vn = _adamw_update(w_refs[k][...], m_refs[k][...], v_refs[k][...], g)
            for o_ref, val in zip(outs[4 * k:4 * k + 4], (g, delta, mn, vn)):
                o_ref[...] = val

    shapes = [jax.ShapeDtypeStruct(w.shape, F32) for w in ws for _ in range(4)]
    outs = pl.pallas_call(body, name=name, out_shape=shapes,
                          compiler_params=pltpu.CompilerParams(vmem_limit_bytes=VMEM_LIMIT))(*ws, *ms, *vs, packs)
    return {nm: tuple(outs[4 * k:4 * k + 4]) for k, nm in enumerate(names)}


def _adamw(w, m, v, parts, name):
    rows, cols = w.shape
    tr = _row_tile(rows, cols)
    stacked = [p.ndim == 3 for p in parts]

    def body(*refs):
        w_ref, m_ref, v_ref = refs[:3]
        p_refs = refs[3:3 + len(parts)]
        g_ref, d_ref, mo_ref, vo_ref = refs[3 + len(parts):]
        g = None
        for p_ref, st in zip(p_refs, stacked):
            terms = [p_ref[k].astype(F32) for k in range(p_ref.shape[0])] if st else [p_ref[...].astype(F32)]
            for term in terms:
                g = term if g is None else g + term
        delta, mn, vn = _adamw_update(w_ref[...], m_ref[...], v_ref[...], g)
        g_ref[...] = g
        mo_ref[...] = mn
        vo_ref[...] = vn
        d_ref[...] = delta

    tile = pl.BlockSpec((tr, cols), lambda i: (i, 0))
    p_specs = [pl.BlockSpec((p.shape[0], tr, cols), lambda i: (0, i, 0)) if st else tile for p, st in zip(parts, stacked)]
    shp = jax.ShapeDtypeStruct((rows, cols), F32)
    return pl.pallas_call(
        body, name=name, grid=(rows // tr,),
        in_specs=[tile, tile, tile] + p_specs, out_specs=[tile] * 4, out_shape=[shp] * 4,
        compiler_params=_cparams(1),
    )(w, m, v, *parts)


def _mesh_pos():
    return lax.axis_index("x"), lax.axis_index("y"), lax.axis_index("c")


def _other_chips(x, y):
    return [(1 - x, y), (x, 1 - y), (1 - x, 1 - y)]


def _block_of(ref, axis, index, size):
    if axis == 0:
        return ref.at[index]
    return ref.at[:, pl.ds(pl.multiple_of(index * size, 128), size)]


def _all_gather(shards, axes, name):
    n = len(shards)
    per = 7

    def body(*refs):
        ins, outs, done = refs[:n], refs[n:2 * n], refs[2 * n]
        send_sems, recv_sems, local_sems = refs[2 * n + 1:]
        x, y, c = _mesh_pos()
        me, sibling = (x, y, c), (x, y, 1 - c)
        chips = _other_chips(x, y)

        def rows(a, pos):
            return _block_of(outs[a], axes[a], 4 * pos[0] + 2 * pos[1] + pos[2], shards[a].shape[-1])

        def copy(a, k, block, to, src=None):
            return pltpu.make_async_remote_copy(
                src_ref=rows(a, block) if src is None else src, dst_ref=rows(a, block),
                send_sem=send_sems.at[a * per + k], recv_sem=recv_sems.at[a * per + k],
                device_id=to, device_id_type=MESH_IDS)

        mine = [pltpu.make_async_copy(ins[a], rows(a, me), local_sems.at[a]) for a in range(n)]
        for cp in mine:
            cp.start()
        first = []
        for a in range(n):
            first.append(copy(a, 0, me, sibling, src=ins[a]))
            first += [copy(a, 1 + j, me, (*chip, c), src=ins[a]) for j, chip in enumerate(chips)]
        for cp in first:
            cp.start()
        passed = []
        for j, chip in enumerate(chips):
            for a in range(n):
                copy(a, 1 + j, (*chip, c), me).wait_recv()
                fwd = copy(a, 4 + j, (*chip, c), sibling)
                fwd.start()
                passed.append(fwd)
        for a in range(n):
            copy(a, 0, sibling, me).wait_recv()
            for j, chip in enumerate(chips):
                copy(a, 4 + j, (*chip, 1 - c), me).wait_recv()
        for cp in first + passed:
            cp.wait_send()
        for cp in mine:
            cp.wait()
        done[...] = jnp.zeros_like(done)

    def full_shape(s, ax):
        return (N_DEV,) + s.shape if ax == 0 else s.shape[:-1] + (N_DEV * s.shape[-1],)

    any_spec = pl.BlockSpec(memory_space=pl.ANY)
    outs = pl.pallas_call(
        body, name=name,
        in_specs=[any_spec] * n, out_specs=[any_spec] * n + [pl.BlockSpec(memory_space=pltpu.VMEM)],
        out_shape=[jax.ShapeDtypeStruct(full_shape(s, ax), s.dtype) for s, ax in zip(shards, axes)]
        + [jax.ShapeDtypeStruct((SUBLANES, LANES), F32)],
        scratch_shapes=[pltpu.SemaphoreType.DMA((n * per,)), pltpu.SemaphoreType.DMA((n * per,)),
                        pltpu.SemaphoreType.DMA((n,))],
    )(*shards)
    return outs[:n], outs[n]


def _chip_blocks(x, y):
    return [(x, y)] + _other_chips(x, y)


def _sibling_reduce(gs, axis, name):
    g0, n = gs[0], len(gs)
    rows, cols = (g0.shape[1], g0.shape[2]) if axis == 0 else (g0.shape[0], g0.shape[1] // N_DEV)
    chunk = math.gcd(rows, 64)

    def body(*refs):
        g_refs, own_refs, pay_refs = refs[:n], refs[n:2 * n], refs[2 * n:3 * n]
        send_buf, keep_buf, recv_buf, pay_buf, send_sems, recv_sems, stage_sems, keep_sems, out_sems = refs[3 * n:]
        x, y, c = _mesh_pos()
        sibling = (x, y, 1 - c)
        chips = _chip_blocks(x, y)
        stage, keep, push = [], [], []
        for a in range(n):
            for j, (px, py) in enumerate(chips):
                s = 4 * a + j
                theirs = _block_of(g_refs[a], axis, 4 * px + 2 * py + (1 - c), cols)
                ours = _block_of(g_refs[a], axis, 4 * px + 2 * py + c, cols)
                stage.append(pltpu.make_async_copy(theirs, send_buf.at[s], stage_sems.at[s]))
                keep.append(pltpu.make_async_copy(ours, keep_buf.at[s], keep_sems.at[s]))
                push.append(pltpu.make_async_remote_copy(
                    src_ref=send_buf.at[s], dst_ref=recv_buf.at[s], send_sem=send_sems.at[s],
                    recv_sem=recv_sems.at[s], device_id=sibling, device_id_type=MESH_IDS))
        for cp in stage + keep:
            cp.start()
        for s in range(4 * n):
            stage[s].wait()
            push[s].start()
        written = []
        for s in range(4 * n):
            push[s].wait_recv()
            keep[s].wait()
            a, j = divmod(s, 4)
            res = keep_buf.at[s] if j == 0 else pay_buf.at[3 * a + j - 1]

            def add(r, carry, s=s, res=res):
                sl = pl.ds(pl.multiple_of(r * chunk, chunk), chunk)
                res[sl, :] = (keep_buf[s, sl, :] + recv_buf[s, sl, :]).astype(res.dtype)
                return carry

            lax.fori_loop(0, rows // chunk, add, 0)
            out = pltpu.make_async_copy(res, own_refs[a] if j == 0 else pay_refs[a].at[j - 1], out_sems.at[s])
            out.start()
            written.append(out)
        for cp in push:
            cp.wait_send()
        for cp in written:
            cp.wait()

    any_spec = pl.BlockSpec(memory_space=pl.ANY)
    buf = pltpu.VMEM((4 * n, rows, cols), F32)
    sems = pltpu.SemaphoreType.DMA((4 * n,))
    outs = pl.pallas_call(
        body, name=name,
        in_specs=[any_spec] * n, out_specs=[any_spec] * (2 * n),
        out_shape=[jax.ShapeDtypeStruct((rows, cols), F32)] * n + [jax.ShapeDtypeStruct((3, rows, cols), BF16)] * n,
        scratch_shapes=[buf, buf, buf, pltpu.VMEM((3 * n, rows, cols), BF16), sems, sems, sems, sems, sems],
        compiler_params=pltpu.CompilerParams(vmem_limit_bytes=VMEM_LIMIT),
    )(*gs)
    return list(zip(outs[:n], outs[n:]))


_HBM_SPEC = pl.BlockSpec(memory_space=pltpu.HBM)
_SEM_SPEC = pl.BlockSpec(memory_space=pltpu.SEMAPHORE)
_SIDE_EFFECT = pltpu.SideEffectType.DATAFLOW_SIDE_EFFECTING


def _exchange_start(name, srcs, lands, plan, n_copies):
    nb = len(srcs) + len(lands)

    def body(*refs):
        bufs, send_sems, recv_sems, token = refs[:nb], refs[nb], refs[nb + 1], refs[-1]
        for cp in plan(bufs[:len(srcs)], bufs[len(srcs):], send_sems, recv_sems):
            cp.start()
        token[...] = jnp.zeros_like(token)

    arrays = list(srcs) + list(lands)
    outs = pl.pallas_call(
        body, name=name,
        out_shape=(pltpu.SemaphoreType.DMA((n_copies,)), pltpu.SemaphoreType.DMA((n_copies,)),
                   *[pltpu.HBM(a.shape, a.dtype) for a in arrays], jax.ShapeDtypeStruct((SUBLANES, LANES), F32)),
        in_specs=[_HBM_SPEC] * nb,
        out_specs=(_SEM_SPEC, _SEM_SPEC, *[_HBM_SPEC] * nb, pl.BlockSpec(memory_space=pltpu.VMEM)),
        input_output_aliases={k: 2 + k for k in range(nb)},
        compiler_params=pltpu.CompilerParams(has_side_effects=_SIDE_EFFECT),
    )(*[pltpu.with_memory_space_constraint(a, pltpu.HBM) for a in arrays])
    return outs[0], outs[1], outs[2:2 + len(srcs)], outs[2 + len(srcs):2 + nb], outs[-1]


def _exchange_wait(name, send_sems, recv_sems, srcs, lands, plan, after):
    nb = len(srcs) + len(lands)
    after = list(after)

    def body(*refs):
        bufs, send_ref, recv_ref = refs[:nb], refs[nb], refs[nb + 1]
        for cp in plan(bufs[:len(srcs)], bufs[len(srcs):], send_ref, recv_ref):
            cp.wait_send()
            cp.wait_recv()

    arrays = list(srcs) + list(lands)
    outs = pl.pallas_call(
        body, name=name,
        out_shape=tuple(pltpu.HBM(a.shape, a.dtype) for a in arrays),
        in_specs=[_HBM_SPEC] * nb + [_SEM_SPEC, _SEM_SPEC] + [pl.BlockSpec(memory_space=pl.ANY)] * len(after),
        out_specs=tuple([_HBM_SPEC] * nb),
        input_output_aliases={k: k for k in range(nb)},
        compiler_params=pltpu.CompilerParams(has_side_effects=_SIDE_EFFECT),
    )(*arrays, send_sems, recv_sems, *after)
    return outs[len(srcs):]


def _gather_plan(axes, sizes):
    def plan(src_refs, land_refs, send_sems, recv_sems):
        x, y, c = _mesh_pos()
        copies = []
        for a, (src, land) in enumerate(zip(src_refs, land_refs)):
            mine = _block_of(land, axes[a], 4 * x + 2 * y + c, sizes[a])
            for k in range(1, N_DEV):
                peer = (1 - x if k & 4 else x, 1 - y if k & 2 else y, 1 - c if k & 1 else c)
                idx = a * (N_DEV - 1) + k - 1
                copies.append(pltpu.make_async_remote_copy(
                    src_ref=src, dst_ref=mine, send_sem=send_sems.at[idx], recv_sem=recv_sems.at[idx],
                    device_id=peer, device_id_type=MESH_IDS))
        return copies
    return plan


def _chip_plan(src_refs, land_refs, send_sems, recv_sems):
    x, y, c = _mesh_pos()
    copies = []
    for a, (src, land) in enumerate(zip(src_refs, land_refs)):
        for j, chip in enumerate(_other_chips(x, y)):
            copies.append(pltpu.make_async_remote_copy(
                src_ref=src.at[j], dst_ref=land.at[j], send_sem=send_sems.at[3 * a + j],
                recv_sem=recv_sems.at[3 * a + j], device_id=(*chip, c), device_id_type=MESH_IDS))
    return copies


def _own_block_placed(shard, axis, me):
    if axis == 0:
        full = lax.empty((N_DEV,) + shard.shape, shard.dtype)
        return lax.dynamic_update_slice(full, shard[None], (me,) + (0,) * shard.ndim)
    rows, cols = shard.shape

    def body(me_ref, s_ref, o_ref):
        del me_ref
        o_ref[...] = s_ref[...]

    return pl.pallas_call(
        body, name="place_own_columns",
        grid_spec=pltpu.PrefetchScalarGridSpec(
            num_scalar_prefetch=1, grid=(1,),
            in_specs=[pl.BlockSpec((rows, cols), lambda i, me_ref: (0, 0))],
            out_specs=pl.BlockSpec((rows, cols), lambda i, me_ref: (0, me_ref[0]))),
        out_shape=jax.ShapeDtypeStruct((rows, N_DEV * cols), shard.dtype),
    )(jnp.reshape(me, (1,)).astype(jnp.int32), shard)


def _local_step(x, target, mod, win, late_weights, p, grads_ready=None):
    shift1, scale1, gate1, shift2, scale2, gate2 = ((mod, k) for k in range(6))

    def after_token(v, token):
        return v if token is None else v + token[0:1, 0:1]
    wa, wx = p["lru_w_a"].astype(BF16), p["lru_w_x"].astype(BF16)
    mask = jnp.tril(jnp.ones((HD, HD), F32))
    wm = (p["sgu_w_s"] * mask).astype(BF16)
    wmt = jnp.swapaxes(wm, 1, 2)
    bst = jnp.transpose(p["sgu_b_s"])

    h1, z = _norm_proj(x, p["norm_mix_g"], scale1, shift1, win, "mix_proj")
    hstate, ya_pre, *rnn_saved = _rnn_fwd(
        z, p["rnn_conv_w"], p["rnn_conv_b"], wa, p["lru_b_a"], wx, p["lru_b_x"], p["lru_lambda"])
    yb_pre, *sgu_saved = _sgu_fwd(z, p["sgu_ln_g"], p["sgu_ln_b"], wm, bst)
    wba, wbb, wout = late_weights("merge", [ya_pre, yb_pre])
    x2, ya, yb, merged, o1 = _merge_fwd(ya_pre, yb_pre, z, x, gate1, wba, wbb, wout)
    wup = late_weights("ffn_up", [x2])
    h2, up_a, up_v, ff, fa, fv = _ffn_proj_mid(
        x2, p["norm_ffn_g"], scale2, shift2, wup, p["ffn_conv_w"], p["ffn_conv_b"])
    wd = late_weights("ffn_down", [ff])
    dx3, do2, loss, d_gfin, d_gate2 = _ffn_out_loss(ff, wd, x2, target, gate2, p["norm_final_g"])

    dact, dval, d_wd, dcb_a, dcb_v = _ffn_down_bwd(do2, ff, fa, fv, wd)
    dup, dx2, do1, d_cwf, d_shift2, d_scale2, d_gffn, d_gate1 = _ffn_up_bwd(
        dact, dval, up_a, up_v, p["ffn_conv_w"], wup, x2, dx3, p["norm_ffn_g"], scale2, o1, gate1)
    d_wup = _xt_y(h2, dup, "w_up_grad")
    ready = grads_ready if grads_ready else (lambda stage, big, small: None)
    token = ready("ffn", {"w_up": d_wup, "w_down": d_wd}, {})

    dya, dyb, dz, d_wout, d_win = _out_bwd(do1, wout, merged, ya, yb, z, h1)
    dz, d_win, d_wba, d_cw, d_cb, d_wa, d_ba, d_wx, d_bx, d_lam = _rnn_bwd(
        dya, ya_pre, wba, h1, z, rnn_saved, hstate, dz, d_win, p["rnn_conv_w"], wa, wx,
        after_token(p["lru_lambda"], token))
    small = {
        "rnn_conv_w": d_cw, "rnn_conv_b": d_cb, "lru_w_a": d_wa, "lru_b_a": d_ba, "lru_w_x": d_wx, "lru_b_x": d_bx,
        "lru_lambda": d_lam, "norm_ffn_g": d_gffn, "ffn_conv_w": d_cwf,
        "ffn_conv_b": jnp.concatenate([dcb_a, dcb_v], axis=1), "norm_final_g": d_gfin,
    }
    token = ready("rnn", {}, small)
    dz, d_win, d_wbb, d_ws, d_bst, d_lng, d_lnb = _sgu_bwd(
        dyb, yb_pre, wbb, h1, sgu_saved, dz, d_win, p["sgu_ln_g"], after_token(p["sgu_ln_b"], token), wmt, mask)
    sgu_small = {"sgu_ln_g": d_lng, "sgu_ln_b": d_lnb, "sgu_w_s": d_ws, "sgu_b_s": jnp.transpose(d_bst)}
    mixer = {"w_in": d_win, "w_out": d_wout, "w_branch_a": d_wba, "w_branch_b": d_wbb}
    token = ready("mixer", mixer, sgu_small)
    grad_x, d_shift1, d_scale1, d_gmix = _in_bwd(dz, win, x, dx2, after_token(p["norm_mix_g"], token), scale1)

    small.update(sgu_small)
    small["norm_mix_g"] = d_gmix
    dmod = jnp.stack([d_shift1, d_scale1, d_gate1, d_shift2, d_scale2, d_gate2])
    big = {"w_in": d_win, "w_up": d_wup, "w_branch_a": d_wba, "w_branch_b": d_wbb, "w_out": d_wout, "w_down": d_wd}
    return loss, grad_x, big, small, dmod


LAST_REP = ["b_ada", "norm_mix_g"]
EARLY_REP = ["rnn_conv_b", "lru_w_a", "lru_b_a", "lru_w_x", "lru_b_x", "lru_lambda", "norm_ffn_g", "ffn_conv_b",
             "norm_final_g"]
MID_REP = ["sgu_ln_g", "sgu_ln_b", "sgu_w_s", "sgu_b_s"]
COL_SHARDED = ["rnn_conv_w", "ffn_conv_w"]
SMALL_GROUPS = {"rnn": EARLY_REP + COL_SHARDED, "mixer": MID_REP, "last": LAST_REP}
REPLICATED = LAST_REP + EARLY_REP + MID_REP
SMALL_NAMES = REPLICATED + COL_SHARDED
BIG_NAMES = ["w_in", "w_up", "w_branch_a", "w_branch_b", "w_out", "w_down"]
BIG_AXES = [1, 1, 0, 0, 0, 0]
WEIGHTS = ["w_ada", "b_ada", "norm_mix_g", "w_in", "rnn_conv_w", "rnn_conv_b", "lru_w_a", "lru_b_a", "lru_w_x",
           "lru_b_x", "lru_lambda", "sgu_ln_g", "sgu_ln_b", "sgu_w_s", "sgu_b_s", "w_branch_a", "w_branch_b",
           "w_out", "norm_ffn_g", "w_up", "ffn_conv_w", "ffn_conv_b", "w_down", "norm_final_g"]


def _pack_rows(shape):
    return math.prod(shape) // LANES


def _pack(arrays):
    return jnp.concatenate([a.reshape(-1, LANES) for a in arrays], axis=0)


def kernel(x, c, w_ada, b_ada, norm_mix_g, w_in, rnn_conv_w, rnn_conv_b, lru_w_a, lru_b_a, lru_w_x, lru_b_x, lru_lambda, sgu_ln_g, sgu_ln_b, sgu_w_s, sgu_b_s, w_branch_a, w_branch_b, w_out, norm_ffn_g, w_up, ffn_conv_w, ffn_conv_b, w_down, norm_final_g, loss_target, m_w_ada, m_b_ada, m_norm_mix_g, m_w_in, m_rnn_conv_w, m_rnn_conv_b, m_lru_w_a, m_lru_b_a, m_lru_w_x, m_lru_b_x, m_lru_lambda, m_sgu_ln_g, m_sgu_ln_b, m_sgu_w_s, m_sgu_b_s, m_w_branch_a, m_w_branch_b, m_w_out, m_norm_ffn_g, m_w_up, m_ffn_conv_w, m_ffn_conv_b, m_w_down, m_norm_final_g, v_w_ada, v_b_ada, v_norm_mix_g, v_w_in, v_rnn_conv_w, v_rnn_conv_b, v_lru_w_a, v_lru_b_a, v_lru_w_x, v_lru_b_x, v_lru_lambda, v_sgu_ln_g, v_sgu_ln_b, v_sgu_w_s, v_sgu_b_s, v_w_branch_a, v_w_branch_b, v_w_out, v_norm_ffn_g, v_w_up, v_ffn_conv_w, v_ffn_conv_b, v_w_down, v_norm_final_g):
    given = dict(locals())
    me = 4 * lax.axis_index("x") + 2 * lax.axis_index("y") + lax.axis_index("c")
    ada_cols = w_ada.shape[2]
    conv_cols = {"rnn_conv_w": rnn_conv_w.shape[2], "ffn_conv_w": ffn_conv_w.shape[2]}

    (win, c_all, cw_rnn, cw_ffn), _ = _all_gather(
        [w_in[0].astype(BF16), c.reshape(1, 1, D), rnn_conv_w[0], ffn_conv_w[0]], [1, 0, 1, 1], "gather_first")
    c_all = c_all.reshape(N_DEV, D)

    b_cols = lax.dynamic_slice_in_dim(b_ada, me * ada_cols, ada_cols, axis=1)
    (mod_all,), mod_done = _all_gather(
        [_mod_cols(c_all, w_ada[0], b_cols).reshape(1, N_DEV, ada_cols)], [0], "gather_mod")
    mod_all = mod_all.reshape(N_DEV, N_DEV, ada_cols)
    mod_mine = lax.dynamic_index_in_dim(mod_all, me, axis=1, keepdims=False).reshape(6, 1, D)

    late_groups = {"merge": (["w_branch_a", "w_branch_b", "w_out"], [0, 0, 0]), "ffn_up": (["w_up"], [1]),
                   "ffn_down": (["w_down"], [0])}
    in_flight, started = {}, mod_done[0:1, 0:1]
    for stage, (names, axes) in late_groups.items():
        shards = [(given[n][0] + started).astype(BF16) for n in names]
        plan = _gather_plan(axes, [s.shape[-1] for s in shards])
        send, recv, srcs, lands, token = _exchange_start(
            "gather_start_" + stage, shards, [_own_block_placed(s, ax, me) for s, ax in zip(shards, axes)], plan,
            len(shards) * (N_DEV - 1))
        in_flight[stage] = (send, recv, srcs, lands, plan)
        started = started + token[0:1, 0:1]

    def late_weights(stage, after):
        send, recv, srcs, lands, plan = in_flight[stage]
        full = _exchange_wait("gather_wait_" + stage, send, recv, srcs, lands, plan, after)
        full = [w.reshape(-1, D) if ax == 0 else w for w, ax in zip(full, late_groups[stage][1])]
        return full if len(full) > 1 else full[0]

    mod_mine = mod_mine + started

    reducing, packing = {}, {}

    def start_pack(stage, small):
        pack = _pack([small[n] for n in SMALL_GROUPS[stage]])[None]
        plan = _gather_plan([0], [LANES])
        send, recv, srcs, lands, tok = _exchange_start(
            "small_start_" + stage, [pack], [_own_block_placed(pack, 0, me)], plan, N_DEV - 1)
        packing[stage] = (send, recv, srcs, lands, plan)
        return tok

    def grads_ready(stage, grads, small):
        tokens = [start_pack(stage, small)] if small else []
        if grads:
            tokens.append(start_reduce(stage, grads))
        return sum(tokens[1:], tokens[0])

    def start_reduce(stage, grads):
        names = [n for n in BIG_NAMES if n in grads]
        blocked = {}
        for n in names:
            ax = BIG_AXES[BIG_NAMES.index(n)]
            g = grads[n] if ax == 1 else grads[n].reshape(N_DEV, grads[n].shape[0] // N_DEV, grads[n].shape[1])
            blocked.setdefault((ax, g.shape), []).append((n, g))
        sums = {}
        for (ax, _), group in blocked.items():
            reduced = _sibling_reduce([g for _, g in group], ax, "reduce_sibling_" + "_".join(n for n, _ in group))
            sums.update({n: r for (n, _), r in zip(group, reduced)})
        sums = [sums[n] for n in names]
        pays = [pay for _, pay in sums]
        send, recv, srcs, lands, tok = _exchange_start(
            "reduce_start_" + stage, pays, [lax.empty(p_.shape, p_.dtype) for p_ in pays], _chip_plan, 3 * len(pays))
        reducing[stage] = (names, [own for own, _ in sums], send, recv, srcs, lands)
        return tok

    p = {n: given[n][0] for n in REPLICATED if n not in ("b_ada", "norm_final_g")}
    p = {n: (a.reshape(1, -1) if a.ndim == 1 else a) for n, a in p.items()}
    p["rnn_conv_w"], p["ffn_conv_w"] = cw_rnn, cw_ffn
    p["norm_final_g"] = norm_final_g.reshape(1, D)
    loss, grad_x, _, small, dmod = _local_step(x[0], loss_target[0], mod_mine, win, late_weights, p, grads_ready)

    small["b_ada"] = dmod.reshape(1, 6 * D)
    rows_of = {n: _pack_rows(small[n].shape) for n in SMALL_NAMES}
    (last,), _ = _all_gather([_pack([small[n] for n in LAST_REP])[None]], [0], "gather_small")
    gathered = {"last": last}
    for stage, (send, recv, srcs, lands, plan) in packing.items():
        (gathered[stage],) = _exchange_wait("small_wait_" + stage, send, recv, srcs, lands, plan, [grad_x])
    gathered = {k: v.reshape(N_DEV, -1, LANES) for k, v in gathered.items()}

    out = {}
    for stage, (names, owns, send, recv, srcs, lands) in reducing.items():
        landed = _exchange_wait("reduce_wait_" + stage, send, recv, srcs, lands, _chip_plan, [last])
        for n, own, got in zip(names, owns, landed):
            out[n] = _adamw(given[n][0], given["m_" + n][0], given["v_" + n][0], [own, got], "adamw_" + n)

    dmod_all = gathered["last"][:, :rows_of["b_ada"]].reshape(N_DEV, 6 * D)
    dmod_cols = lax.dynamic_slice_in_dim(dmod_all, me * ada_cols, ada_cols, axis=1)
    out["w_ada"] = _adamw(w_ada[0], m_w_ada[0], v_w_ada[0], [_ada_grad(c_all, dmod_cols)], "adamw_w_ada")

    def rows_form(a):
        return a.reshape(1, -1) if a.size // a.shape[-1] == 1 or a.ndim == 1 else a.reshape(-1, LANES)

    for stage, names in (("last", LAST_REP), ("rnn", EARLY_REP), ("mixer", MID_REP)):
        out.update(_adamw_group(names, *[[rows_form(given[pre + n]) for n in names] for pre in ("", "m_", "v_")],
                                gathered[stage], "adamw_small_" + stage))

    row0 = sum(rows_of[n] for n in EARLY_REP)
    for n in COL_SHARDED:
        full = gathered["rnn"][:, row0:row0 + rows_of[n]].reshape(N_DEV, small[n].shape[0], small[n].shape[1])
        mine = lax.dynamic_slice_in_dim(full, me * conv_cols[n], conv_cols[n], axis=2)
        out[n] = _adamw(given[n][0], given["m_" + n][0], given["v_" + n][0], [mine], "adamw_" + n)
        row0 += rows_of[n]

    total = lax.psum(loss[0, 0], ("x", "y", "c"))
    results = [total, grad_x[None]]
    for kind in range(4):
        results += [out[n][kind].reshape(given[n].shape) for n in WEIGHTS]
    return tuple(results)
```

```python
import math

import jax
import jax.numpy as jnp
from jax import lax
from jax.experimental import pallas as pl
from jax.experimental.pallas import tpu as pltpu

F32 = jnp.float32
BF16 = jnp.bfloat16
MESH_IDS = pl.DeviceIdType.MESH

D = 1024
NH = 8
HD = 128
NCOL_IN = 6 * D
DFF = 3 * D
N_DEV = 8
EPS = 1e-6
LRU_C = 8.0
ADAM_LR, ADAM_B1, ADAM_B2, ADAM_EPS, ADAM_WD, ADAM_STEP = 0.001, 0.9, 0.999, 1e-08, 0.01, 10

SUBLANES = 8
LANES = 128
HALO = 16
VMEM_LIMIT = 56 * 1024 * 1024
GELU_K = math.sqrt(2.0 / math.pi)
GELU_C = 0.044715


def _cparams(n_axes):
    return pltpu.CompilerParams(dimension_semantics=("arbitrary",) * n_axes, vmem_limit_bytes=VMEM_LIMIT)


def _const_spec(shape, single_buffer=False):
    nd = len(shape)
    if single_buffer:
        return pl.BlockSpec(shape, lambda *_: (0,) * nd, pipeline_mode=pl.Buffered(1))
    return pl.BlockSpec(shape, lambda *_: (0,) * nd)


def _vec_operand(v):
    if isinstance(v, tuple):
        stack, k = v
        return stack, pl.BlockSpec((None, 1, D), lambda *_: (k, 0, 0))
    return v, _const_spec((1, D))


def _tile_big(t):
    return min(512, t)


def _tile_seq(t):
    return min(256, t)


def _row_tile(rows, cols):
    cap = max(SUBLANES, (2 * 1024 * 1024) // (4 * cols) // SUBLANES * SUBLANES)
    if rows <= cap:
        return rows
    return next(tr for tr in range(cap, 0, -SUBLANES) if rows % tr == 0)


def _gelu_t(x):
    x2 = x * x
    t = jnp.tanh(x * (GELU_K + (GELU_K * GELU_C) * x2))
    hx = 0.5 * x
    return hx + hx * t, (x2, hx, t)


def _gelu_grad(shared):
    x2, hx, t = shared
    return (0.5 + 0.5 * t) + (hx * (1.0 - t * t)) * (GELU_K + (3.0 * GELU_K * GELU_C) * x2)


def _sigmoid(x):
    return 1.0 / (1.0 + jnp.exp(-x))


def _log_sigmoid(x):
    return -(jnp.maximum(-x, 0.0) + jnp.log1p(jnp.exp(-jnp.abs(x))))


def _row_iota(cols):
    return lax.broadcasted_iota(jnp.int32, (SUBLANES, cols), 0)


def _shift_down(x, k, prev8):
    if k == 0:
        return x
    r = pltpu.roll(x, k, 0)
    p = pltpu.roll(prev8, k, 0)
    head = jnp.where(_row_iota(x.shape[1]) < k, p, r[:SUBLANES])
    return jnp.concatenate([head, r[SUBLANES:]], axis=0)


def _shift_up(x, k, next8):
    if k == 0:
        return x
    n = x.shape[0]
    r = pltpu.roll(x, n - k, 0)
    q = pltpu.roll(next8, SUBLANES - k, 0)
    tail = jnp.where(_row_iota(x.shape[1]) >= SUBLANES - k, q, r[n - SUBLANES:])
    return jnp.concatenate([r[:n - SUBLANES], tail], axis=0)


def _heads_nn(x_bf, w_ref):
    return jnp.concatenate(
        [jnp.dot(x_bf[:, h * HD:(h + 1) * HD], w_ref[h], preferred_element_type=F32) for h in range(NH)], axis=1)


def _heads_nt(x_bf, w_ref):
    return jnp.concatenate(
        [lax.dot_general(x_bf[:, h * HD:(h + 1) * HD], w_ref[h], (((1,), (1,)), ((), ())), preferred_element_type=F32)
         for h in range(NH)], axis=1)


def _dot_nt(a, b):
    return lax.dot_general(a, b, (((1,), (1,)), ((), ())), preferred_element_type=F32)


def _dot_tn(a, b):
    return lax.dot_general(a, b, (((0,), (0,)), ((), ())), preferred_element_type=F32)


def _colsum(x):
    return jnp.sum(x, axis=0, keepdims=True)


def _prev_halo_map(tm, col):
    return lambda i, *_: (jnp.maximum(i * (tm // HALO) - 1, 0), col)


def _norm_proj(x, g, scale, shift, w, name):
    t, n = x.shape[0], w.shape[1]
    tm = _tile_big(t)

    def body(x_ref, g_ref, sc_ref, sh_ref, w_ref, h_ref, z_ref):
        xv = x_ref[...]
        r = lax.rsqrt(jnp.mean(xv * xv, axis=-1, keepdims=True) + EPS)
        hb = ((xv * r * g_ref[...]) * (1.0 + sc_ref[...]) + sh_ref[...]).astype(BF16)
        h_ref[...] = hb
        for c0 in range(0, n, D):
            z_ref[:, c0:c0 + D] = jnp.dot(hb, w_ref[:, c0:c0 + D], preferred_element_type=F32).astype(BF16)

    vec = _const_spec((1, D))
    (scale, sc_spec), (shift, sh_spec) = _vec_operand(scale), _vec_operand(shift)
    return pl.pallas_call(
        body, name=name, grid=(t // tm,),
        in_specs=[pl.BlockSpec((tm, D), lambda i: (i, 0)), vec, sc_spec, sh_spec, _const_spec((D, n), True)],
        out_specs=[pl.BlockSpec((tm, D), lambda i: (i, 0)), pl.BlockSpec((tm, n), lambda i: (i, 0))],
        out_shape=[jax.ShapeDtypeStruct((t, D), BF16), jax.ShapeDtypeStruct((t, n), BF16)],
        compiler_params=_cparams(1),
    )(x, g, scale, shift, w)


def _lru_gates(xc, wa_ref, ba, wx_ref, bx, ls):
    xb = xc.astype(BF16)
    ra = _sigmoid(_heads_nn(xb, wa_ref) + ba)
    ia = _sigmoid(_heads_nn(xb, wx_ref) + bx)
    la = LRU_C * ra * ls
    a = jnp.exp(la)
    mult = jnp.sqrt(-jnp.tanh(la) * (1.0 + a * a))
    return ra, ia, a, mult


def _conv4(xr, prev8, cw_ref, cb):
    return (cb + cw_ref[3:4, :] * xr + cw_ref[2:3, :] * _shift_down(xr, 1, prev8)
            + cw_ref[1:2, :] * _shift_down(xr, 2, prev8) + cw_ref[0:1, :] * _shift_down(xr, 3, prev8))


def _rnn_fwd(z, cw, cb, wa, ba, wx, bx, lam):
    t = z.shape[0]
    tm = _tile_seq(t)
    ngrp = tm // SUBLANES

    def body(xr_ref, xp_ref, gr_ref, cw_ref, cb_ref, wa_ref, ba_ref, wx_ref, bx_ref, lam_ref,
             h_ref, ya_ref, xc_ref, ra_ref, ia_ref, gg_ref, hg_ref, carry_ref, a_scr, u_scr):
        i = pl.program_id(0)

        @pl.when(i == 0)
        def _():
            carry_ref[...] = jnp.zeros_like(carry_ref)

        xr = xr_ref[...].astype(F32)
        prev8 = jnp.where(i == 0, 0.0, xp_ref[...].astype(F32)[HALO - SUBLANES:])
        xc = _conv4(xr, prev8, cw_ref, cb_ref[...])
        ra, ia, a, mult = _lru_gates(xc, wa_ref, ba_ref[...], wx_ref, bx_ref[...], _log_sigmoid(lam_ref[...]))
        xc_ref[...] = xc.astype(BF16)
        ra_ref[...] = ra.astype(BF16)
        ia_ref[...] = ia.astype(BF16)
        a_scr[...] = a
        u_scr[...] = mult * (ia * xc)
        row = _row_iota(D)

        def grp(j, carry):
            r0 = pl.multiple_of(j * SUBLANES, SUBLANES)
            av = a_scr[pl.ds(r0, SUBLANES), :]
            uv = u_scr[pl.ds(r0, SUBLANES), :]
            for d in (1, 2, 4):
                m = row >= d
                uv = jnp.where(m, av * pltpu.roll(uv, d, 0) + uv, uv)
                av = jnp.where(m, av * pltpu.roll(av, d, 0), av)
            hv = uv + av * carry
            h_ref[pl.ds(r0, SUBLANES), :] = hv
            return hv[SUBLANES - 1:SUBLANES, :]

        carry_ref[0:1, :] = lax.fori_loop(0, ngrp, grp, carry_ref[0:1, :])
        grv = gr_ref[...].astype(F32)
        gg, tg = _gelu_t(grv)
        hv = h_ref[...]
        ya_ref[...] = (hv * gg).astype(BF16)
        gg_ref[...] = gg.astype(BF16)
        hg_ref[...] = (hv * _gelu_grad(tg)).astype(BF16)

    vec = _const_spec((1, D))
    wspec = _const_spec((NH, HD, HD))
    tile = pl.BlockSpec((tm, D), lambda i: (i, 0))
    bshape = jax.ShapeDtypeStruct((t, D), BF16)
    return pl.pallas_call(
        body, name="rnn_fwd", grid=(t // tm,),
        in_specs=[tile, pl.BlockSpec((HALO, D), _prev_halo_map(tm, 0)),
                  pl.BlockSpec((tm, D), lambda i: (i, 1)), _const_spec((4, D)), vec, wspec, vec, wspec, vec, vec],
        out_specs=[tile] * 7,
        out_shape=[jax.ShapeDtypeStruct((t, D), F32)] + [bshape] * 6,
        scratch_shapes=[pltpu.VMEM((SUBLANES, D), F32), pltpu.VMEM((tm, D), F32), pltpu.VMEM((tm, D), F32)],
        compiler_params=_cparams(1),
    )(z, z, z, cw, cb, wa, ba, wx, bx, lam)


def _sgu_fwd(z, lng, lnb, wm, bst):
    t = z.shape[0]
    tm = _tile_seq(t)

    def body(zu_ref, zv_ref, lng_ref, lnb_ref, wm_ref, bst_ref, yb_ref, gu_ref, mg_ref, vh_ref, gpv_ref, rstd_ref):
        gu, su = _gelu_t(zu_ref[...].astype(F32))
        gv, sv = _gelu_t(zv_ref[...].astype(F32))
        mu = jnp.mean(gv, axis=-1, keepdims=True)
        cen = gv - mu
        rstd = lax.rsqrt(jnp.mean(cen * cen, axis=-1, keepdims=True) + EPS)
        vhat = cen * rstd
        vb = (vhat * lng_ref[...] + lnb_ref[...]).astype(BF16)
        rows = []
        for b0 in range(0, tm, HD):
            rows.append(jnp.concatenate(
                [jnp.dot(wm_ref[g], vb[b0:b0 + HD, g * HD:(g + 1) * HD], preferred_element_type=F32)
                 + bst_ref[:, g:g + 1] for g in range(NH)], axis=1))
        mixed = jnp.concatenate(rows, axis=0) if len(rows) > 1 else rows[0]
        yb_ref[...] = (gu * mixed).astype(BF16)
        gu_ref[...] = gu.astype(BF16)
        mg_ref[...] = (mixed * _gelu_grad(su)).astype(BF16)
        vh_ref[...] = vhat.astype(BF16)
        gpv_ref[...] = _gelu_grad(sv).astype(BF16)
        rstd_ref[...] = rstd

    vec = _const_spec((1, D))
    tile = pl.BlockSpec((tm, D), lambda i: (i, 0))
    bshape = jax.ShapeDtypeStruct((t, D), BF16)
    return pl.pallas_call(
        body, name="sgu_fwd", grid=(t // tm,),
        in_specs=[pl.BlockSpec((tm, D), lambda i: (i, 2)), pl.BlockSpec((tm, D), lambda i: (i, 3)), vec, vec,
                  _const_spec((NH, HD, HD)), _const_spec((HD, NH))],
        out_specs=[tile] * 5 + [pl.BlockSpec((tm, 1), lambda i: (i, 0))],
        out_shape=[bshape] * 5 + [jax.ShapeDtypeStruct((t, 1), F32)],
        compiler_params=_cparams(1),
    )(z, z, lng, lnb, wm, bst)


def _merge_fwd(ya_pre, yb_pre, z, x, gate1, wba, wbb, wout):
    t = x.shape[0]
    tm = _tile_big(t)

    def body(yap_ref, ybp_ref, ga_ref, gb_ref, x_ref, g1_ref, wba_ref, wbb_ref, wo_ref,
             x2_ref, ya_ref, yb_ref, mg_ref, o1_ref):
        ya = jnp.dot(yap_ref[...], wba_ref[...], preferred_element_type=F32)
        yb = jnp.dot(ybp_ref[...], wbb_ref[...], preferred_element_type=F32)
        merged = _sigmoid(ga_ref[...].astype(F32)) * ya + _sigmoid(gb_ref[...].astype(F32)) * yb
        mb = merged.astype(BF16)
        o1 = jnp.dot(mb, wo_ref[...], preferred_element_type=F32)
        x2_ref[...] = x_ref[...] + g1_ref[...] * o1
        ya_ref[...] = ya.astype(BF16)
        yb_ref[...] = yb.astype(BF16)
        mg_ref[...] = mb
        o1_ref[...] = o1.astype(BF16)

    tile = pl.BlockSpec((tm, D), lambda i: (i, 0))
    wspec = _const_spec((D, D))
    bshape = jax.ShapeDtypeStruct((t, D), BF16)
    gate1, g1_spec = _vec_operand(gate1)
    return pl.pallas_call(
        body, name="merge_fwd", grid=(t // tm,),
        in_specs=[tile, tile, pl.BlockSpec((tm, D), lambda i: (i, 4)), pl.BlockSpec((tm, D), lambda i: (i, 5)),
                  tile, g1_spec, wspec, wspec, wspec],
        out_specs=[tile] * 5,
        out_shape=[jax.ShapeDtypeStruct((t, D), F32), bshape, bshape, bshape, bshape],
        compiler_params=_cparams(1),
    )(ya_pre, yb_pre, z, z, x, gate1, wba, wbb, wout)


def _conv3(u, prev8, cw_ref, cb):
    return cb + cw_ref[2:3, :] * u + cw_ref[1:2, :] * _shift_down(u, 1, prev8) + cw_ref[0:1, :] * _shift_down(u, 2, prev8)


def _ffn_proj_mid(x2, g, scale, shift, w, cw, cb):
    t = x2.shape[0]
    tm = _tile_big(t)
    nc = DFF // D

    def body(x_ref, g_ref, sc_ref, sh_ref, wa_ref, wv_ref, cwa_ref, cwv_ref, cba_ref, cbv_ref,
             h_ref, upa_ref, upv_ref, ff_ref, fa_ref, fv_ref, hb_scr, prev_ref):
        i, c = pl.program_id(0), pl.program_id(1)

        @pl.when(i == 0)
        def _():
            prev_ref[c] = jnp.zeros((2, SUBLANES, D), F32)

        @pl.when(c == 0)
        def _():
            xv = x_ref[...]
            r = lax.rsqrt(jnp.mean(xv * xv, axis=-1, keepdims=True) + EPS)
            hb_scr[...] = ((xv * r * g_ref[...]) * (1.0 + sc_ref[...]) + sh_ref[...]).astype(BF16)
            h_ref[...] = hb_scr[...]

        hb = hb_scr[...]
        halves = []
        for s, (w_ref, up_ref, cw_ref, cb_ref) in enumerate(((wa_ref, upa_ref, cwa_ref, cba_ref),
                                                             (wv_ref, upv_ref, cwv_ref, cbv_ref))):
            u = jnp.dot(hb, w_ref[...], preferred_element_type=F32)
            up_ref[...] = u.astype(BF16)
            halves.append(_conv3(u, prev_ref[c, s], cw_ref, cb_ref[...]))
            prev_ref[c, s] = u[tm - SUBLANES:]
        act, val = halves
        ga, ta = _gelu_t(act)
        ff_ref[...] = (ga * val).astype(BF16)
        fa_ref[...] = (val * _gelu_grad(ta)).astype(BF16)
        fv_ref[...] = ga.astype(BF16)

    def cols(rows, off):
        return pl.BlockSpec((rows, D), lambda i, c: (0, off + c))

    vec = pl.BlockSpec((1, D), lambda i, c: (0, 0))
    row_tile = pl.BlockSpec((tm, D), lambda i, c: (i, 0))
    chunk = pl.BlockSpec((tm, D), lambda i, c: (i, c))
    hshape = jax.ShapeDtypeStruct((t, DFF), BF16)
    (scale, sc_spec), (shift, sh_spec) = _vec_operand(scale), _vec_operand(shift)
    return pl.pallas_call(
        body, name="ffn_proj_mid", grid=(t // tm, nc),
        in_specs=[row_tile, vec, sc_spec, sh_spec, cols(D, 0), cols(D, nc), cols(3, 0), cols(3, nc), cols(1, 0), cols(1, nc)],
        out_specs=[row_tile, chunk, chunk, chunk, chunk, chunk],
        out_shape=[jax.ShapeDtypeStruct((t, D), BF16), hshape, hshape, hshape, hshape, hshape],
        scratch_shapes=[pltpu.VMEM((tm, D), BF16), pltpu.VMEM((nc, 2, SUBLANES, D), F32)],
        compiler_params=_cparams(2),
    )(x2, g, scale, shift, w, w, cw, cw, cb, cb)


def _ffn_out_loss(ff, wd, x2, target, gate2, gfin):
    t = x2.shape[0]
    tm = _tile_big(t)

    def body(ff_ref, wd_ref, x2_ref, tg_ref, g2_ref, gf_ref, dx3_ref, do2_ref, loss_ref, dgf_ref, dg2_ref):
        @pl.when(pl.program_id(0) == 0)
        def _():
            loss_ref[...] = jnp.zeros_like(loss_ref)
            dgf_ref[...] = jnp.zeros_like(dgf_ref)
            dg2_ref[...] = jnp.zeros_like(dg2_ref)

        o2 = jnp.dot(ff_ref[...], wd_ref[...], preferred_element_type=F32)
        x3 = x2_ref[...] + g2_ref[...] * o2
        r = lax.rsqrt(jnp.mean(x3 * x3, axis=-1, keepdims=True) + EPS)
        xhat = x3 * r
        err = xhat * gf_ref[...] - tg_ref[...]
        loss_ref[...] += 0.5 * jnp.sum(jnp.mean(err * err, axis=-1, keepdims=True), axis=0, keepdims=True)
        dy = err * (1.0 / D)
        dgf_ref[...] += _colsum(dy * xhat)
        dxh = dy * gf_ref[...]
        dx3 = r * (dxh - xhat * jnp.mean(dxh * xhat, axis=-1, keepdims=True))
        dx3_ref[...] = dx3
        do2_ref[...] = (dx3 * g2_ref[...]).astype(BF16)
        dg2_ref[...] += _colsum(dx3 * o2)

    tile = pl.BlockSpec((tm, D), lambda i: (i, 0))
    vec = _const_spec((1, D))
    gate2, g2_spec = _vec_operand(gate2)
    return pl.pallas_call(
        body, name="ffn_out_loss", grid=(t // tm,),
        in_specs=[pl.BlockSpec((tm, DFF), lambda i: (i, 0)), _const_spec((DFF, D), True), tile, tile, g2_spec, vec],
        out_specs=[tile, tile, _const_spec((1, 1)), vec, vec],
        out_shape=[jax.ShapeDtypeStruct((t, D), F32), jax.ShapeDtypeStruct((t, D), BF16),
                   jax.ShapeDtypeStruct((1, 1), F32),
                   jax.ShapeDtypeStruct((1, D), F32), jax.ShapeDtypeStruct((1, D), F32)],
        compiler_params=_cparams(1),
    )(ff, wd, x2, target, gate2, gfin)


def _ffn_down_bwd(do2, ff, fa, fv, wd):
    t = do2.shape[0]
    tm = min(1024, t)
    nc = DFF // D

    def body(do2_ref, ff_ref, fa_ref, fv_ref, wd_ref, da_ref, dv_ref, dwd_ref, dcba_ref, dcbv_ref):
        @pl.when(pl.program_id(1) == 0)
        def _():
            for r in (dwd_ref, dcba_ref, dcbv_ref):
                r[...] = jnp.zeros_like(r)

        do2 = do2_ref[...]
        dwd_ref[...] += _dot_tn(ff_ref[...], do2)
        dff = _dot_nt(do2, wd_ref[...])
        dact = dff * fa_ref[...].astype(F32)
        dval = dff * fv_ref[...].astype(F32)
        da_ref[...] = dact.astype(BF16)
        dv_ref[...] = dval.astype(BF16)
        dcba_ref[...] += _colsum(dact)
        dcbv_ref[...] += _colsum(dval)

    blk = pl.BlockSpec((tm, D), lambda c, i: (i, c))
    vec = pl.BlockSpec((1, D), lambda c, i: (0, c))
    return pl.pallas_call(
        body, name="ffn_down_bwd", grid=(nc, t // tm),
        in_specs=[pl.BlockSpec((tm, D), lambda c, i: (i, 0)),
                  blk, blk, blk, pl.BlockSpec((D, D), lambda c, i: (c, 0))],
        out_specs=[blk, blk, pl.BlockSpec((D, D), lambda c, i: (c, 0)), vec, vec],
        out_shape=[jax.ShapeDtypeStruct((t, DFF), BF16), jax.ShapeDtypeStruct((t, DFF), BF16),
                   jax.ShapeDtypeStruct((DFF, D), F32),
                   jax.ShapeDtypeStruct((1, DFF), F32), jax.ShapeDtypeStruct((1, DFF), F32)],
        compiler_params=_cparams(2),
    )(do2, ff, fa, fv, wd)


def _modnorm_bwd(dh, xv, g, scale):
    r = lax.rsqrt(jnp.mean(xv * xv, axis=-1, keepdims=True) + EPS)
    xhat = xv * r
    dxn = dh * (1.0 + scale)
    dxh = dxn * g
    dx = r * (dxh - xhat * jnp.mean(dxh * xhat, axis=-1, keepdims=True))
    return dx, _colsum(dh), _colsum(dh * (xhat * g)), _colsum(dxn * xhat)


def _ffn_up_bwd(dact, dval, up_a, up_v, cw, wup, x2, dx3, gffn, scale2, o1, gate1):
    t = x2.shape[0]
    tm = _tile_seq(t)
    nt = t // tm
    nc = DFF // D

    def body(da_ref, dan_ref, dv_ref, dvn_ref, ua_ref, uv_ref, cw_ref, w_ref, x2_ref, dx3_ref, g_ref, sc_ref, o1_ref, g1_ref,
             dup_ref, dx2_ref, do1_ref, dcw_ref, dsh_ref, dsc_ref, dg_ref, dg1_ref):
        i = pl.program_id(0)

        @pl.when(i == 0)
        def _():
            for r in (dcw_ref, dsh_ref, dsc_ref, dg_ref, dg1_ref):
                r[...] = jnp.zeros_like(r)

        last = i == nt - 1
        dh = jnp.zeros((tm, D), F32)
        for half, (d_ref, dn_ref, u_ref) in enumerate(((da_ref, dan_ref, ua_ref), (dv_ref, dvn_ref, uv_ref))):
            nxt = jnp.where(last, 0.0, dn_ref[...].astype(F32)[:SUBLANES])
            for c in range(nc):
                c0 = half * DFF + c * D
                dv = d_ref[:, c * D:(c + 1) * D].astype(F32)
                nx = nxt[:, c * D:(c + 1) * D]
                taps = (_shift_up(dv, 2, nx), _shift_up(dv, 1, nx), dv)
                dup = (cw_ref[2:3, c0:c0 + D] * taps[2] + cw_ref[1:2, c0:c0 + D] * taps[1]
                       + cw_ref[0:1, c0:c0 + D] * taps[0]).astype(BF16)
                upv = u_ref[:, c * D:(c + 1) * D].astype(F32)
                for k in range(3):
                    dcw_ref[k:k + 1, c0:c0 + D] += _colsum(taps[k] * upv)
                dup_ref[:, c0:c0 + D] = dup
                dh = dh + _dot_nt(dup, w_ref[:, c0:c0 + D])
        dxn, dsh, dsc, dg = _modnorm_bwd(dh, x2_ref[...], g_ref[...], sc_ref[...])
        dx2 = dx3_ref[...] + dxn
        dx2_ref[...] = dx2
        do1_ref[...] = (dx2 * g1_ref[...]).astype(BF16)
        dsh_ref[...] += dsh
        dsc_ref[...] += dsc
        dg_ref[...] += dg
        dg1_ref[...] += _colsum(dx2 * o1_ref[...].astype(F32))

    tile = pl.BlockSpec((tm, D), lambda i: (i, 0))
    wide = pl.BlockSpec((tm, DFF), lambda i: (i, 0))
    nxt = pl.BlockSpec((HALO, DFF), lambda i: (jnp.minimum((i + 1) * (tm // HALO), t // HALO - 1), 0))
    vec = _const_spec((1, D))
    vshape = jax.ShapeDtypeStruct((1, D), F32)
    (scale2, sc_spec), (gate1, g1_spec) = _vec_operand(scale2), _vec_operand(gate1)
    return pl.pallas_call(
        body, name="ffn_up_bwd", grid=(nt,),
        in_specs=[wide, nxt, wide, nxt, wide, wide,
                  _const_spec((3, 2 * DFF)), _const_spec((D, 2 * DFF), True),
                  tile, tile, vec, sc_spec, tile, g1_spec],
        out_specs=[pl.BlockSpec((tm, 2 * DFF), lambda i: (i, 0)), tile, tile, _const_spec((3, 2 * DFF)),
                   vec, vec, vec, vec],
        out_shape=[jax.ShapeDtypeStruct((t, 2 * DFF), BF16), jax.ShapeDtypeStruct((t, D), F32),
                   jax.ShapeDtypeStruct((t, D), BF16), jax.ShapeDtypeStruct((3, 2 * DFF), F32),
                   vshape, vshape, vshape, vshape],
        compiler_params=_cparams(1),
    )(dact, dact, dval, dval, up_a, up_v, cw, wup, x2, dx3, gffn, scale2, o1, gate1)


def _xt_y(a, b, name):
    t, k = a.shape
    n = b.shape[1]
    tm = min(1024, t)
    bn = 3072 if n % 3072 == 0 else D

    def body(a_ref, b_ref, o_ref):
        @pl.when(pl.program_id(1) == 0)
        def _():
            o_ref[...] = jnp.zeros_like(o_ref)

        o_ref[...] += _dot_tn(a_ref[...], b_ref[...])

    return pl.pallas_call(
        body, name=name, grid=(n // bn, t // tm),
        in_specs=[pl.BlockSpec((tm, k), lambda j, i: (i, 0)), pl.BlockSpec((tm, bn), lambda j, i: (i, j))],
        out_specs=pl.BlockSpec((k, bn), lambda j, i: (0, j)),
        out_shape=jax.ShapeDtypeStruct((k, n), F32),
        compiler_params=_cparams(2),
    )(a, b)


def _acc_spec(shape, index):
    return pl.BlockSpec(shape, lambda *_: index, pipeline_mode=pl.Buffered(1))


def _out_bwd(do1, wout, merged, ya, yb, z, h1):
    t = do1.shape[0]
    tm = _tile_big(t)

    def body(do1_ref, wo_ref, mg_ref, ya_ref, yb_ref, ga_ref, gb_ref, h1_ref,
             dya_ref, dyb_ref, dz_ref, dwo_ref, dwin_ref):
        @pl.when(pl.program_id(0) == 0)
        def _():
            dwo_ref[...] = jnp.zeros_like(dwo_ref)
            dwin_ref[...] = jnp.zeros_like(dwin_ref)

        do1v = do1_ref[...]
        dwo_ref[...] += _dot_tn(mg_ref[...], do1v)
        dm = _dot_nt(do1v, wo_ref[...])
        sa = _sigmoid(ga_ref[...].astype(F32))
        sb = _sigmoid(gb_ref[...].astype(F32))
        dya_ref[...] = (dm * sa).astype(BF16)
        dyb_ref[...] = (dm * sb).astype(BF16)
        dga = (dm * ya_ref[...].astype(F32) * sa * (1.0 - sa)).astype(BF16)
        dgb = (dm * yb_ref[...].astype(F32) * sb * (1.0 - sb)).astype(BF16)
        dz_ref[:, 0:D] = dga
        dz_ref[:, D:2 * D] = dgb
        h1v = h1_ref[...]
        dwin_ref[:, 0:D] += _dot_tn(h1v, dga)
        dwin_ref[:, D:2 * D] += _dot_tn(h1v, dgb)

    tile = pl.BlockSpec((tm, D), lambda i: (i, 0))
    bshape = jax.ShapeDtypeStruct((t, D), BF16)
    return pl.pallas_call(
        body, name="out_bwd", grid=(t // tm,),
        in_specs=[tile, _const_spec((D, D), True), tile, tile, tile,
                  pl.BlockSpec((tm, D), lambda i: (i, 4)), pl.BlockSpec((tm, D), lambda i: (i, 5)), tile],
        out_specs=[tile, tile, pl.BlockSpec((tm, 2 * D), lambda i: (i, 2)), _acc_spec((D, D), (0, 0)),
                   _acc_spec((D, 2 * D), (0, 2))],
        out_shape=[bshape, bshape, jax.ShapeDtypeStruct((t, NCOL_IN), BF16), jax.ShapeDtypeStruct((D, D), F32),
                   jax.ShapeDtypeStruct((D, NCOL_IN), F32)],
        compiler_params=_cparams(1),
    )(do1, wout, merged, ya, yb, z, z, h1)


def _rnn_bwd(dya, ya_pre, wba, h1, z, saved, h, dz, dwin, cw, wa, wx, lam):
    t = z.shape[0]
    tm = _tile_seq(t)
    nt = t // tm
    ngrp = tm // SUBLANES
    hpt = tm // HALO

    def body(dya_ref, yap_ref, wba_ref, h1_ref, xr_ref, xc_ref, ra_ref, ia_ref, gg_ref, hg_ref, h_ref, hp_ref,
             dz_any, dwin_any, cw_ref, wa_ref, wx_ref, lam_ref,
             dz_ref, dwin_ref, dwba_ref, dcw_ref, dcb_ref, dwa_ref, dba_ref, dwx_ref, dbx_ref, dlam_ref,
             a_first, g_first, dxc_first, b_scr, d_scr, g_scr):
        del dz_any, dwin_any
        i = pl.program_id(0)

        @pl.when(i == 0)
        def _():
            for r in (dwin_ref, dwba_ref, dcw_ref, dcb_ref, dwa_ref, dba_ref, dwx_ref, dbx_ref, dlam_ref,
                      a_first, g_first, dxc_first):
                r[...] = jnp.zeros_like(r)

        dya_v = dya_ref[...]
        dwba_ref[...] += _dot_tn(yap_ref[...], dya_v)
        dyap_v = _dot_nt(dya_v, wba_ref[...])
        h1v = h1_ref[...]

        first_tile = i == nt - 1
        xc = xc_ref[...].astype(F32)
        ra = ra_ref[...].astype(F32)
        ia = ia_ref[...].astype(F32)
        lam_v = lam_ref[...]
        ls = _log_sigmoid(lam_v)
        la = LRU_C * ra * ls
        a = jnp.exp(la)
        mult = jnp.sqrt(-jnp.tanh(la) * (1.0 + a * a))
        hprev8 = jnp.where(first_tile, 0.0, hp_ref[...][HALO - SUBLANES:])
        h_prev = _shift_down(h_ref[...], 1, hprev8)
        dgr = (dyap_v * hg_ref[...].astype(F32)).astype(BF16)
        dz_ref[:, D:2 * D] = dgr
        dwin_ref[:, D:2 * D] += _dot_tn(h1v, dgr)

        b_scr[...] = _shift_up(a, 1, a_first[...])
        d_scr[...] = dyap_v * gg_ref[...].astype(F32)
        row = _row_iota(D)

        def grp(jj, carry):
            r0 = pl.multiple_of((ngrp - 1 - jj) * SUBLANES, SUBLANES)
            bv = b_scr[pl.ds(r0, SUBLANES), :]
            dv = d_scr[pl.ds(r0, SUBLANES), :]
            for d in (1, 2, 4):
                m = row < SUBLANES - d
                dv = jnp.where(m, dv + bv * pltpu.roll(dv, SUBLANES - d, 0), dv)
                bv = jnp.where(m, bv * pltpu.roll(bv, SUBLANES - d, 0), bv)
            gv = dv + bv * carry
            g_scr[pl.ds(r0, SUBLANES), :] = gv
            return gv[0:1, :]

        lax.fori_loop(0, ngrp, grp, g_first[0:1, :])
        g = g_scr[...]
        a_first[...] = a[:SUBLANES]
        g_first[...] = g[:SUBLANES]

        da = g * h_prev
        gx = g * xc
        dmult = gx * ia
        dia = gx * mult
        dxc = g * (mult * ia)
        dla = da * a - dmult * (a * a) / mult
        dra = dla * (LRU_C * ls)
        dlam_ref[...] += _colsum(dla * ra) * (LRU_C * _sigmoid(-lam_v))
        dpa = dra * ra * (1.0 - ra)
        dpx = dia * ia * (1.0 - ia)
        dba_ref[...] += _colsum(dpa)
        dbx_ref[...] += _colsum(dpx)
        dpab = dpa.astype(BF16)
        dpxb = dpx.astype(BF16)
        xcb = xc_ref[...]
        for hd in range(NH):
            sl = slice(hd * HD, (hd + 1) * HD)
            dwa_ref[hd] += _dot_tn(xcb[:, sl], dpab[:, sl])
            dwx_ref[hd] += _dot_tn(xcb[:, sl], dpxb[:, sl])
        dxc = dxc + _heads_nt(dpab, wa_ref) + _heads_nt(dpxb, wx_ref)

        nxt = dxc_first[...]
        taps = (_shift_up(dxc, 3, nxt), _shift_up(dxc, 2, nxt), _shift_up(dxc, 1, nxt), dxc)
        dxr = cw_ref[0:1, :] * taps[0]
        for k in range(1, 4):
            dxr = dxr + cw_ref[k:k + 1, :] * taps[k]
        dxrb = dxr.astype(BF16)
        dz_ref[:, 0:D] = dxrb
        dwin_ref[:, 0:D] += _dot_tn(h1v, dxrb)
        dxc_first[...] = dxc[:SUBLANES]
        dcb_ref[...] += _colsum(dxc)
        xr = xr_ref[...].astype(F32)
        for k in range(4):
            dcw_ref[k:k + 1, :] += _colsum(taps[k] * xr)

    def rev(col):
        return lambda i: (nt - 1 - i, col)

    vec = _const_spec((1, D))
    wspec = _const_spec((NH, HD, HD))
    vshape = jax.ShapeDtypeStruct((1, D), F32)
    wshape = jax.ShapeDtypeStruct((NH, HD, HD), F32)
    any_spec = pl.BlockSpec(memory_space=pl.ANY)
    tile = pl.BlockSpec((tm, D), rev(0))
    outs = pl.pallas_call(
        body, name="rnn_bwd", grid=(nt,),
        in_specs=[tile, tile, _const_spec((D, D), True), tile, tile, tile, tile, tile, tile, tile, tile,
                  pl.BlockSpec((HALO, D), lambda i: (jnp.maximum((nt - 1 - i) * hpt - 1, 0), 0)),
                  any_spec, any_spec, _const_spec((4, D)), wspec, wspec, vec],
        out_specs=[pl.BlockSpec((tm, 2 * D), rev(0)), _acc_spec((D, 2 * D), (0, 0)), _acc_spec((D, D), (0, 0)),
                   _const_spec((4, D)), vec, wspec, vec, wspec, vec, vec],
        out_shape=[jax.ShapeDtypeStruct((t, NCOL_IN), BF16), jax.ShapeDtypeStruct((D, NCOL_IN), F32),
                   jax.ShapeDtypeStruct((D, D), F32), jax.ShapeDtypeStruct((4, D), F32), vshape,
                   wshape, vshape, wshape, vshape, vshape],
        scratch_shapes=[pltpu.VMEM((SUBLANES, D), F32), pltpu.VMEM((SUBLANES, D), F32), pltpu.VMEM((SUBLANES, D), F32),
                        pltpu.VMEM((tm, D), F32), pltpu.VMEM((tm, D), F32), pltpu.VMEM((tm, D), F32)],
        input_output_aliases={12: 0, 13: 1},
        compiler_params=_cparams(1),
    )(dya, ya_pre, wba, h1, z, *saved, h, h, dz, dwin, cw, wa, wx, lam)
    return outs


def _sgu_bwd(dyb, yb_pre, wbb, h1, saved, dz, dwin, lng, lnb, wmt, mask):
    t = dyb.shape[0]
    tm = _tile_big(t)

    def body(dyb_ref, ybp_ref, wbb_ref, h1_ref, gu_ref, mg_ref, vh_ref, gpv_ref, rstd_ref, dz_any, dwin_any,
             lng_ref, lnb_ref, wmt_ref, mask_ref,
             dz_ref, dwin_ref, dwbb_ref, dws_ref, dbst_ref, dlng_ref, dlnb_ref):
        del dz_any, dwin_any

        @pl.when(pl.program_id(0) == 0)
        def _():
            for r in (dwin_ref, dwbb_ref, dws_ref, dbst_ref, dlng_ref, dlnb_ref):
                r[...] = jnp.zeros_like(r)

        lng_v = lng_ref[...]
        vhat = vh_ref[...].astype(F32)
        vb = (vhat * lng_v + lnb_ref[...]).astype(BF16)
        rstd = rstd_ref[...]
        dyb_v = dyb_ref[...]
        dwbb_ref[...] += _dot_tn(ybp_ref[...], dyb_v)
        dyb = _dot_nt(dyb_v, wbb_ref[...])
        h1v = h1_ref[...]
        dzu = (dyb * mg_ref[...].astype(F32)).astype(BF16)
        dz_ref[:, 0:D] = dzu
        dwin_ref[:, 0:D] += _dot_tn(h1v, dzu)
        dmix = dyb * gu_ref[...].astype(F32)
        dmb = dmix.astype(BF16)
        rows = []
        lane = lax.broadcasted_iota(jnp.int32, (HD, NH), 1)
        dbst = jnp.zeros((HD, NH), F32)
        for b0 in range(0, tm, HD):
            cols = []
            for g in range(NH):
                sl = slice(g * HD, (g + 1) * HD)
                dmg = dmb[b0:b0 + HD, sl]
                dws_ref[g] += _dot_nt(dmg, vb[b0:b0 + HD, sl]) * mask_ref[...]
                cols.append(jnp.dot(wmt_ref[g], dmg, preferred_element_type=F32))
                dbst = dbst + jnp.where(lane == g, jnp.sum(dmix[b0:b0 + HD, sl], axis=1, keepdims=True), 0.0)
            rows.append(jnp.concatenate(cols, axis=1))
        dbst_ref[...] += dbst
        dvln = jnp.concatenate(rows, axis=0) if len(rows) > 1 else rows[0]
        dlng_ref[...] += _colsum(dvln * vhat)
        dlnb_ref[...] += _colsum(dvln)
        dvh = dvln * lng_v
        dgv = rstd * (dvh - jnp.mean(dvh, axis=-1, keepdims=True)
                      - vhat * jnp.mean(dvh * vhat, axis=-1, keepdims=True))
        dzv = (dgv * gpv_ref[...].astype(F32)).astype(BF16)
        dz_ref[:, D:2 * D] = dzv
        dwin_ref[:, D:2 * D] += _dot_tn(h1v, dzv)

    vec = _const_spec((1, D))
    wspec = _const_spec((NH, HD, HD))
    vshape = jax.ShapeDtypeStruct((1, D), F32)
    tile = pl.BlockSpec((tm, D), lambda i: (i, 0))
    any_spec = pl.BlockSpec(memory_space=pl.ANY)
    return pl.pallas_call(
        body, name="sgu_bwd", grid=(t // tm,),
        in_specs=[tile, tile, _const_spec((D, D), True), tile, tile, tile, tile, tile,
                  pl.BlockSpec((tm, 1), lambda i: (i, 0)), any_spec, any_spec,
                  vec, vec, wspec, _const_spec((HD, HD))],
        out_specs=[pl.BlockSpec((tm, 2 * D), lambda i: (i, 1)), _acc_spec((D, 2 * D), (0, 1)), _acc_spec((D, D), (0, 0)),
                   wspec, _const_spec((HD, NH)), vec, vec],
        out_shape=[jax.ShapeDtypeStruct((t, NCOL_IN), BF16), jax.ShapeDtypeStruct((D, NCOL_IN), F32),
                   jax.ShapeDtypeStruct((D, D), F32), jax.ShapeDtypeStruct((NH, HD, HD), F32),
                   jax.ShapeDtypeStruct((HD, NH), F32), vshape, vshape],
        input_output_aliases={9: 0, 10: 1},
        compiler_params=_cparams(1),
    )(dyb, yb_pre, wbb, h1, *saved, dz, dwin, lng, lnb, wmt, mask)


def _in_bwd(dz, win, x, dx2, g, scale1):
    t = x.shape[0]
    tm = _tile_big(t)

    def body(dz_ref, w_ref, x_ref, dx2_ref, g_ref, sc_ref, dx_ref, dsh_ref, dsc_ref, dg_ref):
        @pl.when(pl.program_id(0) == 0)
        def _():
            for r in (dsh_ref, dsc_ref, dg_ref):
                r[...] = jnp.zeros_like(r)

        dh = jnp.zeros((tm, D), F32)
        for c0 in range(0, NCOL_IN, D):
            dh = dh + _dot_nt(dz_ref[:, c0:c0 + D], w_ref[:, c0:c0 + D])
        dxn, dsh, dsc, dg = _modnorm_bwd(dh, x_ref[...], g_ref[...], sc_ref[...])
        dx_ref[...] = dx2_ref[...] + dxn
        dsh_ref[...] += dsh
        dsc_ref[...] += dsc
        dg_ref[...] += dg

    tile = pl.BlockSpec((tm, D), lambda i: (i, 0))
    vec = _const_spec((1, D))
    vshape = jax.ShapeDtypeStruct((1, D), F32)
    scale1, sc_spec = _vec_operand(scale1)
    return pl.pallas_call(
        body, name="in_bwd", grid=(t // tm,),
        in_specs=[pl.BlockSpec((tm, NCOL_IN), lambda i: (i, 0)), _const_spec((D, NCOL_IN), True), tile, tile, vec,
                  sc_spec],
        out_specs=[tile, vec, vec, vec],
        out_shape=[jax.ShapeDtypeStruct((t, D), F32), vshape, vshape, vshape],
        compiler_params=_cparams(1),
    )(dz, win, x, dx2, g, scale1)


def _mod_cols(c_all, w_ada, b_cols):
    nb, cols = c_all.shape[0], w_ada.shape[1]

    def body(c_ref, w_ref, b_ref, o_ref):
        cv = c_ref[...]
        ca = (cv * _sigmoid(cv)).astype(BF16)
        o_ref[...] = jnp.dot(ca, w_ref[...].astype(BF16), preferred_element_type=F32) + b_ref[...]

    return pl.pallas_call(body, name="mod_cols", out_shape=jax.ShapeDtypeStruct((nb, cols), F32))(c_all, w_ada, b_cols)


def _ada_grad(c_all, dmod_cols):
    cols = dmod_cols.shape[1]

    def body(c_ref, d_ref, o_ref):
        cv = c_ref[...]
        ca = (cv * _sigmoid(cv)).astype(BF16)
        o_ref[...] = _dot_tn(ca, d_ref[...].astype(BF16))

    return pl.pallas_call(body, name="ada_grad", out_shape=jax.ShapeDtypeStruct((D, cols), F32))(c_all, dmod_cols)


def _adamw_update(w, m, v, g):
    bc1 = 1.0 - ADAM_B1 ** ADAM_STEP
    bc2 = 1.0 - ADAM_B2 ** ADAM_STEP
    mn = ADAM_B1 * m + (1.0 - ADAM_B1) * g
    vn = ADAM_B2 * v + (1.0 - ADAM_B2) * (g * g)
    return -ADAM_LR * ((mn / bc1) / (jnp.sqrt(vn / bc2) + ADAM_EPS) + ADAM_WD * w), mn, vn


def _adamw_group(names, ws, ms, vs, packs, name):
    n = len(names)
    starts, r0 = [], 0
    for w in ws:
        starts.append(r0)
        r0 += _pack_rows(w.shape)

    def body(*refs):
        w_refs, m_refs, v_refs, p_ref = refs[:n], refs[n:2 * n], refs[2 * n:3 * n], refs[3 * n]
        outs = refs[3 * n + 1:]
        for k in range(n):
            rows = _pack_rows(ws[k].shape)
            g = None
            for dev in range(N_DEV):
                if ws[k].shape[0] == 1:
                    term = jnp.concatenate(
                        [p_ref[dev, starts[k] + r:starts[k] + r + 1, :] for r in range(rows)], axis=1)
                else:
                    term = p_ref[dev, starts[k]:starts[k] + rows, :]
                g = term if g is None else g + term
            delta, mn, vn = _adamw_update(w_refs[k][...], m_refs[k][...], v_refs[k][...], g)
            for o_ref, val in zip(outs[4 * k:4 * k + 4], (g, delta, mn, vn)):
                o_ref[...] = val

    shapes = [jax.ShapeDtypeStruct(w.shape, F32) for w in ws for _ in range(4)]
    outs = pl.pallas_call(body, name=name, out_shape=shapes,
                          compiler_params=pltpu.CompilerParams(vmem_limit_bytes=VMEM_LIMIT))(*ws, *ms, *vs, packs)
    return {nm: tuple(outs[4 * k:4 * k + 4]) for k, nm in enumerate(names)}


def _adamw(w, m, v, parts, name):
    rows, cols = w.shape
    tr = _row_tile(rows, cols)
    stacked = [p.ndim == 3 for p in parts]

    def body(*refs):
        w_ref, m_ref, v_ref = refs[:3]
        p_refs = refs[3:3 + len(parts)]
        g_ref, d_ref, mo_ref, vo_ref = refs[3 + len(parts):]
        g = None
        for p_ref, st in zip(p_refs, stacked):
            terms = [p_ref[k].astype(F32) for k in range(p_ref.shape[0])] if st else [p_ref[...].astype(F32)]
            for term in terms:
                g = term if g is None else g + term
        delta, mn, vn = _adamw_update(w_ref[...], m_ref[...], v_ref[...], g)
        g_ref[...] = g
        mo_ref[...] = mn
        vo_ref[...] = vn
        d_ref[...] = delta

    tile = pl.BlockSpec((tr, cols), lambda i: (i, 0))
    p_specs = [pl.BlockSpec((p.shape[0], tr, cols), lambda i: (0, i, 0)) if st else tile for p, st in zip(parts, stacked)]
    shp = jax.ShapeDtypeStruct((rows, cols), F32)
    return pl.pallas_call(
        body, name=name, grid=(rows // tr,),
        in_specs=[tile, tile, tile] + p_specs, out_specs=[tile] * 4, out_shape=[shp] * 4,
        compiler_params=_cparams(1),
    )(w, m, v, *parts)


def _mesh_pos():
    return lax.axis_index("x"), lax.axis_index("y"), lax.axis_index("c")


def _other_chips(x, y):
    return [(1 - x, y), (x, 1 - y), (1 - x, 1 - y)]


def _block_of(ref, axis, index, size):
    if axis == 0:
        return ref.at[index]
    return ref.at[:, pl.ds(pl.multiple_of(index * size, 128), size)]


def _all_gather(shards, axes, name):
    n = len(shards)
    per = 7

    def body(*refs):
        ins, outs, done = refs[:n], refs[n:2 * n], refs[2 * n]
        send_sems, recv_sems, local_sems = refs[2 * n + 1:]
        x, y, c = _mesh_pos()
        me, sibling = (x, y, c), (x, y, 1 - c)
        chips = _other_chips(x, y)

        def rows(a, pos):
            return _block_of(outs[a], axes[a], 4 * pos[0] + 2 * pos[1] + pos[2], shards[a].shape[-1])

        def copy(a, k, block, to, src=None):
            return pltpu.make_async_remote_copy(
                src_ref=rows(a, block) if src is None else src, dst_ref=rows(a, block),
                send_sem=send_sems.at[a * per + k], recv_sem=recv_sems.at[a * per + k],
                device_id=to, device_id_type=MESH_IDS)

        mine = [pltpu.make_async_copy(ins[a], rows(a, me), local_sems.at[a]) for a in range(n)]
        for cp in mine:
            cp.start()
        first = []
        for a in range(n):
            first.append(copy(a, 0, me, sibling, src=ins[a]))
            first += [copy(a, 1 + j, me, (*chip, c), src=ins[a]) for j, chip in enumerate(chips)]
        for cp in first:
            cp.start()
        passed = []
        for j, chip in enumerate(chips):
            for a in range(n):
                copy(a, 1 + j, (*chip, c), me).wait_recv()
                fwd = copy(a, 4 + j, (*chip, c), sibling)
                fwd.start()
                passed.append(fwd)
        for a in range(n):
            copy(a, 0, sibling, me).wait_recv()
            for j, chip in enumerate(chips):
                copy(a, 4 + j, (*chip, 1 - c), me).wait_recv()
        for cp in first + passed:
            cp.wait_send()
        for cp in mine:
            cp.wait()
        done[...] = jnp.zeros_like(done)

    def full_shape(s, ax):
        return (N_DEV,) + s.shape if ax == 0 else s.shape[:-1] + (N_DEV * s.shape[-1],)

    any_spec = pl.BlockSpec(memory_space=pl.ANY)
    outs = pl.pallas_call(
        body, name=name,
        in_specs=[any_spec] * n, out_specs=[any_spec] * n + [pl.BlockSpec(memory_space=pltpu.VMEM)],
        out_shape=[jax.ShapeDtypeStruct(full_shape(s, ax), s.dtype) for s, ax in zip(shards, axes)]
        + [jax.ShapeDtypeStruct((SUBLANES, LANES), F32)],
        scratch_shapes=[pltpu.SemaphoreType.DMA((n * per,)), pltpu.SemaphoreType.DMA((n * per,)),
                        pltpu.SemaphoreType.DMA((n,))],
    )(*shards)
    return outs[:n], outs[n]


def _chip_blocks(x, y):
    return [(x, y)] + _other_chips(x, y)


def _sibling_reduce(gs, axis, name):
    g0, n = gs[0], len(gs)
    rows, cols = (g0.shape[1], g0.shape[2]) if axis == 0 else (g0.shape[0], g0.shape[1] // N_DEV)
    chunk = math.gcd(rows, 64)

    def body(*refs):
        g_refs, own_refs, pay_refs = refs[:n], refs[n:2 * n], refs[2 * n:3 * n]
        send_buf, keep_buf, recv_buf, pay_buf, send_sems, recv_sems, stage_sems, keep_sems, out_sems = refs[3 * n:]
        x, y, c = _mesh_pos()
        sibling = (x, y, 1 - c)
        chips = _chip_blocks(x, y)
        stage, keep, push = [], [], []
        for a in range(n):
            for j, (px, py) in enumerate(chips):
                s = 4 * a + j
                theirs = _block_of(g_refs[a], axis, 4 * px + 2 * py + (1 - c), cols)
                ours = _block_of(g_refs[a], axis, 4 * px + 2 * py + c, cols)
                stage.append(pltpu.make_async_copy(theirs, send_buf.at[s], stage_sems.at[s]))
                keep.append(pltpu.make_async_copy(ours, keep_buf.at[s], keep_sems.at[s]))
                push.append(pltpu.make_async_remote_copy(
                    src_ref=send_buf.at[s], dst_ref=recv_buf.at[s], send_sem=send_sems.at[s],
                    recv_sem=recv_sems.at[s], device_id=sibling, device_id_type=MESH_IDS))
        for cp in stage + keep:
            cp.start()
        for s in range(4 * n):
            stage[s].wait()
            push[s].start()
        written = []
        for s in range(4 * n):
            push[s].wait_recv()
            keep[s].wait()
            a, j = divmod(s, 4)
            res = keep_buf.at[s] if j == 0 else pay_buf.at[3 * a + j - 1]

            def add(r, carry, s=s, res=res):
                sl = pl.ds(pl.multiple_of(r * chunk, chunk), chunk)
                res[sl, :] = (keep_buf[s, sl, :] + recv_buf[s, sl, :]).astype(res.dtype)
                return carry

            lax.fori_loop(0, rows // chunk, add, 0)
            out = pltpu.make_async_copy(res, own_refs[a] if j == 0 else pay_refs[a].at[j - 1], out_sems.at[s])
            out.start()
            written.append(out)
        for cp in push:
            cp.wait_send()
        for cp in written:
            cp.wait()

    any_spec = pl.BlockSpec(memory_space=pl.ANY)
    buf = pltpu.VMEM((4 * n, rows, cols), F32)
    sems = pltpu.SemaphoreType.DMA((4 * n,))
    outs = pl.pallas_call(
        body, name=name,
        in_specs=[any_spec] * n, out_specs=[any_spec] * (2 * n),
        out_shape=[jax.ShapeDtypeStruct((rows, cols), F32)] * n + [jax.ShapeDtypeStruct((3, rows, cols), BF16)] * n,
        scratch_shapes=[buf, buf, buf, pltpu.VMEM((3 * n, rows, cols), BF16), sems, sems, sems, sems, sems],
        compiler_params=pltpu.CompilerParams(vmem_limit_bytes=VMEM_LIMIT),
    )(*gs)
    return list(zip(outs[:n], outs[n:]))


_HBM_SPEC = pl.BlockSpec(memory_space=pltpu.HBM)
_SEM_SPEC = pl.BlockSpec(memory_space=pltpu.SEMAPHORE)
_SIDE_EFFECT = pltpu.SideEffectType.DATAFLOW_SIDE_EFFECTING


def _exchange_start(name, srcs, lands, plan, n_copies):
    nb = len(srcs) + len(lands)

    def body(*refs):
        bufs, send_sems, recv_sems, token = refs[:nb], refs[nb], refs[nb + 1], refs[-1]
        for cp in plan(bufs[:len(srcs)], bufs[len(srcs):], send_sems, recv_sems):
            cp.start()
        token[...] = jnp.zeros_like(token)

    arrays = list(srcs) + list(lands)
    outs = pl.pallas_call(
        body, name=name,
        out_shape=(pltpu.SemaphoreType.DMA((n_copies,)), pltpu.SemaphoreType.DMA((n_copies,)),
                   *[pltpu.HBM(a.shape, a.dtype) for a in arrays], jax.ShapeDtypeStruct((SUBLANES, LANES), F32)),
        in_specs=[_HBM_SPEC] * nb,
        out_specs=(_SEM_SPEC, _SEM_SPEC, *[_HBM_SPEC] * nb, pl.BlockSpec(memory_space=pltpu.VMEM)),
        input_output_aliases={k: 2 + k for k in range(nb)},
        compiler_params=pltpu.CompilerParams(has_side_effects=_SIDE_EFFECT),
    )(*[pltpu.with_memory_space_constraint(a, pltpu.HBM) for a in arrays])
    return outs[0], outs[1], outs[2:2 + len(srcs)], outs[2 + len(srcs):2 + nb], outs[-1]


def _exchange_wait(name, send_sems, recv_sems, srcs, lands, plan, after):
    nb = len(srcs) + len(lands)
    after = list(after)

    def body(*refs):
        bufs, send_ref, recv_ref = refs[:nb], refs[nb], refs[nb + 1]
        for cp in plan(bufs[:len(srcs)], bufs[len(srcs):], send_ref, recv_ref):
            cp.wait_send()
            cp.wait_recv()

    arrays = list(srcs) + list(lands)
    outs = pl.pallas_call(
        body, name=name,
        out_shape=tuple(pltpu.HBM(a.shape, a.dtype) for a in arrays),
        in_specs=[_HBM_SPEC] * nb + [_SEM_SPEC, _SEM_SPEC] + [pl.BlockSpec(memory_space=pl.ANY)] * len(after),
        out_specs=tuple([_HBM_SPEC] * nb),
        input_output_aliases={k: k for k in range(nb)},
        compiler_params=pltpu.CompilerParams(has_side_effects=_SIDE_EFFECT),
    )(*arrays, send_sems, recv_sems, *after)
    return outs[len(srcs):]


def _gather_plan(axes, sizes):
    def plan(src_refs, land_refs, send_sems, recv_sems):
        x, y, c = _mesh_pos()
        copies = []
        for a, (src, land) in enumerate(zip(src_refs, land_refs)):
            mine = _block_of(land, axes[a], 4 * x + 2 * y + c, sizes[a])
            for k in range(1, N_DEV):
                peer = (1 - x if k & 4 else x, 1 - y if k & 2 else y, 1 - c if k & 1 else c)
                idx = a * (N_DEV - 1) + k - 1
                copies.append(pltpu.make_async_remote_copy(
                    src_ref=src, dst_ref=mine, send_sem=send_sems.at[idx], recv_sem=recv_sems.at[idx],
                    device_id=peer, device_id_type=MESH_IDS))
        return copies
    return plan


def _chip_plan(src_refs, land_refs, send_sems, recv_sems):
    x, y, c = _mesh_pos()
    copies = []
    for a, (src, land) in enumerate(zip(src_refs, land_refs)):
        for j, chip in enumerate(_other_chips(x, y)):
            copies.append(pltpu.make_async_remote_copy(
                src_ref=src.at[j], dst_ref=land.at[j], send_sem=send_sems.at[3 * a + j],
                recv_sem=recv_sems.at[3 * a + j], device_id=(*chip, c), device_id_type=MESH_IDS))
    return copies


def _own_block_placed(shard, axis, me):
    if axis == 0:
        full = lax.empty((N_DEV,) + shard.shape, shard.dtype)
        return lax.dynamic_update_slice(full, shard[None], (me,) + (0,) * shard.ndim)
    rows, cols = shard.shape

    def body(me_ref, s_ref, o_ref):
        del me_ref
        o_ref[...] = s_ref[...]

    return pl.pallas_call(
        body, name="place_own_columns",
        grid_spec=pltpu.PrefetchScalarGridSpec(
            num_scalar_prefetch=1, grid=(1,),
            in_specs=[pl.BlockSpec((rows, cols), lambda i, me_ref: (0, 0))],
            out_specs=pl.BlockSpec((rows, cols), lambda i, me_ref: (0, me_ref[0]))),
        out_shape=jax.ShapeDtypeStruct((rows, N_DEV * cols), shard.dtype),
    )(jnp.reshape(me, (1,)).astype(jnp.int32), shard)


def _local_step(x, target, mod, win, late_weights, p, grads_ready=None):
    shift1, scale1, gate1, shift2, scale2, gate2 = ((mod, k) for k in range(6))

    def after_token(v, token):
        return v if token is None else v + token[0:1, 0:1]
    wa, wx = p["lru_w_a"].astype(BF16), p["lru_w_x"].astype(BF16)
    mask = jnp.tril(jnp.ones((HD, HD), F32))
    wm = (p["sgu_w_s"] * mask).astype(BF16)
    wmt = jnp.swapaxes(wm, 1, 2)
    bst = jnp.transpose(p["sgu_b_s"])

    h1, z = _norm_proj(x, p["norm_mix_g"], scale1, shift1, win, "mix_proj")
    hstate, ya_pre, *rnn_saved = _rnn_fwd(
        z, p["rnn_conv_w"], p["rnn_conv_b"], wa, p["lru_b_a"], wx, p["lru_b_x"], p["lru_lambda"])
    yb_pre, *sgu_saved = _sgu_fwd(z, p["sgu_ln_g"], p["sgu_ln_b"], wm, bst)
    wba, wbb, wout = late_weights("merge", [ya_pre, yb_pre])
    x2, ya, yb, merged, o1 = _merge_fwd(ya_pre, yb_pre, z, x, gate1, wba, wbb, wout)
    wup = late_weights("ffn_up", [x2])
    h2, up_a, up_v, ff, fa, fv = _ffn_proj_mid(
        x2, p["norm_ffn_g"], scale2, shift2, wup, p["ffn_conv_w"], p["ffn_conv_b"])
    wd = late_weights("ffn_down", [ff])
    dx3, do2, loss, d_gfin, d_gate2 = _ffn_out_loss(ff, wd, x2, target, gate2, p["norm_final_g"])

    dact, dval, d_wd, dcb_a, dcb_v = _ffn_down_bwd(do2, ff, fa, fv, wd)
    dup, dx2, do1, d_cwf, d_shift2, d_scale2, d_gffn, d_gate1 = _ffn_up_bwd(
        dact, dval, up_a, up_v, p["ffn_conv_w"], wup, x2, dx3, p["norm_ffn_g"], scale2, o1, gate1)
    d_wup = _xt_y(h2, dup, "w_up_grad")
    ready = grads_ready if grads_ready else (lambda stage, big, small: None)
    token = ready("ffn", {"w_up": d_wup, "w_down": d_wd}, {})

    dya, dyb, dz, d_wout, d_win = _out_bwd(do1, wout, merged, ya, yb, z, h1)
    dz, d_win, d_wba, d_cw, d_cb, d_wa, d_ba, d_wx, d_bx, d_lam = _rnn_bwd(
        dya, ya_pre, wba, h1, z, rnn_saved, hstate, dz, d_win, p["rnn_conv_w"], wa, wx,
        after_token(p["lru_lambda"], token))
    small = {
        "rnn_conv_w": d_cw, "rnn_conv_b": d_cb, "lru_w_a": d_wa, "lru_b_a": d_ba, "lru_w_x": d_wx, "lru_b_x": d_bx,
        "lru_lambda": d_lam, "norm_ffn_g": d_gffn, "ffn_conv_w": d_cwf,
        "ffn_conv_b": jnp.concatenate([dcb_a, dcb_v], axis=1), "norm_final_g": d_gfin,
    }
    token = ready("rnn", {}, small)
    dz, d_win, d_wbb, d_ws, d_bst, d_lng, d_lnb = _sgu_bwd(
        dyb, yb_pre, wbb, h1, sgu_saved, dz, d_win, p["sgu_ln_g"], after_token(p["sgu_ln_b"], token), wmt, mask)
    sgu_small = {"sgu_ln_g": d_lng, "sgu_ln_b": d_lnb, "sgu_w_s": d_ws, "sgu_b_s": jnp.transpose(d_bst)}
    mixer = {"w_in": d_win, "w_out": d_wout, "w_branch_a": d_wba, "w_branch_b": d_wbb}
    token = ready("mixer", mixer, sgu_small)
    grad_x, d_shift1, d_scale1, d_gmix = _in_bwd(dz, win, x, dx2, after_token(p["norm_mix_g"], token), scale1)

    small.update(sgu_small)
    small["norm_mix_g"] = d_gmix
    dmod = jnp.stack([d_shift1, d_scale1, d_gate1, d_shift2, d_scale2, d_gate2])
    big = {"w_in": d_win, "w_up": d_wup, "w_branch_a": d_wba, "w_branch_b": d_wbb, "w_out": d_wout, "w_down": d_wd}
    return loss, grad_x, big, small, dmod


LAST_REP = ["b_ada", "norm_mix_g"]
EARLY_REP = ["rnn_conv_b", "lru_w_a", "lru_b_a", "lru_w_x", "lru_b_x", "lru_lambda", "norm_ffn_g", "ffn_conv_b",
             "norm_final_g"]
MID_REP = ["sgu_ln_g", "sgu_ln_b", "sgu_w_s", "sgu_b_s"]
COL_SHARDED = ["rnn_conv_w", "ffn_conv_w"]
SMALL_GROUPS = {"rnn": EARLY_REP + COL_SHARDED, "mixer": MID_REP, "last": LAST_REP}
REPLICATED = LAST_REP + EARLY_REP + MID_REP
SMALL_NAMES = REPLICATED + COL_SHARDED
BIG_NAMES = ["w_in", "w_up", "w_branch_a", "w_branch_b", "w_out", "w_down"]
BIG_AXES = [1, 1, 0, 0, 0, 0]
WEIGHTS = ["w_ada", "b_ada", "norm_mix_g", "w_in", "rnn_conv_w", "rnn_conv_b", "lru_w_a", "lru_b_a", "lru_w_x",
           "lru_b_x", "lru_lambda", "sgu_ln_g", "sgu_ln_b", "sgu_w_s", "sgu_b_s", "w_branch_a", "w_branch_b",
           "w_out", "norm_ffn_g", "w_up", "ffn_conv_w", "ffn_conv_b", "w_down", "norm_final_g"]


def _pack_rows(shape):
    return math.prod(shape) // LANES


def _pack(arrays):
    return jnp.concatenate([a.reshape(-1, LANES) for a in arrays], axis=0)


def kernel(x, c, w_ada, b_ada, norm_mix_g, w_in, rnn_conv_w, rnn_conv_b, lru_w_a, lru_b_a, lru_w_x, lru_b_x, lru_lambda, sgu_ln_g, sgu_ln_b, sgu_w_s, sgu_b_s, w_branch_a, w_branch_b, w_out, norm_ffn_g, w_up, ffn_conv_w, ffn_conv_b, w_down, norm_final_g, loss_target, m_w_ada, m_b_ada, m_norm_mix_g, m_w_in, m_rnn_conv_w, m_rnn_conv_b, m_lru_w_a, m_lru_b_a, m_lru_w_x, m_lru_b_x, m_lru_lambda, m_sgu_ln_g, m_sgu_ln_b, m_sgu_w_s, m_sgu_b_s, m_w_branch_a, m_w_branch_b, m_w_out, m_norm_ffn_g, m_w_up, m_ffn_conv_w, m_ffn_conv_b, m_w_down, m_norm_final_g, v_w_ada, v_b_ada, v_norm_mix_g, v_w_in, v_rnn_conv_w, v_rnn_conv_b, v_lru_w_a, v_lru_b_a, v_lru_w_x, v_lru_b_x, v_lru_lambda, v_sgu_ln_g, v_sgu_ln_b, v_sgu_w_s, v_sgu_b_s, v_w_branch_a, v_w_branch_b, v_w_out, v_norm_ffn_g, v_w_up, v_ffn_conv_w, v_ffn_conv_b, v_w_down, v_norm_final_g):
    given = dict(locals())
    me = 4 * lax.axis_index("x") + 2 * lax.axis_index("y") + lax.axis_index("c")
    ada_cols = w_ada.shape[2]
    conv_cols = {"rnn_conv_w": rnn_conv_w.shape[2], "ffn_conv_w": ffn_conv_w.shape[2]}

    (win, c_all, cw_rnn, cw_ffn), _ = _all_gather(
        [w_in[0].astype(BF16), c.reshape(1, 1, D), rnn_conv_w[0], ffn_conv_w[0]], [1, 0, 1, 1], "gather_first")
    c_all = c_all.reshape(N_DEV, D)

    b_cols = lax.dynamic_slice_in_dim(b_ada, me * ada_cols, ada_cols, axis=1)
    (mod_all,), mod_done = _all_gather(
        [_mod_cols(c_all, w_ada[0], b_cols).reshape(1, N_DEV, ada_cols)], [0], "gather_mod")
    mod_all = mod_all.reshape(N_DEV, N_DEV, ada_cols)
    mod_mine = lax.dynamic_index_in_dim(mod_all, me, axis=1, keepdims=False).reshape(6, 1, D)

    late_groups = {"merge": (["w_branch_a", "w_branch_b", "w_out"], [0, 0, 0]), "ffn_up": (["w_up"], [1]),
                   "ffn_down": (["w_down"], [0])}
    in_flight, started = {}, mod_done[0:1, 0:1]
    for stage, (names, axes) in late_groups.items():
        shards = [(given[n][0] + started).astype(BF16) for n in names]
        plan = _gather_plan(axes, [s.shape[-1] for s in shards])
        send, recv, srcs, lands, token = _exchange_start(
            "gather_start_" + stage, shards, [_own_block_placed(s, ax, me) for s, ax in zip(shards, axes)], plan,
            len(shards) * (N_DEV - 1))
        in_flight[stage] = (send, recv, srcs, lands, plan)
        started = started + token[0:1, 0:1]

    def late_weights(stage, after):
        send, recv, srcs, lands, plan = in_flight[stage]
        full = _exchange_wait("gather_wait_" + stage, send, recv, srcs, lands, plan, after)
        full = [w.reshape(-1, D) if ax == 0 else w for w, ax in zip(full, late_groups[stage][1])]
        return full if len(full) > 1 else full[0]

    mod_mine = mod_mine + started

    reducing, packing = {}, {}

    def start_pack(stage, small):
        pack = _pack([small[n] for n in SMALL_GROUPS[stage]])[None]
        plan = _gather_plan([0], [LANES])
        send, recv, srcs, lands, tok = _exchange_start(
            "small_start_" + stage, [pack], [_own_block_placed(pack, 0, me)], plan, N_DEV - 1)
        packing[stage] = (send, recv, srcs, lands, plan)
        return tok

    def grads_ready(stage, grads, small):
        tokens = [start_pack(stage, small)] if small else []
        if grads:
            tokens.append(start_reduce(stage, grads))
        return sum(tokens[1:], tokens[0])

    def start_reduce(stage, grads):
        names = [n for n in BIG_NAMES if n in grads]
        blocked = {}
        for n in names:
            ax = BIG_AXES[BIG_NAMES.index(n)]
            g = grads[n] if ax == 1 else grads[n].reshape(N_DEV, grads[n].shape[0] // N_DEV, grads[n].shape[1])
            blocked.setdefault((ax, g.shape), []).append((n, g))
        sums = {}
        for (ax, _), group in blocked.items():
            reduced = _sibling_reduce([g for _, g in group], ax, "reduce_sibling_" + "_".join(n for n, _ in group))
            sums.update({n: r for (n, _), r in zip(group, reduced)})
        sums = [sums[n] for n in names]
        pays = [pay for _, pay in sums]
        send, recv, srcs, lands, tok = _exchange_start(
            "reduce_start_" + stage, pays, [lax.empty(p_.shape, p_.dtype) for p_ in pays], _chip_plan, 3 * len(pays))
        reducing[stage] = (names, [own for own, _ in sums], send, recv, srcs, lands)
        return tok

    p = {n: given[n][0] for n in REPLICATED if n not in ("b_ada", "norm_final_g")}
    p = {n: (a.reshape(1, -1) if a.ndim == 1 else a) for n, a in p.items()}
    p["rnn_conv_w"], p["ffn_conv_w"] = cw_rnn, cw_ffn
    p["norm_final_g"] = norm_final_g.reshape(1, D)
    loss, grad_x, _, small, dmod = _local_step(x[0], loss_target[0], mod_mine, win, late_weights, p, grads_ready)

    small["b_ada"] = dmod.reshape(1, 6 * D)
    rows_of = {n: _pack_rows(small[n].shape) for n in SMALL_NAMES}
    (last,), _ = _all_gather([_pack([small[n] for n in LAST_REP])[None]], [0], "gather_small")
    gathered = {"last": last}
    for stage, (send, recv, srcs, lands, plan) in packing.items():
        (gathered[stage],) = _exchange_wait("small_wait_" + stage, send, recv, srcs, lands, plan, [grad_x])
    gathered = {k: v.reshape(N_DEV, -1, LANES) for k, v in gathered.items()}

    out = {}
    for stage, (names, owns, send, recv, srcs, lands) in reducing.items():
        landed = _exchange_wait("reduce_wait_" + stage, send, recv, srcs, lands, _chip_plan, [last])
        for n, own, got in zip(names, owns, landed):
            out[n] = _adamw(given[n][0], given["m_" + n][0], given["v_" + n][0], [own, got], "adamw_" + n)

    dmod_all = gathered["last"][:, :rows_of["b_ada"]].reshape(N_DEV, 6 * D)
    dmod_cols = lax.dynamic_slice_in_dim(dmod_all, me * ada_cols, ada_cols, axis=1)
    out["w_ada"] = _adamw(w_ada[0], m_w_ada[0], v_w_ada[0], [_ada_grad(c_all, dmod_cols)], "adamw_w_ada")

    def rows_form(a):
        return a.reshape(1, -1) if a.size // a.shape[-1] == 1 or a.ndim == 1 else a.reshape(-1, LANES)

    for stage, names in (("last", LAST_REP), ("rnn", EARLY_REP), ("mixer", MID_REP)):
        out.update(_adamw_group(names, *[[rows_form(given[pre + n]) for n in names] for pre in ("", "m_", "v_")],
                                gathered[stage], "adamw_small_" + stage))

    row0 = sum(rows_of[n] for n in EARLY_REP)
    for n in COL_SHARDED:
        full = gathered["rnn"][:, row0:row0 + rows_of[n]].reshape(N_DEV, small[n].shape[0], small[n].shape[1])
        mine = lax.dynamic_slice_in_dim(full, me * conv_cols[n], conv_cols[n], axis=2)
        out[n] = _adamw(given[n][0], given["m_" + n][0], given["v_" + n][0], [mine], "adamw_" + n)
        row0 += rows_of[n]

    total = lax.psum(loss[0, 0], ("x", "y", "c"))
    results = [total, grad_x[None]]
    for kind in range(4):
        results += [out[n][kind].reshape(given[n].shape) for n in WEIGHTS]
    return tuple(results)
```

```python
import math

import jax
import jax.numpy as jnp
from jax import lax
from jax.experimental import pallas as pl
from jax.experimental.pallas import tpu as pltpu

F32 = jnp.float32
BF16 = jnp.bfloat16
MESH_IDS = pl.DeviceIdType.MESH

D = 1024
NH = 8
HD = 128
NCOL_IN = 6 * D
DFF = 3 * D
N_DEV = 8
EPS = 1e-6
LRU_C = 8.0
ADAM_LR, ADAM_B1, ADAM_B2, ADAM_EPS, ADAM_WD, ADAM_STEP = 0.001, 0.9, 0.999, 1e-08, 0.01, 10

SUBLANES = 8
LANES = 128
HALO = 16
VMEM_LIMIT = 56 * 1024 * 1024
GELU_K = math.sqrt(2.0 / math.pi)
GELU_C = 0.044715


def _cparams(n_axes):
    return pltpu.CompilerParams(dimension_semantics=("arbitrary",) * n_axes, vmem_limit_bytes=VMEM_LIMIT)


def _const_spec(shape, single_buffer=False):
    nd = len(shape)
    if single_buffer:
        return pl.BlockSpec(shape, lambda *_: (0,) * nd, pipeline_mode=pl.Buffered(1))
    return pl.BlockSpec(shape, lambda *_: (0,) * nd)


def _vec_operand(v):
    if isinstance(v, tuple):
        stack, k = v
        return stack, pl.BlockSpec((None, 1, D), lambda *_: (k, 0, 0))
    return v, _const_spec((1, D))


def _tile_big(t):
    return min(512, t)


def _tile_seq(t):
    return min(256, t)


def _row_tile(rows, cols):
    cap = max(SUBLANES, (2 * 1024 * 1024) // (4 * cols) // SUBLANES * SUBLANES)
    if rows <= cap:
        return rows
    return next(tr for tr in range(cap, 0, -SUBLANES) if rows % tr == 0)


def _gelu_t(x):
    x2 = x * x
    t = jnp.tanh(x * (GELU_K + (GELU_K * GELU_C) * x2))
    hx = 0.5 * x
    return hx + hx * t, (x2, hx, t)


def _gelu_grad(shared):
    x2, hx, t = shared
    return (0.5 + 0.5 * t) + (hx * (1.0 - t * t)) * (GELU_K + (3.0 * GELU_K * GELU_C) * x2)


def _sigmoid(x):
    return 1.0 / (1.0 + jnp.exp(-x))


def _log_sigmoid(x):
    return -(jnp.maximum(-x, 0.0) + jnp.log1p(jnp.exp(-jnp.abs(x))))


def _row_iota(cols):
    return lax.broadcasted_iota(jnp.int32, (SUBLANES, cols), 0)


def _shift_down(x, k, prev8):
    if k == 0:
        return x
    r = pltpu.roll(x, k, 0)
    p = pltpu.roll(prev8, k, 0)
    head = jnp.where(_row_iota(x.shape[1]) < k, p, r[:SUBLANES])
    return jnp.concatenate([head, r[SUBLANES:]], axis=0)


def _shift_up(x, k, next8):
    if k == 0:
        return x
    n = x.shape[0]
    r = pltpu.roll(x, n - k, 0)
    q = pltpu.roll(next8, SUBLANES - k, 0)
    tail = jnp.where(_row_iota(x.shape[1]) >= SUBLANES - k, q, r[n - SUBLANES:])
    return jnp.concatenate([r[:n - SUBLANES], tail], axis=0)


def _heads_nn(x_bf, w_ref):
    return jnp.concatenate(
        [jnp.dot(x_bf[:, h * HD:(h + 1) * HD], w_ref[h], preferred_element_type=F32) for h in range(NH)], axis=1)


def _heads_nt(x_bf, w_ref):
    return jnp.concatenate(
        [lax.dot_general(x_bf[:, h * HD:(h + 1) * HD], w_ref[h], (((1,), (1,)), ((), ())), preferred_element_type=F32)
         for h in range(NH)], axis=1)


def _dot_nt(a, b):
    return lax.dot_general(a, b, (((1,), (1,)), ((), ())), preferred_element_type=F32)


def _dot_tn(a, b):
    return lax.dot_general(a, b, (((0,), (0,)), ((), ())), preferred_element_type=F32)


def _colsum(x):
    return jnp.sum(x, axis=0, keepdims=True)


def _prev_halo_map(tm, col):
    return lambda i, *_: (jnp.maximum(i * (tm // HALO) - 1, 0), col)


def _norm_proj(x, g, scale, shift, w, name):
    t, n = x.shape[0], w.shape[1]
    tm = _tile_big(t)

    def body(x_ref, g_ref, sc_ref, sh_ref, w_ref, h_ref, z_ref):
        xv = x_ref[...]
        r = lax.rsqrt(jnp.mean(xv * xv, axis=-1, keepdims=True) + EPS)
        hb = ((xv * r * g_ref[...]) * (1.0 + sc_ref[...]) + sh_ref[...]).astype(BF16)
        h_ref[...] = hb
        for c0 in range(0, n, D):
            z_ref[:, c0:c0 + D] = jnp.dot(hb, w_ref[:, c0:c0 + D], preferred_element_type=F32).astype(BF16)

    vec = _const_spec((1, D))
    (scale, sc_spec), (shift, sh_spec) = _vec_operand(scale), _vec_operand(shift)
    return pl.pallas_call(
        body, name=name, grid=(t // tm,),
        in_specs=[pl.BlockSpec((tm, D), lambda i: (i, 0)), vec, sc_spec, sh_spec, _const_spec((D, n), True)],
        out_specs=[pl.BlockSpec((tm, D), lambda i: (i, 0)), pl.BlockSpec((tm, n), lambda i: (i, 0))],
        out_shape=[jax.ShapeDtypeStruct((t, D), BF16), jax.ShapeDtypeStruct((t, n), BF16)],
        compiler_params=_cparams(1),
    )(x, g, scale, shift, w)


def _lru_gates(xc, wa_ref, ba, wx_ref, bx, ls):
    xb = xc.astype(BF16)
    ra = _sigmoid(_heads_nn(xb, wa_ref) + ba)
    ia = _sigmoid(_heads_nn(xb, wx_ref) + bx)
    la = LRU_C * ra * ls
    a = jnp.exp(la)
    mult = jnp.sqrt(-jnp.tanh(la) * (1.0 + a * a))
    return ra, ia, a, mult


def _conv4(xr, prev8, cw_ref, cb):
    return (cb + cw_ref[3:4, :] * xr + cw_ref[2:3, :] * _shift_down(xr, 1, prev8)
            + cw_ref[1:2, :] * _shift_down(xr, 2, prev8) + cw_ref[0:1, :] * _shift_down(xr, 3, prev8))


def _rnn_fwd(z, cw, cb, wa, ba, wx, bx, lam):
    t = z.shape[0]
    tm = _tile_seq(t)
    ngrp = tm // SUBLANES

    def body(xr_ref, xp_ref, gr_ref, cw_ref, cb_ref, wa_ref, ba_ref, wx_ref, bx_ref, lam_ref,
             h_ref, ya_ref, xc_ref, ra_ref, ia_ref, gg_ref, hg_ref, carry_ref, a_scr, u_scr):
        i = pl.program_id(0)

        @pl.when(i == 0)
        def _():
            carry_ref[...] = jnp.zeros_like(carry_ref)

        xr = xr_ref[...].astype(F32)
        prev8 = jnp.where(i == 0, 0.0, xp_ref[...].astype(F32)[HALO - SUBLANES:])
        xc = _conv4(xr, prev8, cw_ref, cb_ref[...])
        ra, ia, a, mult = _lru_gates(xc, wa_ref, ba_ref[...], wx_ref, bx_ref[...], _log_sigmoid(lam_ref[...]))
        xc_ref[...] = xc.astype(BF16)
        ra_ref[...] = ra.astype(BF16)
        ia_ref[...] = ia.astype(BF16)
        a_scr[...] = a
        u_scr[...] = mult * (ia * xc)
        row = _row_iota(D)

        def grp(j, carry):
            r0 = pl.multiple_of(j * SUBLANES, SUBLANES)
            av = a_scr[pl.ds(r0, SUBLANES), :]
            uv = u_scr[pl.ds(r0, SUBLANES), :]
            for d in (1, 2, 4):
                m = row >= d
                uv = jnp.where(m, av * pltpu.roll(uv, d, 0) + uv, uv)
                av = jnp.where(m, av * pltpu.roll(av, d, 0), av)
            hv = uv + av * carry
            h_ref[pl.ds(r0, SUBLANES), :] = hv
            return hv[SUBLANES - 1:SUBLANES, :]

        carry_ref[0:1, :] = lax.fori_loop(0, ngrp, grp, carry_ref[0:1, :])
        grv = gr_ref[...].astype(F32)
        gg, tg = _gelu_t(grv)
        hv = h_ref[...]
        ya_ref[...] = (hv * gg).astype(BF16)
        gg_ref[...] = gg.astype(BF16)
        hg_ref[...] = (hv * _gelu_grad(tg)).astype(BF16)

    vec = _const_spec((1, D))
    wspec = _const_spec((NH, HD, HD))
    tile = pl.BlockSpec((tm, D), lambda i: (i, 0))
    bshape = jax.ShapeDtypeStruct((t, D), BF16)
    return pl.pallas_call(
        body, name="rnn_fwd", grid=(t // tm,),
        in_specs=[tile, pl.BlockSpec((HALO, D), _prev_halo_map(tm, 0)),
                  pl.BlockSpec((tm, D), lambda i: (i, 1)), _const_spec((4, D)), vec, wspec, vec, wspec, vec, vec],
        out_specs=[tile] * 7,
        out_shape=[jax.ShapeDtypeStruct((t, D), F32)] + [bshape] * 6,
        scratch_shapes=[pltpu.VMEM((SUBLANES, D), F32), pltpu.VMEM((tm, D), F32), pltpu.VMEM((tm, D), F32)],
        compiler_params=_cparams(1),
    )(z, z, z, cw, cb, wa, ba, wx, bx, lam)


def _sgu_fwd(z, lng, lnb, wm, bst):
    t = z.shape[0]
    tm = _tile_seq(t)

    def body(zu_ref, zv_ref, lng_ref, lnb_ref, wm_ref, bst_ref, yb_ref, gu_ref, mg_ref, vh_ref, gpv_ref, rstd_ref):
        gu, su = _gelu_t(zu_ref[...].astype(F32))
        gv, sv = _gelu_t(zv_ref[...].astype(F32))
        mu = jnp.mean(gv, axis=-1, keepdims=True)
        cen = gv - mu
        rstd = lax.rsqrt(jnp.mean(cen * cen, axis=-1, keepdims=True) + EPS)
        vhat = cen * rstd
        vb = (vhat * lng_ref[...] + lnb_ref[...]).astype(BF16)
        rows = []
        for b0 in range(0, tm, HD):
            rows.append(jnp.concatenate(
                [jnp.dot(wm_ref[g], vb[b0:b0 + HD, g * HD:(g + 1) * HD], preferred_element_type=F32)
                 + bst_ref[:, g:g + 1] for g in range(NH)], axis=1))
        mixed = jnp.concatenate(rows, axis=0) if len(rows) > 1 else rows[0]
        yb_ref[...] = (gu * mixed).astype(BF16)
        gu_ref[...] = gu.astype(BF16)
        mg_ref[...] = (mixed * _gelu_grad(su)).astype(BF16)
        vh_ref[...] = vhat.astype(BF16)
        gpv_ref[...] = _gelu_grad(sv).astype(BF16)
        rstd_ref[...] = rstd

    vec = _const_spec((1, D))
    tile = pl.BlockSpec((tm, D), lambda i: (i, 0))
    bshape = jax.ShapeDtypeStruct((t, D), BF16)
    return pl.pallas_call(
        body, name="sgu_fwd", grid=(t // tm,),
        in_specs=[pl.BlockSpec((tm, D), lambda i: (i, 2)), pl.BlockSpec((tm, D), lambda i: (i, 3)), vec, vec,
                  _const_spec((NH, HD, HD)), _const_spec((HD, NH))],
        out_specs=[tile] * 5 + [pl.BlockSpec((tm, 1), lambda i: (i, 0))],
        out_shape=[bshape] * 5 + [jax.ShapeDtypeStruct((t, 1), F32)],
        compiler_params=_cparams(1),
    )(z, z, lng, lnb, wm, bst)


def _merge_fwd(ya_pre, yb_pre, z, x, gate1, wba, wbb, wout):
    t = x.shape[0]
    tm = _tile_big(t)

    def body(yap_ref, ybp_ref, ga_ref, gb_ref, x_ref, g1_ref, wba_ref, wbb_ref, wo_ref,
             x2_ref, ya_ref, yb_ref, mg_ref, o1_ref):
        ya = jnp.dot(yap_ref[...], wba_ref[...], preferred_element_type=F32)
        yb = jnp.dot(ybp_ref[...], wbb_ref[...], preferred_element_type=F32)
        merged = _sigmoid(ga_ref[...].astype(F32)) * ya + _sigmoid(gb_ref[...].astype(F32)) * yb
        mb = merged.astype(BF16)
        o1 = jnp.dot(mb, wo_ref[...], preferred_element_type=F32)
        x2_ref[...] = x_ref[...] + g1_ref[...] * o1
        ya_ref[...] = ya.astype(BF16)
        yb_ref[...] = yb.astype(BF16)
        mg_ref[...] = mb
        o1_ref[...] = o1.astype(BF16)

    tile = pl.BlockSpec((tm, D), lambda i: (i, 0))
    wspec = _const_spec((D, D))
    bshape = jax.ShapeDtypeStruct((t, D), BF16)
    gate1, g1_spec = _vec_operand(gate1)
    return pl.pallas_call(
        body, name="merge_fwd", grid=(t // tm,),
        in_specs=[tile, tile, pl.BlockSpec((tm, D), lambda i: (i, 4)), pl.BlockSpec((tm, D), lambda i: (i, 5)),
                  tile, g1_spec, wspec, wspec, wspec],
        out_specs=[tile] * 5,
        out_shape=[jax.ShapeDtypeStruct((t, D), F32), bshape, bshape, bshape, bshape],
        compiler_params=_cparams(1),
    )(ya_pre, yb_pre, z, z, x, gate1, wba, wbb, wout)


def _conv3(u, prev8, cw_ref, cb):
    return cb + cw_ref[2:3, :] * u + cw_ref[1:2, :] * _shift_down(u, 1, prev8) + cw_ref[0:1, :] * _shift_down(u, 2, prev8)


def _ffn_proj_mid(x2, g, scale, shift, w, cw, cb):
    t = x2.shape[0]
    tm = _tile_big(t)
    nc = DFF // D

    def body(x_ref, g_ref, sc_ref, sh_ref, wa_ref, wv_ref, cwa_ref, cwv_ref, cba_ref, cbv_ref,
             h_ref, upa_ref, upv_ref, ff_ref, fa_ref, fv_ref, hb_scr, prev_ref):
        i, c = pl.program_id(0), pl.program_id(1)

        @pl.when(i == 0)
        def _():
            prev_ref[c] = jnp.zeros((2, SUBLANES, D), F32)

        @pl.when(c == 0)
        def _():
            xv = x_ref[...]
            r = lax.rsqrt(jnp.mean(xv * xv, axis=-1, keepdims=True) + EPS)
            hb_scr[...] = ((xv * r * g_ref[...]) * (1.0 + sc_ref[...]) + sh_ref[...]).astype(BF16)
            h_ref[...] = hb_scr[...]

        hb = hb_scr[...]
        halves = []
        for s, (w_ref, up_ref, cw_ref, cb_ref) in enumerate(((wa_ref, upa_ref, cwa_ref, cba_ref),
                                                             (wv_ref, upv_ref, cwv_ref, cbv_ref))):
            u = jnp.dot(hb, w_ref[...], preferred_element_type=F32)
            up_ref[...] = u.astype(BF16)
            halves.append(_conv3(u, prev_ref[c, s], cw_ref, cb_ref[...]))
            prev_ref[c, s] = u[tm - SUBLANES:]
        act, val = halves
        ga, ta = _gelu_t(act)
        ff_ref[...] = (ga * val).astype(BF16)
        fa_ref[...] = (val * _gelu_grad(ta)).astype(BF16)
        fv_ref[...] = ga.astype(BF16)

    def cols(rows, off):
        return pl.BlockSpec((rows, D), lambda i, c: (0, off + c))

    vec = pl.BlockSpec((1, D), lambda i, c: (0, 0))
    row_tile = pl.BlockSpec((tm, D), lambda i, c: (i, 0))
    chunk = pl.BlockSpec((tm, D), lambda i, c: (i, c))
    hshape = jax.ShapeDtypeStruct((t, DFF), BF16)
    (scale, sc_spec), (shift, sh_spec) = _vec_operand(scale), _vec_operand(shift)
    return pl.pallas_call(
        body, name="ffn_proj_mid", grid=(t // tm, nc),
        in_specs=[row_tile, vec, sc_spec, sh_spec, cols(D, 0), cols(D, nc), cols(3, 0), cols(3, nc), cols(1, 0), cols(1, nc)],
        out_specs=[row_tile, chunk, chunk, chunk, chunk, chunk],
        out_shape=[jax.ShapeDtypeStruct((t, D), BF16), hshape, hshape, hshape, hshape, hshape],
        scratch_shapes=[pltpu.VMEM((tm, D), BF16), pltpu.VMEM((nc, 2, SUBLANES, D), F32)],
        compiler_params=_cparams(2),
    )(x2, g, scale, shift, w, w, cw, cw, cb, cb)


def _ffn_out_loss(ff, wd, x2, target, gate2, gfin):
    t = x2.shape[0]
    tm = _tile_big(t)

    def body(ff_ref, wd_ref, x2_ref, tg_ref, g2_ref, gf_ref, dx3_ref, do2_ref, loss_ref, dgf_ref, dg2_ref):
        @pl.when(pl.program_id(0) == 0)
        def _():
            loss_ref[...] = jnp.zeros_like(loss_ref)
            dgf_ref[...] = jnp.zeros_like(dgf_ref)
            dg2_ref[...] = jnp.zeros_like(dg2_ref)

        o2 = jnp.dot(ff_ref[...], wd_ref[...], preferred_element_type=F32)
        x3 = x2_ref[...] + g2_ref[...] * o2
        r = lax.rsqrt(jnp.mean(x3 * x3, axis=-1, keepdims=True) + EPS)
        xhat = x3 * r
        err = xhat * gf_ref[...] - tg_ref[...]
        loss_ref[...] += 0.5 * jnp.sum(jnp.mean(err * err, axis=-1, keepdims=True), axis=0, keepdims=True)
        dy = err * (1.0 / D)
        dgf_ref[...] += _colsum(dy * xhat)
        dxh = dy * gf_ref[...]
        dx3 = r * (dxh - xhat * jnp.mean(dxh * xhat, axis=-1, keepdims=True))
        dx3_ref[...] = dx3
        do2_ref[...] = (dx3 * g2_ref[...]).astype(BF16)
        dg2_ref[...] += _colsum(dx3 * o2)

    tile = pl.BlockSpec((tm, D), lambda i: (i, 0))
    vec = _const_spec((1, D))
    gate2, g2_spec = _vec_operand(gate2)
    return pl.pallas_call(
        body, name="ffn_out_loss", grid=(t // tm,),
        in_specs=[pl.BlockSpec((tm, DFF), lambda i: (i, 0)), _const_spec((DFF, D), True), tile, tile, g2_spec, vec],
        out_specs=[tile, tile, _const_spec((1, 1)), vec, vec],
        out_shape=[jax.ShapeDtypeStruct((t, D), F32), jax.ShapeDtypeStruct((t, D), BF16),
                   jax.ShapeDtypeStruct((1, 1), F32),
                   jax.ShapeDtypeStruct((1, D), F32), jax.ShapeDtypeStruct((1, D), F32)],
        compiler_params=_cparams(1),
    )(ff, wd, x2, target, gate2, gfin)


def _ffn_down_bwd(do2, ff, fa, fv, wd):
    t = do2.shape[0]
    tm = min(1024, t)
    nc = DFF // D

    def body(do2_ref, ff_ref, fa_ref, fv_ref, wd_ref, da_ref, dv_ref, dwd_ref, dcba_ref, dcbv_ref):
        @pl.when(pl.program_id(1) == 0)
        def _():
            for r in (dwd_ref, dcba_ref, dcbv_ref):
                r[...] = jnp.zeros_like(r)

        do2 = do2_ref[...]
        dwd_ref[...] += _dot_tn(ff_ref[...], do2)
        dff = _dot_nt(do2, wd_ref[...])
        dact = dff * fa_ref[...].astype(F32)
        dval = dff * fv_ref[...].astype(F32)
        da_ref[...] = dact.astype(BF16)
        dv_ref[...] = dval.astype(BF16)
        dcba_ref[...] += _colsum(dact)
        dcbv_ref[...] += _colsum(dval)

    blk = pl.BlockSpec((tm, D), lambda c, i: (i, c))
    vec = pl.BlockSpec((1, D), lambda c, i: (0, c))
    return pl.pallas_call(
        body, name="ffn_down_bwd", grid=(nc, t // tm),
        in_specs=[pl.BlockSpec((tm, D), lambda c, i: (i, 0)),
                  blk, blk, blk, pl.BlockSpec((D, D), lambda c, i: (c, 0))],
        out_specs=[blk, blk, pl.BlockSpec((D, D), lambda c, i: (c, 0)), vec, vec],
        out_shape=[jax.ShapeDtypeStruct((t, DFF), BF16), jax.ShapeDtypeStruct((t, DFF), BF16),
                   jax.ShapeDtypeStruct((DFF, D), F32),
                   jax.ShapeDtypeStruct((1, DFF), F32), jax.ShapeDtypeStruct((1, DFF), F32)],
        compiler_params=_cparams(2),
    )(do2, ff, fa, fv, wd)


def _modnorm_bwd(dh, xv, g, scale):
    r = lax.rsqrt(jnp.mean(xv * xv, axis=-1, keepdims=True) + EPS)
    xhat = xv * r
    dxn = dh * (1.0 + scale)
    dxh = dxn * g
    dx = r * (dxh - xhat * jnp.mean(dxh * xhat, axis=-1, keepdims=True))
    return dx, _colsum(dh), _colsum(dh * (xhat * g)), _colsum(dxn * xhat)


def _ffn_up_bwd(dact, dval, up_a, up_v, cw, wup, x2, dx3, gffn, scale2, o1, gate1):
    t = x2.shape[0]
    tm = _tile_seq(t)
    nt = t // tm
    nc = DFF // D

    def body(da_ref, dan_ref, dv_ref, dvn_ref, ua_ref, uv_ref, cw_ref, w_ref, x2_ref, dx3_ref, g_ref, sc_ref, o1_ref, g1_ref,
             dup_ref, dx2_ref, do1_ref, dcw_ref, dsh_ref, dsc_ref, dg_ref, dg1_ref):
        i = pl.program_id(0)

        @pl.when(i == 0)
        def _():
            for r in (dcw_ref, dsh_ref, dsc_ref, dg_ref, dg1_ref):
                r[...] = jnp.zeros_like(r)

        last = i == nt - 1
        dh = jnp.zeros((tm, D), F32)
        for half, (d_ref, dn_ref, u_ref) in enumerate(((da_ref, dan_ref, ua_ref), (dv_ref, dvn_ref, uv_ref))):
            nxt = jnp.where(last, 0.0, dn_ref[...].astype(F32)[:SUBLANES])
            for c in range(nc):
                c0 = half * DFF + c * D
                dv = d_ref[:, c * D:(c + 1) * D].astype(F32)
                nx = nxt[:, c * D:(c + 1) * D]
                taps = (_shift_up(dv, 2, nx), _shift_up(dv, 1, nx), dv)
                dup = (cw_ref[2:3, c0:c0 + D] * taps[2] + cw_ref[1:2, c0:c0 + D] * taps[1]
                       + cw_ref[0:1, c0:c0 + D] * taps[0]).astype(BF16)
                upv = u_ref[:, c * D:(c + 1) * D].astype(F32)
                for k in range(3):
                    dcw_ref[k:k + 1, c0:c0 + D] += _colsum(taps[k] * upv)
                dup_ref[:, c0:c0 + D] = dup
                dh = dh + _dot_nt(dup, w_ref[:, c0:c0 + D])
        dxn, dsh, dsc, dg = _modnorm_bwd(dh, x2_ref[...], g_ref[...], sc_ref[...])
        dx2 = dx3_ref[...] + dxn
        dx2_ref[...] = dx2
        do1_ref[...] = (dx2 * g1_ref[...]).astype(BF16)
        dsh_ref[...] += dsh
        dsc_ref[...] += dsc
        dg_ref[...] += dg
        dg1_ref[...] += _colsum(dx2 * o1_ref[...].astype(F32))

    tile = pl.BlockSpec((tm, D), lambda i: (i, 0))
    wide = pl.BlockSpec((tm, DFF), lambda i: (i, 0))
    nxt = pl.BlockSpec((HALO, DFF), lambda i: (jnp.minimum((i + 1) * (tm // HALO), t // HALO - 1), 0))
    vec = _const_spec((1, D))
    vshape = jax.ShapeDtypeStruct((1, D), F32)
    (scale2, sc_spec), (gate1, g1_spec) = _vec_operand(scale2), _vec_operand(gate1)
    return pl.pallas_call(
        body, name="ffn_up_bwd", grid=(nt,),
        in_specs=[wide, nxt, wide, nxt, wide, wide,
                  _const_spec((3, 2 * DFF)), _const_spec((D, 2 * DFF), True),
                  tile, tile, vec, sc_spec, tile, g1_spec],
        out_specs=[pl.BlockSpec((tm, 2 * DFF), lambda i: (i, 0)), tile, tile, _const_spec((3, 2 * DFF)),
                   vec, vec, vec, vec],
        out_shape=[jax.ShapeDtypeStruct((t, 2 * DFF), BF16), jax.ShapeDtypeStruct((t, D), F32),
                   jax.ShapeDtypeStruct((t, D), BF16), jax.ShapeDtypeStruct((3, 2 * DFF), F32),
                   vshape, vshape, vshape, vshape],
        compiler_params=_cparams(1),
    )(dact, dact, dval, dval, up_a, up_v, cw, wup, x2, dx3, gffn, scale2, o1, gate1)


def _xt_y(a, b, name):
    t, k = a.shape
    n = b.shape[1]
    tm = min(1024, t)
    bn = 3072 if n % 3072 == 0 else D

    def body(a_ref, b_ref, o_ref):
        @pl.when(pl.program_id(1) == 0)
        def _():
            o_ref[...] = jnp.zeros_like(o_ref)

        o_ref[...] += _dot_tn(a_ref[...], b_ref[...])

    return pl.pallas_call(
        body, name=name, grid=(n // bn, t // tm),
        in_specs=[pl.BlockSpec((tm, k), lambda j, i: (i, 0)), pl.BlockSpec((tm, bn), lambda j, i: (i, j))],
        out_specs=pl.BlockSpec((k, bn), lambda j, i: (0, j)),
        out_shape=jax.ShapeDtypeStruct((k, n), F32),
        compiler_params=_cparams(2),
    )(a, b)


def _acc_spec(shape, index):
    return pl.BlockSpec(shape, lambda *_: index, pipeline_mode=pl.Buffered(1))


def _out_bwd(do1, wout, merged, ya, yb, z, h1):
    t = do1.shape[0]
    tm = _tile_big(t)

    def body(do1_ref, wo_ref, mg_ref, ya_ref, yb_ref, ga_ref, gb_ref, h1_ref,
             dya_ref, dyb_ref, dz_ref, dwo_ref, dwin_ref):
        @pl.when(pl.program_id(0) == 0)
        def _():
            dwo_ref[...] = jnp.zeros_like(dwo_ref)
            dwin_ref[...] = jnp.zeros_like(dwin_ref)

        do1v = do1_ref[...]
        dwo_ref[...] += _dot_tn(mg_ref[...], do1v)
        dm = _dot_nt(do1v, wo_ref[...])
        sa = _sigmoid(ga_ref[...].astype(F32))
        sb = _sigmoid(gb_ref[...].astype(F32))
        dya_ref[...] = (dm * sa).astype(BF16)
        dyb_ref[...] = (dm * sb).astype(BF16)
        dga = (dm * ya_ref[...].astype(F32) * sa * (1.0 - sa)).astype(BF16)
        dgb = (dm * yb_ref[...].astype(F32) * sb * (1.0 - sb)).astype(BF16)
        dz_ref[:, 0:D] = dga
        dz_ref[:, D:2 * D] = dgb
        h1v = h1_ref[...]
        dwin_ref[:, 0:D] += _dot_tn(h1v, dga)
        dwin_ref[:, D:2 * D] += _dot_tn(h1v, dgb)

    tile = pl.BlockSpec((tm, D), lambda i: (i, 0))
    bshape = jax.ShapeDtypeStruct((t, D), BF16)
    return pl.pallas_call(
        body, name="out_bwd", grid=(t // tm,),
        in_specs=[tile, _const_spec((D, D), True), tile, tile, tile,
                  pl.BlockSpec((tm, D), lambda i: (i, 4)), pl.BlockSpec((tm, D), lambda i: (i, 5)), tile],
        out_specs=[tile, tile, pl.BlockSpec((tm, 2 * D), lambda i: (i, 2)), _acc_spec((D, D), (0, 0)),
                   _acc_spec((D, 2 * D), (0, 2))],
        out_shape=[bshape, bshape, jax.ShapeDtypeStruct((t, NCOL_IN), BF16), jax.ShapeDtypeStruct((D, D), F32),
                   jax.ShapeDtypeStruct((D, NCOL_IN), F32)],
        compiler_params=_cparams(1),
    )(do1, wout, merged, ya, yb, z, z, h1)


def _rnn_bwd(dya, ya_pre, wba, h1, z, saved, h, dz, dwin, cw, wa, wx, lam):
    t = z.shape[0]
    tm = _tile_seq(t)
    nt = t // tm
    ngrp = tm // SUBLANES
    hpt = tm // HALO

    def body(dya_ref, yap_ref, wba_ref, h1_ref, xr_ref, xc_ref, ra_ref, ia_ref, gg_ref, hg_ref, h_ref, hp_ref,
             dz_any, dwin_any, cw_ref, wa_ref, wx_ref, lam_ref,
             dz_ref, dwin_ref, dwba_ref, dcw_ref, dcb_ref, dwa_ref, dba_ref, dwx_ref, dbx_ref, dlam_ref,
             a_first, g_first, dxc_first, b_scr, d_scr, g_scr):
        del dz_any, dwin_any
        i = pl.program_id(0)

        @pl.when(i == 0)
        def _():
            for r in (dwin_ref, dwba_ref, dcw_ref, dcb_ref, dwa_ref, dba_ref, dwx_ref, dbx_ref, dlam_ref,
                      a_first, g_first, dxc_first):
                r[...] = jnp.zeros_like(r)

        dya_v = dya_ref[...]
        dwba_ref[...] += _dot_tn(yap_ref[...], dya_v)
        dyap_v = _dot_nt(dya_v, wba_ref[...])
        h1v = h1_ref[...]

        first_tile = i == nt - 1
        xc = xc_ref[...].astype(F32)
        ra = ra_ref[...].astype(F32)
        ia = ia_ref[...].astype(F32)
        lam_v = lam_ref[...]
        ls = _log_sigmoid(lam_v)
        la = LRU_C * ra * ls
        a = jnp.exp(la)
        mult = jnp.sqrt(-jnp.tanh(la) * (1.0 + a * a))
        hprev8 = jnp.where(first_tile, 0.0, hp_ref[...][HALO - SUBLANES:])
        h_prev = _shift_down(h_ref[...], 1, hprev8)
        dgr = (dyap_v * hg_ref[...].astype(F32)).astype(BF16)
        dz_ref[:, D:2 * D] = dgr
        dwin_ref[:, D:2 * D] += _dot_tn(h1v, dgr)

        b_scr[...] = _shift_up(a, 1, a_first[...])
        d_scr[...] = dyap_v * gg_ref[...].astype(F32)
        row = _row_iota(D)

        def grp(jj, carry):
            r0 = pl.multiple_of((ngrp - 1 - jj) * SUBLANES, SUBLANES)
            bv = b_scr[pl.ds(r0, SUBLANES), :]
            dv = d_scr[pl.ds(r0, SUBLANES), :]
            for d in (1, 2, 4):
                m = row < SUBLANES - d
                dv = jnp.where(m, dv + bv * pltpu.roll(dv, SUBLANES - d, 0), dv)
                bv = jnp.where(m, bv * pltpu.roll(bv, SUBLANES - d, 0), bv)
            gv = dv + bv * carry
            g_scr[pl.ds(r0, SUBLANES), :] = gv
            return gv[0:1, :]

        lax.fori_loop(0, ngrp, grp, g_first[0:1, :])
        g = g_scr[...]
        a_first[...] = a[:SUBLANES]
        g_first[...] = g[:SUBLANES]

        da = g * h_prev
        gx = g * xc
        dmult = gx * ia
        dia = gx * mult
        dxc = g * (mult * ia)
        dla = da * a - dmult * (a * a) / mult
        dra = dla * (LRU_C * ls)
        dlam_ref[...] += _colsum(dla * ra) * (LRU_C * _sigmoid(-lam_v))
        dpa = dra * ra * (1.0 - ra)
        dpx = dia * ia * (1.0 - ia)
        dba_ref[...] += _colsum(dpa)
        dbx_ref[...] += _colsum(dpx)
        dpab = dpa.astype(BF16)
        dpxb = dpx.astype(BF16)
        xcb = xc_ref[...]
        for hd in range(NH):
            sl = slice(hd * HD, (hd + 1) * HD)
            dwa_ref[hd] += _dot_tn(xcb[:, sl], dpab[:, sl])
            dwx_ref[hd] += _dot_tn(xcb[:, sl], dpxb[:, sl])
        dxc = dxc + _heads_nt(dpab, wa_ref) + _heads_nt(dpxb, wx_ref)

        nxt = dxc_first[...]
        taps = (_shift_up(dxc, 3, nxt), _shift_up(dxc, 2, nxt), _shift_up(dxc, 1, nxt), dxc)
        dxr = cw_ref[0:1, :] * taps[0]
        for k in range(1, 4):
            dxr = dxr + cw_ref[k:k + 1, :] * taps[k]
        dxrb = dxr.astype(BF16)
        dz_ref[:, 0:D] = dxrb
        dwin_ref[:, 0:D] += _dot_tn(h1v, dxrb)
        dxc_first[...] = dxc[:SUBLANES]
        dcb_ref[...] += _colsum(dxc)
        xr = xr_ref[...].astype(F32)
        for k in range(4):
            dcw_ref[k:k + 1, :] += _colsum(taps[k] * xr)

    def rev(col):
        return lambda i: (nt - 1 - i, col)

    vec = _const_spec((1, D))
    wspec = _const_spec((NH, HD, HD))
    vshape = jax.ShapeDtypeStruct((1, D), F32)
    wshape = jax.ShapeDtypeStruct((NH, HD, HD), F32)
    any_spec = pl.BlockSpec(memory_space=pl.ANY)
    tile = pl.BlockSpec((tm, D), rev(0))
    outs = pl.pallas_call(
        body, name="rnn_bwd", grid=(nt,),
        in_specs=[tile, tile, _const_spec((D, D), True), tile, tile, tile, tile, tile, tile, tile, tile,
                  pl.BlockSpec((HALO, D), lambda i: (jnp.maximum((nt - 1 - i) * hpt - 1, 0), 0)),
                  any_spec, any_spec, _const_spec((4, D)), wspec, wspec, vec],
        out_specs=[pl.BlockSpec((tm, 2 * D), rev(0)), _acc_spec((D, 2 * D), (0, 0)), _acc_spec((D, D), (0, 0)),
                   _const_spec((4, D)), vec, wspec, vec, wspec, vec, vec],
        out_shape=[jax.ShapeDtypeStruct((t, NCOL_IN), BF16), jax.ShapeDtypeStruct((D, NCOL_IN), F32),
                   jax.ShapeDtypeStruct((D, D), F32), jax.ShapeDtypeStruct((4, D), F32), vshape,
                   wshape, vshape, wshape, vshape, vshape],
        scratch_shapes=[pltpu.VMEM((SUBLANES, D), F32), pltpu.VMEM((SUBLANES, D), F32), pltpu.VMEM((SUBLANES, D), F32),
                        pltpu.VMEM((tm, D), F32), pltpu.VMEM((tm, D), F32), pltpu.VMEM((tm, D), F32)],
        input_output_aliases={12: 0, 13: 1},
        compiler_params=_cparams(1),
    )(dya, ya_pre, wba, h1, z, *saved, h, h, dz, dwin, cw, wa, wx, lam)
    return outs


def _sgu_bwd(dyb, yb_pre, wbb, h1, saved, dz, dwin, lng, lnb, wmt, mask):
    t = dyb.shape[0]
    tm = _tile_big(t)

    def body(dyb_ref, ybp_ref, wbb_ref, h1_ref, gu_ref, mg_ref, vh_ref, gpv_ref, rstd_ref, dz_any, dwin_any,
             lng_ref, lnb_ref, wmt_ref, mask_ref,
             dz_ref, dwin_ref, dwbb_ref, dws_ref, dbst_ref, dlng_ref, dlnb_ref):
        del dz_any, dwin_any

        @pl.when(pl.program_id(0) == 0)
        def _():
            for r in (dwin_ref, dwbb_ref, dws_ref, dbst_ref, dlng_ref, dlnb_ref):
                r[...] = jnp.zeros_like(r)

        lng_v = lng_ref[...]
        vhat = vh_ref[...].astype(F32)
        vb = (vhat * lng_v + lnb_ref[...]).astype(BF16)
        rstd = rstd_ref[...]
        dyb_v = dyb_ref[...]
        dwbb_ref[...] += _dot_tn(ybp_ref[...], dyb_v)
        dyb = _dot_nt(dyb_v, wbb_ref[...])
        h1v = h1_ref[...]
        dzu = (dyb * mg_ref[...].astype(F32)).astype(BF16)
        dz_ref[:, 0:D] = dzu
        dwin_ref[:, 0:D] += _dot_tn(h1v, dzu)
        dmix = dyb * gu_ref[...].astype(F32)
        dmb = dmix.astype(BF16)
        rows = []
        lane = lax.broadcasted_iota(jnp.int32, (HD, NH), 1)
        dbst = jnp.zeros((HD, NH), F32)
        for b0 in range(0, tm, HD):
            cols = []
            for g in range(NH):
                sl = slice(g * HD, (g + 1) * HD)
                dmg = dmb[b0:b0 + HD, sl]
                dws_ref[g] += _dot_nt(dmg, vb[b0:b0 + HD, sl]) * mask_ref[...]
                cols.append(jnp.dot(wmt_ref[g], dmg, preferred_element_type=F32))
                dbst = dbst + jnp.where(lane == g, jnp.sum(dmix[b0:b0 + HD, sl], axis=1, keepdims=True), 0.0)
            rows.append(jnp.concatenate(cols, axis=1))
        dbst_ref[...] += dbst
        dvln = jnp.concatenate(rows, axis=0) if len(rows) > 1 else rows[0]
        dlng_ref[...] += _colsum(dvln * vhat)
        dlnb_ref[...] += _colsum(dvln)
        dvh = dvln * lng_v
        dgv = rstd * (dvh - jnp.mean(dvh, axis=-1, keepdims=True)
                      - vhat * jnp.mean(dvh * vhat, axis=-1, keepdims=True))
        dzv = (dgv * gpv_ref[...].astype(F32)).astype(BF16)
        dz_ref[:, D:2 * D] = dzv
        dwin_ref[:, D:2 * D] += _dot_tn(h1v, dzv)

    vec = _const_spec((1, D))
    wspec = _const_spec((NH, HD, HD))
    vshape = jax.ShapeDtypeStruct((1, D), F32)
    tile = pl.BlockSpec((tm, D), lambda i: (i, 0))
    any_spec = pl.BlockSpec(memory_space=pl.ANY)
    return pl.pallas_call(
        body, name="sgu_bwd", grid=(t // tm,),
        in_specs=[tile, tile, _const_spec((D, D), True), tile, tile, tile, tile, tile,
                  pl.BlockSpec((tm, 1), lambda i: (i, 0)), any_spec, any_spec,
                  vec, vec, wspec, _const_spec((HD, HD))],
        out_specs=[pl.BlockSpec((tm, 2 * D), lambda i: (i, 1)), _acc_spec((D, 2 * D), (0, 1)), _acc_spec((D, D), (0, 0)),
                   wspec, _const_spec((HD, NH)), vec, vec],
        out_shape=[jax.ShapeDtypeStruct((t, NCOL_IN), BF16), jax.ShapeDtypeStruct((D, NCOL_IN), F32),
                   jax.ShapeDtypeStruct((D, D), F32), jax.ShapeDtypeStruct((NH, HD, HD), F32),
                   jax.ShapeDtypeStruct((HD, NH), F32), vshape, vshape],
        input_output_aliases={9: 0, 10: 1},
        compiler_params=_cparams(1),
    )(dyb, yb_pre, wbb, h1, *saved, dz, dwin, lng, lnb, wmt, mask)


def _in_bwd(dz, win, x, dx2, g, scale1):
    t = x.shape[0]
    tm = _tile_big(t)

    def body(dz_ref, w_ref, x_ref, dx2_ref, g_ref, sc_ref, dx_ref, dsh_ref, dsc_ref, dg_ref):
        @pl.when(pl.program_id(0) == 0)
        def _():
            for r in (dsh_ref, dsc_ref, dg_ref):
                r[...] = jnp.zeros_like(r)

        dh = jnp.zeros((tm, D), F32)
        for c0 in range(0, NCOL_IN, D):
            dh = dh + _dot_nt(dz_ref[:, c0:c0 + D], w_ref[:, c0:c0 + D])
        dxn, dsh, dsc, dg = _modnorm_bwd(dh, x_ref[...], g_ref[...], sc_ref[...])
        dx_ref[...] = dx2_ref[...] + dxn
        dsh_ref[...] += dsh
        dsc_ref[...] += dsc
        dg_ref[...] += dg

    tile = pl.BlockSpec((tm, D), lambda i: (i, 0))
    vec = _const_spec((1, D))
    vshape = jax.ShapeDtypeStruct((1, D), F32)
    scale1, sc_spec = _vec_operand(scale1)
    return pl.pallas_call(
        body, name="in_bwd", grid=(t // tm,),
        in_specs=[pl.BlockSpec((tm, NCOL_IN), lambda i: (i, 0)), _const_spec((D, NCOL_IN), True), tile, tile, vec,
                  sc_spec],
        out_specs=[tile, vec, vec, vec],
        out_shape=[jax.ShapeDtypeStruct((t, D), F32), vshape, vshape, vshape],
        compiler_params=_cparams(1),
    )(dz, win, x, dx2, g, scale1)


def _mod_cols(c_all, w_ada, b_cols):
    nb, cols = c_all.shape[0], w_ada.shape[1]

    def body(c_ref, w_ref, b_ref, o_ref):
        cv = c_ref[...]
        ca = (cv * _sigmoid(cv)).astype(BF16)
        o_ref[...] = jnp.dot(ca, w_ref[...].astype(BF16), preferred_element_type=F32) + b_ref[...]

    return pl.pallas_call(body, name="mod_cols", out_shape=jax.ShapeDtypeStruct((nb, cols), F32))(c_all, w_ada, b_cols)


def _ada_grad(c_all, dmod_cols):
    cols = dmod_cols.shape[1]

    def body(c_ref, d_ref, o_ref):
        cv = c_ref[...]
        ca = (cv * _sigmoid(cv)).astype(BF16)
        o_ref[...] = _dot_tn(ca, d_ref[...].astype(BF16))

    return pl.pallas_call(body, name="ada_grad", out_shape=jax.ShapeDtypeStruct((D, cols), F32))(c_all, dmod_cols)


def _adamw_update(w, m, v, g):
    bc1 = 1.0 - ADAM_B1 ** ADAM_STEP
    bc2 = 1.0 - ADAM_B2 ** ADAM_STEP
    mn = ADAM_B1 * m + (1.0 - ADAM_B1) * g
    vn = ADAM_B2 * v + (1.0 - ADAM_B2) * (g * g)
    return -ADAM_LR * ((mn / bc1) / (jnp.sqrt(vn / bc2) + ADAM_EPS) + ADAM_WD * w), mn, vn


def _adamw_group(names, ws, ms, vs, packs, name):
    n = len(names)
    starts, r0 = [], 0
    for w in ws:
        starts.append(r0)
        r0 += _pack_rows(w.shape)

    def body(*refs):
        w_refs, m_refs, v_refs, p_ref = refs[:n], refs[n:2 * n], refs[2 * n:3 * n], refs[3 * n]
        outs = refs[3 * n + 1:]
        for k in range(n):
            rows = _pack_rows(ws[k].shape)
            g = None
            for dev in range(N_DEV):
                if ws[k].shape[0] == 1:
                    term = jnp.concatenate(
                        [p_ref[dev, starts[k] + r:starts[k] + r + 1, :] for r in range(rows)], axis=1)
                else:
                    term = p_ref[dev, starts[k]:starts[k] + rows, :]
                g = term if g is None else g + term
            delta, mn, vn = _adamw_update(w_refs[k][...], m_refs[k][...], v_refs[k][...], g)
            for o_ref, val in zip(outs[4 * k:4 * k + 4], (g, delta, mn, vn)):
                o_ref[...] = val

    shapes = [jax.ShapeDtypeStruct(w.shape, F32) for w in ws for _ in range(4)]
    outs = pl.pallas_call(body, name=name, out_shape=shapes,
                          compiler_params=pltpu.CompilerParams(vmem_limit_bytes=VMEM_LIMIT))(*ws, *ms, *vs, packs)
    return {nm: tuple(outs[4 * k:4 * k + 4]) for k, nm in enumerate(names)}


def _adamw(w, m, v, parts, name):
    rows, cols = w.shape
    tr = _row_tile(rows, cols)
    stacked = [p.ndim == 3 for p in parts]

    def body(*refs):
        w_ref, m_ref, v_ref = refs[:3]
        p_refs = refs[3:3 + len(parts)]
        g_ref, d_ref, mo_ref, vo_ref = refs[3 + len(parts):]
        g = None
        for p_ref, st in zip(p_refs, stacked):
            terms = [p_ref[k].astype(F32) for k in range(p_ref.shape[0])] if st else [p_ref[...].astype(F32)]
            for term in terms:
                g = term if g is None else g + term
        delta, mn, vn = _adamw_update(w_ref[...], m_ref[...], v_ref[...], g)
        g_ref[...] = g
        mo_ref[...] = mn
        vo_ref[...] = vn
        d_ref[...] = delta

    tile = pl.BlockSpec((tr, cols), lambda i: (i, 0))
    p_specs = [pl.BlockSpec((p.shape[0], tr, cols), lambda i: (0, i, 0)) if st else tile for p, st in zip(parts, stacked)]
    shp = jax.ShapeDtypeStruct((rows, cols), F32)
    return pl.pallas_call(
        body, name=name, grid=(rows // tr,),
        in_specs=[tile, tile, tile] + p_specs, out_specs=[tile] * 4, out_shape=[shp] * 4,
        compiler_params=_cparams(1),
    )(w, m, v, *parts)


def _mesh_pos():
    return lax.axis_index("x"), lax.axis_index("y"), lax.axis_index("c")


def _other_chips(x, y):
    return [(1 - x, y), (x, 1 - y), (1 - x, 1 - y)]


def _block_of(ref, axis, index, size):
    if axis == 0:
        return ref.at[index]
    return ref.at[:, pl.ds(pl.multiple_of(index * size, 128), size)]


def _all_gather(shards, axes, name):
    n = len(shards)
    per = 7

    def body(*refs):
        ins, outs, done = refs[:n], refs[n:2 * n], refs[2 * n]
        send_sems, recv_sems, local_sems = refs[2 * n + 1:]
        x, y, c = _mesh_pos()
        me, sibling = (x, y, c), (x, y, 1 - c)
        chips = _other_chips(x, y)

        def rows(a, pos):
            return _block_of(outs[a], axes[a], 4 * pos[0] + 2 * pos[1] + pos[2], shards[a].shape[-1])

        def copy(a, k, block, to, src=None):
            return pltpu.make_async_remote_copy(
                src_ref=rows(a, block) if src is None else src, dst_ref=rows(a, block),
                send_sem=send_sems.at[a * per + k], recv_sem=recv_sems.at[a * per + k],
                device_id=to, device_id_type=MESH_IDS)

        mine = [pltpu.make_async_copy(ins[a], rows(a, me), local_sems.at[a]) for a in range(n)]
        for cp in mine:
            cp.start()
        first = []
        for a in range(n):
            first.append(copy(a, 0, me, sibling, src=ins[a]))
            first += [copy(a, 1 + j, me, (*chip, c), src=ins[a]) for j, chip in enumerate(chips)]
        for cp in first:
            cp.start()
        passed = []
        for j, chip in enumerate(chips):
            for a in range(n):
                copy(a, 1 + j, (*chip, c), me).wait_recv()
                fwd = copy(a, 4 + j, (*chip, c), sibling)
                fwd.start()
                passed.append(fwd)
        for a in range(n):
            copy(a, 0, sibling, me).wait_recv()
            for j, chip in enumerate(chips):
                copy(a, 4 + j, (*chip, 1 - c), me).wait_recv()
        for cp in first + passed:
            cp.wait_send()
        for cp in mine:
            cp.wait()
        done[...] = jnp.zeros_like(done)

    def full_shape(s, ax):
        return (N_DEV,) + s.shape if ax == 0 else s.shape[:-1] + (N_DEV * s.shape[-1],)

    any_spec = pl.BlockSpec(memory_space=pl.ANY)
    outs = pl.pallas_call(
        body, name=name,
        in_specs=[any_spec] * n, out_specs=[any_spec] * n + [pl.BlockSpec(memory_space=pltpu.VMEM)],
        out_shape=[jax.ShapeDtypeStruct(full_shape(s, ax), s.dtype) for s, ax in zip(shards, axes)]
        + [jax.ShapeDtypeStruct((SUBLANES, LANES), F32)],
        scratch_shapes=[pltpu.SemaphoreType.DMA((n * per,)), pltpu.SemaphoreType.DMA((n * per,)),
                        pltpu.SemaphoreType.DMA((n,))],
    )(*shards)
    return outs[:n], outs[n]


def _chip_blocks(x, y):
    return [(x, y)] + _other_chips(x, y)


def _sibling_reduce(gs, axis, name):
    g0, n = gs[0], len(gs)
    rows, cols = (g0.shape[1], g0.shape[2]) if axis == 0 else (g0.shape[0], g0.shape[1] // N_DEV)
    chunk = math.gcd(rows, 64)

    def body(*refs):
        g_refs, own_refs, pay_refs = refs[:n], refs[n:2 * n], refs[2 * n:3 * n]
        (stage_buf, send_buf, keep_buf, recv_buf, pay_buf,
         send_sems, recv_sems, stage_sems, keep_sems, out_sems) = refs[3 * n:]
        x, y, c = _mesh_pos()
        sibling = (x, y, 1 - c)
        chips = _chip_blocks(x, y)
        stage, keep, push = [], [], []
        for a in range(n):
            for j, (px, py) in enumerate(chips):
                s = 4 * a + j
                theirs = _block_of(g_refs[a], axis, 4 * px + 2 * py + (1 - c), cols)
                ours = _block_of(g_refs[a], axis, 4 * px + 2 * py + c, cols)
                stage.append(pltpu.make_async_copy(theirs, stage_buf.at[s], stage_sems.at[s]))
                keep.append(pltpu.make_async_copy(ours, keep_buf.at[s], keep_sems.at[s]))
                push.append(pltpu.make_async_remote_copy(
                    src_ref=send_buf.at[s], dst_ref=recv_buf.at[s], send_sem=send_sems.at[s],
                    recv_sem=recv_sems.at[s], device_id=sibling, device_id_type=MESH_IDS))
        for cp in stage + keep:
            cp.start()
        for s in range(4 * n):
            stage[s].wait()

            def narrow(r, carry, s=s):
                sl = pl.ds(pl.multiple_of(r * chunk, chunk), chunk)
                send_buf[s, sl, :] = stage_buf[s, sl, :].astype(BF16)
                return carry

            lax.fori_loop(0, rows // chunk, narrow, 0)
            push[s].start()
        written = []
        for s in range(4 * n):
            push[s].wait_recv()
            keep[s].wait()
            a, j = divmod(s, 4)
            res = keep_buf.at[s] if j == 0 else pay_buf.at[3 * a + j - 1]

            def add(r, carry, s=s, res=res):
                sl = pl.ds(pl.multiple_of(r * chunk, chunk), chunk)
                res[sl, :] = (keep_buf[s, sl, :] + recv_buf[s, sl, :].astype(F32)).astype(res.dtype)
                return carry

            lax.fori_loop(0, rows // chunk, add, 0)
            out = pltpu.make_async_copy(res, own_refs[a] if j == 0 else pay_refs[a].at[j - 1], out_sems.at[s])
            out.start()
            written.append(out)
        for cp in push:
            cp.wait_send()
        for cp in written:
            cp.wait()

    any_spec = pl.BlockSpec(memory_space=pl.ANY)
    buf = pltpu.VMEM((4 * n, rows, cols), F32)
    buf16 = pltpu.VMEM((4 * n, rows, cols), BF16)
    sems = pltpu.SemaphoreType.DMA((4 * n,))
    outs = pl.pallas_call(
        body, name=name,
        in_specs=[any_spec] * n, out_specs=[any_spec] * (2 * n),
        out_shape=[jax.ShapeDtypeStruct((rows, cols), F32)] * n + [jax.ShapeDtypeStruct((3, rows, cols), BF16)] * n,
        scratch_shapes=[buf, buf16, buf, buf16, pltpu.VMEM((3 * n, rows, cols), BF16),
                        sems, sems, sems, sems, sems],
        compiler_params=pltpu.CompilerParams(vmem_limit_bytes=VMEM_LIMIT),
    )(*gs)
    return list(zip(outs[:n], outs[n:]))


_HBM_SPEC = pl.BlockSpec(memory_space=pltpu.HBM)
_SEM_SPEC = pl.BlockSpec(memory_space=pltpu.SEMAPHORE)
_SIDE_EFFECT = pltpu.SideEffectType.DATAFLOW_SIDE_EFFECTING


def _exchange_start(name, srcs, lands, plan, n_copies):
    nb = len(srcs) + len(lands)

    def body(*refs):
        bufs, send_sems, recv_sems, token = refs[:nb], refs[nb], refs[nb + 1], refs[-1]
        for cp in plan(bufs[:len(srcs)], bufs[len(srcs):], send_sems, recv_sems):
            cp.start()
        token[...] = jnp.zeros_like(token)

    arrays = list(srcs) + list(lands)
    outs = pl.pallas_call(
        body, name=name,
        out_shape=(pltpu.SemaphoreType.DMA((n_copies,)), pltpu.SemaphoreType.DMA((n_copies,)),
                   *[pltpu.HBM(a.shape, a.dtype) for a in arrays], jax.ShapeDtypeStruct((SUBLANES, LANES), F32)),
        in_specs=[_HBM_SPEC] * nb,
        out_specs=(_SEM_SPEC, _SEM_SPEC, *[_HBM_SPEC] * nb, pl.BlockSpec(memory_space=pltpu.VMEM)),
        input_output_aliases={k: 2 + k for k in range(nb)},
        compiler_params=pltpu.CompilerParams(has_side_effects=_SIDE_EFFECT),
    )(*[pltpu.with_memory_space_constraint(a, pltpu.HBM) for a in arrays])
    return outs[0], outs[1], outs[2:2 + len(srcs)], outs[2 + len(srcs):2 + nb], outs[-1]


def _exchange_wait(name, send_sems, recv_sems, srcs, lands, plan, after):
    nb = len(srcs) + len(lands)
    after = list(after)

    def body(*refs):
        bufs, send_ref, recv_ref = refs[:nb], refs[nb], refs[nb + 1]
        for cp in plan(bufs[:len(srcs)], bufs[len(srcs):], send_ref, recv_ref):
            cp.wait_send()
            cp.wait_recv()

    arrays = list(srcs) + list(lands)
    outs = pl.pallas_call(
        body, name=name,
        out_shape=tuple(pltpu.HBM(a.shape, a.dtype) for a in arrays),
        in_specs=[_HBM_SPEC] * nb + [_SEM_SPEC, _SEM_SPEC] + [pl.BlockSpec(memory_space=pl.ANY)] * len(after),
        out_specs=tuple([_HBM_SPEC] * nb),
        input_output_aliases={k: k for k in range(nb)},
        compiler_params=pltpu.CompilerParams(has_side_effects=_SIDE_EFFECT),
    )(*arrays, send_sems, recv_sems, *after)
    return outs[len(srcs):]


def _gather_plan(axes, sizes):
    def plan(src_refs, land_refs, send_sems, recv_sems):
        x, y, c = _mesh_pos()
        copies = []
        for a, (src, land) in enumerate(zip(src_refs, land_refs)):
            mine = _block_of(land, axes[a], 4 * x + 2 * y + c, sizes[a])
            for k in range(1, N_DEV):
                peer = (1 - x if k & 4 else x, 1 - y if k & 2 else y, 1 - c if k & 1 else c)
                idx = a * (N_DEV - 1) + k - 1
                copies.append(pltpu.make_async_remote_copy(
                    src_ref=src, dst_ref=mine, send_sem=send_sems.at[idx], recv_sem=recv_sems.at[idx],
                    device_id=peer, device_id_type=MESH_IDS))
        return copies
    return plan


def _chip_plan(src_refs, land_refs, send_sems, recv_sems):
    x, y, c = _mesh_pos()
    copies = []
    for a, (src, land) in enumerate(zip(src_refs, land_refs)):
        for j, chip in enumerate(_other_chips(x, y)):
            copies.append(pltpu.make_async_remote_copy(
                src_ref=src.at[j], dst_ref=land.at[j], send_sem=send_sems.at[3 * a + j],
                recv_sem=recv_sems.at[3 * a + j], device_id=(*chip, c), device_id_type=MESH_IDS))
    return copies


def _own_block_placed(shard, axis, me):
    if axis == 0:
        full = lax.empty((N_DEV,) + shard.shape, shard.dtype)
        return lax.dynamic_update_slice(full, shard[None], (me,) + (0,) * shard.ndim)
    rows, cols = shard.shape

    def body(me_ref, s_ref, o_ref):
        del me_ref
        o_ref[...] = s_ref[...]

    return pl.pallas_call(
        body, name="place_own_columns",
        grid_spec=pltpu.PrefetchScalarGridSpec(
            num_scalar_prefetch=1, grid=(1,),
            in_specs=[pl.BlockSpec((rows, cols), lambda i, me_ref: (0, 0))],
            out_specs=pl.BlockSpec((rows, cols), lambda i, me_ref: (0, me_ref[0]))),
        out_shape=jax.ShapeDtypeStruct((rows, N_DEV * cols), shard.dtype),
    )(jnp.reshape(me, (1,)).astype(jnp.int32), shard)


def _local_step(x, target, mod, win, late_weights, p, grads_ready=None):
    shift1, scale1, gate1, shift2, scale2, gate2 = ((mod, k) for k in range(6))

    def after_token(v, token):
        return v if token is None else v + token[0:1, 0:1]
    wa, wx = p["lru_w_a"].astype(BF16), p["lru_w_x"].astype(BF16)
    mask = jnp.tril(jnp.ones((HD, HD), F32))
    wm = (p["sgu_w_s"] * mask).astype(BF16)
    wmt = jnp.swapaxes(wm, 1, 2)
    bst = jnp.transpose(p["sgu_b_s"])

    h1, z = _norm_proj(x, p["norm_mix_g"], scale1, shift1, win, "mix_proj")
    hstate, ya_pre, *rnn_saved = _rnn_fwd(
        z, p["rnn_conv_w"], p["rnn_conv_b"], wa, p["lru_b_a"], wx, p["lru_b_x"], p["lru_lambda"])
    yb_pre, *sgu_saved = _sgu_fwd(z, p["sgu_ln_g"], p["sgu_ln_b"], wm, bst)
    wba, wbb, wout = late_weights("merge", [ya_pre, yb_pre])
    x2, ya, yb, merged, o1 = _merge_fwd(ya_pre, yb_pre, z, x, gate1, wba, wbb, wout)
    wup = late_weights("ffn_up", [x2])
    h2, up_a, up_v, ff, fa, fv = _ffn_proj_mid(
        x2, p["norm_ffn_g"], scale2, shift2, wup, p["ffn_conv_w"], p["ffn_conv_b"])
    wd = late_weights("ffn_down", [ff])
    dx3, do2, loss, d_gfin, d_gate2 = _ffn_out_loss(ff, wd, x2, target, gate2, p["norm_final_g"])

    dact, dval, d_wd, dcb_a, dcb_v = _ffn_down_bwd(do2, ff, fa, fv, wd)
    dup, dx2, do1, d_cwf, d_shift2, d_scale2, d_gffn, d_gate1 = _ffn_up_bwd(
        dact, dval, up_a, up_v, p["ffn_conv_w"], wup, x2, dx3, p["norm_ffn_g"], scale2, o1, gate1)
    d_wup = _xt_y(h2, dup, "w_up_grad")
    ready = grads_ready if grads_ready else (lambda stage, big, small: None)
    token = ready("ffn", {"w_up": d_wup, "w_down": d_wd}, {})

    dya, dyb, dz, d_wout, d_win = _out_bwd(do1, wout, merged, ya, yb, z, h1)
    dz, d_win, d_wba, d_cw, d_cb, d_wa, d_ba, d_wx, d_bx, d_lam = _rnn_bwd(
        dya, ya_pre, wba, h1, z, rnn_saved, hstate, dz, d_win, p["rnn_conv_w"], wa, wx,
        after_token(p["lru_lambda"], token))
    small = {
        "rnn_conv_w": d_cw, "rnn_conv_b": d_cb, "lru_w_a": d_wa, "lru_b_a": d_ba, "lru_w_x": d_wx, "lru_b_x": d_bx,
        "lru_lambda": d_lam, "norm_ffn_g": d_gffn, "ffn_conv_w": d_cwf,
        "ffn_conv_b": jnp.concatenate([dcb_a, dcb_v], axis=1), "norm_final_g": d_gfin,
    }
    token = ready("rnn", {}, small)
    dz, d_win, d_wbb, d_ws, d_bst, d_lng, d_lnb = _sgu_bwd(
        dyb, yb_pre, wbb, h1, sgu_saved, dz, d_win, p["sgu_ln_g"], after_token(p["sgu_ln_b"], token), wmt, mask)
    sgu_small = {"sgu_ln_g": d_lng, "sgu_ln_b": d_lnb, "sgu_w_s": d_ws, "sgu_b_s": jnp.transpose(d_bst)}
    mixer = {"w_in": d_win, "w_out": d_wout, "w_branch_a": d_wba, "w_branch_b": d_wbb}
    token = ready("mixer", mixer, sgu_small)
    grad_x, d_shift1, d_scale1, d_gmix = _in_bwd(dz, win, x, dx2, after_token(p["norm_mix_g"], token), scale1)

    small.update(sgu_small)
    small["norm_mix_g"] = d_gmix
    dmod = jnp.stack([d_shift1, d_scale1, d_gate1, d_shift2, d_scale2, d_gate2])
    big = {"w_in": d_win, "w_up": d_wup, "w_branch_a": d_wba, "w_branch_b": d_wbb, "w_out": d_wout, "w_down": d_wd}
    return loss, grad_x, big, small, dmod


LAST_REP = ["b_ada", "norm_mix_g"]
EARLY_REP = ["rnn_conv_b", "lru_w_a", "lru_b_a", "lru_w_x", "lru_b_x", "lru_lambda", "norm_ffn_g", "ffn_conv_b",
             "norm_final_g"]
MID_REP = ["sgu_ln_g", "sgu_ln_b", "sgu_w_s", "sgu_b_s"]
COL_SHARDED = ["rnn_conv_w", "ffn_conv_w"]
SMALL_GROUPS = {"rnn": EARLY_REP + COL_SHARDED, "mixer": MID_REP, "last": LAST_REP}
REPLICATED = LAST_REP + EARLY_REP + MID_REP
SMALL_NAMES = REPLICATED + COL_SHARDED
BIG_NAMES = ["w_in", "w_up", "w_branch_a", "w_branch_b", "w_out", "w_down"]
BIG_AXES = [1, 1, 0, 0, 0, 0]
WEIGHTS = ["w_ada", "b_ada", "norm_mix_g", "w_in", "rnn_conv_w", "rnn_conv_b", "lru_w_a", "lru_b_a", "lru_w_x",
           "lru_b_x", "lru_lambda", "sgu_ln_g", "sgu_ln_b", "sgu_w_s", "sgu_b_s", "w_branch_a", "w_branch_b",
           "w_out", "norm_ffn_g", "w_up", "ffn_conv_w", "ffn_conv_b", "w_down", "norm_final_g"]


def _pack_rows(shape):
    return math.prod(shape) // LANES


def _pack(arrays):
    return jnp.concatenate([a.reshape(-1, LANES) for a in arrays], axis=0)


def kernel(x, c, w_ada, b_ada, norm_mix_g, w_in, rnn_conv_w, rnn_conv_b, lru_w_a, lru_b_a, lru_w_x, lru_b_x, lru_lambda, sgu_ln_g, sgu_ln_b, sgu_w_s, sgu_b_s, w_branch_a, w_branch_b, w_out, norm_ffn_g, w_up, ffn_conv_w, ffn_conv_b, w_down, norm_final_g, loss_target, m_w_ada, m_b_ada, m_norm_mix_g, m_w_in, m_rnn_conv_w, m_rnn_conv_b, m_lru_w_a, m_lru_b_a, m_lru_w_x, m_lru_b_x, m_lru_lambda, m_sgu_ln_g, m_sgu_ln_b, m_sgu_w_s, m_sgu_b_s, m_w_branch_a, m_w_branch_b, m_w_out, m_norm_ffn_g, m_w_up, m_ffn_conv_w, m_ffn_conv_b, m_w_down, m_norm_final_g, v_w_ada, v_b_ada, v_norm_mix_g, v_w_in, v_rnn_conv_w, v_rnn_conv_b, v_lru_w_a, v_lru_b_a, v_lru_w_x, v_lru_b_x, v_lru_lambda, v_sgu_ln_g, v_sgu_ln_b, v_sgu_w_s, v_sgu_b_s, v_w_branch_a, v_w_branch_b, v_w_out, v_norm_ffn_g, v_w_up, v_ffn_conv_w, v_ffn_conv_b, v_w_down, v_norm_final_g):
    given = dict(locals())
    me = 4 * lax.axis_index("x") + 2 * lax.axis_index("y") + lax.axis_index("c")
    ada_cols = w_ada.shape[2]
    conv_cols = {"rnn_conv_w": rnn_conv_w.shape[2], "ffn_conv_w": ffn_conv_w.shape[2]}

    (win, c_all, cw_rnn, cw_ffn), _ = _all_gather(
        [w_in[0].astype(BF16), c.reshape(1, 1, D), rnn_conv_w[0], ffn_conv_w[0]], [1, 0, 1, 1], "gather_first")
    c_all = c_all.reshape(N_DEV, D)

    b_cols = lax.dynamic_slice_in_dim(b_ada, me * ada_cols, ada_cols, axis=1)
    (mod_all,), mod_done = _all_gather(
        [_mod_cols(c_all, w_ada[0], b_cols).reshape(1, N_DEV, ada_cols)], [0], "gather_mod")
    mod_all = mod_all.reshape(N_DEV, N_DEV, ada_cols)
    mod_mine = lax.dynamic_index_in_dim(mod_all, me, axis=1, keepdims=False).reshape(6, 1, D)

    late_groups = {"merge": (["w_branch_a", "w_branch_b", "w_out"], [0, 0, 0]), "ffn_up": (["w_up"], [1]),
                   "ffn_down": (["w_down"], [0])}
    in_flight, started = {}, mod_done[0:1, 0:1]
    for stage, (names, axes) in late_groups.items():
        shards = [(given[n][0] + started).astype(BF16) for n in names]
        plan = _gather_plan(axes, [s.shape[-1] for s in shards])
        send, recv, srcs, lands, token = _exchange_start(
            "gather_start_" + stage, shards, [_own_block_placed(s, ax, me) for s, ax in zip(shards, axes)], plan,
            len(shards) * (N_DEV - 1))
        in_flight[stage] = (send, recv, srcs, lands, plan)
        started = started + token[0:1, 0:1]

    def late_weights(stage, after):
        send, recv, srcs, lands, plan = in_flight[stage]
        full = _exchange_wait("gather_wait_" + stage, send, recv, srcs, lands, plan, after)
        full = [w.reshape(-1, D) if ax == 0 else w for w, ax in zip(full, late_groups[stage][1])]
        return full if len(full) > 1 else full[0]

    mod_mine = mod_mine + started

    reducing, packing = {}, {}

    def start_pack(stage, small):
        pack = _pack([small[n] for n in SMALL_GROUPS[stage]])[None]
        plan = _gather_plan([0], [LANES])
        send, recv, srcs, lands, tok = _exchange_start(
            "small_start_" + stage, [pack], [_own_block_placed(pack, 0, me)], plan, N_DEV - 1)
        packing[stage] = (send, recv, srcs, lands, plan)
        return tok

    def grads_ready(stage, grads, small):
        tokens = [start_pack(stage, small)] if small else []
        if grads:
            tokens.append(start_reduce(stage, grads))
        return sum(tokens[1:], tokens[0])

    def start_reduce(stage, grads):
        names = [n for n in BIG_NAMES if n in grads]
        blocked = {}
        for n in names:
            ax = BIG_AXES[BIG_NAMES.index(n)]
            g = grads[n] if ax == 1 else grads[n].reshape(N_DEV, grads[n].shape[0] // N_DEV, grads[n].shape[1])
            blocked.setdefault((ax, g.shape), []).append((n, g))
        sums = {}
        for (ax, _), group in blocked.items():
            reduced = _sibling_reduce([g for _, g in group], ax, "reduce_sibling_" + "_".join(n for n, _ in group))
            sums.update({n: r for (n, _), r in zip(group, reduced)})
        sums = [sums[n] for n in names]
        pays = [pay for _, pay in sums]
        send, recv, srcs, lands, tok = _exchange_start(
            "reduce_start_" + stage, pays, [lax.empty(p_.shape, p_.dtype) for p_ in pays], _chip_plan, 3 * len(pays))
        reducing[stage] = (names, [own for own, _ in sums], send, recv, srcs, lands)
        return tok

    p = {n: given[n][0] for n in REPLICATED if n not in ("b_ada", "norm_final_g")}
    p = {n: (a.reshape(1, -1) if a.ndim == 1 else a) for n, a in p.items()}
    p["rnn_conv_w"], p["ffn_conv_w"] = cw_rnn, cw_ffn
    p["norm_final_g"] = norm_final_g.reshape(1, D)
    loss, grad_x, _, small, dmod = _local_step(x[0], loss_target[0], mod_mine, win, late_weights, p, grads_ready)

    small["b_ada"] = dmod.reshape(1, 6 * D)
    rows_of = {n: _pack_rows(small[n].shape) for n in SMALL_NAMES}
    (last,), _ = _all_gather([_pack([small[n] for n in LAST_REP])[None]], [0], "gather_small")
    gathered = {"last": last}
    for stage, (send, recv, srcs, lands, plan) in packing.items():
        (gathered[stage],) = _exchange_wait("small_wait_" + stage, send, recv, srcs, lands, plan, [grad_x])
    gathered = {k: v.reshape(N_DEV, -1, LANES) for k, v in gathered.items()}

    out = {}
    for stage, (names, owns, send, recv, srcs, lands) in reducing.items():
        landed = _exchange_wait("reduce_wait_" + stage, send, recv, srcs, lands, _chip_plan, [last])
        for n, own, got in zip(names, owns, landed):
            out[n] = _adamw(given[n][0], given["m_" + n][0], given["v_" + n][0], [own, got], "adamw_" + n)

    dmod_all = gathered["last"][:, :rows_of["b_ada"]].reshape(N_DEV, 6 * D)
    dmod_cols = lax.dynamic_slice_in_dim(dmod_all, me * ada_cols, ada_cols, axis=1)
    out["w_ada"] = _adamw(w_ada[0], m_w_ada[0], v_w_ada[0], [_ada_grad(c_all, dmod_cols)], "adamw_w_ada")

    def rows_form(a):
        return a.reshape(1, -1) if a.size // a.shape[-1] == 1 or a.ndim == 1 else a.reshape(-1, LANES)

    for stage, names in (("last", LAST_REP), ("rnn", EARLY_REP), ("mixer", MID_REP)):
        out.update(_adamw_group(names, *[[rows_form(given[pre + n]) for n in names] for pre in ("", "m_", "v_")],
                                gathered[stage], "adamw_small_" + stage))

    row0 = sum(rows_of[n] for n in EARLY_REP)
    for n in COL_SHARDED:
        full = gathered["rnn"][:, row0:row0 + rows_of[n]].reshape(N_DEV, small[n].shape[0], small[n].shape[1])
        mine = lax.dynamic_slice_in_dim(full, me * conv_cols[n], conv_cols[n], axis=2)
        out[n] = _adamw(given[n][0], given["m_" + n][0], given["v_" + n][0], [mine], "adamw_" + n)
        row0 += rows_of[n]

    total = lax.psum(loss[0, 0], ("x", "y", "c"))
    results = [total, grad_x[None]]
    for kind in range(4):
        results += [out[n][kind].reshape(given[n].shape) for n in WEIGHTS]
    return tuple(results)
```

```python
import math

import jax
import jax.numpy as jnp
from jax import lax
from jax.experimental import pallas as pl
from jax.experimental.pallas import tpu as pltpu

F32 = jnp.float32
BF16 = jnp.bfloat16
MESH_IDS = pl.DeviceIdType.MESH

D = 1024
NH = 8
HD = 128
NCOL_IN = 6 * D
DFF = 3 * D
N_DEV = 8
EPS = 1e-6
LRU_C = 8.0
ADAM_LR, ADAM_B1, ADAM_B2, ADAM_EPS, ADAM_WD, ADAM_STEP = 0.001, 0.9, 0.999, 1e-08, 0.01, 10

SUBLANES = 8
LANES = 128
HALO = 16
VMEM_LIMIT = 56 * 1024 * 1024
GELU_K = math.sqrt(2.0 / math.pi)
GELU_C = 0.044715


def _cparams(n_axes):
    return pltpu.CompilerParams(dimension_semantics=("arbitrary",) * n_axes, vmem_limit_bytes=VMEM_LIMIT)


def _const_spec(shape, single_buffer=False):
    nd = len(shape)
    if single_buffer:
        return pl.BlockSpec(shape, lambda *_: (0,) * nd, pipeline_mode=pl.Buffered(1))
    return pl.BlockSpec(shape, lambda *_: (0,) * nd)


def _vec_operand(v):
    if isinstance(v, tuple):
        stack, k = v
        return stack, pl.BlockSpec((None, 1, D), lambda *_: (k, 0, 0))
    return v, _const_spec((1, D))


def _tile_big(t):
    return min(512, t)


def _tile_seq(t):
    return min(256, t)


def _row_tile(rows, cols):
    cap = max(SUBLANES, (2 * 1024 * 1024) // (4 * cols) // SUBLANES * SUBLANES)
    if rows <= cap:
        return rows
    return next(tr for tr in range(cap, 0, -SUBLANES) if rows % tr == 0)


def _gelu_t(x):
    x2 = x * x
    t = jnp.tanh(x * (GELU_K + (GELU_K * GELU_C) * x2))
    hx = 0.5 * x
    return hx + hx * t, (x2, hx, t)


def _gelu_grad(shared):
    x2, hx, t = shared
    return (0.5 + 0.5 * t) + (hx * (1.0 - t * t)) * (GELU_K + (3.0 * GELU_K * GELU_C) * x2)


def _sigmoid(x):
    return 1.0 / (1.0 + jnp.exp(-x))


def _log_sigmoid(x):
    return -(jnp.maximum(-x, 0.0) + jnp.log1p(jnp.exp(-jnp.abs(x))))


def _row_iota(cols):
    return lax.broadcasted_iota(jnp.int32, (SUBLANES, cols), 0)


def _shift_down(x, k, prev8):
    if k == 0:
        return x
    r = pltpu.roll(x, k, 0)
    p = pltpu.roll(prev8, k, 0)
    head = jnp.where(_row_iota(x.shape[1]) < k, p, r[:SUBLANES])
    return jnp.concatenate([head, r[SUBLANES:]], axis=0)


def _shift_up(x, k, next8):
    if k == 0:
        return x
    n = x.shape[0]
    r = pltpu.roll(x, n - k, 0)
    q = pltpu.roll(next8, SUBLANES - k, 0)
    tail = jnp.where(_row_iota(x.shape[1]) >= SUBLANES - k, q, r[n - SUBLANES:])
    return jnp.concatenate([r[:n - SUBLANES], tail], axis=0)


def _heads_nn(x_bf, w_ref):
    return jnp.concatenate(
        [jnp.dot(x_bf[:, h * HD:(h + 1) * HD], w_ref[h], preferred_element_type=F32) for h in range(NH)], axis=1)


def _heads_nt(x_bf, w_ref):
    return jnp.concatenate(
        [lax.dot_general(x_bf[:, h * HD:(h + 1) * HD], w_ref[h], (((1,), (1,)), ((), ())), preferred_element_type=F32)
         for h in range(NH)], axis=1)


def _dot_nt(a, b):
    return lax.dot_general(a, b, (((1,), (1,)), ((), ())), preferred_element_type=F32)


def _dot_tn(a, b):
    return lax.dot_general(a, b, (((0,), (0,)), ((), ())), preferred_element_type=F32)


def _colsum(x):
    return jnp.sum(x, axis=0, keepdims=True)


def _prev_halo_map(tm, col):
    return lambda i, *_: (jnp.maximum(i * (tm // HALO) - 1, 0), col)


def _norm_proj(x, g, scale, shift, w, name):
    t, n = x.shape[0], w.shape[1]
    tm = _tile_big(t)

    def body(x_ref, g_ref, sc_ref, sh_ref, w_ref, h_ref, z_ref):
        xv = x_ref[...]
        r = lax.rsqrt(jnp.mean(xv * xv, axis=-1, keepdims=True) + EPS)
        hb = ((xv * r * g_ref[...]) * (1.0 + sc_ref[...]) + sh_ref[...]).astype(BF16)
        h_ref[...] = hb
        for c0 in range(0, n, D):
            z_ref[:, c0:c0 + D] = jnp.dot(hb, w_ref[:, c0:c0 + D], preferred_element_type=F32).astype(BF16)

    vec = _const_spec((1, D))
    (scale, sc_spec), (shift, sh_spec) = _vec_operand(scale), _vec_operand(shift)
    return pl.pallas_call(
        body, name=name, grid=(t // tm,),
        in_specs=[pl.BlockSpec((tm, D), lambda i: (i, 0)), vec, sc_spec, sh_spec, _const_spec((D, n), True)],
        out_specs=[pl.BlockSpec((tm, D), lambda i: (i, 0)), pl.BlockSpec((tm, n), lambda i: (i, 0))],
        out_shape=[jax.ShapeDtypeStruct((t, D), BF16), jax.ShapeDtypeStruct((t, n), BF16)],
        compiler_params=_cparams(1),
    )(x, g, scale, shift, w)


def _lru_gates(xc, wa_ref, ba, wx_ref, bx, ls):
    xb = xc.astype(BF16)
    ra = _sigmoid(_heads_nn(xb, wa_ref) + ba)
    ia = _sigmoid(_heads_nn(xb, wx_ref) + bx)
    la = LRU_C * ra * ls
    a = jnp.exp(la)
    mult = jnp.sqrt(-jnp.tanh(la) * (1.0 + a * a))
    return ra, ia, a, mult


def _conv4(xr, prev8, cw_ref, cb):
    return (cb + cw_ref[3:4, :] * xr + cw_ref[2:3, :] * _shift_down(xr, 1, prev8)
            + cw_ref[1:2, :] * _shift_down(xr, 2, prev8) + cw_ref[0:1, :] * _shift_down(xr, 3, prev8))


def _rnn_fwd(z, cw, cb, wa, ba, wx, bx, lam):
    t = z.shape[0]
    tm = _tile_seq(t)
    ngrp = tm // SUBLANES

    def body(xr_ref, xp_ref, gr_ref, cw_ref, cb_ref, wa_ref, ba_ref, wx_ref, bx_ref, lam_ref,
             h_ref, ya_ref, xc_ref, ra_ref, ia_ref, gg_ref, hg_ref, carry_ref, a_scr, u_scr):
        i = pl.program_id(0)

        @pl.when(i == 0)
        def _():
            carry_ref[...] = jnp.zeros_like(carry_ref)

        xr = xr_ref[...].astype(F32)
        prev8 = jnp.where(i == 0, 0.0, xp_ref[...].astype(F32)[HALO - SUBLANES:])
        xc = _conv4(xr, prev8, cw_ref, cb_ref[...])
        ra, ia, a, mult = _lru_gates(xc, wa_ref, ba_ref[...], wx_ref, bx_ref[...], _log_sigmoid(lam_ref[...]))
        xc_ref[...] = xc.astype(BF16)
        ra_ref[...] = ra.astype(BF16)
        ia_ref[...] = ia.astype(BF16)
        a_scr[...] = a
        u_scr[...] = mult * (ia * xc)
        row = _row_iota(D)

        def grp(j, carry):
            r0 = pl.multiple_of(j * SUBLANES, SUBLANES)
            av = a_scr[pl.ds(r0, SUBLANES), :]
            uv = u_scr[pl.ds(r0, SUBLANES), :]
            for d in (1, 2, 4):
                m = row >= d
                uv = jnp.where(m, av * pltpu.roll(uv, d, 0) + uv, uv)
                av = jnp.where(m, av * pltpu.roll(av, d, 0), av)
            hv = uv + av * carry
            h_ref[pl.ds(r0, SUBLANES), :] = hv
            return hv[SUBLANES - 1:SUBLANES, :]

        carry_ref[0:1, :] = lax.fori_loop(0, ngrp, grp, carry_ref[0:1, :])
        grv = gr_ref[...].astype(F32)
        gg, tg = _gelu_t(grv)
        hv = h_ref[...]
        ya_ref[...] = (hv * gg).astype(BF16)
        gg_ref[...] = gg.astype(BF16)
        hg_ref[...] = (hv * _gelu_grad(tg)).astype(BF16)

    vec = _const_spec((1, D))
    wspec = _const_spec((NH, HD, HD))
    tile = pl.BlockSpec((tm, D), lambda i: (i, 0))
    bshape = jax.ShapeDtypeStruct((t, D), BF16)
    return pl.pallas_call(
        body, name="rnn_fwd", grid=(t // tm,),
        in_specs=[tile, pl.BlockSpec((HALO, D), _prev_halo_map(tm, 0)),
                  pl.BlockSpec((tm, D), lambda i: (i, 1)), _const_spec((4, D)), vec, wspec, vec, wspec, vec, vec],
        out_specs=[tile] * 7,
        out_shape=[jax.ShapeDtypeStruct((t, D), F32)] + [bshape] * 6,
        scratch_shapes=[pltpu.VMEM((SUBLANES, D), F32), pltpu.VMEM((tm, D), F32), pltpu.VMEM((tm, D), F32)],
        compiler_params=_cparams(1),
    )(z, z, z, cw, cb, wa, ba, wx, bx, lam)


def _sgu_fwd(z, lng, lnb, wm, bst):
    t = z.shape[0]
    tm = _tile_seq(t)

    def body(zu_ref, zv_ref, lng_ref, lnb_ref, wm_ref, bst_ref, yb_ref, gu_ref, mg_ref, vh_ref, gpv_ref, rstd_ref):
        gu, su = _gelu_t(zu_ref[...].astype(F32))
        gv, sv = _gelu_t(zv_ref[...].astype(F32))
        mu = jnp.mean(gv, axis=-1, keepdims=True)
        cen = gv - mu
        rstd = lax.rsqrt(jnp.mean(cen * cen, axis=-1, keepdims=True) + EPS)
        vhat = cen * rstd
        vb = (vhat * lng_ref[...] + lnb_ref[...]).astype(BF16)
        rows = []
        for b0 in range(0, tm, HD):
            rows.append(jnp.concatenate(
                [jnp.dot(wm_ref[g], vb[b0:b0 + HD, g * HD:(g + 1) * HD], preferred_element_type=F32)
                 + bst_ref[:, g:g + 1] for g in range(NH)], axis=1))
        mixed = jnp.concatenate(rows, axis=0) if len(rows) > 1 else rows[0]
        yb_ref[...] = (gu * mixed).astype(BF16)
        gu_ref[...] = gu.astype(BF16)
        mg_ref[...] = (mixed * _gelu_grad(su)).astype(BF16)
        vh_ref[...] = vhat.astype(BF16)
        gpv_ref[...] = _gelu_grad(sv).astype(BF16)
        rstd_ref[...] = rstd

    vec = _const_spec((1, D))
    tile = pl.BlockSpec((tm, D), lambda i: (i, 0))
    bshape = jax.ShapeDtypeStruct((t, D), BF16)
    return pl.pallas_call(
        body, name="sgu_fwd", grid=(t // tm,),
        in_specs=[pl.BlockSpec((tm, D), lambda i: (i, 2)), pl.BlockSpec((tm, D), lambda i: (i, 3)), vec, vec,
                  _const_spec((NH, HD, HD)), _const_spec((HD, NH))],
        out_specs=[tile] * 5 + [pl.BlockSpec((tm, 1), lambda i: (i, 0))],
        out_shape=[bshape] * 5 + [jax.ShapeDtypeStruct((t, 1), F32)],
        compiler_params=_cparams(1),
    )(z, z, lng, lnb, wm, bst)


def _merge_fwd(ya_pre, yb_pre, z, x, gate1, wba, wbb, wout):
    t = x.shape[0]
    tm = _tile_big(t)

    def body(yap_ref, ybp_ref, ga_ref, gb_ref, x_ref, g1_ref, wba_ref, wbb_ref, wo_ref,
             x2_ref, ya_ref, yb_ref, mg_ref, o1_ref):
        ya = jnp.dot(yap_ref[...], wba_ref[...], preferred_element_type=F32)
        yb = jnp.dot(ybp_ref[...], wbb_ref[...], preferred_element_type=F32)
        merged = _sigmoid(ga_ref[...].astype(F32)) * ya + _sigmoid(gb_ref[...].astype(F32)) * yb
        mb = merged.astype(BF16)
        o1 = jnp.dot(mb, wo_ref[...], preferred_element_type=F32)
        x2_ref[...] = x_ref[...] + g1_ref[...] * o1
        ya_ref[...] = ya.astype(BF16)
        yb_ref[...] = yb.astype(BF16)
        mg_ref[...] = mb
        o1_ref[...] = o1.astype(BF16)

    tile = pl.BlockSpec((tm, D), lambda i: (i, 0))
    wspec = _const_spec((D, D))
    bshape = jax.ShapeDtypeStruct((t, D), BF16)
    gate1, g1_spec = _vec_operand(gate1)
    return pl.pallas_call(
        body, name="merge_fwd", grid=(t // tm,),
        in_specs=[tile, tile, pl.BlockSpec((tm, D), lambda i: (i, 4)), pl.BlockSpec((tm, D), lambda i: (i, 5)),
                  tile, g1_spec, wspec, wspec, wspec],
        out_specs=[tile] * 5,
        out_shape=[jax.ShapeDtypeStruct((t, D), F32), bshape, bshape, bshape, bshape],
        compiler_params=_cparams(1),
    )(ya_pre, yb_pre, z, z, x, gate1, wba, wbb, wout)


def _conv3(u, prev8, cw_ref, cb):
    return cb + cw_ref[2:3, :] * u + cw_ref[1:2, :] * _shift_down(u, 1, prev8) + cw_ref[0:1, :] * _shift_down(u, 2, prev8)


def _ffn_proj_mid(x2, g, scale, shift, w, cw, cb):
    t = x2.shape[0]
    tm = _tile_big(t)
    nc = DFF // D

    def body(x_ref, g_ref, sc_ref, sh_ref, wa_ref, wv_ref, cwa_ref, cwv_ref, cba_ref, cbv_ref,
             h_ref, upa_ref, upv_ref, ff_ref, fa_ref, fv_ref, hb_scr, prev_ref):
        i, c = pl.program_id(0), pl.program_id(1)

        @pl.when(i == 0)
        def _():
            prev_ref[c] = jnp.zeros((2, SUBLANES, D), F32)

        @pl.when(c == 0)
        def _():
            xv = x_ref[...]
            r = lax.rsqrt(jnp.mean(xv * xv, axis=-1, keepdims=True) + EPS)
            hb_scr[...] = ((xv * r * g_ref[...]) * (1.0 + sc_ref[...]) + sh_ref[...]).astype(BF16)
            h_ref[...] = hb_scr[...]

        hb = hb_scr[...]
        halves = []
        for s, (w_ref, up_ref, cw_ref, cb_ref) in enumerate(((wa_ref, upa_ref, cwa_ref, cba_ref),
                                                             (wv_ref, upv_ref, cwv_ref, cbv_ref))):
            u = jnp.dot(hb, w_ref[...], preferred_element_type=F32)
            up_ref[...] = u.astype(BF16)
            halves.append(_conv3(u, prev_ref[c, s], cw_ref, cb_ref[...]))
            prev_ref[c, s] = u[tm - SUBLANES:]
        act, val = halves
        ga, ta = _gelu_t(act)
        ff_ref[...] = (ga * val).astype(BF16)
        fa_ref[...] = (val * _gelu_grad(ta)).astype(BF16)
        fv_ref[...] = ga.astype(BF16)

    def cols(rows, off):
        return pl.BlockSpec((rows, D), lambda i, c: (0, off + c))

    vec = pl.BlockSpec((1, D), lambda i, c: (0, 0))
    row_tile = pl.BlockSpec((tm, D), lambda i, c: (i, 0))
    chunk = pl.BlockSpec((tm, D), lambda i, c: (i, c))
    hshape = jax.ShapeDtypeStruct((t, DFF), BF16)
    (scale, sc_spec), (shift, sh_spec) = _vec_operand(scale), _vec_operand(shift)
    return pl.pallas_call(
        body, name="ffn_proj_mid", grid=(t // tm, nc),
        in_specs=[row_tile, vec, sc_spec, sh_spec, cols(D, 0), cols(D, nc), cols(3, 0), cols(3, nc), cols(1, 0), cols(1, nc)],
        out_specs=[row_tile, chunk, chunk, chunk, chunk, chunk],
        out_shape=[jax.ShapeDtypeStruct((t, D), BF16), hshape, hshape, hshape, hshape, hshape],
        scratch_shapes=[pltpu.VMEM((tm, D), BF16), pltpu.VMEM((nc, 2, SUBLANES, D), F32)],
        compiler_params=_cparams(2),
    )(x2, g, scale, shift, w, w, cw, cw, cb, cb)


def _ffn_out_loss(ff, wd, x2, target, gate2, gfin):
    t = x2.shape[0]
    tm = _tile_big(t)

    def body(ff_ref, wd_ref, x2_ref, tg_ref, g2_ref, gf_ref, dx3_ref, do2_ref, loss_ref, dgf_ref, dg2_ref):
        @pl.when(pl.program_id(0) == 0)
        def _():
            loss_ref[...] = jnp.zeros_like(loss_ref)
            dgf_ref[...] = jnp.zeros_like(dgf_ref)
            dg2_ref[...] = jnp.zeros_like(dg2_ref)

        o2 = jnp.dot(ff_ref[...], wd_ref[...], preferred_element_type=F32)
        x3 = x2_ref[...] + g2_ref[...] * o2
        r = lax.rsqrt(jnp.mean(x3 * x3, axis=-1, keepdims=True) + EPS)
        xhat = x3 * r
        err = xhat * gf_ref[...] - tg_ref[...]
        loss_ref[...] += 0.5 * jnp.sum(jnp.mean(err * err, axis=-1, keepdims=True), axis=0, keepdims=True)
        dy = err * (1.0 / D)
        dgf_ref[...] += _colsum(dy * xhat)
        dxh = dy * gf_ref[...]
        dx3 = r * (dxh - xhat * jnp.mean(dxh * xhat, axis=-1, keepdims=True))
        dx3_ref[...] = dx3
        do2_ref[...] = (dx3 * g2_ref[...]).astype(BF16)
        dg2_ref[...] += _colsum(dx3 * o2)

    tile = pl.BlockSpec((tm, D), lambda i: (i, 0))
    vec = _const_spec((1, D))
    gate2, g2_spec = _vec_operand(gate2)
    return pl.pallas_call(
        body, name="ffn_out_loss", grid=(t // tm,),
        in_specs=[pl.BlockSpec((tm, DFF), lambda i: (i, 0)), _const_spec((DFF, D), True), tile, tile, g2_spec, vec],
        out_specs=[tile, tile, _const_spec((1, 1)), vec, vec],
        out_shape=[jax.ShapeDtypeStruct((t, D), F32), jax.ShapeDtypeStruct((t, D), BF16),
                   jax.ShapeDtypeStruct((1, 1), F32),
                   jax.ShapeDtypeStruct((1, D), F32), jax.ShapeDtypeStruct((1, D), F32)],
        compiler_params=_cparams(1),
    )(ff, wd, x2, target, gate2, gfin)


def _ffn_down_bwd(do2, ff, fa, fv, wd):
    t = do2.shape[0]
    tm = min(1024, t)
    nc = DFF // D

    def body(do2_ref, ff_ref, fa_ref, fv_ref, wd_ref, da_ref, dv_ref, dwd_ref, dcba_ref, dcbv_ref):
        @pl.when(pl.program_id(1) == 0)
        def _():
            for r in (dwd_ref, dcba_ref, dcbv_ref):
                r[...] = jnp.zeros_like(r)

        do2 = do2_ref[...]
        dwd_ref[...] += _dot_tn(ff_ref[...], do2)
        dff = _dot_nt(do2, wd_ref[...])
        dact = dff * fa_ref[...].astype(F32)
        dval = dff * fv_ref[...].astype(F32)
        da_ref[...] = dact.astype(BF16)
        dv_ref[...] = dval.astype(BF16)
        dcba_ref[...] += _colsum(dact)
        dcbv_ref[...] += _colsum(dval)

    blk = pl.BlockSpec((tm, D), lambda c, i: (i, c))
    vec = pl.BlockSpec((1, D), lambda c, i: (0, c))
    return pl.pallas_call(
        body, name="ffn_down_bwd", grid=(nc, t // tm),
        in_specs=[pl.BlockSpec((tm, D), lambda c, i: (i, 0)),
                  blk, blk, blk, pl.BlockSpec((D, D), lambda c, i: (c, 0))],
        out_specs=[blk, blk, pl.BlockSpec((D, D), lambda c, i: (c, 0)), vec, vec],
        out_shape=[jax.ShapeDtypeStruct((t, DFF), BF16), jax.ShapeDtypeStruct((t, DFF), BF16),
                   jax.ShapeDtypeStruct((DFF, D), F32),
                   jax.ShapeDtypeStruct((1, DFF), F32), jax.ShapeDtypeStruct((1, DFF), F32)],
        compiler_params=_cparams(2),
    )(do2, ff, fa, fv, wd)


def _modnorm_bwd(dh, xv, g, scale):
    r = lax.rsqrt(jnp.mean(xv * xv, axis=-1, keepdims=True) + EPS)
    xhat = xv * r
    dxn = dh * (1.0 + scale)
    dxh = dxn * g
    dx = r * (dxh - xhat * jnp.mean(dxh * xhat, axis=-1, keepdims=True))
    return dx, _colsum(dh), _colsum(dh * (xhat * g)), _colsum(dxn * xhat)


def _ffn_up_bwd(dact, dval, up_a, up_v, cw, wup, x2, dx3, gffn, scale2, o1, gate1):
    t = x2.shape[0]
    tm = _tile_seq(t)
    nt = t // tm
    nc = DFF // D

    def body(da_ref, dan_ref, dv_ref, dvn_ref, ua_ref, uv_ref, cw_ref, w_ref, x2_ref, dx3_ref, g_ref, sc_ref, o1_ref, g1_ref,
             dup_ref, dx2_ref, do1_ref, dcw_ref, dsh_ref, dsc_ref, dg_ref, dg1_ref):
        i = pl.program_id(0)

        @pl.when(i == 0)
        def _():
            for r in (dcw_ref, dsh_ref, dsc_ref, dg_ref, dg1_ref):
                r[...] = jnp.zeros_like(r)

        last = i == nt - 1
        dh = jnp.zeros((tm, D), F32)
        for half, (d_ref, dn_ref, u_ref) in enumerate(((da_ref, dan_ref, ua_ref), (dv_ref, dvn_ref, uv_ref))):
            nxt = jnp.where(last, 0.0, dn_ref[...].astype(F32)[:SUBLANES])
            for c in range(nc):
                c0 = half * DFF + c * D
                dv = d_ref[:, c * D:(c + 1) * D].astype(F32)
                nx = nxt[:, c * D:(c + 1) * D]
                taps = (_shift_up(dv, 2, nx), _shift_up(dv, 1, nx), dv)
                dup = (cw_ref[2:3, c0:c0 + D] * taps[2] + cw_ref[1:2, c0:c0 + D] * taps[1]
                       + cw_ref[0:1, c0:c0 + D] * taps[0]).astype(BF16)
                upv = u_ref[:, c * D:(c + 1) * D].astype(F32)
                for k in range(3):
                    dcw_ref[k:k + 1, c0:c0 + D] += _colsum(taps[k] * upv)
                dup_ref[:, c0:c0 + D] = dup
                dh = dh + _dot_nt(dup, w_ref[:, c0:c0 + D])
        dxn, dsh, dsc, dg = _modnorm_bwd(dh, x2_ref[...], g_ref[...], sc_ref[...])
        dx2 = dx3_ref[...] + dxn
        dx2_ref[...] = dx2
        do1_ref[...] = (dx2 * g1_ref[...]).astype(BF16)
        dsh_ref[...] += dsh
        dsc_ref[...] += dsc
        dg_ref[...] += dg
        dg1_ref[...] += _colsum(dx2 * o1_ref[...].astype(F32))

    tile = pl.BlockSpec((tm, D), lambda i: (i, 0))
    wide = pl.BlockSpec((tm, DFF), lambda i: (i, 0))
    nxt = pl.BlockSpec((HALO, DFF), lambda i: (jnp.minimum((i + 1) * (tm // HALO), t // HALO - 1), 0))
    vec = _const_spec((1, D))
    vshape = jax.ShapeDtypeStruct((1, D), F32)
    (scale2, sc_spec), (gate1, g1_spec) = _vec_operand(scale2), _vec_operand(gate1)
    return pl.pallas_call(
        body, name="ffn_up_bwd", grid=(nt,),
        in_specs=[wide, nxt, wide, nxt, wide, wide,
                  _const_spec((3, 2 * DFF)), _const_spec((D, 2 * DFF), True),
                  tile, tile, vec, sc_spec, tile, g1_spec],
        out_specs=[pl.BlockSpec((tm, 2 * DFF), lambda i: (i, 0)), tile, tile, _const_spec((3, 2 * DFF)),
                   vec, vec, vec, vec],
        out_shape=[jax.ShapeDtypeStruct((t, 2 * DFF), BF16), jax.ShapeDtypeStruct((t, D), F32),
                   jax.ShapeDtypeStruct((t, D), BF16), jax.ShapeDtypeStruct((3, 2 * DFF), F32),
                   vshape, vshape, vshape, vshape],
        compiler_params=_cparams(1),
    )(dact, dact, dval, dval, up_a, up_v, cw, wup, x2, dx3, gffn, scale2, o1, gate1)


def _xt_y(a, b, name):
    t, k = a.shape
    n = b.shape[1]
    tm = min(1024, t)
    bn = 3072 if n % 3072 == 0 else D

    def body(a_ref, b_ref, o_ref):
        @pl.when(pl.program_id(1) == 0)
        def _():
            o_ref[...] = jnp.zeros_like(o_ref)

        o_ref[...] += _dot_tn(a_ref[...], b_ref[...])

    return pl.pallas_call(
        body, name=name, grid=(n // bn, t // tm),
        in_specs=[pl.BlockSpec((tm, k), lambda j, i: (i, 0)), pl.BlockSpec((tm, bn), lambda j, i: (i, j))],
        out_specs=pl.BlockSpec((k, bn), lambda j, i: (0, j)),
        out_shape=jax.ShapeDtypeStruct((k, n), F32),
        compiler_params=_cparams(2),
    )(a, b)


def _acc_spec(shape, index):
    return pl.BlockSpec(shape, lambda *_: index, pipeline_mode=pl.Buffered(1))


def _out_bwd(do1, wout, merged, ya, yb, z, h1):
    t = do1.shape[0]
    tm = _tile_big(t)

    def body(do1_ref, wo_ref, mg_ref, ya_ref, yb_ref, ga_ref, gb_ref, h1_ref,
             dya_ref, dyb_ref, dz_ref, dwo_ref, dwin_ref):
        @pl.when(pl.program_id(0) == 0)
        def _():
            dwo_ref[...] = jnp.zeros_like(dwo_ref)
            dwin_ref[...] = jnp.zeros_like(dwin_ref)

        do1v = do1_ref[...]
        dwo_ref[...] += _dot_tn(mg_ref[...], do1v)
        dm = _dot_nt(do1v, wo_ref[...])
        sa = _sigmoid(ga_ref[...].astype(F32))
        sb = _sigmoid(gb_ref[...].astype(F32))
        dya_ref[...] = (dm * sa).astype(BF16)
        dyb_ref[...] = (dm * sb).astype(BF16)
        dga = (dm * ya_ref[...].astype(F32) * sa * (1.0 - sa)).astype(BF16)
        dgb = (dm * yb_ref[...].astype(F32) * sb * (1.0 - sb)).astype(BF16)
        dz_ref[:, 0:D] = dga
        dz_ref[:, D:2 * D] = dgb
        h1v = h1_ref[...]
        dwin_ref[:, 0:D] += _dot_tn(h1v, dga)
        dwin_ref[:, D:2 * D] += _dot_tn(h1v, dgb)

    tile = pl.BlockSpec((tm, D), lambda i: (i, 0))
    bshape = jax.ShapeDtypeStruct((t, D), BF16)
    return pl.pallas_call(
        body, name="out_bwd", grid=(t // tm,),
        in_specs=[tile, _const_spec((D, D), True), tile, tile, tile,
                  pl.BlockSpec((tm, D), lambda i: (i, 4)), pl.BlockSpec((tm, D), lambda i: (i, 5)), tile],
        out_specs=[tile, tile, pl.BlockSpec((tm, 2 * D), lambda i: (i, 2)), _acc_spec((D, D), (0, 0)),
                   _acc_spec((D, 2 * D), (0, 2))],
        out_shape=[bshape, bshape, jax.ShapeDtypeStruct((t, NCOL_IN), BF16), jax.ShapeDtypeStruct((D, D), F32),
                   jax.ShapeDtypeStruct((D, NCOL_IN), F32)],
        compiler_params=_cparams(1),
    )(do1, wout, merged, ya, yb, z, z, h1)


def _rnn_bwd(dya, ya_pre, wba, h1, z, saved, h, dz, dwin, cw, wa, wx, lam):
    t = z.shape[0]
    tm = _tile_seq(t)
    nt = t // tm
    ngrp = tm // SUBLANES
    hpt = tm // HALO

    def body(dya_ref, yap_ref, wba_ref, h1_ref, xr_ref, xc_ref, ra_ref, ia_ref, gg_ref, hg_ref, h_ref, hp_ref,
             dz_any, dwin_any, cw_ref, wa_ref, wx_ref, lam_ref,
             dz_ref, dwin_ref, dwba_ref, dcw_ref, dcb_ref, dwa_ref, dba_ref, dwx_ref, dbx_ref, dlam_ref,
             a_first, g_first, dxc_first, b_scr, d_scr, g_scr):
        del dz_any, dwin_any
        i = pl.program_id(0)

        @pl.when(i == 0)
        def _():
            for r in (dwin_ref, dwba_ref, dcw_ref, dcb_ref, dwa_ref, dba_ref, dwx_ref, dbx_ref, dlam_ref,
                      a_first, g_first, dxc_first):
                r[...] = jnp.zeros_like(r)

        dya_v = dya_ref[...]
        dwba_ref[...] += _dot_tn(yap_ref[...], dya_v)
        dyap_v = _dot_nt(dya_v, wba_ref[...])
        h1v = h1_ref[...]

        first_tile = i == nt - 1
        xc = xc_ref[...].astype(F32)
        ra = ra_ref[...].astype(F32)
        ia = ia_ref[...].astype(F32)
        lam_v = lam_ref[...]
        ls = _log_sigmoid(lam_v)
        la = LRU_C * ra * ls
        a = jnp.exp(la)
        mult = jnp.sqrt(-jnp.tanh(la) * (1.0 + a * a))
        hprev8 = jnp.where(first_tile, 0.0, hp_ref[...][HALO - SUBLANES:])
        h_prev = _shift_down(h_ref[...], 1, hprev8)
        dgr = (dyap_v * hg_ref[...].astype(F32)).astype(BF16)
        dz_ref[:, D:2 * D] = dgr
        dwin_ref[:, D:2 * D] += _dot_tn(h1v, dgr)

        b_scr[...] = _shift_up(a, 1, a_first[...])
        d_scr[...] = dyap_v * gg_ref[...].astype(F32)
        row = _row_iota(D)

        def grp(jj, carry):
            r0 = pl.multiple_of((ngrp - 1 - jj) * SUBLANES, SUBLANES)
            bv = b_scr[pl.ds(r0, SUBLANES), :]
            dv = d_scr[pl.ds(r0, SUBLANES), :]
            for d in (1, 2, 4):
                m = row < SUBLANES - d
                dv = jnp.where(m, dv + bv * pltpu.roll(dv, SUBLANES - d, 0), dv)
                bv = jnp.where(m, bv * pltpu.roll(bv, SUBLANES - d, 0), bv)
            gv = dv + bv * carry
            g_scr[pl.ds(r0, SUBLANES), :] = gv
            return gv[0:1, :]

        lax.fori_loop(0, ngrp, grp, g_first[0:1, :])
        g = g_scr[...]
        a_first[...] = a[:SUBLANES]
        g_first[...] = g[:SUBLANES]

        da = g * h_prev
        gx = g * xc
        dmult = gx * ia
        dia = gx * mult
        dxc = g * (mult * ia)
        dla = da * a - dmult * (a * a) / mult
        dra = dla * (LRU_C * ls)
        dlam_ref[...] += _colsum(dla * ra) * (LRU_C * _sigmoid(-lam_v))
        dpa = dra * ra * (1.0 - ra)
        dpx = dia * ia * (1.0 - ia)
        dba_ref[...] += _colsum(dpa)
        dbx_ref[...] += _colsum(dpx)
        dpab = dpa.astype(BF16)
        dpxb = dpx.astype(BF16)
        xcb = xc_ref[...]
        for hd in range(NH):
            sl = slice(hd * HD, (hd + 1) * HD)
            dwa_ref[hd] += _dot_tn(xcb[:, sl], dpab[:, sl])
            dwx_ref[hd] += _dot_tn(xcb[:, sl], dpxb[:, sl])
        dxc = dxc + _heads_nt(dpab, wa_ref) + _heads_nt(dpxb, wx_ref)

        nxt = dxc_first[...]
        taps = (_shift_up(dxc, 3, nxt), _shift_up(dxc, 2, nxt), _shift_up(dxc, 1, nxt), dxc)
        dxr = cw_ref[0:1, :] * taps[0]
        for k in range(1, 4):
            dxr = dxr + cw_ref[k:k + 1, :] * taps[k]
        dxrb = dxr.astype(BF16)
        dz_ref[:, 0:D] = dxrb
        dwin_ref[:, 0:D] += _dot_tn(h1v, dxrb)
        dxc_first[...] = dxc[:SUBLANES]
        dcb_ref[...] += _colsum(dxc)
        xr = xr_ref[...].astype(F32)
        for k in range(4):
            dcw_ref[k:k + 1, :] += _colsum(taps[k] * xr)

    def rev(col):
        return lambda i: (nt - 1 - i, col)

    vec = _const_spec((1, D))
    wspec = _const_spec((NH, HD, HD))
    vshape = jax.ShapeDtypeStruct((1, D), F32)
    wshape = jax.ShapeDtypeStruct((NH, HD, HD), F32)
    any_spec = pl.BlockSpec(memory_space=pl.ANY)
    tile = pl.BlockSpec((tm, D), rev(0))
    outs = pl.pallas_call(
        body, name="rnn_bwd", grid=(nt,),
        in_specs=[tile, tile, _const_spec((D, D), True), tile, tile, tile, tile, tile, tile, tile, tile,
                  pl.BlockSpec((HALO, D), lambda i: (jnp.maximum((nt - 1 - i) * hpt - 1, 0), 0)),
                  any_spec, any_spec, _const_spec((4, D)), wspec, wspec, vec],
        out_specs=[pl.BlockSpec((tm, 2 * D), rev(0)), _acc_spec((D, 2 * D), (0, 0)), _acc_spec((D, D), (0, 0)),
                   _const_spec((4, D)), vec, wspec, vec, wspec, vec, vec],
        out_shape=[jax.ShapeDtypeStruct((t, NCOL_IN), BF16), jax.ShapeDtypeStruct((D, NCOL_IN), F32),
                   jax.ShapeDtypeStruct((D, D), F32), jax.ShapeDtypeStruct((4, D), F32), vshape,
                   wshape, vshape, wshape, vshape, vshape],
        scratch_shapes=[pltpu.VMEM((SUBLANES, D), F32), pltpu.VMEM((SUBLANES, D), F32), pltpu.VMEM((SUBLANES, D), F32),
                        pltpu.VMEM((tm, D), F32), pltpu.VMEM((tm, D), F32), pltpu.VMEM((tm, D), F32)],
        input_output_aliases={12: 0, 13: 1},
        compiler_params=_cparams(1),
    )(dya, ya_pre, wba, h1, z, *saved, h, h, dz, dwin, cw, wa, wx, lam)
    return outs


def _sgu_bwd(dyb, yb_pre, wbb, h1, saved, dz, dwin, lng, lnb, wmt, mask):
    t = dyb.shape[0]
    tm = _tile_big(t)

    def body(dyb_ref, ybp_ref, wbb_ref, h1_ref, gu_ref, mg_ref, vh_ref, gpv_ref, rstd_ref, dz_any, dwin_any,
             lng_ref, lnb_ref, wmt_ref, mask_ref,
             dz_ref, dwin_ref, dwbb_ref, dws_ref, dbst_ref, dlng_ref, dlnb_ref):
        del dz_any, dwin_any

        @pl.when(pl.program_id(0) == 0)
        def _():
            for r in (dwin_ref, dwbb_ref, dws_ref, dbst_ref, dlng_ref, dlnb_ref):
                r[...] = jnp.zeros_like(r)

        lng_v = lng_ref[...]
        vhat = vh_ref[...].astype(F32)
        vb = (vhat * lng_v + lnb_ref[...]).astype(BF16)
        rstd = rstd_ref[...]
        dyb_v = dyb_ref[...]
        dwbb_ref[...] += _dot_tn(ybp_ref[...], dyb_v)
        dyb = _dot_nt(dyb_v, wbb_ref[...])
        h1v = h1_ref[...]
        dzu = (dyb * mg_ref[...].astype(F32)).astype(BF16)
        dz_ref[:, 0:D] = dzu
        dwin_ref[:, 0:D] += _dot_tn(h1v, dzu)
        dmix = dyb * gu_ref[...].astype(F32)
        dmb = dmix.astype(BF16)
        rows = []
        lane = lax.broadcasted_iota(jnp.int32, (HD, NH), 1)
        dbst = jnp.zeros((HD, NH), F32)
        for b0 in range(0, tm, HD):
            cols = []
            for g in range(NH):
                sl = slice(g * HD, (g + 1) * HD)
                dmg = dmb[b0:b0 + HD, sl]
                dws_ref[g] += _dot_nt(dmg, vb[b0:b0 + HD, sl]) * mask_ref[...]
                cols.append(jnp.dot(wmt_ref[g], dmg, preferred_element_type=F32))
                dbst = dbst + jnp.where(lane == g, jnp.sum(dmix[b0:b0 + HD, sl], axis=1, keepdims=True), 0.0)
            rows.append(jnp.concatenate(cols, axis=1))
        dbst_ref[...] += dbst
        dvln = jnp.concatenate(rows, axis=0) if len(rows) > 1 else rows[0]
        dlng_ref[...] += _colsum(dvln * vhat)
        dlnb_ref[...] += _colsum(dvln)
        dvh = dvln * lng_v
        dgv = rstd * (dvh - jnp.mean(dvh, axis=-1, keepdims=True)
                      - vhat * jnp.mean(dvh * vhat, axis=-1, keepdims=True))
        dzv = (dgv * gpv_ref[...].astype(F32)).astype(BF16)
        dz_ref[:, D:2 * D] = dzv
        dwin_ref[:, D:2 * D] += _dot_tn(h1v, dzv)

    vec = _const_spec((1, D))
    wspec = _const_spec((NH, HD, HD))
    vshape = jax.ShapeDtypeStruct((1, D), F32)
    tile = pl.BlockSpec((tm, D), lambda i: (i, 0))
    any_spec = pl.BlockSpec(memory_space=pl.ANY)
    return pl.pallas_call(
        body, name="sgu_bwd", grid=(t // tm,),
        in_specs=[tile, tile, _const_spec((D, D), True), tile, tile, tile, tile, tile,
                  pl.BlockSpec((tm, 1), lambda i: (i, 0)), any_spec, any_spec,
                  vec, vec, wspec, _const_spec((HD, HD))],
        out_specs=[pl.BlockSpec((tm, 2 * D), lambda i: (i, 1)), _acc_spec((D, 2 * D), (0, 1)), _acc_spec((D, D), (0, 0)),
                   wspec, _const_spec((HD, NH)), vec, vec],
        out_shape=[jax.ShapeDtypeStruct((t, NCOL_IN), BF16), jax.ShapeDtypeStruct((D, NCOL_IN), F32),
                   jax.ShapeDtypeStruct((D, D), F32), jax.ShapeDtypeStruct((NH, HD, HD), F32),
                   jax.ShapeDtypeStruct((HD, NH), F32), vshape, vshape],
        input_output_aliases={9: 0, 10: 1},
        compiler_params=_cparams(1),
    )(dyb, yb_pre, wbb, h1, *saved, dz, dwin, lng, lnb, wmt, mask)


def _in_bwd(dz, win, x, dx2, g, scale1):
    t = x.shape[0]
    tm = _tile_big(t)

    def body(dz_ref, w_ref, x_ref, dx2_ref, g_ref, sc_ref, dx_ref, dsh_ref, dsc_ref, dg_ref):
        @pl.when(pl.program_id(0) == 0)
        def _():
            for r in (dsh_ref, dsc_ref, dg_ref):
                r[...] = jnp.zeros_like(r)

        dh = jnp.zeros((tm, D), F32)
        for c0 in range(0, NCOL_IN, D):
            dh = dh + _dot_nt(dz_ref[:, c0:c0 + D], w_ref[:, c0:c0 + D])
        dxn, dsh, dsc, dg = _modnorm_bwd(dh, x_ref[...], g_ref[...], sc_ref[...])
        dx_ref[...] = dx2_ref[...] + dxn
        dsh_ref[...] += dsh
        dsc_ref[...] += dsc
        dg_ref[...] += dg

    tile = pl.BlockSpec((tm, D), lambda i: (i, 0))
    vec = _const_spec((1, D))
    vshape = jax.ShapeDtypeStruct((1, D), F32)
    scale1, sc_spec = _vec_operand(scale1)
    return pl.pallas_call(
        body, name="in_bwd", grid=(t // tm,),
        in_specs=[pl.BlockSpec((tm, NCOL_IN), lambda i: (i, 0)), _const_spec((D, NCOL_IN), True), tile, tile, vec,
                  sc_spec],
        out_specs=[tile, vec, vec, vec],
        out_shape=[jax.ShapeDtypeStruct((t, D), F32), vshape, vshape, vshape],
        compiler_params=_cparams(1),
    )(dz, win, x, dx2, g, scale1)


def _mod_cols(c_all, w_ada, b_cols):
    nb, cols = c_all.shape[0], w_ada.shape[1]

    def body(c_ref, w_ref, b_ref, o_ref):
        cv = c_ref[...]
        ca = (cv * _sigmoid(cv)).astype(BF16)
        o_ref[...] = jnp.dot(ca, w_ref[...].astype(BF16), preferred_element_type=F32) + b_ref[...]

    return pl.pallas_call(body, name="mod_cols", out_shape=jax.ShapeDtypeStruct((nb, cols), F32))(c_all, w_ada, b_cols)


def _ada_grad(c_all, dmod_cols):
    cols = dmod_cols.shape[1]

    def body(c_ref, d_ref, o_ref):
        cv = c_ref[...]
        ca = (cv * _sigmoid(cv)).astype(BF16)
        o_ref[...] = _dot_tn(ca, d_ref[...].astype(BF16))

    return pl.pallas_call(body, name="ada_grad", out_shape=jax.ShapeDtypeStruct((D, cols), F32))(c_all, dmod_cols)


def _adamw_update(w, m, v, g):
    bc1 = 1.0 - ADAM_B1 ** ADAM_STEP
    bc2 = 1.0 - ADAM_B2 ** ADAM_STEP
    mn = ADAM_B1 * m + (1.0 - ADAM_B1) * g
    vn = ADAM_B2 * v + (1.0 - ADAM_B2) * (g * g)
    return -ADAM_LR * ((mn / bc1) / (jnp.sqrt(vn / bc2) + ADAM_EPS) + ADAM_WD * w), mn, vn


def _adamw_group(names, ws, ms, vs, packs, name):
    n = len(names)
    starts, r0 = [], 0
    for w in ws:
        starts.append(r0)
        r0 += _pack_rows(w.shape)

    def body(*refs):
        w_refs, m_refs, v_refs, p_ref = refs[:n], refs[n:2 * n], refs[2 * n:3 * n], refs[3 * n]
        outs = refs[3 * n + 1:]
        for k in range(n):
            rows = _pack_rows(ws[k].shape)
            g = None
            for dev in range(N_DEV):
                if ws[k].shape[0] == 1:
                    term = jnp.concatenate(
                        [p_ref[dev, starts[k] + r:starts[k] + r + 1, :] for r in range(rows)], axis=1)
                else:
                    term = p_ref[dev, starts[k]:starts[k] + rows, :]
                g = term if g is None else g + term
            delta, mn, vn = _adamw_update(w_refs[k][...], m_refs[k][...], v_refs[k][...], g)
            for o_ref, val in zip(outs[4 * k:4 * k + 4], (g, delta, mn, vn)):
                o_ref[...] = val

    shapes = [jax.ShapeDtypeStruct(w.shape, F32) for w in ws for _ in range(4)]
    outs = pl.pallas_call(body, name=name, out_shape=shapes,
                          compiler_params=pltpu.CompilerParams(vmem_limit_bytes=VMEM_LIMIT))(*ws, *ms, *vs, packs)
    return {nm: tuple(outs[4 * k:4 * k + 4]) for k, nm in enumerate(names)}


def _adamw(w, m, v, parts, name):
    rows, cols = w.shape
    tr = _row_tile(rows, cols)
    stacked = [p.ndim == 3 for p in parts]

    def body(*refs):
        w_ref, m_ref, v_ref = refs[:3]
        p_refs = refs[3:3 + len(parts)]
        g_ref, d_ref, mo_ref, vo_ref = refs[3 + len(parts):]
        g = None
        for p_ref, st in zip(p_refs, stacked):
            terms = [p_ref[k].astype(F32) for k in range(p_ref.shape[0])] if st else [p_ref[...].astype(F32)]
            for term in terms:
                g = term if g is None else g + term
        delta, mn, vn = _adamw_update(w_ref[...], m_ref[...], v_ref[...], g)
        g_ref[...] = g
        mo_ref[...] = mn
        vo_ref[...] = vn
        d_ref[...] = delta

    tile = pl.BlockSpec((tr, cols), lambda i: (i, 0))
    p_specs = [pl.BlockSpec((p.shape[0], tr, cols), lambda i: (0, i, 0)) if st else tile for p, st in zip(parts, stacked)]
    shp = jax.ShapeDtypeStruct((rows, cols), F32)
    return pl.pallas_call(
        body, name=name, grid=(rows // tr,),
        in_specs=[tile, tile, tile] + p_specs, out_specs=[tile] * 4, out_shape=[shp] * 4,
        compiler_params=_cparams(1),
    )(w, m, v, *parts)


def _mesh_pos():
    return lax.axis_index("x"), lax.axis_index("y"), lax.axis_index("c")


def _other_chips(x, y):
    return [(1 - x, y), (x, 1 - y), (1 - x, 1 - y)]


def _block_of(ref, axis, index, size):
    if axis == 0:
        return ref.at[index]
    return ref.at[:, pl.ds(pl.multiple_of(index * size, 128), size)]


def _all_gather(shards, axes, name):
    n = len(shards)
    per = 7

    def body(*refs):
        ins, outs, done = refs[:n], refs[n:2 * n], refs[2 * n]
        send_sems, recv_sems, local_sems = refs[2 * n + 1:]
        x, y, c = _mesh_pos()
        me, sibling = (x, y, c), (x, y, 1 - c)
        chips = _other_chips(x, y)

        def rows(a, pos):
            return _block_of(outs[a], axes[a], 4 * pos[0] + 2 * pos[1] + pos[2], shards[a].shape[-1])

        def copy(a, k, block, to, src=None):
            return pltpu.make_async_remote_copy(
                src_ref=rows(a, block) if src is None else src, dst_ref=rows(a, block),
                send_sem=send_sems.at[a * per + k], recv_sem=recv_sems.at[a * per + k],
                device_id=to, device_id_type=MESH_IDS)

        mine = [pltpu.make_async_copy(ins[a], rows(a, me), local_sems.at[a]) for a in range(n)]
        for cp in mine:
            cp.start()
        first = []
        for a in range(n):
            first.append(copy(a, 0, me, sibling, src=ins[a]))
            first += [copy(a, 1 + j, me, (*chip, c), src=ins[a]) for j, chip in enumerate(chips)]
        for cp in first:
            cp.start()
        passed = []
        for j, chip in enumerate(chips):
            for a in range(n):
                copy(a, 1 + j, (*chip, c), me).wait_recv()
                fwd = copy(a, 4 + j, (*chip, c), sibling)
                fwd.start()
                passed.append(fwd)
        for a in range(n):
            copy(a, 0, sibling, me).wait_recv()
            for j, chip in enumerate(chips):
                copy(a, 4 + j, (*chip, 1 - c), me).wait_recv()
        for cp in first + passed:
            cp.wait_send()
        for cp in mine:
            cp.wait()
        done[...] = jnp.zeros_like(done)

    def full_shape(s, ax):
        return (N_DEV,) + s.shape if ax == 0 else s.shape[:-1] + (N_DEV * s.shape[-1],)

    any_spec = pl.BlockSpec(memory_space=pl.ANY)
    outs = pl.pallas_call(
        body, name=name,
        in_specs=[any_spec] * n, out_specs=[any_spec] * n + [pl.BlockSpec(memory_space=pltpu.VMEM)],
        out_shape=[jax.ShapeDtypeStruct(full_shape(s, ax), s.dtype) for s, ax in zip(shards, axes)]
        + [jax.ShapeDtypeStruct((SUBLANES, LANES), F32)],
        scratch_shapes=[pltpu.SemaphoreType.DMA((n * per,)), pltpu.SemaphoreType.DMA((n * per,)),
                        pltpu.SemaphoreType.DMA((n,))],
    )(*shards)
    return outs[:n], outs[n]


def _chip_blocks(x, y):
    return [(x, y)] + _other_chips(x, y)


def _sibling_reduce(gs, axis, name):
    g0, n = gs[0], len(gs)
    rows, cols = (g0.shape[1], g0.shape[2]) if axis == 0 else (g0.shape[0], g0.shape[1] // N_DEV)
    chunk = math.gcd(rows, 64)

    def body(*refs):
        g_refs, own_refs, pay_refs = refs[:n], refs[n:2 * n], refs[2 * n:3 * n]
        (stage_buf, send_buf, keep_buf, recv_buf, pay_buf,
         send_sems, recv_sems, stage_sems, keep_sems, out_sems) = refs[3 * n:]
        x, y, c = _mesh_pos()
        sibling = (x, y, 1 - c)
        chips = _chip_blocks(x, y)
        stage, keep, push = [], [], []
        for a in range(n):
            for j, (px, py) in enumerate(chips):
                s = 4 * a + j
                theirs = _block_of(g_refs[a], axis, 4 * px + 2 * py + (1 - c), cols)
                ours = _block_of(g_refs[a], axis, 4 * px + 2 * py + c, cols)
                stage.append(pltpu.make_async_copy(theirs, stage_buf.at[s], stage_sems.at[s]))
                keep.append(pltpu.make_async_copy(ours, keep_buf.at[s], keep_sems.at[s]))
                push.append(pltpu.make_async_remote_copy(
                    src_ref=send_buf.at[s], dst_ref=recv_buf.at[s], send_sem=send_sems.at[s],
                    recv_sem=recv_sems.at[s], device_id=sibling, device_id_type=MESH_IDS))
        for cp in stage[:2]:
            cp.start()
        for s in range(4 * n):
            if s + 2 < 4 * n:
                stage[s + 2].start()
            stage[s].wait()
            keep[s].start()

            def narrow(r, carry, s=s):
                sl = pl.ds(pl.multiple_of(r * chunk, chunk), chunk)
                send_buf[s, sl, :] = stage_buf[s, sl, :].astype(BF16)
                return carry

            lax.fori_loop(0, rows // chunk, narrow, 0)
            push[s].start()
        written = []
        for s in range(4 * n):
            push[s].wait_recv()
            keep[s].wait()
            a, j = divmod(s, 4)
            res = keep_buf.at[s] if j == 0 else pay_buf.at[3 * a + j - 1]

            def add(r, carry, s=s, res=res):
                sl = pl.ds(pl.multiple_of(r * chunk, chunk), chunk)
                res[sl, :] = (keep_buf[s, sl, :] + recv_buf[s, sl, :].astype(F32)).astype(res.dtype)
                return carry

            lax.fori_loop(0, rows // chunk, add, 0)
            out = pltpu.make_async_copy(res, own_refs[a] if j == 0 else pay_refs[a].at[j - 1], out_sems.at[s])
            out.start()
            written.append(out)
        for cp in push:
            cp.wait_send()
        for cp in written:
            cp.wait()

    any_spec = pl.BlockSpec(memory_space=pl.ANY)
    buf = pltpu.VMEM((4 * n, rows, cols), F32)
    buf16 = pltpu.VMEM((4 * n, rows, cols), BF16)
    sems = pltpu.SemaphoreType.DMA((4 * n,))
    outs = pl.pallas_call(
        body, name=name,
        in_specs=[any_spec] * n, out_specs=[any_spec] * (2 * n),
        out_shape=[jax.ShapeDtypeStruct((rows, cols), F32)] * n + [jax.ShapeDtypeStruct((3, rows, cols), BF16)] * n,
        scratch_shapes=[buf, buf16, buf, buf16, pltpu.VMEM((3 * n, rows, cols), BF16),
                        sems, sems, sems, sems, sems],
        compiler_params=pltpu.CompilerParams(vmem_limit_bytes=VMEM_LIMIT),
    )(*gs)
    return list(zip(outs[:n], outs[n:]))


_HBM_SPEC = pl.BlockSpec(memory_space=pltpu.HBM)
_SEM_SPEC = pl.BlockSpec(memory_space=pltpu.SEMAPHORE)
_SIDE_EFFECT = pltpu.SideEffectType.DATAFLOW_SIDE_EFFECTING


def _exchange_start(name, srcs, lands, plan, n_copies):
    nb = len(srcs) + len(lands)

    def body(*refs):
        bufs, send_sems, recv_sems, token = refs[:nb], refs[nb], refs[nb + 1], refs[-1]
        for cp in plan(bufs[:len(srcs)], bufs[len(srcs):], send_sems, recv_sems):
            cp.start()
        token[...] = jnp.zeros_like(token)

    arrays = list(srcs) + list(lands)
    outs = pl.pallas_call(
        body, name=name,
        out_shape=(pltpu.SemaphoreType.DMA((n_copies,)), pltpu.SemaphoreType.DMA((n_copies,)),
                   *[pltpu.HBM(a.shape, a.dtype) for a in arrays], jax.ShapeDtypeStruct((SUBLANES, LANES), F32)),
        in_specs=[_HBM_SPEC] * nb,
        out_specs=(_SEM_SPEC, _SEM_SPEC, *[_HBM_SPEC] * nb, pl.BlockSpec(memory_space=pltpu.VMEM)),
        input_output_aliases={k: 2 + k for k in range(nb)},
        compiler_params=pltpu.CompilerParams(has_side_effects=_SIDE_EFFECT),
    )(*[pltpu.with_memory_space_constraint(a, pltpu.HBM) for a in arrays])
    return outs[0], outs[1], outs[2:2 + len(srcs)], outs[2 + len(srcs):2 + nb], outs[-1]


def _exchange_wait(name, send_sems, recv_sems, srcs, lands, plan, after):
    nb = len(srcs) + len(lands)
    after = list(after)

    def body(*refs):
        bufs, send_ref, recv_ref = refs[:nb], refs[nb], refs[nb + 1]
        for cp in plan(bufs[:len(srcs)], bufs[len(srcs):], send_ref, recv_ref):
            cp.wait_send()
            cp.wait_recv()

    arrays = list(srcs) + list(lands)
    outs = pl.pallas_call(
        body, name=name,
        out_shape=tuple(pltpu.HBM(a.shape, a.dtype) for a in arrays),
        in_specs=[_HBM_SPEC] * nb + [_SEM_SPEC, _SEM_SPEC] + [pl.BlockSpec(memory_space=pl.ANY)] * len(after),
        out_specs=tuple([_HBM_SPEC] * nb),
        input_output_aliases={k: k for k in range(nb)},
        compiler_params=pltpu.CompilerParams(has_side_effects=_SIDE_EFFECT),
    )(*arrays, send_sems, recv_sems, *after)
    return outs[len(srcs):]


def _gather_plan(axes, sizes):
    def plan(src_refs, land_refs, send_sems, recv_sems):
        x, y, c = _mesh_pos()
        copies = []
        for a, (src, land) in enumerate(zip(src_refs, land_refs)):
            mine = _block_of(land, axes[a], 4 * x + 2 * y + c, sizes[a])
            for k in range(1, N_DEV):
                peer = (1 - x if k & 4 else x, 1 - y if k & 2 else y, 1 - c if k & 1 else c)
                idx = a * (N_DEV - 1) + k - 1
                copies.append(pltpu.make_async_remote_copy(
                    src_ref=src, dst_ref=mine, send_sem=send_sems.at[idx], recv_sem=recv_sems.at[idx],
                    device_id=peer, device_id_type=MESH_IDS))
        return copies
    return plan


def _chip_plan(src_refs, land_refs, send_sems, recv_sems):
    x, y, c = _mesh_pos()
    copies = []
    for a, (src, land) in enumerate(zip(src_refs, land_refs)):
        for j, chip in enumerate(_other_chips(x, y)):
            copies.append(pltpu.make_async_remote_copy(
                src_ref=src.at[j], dst_ref=land.at[j], send_sem=send_sems.at[3 * a + j],
                recv_sem=recv_sems.at[3 * a + j], device_id=(*chip, c), device_id_type=MESH_IDS))
    return copies


def _own_block_placed(shard, axis, me):
    if axis == 0:
        full = lax.empty((N_DEV,) + shard.shape, shard.dtype)
        return lax.dynamic_update_slice(full, shard[None], (me,) + (0,) * shard.ndim)
    rows, cols = shard.shape

    def body(me_ref, s_ref, o_ref):
        del me_ref
        o_ref[...] = s_ref[...]

    return pl.pallas_call(
        body, name="place_own_columns",
        grid_spec=pltpu.PrefetchScalarGridSpec(
            num_scalar_prefetch=1, grid=(1,),
            in_specs=[pl.BlockSpec((rows, cols), lambda i, me_ref: (0, 0))],
            out_specs=pl.BlockSpec((rows, cols), lambda i, me_ref: (0, me_ref[0]))),
        out_shape=jax.ShapeDtypeStruct((rows, N_DEV * cols), shard.dtype),
    )(jnp.reshape(me, (1,)).astype(jnp.int32), shard)


def _local_step(x, target, mod, win, late_weights, p, grads_ready=None):
    shift1, scale1, gate1, shift2, scale2, gate2 = ((mod, k) for k in range(6))

    def after_token(v, token):
        return v if token is None else v + token[0:1, 0:1]
    wa, wx = p["lru_w_a"].astype(BF16), p["lru_w_x"].astype(BF16)
    mask = jnp.tril(jnp.ones((HD, HD), F32))
    wm = (p["sgu_w_s"] * mask).astype(BF16)
    wmt = jnp.swapaxes(wm, 1, 2)
    bst = jnp.transpose(p["sgu_b_s"])

    h1, z = _norm_proj(x, p["norm_mix_g"], scale1, shift1, win, "mix_proj")
    hstate, ya_pre, *rnn_saved = _rnn_fwd(
        z, p["rnn_conv_w"], p["rnn_conv_b"], wa, p["lru_b_a"], wx, p["lru_b_x"], p["lru_lambda"])
    yb_pre, *sgu_saved = _sgu_fwd(z, p["sgu_ln_g"], p["sgu_ln_b"], wm, bst)
    wba, wbb, wout = late_weights("merge", [ya_pre, yb_pre])
    x2, ya, yb, merged, o1 = _merge_fwd(ya_pre, yb_pre, z, x, gate1, wba, wbb, wout)
    wup = late_weights("ffn_up", [x2])
    h2, up_a, up_v, ff, fa, fv = _ffn_proj_mid(
        x2, p["norm_ffn_g"], scale2, shift2, wup, p["ffn_conv_w"], p["ffn_conv_b"])
    wd = late_weights("ffn_down", [ff])
    dx3, do2, loss, d_gfin, d_gate2 = _ffn_out_loss(ff, wd, x2, target, gate2, p["norm_final_g"])

    dact, dval, d_wd, dcb_a, dcb_v = _ffn_down_bwd(do2, ff, fa, fv, wd)
    dup, dx2, do1, d_cwf, d_shift2, d_scale2, d_gffn, d_gate1 = _ffn_up_bwd(
        dact, dval, up_a, up_v, p["ffn_conv_w"], wup, x2, dx3, p["norm_ffn_g"], scale2, o1, gate1)
    d_wup = _xt_y(h2, dup, "w_up_grad")
    ready = grads_ready if grads_ready else (lambda stage, big, small: None)
    token = ready("ffn", {"w_up": d_wup, "w_down": d_wd}, {})

    dya, dyb, dz, d_wout, d_win = _out_bwd(do1, wout, merged, ya, yb, z, h1)
    dz, d_win, d_wba, d_cw, d_cb, d_wa, d_ba, d_wx, d_bx, d_lam = _rnn_bwd(
        dya, ya_pre, wba, h1, z, rnn_saved, hstate, dz, d_win, p["rnn_conv_w"], wa, wx,
        after_token(p["lru_lambda"], token))
    small = {
        "rnn_conv_w": d_cw, "rnn_conv_b": d_cb, "lru_w_a": d_wa, "lru_b_a": d_ba, "lru_w_x": d_wx, "lru_b_x": d_bx,
        "lru_lambda": d_lam, "norm_ffn_g": d_gffn, "ffn_conv_w": d_cwf,
        "ffn_conv_b": jnp.concatenate([dcb_a, dcb_v], axis=1), "norm_final_g": d_gfin,
    }
    token = ready("rnn", {}, small)
    dz, d_win, d_wbb, d_ws, d_bst, d_lng, d_lnb = _sgu_bwd(
        dyb, yb_pre, wbb, h1, sgu_saved, dz, d_win, p["sgu_ln_g"], after_token(p["sgu_ln_b"], token), wmt, mask)
    sgu_small = {"sgu_ln_g": d_lng, "sgu_ln_b": d_lnb, "sgu_w_s": d_ws, "sgu_b_s": jnp.transpose(d_bst)}
    mixer = {"w_in": d_win, "w_out": d_wout, "w_branch_a": d_wba, "w_branch_b": d_wbb}
    token = ready("mixer", mixer, sgu_small)
    grad_x, d_shift1, d_scale1, d_gmix = _in_bwd(dz, win, x, dx2, after_token(p["norm_mix_g"], token), scale1)

    small.update(sgu_small)
    small["norm_mix_g"] = d_gmix
    dmod = jnp.stack([d_shift1, d_scale1, d_gate1, d_shift2, d_scale2, d_gate2])
    big = {"w_in": d_win, "w_up": d_wup, "w_branch_a": d_wba, "w_branch_b": d_wbb, "w_out": d_wout, "w_down": d_wd}
    return loss, grad_x, big, small, dmod


LAST_REP = ["b_ada", "norm_mix_g"]
EARLY_REP = ["rnn_conv_b", "lru_w_a", "lru_b_a", "lru_w_x", "lru_b_x", "lru_lambda", "norm_ffn_g", "ffn_conv_b",
             "norm_final_g"]
MID_REP = ["sgu_ln_g", "sgu_ln_b", "sgu_w_s", "sgu_b_s"]
COL_SHARDED = ["rnn_conv_w", "ffn_conv_w"]
SMALL_GROUPS = {"rnn": EARLY_REP + COL_SHARDED, "mixer": MID_REP, "last": LAST_REP}
REPLICATED = LAST_REP + EARLY_REP + MID_REP
SMALL_NAMES = REPLICATED + COL_SHARDED
BIG_NAMES = ["w_in", "w_up", "w_branch_a", "w_branch_b", "w_out", "w_down"]
BIG_AXES = [1, 1, 0, 0, 0, 0]
WEIGHTS = ["w_ada", "b_ada", "norm_mix_g", "w_in", "rnn_conv_w", "rnn_conv_b", "lru_w_a", "lru_b_a", "lru_w_x",
           "lru_b_x", "lru_lambda", "sgu_ln_g", "sgu_ln_b", "sgu_w_s", "sgu_b_s", "w_branch_a", "w_branch_b",
           "w_out", "norm_ffn_g", "w_up", "ffn_conv_w", "ffn_conv_b", "w_down", "norm_final_g"]


def _pack_rows(shape):
    return math.prod(shape) // LANES


def _pack(arrays):
    return jnp.concatenate([a.reshape(-1, LANES) for a in arrays], axis=0)


def kernel(x, c, w_ada, b_ada, norm_mix_g, w_in, rnn_conv_w, rnn_conv_b, lru_w_a, lru_b_a, lru_w_x, lru_b_x, lru_lambda, sgu_ln_g, sgu_ln_b, sgu_w_s, sgu_b_s, w_branch_a, w_branch_b, w_out, norm_ffn_g, w_up, ffn_conv_w, ffn_conv_b, w_down, norm_final_g, loss_target, m_w_ada, m_b_ada, m_norm_mix_g, m_w_in, m_rnn_conv_w, m_rnn_conv_b, m_lru_w_a, m_lru_b_a, m_lru_w_x, m_lru_b_x, m_lru_lambda, m_sgu_ln_g, m_sgu_ln_b, m_sgu_w_s, m_sgu_b_s, m_w_branch_a, m_w_branch_b, m_w_out, m_norm_ffn_g, m_w_up, m_ffn_conv_w, m_ffn_conv_b, m_w_down, m_norm_final_g, v_w_ada, v_b_ada, v_norm_mix_g, v_w_in, v_rnn_conv_w, v_rnn_conv_b, v_lru_w_a, v_lru_b_a, v_lru_w_x, v_lru_b_x, v_lru_lambda, v_sgu_ln_g, v_sgu_ln_b, v_sgu_w_s, v_sgu_b_s, v_w_branch_a, v_w_branch_b, v_w_out, v_norm_ffn_g, v_w_up, v_ffn_conv_w, v_ffn_conv_b, v_w_down, v_norm_final_g):
    given = dict(locals())
    me = 4 * lax.axis_index("x") + 2 * lax.axis_index("y") + lax.axis_index("c")
    ada_cols = w_ada.shape[2]
    conv_cols = {"rnn_conv_w": rnn_conv_w.shape[2], "ffn_conv_w": ffn_conv_w.shape[2]}

    (win, c_all, cw_rnn, cw_ffn), _ = _all_gather(
        [w_in[0].astype(BF16), c.reshape(1, 1, D), rnn_conv_w[0], ffn_conv_w[0]], [1, 0, 1, 1], "gather_first")
    c_all = c_all.reshape(N_DEV, D)

    b_cols = lax.dynamic_slice_in_dim(b_ada, me * ada_cols, ada_cols, axis=1)
    (mod_all,), mod_done = _all_gather(
        [_mod_cols(c_all, w_ada[0], b_cols).reshape(1, N_DEV, ada_cols)], [0], "gather_mod")
    mod_all = mod_all.reshape(N_DEV, N_DEV, ada_cols)
    mod_mine = lax.dynamic_index_in_dim(mod_all, me, axis=1, keepdims=False).reshape(6, 1, D)

    late_groups = {"merge": (["w_branch_a", "w_branch_b", "w_out"], [0, 0, 0]), "ffn_up": (["w_up"], [1]),
                   "ffn_down": (["w_down"], [0])}
    in_flight, started = {}, mod_done[0:1, 0:1]
    for stage, (names, axes) in late_groups.items():
        shards = [(given[n][0] + started).astype(BF16) for n in names]
        plan = _gather_plan(axes, [s.shape[-1] for s in shards])
        send, recv, srcs, lands, token = _exchange_start(
            "gather_start_" + stage, shards, [_own_block_placed(s, ax, me) for s, ax in zip(shards, axes)], plan,
            len(shards) * (N_DEV - 1))
        in_flight[stage] = (send, recv, srcs, lands, plan)
        started = started + token[0:1, 0:1]

    def late_weights(stage, after):
        send, recv, srcs, lands, plan = in_flight[stage]
        full = _exchange_wait("gather_wait_" + stage, send, recv, srcs, lands, plan, after)
        full = [w.reshape(-1, D) if ax == 0 else w for w, ax in zip(full, late_groups[stage][1])]
        return full if len(full) > 1 else full[0]

    mod_mine = mod_mine + started

    reducing, packing = {}, {}

    def start_pack(stage, small):
        pack = _pack([small[n] for n in SMALL_GROUPS[stage]])[None]
        plan = _gather_plan([0], [LANES])
        send, recv, srcs, lands, tok = _exchange_start(
            "small_start_" + stage, [pack], [_own_block_placed(pack, 0, me)], plan, N_DEV - 1)
        packing[stage] = (send, recv, srcs, lands, plan)
        return tok

    def grads_ready(stage, grads, small):
        tokens = [start_pack(stage, small)] if small else []
        if grads:
            tokens.append(start_reduce(stage, grads))
        return sum(tokens[1:], tokens[0])

    def start_reduce(stage, grads):
        names = [n for n in BIG_NAMES if n in grads]
        blocked = {}
        for n in names:
            ax = BIG_AXES[BIG_NAMES.index(n)]
            g = grads[n] if ax == 1 else grads[n].reshape(N_DEV, grads[n].shape[0] // N_DEV, grads[n].shape[1])
            blocked.setdefault((ax, g.shape), []).append((n, g))
        sums = {}
        for (ax, _), group in blocked.items():
            reduced = _sibling_reduce([g for _, g in group], ax, "reduce_sibling_" + "_".join(n for n, _ in group))
            sums.update({n: r for (n, _), r in zip(group, reduced)})
        sums = [sums[n] for n in names]
        pays = [pay for _, pay in sums]
        send, recv, srcs, lands, tok = _exchange_start(
            "reduce_start_" + stage, pays, [lax.empty(p_.shape, p_.dtype) for p_ in pays], _chip_plan, 3 * len(pays))
        reducing[stage] = (names, [own for own, _ in sums], send, recv, srcs, lands)
        return tok

    p = {n: given[n][0] for n in REPLICATED if n not in ("b_ada", "norm_final_g")}
    p = {n: (a.reshape(1, -1) if a.ndim == 1 else a) for n, a in p.items()}
    p["rnn_conv_w"], p["ffn_conv_w"] = cw_rnn, cw_ffn
    p["norm_final_g"] = norm_final_g.reshape(1, D)
    loss, grad_x, _, small, dmod = _local_step(x[0], loss_target[0], mod_mine, win, late_weights, p, grads_ready)

    small["b_ada"] = dmod.reshape(1, 6 * D)
    rows_of = {n: _pack_rows(small[n].shape) for n in SMALL_NAMES}
    (last,), _ = _all_gather([_pack([small[n] for n in LAST_REP])[None]], [0], "gather_small")
    gathered = {"last": last}
    for stage, (send, recv, srcs, lands, plan) in packing.items():
        (gathered[stage],) = _exchange_wait("small_wait_" + stage, send, recv, srcs, lands, plan, [grad_x])
    gathered = {k: v.reshape(N_DEV, -1, LANES) for k, v in gathered.items()}

    out = {}
    for stage, (names, owns, send, recv, srcs, lands) in reducing.items():
        landed = _exchange_wait("reduce_wait_" + stage, send, recv, srcs, lands, _chip_plan, [last])
        for n, own, got in zip(names, owns, landed):
            out[n] = _adamw(given[n][0], given["m_" + n][0], given["v_" + n][0], [own, got], "adamw_" + n)

    dmod_all = gathered["last"][:, :rows_of["b_ada"]].reshape(N_DEV, 6 * D)
    dmod_cols = lax.dynamic_slice_in_dim(dmod_all, me * ada_cols, ada_cols, axis=1)
    out["w_ada"] = _adamw(w_ada[0], m_w_ada[0], v_w_ada[0], [_ada_grad(c_all, dmod_cols)], "adamw_w_ada")

    def rows_form(a):
        return a.reshape(1, -1) if a.size // a.shape[-1] == 1 or a.ndim == 1 else a.reshape(-1, LANES)

    for stage, names in (("last", LAST_REP), ("rnn", EARLY_REP), ("mixer", MID_REP)):
        out.update(_adamw_group(names, *[[rows_form(given[pre + n]) for n in names] for pre in ("", "m_", "v_")],
                                gathered[stage], "adamw_small_" + stage))

    row0 = sum(rows_of[n] for n in EARLY_REP)
    for n in COL_SHARDED:
        full = gathered["rnn"][:, row0:row0 + rows_of[n]].reshape(N_DEV, small[n].shape[0], small[n].shape[1])
        mine = lax.dynamic_slice_in_dim(full, me * conv_cols[n], conv_cols[n], axis=2)
        out[n] = _adamw(given[n][0], given["m_" + n][0], given["v_" + n][0], [mine], "adamw_" + n)
        row0 += rows_of[n]

    total = lax.psum(loss[0, 0], ("x", "y", "c"))
    results = [total, grad_x[None]]
    for kind in range(4):
        results += [out[n][kind].reshape(given[n].shape) for n in WEIGHTS]
    return tuple(results)
```

```python
import math

import jax
import jax.numpy as jnp
from jax import lax
from jax.experimental import pallas as pl
from jax.experimental.pallas import tpu as pltpu

F32 = jnp.float32
BF16 = jnp.bfloat16
MESH_IDS = pl.DeviceIdType.MESH

D = 1024
NH = 8
HD = 128
NCOL_IN = 6 * D
DFF = 3 * D
N_DEV = 8
EPS = 1e-6
LRU_C = 8.0
ADAM_LR, ADAM_B1, ADAM_B2, ADAM_EPS, ADAM_WD, ADAM_STEP = 0.001, 0.9, 0.999, 1e-08, 0.01, 10

SUBLANES = 8
LANES = 128
HALO = 16
VMEM_LIMIT = 56 * 1024 * 1024
GELU_K = math.sqrt(2.0 / math.pi)
GELU_C = 0.044715


def _cparams(n_axes):
    return pltpu.CompilerParams(dimension_semantics=("arbitrary",) * n_axes, vmem_limit_bytes=VMEM_LIMIT)


def _const_spec(shape, single_buffer=False):
    nd = len(shape)
    if single_buffer:
        return pl.BlockSpec(shape, lambda *_: (0,) * nd, pipeline_mode=pl.Buffered(1))
    return pl.BlockSpec(shape, lambda *_: (0,) * nd)


def _vec_operand(v):
    if isinstance(v, tuple):
        stack, k = v
        return stack, pl.BlockSpec((None, 1, D), lambda *_: (k, 0, 0))
    return v, _const_spec((1, D))


def _tile_big(t):
    return min(512, t)


def _tile_seq(t):
    return min(256, t)


def _row_tile(rows, cols):
    cap = max(SUBLANES, (2 * 1024 * 1024) // (4 * cols) // SUBLANES * SUBLANES)
    if rows <= cap:
        return rows
    return next(tr for tr in range(cap, 0, -SUBLANES) if rows % tr == 0)


def _gelu_t(x):
    x2 = x * x
    t = jnp.tanh(x * (GELU_K + (GELU_K * GELU_C) * x2))
    hx = 0.5 * x
    return hx + hx * t, (x2, hx, t)


def _gelu_grad(shared):
    x2, hx, t = shared
    return (0.5 + 0.5 * t) + (hx * (1.0 - t * t)) * (GELU_K + (3.0 * GELU_K * GELU_C) * x2)


def _sigmoid(x):
    return 1.0 / (1.0 + jnp.exp(-x))


def _log_sigmoid(x):
    return -(jnp.maximum(-x, 0.0) + jnp.log1p(jnp.exp(-jnp.abs(x))))


def _row_iota(cols):
    return lax.broadcasted_iota(jnp.int32, (SUBLANES, cols), 0)


def _shift_down(x, k, prev8):
    if k == 0:
        return x
    r = pltpu.roll(x, k, 0)
    p = pltpu.roll(prev8, k, 0)
    head = jnp.where(_row_iota(x.shape[1]) < k, p, r[:SUBLANES])
    return jnp.concatenate([head, r[SUBLANES:]], axis=0)


def _shift_up(x, k, next8):
    if k == 0:
        return x
    n = x.shape[0]
    r = pltpu.roll(x, n - k, 0)
    q = pltpu.roll(next8, SUBLANES - k, 0)
    tail = jnp.where(_row_iota(x.shape[1]) >= SUBLANES - k, q, r[n - SUBLANES:])
    return jnp.concatenate([r[:n - SUBLANES], tail], axis=0)


def _heads_nn(x_bf, w_ref):
    return jnp.concatenate(
        [jnp.dot(x_bf[:, h * HD:(h + 1) * HD], w_ref[h], preferred_element_type=F32) for h in range(NH)], axis=1)


def _heads_nt(x_bf, w_ref):
    return jnp.concatenate(
        [lax.dot_general(x_bf[:, h * HD:(h + 1) * HD], w_ref[h], (((1,), (1,)), ((), ())), preferred_element_type=F32)
         for h in range(NH)], axis=1)


def _dot_nt(a, b):
    return lax.dot_general(a, b, (((1,), (1,)), ((), ())), preferred_element_type=F32)


def _dot_tn(a, b):
    return lax.dot_general(a, b, (((0,), (0,)), ((), ())), preferred_element_type=F32)


def _colsum(x):
    return jnp.sum(x, axis=0, keepdims=True)


def _prev_halo_map(tm, col):
    return lambda i, *_: (jnp.maximum(i * (tm // HALO) - 1, 0), col)


def _norm_proj(x, g, scale, shift, w, name):
    t, n = x.shape[0], w.shape[1]
    tm = _tile_big(t)

    def body(x_ref, g_ref, sc_ref, sh_ref, w_ref, h_ref, z_ref):
        xv = x_ref[...]
        r = lax.rsqrt(jnp.mean(xv * xv, axis=-1, keepdims=True) + EPS)
        hb = ((xv * r * g_ref[...]) * (1.0 + sc_ref[...]) + sh_ref[...]).astype(BF16)
        h_ref[...] = hb
        for c0 in range(0, n, D):
            z_ref[:, c0:c0 + D] = jnp.dot(hb, w_ref[:, c0:c0 + D], preferred_element_type=F32).astype(BF16)

    vec = _const_spec((1, D))
    (scale, sc_spec), (shift, sh_spec) = _vec_operand(scale), _vec_operand(shift)
    return pl.pallas_call(
        body, name=name, grid=(t // tm,),
        in_specs=[pl.BlockSpec((tm, D), lambda i: (i, 0)), vec, sc_spec, sh_spec, _const_spec((D, n), True)],
        out_specs=[pl.BlockSpec((tm, D), lambda i: (i, 0)), pl.BlockSpec((tm, n), lambda i: (i, 0))],
        out_shape=[jax.ShapeDtypeStruct((t, D), BF16), jax.ShapeDtypeStruct((t, n), BF16)],
        compiler_params=_cparams(1),
    )(x, g, scale, shift, w)


def _lru_gates(xc, wa_ref, ba, wx_ref, bx, ls):
    xb = xc.astype(BF16)
    ra = _sigmoid(_heads_nn(xb, wa_ref) + ba)
    ia = _sigmoid(_heads_nn(xb, wx_ref) + bx)
    la = LRU_C * ra * ls
    a = jnp.exp(la)
    mult = jnp.sqrt(-jnp.tanh(la) * (1.0 + a * a))
    return ra, ia, a, mult


def _conv4(xr, prev8, cw_ref, cb):
    return (cb + cw_ref[3:4, :] * xr + cw_ref[2:3, :] * _shift_down(xr, 1, prev8)
            + cw_ref[1:2, :] * _shift_down(xr, 2, prev8) + cw_ref[0:1, :] * _shift_down(xr, 3, prev8))


def _rnn_fwd(z, cw, cb, wa, ba, wx, bx, lam):
    t = z.shape[0]
    tm = _tile_seq(t)
    ngrp = tm // SUBLANES

    def body(xr_ref, xp_ref, gr_ref, cw_ref, cb_ref, wa_ref, ba_ref, wx_ref, bx_ref, lam_ref,
             h_ref, ya_ref, xc_ref, ra_ref, ia_ref, gg_ref, hg_ref, carry_ref, a_scr, u_scr):
        i = pl.program_id(0)

        @pl.when(i == 0)
        def _():
            carry_ref[...] = jnp.zeros_like(carry_ref)

        xr = xr_ref[...].astype(F32)
        prev8 = jnp.where(i == 0, 0.0, xp_ref[...].astype(F32)[HALO - SUBLANES:])
        xc = _conv4(xr, prev8, cw_ref, cb_ref[...])
        ra, ia, a, mult = _lru_gates(xc, wa_ref, ba_ref[...], wx_ref, bx_ref[...], _log_sigmoid(lam_ref[...]))
        xc_ref[...] = xc.astype(BF16)
        ra_ref[...] = ra.astype(BF16)
        ia_ref[...] = ia.astype(BF16)
        a_scr[...] = a
        u_scr[...] = mult * (ia * xc)
        row = _row_iota(D)

        def grp(j, carry):
            r0 = pl.multiple_of(j * SUBLANES, SUBLANES)
            av = a_scr[pl.ds(r0, SUBLANES), :]
            uv = u_scr[pl.ds(r0, SUBLANES), :]
            for d in (1, 2, 4):
                m = row >= d
                uv = jnp.where(m, av * pltpu.roll(uv, d, 0) + uv, uv)
                av = jnp.where(m, av * pltpu.roll(av, d, 0), av)
            hv = uv + av * carry
            h_ref[pl.ds(r0, SUBLANES), :] = hv
            return hv[SUBLANES - 1:SUBLANES, :]

        carry_ref[0:1, :] = lax.fori_loop(0, ngrp, grp, carry_ref[0:1, :])
        grv = gr_ref[...].astype(F32)
        gg, tg = _gelu_t(grv)
        hv = h_ref[...]
        ya_ref[...] = (hv * gg).astype(BF16)
        gg_ref[...] = gg.astype(BF16)
        hg_ref[...] = (hv * _gelu_grad(tg)).astype(BF16)

    vec = _const_spec((1, D))
    wspec = _const_spec((NH, HD, HD))
    tile = pl.BlockSpec((tm, D), lambda i: (i, 0))
    bshape = jax.ShapeDtypeStruct((t, D), BF16)
    return pl.pallas_call(
        body, name="rnn_fwd", grid=(t // tm,),
        in_specs=[tile, pl.BlockSpec((HALO, D), _prev_halo_map(tm, 0)),
                  pl.BlockSpec((tm, D), lambda i: (i, 1)), _const_spec((4, D)), vec, wspec, vec, wspec, vec, vec],
        out_specs=[tile] * 7,
        out_shape=[jax.ShapeDtypeStruct((t, D), F32)] + [bshape] * 6,
        scratch_shapes=[pltpu.VMEM((SUBLANES, D), F32), pltpu.VMEM((tm, D), F32), pltpu.VMEM((tm, D), F32)],
        compiler_params=_cparams(1),
    )(z, z, z, cw, cb, wa, ba, wx, bx, lam)


def _sgu_fwd(z, lng, lnb, wm, bst):
    t = z.shape[0]
    tm = _tile_seq(t)

    def body(zu_ref, zv_ref, lng_ref, lnb_ref, wm_ref, bst_ref, yb_ref, gu_ref, mg_ref, vh_ref, gpv_ref, rstd_ref):
        gu, su = _gelu_t(zu_ref[...].astype(F32))
        gv, sv = _gelu_t(zv_ref[...].astype(F32))
        mu = jnp.mean(gv, axis=-1, keepdims=True)
        cen = gv - mu
        rstd = lax.rsqrt(jnp.mean(cen * cen, axis=-1, keepdims=True) + EPS)
        vhat = cen * rstd
        vb = (vhat * lng_ref[...] + lnb_ref[...]).astype(BF16)
        rows = []
        for b0 in range(0, tm, HD):
            rows.append(jnp.concatenate(
                [jnp.dot(wm_ref[g], vb[b0:b0 + HD, g * HD:(g + 1) * HD], preferred_element_type=F32)
                 + bst_ref[:, g:g + 1] for g in range(NH)], axis=1))
        mixed = jnp.concatenate(rows, axis=0) if len(rows) > 1 else rows[0]
        yb_ref[...] = (gu * mixed).astype(BF16)
        gu_ref[...] = gu.astype(BF16)
        mg_ref[...] = (mixed * _gelu_grad(su)).astype(BF16)
        vh_ref[...] = vhat.astype(BF16)
        gpv_ref[...] = _gelu_grad(sv).astype(BF16)
        rstd_ref[...] = rstd

    vec = _const_spec((1, D))
    tile = pl.BlockSpec((tm, D), lambda i: (i, 0))
    bshape = jax.ShapeDtypeStruct((t, D), BF16)
    return pl.pallas_call(
        body, name="sgu_fwd", grid=(t // tm,),
        in_specs=[pl.BlockSpec((tm, D), lambda i: (i, 2)), pl.BlockSpec((tm, D), lambda i: (i, 3)), vec, vec,
                  _const_spec((NH, HD, HD)), _const_spec((HD, NH))],
        out_specs=[tile] * 5 + [pl.BlockSpec((tm, 1), lambda i: (i, 0))],
        out_shape=[bshape] * 5 + [jax.ShapeDtypeStruct((t, 1), F32)],
        compiler_params=_cparams(1),
    )(z, z, lng, lnb, wm, bst)


def _merge_fwd(ya_pre, yb_pre, z, x, gate1, wba, wbb, wout):
    t = x.shape[0]
    tm = _tile_big(t)

    def body(yap_ref, ybp_ref, ga_ref, gb_ref, x_ref, g1_ref, wba_ref, wbb_ref, wo_ref,
             x2_ref, ya_ref, yb_ref, mg_ref, o1_ref):
        ya = jnp.dot(yap_ref[...], wba_ref[...], preferred_element_type=F32)
        yb = jnp.dot(ybp_ref[...], wbb_ref[...], preferred_element_type=F32)
        merged = _sigmoid(ga_ref[...].astype(F32)) * ya + _sigmoid(gb_ref[...].astype(F32)) * yb
        mb = merged.astype(BF16)
        o1 = jnp.dot(mb, wo_ref[...], preferred_element_type=F32)
        x2_ref[...] = x_ref[...] + g1_ref[...] * o1
        ya_ref[...] = ya.astype(BF16)
        yb_ref[...] = yb.astype(BF16)
        mg_ref[...] = mb
        o1_ref[...] = o1.astype(BF16)

    tile = pl.BlockSpec((tm, D), lambda i: (i, 0))
    wspec = _const_spec((D, D))
    bshape = jax.ShapeDtypeStruct((t, D), BF16)
    gate1, g1_spec = _vec_operand(gate1)
    return pl.pallas_call(
        body, name="merge_fwd", grid=(t // tm,),
        in_specs=[tile, tile, pl.BlockSpec((tm, D), lambda i: (i, 4)), pl.BlockSpec((tm, D), lambda i: (i, 5)),
                  tile, g1_spec, wspec, wspec, wspec],
        out_specs=[tile] * 5,
        out_shape=[jax.ShapeDtypeStruct((t, D), F32), bshape, bshape, bshape, bshape],
        compiler_params=_cparams(1),
    )(ya_pre, yb_pre, z, z, x, gate1, wba, wbb, wout)


def _conv3(u, prev8, cw_ref, cb):
    return cb + cw_ref[2:3, :] * u + cw_ref[1:2, :] * _shift_down(u, 1, prev8) + cw_ref[0:1, :] * _shift_down(u, 2, prev8)


def _ffn_proj_mid(x2, g, scale, shift, w, cw, cb):
    t = x2.shape[0]
    tm = _tile_big(t)
    nc = DFF // D

    def body(x_ref, g_ref, sc_ref, sh_ref, wa_ref, wv_ref, cwa_ref, cwv_ref, cba_ref, cbv_ref,
             h_ref, upa_ref, upv_ref, ff_ref, fa_ref, fv_ref, hb_scr, prev_ref):
        i, c = pl.program_id(0), pl.program_id(1)

        @pl.when(i == 0)
        def _():
            prev_ref[c] = jnp.zeros((2, SUBLANES, D), F32)

        @pl.when(c == 0)
        def _():
            xv = x_ref[...]
            r = lax.rsqrt(jnp.mean(xv * xv, axis=-1, keepdims=True) + EPS)
            hb_scr[...] = ((xv * r * g_ref[...]) * (1.0 + sc_ref[...]) + sh_ref[...]).astype(BF16)
            h_ref[...] = hb_scr[...]

        hb = hb_scr[...]
        halves = []
        for s, (w_ref, up_ref, cw_ref, cb_ref) in enumerate(((wa_ref, upa_ref, cwa_ref, cba_ref),
                                                             (wv_ref, upv_ref, cwv_ref, cbv_ref))):
            u = jnp.dot(hb, w_ref[...], preferred_element_type=F32)
            up_ref[...] = u.astype(BF16)
            halves.append(_conv3(u, prev_ref[c, s], cw_ref, cb_ref[...]))
            prev_ref[c, s] = u[tm - SUBLANES:]
        act, val = halves
        ga, ta = _gelu_t(act)
        ff_ref[...] = (ga * val).astype(BF16)
        fa_ref[...] = (val * _gelu_grad(ta)).astype(BF16)
        fv_ref[...] = ga.astype(BF16)

    def cols(rows, off):
        return pl.BlockSpec((rows, D), lambda i, c: (0, off + c))

    vec = pl.BlockSpec((1, D), lambda i, c: (0, 0))
    row_tile = pl.BlockSpec((tm, D), lambda i, c: (i, 0))
    chunk = pl.BlockSpec((tm, D), lambda i, c: (i, c))
    hshape = jax.ShapeDtypeStruct((t, DFF), BF16)
    (scale, sc_spec), (shift, sh_spec) = _vec_operand(scale), _vec_operand(shift)
    return pl.pallas_call(
        body, name="ffn_proj_mid", grid=(t // tm, nc),
        in_specs=[row_tile, vec, sc_spec, sh_spec, cols(D, 0), cols(D, nc), cols(3, 0), cols(3, nc), cols(1, 0), cols(1, nc)],
        out_specs=[row_tile, chunk, chunk, chunk, chunk, chunk],
        out_shape=[jax.ShapeDtypeStruct((t, D), BF16), hshape, hshape, hshape, hshape, hshape],
        scratch_shapes=[pltpu.VMEM((tm, D), BF16), pltpu.VMEM((nc, 2, SUBLANES, D), F32)],
        compiler_params=_cparams(2),
    )(x2, g, scale, shift, w, w, cw, cw, cb, cb)


def _ffn_out_loss(ff, wd, x2, target, gate2, gfin):
    t = x2.shape[0]
    tm = _tile_big(t)

    def body(ff_ref, wd_ref, x2_ref, tg_ref, g2_ref, gf_ref, dx3_ref, do2_ref, loss_ref, dgf_ref, dg2_ref):
        @pl.when(pl.program_id(0) == 0)
        def _():
            loss_ref[...] = jnp.zeros_like(loss_ref)
            dgf_ref[...] = jnp.zeros_like(dgf_ref)
            dg2_ref[...] = jnp.zeros_like(dg2_ref)

        o2 = jnp.dot(ff_ref[...], wd_ref[...], preferred_element_type=F32)
        x3 = x2_ref[...] + g2_ref[...] * o2
        r = lax.rsqrt(jnp.mean(x3 * x3, axis=-1, keepdims=True) + EPS)
        xhat = x3 * r
        err = xhat * gf_ref[...] - tg_ref[...]
        loss_ref[...] += 0.5 * jnp.sum(jnp.mean(err * err, axis=-1, keepdims=True), axis=0, keepdims=True)
        dy = err * (1.0 / D)
        dgf_ref[...] += _colsum(dy * xhat)
        dxh = dy * gf_ref[...]
        dx3 = r * (dxh - xhat * jnp.mean(dxh * xhat, axis=-1, keepdims=True))
        dx3_ref[...] = dx3
        do2_ref[...] = (dx3 * g2_ref[...]).astype(BF16)
        dg2_ref[...] += _colsum(dx3 * o2)

    tile = pl.BlockSpec((tm, D), lambda i: (i, 0))
    vec = _const_spec((1, D))
    gate2, g2_spec = _vec_operand(gate2)
    return pl.pallas_call(
        body, name="ffn_out_loss", grid=(t // tm,),
        in_specs=[pl.BlockSpec((tm, DFF), lambda i: (i, 0)), _const_spec((DFF, D), True), tile, tile, g2_spec, vec],
        out_specs=[tile, tile, _const_spec((1, 1)), vec, vec],
        out_shape=[jax.ShapeDtypeStruct((t, D), F32), jax.ShapeDtypeStruct((t, D), BF16),
                   jax.ShapeDtypeStruct((1, 1), F32),
                   jax.ShapeDtypeStruct((1, D), F32), jax.ShapeDtypeStruct((1, D), F32)],
        compiler_params=_cparams(1),
    )(ff, wd, x2, target, gate2, gfin)


def _ffn_down_bwd(do2, ff, fa, fv, wd):
    t = do2.shape[0]
    tm = min(1024, t)
    nc = DFF // D

    def body(do2_ref, ff_ref, fa_ref, fv_ref, wd_ref, da_ref, dv_ref, dwd_ref, dcba_ref, dcbv_ref):
        @pl.when(pl.program_id(1) == 0)
        def _():
            for r in (dwd_ref, dcba_ref, dcbv_ref):
                r[...] = jnp.zeros_like(r)

        do2 = do2_ref[...]
        dwd_ref[...] += _dot_tn(ff_ref[...], do2)
        dff = _dot_nt(do2, wd_ref[...])
        dact = dff * fa_ref[...].astype(F32)
        dval = dff * fv_ref[...].astype(F32)
        da_ref[...] = dact.astype(BF16)
        dv_ref[...] = dval.astype(BF16)
        dcba_ref[...] += _colsum(dact)
        dcbv_ref[...] += _colsum(dval)

    blk = pl.BlockSpec((tm, D), lambda c, i: (i, c))
    vec = pl.BlockSpec((1, D), lambda c, i: (0, c))
    return pl.pallas_call(
        body, name="ffn_down_bwd", grid=(nc, t // tm),
        in_specs=[pl.BlockSpec((tm, D), lambda c, i: (i, 0)),
                  blk, blk, blk, pl.BlockSpec((D, D), lambda c, i: (c, 0))],
        out_specs=[blk, blk, pl.BlockSpec((D, D), lambda c, i: (c, 0)), vec, vec],
        out_shape=[jax.ShapeDtypeStruct((t, DFF), BF16), jax.ShapeDtypeStruct((t, DFF), BF16),
                   jax.ShapeDtypeStruct((DFF, D), F32),
                   jax.ShapeDtypeStruct((1, DFF), F32), jax.ShapeDtypeStruct((1, DFF), F32)],
        compiler_params=_cparams(2),
    )(do2, ff, fa, fv, wd)


def _modnorm_bwd(dh, xv, g, scale):
    r = lax.rsqrt(jnp.mean(xv * xv, axis=-1, keepdims=True) + EPS)
    xhat = xv * r
    dxn = dh * (1.0 + scale)
    dxh = dxn * g
    dx = r * (dxh - xhat * jnp.mean(dxh * xhat, axis=-1, keepdims=True))
    return dx, _colsum(dh), _colsum(dh * (xhat * g)), _colsum(dxn * xhat)


def _ffn_up_bwd(dact, dval, up_a, up_v, cw, wup, x2, dx3, gffn, scale2, o1, gate1):
    t = x2.shape[0]
    tm = _tile_seq(t)
    nt = t // tm
    nc = DFF // D

    def body(da_ref, dan_ref, dv_ref, dvn_ref, ua_ref, uv_ref, cw_ref, w_ref, x2_ref, dx3_ref, g_ref, sc_ref, o1_ref, g1_ref,
             dup_ref, dx2_ref, do1_ref, dcw_ref, dsh_ref, dsc_ref, dg_ref, dg1_ref):
        i = pl.program_id(0)

        @pl.when(i == 0)
        def _():
            for r in (dcw_ref, dsh_ref, dsc_ref, dg_ref, dg1_ref):
                r[...] = jnp.zeros_like(r)

        last = i == nt - 1
        dh = jnp.zeros((tm, D), F32)
        for half, (d_ref, dn_ref, u_ref) in enumerate(((da_ref, dan_ref, ua_ref), (dv_ref, dvn_ref, uv_ref))):
            nxt = jnp.where(last, 0.0, dn_ref[...].astype(F32)[:SUBLANES])
            for c in range(nc):
                c0 = half * DFF + c * D
                dv = d_ref[:, c * D:(c + 1) * D].astype(F32)
                nx = nxt[:, c * D:(c + 1) * D]
                taps = (_shift_up(dv, 2, nx), _shift_up(dv, 1, nx), dv)
                dup = (cw_ref[2:3, c0:c0 + D] * taps[2] + cw_ref[1:2, c0:c0 + D] * taps[1]
                       + cw_ref[0:1, c0:c0 + D] * taps[0]).astype(BF16)
                upv = u_ref[:, c * D:(c + 1) * D].astype(F32)
                for k in range(3):
                    dcw_ref[k:k + 1, c0:c0 + D] += _colsum(taps[k] * upv)
                dup_ref[:, c0:c0 + D] = dup
                dh = dh + _dot_nt(dup, w_ref[:, c0:c0 + D])
        dxn, dsh, dsc, dg = _modnorm_bwd(dh, x2_ref[...], g_ref[...], sc_ref[...])
        dx2 = dx3_ref[...] + dxn
        dx2_ref[...] = dx2
        do1_ref[...] = (dx2 * g1_ref[...]).astype(BF16)
        dsh_ref[...] += dsh
        dsc_ref[...] += dsc
        dg_ref[...] += dg
        dg1_ref[...] += _colsum(dx2 * o1_ref[...].astype(F32))

    tile = pl.BlockSpec((tm, D), lambda i: (i, 0))
    wide = pl.BlockSpec((tm, DFF), lambda i: (i, 0))
    nxt = pl.BlockSpec((HALO, DFF), lambda i: (jnp.minimum((i + 1) * (tm // HALO), t // HALO - 1), 0))
    vec = _const_spec((1, D))
    vshape = jax.ShapeDtypeStruct((1, D), F32)
    (scale2, sc_spec), (gate1, g1_spec) = _vec_operand(scale2), _vec_operand(gate1)
    return pl.pallas_call(
        body, name="ffn_up_bwd", grid=(nt,),
        in_specs=[wide, nxt, wide, nxt, wide, wide,
                  _const_spec((3, 2 * DFF)), _const_spec((D, 2 * DFF), True),
                  tile, tile, vec, sc_spec, tile, g1_spec],
        out_specs=[pl.BlockSpec((tm, 2 * DFF), lambda i: (i, 0)), tile, tile, _const_spec((3, 2 * DFF)),
                   vec, vec, vec, vec],
        out_shape=[jax.ShapeDtypeStruct((t, 2 * DFF), BF16), jax.ShapeDtypeStruct((t, D), F32),
                   jax.ShapeDtypeStruct((t, D), BF16), jax.ShapeDtypeStruct((3, 2 * DFF), F32),
                   vshape, vshape, vshape, vshape],
        compiler_params=_cparams(1),
    )(dact, dact, dval, dval, up_a, up_v, cw, wup, x2, dx3, gffn, scale2, o1, gate1)


def _xt_y(a, b, name):
    t, k = a.shape
    n = b.shape[1]
    tm = min(1024, t)
    bn = 3072 if n % 3072 == 0 else D

    def body(a_ref, b_ref, o_ref):
        @pl.when(pl.program_id(1) == 0)
        def _():
            o_ref[...] = jnp.zeros_like(o_ref)

        o_ref[...] += _dot_tn(a_ref[...], b_ref[...])

    return pl.pallas_call(
        body, name=name, grid=(n // bn, t // tm),
        in_specs=[pl.BlockSpec((tm, k), lambda j, i: (i, 0)), pl.BlockSpec((tm, bn), lambda j, i: (i, j))],
        out_specs=pl.BlockSpec((k, bn), lambda j, i: (0, j)),
        out_shape=jax.ShapeDtypeStruct((k, n), F32),
        compiler_params=_cparams(2),
    )(a, b)


def _acc_spec(shape, index):
    return pl.BlockSpec(shape, lambda *_: index, pipeline_mode=pl.Buffered(1))


def _out_bwd(do1, wout, merged, ya, yb, z, h1):
    t = do1.shape[0]
    tm = _tile_big(t)

    def body(do1_ref, wo_ref, mg_ref, ya_ref, yb_ref, ga_ref, gb_ref, h1_ref,
             dya_ref, dyb_ref, dz_ref, dwo_ref, dwin_ref):
        @pl.when(pl.program_id(0) == 0)
        def _():
            dwo_ref[...] = jnp.zeros_like(dwo_ref)
            dwin_ref[...] = jnp.zeros_like(dwin_ref)

        do1v = do1_ref[...]
        dwo_ref[...] += _dot_tn(mg_ref[...], do1v)
        dm = _dot_nt(do1v, wo_ref[...])
        sa = _sigmoid(ga_ref[...].astype(F32))
        sb = _sigmoid(gb_ref[...].astype(F32))
        dya_ref[...] = (dm * sa).astype(BF16)
        dyb_ref[...] = (dm * sb).astype(BF16)
        dga = (dm * ya_ref[...].astype(F32) * sa * (1.0 - sa)).astype(BF16)
        dgb = (dm * yb_ref[...].astype(F32) * sb * (1.0 - sb)).astype(BF16)
        dz_ref[:, 0:D] = dga
        dz_ref[:, D:2 * D] = dgb
        h1v = h1_ref[...]
        dwin_ref[:, 0:D] += _dot_tn(h1v, dga)
        dwin_ref[:, D:2 * D] += _dot_tn(h1v, dgb)

    tile = pl.BlockSpec((tm, D), lambda i: (i, 0))
    bshape = jax.ShapeDtypeStruct((t, D), BF16)
    return pl.pallas_call(
        body, name="out_bwd", grid=(t // tm,),
        in_specs=[tile, _const_spec((D, D), True), tile, tile, tile,
                  pl.BlockSpec((tm, D), lambda i: (i, 4)), pl.BlockSpec((tm, D), lambda i: (i, 5)), tile],
        out_specs=[tile, tile, pl.BlockSpec((tm, 2 * D), lambda i: (i, 2)), _acc_spec((D, D), (0, 0)),
                   _acc_spec((D, 2 * D), (0, 2))],
        out_shape=[bshape, bshape, jax.ShapeDtypeStruct((t, NCOL_IN), BF16), jax.ShapeDtypeStruct((D, D), F32),
                   jax.ShapeDtypeStruct((D, NCOL_IN), F32)],
        compiler_params=_cparams(1),
    )(do1, wout, merged, ya, yb, z, z, h1)


def _rnn_bwd(dya, ya_pre, wba, h1, z, saved, h, dz, dwin, cw, wa, wx, lam):
    t = z.shape[0]
    tm = _tile_seq(t)
    nt = t // tm
    ngrp = tm // SUBLANES
    hpt = tm // HALO

    def body(dya_ref, yap_ref, wba_ref, h1_ref, xr_ref, xc_ref, ra_ref, ia_ref, gg_ref, hg_ref, h_ref, hp_ref,
             dz_any, dwin_any, cw_ref, wa_ref, wx_ref, lam_ref,
             dz_ref, dwin_ref, dwba_ref, dcw_ref, dcb_ref, dwa_ref, dba_ref, dwx_ref, dbx_ref, dlam_ref,
             a_first, g_first, dxc_first, b_scr, d_scr, g_scr):
        del dz_any, dwin_any
        i = pl.program_id(0)

        @pl.when(i == 0)
        def _():
            for r in (dwin_ref, dwba_ref, dcw_ref, dcb_ref, dwa_ref, dba_ref, dwx_ref, dbx_ref, dlam_ref,
                      a_first, g_first, dxc_first):
                r[...] = jnp.zeros_like(r)

        dya_v = dya_ref[...]
        dwba_ref[...] += _dot_tn(yap_ref[...], dya_v)
        dyap_v = _dot_nt(dya_v, wba_ref[...])
        h1v = h1_ref[...]

        first_tile = i == nt - 1
        xc = xc_ref[...].astype(F32)
        ra = ra_ref[...].astype(F32)
        ia = ia_ref[...].astype(F32)
        lam_v = lam_ref[...]
        ls = _log_sigmoid(lam_v)
        la = LRU_C * ra * ls
        a = jnp.exp(la)
        mult = jnp.sqrt(-jnp.tanh(la) * (1.0 + a * a))
        hprev8 = jnp.where(first_tile, 0.0, hp_ref[...][HALO - SUBLANES:])
        h_prev = _shift_down(h_ref[...], 1, hprev8)
        dgr = (dyap_v * hg_ref[...].astype(F32)).astype(BF16)
        dz_ref[:, D:2 * D] = dgr
        dwin_ref[:, D:2 * D] += _dot_tn(h1v, dgr)

        b_scr[...] = _shift_up(a, 1, a_first[...])
        d_scr[...] = dyap_v * gg_ref[...].astype(F32)
        row = _row_iota(D)

        def grp(jj, carry):
            r0 = pl.multiple_of((ngrp - 1 - jj) * SUBLANES, SUBLANES)
            bv = b_scr[pl.ds(r0, SUBLANES), :]
            dv = d_scr[pl.ds(r0, SUBLANES), :]
            for d in (1, 2, 4):
                m = row < SUBLANES - d
                dv = jnp.where(m, dv + bv * pltpu.roll(dv, SUBLANES - d, 0), dv)
                bv = jnp.where(m, bv * pltpu.roll(bv, SUBLANES - d, 0), bv)
            gv = dv + bv * carry
            g_scr[pl.ds(r0, SUBLANES), :] = gv
            return gv[0:1, :]

        lax.fori_loop(0, ngrp, grp, g_first[0:1, :])
        g = g_scr[...]
        a_first[...] = a[:SUBLANES]
        g_first[...] = g[:SUBLANES]

        da = g * h_prev
        gx = g * xc
        dmult = gx * ia
        dia = gx * mult
        dxc = g * (mult * ia)
        dla = da * a - dmult * (a * a) / mult
        dra = dla * (LRU_C * ls)
        dlam_ref[...] += _colsum(dla * ra) * (LRU_C * _sigmoid(-lam_v))
        dpa = dra * ra * (1.0 - ra)
        dpx = dia * ia * (1.0 - ia)
        dba_ref[...] += _colsum(dpa)
        dbx_ref[...] += _colsum(dpx)
        dpab = dpa.astype(BF16)
        dpxb = dpx.astype(BF16)
        xcb = xc_ref[...]
        for hd in range(NH):
            sl = slice(hd * HD, (hd + 1) * HD)
            dwa_ref[hd] += _dot_tn(xcb[:, sl], dpab[:, sl])
            dwx_ref[hd] += _dot_tn(xcb[:, sl], dpxb[:, sl])
        dxc = dxc + _heads_nt(dpab, wa_ref) + _heads_nt(dpxb, wx_ref)

        nxt = dxc_first[...]
        taps = (_shift_up(dxc, 3, nxt), _shift_up(dxc, 2, nxt), _shift_up(dxc, 1, nxt), dxc)
        dxr = cw_ref[0:1, :] * taps[0]
        for k in range(1, 4):
            dxr = dxr + cw_ref[k:k + 1, :] * taps[k]
        dxrb = dxr.astype(BF16)
        dz_ref[:, 0:D] = dxrb
        dwin_ref[:, 0:D] += _dot_tn(h1v, dxrb)
        dxc_first[...] = dxc[:SUBLANES]
        dcb_ref[...] += _colsum(dxc)
        xr = xr_ref[...].astype(F32)
        for k in range(4):
            dcw_ref[k:k + 1, :] += _colsum(taps[k] * xr)

    def rev(col):
        return lambda i: (nt - 1 - i, col)

    vec = _const_spec((1, D))
    wspec = _const_spec((NH, HD, HD))
    vshape = jax.ShapeDtypeStruct((1, D), F32)
    wshape = jax.ShapeDtypeStruct((NH, HD, HD), F32)
    any_spec = pl.BlockSpec(memory_space=pl.ANY)
    tile = pl.BlockSpec((tm, D), rev(0))
    outs = pl.pallas_call(
        body, name="rnn_bwd", grid=(nt,),
        in_specs=[tile, tile, _const_spec((D, D), True), tile, tile, tile, tile, tile, tile, tile, tile,
                  pl.BlockSpec((HALO, D), lambda i: (jnp.maximum((nt - 1 - i) * hpt - 1, 0), 0)),
                  any_spec, any_spec, _const_spec((4, D)), wspec, wspec, vec],
        out_specs=[pl.BlockSpec((tm, 2 * D), rev(0)), _acc_spec((D, 2 * D), (0, 0)), _acc_spec((D, D), (0, 0)),
                   _const_spec((4, D)), vec, wspec, vec, wspec, vec, vec],
        out_shape=[jax.ShapeDtypeStruct((t, NCOL_IN), BF16), jax.ShapeDtypeStruct((D, NCOL_IN), F32),
                   jax.ShapeDtypeStruct((D, D), F32), jax.ShapeDtypeStruct((4, D), F32), vshape,
                   wshape, vshape, wshape, vshape, vshape],
        scratch_shapes=[pltpu.VMEM((SUBLANES, D), F32), pltpu.VMEM((SUBLANES, D), F32), pltpu.VMEM((SUBLANES, D), F32),
                        pltpu.VMEM((tm, D), F32), pltpu.VMEM((tm, D), F32), pltpu.VMEM((tm, D), F32)],
        input_output_aliases={12: 0, 13: 1},
        compiler_params=_cparams(1),
    )(dya, ya_pre, wba, h1, z, *saved, h, h, dz, dwin, cw, wa, wx, lam)
    return outs


def _sgu_bwd(dyb, yb_pre, wbb, h1, saved, dz, dwin, lng, lnb, wmt, mask):
    t = dyb.shape[0]
    tm = _tile_big(t)

    def body(dyb_ref, ybp_ref, wbb_ref, h1_ref, gu_ref, mg_ref, vh_ref, gpv_ref, rstd_ref, dz_any, dwin_any,
             lng_ref, lnb_ref, wmt_ref, mask_ref,
             dz_ref, dwin_ref, dwbb_ref, dws_ref, dbst_ref, dlng_ref, dlnb_ref):
        del dz_any, dwin_any

        @pl.when(pl.program_id(0) == 0)
        def _():
            for r in (dwin_ref, dwbb_ref, dws_ref, dbst_ref, dlng_ref, dlnb_ref):
                r[...] = jnp.zeros_like(r)

        lng_v = lng_ref[...]
        vhat = vh_ref[...].astype(F32)
        vb = (vhat * lng_v + lnb_ref[...]).astype(BF16)
        rstd = rstd_ref[...]
        dyb_v = dyb_ref[...]
        dwbb_ref[...] += _dot_tn(ybp_ref[...], dyb_v)
        dyb = _dot_nt(dyb_v, wbb_ref[...])
        h1v = h1_ref[...]
        dzu = (dyb * mg_ref[...].astype(F32)).astype(BF16)
        dz_ref[:, 0:D] = dzu
        dwin_ref[:, 0:D] += _dot_tn(h1v, dzu)
        dmix = dyb * gu_ref[...].astype(F32)
        dmb = dmix.astype(BF16)
        rows = []
        lane = lax.broadcasted_iota(jnp.int32, (HD, NH), 1)
        dbst = jnp.zeros((HD, NH), F32)
        for b0 in range(0, tm, HD):
            cols = []
            for g in range(NH):
                sl = slice(g * HD, (g + 1) * HD)
                dmg = dmb[b0:b0 + HD, sl]
                dws_ref[g] += _dot_nt(dmg, vb[b0:b0 + HD, sl]) * mask_ref[...]
                cols.append(jnp.dot(wmt_ref[g], dmg, preferred_element_type=F32))
                dbst = dbst + jnp.where(lane == g, jnp.sum(dmix[b0:b0 + HD, sl], axis=1, keepdims=True), 0.0)
            rows.append(jnp.concatenate(cols, axis=1))
        dbst_ref[...] += dbst
        dvln = jnp.concatenate(rows, axis=0) if len(rows) > 1 else rows[0]
        dlng_ref[...] += _colsum(dvln * vhat)
        dlnb_ref[...] += _colsum(dvln)
        dvh = dvln * lng_v
        dgv = rstd * (dvh - jnp.mean(dvh, axis=-1, keepdims=True)
                      - vhat * jnp.mean(dvh * vhat, axis=-1, keepdims=True))
        dzv = (dgv * gpv_ref[...].astype(F32)).astype(BF16)
        dz_ref[:, D:2 * D] = dzv
        dwin_ref[:, D:2 * D] += _dot_tn(h1v, dzv)

    vec = _const_spec((1, D))
    wspec = _const_spec((NH, HD, HD))
    vshape = jax.ShapeDtypeStruct((1, D), F32)
    tile = pl.BlockSpec((tm, D), lambda i: (i, 0))
    any_spec = pl.BlockSpec(memory_space=pl.ANY)
    return pl.pallas_call(
        body, name="sgu_bwd", grid=(t // tm,),
        in_specs=[tile, tile, _const_spec((D, D), True), tile, tile, tile, tile, tile,
                  pl.BlockSpec((tm, 1), lambda i: (i, 0)), any_spec, any_spec,
                  vec, vec, wspec, _const_spec((HD, HD))],
        out_specs=[pl.BlockSpec((tm, 2 * D), lambda i: (i, 1)), _acc_spec((D, 2 * D), (0, 1)), _acc_spec((D, D), (0, 0)),
                   wspec, _const_spec((HD, NH)), vec, vec],
        out_shape=[jax.ShapeDtypeStruct((t, NCOL_IN), BF16), jax.ShapeDtypeStruct((D, NCOL_IN), F32),
                   jax.ShapeDtypeStruct((D, D), F32), jax.ShapeDtypeStruct((NH, HD, HD), F32),
                   jax.ShapeDtypeStruct((HD, NH), F32), vshape, vshape],
        input_output_aliases={9: 0, 10: 1},
        compiler_params=_cparams(1),
    )(dyb, yb_pre, wbb, h1, *saved, dz, dwin, lng, lnb, wmt, mask)


def _in_bwd(dz, win, x, dx2, g, scale1):
    t = x.shape[0]
    tm = _tile_big(t)

    def body(dz_ref, w_ref, x_ref, dx2_ref, g_ref, sc_ref, dx_ref, dsh_ref, dsc_ref, dg_ref):
        @pl.when(pl.program_id(0) == 0)
        def _():
            for r in (dsh_ref, dsc_ref, dg_ref):
                r[...] = jnp.zeros_like(r)

        dh = jnp.zeros((tm, D), F32)
        for c0 in range(0, NCOL_IN, D):
            dh = dh + _dot_nt(dz_ref[:, c0:c0 + D], w_ref[:, c0:c0 + D])
        dxn, dsh, dsc, dg = _modnorm_bwd(dh, x_ref[...], g_ref[...], sc_ref[...])
        dx_ref[...] = dx2_ref[...] + dxn
        dsh_ref[...] += dsh
        dsc_ref[...] += dsc
        dg_ref[...] += dg

    tile = pl.BlockSpec((tm, D), lambda i: (i, 0))
    vec = _const_spec((1, D))
    vshape = jax.ShapeDtypeStruct((1, D), F32)
    scale1, sc_spec = _vec_operand(scale1)
    return pl.pallas_call(
        body, name="in_bwd", grid=(t // tm,),
        in_specs=[pl.BlockSpec((tm, NCOL_IN), lambda i: (i, 0)), _const_spec((D, NCOL_IN), True), tile, tile, vec,
                  sc_spec],
        out_specs=[tile, vec, vec, vec],
        out_shape=[jax.ShapeDtypeStruct((t, D), F32), vshape, vshape, vshape],
        compiler_params=_cparams(1),
    )(dz, win, x, dx2, g, scale1)


def _mod_cols(c_all, w_ada, b_cols):
    nb, cols = c_all.shape[0], w_ada.shape[1]

    def body(c_ref, w_ref, b_ref, o_ref):
        cv = c_ref[...]
        ca = (cv * _sigmoid(cv)).astype(BF16)
        o_ref[...] = jnp.dot(ca, w_ref[...].astype(BF16), preferred_element_type=F32) + b_ref[...]

    return pl.pallas_call(body, name="mod_cols", out_shape=jax.ShapeDtypeStruct((nb, cols), F32))(c_all, w_ada, b_cols)


def _ada_grad(c_all, dmod_cols):
    cols = dmod_cols.shape[1]

    def body(c_ref, d_ref, o_ref):
        cv = c_ref[...]
        ca = (cv * _sigmoid(cv)).astype(BF16)
        o_ref[...] = _dot_tn(ca, d_ref[...].astype(BF16))

    return pl.pallas_call(body, name="ada_grad", out_shape=jax.ShapeDtypeStruct((D, cols), F32))(c_all, dmod_cols)


def _adamw_update(w, m, v, g):
    bc1 = 1.0 - ADAM_B1 ** ADAM_STEP
    bc2 = 1.0 - ADAM_B2 ** ADAM_STEP
    mn = ADAM_B1 * m + (1.0 - ADAM_B1) * g
    vn = ADAM_B2 * v + (1.0 - ADAM_B2) * (g * g)
    return -ADAM_LR * ((mn / bc1) / (jnp.sqrt(vn / bc2) + ADAM_EPS) + ADAM_WD * w), mn, vn


def _adamw_group(names, ws, ms, vs, packs, name):
    n = len(names)
    starts, r0 = [], 0
    for w in ws:
        starts.append(r0)
        r0 += _pack_rows(w.shape)

    def body(*refs):
        w_refs, m_refs, v_refs, p_ref = refs[:n], refs[n:2 * n], refs[2 * n:3 * n], refs[3 * n]
        outs = refs[3 * n + 1:]
        for k in range(n):
            rows = _pack_rows(ws[k].shape)
            g = None
            for dev in range(N_DEV):
                if ws[k].shape[0] == 1:
                    term = jnp.concatenate(
                        [p_ref[dev, starts[k] + r:starts[k] + r + 1, :] for r in range(rows)], axis=1)
                else:
                    term = p_ref[dev, starts[k]:starts[k] + rows, :]
                g = term if g is None else g + term
            delta, mn, vn = _adamw_update(w_refs[k][...], m_refs[k][...], v_refs[k][...], g)
            for o_ref, val in zip(outs[4 * k:4 * k + 4], (g, delta, mn, vn)):
                o_ref[...] = val

    shapes = [jax.ShapeDtypeStruct(w.shape, F32) for w in ws for _ in range(4)]
    outs = pl.pallas_call(body, name=name, out_shape=shapes,
                          compiler_params=pltpu.CompilerParams(vmem_limit_bytes=VMEM_LIMIT))(*ws, *ms, *vs, packs)
    return {nm: tuple(outs[4 * k:4 * k + 4]) for k, nm in enumerate(names)}


def _adamw(w, m, v, parts, name):
    rows, cols = w.shape
    tr = _row_tile(rows, cols)
    stacked = [p.ndim == 3 for p in parts]

    def body(*refs):
        w_ref, m_ref, v_ref = refs[:3]
        p_refs = refs[3:3 + len(parts)]
        g_ref, d_ref, mo_ref, vo_ref = refs[3 + len(parts):]
        g = None
        for p_ref, st in zip(p_refs, stacked):
            terms = [p_ref[k].astype(F32) for k in range(p_ref.shape[0])] if st else [p_ref[...].astype(F32)]
            for term in terms:
                g = term if g is None else g + term
        delta, mn, vn = _adamw_update(w_ref[...], m_ref[...], v_ref[...], g)
        g_ref[...] = g
        mo_ref[...] = mn
        vo_ref[...] = vn
        d_ref[...] = delta

    tile = pl.BlockSpec((tr, cols), lambda i: (i, 0))
    p_specs = [pl.BlockSpec((p.shape[0], tr, cols), lambda i: (0, i, 0)) if st else tile for p, st in zip(parts, stacked)]
    shp = jax.ShapeDtypeStruct((rows, cols), F32)
    return pl.pallas_call(
        body, name=name, grid=(rows // tr,),
        in_specs=[tile, tile, tile] + p_specs, out_specs=[tile] * 4, out_shape=[shp] * 4,
        compiler_params=_cparams(1),
    )(w, m, v, *parts)


def _adamw_alike(ws, ms, vs, parts_of, name):
    k, npar = len(ws), len(parts_of[0])
    rows, cols = ws[0].shape
    tr = _row_tile(rows, cols)
    stacked = [p.ndim == 3 for p in parts_of[0]]
    per = 3 + npar

    def body(*refs):
        ins, outs = refs[:k * per], refs[k * per:]
        for a in range(k):
            w_ref, m_ref, v_ref = ins[a * per:a * per + 3]
            g = None
            for p_ref, st in zip(ins[a * per + 3:(a + 1) * per], stacked):
                terms = [p_ref[j].astype(F32) for j in range(p_ref.shape[0])] if st else [p_ref[...].astype(F32)]
                for term in terms:
                    g = term if g is None else g + term
            delta, mn, vn = _adamw_update(w_ref[...], m_ref[...], v_ref[...], g)
            for o_ref, val in zip(outs[4 * a:4 * a + 4], (g, delta, mn, vn)):
                o_ref[...] = val

    tile = pl.BlockSpec((tr, cols), lambda i: (i, 0))
    p_specs = [pl.BlockSpec((p.shape[0], tr, cols), lambda i: (0, i, 0)) if st else tile
               for p, st in zip(parts_of[0], stacked)]
    shp = jax.ShapeDtypeStruct((rows, cols), F32)
    operands = [o for a in range(k) for o in (ws[a], ms[a], vs[a], *parts_of[a])]
    outs = pl.pallas_call(
        body, name=name, grid=(rows // tr,),
        in_specs=([tile, tile, tile] + p_specs) * k, out_specs=[tile] * (4 * k), out_shape=[shp] * (4 * k),
        compiler_params=_cparams(1),
    )(*operands)
    return [tuple(outs[4 * a:4 * a + 4]) for a in range(k)]


def _mesh_pos():
    return lax.axis_index("x"), lax.axis_index("y"), lax.axis_index("c")


def _other_chips(x, y):
    return [(1 - x, y), (x, 1 - y), (1 - x, 1 - y)]


def _block_of(ref, axis, index, size):
    if axis == 0:
        return ref.at[index]
    return ref.at[:, pl.ds(pl.multiple_of(index * size, 128), size)]


def _all_gather(shards, axes, name):
    n = len(shards)
    per = 7

    def body(*refs):
        ins, outs, done = refs[:n], refs[n:2 * n], refs[2 * n]
        send_sems, recv_sems, local_sems = refs[2 * n + 1:]
        x, y, c = _mesh_pos()
        me, sibling = (x, y, c), (x, y, 1 - c)
        chips = _other_chips(x, y)

        def rows(a, pos):
            return _block_of(outs[a], axes[a], 4 * pos[0] + 2 * pos[1] + pos[2], shards[a].shape[-1])

        def copy(a, k, block, to, src=None):
            return pltpu.make_async_remote_copy(
                src_ref=rows(a, block) if src is None else src, dst_ref=rows(a, block),
                send_sem=send_sems.at[a * per + k], recv_sem=recv_sems.at[a * per + k],
                device_id=to, device_id_type=MESH_IDS)

        mine = [pltpu.make_async_copy(ins[a], rows(a, me), local_sems.at[a]) for a in range(n)]
        for cp in mine:
            cp.start()
        first = []
        for a in range(n):
            first.append(copy(a, 0, me, sibling, src=ins[a]))
            first += [copy(a, 1 + j, me, (*chip, c), src=ins[a]) for j, chip in enumerate(chips)]
        for cp in first:
            cp.start()
        passed = []
        for j, chip in enumerate(chips):
            for a in range(n):
                copy(a, 1 + j, (*chip, c), me).wait_recv()
                fwd = copy(a, 4 + j, (*chip, c), sibling)
                fwd.start()
                passed.append(fwd)
        for a in range(n):
            copy(a, 0, sibling, me).wait_recv()
            for j, chip in enumerate(chips):
                copy(a, 4 + j, (*chip, 1 - c), me).wait_recv()
        for cp in first + passed:
            cp.wait_send()
        for cp in mine:
            cp.wait()
        done[...] = jnp.zeros_like(done)

    def full_shape(s, ax):
        return (N_DEV,) + s.shape if ax == 0 else s.shape[:-1] + (N_DEV * s.shape[-1],)

    any_spec = pl.BlockSpec(memory_space=pl.ANY)
    outs = pl.pallas_call(
        body, name=name,
        in_specs=[any_spec] * n, out_specs=[any_spec] * n + [pl.BlockSpec(memory_space=pltpu.VMEM)],
        out_shape=[jax.ShapeDtypeStruct(full_shape(s, ax), s.dtype) for s, ax in zip(shards, axes)]
        + [jax.ShapeDtypeStruct((SUBLANES, LANES), F32)],
        scratch_shapes=[pltpu.SemaphoreType.DMA((n * per,)), pltpu.SemaphoreType.DMA((n * per,)),
                        pltpu.SemaphoreType.DMA((n,))],
    )(*shards)
    return outs[:n], outs[n]


def _chip_blocks(x, y):
    return [(x, y)] + _other_chips(x, y)


def _sibling_reduce(gs, axis, name):
    g0, n = gs[0], len(gs)
    rows, cols = (g0.shape[1], g0.shape[2]) if axis == 0 else (g0.shape[0], g0.shape[1] // N_DEV)
    chunk = math.gcd(rows, 64)

    def body(*refs):
        g_refs, own_refs, pay_refs = refs[:n], refs[n:2 * n], refs[2 * n:3 * n]
        (stage_buf, send_buf, keep_buf, recv_buf, pay_buf,
         send_sems, recv_sems, stage_sems, keep_sems, out_sems) = refs[3 * n:]
        x, y, c = _mesh_pos()
        sibling = (x, y, 1 - c)
        chips = _chip_blocks(x, y)
        stage, keep, push = [], [], []
        for a in range(n):
            for j, (px, py) in enumerate(chips):
                s = 4 * a + j
                theirs = _block_of(g_refs[a], axis, 4 * px + 2 * py + (1 - c), cols)
                ours = _block_of(g_refs[a], axis, 4 * px + 2 * py + c, cols)
                stage.append(pltpu.make_async_copy(theirs, stage_buf.at[s], stage_sems.at[s]))
                keep.append(pltpu.make_async_copy(ours, keep_buf.at[s], keep_sems.at[s]))
                push.append(pltpu.make_async_remote_copy(
                    src_ref=send_buf.at[s], dst_ref=recv_buf.at[s], send_sem=send_sems.at[s],
                    recv_sem=recv_sems.at[s], device_id=sibling, device_id_type=MESH_IDS))
        for cp in stage[:2]:
            cp.start()
        for s in range(4 * n):
            if s + 2 < 4 * n:
                stage[s + 2].start()
            stage[s].wait()
            keep[s].start()

            def narrow(r, carry, s=s):
                sl = pl.ds(pl.multiple_of(r * chunk, chunk), chunk)
                send_buf[s, sl, :] = stage_buf[s, sl, :].astype(BF16)
                return carry

            lax.fori_loop(0, rows // chunk, narrow, 0)
            push[s].start()
        written = []
        for s in range(4 * n):
            push[s].wait_recv()
            keep[s].wait()
            a, j = divmod(s, 4)
            res = keep_buf.at[s] if j == 0 else pay_buf.at[3 * a + j - 1]

            def add(r, carry, s=s, res=res):
                sl = pl.ds(pl.multiple_of(r * chunk, chunk), chunk)
                res[sl, :] = (keep_buf[s, sl, :] + recv_buf[s, sl, :].astype(F32)).astype(res.dtype)
                return carry

            lax.fori_loop(0, rows // chunk, add, 0)
            out = pltpu.make_async_copy(res, own_refs[a] if j == 0 else pay_refs[a].at[j - 1], out_sems.at[s])
            out.start()
            written.append(out)
        for cp in push:
            cp.wait_send()
        for cp in written:
            cp.wait()

    any_spec = pl.BlockSpec(memory_space=pl.ANY)
    buf = pltpu.VMEM((4 * n, rows, cols), F32)
    buf16 = pltpu.VMEM((4 * n, rows, cols), BF16)
    sems = pltpu.SemaphoreType.DMA((4 * n,))
    outs = pl.pallas_call(
        body, name=name,
        in_specs=[any_spec] * n, out_specs=[any_spec] * (2 * n),
        out_shape=[jax.ShapeDtypeStruct((rows, cols), F32)] * n + [jax.ShapeDtypeStruct((3, rows, cols), BF16)] * n,
        scratch_shapes=[buf, buf16, buf, buf16, pltpu.VMEM((3 * n, rows, cols), BF16),
                        sems, sems, sems, sems, sems],
        compiler_params=pltpu.CompilerParams(vmem_limit_bytes=VMEM_LIMIT),
    )(*gs)
    return list(zip(outs[:n], outs[n:]))


_HBM_SPEC = pl.BlockSpec(memory_space=pltpu.HBM)
_SEM_SPEC = pl.BlockSpec(memory_space=pltpu.SEMAPHORE)
_SIDE_EFFECT = pltpu.SideEffectType.DATAFLOW_SIDE_EFFECTING


def _exchange_start(name, srcs, lands, plan, n_copies):
    nb = len(srcs) + len(lands)

    def body(*refs):
        bufs, send_sems, recv_sems, token = refs[:nb], refs[nb], refs[nb + 1], refs[-1]
        for cp in plan(bufs[:len(srcs)], bufs[len(srcs):], send_sems, recv_sems):
            cp.start()
        token[...] = jnp.zeros_like(token)

    arrays = list(srcs) + list(lands)
    outs = pl.pallas_call(
        body, name=name,
        out_shape=(pltpu.SemaphoreType.DMA((n_copies,)), pltpu.SemaphoreType.DMA((n_copies,)),
                   *[pltpu.HBM(a.shape, a.dtype) for a in arrays], jax.ShapeDtypeStruct((SUBLANES, LANES), F32)),
        in_specs=[_HBM_SPEC] * nb,
        out_specs=(_SEM_SPEC, _SEM_SPEC, *[_HBM_SPEC] * nb, pl.BlockSpec(memory_space=pltpu.VMEM)),
        input_output_aliases={k: 2 + k for k in range(nb)},
        compiler_params=pltpu.CompilerParams(has_side_effects=_SIDE_EFFECT),
    )(*[pltpu.with_memory_space_constraint(a, pltpu.HBM) for a in arrays])
    return outs[0], outs[1], outs[2:2 + len(srcs)], outs[2 + len(srcs):2 + nb], outs[-1]


def _exchange_wait(name, send_sems, recv_sems, srcs, lands, plan, after):
    nb = len(srcs) + len(lands)
    after = list(after)

    def body(*refs):
        bufs, send_ref, recv_ref = refs[:nb], refs[nb], refs[nb + 1]
        for cp in plan(bufs[:len(srcs)], bufs[len(srcs):], send_ref, recv_ref):
            cp.wait_send()
            cp.wait_recv()

    arrays = list(srcs) + list(lands)
    outs = pl.pallas_call(
        body, name=name,
        out_shape=tuple(pltpu.HBM(a.shape, a.dtype) for a in arrays),
        in_specs=[_HBM_SPEC] * nb + [_SEM_SPEC, _SEM_SPEC] + [pl.BlockSpec(memory_space=pl.ANY)] * len(after),
        out_specs=tuple([_HBM_SPEC] * nb),
        input_output_aliases={k: k for k in range(nb)},
        compiler_params=pltpu.CompilerParams(has_side_effects=_SIDE_EFFECT),
    )(*arrays, send_sems, recv_sems, *after)
    return outs[len(srcs):]


def _gather_plan(axes, sizes):
    def plan(src_refs, land_refs, send_sems, recv_sems):
        x, y, c = _mesh_pos()
        copies = []
        for a, (src, land) in enumerate(zip(src_refs, land_refs)):
            mine = _block_of(land, axes[a], 4 * x + 2 * y + c, sizes[a])
            for k in range(1, N_DEV):
                peer = (1 - x if k & 4 else x, 1 - y if k & 2 else y, 1 - c if k & 1 else c)
                idx = a * (N_DEV - 1) + k - 1
                copies.append(pltpu.make_async_remote_copy(
                    src_ref=src, dst_ref=mine, send_sem=send_sems.at[idx], recv_sem=recv_sems.at[idx],
                    device_id=peer, device_id_type=MESH_IDS))
        return copies
    return plan


def _chip_plan(src_refs, land_refs, send_sems, recv_sems):
    x, y, c = _mesh_pos()
    copies = []
    for a, (src, land) in enumerate(zip(src_refs, land_refs)):
        for j, chip in enumerate(_other_chips(x, y)):
            copies.append(pltpu.make_async_remote_copy(
                src_ref=src.at[j], dst_ref=land.at[j], send_sem=send_sems.at[3 * a + j],
                recv_sem=recv_sems.at[3 * a + j], device_id=(*chip, c), device_id_type=MESH_IDS))
    return copies


def _own_block_placed(shard, axis, me):
    if axis == 0:
        full = lax.empty((N_DEV,) + shard.shape, shard.dtype)
        return lax.dynamic_update_slice(full, shard[None], (me,) + (0,) * shard.ndim)
    rows, cols = shard.shape

    def body(me_ref, s_ref, o_ref):
        del me_ref
        o_ref[...] = s_ref[...]

    return pl.pallas_call(
        body, name="place_own_columns",
        grid_spec=pltpu.PrefetchScalarGridSpec(
            num_scalar_prefetch=1, grid=(1,),
            in_specs=[pl.BlockSpec((rows, cols), lambda i, me_ref: (0, 0))],
            out_specs=pl.BlockSpec((rows, cols), lambda i, me_ref: (0, me_ref[0]))),
        out_shape=jax.ShapeDtypeStruct((rows, N_DEV * cols), shard.dtype),
    )(jnp.reshape(me, (1,)).astype(jnp.int32), shard)


def _local_step(x, target, mod, win, late_weights, p, grads_ready=None):
    shift1, scale1, gate1, shift2, scale2, gate2 = ((mod, k) for k in range(6))

    def after_token(v, token):
        return v if token is None else v + token[0:1, 0:1]
    wa, wx = p["lru_w_a"].astype(BF16), p["lru_w_x"].astype(BF16)
    mask = jnp.tril(jnp.ones((HD, HD), F32))
    wm = (p["sgu_w_s"] * mask).astype(BF16)
    wmt = jnp.swapaxes(wm, 1, 2)
    bst = jnp.transpose(p["sgu_b_s"])

    h1, z = _norm_proj(x, p["norm_mix_g"], scale1, shift1, win, "mix_proj")
    hstate, ya_pre, *rnn_saved = _rnn_fwd(
        z, p["rnn_conv_w"], p["rnn_conv_b"], wa, p["lru_b_a"], wx, p["lru_b_x"], p["lru_lambda"])
    yb_pre, *sgu_saved = _sgu_fwd(z, p["sgu_ln_g"], p["sgu_ln_b"], wm, bst)
    wba, wbb, wout = late_weights("merge", [ya_pre, yb_pre])
    x2, ya, yb, merged, o1 = _merge_fwd(ya_pre, yb_pre, z, x, gate1, wba, wbb, wout)
    wup = late_weights("ffn_up", [x2])
    h2, up_a, up_v, ff, fa, fv = _ffn_proj_mid(
        x2, p["norm_ffn_g"], scale2, shift2, wup, p["ffn_conv_w"], p["ffn_conv_b"])
    wd = late_weights("ffn_down", [ff])
    dx3, do2, loss, d_gfin, d_gate2 = _ffn_out_loss(ff, wd, x2, target, gate2, p["norm_final_g"])

    dact, dval, d_wd, dcb_a, dcb_v = _ffn_down_bwd(do2, ff, fa, fv, wd)
    dup, dx2, do1, d_cwf, d_shift2, d_scale2, d_gffn, d_gate1 = _ffn_up_bwd(
        dact, dval, up_a, up_v, p["ffn_conv_w"], wup, x2, dx3, p["norm_ffn_g"], scale2, o1, gate1)
    d_wup = _xt_y(h2, dup, "w_up_grad")
    ready = grads_ready if grads_ready else (lambda stage, big, small: None)
    token = ready("ffn", {"w_up": d_wup, "w_down": d_wd}, {})

    dya, dyb, dz, d_wout, d_win = _out_bwd(do1, wout, merged, ya, yb, z, h1)
    dz, d_win, d_wba, d_cw, d_cb, d_wa, d_ba, d_wx, d_bx, d_lam = _rnn_bwd(
        dya, ya_pre, wba, h1, z, rnn_saved, hstate, dz, d_win, p["rnn_conv_w"], wa, wx,
        after_token(p["lru_lambda"], token))
    small = {
        "rnn_conv_w": d_cw, "rnn_conv_b": d_cb, "lru_w_a": d_wa, "lru_b_a": d_ba, "lru_w_x": d_wx, "lru_b_x": d_bx,
        "lru_lambda": d_lam, "norm_ffn_g": d_gffn, "ffn_conv_w": d_cwf,
        "ffn_conv_b": jnp.concatenate([dcb_a, dcb_v], axis=1), "norm_final_g": d_gfin,
    }
    token = ready("rnn", {}, small)
    dz, d_win, d_wbb, d_ws, d_bst, d_lng, d_lnb = _sgu_bwd(
        dyb, yb_pre, wbb, h1, sgu_saved, dz, d_win, p["sgu_ln_g"], after_token(p["sgu_ln_b"], token), wmt, mask)
    sgu_small = {"sgu_ln_g": d_lng, "sgu_ln_b": d_lnb, "sgu_w_s": d_ws, "sgu_b_s": jnp.transpose(d_bst)}
    mixer = {"w_in": d_win, "w_out": d_wout, "w_branch_a": d_wba, "w_branch_b": d_wbb}
    token = ready("mixer", mixer, sgu_small)
    grad_x, d_shift1, d_scale1, d_gmix = _in_bwd(dz, win, x, dx2, after_token(p["norm_mix_g"], token), scale1)

    small.update(sgu_small)
    small["norm_mix_g"] = d_gmix
    dmod = jnp.stack([d_shift1, d_scale1, d_gate1, d_shift2, d_scale2, d_gate2])
    big = {"w_in": d_win, "w_up": d_wup, "w_branch_a": d_wba, "w_branch_b": d_wbb, "w_out": d_wout, "w_down": d_wd}
    return loss, grad_x, big, small, dmod


LAST_REP = ["b_ada", "norm_mix_g"]
EARLY_REP = ["rnn_conv_b", "lru_w_a", "lru_b_a", "lru_w_x", "lru_b_x", "lru_lambda", "norm_ffn_g", "ffn_conv_b",
             "norm_final_g"]
MID_REP = ["sgu_ln_g", "sgu_ln_b", "sgu_w_s", "sgu_b_s"]
COL_SHARDED = ["rnn_conv_w", "ffn_conv_w"]
SMALL_GROUPS = {"rnn": EARLY_REP + COL_SHARDED, "mixer": MID_REP, "last": LAST_REP}
REPLICATED = LAST_REP + EARLY_REP + MID_REP
SMALL_NAMES = REPLICATED + COL_SHARDED
BIG_NAMES = ["w_in", "w_up", "w_branch_a", "w_branch_b", "w_out", "w_down"]
BIG_AXES = [1, 1, 0, 0, 0, 0]
WEIGHTS = ["w_ada", "b_ada", "norm_mix_g", "w_in", "rnn_conv_w", "rnn_conv_b", "lru_w_a", "lru_b_a", "lru_w_x",
           "lru_b_x", "lru_lambda", "sgu_ln_g", "sgu_ln_b", "sgu_w_s", "sgu_b_s", "w_branch_a", "w_branch_b",
           "w_out", "norm_ffn_g", "w_up", "ffn_conv_w", "ffn_conv_b", "w_down", "norm_final_g"]


def _pack_rows(shape):
    return math.prod(shape) // LANES


def _pack(arrays):
    return jnp.concatenate([a.reshape(-1, LANES) for a in arrays], axis=0)


def kernel(x, c, w_ada, b_ada, norm_mix_g, w_in, rnn_conv_w, rnn_conv_b, lru_w_a, lru_b_a, lru_w_x, lru_b_x, lru_lambda, sgu_ln_g, sgu_ln_b, sgu_w_s, sgu_b_s, w_branch_a, w_branch_b, w_out, norm_ffn_g, w_up, ffn_conv_w, ffn_conv_b, w_down, norm_final_g, loss_target, m_w_ada, m_b_ada, m_norm_mix_g, m_w_in, m_rnn_conv_w, m_rnn_conv_b, m_lru_w_a, m_lru_b_a, m_lru_w_x, m_lru_b_x, m_lru_lambda, m_sgu_ln_g, m_sgu_ln_b, m_sgu_w_s, m_sgu_b_s, m_w_branch_a, m_w_branch_b, m_w_out, m_norm_ffn_g, m_w_up, m_ffn_conv_w, m_ffn_conv_b, m_w_down, m_norm_final_g, v_w_ada, v_b_ada, v_norm_mix_g, v_w_in, v_rnn_conv_w, v_rnn_conv_b, v_lru_w_a, v_lru_b_a, v_lru_w_x, v_lru_b_x, v_lru_lambda, v_sgu_ln_g, v_sgu_ln_b, v_sgu_w_s, v_sgu_b_s, v_w_branch_a, v_w_branch_b, v_w_out, v_norm_ffn_g, v_w_up, v_ffn_conv_w, v_ffn_conv_b, v_w_down, v_norm_final_g):
    given = dict(locals())
    me = 4 * lax.axis_index("x") + 2 * lax.axis_index("y") + lax.axis_index("c")
    ada_cols = w_ada.shape[2]
    conv_cols = {"rnn_conv_w": rnn_conv_w.shape[2], "ffn_conv_w": ffn_conv_w.shape[2]}

    (win, c_all, cw_rnn, cw_ffn), _ = _all_gather(
        [w_in[0].astype(BF16), c.reshape(1, 1, D), rnn_conv_w[0], ffn_conv_w[0]], [1, 0, 1, 1], "gather_first")
    c_all = c_all.reshape(N_DEV, D)

    b_cols = lax.dynamic_slice_in_dim(b_ada, me * ada_cols, ada_cols, axis=1)
    (mod_all,), mod_done = _all_gather(
        [_mod_cols(c_all, w_ada[0], b_cols).reshape(1, N_DEV, ada_cols)], [0], "gather_mod")
    mod_all = mod_all.reshape(N_DEV, N_DEV, ada_cols)
    mod_mine = lax.dynamic_index_in_dim(mod_all, me, axis=1, keepdims=False).reshape(6, 1, D)

    late_groups = {"merge": (["w_branch_a", "w_branch_b", "w_out"], [0, 0, 0]), "ffn_up": (["w_up"], [1]),
                   "ffn_down": (["w_down"], [0])}
    in_flight, started = {}, mod_done[0:1, 0:1]
    for stage, (names, axes) in late_groups.items():
        shards = [(given[n][0] + started).astype(BF16) for n in names]
        plan = _gather_plan(axes, [s.shape[-1] for s in shards])
        send, recv, srcs, lands, token = _exchange_start(
            "gather_start_" + stage, shards, [_own_block_placed(s, ax, me) for s, ax in zip(shards, axes)], plan,
            len(shards) * (N_DEV - 1))
        in_flight[stage] = (send, recv, srcs, lands, plan)
        started = started + token[0:1, 0:1]

    def late_weights(stage, after):
        send, recv, srcs, lands, plan = in_flight[stage]
        full = _exchange_wait("gather_wait_" + stage, send, recv, srcs, lands, plan, after)
        full = [w.reshape(-1, D) if ax == 0 else w for w, ax in zip(full, late_groups[stage][1])]
        return full if len(full) > 1 else full[0]

    mod_mine = mod_mine + started

    reducing, packing = {}, {}

    def start_pack(stage, small):
        pack = _pack([small[n] for n in SMALL_GROUPS[stage]])[None]
        plan = _gather_plan([0], [LANES])
        send, recv, srcs, lands, tok = _exchange_start(
            "small_start_" + stage, [pack], [_own_block_placed(pack, 0, me)], plan, N_DEV - 1)
        packing[stage] = (send, recv, srcs, lands, plan)
        return tok

    def grads_ready(stage, grads, small):
        tokens = [start_pack(stage, small)] if small else []
        if grads:
            tokens.append(start_reduce(stage, grads))
        return sum(tokens[1:], tokens[0])

    def start_reduce(stage, grads):
        names = [n for n in BIG_NAMES if n in grads]
        blocked = {}
        for n in names:
            ax = BIG_AXES[BIG_NAMES.index(n)]
            g = grads[n] if ax == 1 else grads[n].reshape(N_DEV, grads[n].shape[0] // N_DEV, grads[n].shape[1])
            blocked.setdefault((ax, g.shape), []).append((n, g))
        sums = {}
        for (ax, _), group in blocked.items():
            reduced = _sibling_reduce([g for _, g in group], ax, "reduce_sibling_" + "_".join(n for n, _ in group))
            sums.update({n: r for (n, _), r in zip(group, reduced)})
        sums = [sums[n] for n in names]
        pays = [pay for _, pay in sums]
        send, recv, srcs, lands, tok = _exchange_start(
            "reduce_start_" + stage, pays, [lax.empty(p_.shape, p_.dtype) for p_ in pays], _chip_plan, 3 * len(pays))
        reducing[stage] = (names, [own for own, _ in sums], send, recv, srcs, lands)
        return tok

    p = {n: given[n][0] for n in REPLICATED if n not in ("b_ada", "norm_final_g")}
    p = {n: (a.reshape(1, -1) if a.ndim == 1 else a) for n, a in p.items()}
    p["rnn_conv_w"], p["ffn_conv_w"] = cw_rnn, cw_ffn
    p["norm_final_g"] = norm_final_g.reshape(1, D)
    loss, grad_x, _, small, dmod = _local_step(x[0], loss_target[0], mod_mine, win, late_weights, p, grads_ready)

    small["b_ada"] = dmod.reshape(1, 6 * D)
    rows_of = {n: _pack_rows(small[n].shape) for n in SMALL_NAMES}
    (last,), _ = _all_gather([_pack([small[n] for n in LAST_REP])[None]], [0], "gather_small")
    gathered = {"last": last}
    for stage, (send, recv, srcs, lands, plan) in packing.items():
        (gathered[stage],) = _exchange_wait("small_wait_" + stage, send, recv, srcs, lands, plan, [grad_x])
    gathered = {k: v.reshape(N_DEV, -1, LANES) for k, v in gathered.items()}

    out = {}
    for stage, (names, owns, send, recv, srcs, lands) in reducing.items():
        landed = _exchange_wait("reduce_wait_" + stage, send, recv, srcs, lands, _chip_plan, [last])
        alike = {}
        for n, own, got in zip(names, owns, landed):
            alike.setdefault(own.shape, []).append((n, [own, got]))
        for group in alike.values():
            ns = [n for n, _ in group]
            done = _adamw_alike(*[[given[pre + n][0] for n in ns] for pre in ("", "m_", "v_")],
                                [parts for _, parts in group], "adamw_" + "_".join(ns))
            out.update(zip(ns, done))

    dmod_all = gathered["last"][:, :rows_of["b_ada"]].reshape(N_DEV, 6 * D)
    dmod_cols = lax.dynamic_slice_in_dim(dmod_all, me * ada_cols, ada_cols, axis=1)
    out["w_ada"] = _adamw(w_ada[0], m_w_ada[0], v_w_ada[0], [_ada_grad(c_all, dmod_cols)], "adamw_w_ada")

    def rows_form(a):
        return a.reshape(1, -1) if a.size // a.shape[-1] == 1 or a.ndim == 1 else a.reshape(-1, LANES)

    for stage, names in (("last", LAST_REP), ("rnn", EARLY_REP), ("mixer", MID_REP)):
        out.update(_adamw_group(names, *[[rows_form(given[pre + n]) for n in names] for pre in ("", "m_", "v_")],
                                gathered[stage], "adamw_small_" + stage))

    row0 = sum(rows_of[n] for n in EARLY_REP)
    for n in COL_SHARDED:
        full = gathered["rnn"][:, row0:row0 + rows_of[n]].reshape(N_DEV, small[n].shape[0], small[n].shape[1])
        mine = lax.dynamic_slice_in_dim(full, me * conv_cols[n], conv_cols[n], axis=2)
        out[n] = _adamw(given[n][0], given["m_" + n][0], given["v_" + n][0], [mine], "adamw_" + n)
        row0 += rows_of[n]

    total = lax.psum(loss[0, 0], ("x", "y", "c"))
    results = [total, grad_x[None]]
    for kind in range(4):
        results += [out[n][kind].reshape(given[n].shape) for n in WEIGHTS]
    return tuple(results)
```

```python
import math

import jax
import jax.numpy as jnp
from jax import lax
from jax.experimental import pallas as pl
from jax.experimental.pallas import tpu as pltpu

F32 = jnp.float32
BF16 = jnp.bfloat16
MESH_IDS = pl.DeviceIdType.MESH

D = 1024
NH = 8
HD = 128
NCOL_IN = 6 * D
DFF = 3 * D
N_DEV = 8
EPS = 1e-6
LRU_C = 8.0
ADAM_LR, ADAM_B1, ADAM_B2, ADAM_EPS, ADAM_WD, ADAM_STEP = 0.001, 0.9, 0.999, 1e-08, 0.01, 10

SUBLANES = 8
LANES = 128
HALO = 16
VMEM_LIMIT = 56 * 1024 * 1024
GELU_K = math.sqrt(2.0 / math.pi)
GELU_C = 0.044715


def _cparams(n_axes):
    return pltpu.CompilerParams(dimension_semantics=("arbitrary",) * n_axes, vmem_limit_bytes=VMEM_LIMIT)


def _const_spec(shape, single_buffer=False):
    nd = len(shape)
    if single_buffer:
        return pl.BlockSpec(shape, lambda *_: (0,) * nd, pipeline_mode=pl.Buffered(1))
    return pl.BlockSpec(shape, lambda *_: (0,) * nd)


def _vec_operand(v):
    if isinstance(v, tuple):
        stack, k = v
        return stack, pl.BlockSpec((None, 1, D), lambda *_: (k, 0, 0))
    return v, _const_spec((1, D))


def _tile_big(t):
    return min(512, t)


def _tile_seq(t):
    return min(256, t)


def _row_tile(rows, cols):
    cap = max(SUBLANES, (2 * 1024 * 1024) // (4 * cols) // SUBLANES * SUBLANES)
    if rows <= cap:
        return rows
    return next(tr for tr in range(cap, 0, -SUBLANES) if rows % tr == 0)


def _gelu_t(x):
    x2 = x * x
    t = jnp.tanh(x * (GELU_K + (GELU_K * GELU_C) * x2))
    hx = 0.5 * x
    return hx + hx * t, (x2, hx, t)


def _gelu_grad(shared):
    x2, hx, t = shared
    return (0.5 + 0.5 * t) + (hx * (1.0 - t * t)) * (GELU_K + (3.0 * GELU_K * GELU_C) * x2)


def _sigmoid(x):
    return 1.0 / (1.0 + jnp.exp(-x))


def _log_sigmoid(x):
    return -(jnp.maximum(-x, 0.0) + jnp.log1p(jnp.exp(-jnp.abs(x))))


def _row_iota(cols):
    return lax.broadcasted_iota(jnp.int32, (SUBLANES, cols), 0)


def _shift_down(x, k, prev8):
    if k == 0:
        return x
    r = pltpu.roll(x, k, 0)
    p = pltpu.roll(prev8, k, 0)
    head = jnp.where(_row_iota(x.shape[1]) < k, p, r[:SUBLANES])
    return jnp.concatenate([head, r[SUBLANES:]], axis=0)


def _shift_up(x, k, next8):
    if k == 0:
        return x
    n = x.shape[0]
    r = pltpu.roll(x, n - k, 0)
    q = pltpu.roll(next8, SUBLANES - k, 0)
    tail = jnp.where(_row_iota(x.shape[1]) >= SUBLANES - k, q, r[n - SUBLANES:])
    return jnp.concatenate([r[:n - SUBLANES], tail], axis=0)


def _heads_nn(x_bf, w_ref):
    return jnp.concatenate(
        [jnp.dot(x_bf[:, h * HD:(h + 1) * HD], w_ref[h], preferred_element_type=F32) for h in range(NH)], axis=1)


def _heads_nt(x_bf, w_ref):
    return jnp.concatenate(
        [lax.dot_general(x_bf[:, h * HD:(h + 1) * HD], w_ref[h], (((1,), (1,)), ((), ())), preferred_element_type=F32)
         for h in range(NH)], axis=1)


def _dot_nt(a, b):
    return lax.dot_general(a, b, (((1,), (1,)), ((), ())), preferred_element_type=F32)


def _dot_tn(a, b):
    return lax.dot_general(a, b, (((0,), (0,)), ((), ())), preferred_element_type=F32)


def _colsum(x):
    return jnp.sum(x, axis=0, keepdims=True)


def _prev_halo_map(tm, col):
    return lambda i, *_: (jnp.maximum(i * (tm // HALO) - 1, 0), col)


def _norm_proj(x, g, scale, shift, w, name):
    t, n = x.shape[0], w.shape[1]
    tm = _tile_big(t)

    def body(x_ref, g_ref, sc_ref, sh_ref, w_ref, h_ref, z_ref):
        xv = x_ref[...]
        r = lax.rsqrt(jnp.mean(xv * xv, axis=-1, keepdims=True) + EPS)
        hb = ((xv * r * g_ref[...]) * (1.0 + sc_ref[...]) + sh_ref[...]).astype(BF16)
        h_ref[...] = hb
        for c0 in range(0, n, D):
            z_ref[:, c0:c0 + D] = jnp.dot(hb, w_ref[:, c0:c0 + D], preferred_element_type=F32).astype(BF16)

    vec = _const_spec((1, D))
    (scale, sc_spec), (shift, sh_spec) = _vec_operand(scale), _vec_operand(shift)
    return pl.pallas_call(
        body, name=name, grid=(t // tm,),
        in_specs=[pl.BlockSpec((tm, D), lambda i: (i, 0)), vec, sc_spec, sh_spec, _const_spec((D, n), True)],
        out_specs=[pl.BlockSpec((tm, D), lambda i: (i, 0)), pl.BlockSpec((tm, n), lambda i: (i, 0))],
        out_shape=[jax.ShapeDtypeStruct((t, D), BF16), jax.ShapeDtypeStruct((t, n), BF16)],
        compiler_params=_cparams(1),
    )(x, g, scale, shift, w)


def _lru_gates(xc, wa_ref, ba, wx_ref, bx, ls):
    xb = xc.astype(BF16)
    ra = _sigmoid(_heads_nn(xb, wa_ref) + ba)
    ia = _sigmoid(_heads_nn(xb, wx_ref) + bx)
    la = LRU_C * ra * ls
    a = jnp.exp(la)
    mult = jnp.sqrt(-jnp.tanh(la) * (1.0 + a * a))
    return ra, ia, a, mult


def _conv4(xr, prev8, cw_ref, cb):
    return (cb + cw_ref[3:4, :] * xr + cw_ref[2:3, :] * _shift_down(xr, 1, prev8)
            + cw_ref[1:2, :] * _shift_down(xr, 2, prev8) + cw_ref[0:1, :] * _shift_down(xr, 3, prev8))


def _rnn_fwd(z, cw, cb, wa, ba, wx, bx, lam):
    t = z.shape[0]
    tm = _tile_seq(t)
    ngrp = tm // SUBLANES

    def body(xr_ref, xp_ref, gr_ref, cw_ref, cb_ref, wa_ref, ba_ref, wx_ref, bx_ref, lam_ref,
             h_ref, ya_ref, xc_ref, ra_ref, ia_ref, gg_ref, hg_ref, carry_ref, a_scr, u_scr):
        i = pl.program_id(0)

        @pl.when(i == 0)
        def _():
            carry_ref[...] = jnp.zeros_like(carry_ref)

        xr = xr_ref[...].astype(F32)
        prev8 = jnp.where(i == 0, 0.0, xp_ref[...].astype(F32)[HALO - SUBLANES:])
        xc = _conv4(xr, prev8, cw_ref, cb_ref[...])
        ra, ia, a, mult = _lru_gates(xc, wa_ref, ba_ref[...], wx_ref, bx_ref[...], _log_sigmoid(lam_ref[...]))
        xc_ref[...] = xc.astype(BF16)
        ra_ref[...] = ra.astype(BF16)
        ia_ref[...] = ia.astype(BF16)
        a_scr[...] = a
        u_scr[...] = mult * (ia * xc)
        row = _row_iota(D)

        def grp(j, carry):
            r0 = pl.multiple_of(j * SUBLANES, SUBLANES)
            av = a_scr[pl.ds(r0, SUBLANES), :]
            uv = u_scr[pl.ds(r0, SUBLANES), :]
            for d in (1, 2, 4):
                m = row >= d
                uv = jnp.where(m, av * pltpu.roll(uv, d, 0) + uv, uv)
                av = jnp.where(m, av * pltpu.roll(av, d, 0), av)
            hv = uv + av * carry
            h_ref[pl.ds(r0, SUBLANES), :] = hv
            return hv[SUBLANES - 1:SUBLANES, :]

        carry_ref[0:1, :] = lax.fori_loop(0, ngrp, grp, carry_ref[0:1, :])
        grv = gr_ref[...].astype(F32)
        gg, tg = _gelu_t(grv)
        hv = h_ref[...]
        ya_ref[...] = (hv * gg).astype(BF16)
        gg_ref[...] = gg.astype(BF16)
        hg_ref[...] = (hv * _gelu_grad(tg)).astype(BF16)

    vec = _const_spec((1, D))
    wspec = _const_spec((NH, HD, HD))
    tile = pl.BlockSpec((tm, D), lambda i: (i, 0))
    bshape = jax.ShapeDtypeStruct((t, D), BF16)
    return pl.pallas_call(
        body, name="rnn_fwd", grid=(t // tm,),
        in_specs=[tile, pl.BlockSpec((HALO, D), _prev_halo_map(tm, 0)),
                  pl.BlockSpec((tm, D), lambda i: (i, 1)), _const_spec((4, D)), vec, wspec, vec, wspec, vec, vec],
        out_specs=[tile] * 7,
        out_shape=[jax.ShapeDtypeStruct((t, D), F32)] + [bshape] * 6,
        scratch_shapes=[pltpu.VMEM((SUBLANES, D), F32), pltpu.VMEM((tm, D), F32), pltpu.VMEM((tm, D), F32)],
        compiler_params=_cparams(1),
    )(z, z, z, cw, cb, wa, ba, wx, bx, lam)


def _sgu_fwd(z, lng, lnb, wm, bst):
    t = z.shape[0]
    tm = _tile_big(t)

    def body(zu_ref, zv_ref, lng_ref, lnb_ref, wm_ref, bst_ref, yb_ref, gu_ref, mg_ref, vh_ref, gpv_ref, rstd_ref):
        gu, su = _gelu_t(zu_ref[...].astype(F32))
        gv, sv = _gelu_t(zv_ref[...].astype(F32))
        mu = jnp.mean(gv, axis=-1, keepdims=True)
        cen = gv - mu
        rstd = lax.rsqrt(jnp.mean(cen * cen, axis=-1, keepdims=True) + EPS)
        vhat = cen * rstd
        vb = (vhat * lng_ref[...] + lnb_ref[...]).astype(BF16)
        rows = []
        for b0 in range(0, tm, HD):
            rows.append(jnp.concatenate(
                [jnp.dot(wm_ref[g], vb[b0:b0 + HD, g * HD:(g + 1) * HD], preferred_element_type=F32)
                 + bst_ref[:, g:g + 1] for g in range(NH)], axis=1))
        mixed = jnp.concatenate(rows, axis=0) if len(rows) > 1 else rows[0]
        yb_ref[...] = (gu * mixed).astype(BF16)
        gu_ref[...] = gu.astype(BF16)
        mg_ref[...] = (mixed * _gelu_grad(su)).astype(BF16)
        vh_ref[...] = vhat.astype(BF16)
        gpv_ref[...] = _gelu_grad(sv).astype(BF16)
        rstd_ref[...] = rstd

    vec = _const_spec((1, D))
    tile = pl.BlockSpec((tm, D), lambda i: (i, 0))
    bshape = jax.ShapeDtypeStruct((t, D), BF16)
    return pl.pallas_call(
        body, name="sgu_fwd", grid=(t // tm,),
        in_specs=[pl.BlockSpec((tm, D), lambda i: (i, 2)), pl.BlockSpec((tm, D), lambda i: (i, 3)), vec, vec,
                  _const_spec((NH, HD, HD)), _const_spec((HD, NH))],
        out_specs=[tile] * 5 + [pl.BlockSpec((tm, 1), lambda i: (i, 0))],
        out_shape=[bshape] * 5 + [jax.ShapeDtypeStruct((t, 1), F32)],
        compiler_params=_cparams(1),
    )(z, z, lng, lnb, wm, bst)


def _merge_fwd(ya_pre, yb_pre, z, x, gate1, wba, wbb, wout):
    t = x.shape[0]
    tm = _tile_big(t)

    def body(yap_ref, ybp_ref, ga_ref, gb_ref, x_ref, g1_ref, wba_ref, wbb_ref, wo_ref,
             x2_ref, ya_ref, yb_ref, mg_ref, o1_ref):
        ya = jnp.dot(yap_ref[...], wba_ref[...], preferred_element_type=F32)
        yb = jnp.dot(ybp_ref[...], wbb_ref[...], preferred_element_type=F32)
        merged = _sigmoid(ga_ref[...].astype(F32)) * ya + _sigmoid(gb_ref[...].astype(F32)) * yb
        mb = merged.astype(BF16)
        o1 = jnp.dot(mb, wo_ref[...], preferred_element_type=F32)
        x2_ref[...] = x_ref[...] + g1_ref[...] * o1
        ya_ref[...] = ya.astype(BF16)
        yb_ref[...] = yb.astype(BF16)
        mg_ref[...] = mb
        o1_ref[...] = o1.astype(BF16)

    tile = pl.BlockSpec((tm, D), lambda i: (i, 0))
    wspec = _const_spec((D, D))
    bshape = jax.ShapeDtypeStruct((t, D), BF16)
    gate1, g1_spec = _vec_operand(gate1)
    return pl.pallas_call(
        body, name="merge_fwd", grid=(t // tm,),
        in_specs=[tile, tile, pl.BlockSpec((tm, D), lambda i: (i, 4)), pl.BlockSpec((tm, D), lambda i: (i, 5)),
                  tile, g1_spec, wspec, wspec, wspec],
        out_specs=[tile] * 5,
        out_shape=[jax.ShapeDtypeStruct((t, D), F32), bshape, bshape, bshape, bshape],
        compiler_params=_cparams(1),
    )(ya_pre, yb_pre, z, z, x, gate1, wba, wbb, wout)


def _conv3(u, prev8, cw_ref, cb):
    return cb + cw_ref[2:3, :] * u + cw_ref[1:2, :] * _shift_down(u, 1, prev8) + cw_ref[0:1, :] * _shift_down(u, 2, prev8)


def _ffn_proj_mid(x2, g, scale, shift, w, cw, cb):
    t = x2.shape[0]
    tm = _tile_big(t)
    nc = DFF // D

    def body(x_ref, g_ref, sc_ref, sh_ref, wa_ref, wv_ref, cwa_ref, cwv_ref, cba_ref, cbv_ref,
             h_ref, upa_ref, upv_ref, ff_ref, fa_ref, fv_ref, hb_scr, prev_ref):
        i, c = pl.program_id(0), pl.program_id(1)

        @pl.when(i == 0)
        def _():
            prev_ref[c] = jnp.zeros((2, SUBLANES, D), F32)

        @pl.when(c == 0)
        def _():
            xv = x_ref[...]
            r = lax.rsqrt(jnp.mean(xv * xv, axis=-1, keepdims=True) + EPS)
            hb_scr[...] = ((xv * r * g_ref[...]) * (1.0 + sc_ref[...]) + sh_ref[...]).astype(BF16)
            h_ref[...] = hb_scr[...]

        hb = hb_scr[...]
        halves = []
        for s, (w_ref, up_ref, cw_ref, cb_ref) in enumerate(((wa_ref, upa_ref, cwa_ref, cba_ref),
                                                             (wv_ref, upv_ref, cwv_ref, cbv_ref))):
            u = jnp.dot(hb, w_ref[...], preferred_element_type=F32)
            up_ref[...] = u.astype(BF16)
            halves.append(_conv3(u, prev_ref[c, s], cw_ref, cb_ref[...]))
            prev_ref[c, s] = u[tm - SUBLANES:]
        act, val = halves
        ga, ta = _gelu_t(act)
        ff_ref[...] = (ga * val).astype(BF16)
        fa_ref[...] = (val * _gelu_grad(ta)).astype(BF16)
        fv_ref[...] = ga.astype(BF16)

    def cols(rows, off):
        return pl.BlockSpec((rows, D), lambda i, c: (0, off + c))

    vec = pl.BlockSpec((1, D), lambda i, c: (0, 0))
    row_tile = pl.BlockSpec((tm, D), lambda i, c: (i, 0))
    chunk = pl.BlockSpec((tm, D), lambda i, c: (i, c))
    hshape = jax.ShapeDtypeStruct((t, DFF), BF16)
    (scale, sc_spec), (shift, sh_spec) = _vec_operand(scale), _vec_operand(shift)
    return pl.pallas_call(
        body, name="ffn_proj_mid", grid=(t // tm, nc),
        in_specs=[row_tile, vec, sc_spec, sh_spec, cols(D, 0), cols(D, nc), cols(3, 0), cols(3, nc), cols(1, 0), cols(1, nc)],
        out_specs=[row_tile, chunk, chunk, chunk, chunk, chunk],
        out_shape=[jax.ShapeDtypeStruct((t, D), BF16), hshape, hshape, hshape, hshape, hshape],
        scratch_shapes=[pltpu.VMEM((tm, D), BF16), pltpu.VMEM((nc, 2, SUBLANES, D), F32)],
        compiler_params=_cparams(2),
    )(x2, g, scale, shift, w, w, cw, cw, cb, cb)


def _ffn_out_loss(ff, wd, x2, target, gate2, gfin):
    t = x2.shape[0]
    tm = _tile_big(t)

    def body(ff_ref, wd_ref, x2_ref, tg_ref, g2_ref, gf_ref, dx3_ref, do2_ref, loss_ref, dgf_ref, dg2_ref):
        @pl.when(pl.program_id(0) == 0)
        def _():
            loss_ref[...] = jnp.zeros_like(loss_ref)
            dgf_ref[...] = jnp.zeros_like(dgf_ref)
            dg2_ref[...] = jnp.zeros_like(dg2_ref)

        o2 = jnp.dot(ff_ref[...], wd_ref[...], preferred_element_type=F32)
        x3 = x2_ref[...] + g2_ref[...] * o2
        r = lax.rsqrt(jnp.mean(x3 * x3, axis=-1, keepdims=True) + EPS)
        xhat = x3 * r
        err = xhat * gf_ref[...] - tg_ref[...]
        loss_ref[...] += 0.5 * jnp.sum(jnp.mean(err * err, axis=-1, keepdims=True), axis=0, keepdims=True)
        dy = err * (1.0 / D)
        dgf_ref[...] += _colsum(dy * xhat)
        dxh = dy * gf_ref[...]
        dx3 = r * (dxh - xhat * jnp.mean(dxh * xhat, axis=-1, keepdims=True))
        dx3_ref[...] = dx3
        do2_ref[...] = (dx3 * g2_ref[...]).astype(BF16)
        dg2_ref[...] += _colsum(dx3 * o2)

    tile = pl.BlockSpec((tm, D), lambda i: (i, 0))
    vec = _const_spec((1, D))
    gate2, g2_spec = _vec_operand(gate2)
    return pl.pallas_call(
        body, name="ffn_out_loss", grid=(t // tm,),
        in_specs=[pl.BlockSpec((tm, DFF), lambda i: (i, 0)), _const_spec((DFF, D), True), tile, tile, g2_spec, vec],
        out_specs=[tile, tile, _const_spec((1, 1)), vec, vec],
        out_shape=[jax.ShapeDtypeStruct((t, D), F32), jax.ShapeDtypeStruct((t, D), BF16),
                   jax.ShapeDtypeStruct((1, 1), F32),
                   jax.ShapeDtypeStruct((1, D), F32), jax.ShapeDtypeStruct((1, D), F32)],
        compiler_params=_cparams(1),
    )(ff, wd, x2, target, gate2, gfin)


def _ffn_down_bwd(do2, ff, fa, fv, wd):
    t = do2.shape[0]
    tm = min(1024, t)
    nc = DFF // D

    def body(do2_ref, ff_ref, fa_ref, fv_ref, wd_ref, da_ref, dv_ref, dwd_ref, dcba_ref, dcbv_ref):
        @pl.when(pl.program_id(1) == 0)
        def _():
            for r in (dwd_ref, dcba_ref, dcbv_ref):
                r[...] = jnp.zeros_like(r)

        do2 = do2_ref[...]
        dwd_ref[...] += _dot_tn(ff_ref[...], do2)
        dff = _dot_nt(do2, wd_ref[...])
        dact = dff * fa_ref[...].astype(F32)
        dval = dff * fv_ref[...].astype(F32)
        da_ref[...] = dact.astype(BF16)
        dv_ref[...] = dval.astype(BF16)
        dcba_ref[...] += _colsum(dact)
        dcbv_ref[...] += _colsum(dval)

    blk = pl.BlockSpec((tm, D), lambda c, i: (i, c))
    vec = pl.BlockSpec((1, D), lambda c, i: (0, c))
    return pl.pallas_call(
        body, name="ffn_down_bwd", grid=(nc, t // tm),
        in_specs=[pl.BlockSpec((tm, D), lambda c, i: (i, 0)),
                  blk, blk, blk, pl.BlockSpec((D, D), lambda c, i: (c, 0))],
        out_specs=[blk, blk, pl.BlockSpec((D, D), lambda c, i: (c, 0)), vec, vec],
        out_shape=[jax.ShapeDtypeStruct((t, DFF), BF16), jax.ShapeDtypeStruct((t, DFF), BF16),
                   jax.ShapeDtypeStruct((DFF, D), F32),
                   jax.ShapeDtypeStruct((1, DFF), F32), jax.ShapeDtypeStruct((1, DFF), F32)],
        compiler_params=_cparams(2),
    )(do2, ff, fa, fv, wd)


def _modnorm_bwd(dh, xv, g, scale):
    r = lax.rsqrt(jnp.mean(xv * xv, axis=-1, keepdims=True) + EPS)
    xhat = xv * r
    dxn = dh * (1.0 + scale)
    dxh = dxn * g
    dx = r * (dxh - xhat * jnp.mean(dxh * xhat, axis=-1, keepdims=True))
    return dx, _colsum(dh), _colsum(dh * (xhat * g)), _colsum(dxn * xhat)


def _ffn_up_bwd(dact, dval, up_a, up_v, cw, wup, x2, dx3, gffn, scale2, o1, gate1):
    t = x2.shape[0]
    tm = _tile_seq(t)
    nt = t // tm
    nc = DFF // D

    def body(da_ref, dan_ref, dv_ref, dvn_ref, ua_ref, uv_ref, cw_ref, w_ref, x2_ref, dx3_ref, g_ref, sc_ref, o1_ref, g1_ref,
             dup_ref, dx2_ref, do1_ref, dcw_ref, dsh_ref, dsc_ref, dg_ref, dg1_ref):
        i = pl.program_id(0)

        @pl.when(i == 0)
        def _():
            for r in (dcw_ref, dsh_ref, dsc_ref, dg_ref, dg1_ref):
                r[...] = jnp.zeros_like(r)

        last = i == nt - 1
        dh = jnp.zeros((tm, D), F32)
        for half, (d_ref, dn_ref, u_ref) in enumerate(((da_ref, dan_ref, ua_ref), (dv_ref, dvn_ref, uv_ref))):
            nxt = jnp.where(last, 0.0, dn_ref[...].astype(F32)[:SUBLANES])
            for c in range(nc):
                c0 = half * DFF + c * D
                dv = d_ref[:, c * D:(c + 1) * D].astype(F32)
                nx = nxt[:, c * D:(c + 1) * D]
                taps = (_shift_up(dv, 2, nx), _shift_up(dv, 1, nx), dv)
                dup = (cw_ref[2:3, c0:c0 + D] * taps[2] + cw_ref[1:2, c0:c0 + D] * taps[1]
                       + cw_ref[0:1, c0:c0 + D] * taps[0]).astype(BF16)
                upv = u_ref[:, c * D:(c + 1) * D].astype(F32)
                for k in range(3):
                    dcw_ref[k:k + 1, c0:c0 + D] += _colsum(taps[k] * upv)
                dup_ref[:, c0:c0 + D] = dup
                dh = dh + _dot_nt(dup, w_ref[:, c0:c0 + D])
        dxn, dsh, dsc, dg = _modnorm_bwd(dh, x2_ref[...], g_ref[...], sc_ref[...])
        dx2 = dx3_ref[...] + dxn
        dx2_ref[...] = dx2
        do1_ref[...] = (dx2 * g1_ref[...]).astype(BF16)
        dsh_ref[...] += dsh
        dsc_ref[...] += dsc
        dg_ref[...] += dg
        dg1_ref[...] += _colsum(dx2 * o1_ref[...].astype(F32))

    tile = pl.BlockSpec((tm, D), lambda i: (i, 0))
    wide = pl.BlockSpec((tm, DFF), lambda i: (i, 0))
    nxt = pl.BlockSpec((HALO, DFF), lambda i: (jnp.minimum((i + 1) * (tm // HALO), t // HALO - 1), 0))
    vec = _const_spec((1, D))
    vshape = jax.ShapeDtypeStruct((1, D), F32)
    (scale2, sc_spec), (gate1, g1_spec) = _vec_operand(scale2), _vec_operand(gate1)
    return pl.pallas_call(
        body, name="ffn_up_bwd", grid=(nt,),
        in_specs=[wide, nxt, wide, nxt, wide, wide,
                  _const_spec((3, 2 * DFF)), _const_spec((D, 2 * DFF), True),
                  tile, tile, vec, sc_spec, tile, g1_spec],
        out_specs=[pl.BlockSpec((tm, 2 * DFF), lambda i: (i, 0)), tile, tile, _const_spec((3, 2 * DFF)),
                   vec, vec, vec, vec],
        out_shape=[jax.ShapeDtypeStruct((t, 2 * DFF), BF16), jax.ShapeDtypeStruct((t, D), F32),
                   jax.ShapeDtypeStruct((t, D), BF16), jax.ShapeDtypeStruct((3, 2 * DFF), F32),
                   vshape, vshape, vshape, vshape],
        compiler_params=_cparams(1),
    )(dact, dact, dval, dval, up_a, up_v, cw, wup, x2, dx3, gffn, scale2, o1, gate1)


def _xt_y(a, b, name):
    t, k = a.shape
    n = b.shape[1]
    tm = min(1024, t)
    bn = 3072 if n % 3072 == 0 else D

    def body(a_ref, b_ref, o_ref):
        @pl.when(pl.program_id(1) == 0)
        def _():
            o_ref[...] = jnp.zeros_like(o_ref)

        o_ref[...] += _dot_tn(a_ref[...], b_ref[...])

    return pl.pallas_call(
        body, name=name, grid=(n // bn, t // tm),
        in_specs=[pl.BlockSpec((tm, k), lambda j, i: (i, 0)), pl.BlockSpec((tm, bn), lambda j, i: (i, j))],
        out_specs=pl.BlockSpec((k, bn), lambda j, i: (0, j)),
        out_shape=jax.ShapeDtypeStruct((k, n), F32),
        compiler_params=_cparams(2),
    )(a, b)


def _acc_spec(shape, index):
    return pl.BlockSpec(shape, lambda *_: index, pipeline_mode=pl.Buffered(1))


def _out_bwd(do1, wout, merged, ya, yb, z, h1):
    t = do1.shape[0]
    tm = _tile_big(t)

    def body(do1_ref, wo_ref, mg_ref, ya_ref, yb_ref, ga_ref, gb_ref, h1_ref,
             dya_ref, dyb_ref, dz_ref, dwo_ref, dwin_ref):
        @pl.when(pl.program_id(0) == 0)
        def _():
            dwo_ref[...] = jnp.zeros_like(dwo_ref)
            dwin_ref[...] = jnp.zeros_like(dwin_ref)

        do1v = do1_ref[...]
        dwo_ref[...] += _dot_tn(mg_ref[...], do1v)
        dm = _dot_nt(do1v, wo_ref[...])
        sa = _sigmoid(ga_ref[...].astype(F32))
        sb = _sigmoid(gb_ref[...].astype(F32))
        dya_ref[...] = (dm * sa).astype(BF16)
        dyb_ref[...] = (dm * sb).astype(BF16)
        dga = (dm * ya_ref[...].astype(F32) * sa * (1.0 - sa)).astype(BF16)
        dgb = (dm * yb_ref[...].astype(F32) * sb * (1.0 - sb)).astype(BF16)
        dz_ref[:, 0:D] = dga
        dz_ref[:, D:2 * D] = dgb
        h1v = h1_ref[...]
        dwin_ref[:, 0:D] += _dot_tn(h1v, dga)
        dwin_ref[:, D:2 * D] += _dot_tn(h1v, dgb)

    tile = pl.BlockSpec((tm, D), lambda i: (i, 0))
    bshape = jax.ShapeDtypeStruct((t, D), BF16)
    return pl.pallas_call(
        body, name="out_bwd", grid=(t // tm,),
        in_specs=[tile, _const_spec((D, D), True), tile, tile, tile,
                  pl.BlockSpec((tm, D), lambda i: (i, 4)), pl.BlockSpec((tm, D), lambda i: (i, 5)), tile],
        out_specs=[tile, tile, pl.BlockSpec((tm, 2 * D), lambda i: (i, 2)), _acc_spec((D, D), (0, 0)),
                   _acc_spec((D, 2 * D), (0, 2))],
        out_shape=[bshape, bshape, jax.ShapeDtypeStruct((t, NCOL_IN), BF16), jax.ShapeDtypeStruct((D, D), F32),
                   jax.ShapeDtypeStruct((D, NCOL_IN), F32)],
        compiler_params=_cparams(1),
    )(do1, wout, merged, ya, yb, z, z, h1)


def _rnn_bwd(dya, ya_pre, wba, h1, z, saved, h, dz, dwin, cw, wa, wx, lam):
    t = z.shape[0]
    tm = _tile_seq(t)
    nt = t // tm
    ngrp = tm // SUBLANES
    hpt = tm // HALO

    def body(dya_ref, yap_ref, wba_ref, h1_ref, xr_ref, xc_ref, ra_ref, ia_ref, gg_ref, hg_ref, h_ref, hp_ref,
             dz_any, dwin_any, cw_ref, wa_ref, wx_ref, lam_ref,
             dz_ref, dwin_ref, dwba_ref, dcw_ref, dcb_ref, dwa_ref, dba_ref, dwx_ref, dbx_ref, dlam_ref,
             a_first, g_first, dxc_first, b_scr, d_scr, g_scr):
        del dz_any, dwin_any
        i = pl.program_id(0)

        @pl.when(i == 0)
        def _():
            for r in (dwin_ref, dwba_ref, dcw_ref, dcb_ref, dwa_ref, dba_ref, dwx_ref, dbx_ref, dlam_ref,
                      a_first, g_first, dxc_first):
                r[...] = jnp.zeros_like(r)

        dya_v = dya_ref[...]
        dwba_ref[...] += _dot_tn(yap_ref[...], dya_v)
        dyap_v = _dot_nt(dya_v, wba_ref[...])
        h1v = h1_ref[...]

        first_tile = i == nt - 1
        xc = xc_ref[...].astype(F32)
        ra = ra_ref[...].astype(F32)
        ia = ia_ref[...].astype(F32)
        lam_v = lam_ref[...]
        ls = _log_sigmoid(lam_v)
        la = LRU_C * ra * ls
        a = jnp.exp(la)
        mult = jnp.sqrt(-jnp.tanh(la) * (1.0 + a * a))
        hprev8 = jnp.where(first_tile, 0.0, hp_ref[...][HALO - SUBLANES:])
        h_prev = _shift_down(h_ref[...], 1, hprev8)
        dgr = (dyap_v * hg_ref[...].astype(F32)).astype(BF16)
        dz_ref[:, D:2 * D] = dgr
        dwin_ref[:, D:2 * D] += _dot_tn(h1v, dgr)

        b_scr[...] = _shift_up(a, 1, a_first[...])
        d_scr[...] = dyap_v * gg_ref[...].astype(F32)
        row = _row_iota(D)

        def grp(jj, carry):
            r0 = pl.multiple_of((ngrp - 1 - jj) * SUBLANES, SUBLANES)
            bv = b_scr[pl.ds(r0, SUBLANES), :]
            dv = d_scr[pl.ds(r0, SUBLANES), :]
            for d in (1, 2, 4):
                m = row < SUBLANES - d
                dv = jnp.where(m, dv + bv * pltpu.roll(dv, SUBLANES - d, 0), dv)
                bv = jnp.where(m, bv * pltpu.roll(bv, SUBLANES - d, 0), bv)
            gv = dv + bv * carry
            g_scr[pl.ds(r0, SUBLANES), :] = gv
            return gv[0:1, :]

        lax.fori_loop(0, ngrp, grp, g_first[0:1, :])
        g = g_scr[...]
        a_first[...] = a[:SUBLANES]
        g_first[...] = g[:SUBLANES]

        da = g * h_prev
        gx = g * xc
        dmult = gx * ia
        dia = gx * mult
        dxc = g * (mult * ia)
        dla = da * a - dmult * (a * a) / mult
        dra = dla * (LRU_C * ls)
        dlam_ref[...] += _colsum(dla * ra) * (LRU_C * _sigmoid(-lam_v))
        dpa = dra * ra * (1.0 - ra)
        dpx = dia * ia * (1.0 - ia)
        dba_ref[...] += _colsum(dpa)
        dbx_ref[...] += _colsum(dpx)
        dpab = dpa.astype(BF16)
        dpxb = dpx.astype(BF16)
        xcb = xc_ref[...]
        for hd in range(NH):
            sl = slice(hd * HD, (hd + 1) * HD)
            dwa_ref[hd] += _dot_tn(xcb[:, sl], dpab[:, sl])
            dwx_ref[hd] += _dot_tn(xcb[:, sl], dpxb[:, sl])
        dxc = dxc + _heads_nt(dpab, wa_ref) + _heads_nt(dpxb, wx_ref)

        nxt = dxc_first[...]
        taps = (_shift_up(dxc, 3, nxt), _shift_up(dxc, 2, nxt), _shift_up(dxc, 1, nxt), dxc)
        dxr = cw_ref[0:1, :] * taps[0]
        for k in range(1, 4):
            dxr = dxr + cw_ref[k:k + 1, :] * taps[k]
        dxrb = dxr.astype(BF16)
        dz_ref[:, 0:D] = dxrb
        dwin_ref[:, 0:D] += _dot_tn(h1v, dxrb)
        dxc_first[...] = dxc[:SUBLANES]
        dcb_ref[...] += _colsum(dxc)
        xr = xr_ref[...].astype(F32)
        for k in range(4):
            dcw_ref[k:k + 1, :] += _colsum(taps[k] * xr)

    def rev(col):
        return lambda i: (nt - 1 - i, col)

    vec = _const_spec((1, D))
    wspec = _const_spec((NH, HD, HD))
    vshape = jax.ShapeDtypeStruct((1, D), F32)
    wshape = jax.ShapeDtypeStruct((NH, HD, HD), F32)
    any_spec = pl.BlockSpec(memory_space=pl.ANY)
    tile = pl.BlockSpec((tm, D), rev(0))
    outs = pl.pallas_call(
        body, name="rnn_bwd", grid=(nt,),
        in_specs=[tile, tile, _const_spec((D, D), True), tile, tile, tile, tile, tile, tile, tile, tile,
                  pl.BlockSpec((HALO, D), lambda i: (jnp.maximum((nt - 1 - i) * hpt - 1, 0), 0)),
                  any_spec, any_spec, _const_spec((4, D)), wspec, wspec, vec],
        out_specs=[pl.BlockSpec((tm, 2 * D), rev(0)), _acc_spec((D, 2 * D), (0, 0)), _acc_spec((D, D), (0, 0)),
                   _const_spec((4, D)), vec, wspec, vec, wspec, vec, vec],
        out_shape=[jax.ShapeDtypeStruct((t, NCOL_IN), BF16), jax.ShapeDtypeStruct((D, NCOL_IN), F32),
                   jax.ShapeDtypeStruct((D, D), F32), jax.ShapeDtypeStruct((4, D), F32), vshape,
                   wshape, vshape, wshape, vshape, vshape],
        scratch_shapes=[pltpu.VMEM((SUBLANES, D), F32), pltpu.VMEM((SUBLANES, D), F32), pltpu.VMEM((SUBLANES, D), F32),
                        pltpu.VMEM((tm, D), F32), pltpu.VMEM((tm, D), F32), pltpu.VMEM((tm, D), F32)],
        input_output_aliases={12: 0, 13: 1},
        compiler_params=_cparams(1),
    )(dya, ya_pre, wba, h1, z, *saved, h, h, dz, dwin, cw, wa, wx, lam)
    return outs


def _sgu_bwd(dyb, yb_pre, wbb, h1, saved, dz, dwin, lng, lnb, wmt, mask):
    t = dyb.shape[0]
    tm = _tile_big(t)

    def body(dyb_ref, ybp_ref, wbb_ref, h1_ref, gu_ref, mg_ref, vh_ref, gpv_ref, rstd_ref, dz_any, dwin_any,
             lng_ref, lnb_ref, wmt_ref, mask_ref,
             dz_ref, dwin_ref, dwbb_ref, dws_ref, dbst_ref, dlng_ref, dlnb_ref):
        del dz_any, dwin_any

        @pl.when(pl.program_id(0) == 0)
        def _():
            for r in (dwin_ref, dwbb_ref, dws_ref, dbst_ref, dlng_ref, dlnb_ref):
                r[...] = jnp.zeros_like(r)

        lng_v = lng_ref[...]
        vhat = vh_ref[...].astype(F32)
        vb = (vhat * lng_v + lnb_ref[...]).astype(BF16)
        rstd = rstd_ref[...]
        dyb_v = dyb_ref[...]
        dwbb_ref[...] += _dot_tn(ybp_ref[...], dyb_v)
        dyb = _dot_nt(dyb_v, wbb_ref[...])
        h1v = h1_ref[...]
        dzu = (dyb * mg_ref[...].astype(F32)).astype(BF16)
        dz_ref[:, 0:D] = dzu
        dwin_ref[:, 0:D] += _dot_tn(h1v, dzu)
        dmix = dyb * gu_ref[...].astype(F32)
        dmb = dmix.astype(BF16)
        rows = []
        lane = lax.broadcasted_iota(jnp.int32, (HD, NH), 1)
        dbst = jnp.zeros((HD, NH), F32)
        for b0 in range(0, tm, HD):
            cols = []
            for g in range(NH):
                sl = slice(g * HD, (g + 1) * HD)
                dmg = dmb[b0:b0 + HD, sl]
                dws_ref[g] += _dot_nt(dmg, vb[b0:b0 + HD, sl]) * mask_ref[...]
                cols.append(jnp.dot(wmt_ref[g], dmg, preferred_element_type=F32))
                dbst = dbst + jnp.where(lane == g, jnp.sum(dmix[b0:b0 + HD, sl], axis=1, keepdims=True), 0.0)
            rows.append(jnp.concatenate(cols, axis=1))
        dbst_ref[...] += dbst
        dvln = jnp.concatenate(rows, axis=0) if len(rows) > 1 else rows[0]
        dlng_ref[...] += _colsum(dvln * vhat)
        dlnb_ref[...] += _colsum(dvln)
        dvh = dvln * lng_v
        dgv = rstd * (dvh - jnp.mean(dvh, axis=-1, keepdims=True)
                      - vhat * jnp.mean(dvh * vhat, axis=-1, keepdims=True))
        dzv = (dgv * gpv_ref[...].astype(F32)).astype(BF16)
        dz_ref[:, D:2 * D] = dzv
        dwin_ref[:, D:2 * D] += _dot_tn(h1v, dzv)

    vec = _const_spec((1, D))
    wspec = _const_spec((NH, HD, HD))
    vshape = jax.ShapeDtypeStruct((1, D), F32)
    tile = pl.BlockSpec((tm, D), lambda i: (i, 0))
    any_spec = pl.BlockSpec(memory_space=pl.ANY)
    return pl.pallas_call(
        body, name="sgu_bwd", grid=(t // tm,),
        in_specs=[tile, tile, _const_spec((D, D), True), tile, tile, tile, tile, tile,
                  pl.BlockSpec((tm, 1), lambda i: (i, 0)), any_spec, any_spec,
                  vec, vec, wspec, _const_spec((HD, HD))],
        out_specs=[pl.BlockSpec((tm, 2 * D), lambda i: (i, 1)), _acc_spec((D, 2 * D), (0, 1)), _acc_spec((D, D), (0, 0)),
                   wspec, _const_spec((HD, NH)), vec, vec],
        out_shape=[jax.ShapeDtypeStruct((t, NCOL_IN), BF16), jax.ShapeDtypeStruct((D, NCOL_IN), F32),
                   jax.ShapeDtypeStruct((D, D), F32), jax.ShapeDtypeStruct((NH, HD, HD), F32),
                   jax.ShapeDtypeStruct((HD, NH), F32), vshape, vshape],
        input_output_aliases={9: 0, 10: 1},
        compiler_params=_cparams(1),
    )(dyb, yb_pre, wbb, h1, *saved, dz, dwin, lng, lnb, wmt, mask)


def _in_bwd(dz, win, x, dx2, g, scale1):
    t = x.shape[0]
    tm = _tile_big(t)

    def body(dz_ref, w_ref, x_ref, dx2_ref, g_ref, sc_ref, dx_ref, dsh_ref, dsc_ref, dg_ref):
        @pl.when(pl.program_id(0) == 0)
        def _():
            for r in (dsh_ref, dsc_ref, dg_ref):
                r[...] = jnp.zeros_like(r)

        dh = jnp.zeros((tm, D), F32)
        for c0 in range(0, NCOL_IN, D):
            dh = dh + _dot_nt(dz_ref[:, c0:c0 + D], w_ref[:, c0:c0 + D])
        dxn, dsh, dsc, dg = _modnorm_bwd(dh, x_ref[...], g_ref[...], sc_ref[...])
        dx_ref[...] = dx2_ref[...] + dxn
        dsh_ref[...] += dsh
        dsc_ref[...] += dsc
        dg_ref[...] += dg

    tile = pl.BlockSpec((tm, D), lambda i: (i, 0))
    vec = _const_spec((1, D))
    vshape = jax.ShapeDtypeStruct((1, D), F32)
    scale1, sc_spec = _vec_operand(scale1)
    return pl.pallas_call(
        body, name="in_bwd", grid=(t // tm,),
        in_specs=[pl.BlockSpec((tm, NCOL_IN), lambda i: (i, 0)), _const_spec((D, NCOL_IN), True), tile, tile, vec,
                  sc_spec],
        out_specs=[tile, vec, vec, vec],
        out_shape=[jax.ShapeDtypeStruct((t, D), F32), vshape, vshape, vshape],
        compiler_params=_cparams(1),
    )(dz, win, x, dx2, g, scale1)


def _mod_cols(c_all, w_ada, b_cols):
    nb, cols = c_all.shape[0], w_ada.shape[1]

    def body(c_ref, w_ref, b_ref, o_ref):
        cv = c_ref[...]
        ca = (cv * _sigmoid(cv)).astype(BF16)
        o_ref[...] = jnp.dot(ca, w_ref[...].astype(BF16), preferred_element_type=F32) + b_ref[...]

    return pl.pallas_call(body, name="mod_cols", out_shape=jax.ShapeDtypeStruct((nb, cols), F32))(c_all, w_ada, b_cols)


def _ada_grad(c_all, dmod_cols):
    cols = dmod_cols.shape[1]

    def body(c_ref, d_ref, o_ref):
        cv = c_ref[...]
        ca = (cv * _sigmoid(cv)).astype(BF16)
        o_ref[...] = _dot_tn(ca, d_ref[...].astype(BF16))

    return pl.pallas_call(body, name="ada_grad", out_shape=jax.ShapeDtypeStruct((D, cols), F32))(c_all, dmod_cols)


def _adamw_update(w, m, v, g):
    bc1 = 1.0 - ADAM_B1 ** ADAM_STEP
    bc2 = 1.0 - ADAM_B2 ** ADAM_STEP
    mn = ADAM_B1 * m + (1.0 - ADAM_B1) * g
    vn = ADAM_B2 * v + (1.0 - ADAM_B2) * (g * g)
    return -ADAM_LR * ((mn / bc1) / (jnp.sqrt(vn / bc2) + ADAM_EPS) + ADAM_WD * w), mn, vn


def _adamw_group(names, ws, ms, vs, packs, name):
    n = len(names)
    starts, r0 = [], 0
    for w in ws:
        starts.append(r0)
        r0 += _pack_rows(w.shape)

    def body(*refs):
        w_refs, m_refs, v_refs, p_ref = refs[:n], refs[n:2 * n], refs[2 * n:3 * n], refs[3 * n]
        outs = refs[3 * n + 1:]
        for k in range(n):
            rows = _pack_rows(ws[k].shape)
            g = None
            for dev in range(N_DEV):
                if ws[k].shape[0] == 1:
                    term = jnp.concatenate(
                        [p_ref[dev, starts[k] + r:starts[k] + r + 1, :] for r in range(rows)], axis=1)
                else:
                    term = p_ref[dev, starts[k]:starts[k] + rows, :]
                g = term if g is None else g + term
            delta, mn, vn = _adamw_update(w_refs[k][...], m_refs[k][...], v_refs[k][...], g)
            for o_ref, val in zip(outs[4 * k:4 * k + 4], (g, delta, mn, vn)):
                o_ref[...] = val

    shapes = [jax.ShapeDtypeStruct(w.shape, F32) for w in ws for _ in range(4)]
    outs = pl.pallas_call(body, name=name, out_shape=shapes,
                          compiler_params=pltpu.CompilerParams(vmem_limit_bytes=VMEM_LIMIT))(*ws, *ms, *vs, packs)
    return {nm: tuple(outs[4 * k:4 * k + 4]) for k, nm in enumerate(names)}


def _adamw(w, m, v, parts, name):
    rows, cols = w.shape
    tr = _row_tile(rows, cols)
    stacked = [p.ndim == 3 for p in parts]

    def body(*refs):
        w_ref, m_ref, v_ref = refs[:3]
        p_refs = refs[3:3 + len(parts)]
        g_ref, d_ref, mo_ref, vo_ref = refs[3 + len(parts):]
        g = None
        for p_ref, st in zip(p_refs, stacked):
            terms = [p_ref[k].astype(F32) for k in range(p_ref.shape[0])] if st else [p_ref[...].astype(F32)]
            for term in terms:
                g = term if g is None else g + term
        delta, mn, vn = _adamw_update(w_ref[...], m_ref[...], v_ref[...], g)
        g_ref[...] = g
        mo_ref[...] = mn
        vo_ref[...] = vn
        d_ref[...] = delta

    tile = pl.BlockSpec((tr, cols), lambda i: (i, 0))
    p_specs = [pl.BlockSpec((p.shape[0], tr, cols), lambda i: (0, i, 0)) if st else tile for p, st in zip(parts, stacked)]
    shp = jax.ShapeDtypeStruct((rows, cols), F32)
    return pl.pallas_call(
        body, name=name, grid=(rows // tr,),
        in_specs=[tile, tile, tile] + p_specs, out_specs=[tile] * 4, out_shape=[shp] * 4,
        compiler_params=_cparams(1),
    )(w, m, v, *parts)


def _mesh_pos():
    return lax.axis_index("x"), lax.axis_index("y"), lax.axis_index("c")


def _other_chips(x, y):
    return [(1 - x, y), (x, 1 - y), (1 - x, 1 - y)]


def _block_of(ref, axis, index, size):
    if axis == 0:
        return ref.at[index]
    return ref.at[:, pl.ds(pl.multiple_of(index * size, 128), size)]


def _all_gather(shards, axes, name):
    n = len(shards)
    per = 7

    def body(*refs):
        ins, outs, done = refs[:n], refs[n:2 * n], refs[2 * n]
        send_sems, recv_sems, local_sems = refs[2 * n + 1:]
        x, y, c = _mesh_pos()
        me, sibling = (x, y, c), (x, y, 1 - c)
        chips = _other_chips(x, y)

        def rows(a, pos):
            return _block_of(outs[a], axes[a], 4 * pos[0] + 2 * pos[1] + pos[2], shards[a].shape[-1])

        def copy(a, k, block, to, src=None):
            return pltpu.make_async_remote_copy(
                src_ref=rows(a, block) if src is None else src, dst_ref=rows(a, block),
                send_sem=send_sems.at[a * per + k], recv_sem=recv_sems.at[a * per + k],
                device_id=to, device_id_type=MESH_IDS)

        mine = [pltpu.make_async_copy(ins[a], rows(a, me), local_sems.at[a]) for a in range(n)]
        for cp in mine:
            cp.start()
        first = []
        for a in range(n):
            first.append(copy(a, 0, me, sibling, src=ins[a]))
            first += [copy(a, 1 + j, me, (*chip, c), src=ins[a]) for j, chip in enumerate(chips)]
        for cp in first:
            cp.start()
        passed = []
        for j, chip in enumerate(chips):
            for a in range(n):
                copy(a, 1 + j, (*chip, c), me).wait_recv()
                fwd = copy(a, 4 + j, (*chip, c), sibling)
                fwd.start()
                passed.append(fwd)
        for a in range(n):
            copy(a, 0, sibling, me).wait_recv()
            for j, chip in enumerate(chips):
                copy(a, 4 + j, (*chip, 1 - c), me).wait_recv()
        for cp in first + passed:
            cp.wait_send()
        for cp in mine:
            cp.wait()
        done[...] = jnp.zeros_like(done)

    def full_shape(s, ax):
        return (N_DEV,) + s.shape if ax == 0 else s.shape[:-1] + (N_DEV * s.shape[-1],)

    any_spec = pl.BlockSpec(memory_space=pl.ANY)
    outs = pl.pallas_call(
        body, name=name,
        in_specs=[any_spec] * n, out_specs=[any_spec] * n + [pl.BlockSpec(memory_space=pltpu.VMEM)],
        out_shape=[jax.ShapeDtypeStruct(full_shape(s, ax), s.dtype) for s, ax in zip(shards, axes)]
        + [jax.ShapeDtypeStruct((SUBLANES, LANES), F32)],
        scratch_shapes=[pltpu.SemaphoreType.DMA((n * per,)), pltpu.SemaphoreType.DMA((n * per,)),
                        pltpu.SemaphoreType.DMA((n,))],
    )(*shards)
    return outs[:n], outs[n]


def _chip_blocks(x, y):
    return [(x, y)] + _other_chips(x, y)


def _sibling_reduce(gs, axis, name):
    g0, n = gs[0], len(gs)
    rows, cols = (g0.shape[1], g0.shape[2]) if axis == 0 else (g0.shape[0], g0.shape[1] // N_DEV)
    chunk = math.gcd(rows, 64)

    def body(*refs):
        g_refs, own_refs, pay_refs = refs[:n], refs[n:2 * n], refs[2 * n:3 * n]
        (stage_buf, send_buf, keep_buf, recv_buf, pay_buf,
         send_sems, recv_sems, stage_sems, keep_sems, out_sems) = refs[3 * n:]
        x, y, c = _mesh_pos()
        sibling = (x, y, 1 - c)
        chips = _chip_blocks(x, y)
        stage, keep, push = [], [], []
        for a in range(n):
            for j, (px, py) in enumerate(chips):
                s = 4 * a + j
                theirs = _block_of(g_refs[a], axis, 4 * px + 2 * py + (1 - c), cols)
                ours = _block_of(g_refs[a], axis, 4 * px + 2 * py + c, cols)
                stage.append(pltpu.make_async_copy(theirs, stage_buf.at[s], stage_sems.at[s]))
                keep.append(pltpu.make_async_copy(ours, keep_buf.at[s], keep_sems.at[s]))
                push.append(pltpu.make_async_remote_copy(
                    src_ref=send_buf.at[s], dst_ref=recv_buf.at[s], send_sem=send_sems.at[s],
                    recv_sem=recv_sems.at[s], device_id=sibling, device_id_type=MESH_IDS))
        for cp in stage[:2]:
            cp.start()
        for s in range(4 * n):
            if s + 2 < 4 * n:
                stage[s + 2].start()
            stage[s].wait()
            keep[s].start()

            def narrow(r, carry, s=s):
                sl = pl.ds(pl.multiple_of(r * chunk, chunk), chunk)
                send_buf[s, sl, :] = stage_buf[s, sl, :].astype(BF16)
                return carry

            lax.fori_loop(0, rows // chunk, narrow, 0)
            push[s].start()
        written = []
        for s in range(4 * n):
            push[s].wait_recv()
            keep[s].wait()
            a, j = divmod(s, 4)
            res = keep_buf.at[s] if j == 0 else pay_buf.at[3 * a + j - 1]

            def add(r, carry, s=s, res=res):
                sl = pl.ds(pl.multiple_of(r * chunk, chunk), chunk)
                res[sl, :] = (keep_buf[s, sl, :] + recv_buf[s, sl, :].astype(F32)).astype(res.dtype)
                return carry

            lax.fori_loop(0, rows // chunk, add, 0)
            out = pltpu.make_async_copy(res, own_refs[a] if j == 0 else pay_refs[a].at[j - 1], out_sems.at[s])
            out.start()
            written.append(out)
        for cp in push:
            cp.wait_send()
        for cp in written:
            cp.wait()

    any_spec = pl.BlockSpec(memory_space=pl.ANY)
    buf = pltpu.VMEM((4 * n, rows, cols), F32)
    buf16 = pltpu.VMEM((4 * n, rows, cols), BF16)
    sems = pltpu.SemaphoreType.DMA((4 * n,))
    outs = pl.pallas_call(
        body, name=name,
        in_specs=[any_spec] * n, out_specs=[any_spec] * (2 * n),
        out_shape=[jax.ShapeDtypeStruct((rows, cols), F32)] * n + [jax.ShapeDtypeStruct((3, rows, cols), BF16)] * n,
        scratch_shapes=[buf, buf16, buf, buf16, pltpu.VMEM((3 * n, rows, cols), BF16),
                        sems, sems, sems, sems, sems],
        compiler_params=pltpu.CompilerParams(vmem_limit_bytes=VMEM_LIMIT),
    )(*gs)
    return list(zip(outs[:n], outs[n:]))


_HBM_SPEC = pl.BlockSpec(memory_space=pltpu.HBM)
_SEM_SPEC = pl.BlockSpec(memory_space=pltpu.SEMAPHORE)
_SIDE_EFFECT = pltpu.SideEffectType.DATAFLOW_SIDE_EFFECTING


def _exchange_start(name, srcs, lands, plan, n_copies):
    nb = len(srcs) + len(lands)

    def body(*refs):
        bufs, send_sems, recv_sems, token = refs[:nb], refs[nb], refs[nb + 1], refs[-1]
        for cp in plan(bufs[:len(srcs)], bufs[len(srcs):], send_sems, recv_sems):
            cp.start()
        token[...] = jnp.zeros_like(token)

    arrays = list(srcs) + list(lands)
    outs = pl.pallas_call(
        body, name=name,
        out_shape=(pltpu.SemaphoreType.DMA((n_copies,)), pltpu.SemaphoreType.DMA((n_copies,)),
                   *[pltpu.HBM(a.shape, a.dtype) for a in arrays], jax.ShapeDtypeStruct((SUBLANES, LANES), F32)),
        in_specs=[_HBM_SPEC] * nb,
        out_specs=(_SEM_SPEC, _SEM_SPEC, *[_HBM_SPEC] * nb, pl.BlockSpec(memory_space=pltpu.VMEM)),
        input_output_aliases={k: 2 + k for k in range(nb)},
        compiler_params=pltpu.CompilerParams(has_side_effects=_SIDE_EFFECT),
    )(*[pltpu.with_memory_space_constraint(a, pltpu.HBM) for a in arrays])
    return outs[0], outs[1], outs[2:2 + len(srcs)], outs[2 + len(srcs):2 + nb], outs[-1]


def _exchange_wait(name, send_sems, recv_sems, srcs, lands, plan, after):
    nb = len(srcs) + len(lands)
    after = list(after)

    def body(*refs):
        bufs, send_ref, recv_ref = refs[:nb], refs[nb], refs[nb + 1]
        for cp in plan(bufs[:len(srcs)], bufs[len(srcs):], send_ref, recv_ref):
            cp.wait_send()
            cp.wait_recv()

    arrays = list(srcs) + list(lands)
    outs = pl.pallas_call(
        body, name=name,
        out_shape=tuple(pltpu.HBM(a.shape, a.dtype) for a in arrays),
        in_specs=[_HBM_SPEC] * nb + [_SEM_SPEC, _SEM_SPEC] + [pl.BlockSpec(memory_space=pl.ANY)] * len(after),
        out_specs=tuple([_HBM_SPEC] * nb),
        input_output_aliases={k: k for k in range(nb)},
        compiler_params=pltpu.CompilerParams(has_side_effects=_SIDE_EFFECT),
    )(*arrays, send_sems, recv_sems, *after)
    return outs[len(srcs):]


def _gather_plan(axes, sizes):
    def plan(src_refs, land_refs, send_sems, recv_sems):
        x, y, c = _mesh_pos()
        copies = []
        for a, (src, land) in enumerate(zip(src_refs, land_refs)):
            mine = _block_of(land, axes[a], 4 * x + 2 * y + c, sizes[a])
            for k in range(1, N_DEV):
                peer = (1 - x if k & 4 else x, 1 - y if k & 2 else y, 1 - c if k & 1 else c)
                idx = a * (N_DEV - 1) + k - 1
                copies.append(pltpu.make_async_remote_copy(
                    src_ref=src, dst_ref=mine, send_sem=send_sems.at[idx], recv_sem=recv_sems.at[idx],
                    device_id=peer, device_id_type=MESH_IDS))
        return copies
    return plan


def _chip_plan(src_refs, land_refs, send_sems, recv_sems):
    x, y, c = _mesh_pos()
    copies = []
    for a, (src, land) in enumerate(zip(src_refs, land_refs)):
        for j, chip in enumerate(_other_chips(x, y)):
            copies.append(pltpu.make_async_remote_copy(
                src_ref=src.at[j], dst_ref=land.at[j], send_sem=send_sems.at[3 * a + j],
                recv_sem=recv_sems.at[3 * a + j], device_id=(*chip, c), device_id_type=MESH_IDS))
    return copies


def _own_block_placed(shard, axis, me):
    if axis == 0:
        full = lax.empty((N_DEV,) + shard.shape, shard.dtype)
        return lax.dynamic_update_slice(full, shard[None], (me,) + (0,) * shard.ndim)
    rows, cols = shard.shape

    def body(me_ref, s_ref, o_ref):
        del me_ref
        o_ref[...] = s_ref[...]

    return pl.pallas_call(
        body, name="place_own_columns",
        grid_spec=pltpu.PrefetchScalarGridSpec(
            num_scalar_prefetch=1, grid=(1,),
            in_specs=[pl.BlockSpec((rows, cols), lambda i, me_ref: (0, 0))],
            out_specs=pl.BlockSpec((rows, cols), lambda i, me_ref: (0, me_ref[0]))),
        out_shape=jax.ShapeDtypeStruct((rows, N_DEV * cols), shard.dtype),
    )(jnp.reshape(me, (1,)).astype(jnp.int32), shard)


def _local_step(x, target, mod, win, late_weights, p, grads_ready=None):
    shift1, scale1, gate1, shift2, scale2, gate2 = ((mod, k) for k in range(6))

    def after_token(v, token):
        return v if token is None else v + token[0:1, 0:1]
    wa, wx = p["lru_w_a"].astype(BF16), p["lru_w_x"].astype(BF16)
    mask = jnp.tril(jnp.ones((HD, HD), F32))
    wm = (p["sgu_w_s"] * mask).astype(BF16)
    wmt = jnp.swapaxes(wm, 1, 2)
    bst = jnp.transpose(p["sgu_b_s"])

    h1, z = _norm_proj(x, p["norm_mix_g"], scale1, shift1, win, "mix_proj")
    hstate, ya_pre, *rnn_saved = _rnn_fwd(
        z, p["rnn_conv_w"], p["rnn_conv_b"], wa, p["lru_b_a"], wx, p["lru_b_x"], p["lru_lambda"])
    yb_pre, *sgu_saved = _sgu_fwd(z, p["sgu_ln_g"], p["sgu_ln_b"], wm, bst)
    wba, wbb, wout = late_weights("merge", [ya_pre, yb_pre])
    x2, ya, yb, merged, o1 = _merge_fwd(ya_pre, yb_pre, z, x, gate1, wba, wbb, wout)
    wup = late_weights("ffn_up", [x2])
    h2, up_a, up_v, ff, fa, fv = _ffn_proj_mid(
        x2, p["norm_ffn_g"], scale2, shift2, wup, p["ffn_conv_w"], p["ffn_conv_b"])
    wd = late_weights("ffn_down", [ff])
    dx3, do2, loss, d_gfin, d_gate2 = _ffn_out_loss(ff, wd, x2, target, gate2, p["norm_final_g"])

    dact, dval, d_wd, dcb_a, dcb_v = _ffn_down_bwd(do2, ff, fa, fv, wd)
    dup, dx2, do1, d_cwf, d_shift2, d_scale2, d_gffn, d_gate1 = _ffn_up_bwd(
        dact, dval, up_a, up_v, p["ffn_conv_w"], wup, x2, dx3, p["norm_ffn_g"], scale2, o1, gate1)
    d_wup = _xt_y(h2, dup, "w_up_grad")
    ready = grads_ready if grads_ready else (lambda stage, big, small: None)
    token = ready("ffn", {"w_up": d_wup, "w_down": d_wd}, {})

    dya, dyb, dz, d_wout, d_win = _out_bwd(do1, wout, merged, ya, yb, z, h1)
    dz, d_win, d_wba, d_cw, d_cb, d_wa, d_ba, d_wx, d_bx, d_lam = _rnn_bwd(
        dya, ya_pre, wba, h1, z, rnn_saved, hstate, dz, d_win, p["rnn_conv_w"], wa, wx,
        after_token(p["lru_lambda"], token))
    small = {
        "rnn_conv_w": d_cw, "rnn_conv_b": d_cb, "lru_w_a": d_wa, "lru_b_a": d_ba, "lru_w_x": d_wx, "lru_b_x": d_bx,
        "lru_lambda": d_lam, "norm_ffn_g": d_gffn, "ffn_conv_w": d_cwf,
        "ffn_conv_b": jnp.concatenate([dcb_a, dcb_v], axis=1), "norm_final_g": d_gfin,
    }
    token = ready("rnn", {}, small)
    dz, d_win, d_wbb, d_ws, d_bst, d_lng, d_lnb = _sgu_bwd(
        dyb, yb_pre, wbb, h1, sgu_saved, dz, d_win, p["sgu_ln_g"], after_token(p["sgu_ln_b"], token), wmt, mask)
    sgu_small = {"sgu_ln_g": d_lng, "sgu_ln_b": d_lnb, "sgu_w_s": d_ws, "sgu_b_s": jnp.transpose(d_bst)}
    mixer = {"w_in": d_win, "w_out": d_wout, "w_branch_a": d_wba, "w_branch_b": d_wbb}
    token = ready("mixer", mixer, sgu_small)
    grad_x, d_shift1, d_scale1, d_gmix = _in_bwd(dz, win, x, dx2, after_token(p["norm_mix_g"], token), scale1)

    small.update(sgu_small)
    small["norm_mix_g"] = d_gmix
    dmod = jnp.stack([d_shift1, d_scale1, d_gate1, d_shift2, d_scale2, d_gate2])
    big = {"w_in": d_win, "w_up": d_wup, "w_branch_a": d_wba, "w_branch_b": d_wbb, "w_out": d_wout, "w_down": d_wd}
    return loss, grad_x, big, small, dmod


LAST_REP = ["b_ada", "norm_mix_g"]
EARLY_REP = ["rnn_conv_b", "lru_w_a", "lru_b_a", "lru_w_x", "lru_b_x", "lru_lambda", "norm_ffn_g", "ffn_conv_b",
             "norm_final_g"]
MID_REP = ["sgu_ln_g", "sgu_ln_b", "sgu_w_s", "sgu_b_s"]
COL_SHARDED = ["rnn_conv_w", "ffn_conv_w"]
SMALL_GROUPS = {"rnn": EARLY_REP + COL_SHARDED, "mixer": MID_REP, "last": LAST_REP}
REPLICATED = LAST_REP + EARLY_REP + MID_REP
SMALL_NAMES = REPLICATED + COL_SHARDED
BIG_NAMES = ["w_in", "w_up", "w_branch_a", "w_branch_b", "w_out", "w_down"]
BIG_AXES = [1, 1, 0, 0, 0, 0]
WEIGHTS = ["w_ada", "b_ada", "norm_mix_g", "w_in", "rnn_conv_w", "rnn_conv_b", "lru_w_a", "lru_b_a", "lru_w_x",
           "lru_b_x", "lru_lambda", "sgu_ln_g", "sgu_ln_b", "sgu_w_s", "sgu_b_s", "w_branch_a", "w_branch_b",
           "w_out", "norm_ffn_g", "w_up", "ffn_conv_w", "ffn_conv_b", "w_down", "norm_final_g"]


def _pack_rows(shape):
    return math.prod(shape) // LANES


def _pack(arrays):
    return jnp.concatenate([a.reshape(-1, LANES) for a in arrays], axis=0)


def kernel(x, c, w_ada, b_ada, norm_mix_g, w_in, rnn_conv_w, rnn_conv_b, lru_w_a, lru_b_a, lru_w_x, lru_b_x, lru_lambda, sgu_ln_g, sgu_ln_b, sgu_w_s, sgu_b_s, w_branch_a, w_branch_b, w_out, norm_ffn_g, w_up, ffn_conv_w, ffn_conv_b, w_down, norm_final_g, loss_target, m_w_ada, m_b_ada, m_norm_mix_g, m_w_in, m_rnn_conv_w, m_rnn_conv_b, m_lru_w_a, m_lru_b_a, m_lru_w_x, m_lru_b_x, m_lru_lambda, m_sgu_ln_g, m_sgu_ln_b, m_sgu_w_s, m_sgu_b_s, m_w_branch_a, m_w_branch_b, m_w_out, m_norm_ffn_g, m_w_up, m_ffn_conv_w, m_ffn_conv_b, m_w_down, m_norm_final_g, v_w_ada, v_b_ada, v_norm_mix_g, v_w_in, v_rnn_conv_w, v_rnn_conv_b, v_lru_w_a, v_lru_b_a, v_lru_w_x, v_lru_b_x, v_lru_lambda, v_sgu_ln_g, v_sgu_ln_b, v_sgu_w_s, v_sgu_b_s, v_w_branch_a, v_w_branch_b, v_w_out, v_norm_ffn_g, v_w_up, v_ffn_conv_w, v_ffn_conv_b, v_w_down, v_norm_final_g):
    given = dict(locals())
    me = 4 * lax.axis_index("x") + 2 * lax.axis_index("y") + lax.axis_index("c")
    ada_cols = w_ada.shape[2]
    conv_cols = {"rnn_conv_w": rnn_conv_w.shape[2], "ffn_conv_w": ffn_conv_w.shape[2]}

    (win, c_all, cw_rnn, cw_ffn), _ = _all_gather(
        [w_in[0].astype(BF16), c.reshape(1, 1, D), rnn_conv_w[0], ffn_conv_w[0]], [1, 0, 1, 1], "gather_first")
    c_all = c_all.reshape(N_DEV, D)

    b_cols = lax.dynamic_slice_in_dim(b_ada, me * ada_cols, ada_cols, axis=1)
    (mod_all,), mod_done = _all_gather(
        [_mod_cols(c_all, w_ada[0], b_cols).reshape(1, N_DEV, ada_cols)], [0], "gather_mod")
    mod_all = mod_all.reshape(N_DEV, N_DEV, ada_cols)
    mod_mine = lax.dynamic_index_in_dim(mod_all, me, axis=1, keepdims=False).reshape(6, 1, D)

    late_groups = {"merge": (["w_branch_a", "w_branch_b", "w_out"], [0, 0, 0]), "ffn_up": (["w_up"], [1]),
                   "ffn_down": (["w_down"], [0])}
    in_flight, started = {}, mod_done[0:1, 0:1]
    for stage, (names, axes) in late_groups.items():
        shards = [(given[n][0] + started).astype(BF16) for n in names]
        plan = _gather_plan(axes, [s.shape[-1] for s in shards])
        send, recv, srcs, lands, token = _exchange_start(
            "gather_start_" + stage, shards, [_own_block_placed(s, ax, me) for s, ax in zip(shards, axes)], plan,
            len(shards) * (N_DEV - 1))
        in_flight[stage] = (send, recv, srcs, lands, plan)
        started = started + token[0:1, 0:1]

    def late_weights(stage, after):
        send, recv, srcs, lands, plan = in_flight[stage]
        full = _exchange_wait("gather_wait_" + stage, send, recv, srcs, lands, plan, after)
        full = [w.reshape(-1, D) if ax == 0 else w for w, ax in zip(full, late_groups[stage][1])]
        return full if len(full) > 1 else full[0]

    mod_mine = mod_mine + started

    reducing, packing = {}, {}

    def start_pack(stage, small):
        pack = _pack([small[n] for n in SMALL_GROUPS[stage]])[None]
        plan = _gather_plan([0], [LANES])
        send, recv, srcs, lands, tok = _exchange_start(
            "small_start_" + stage, [pack], [_own_block_placed(pack, 0, me)], plan, N_DEV - 1)
        packing[stage] = (send, recv, srcs, lands, plan)
        return tok

    def grads_ready(stage, grads, small):
        tokens = [start_pack(stage, small)] if small else []
        if grads:
            tokens.append(start_reduce(stage, grads))
        return sum(tokens[1:], tokens[0])

    def start_reduce(stage, grads):
        names = [n for n in BIG_NAMES if n in grads]
        blocked = {}
        for n in names:
            ax = BIG_AXES[BIG_NAMES.index(n)]
            g = grads[n] if ax == 1 else grads[n].reshape(N_DEV, grads[n].shape[0] // N_DEV, grads[n].shape[1])
            blocked.setdefault((ax, g.shape), []).append((n, g))
        sums = {}
        for (ax, _), group in blocked.items():
            reduced = _sibling_reduce([g for _, g in group], ax, "reduce_sibling_" + "_".join(n for n, _ in group))
            sums.update({n: r for (n, _), r in zip(group, reduced)})
        sums = [sums[n] for n in names]
        pays = [pay for _, pay in sums]
        send, recv, srcs, lands, tok = _exchange_start(
            "reduce_start_" + stage, pays, [lax.empty(p_.shape, p_.dtype) for p_ in pays], _chip_plan, 3 * len(pays))
        reducing[stage] = (names, [own for own, _ in sums], send, recv, srcs, lands)
        return tok

    p = {n: given[n][0] for n in REPLICATED if n not in ("b_ada", "norm_final_g")}
    p = {n: (a.reshape(1, -1) if a.ndim == 1 else a) for n, a in p.items()}
    p["rnn_conv_w"], p["ffn_conv_w"] = cw_rnn, cw_ffn
    p["norm_final_g"] = norm_final_g.reshape(1, D)
    loss, grad_x, _, small, dmod = _local_step(x[0], loss_target[0], mod_mine, win, late_weights, p, grads_ready)

    small["b_ada"] = dmod.reshape(1, 6 * D)
    rows_of = {n: _pack_rows(small[n].shape) for n in SMALL_NAMES}
    (last,), _ = _all_gather([_pack([small[n] for n in LAST_REP])[None]], [0], "gather_small")
    gathered = {"last": last}
    for stage, (send, recv, srcs, lands, plan) in packing.items():
        (gathered[stage],) = _exchange_wait("small_wait_" + stage, send, recv, srcs, lands, plan, [grad_x])
    gathered = {k: v.reshape(N_DEV, -1, LANES) for k, v in gathered.items()}

    out = {}
    for stage, (names, owns, send, recv, srcs, lands) in reducing.items():
        landed = _exchange_wait("reduce_wait_" + stage, send, recv, srcs, lands, _chip_plan, [last])
        for n, own, got in zip(names, owns, landed):
            out[n] = _adamw(given[n][0], given["m_" + n][0], given["v_" + n][0], [own, got], "adamw_" + n)

    dmod_all = gathered["last"][:, :rows_of["b_ada"]].reshape(N_DEV, 6 * D)
    dmod_cols = lax.dynamic_slice_in_dim(dmod_all, me * ada_cols, ada_cols, axis=1)
    out["w_ada"] = _adamw(w_ada[0], m_w_ada[0], v_w_ada[0], [_ada_grad(c_all, dmod_cols)], "adamw_w_ada")

    def rows_form(a):
        return a.reshape(1, -1) if a.size // a.shape[-1] == 1 or a.ndim == 1 else a.reshape(-1, LANES)

    for stage, names in (("last", LAST_REP), ("rnn", EARLY_REP), ("mixer", MID_REP)):
        out.update(_adamw_group(names, *[[rows_form(given[pre + n]) for n in names] for pre in ("", "m_", "v_")],
                                gathered[stage], "adamw_small_" + stage))

    row0 = sum(rows_of[n] for n in EARLY_REP)
    for n in COL_SHARDED:
        full = gathered["rnn"][:, row0:row0 + rows_of[n]].reshape(N_DEV, small[n].shape[0], small[n].shape[1])
        mine = lax.dynamic_slice_in_dim(full, me * conv_cols[n], conv_cols[n], axis=2)
        out[n] = _adamw(given[n][0], given["m_" + n][0], given["v_" + n][0], [mine], "adamw_" + n)
        row0 += rows_of[n]

    total = lax.psum(loss[0, 0], ("x", "y", "c"))
    results = [total, grad_x[None]]
    for kind in range(4):
        results += [out[n][kind].reshape(given[n].shape) for n in WEIGHTS]
    return tuple(results)
```

```python
import math

import jax
import jax.numpy as jnp
from jax import lax
from jax.experimental import pallas as pl
from jax.experimental.pallas import tpu as pltpu

F32 = jnp.float32
BF16 = jnp.bfloat16
MESH_IDS = pl.DeviceIdType.MESH

D = 1024
NH = 8
HD = 128
NCOL_IN = 6 * D
DFF = 3 * D
N_DEV = 8
EPS = 1e-6
LRU_C = 8.0
ADAM_LR, ADAM_B1, ADAM_B2, ADAM_EPS, ADAM_WD, ADAM_STEP = 0.001, 0.9, 0.999, 1e-08, 0.01, 10

SUBLANES = 8
LANES = 128
HALO = 16
VMEM_LIMIT = 56 * 1024 * 1024
GELU_K = math.sqrt(2.0 / math.pi)
GELU_C = 0.044715


def _cparams(n_axes):
    return pltpu.CompilerParams(dimension_semantics=("arbitrary",) * n_axes, vmem_limit_bytes=VMEM_LIMIT)


def _const_spec(shape, single_buffer=False):
    nd = len(shape)
    if single_buffer:
        return pl.BlockSpec(shape, lambda *_: (0,) * nd, pipeline_mode=pl.Buffered(1))
    return pl.BlockSpec(shape, lambda *_: (0,) * nd)


def _vec_operand(v):
    if isinstance(v, tuple):
        stack, k = v
        return stack, pl.BlockSpec((None, 1, D), lambda *_: (k, 0, 0))
    return v, _const_spec((1, D))


def _tile_big(t):
    return min(512, t)


def _tile_seq(t):
    return min(256, t)


def _row_tile(rows, cols):
    cap = max(SUBLANES, (2 * 1024 * 1024) // (4 * cols) // SUBLANES * SUBLANES)
    if rows <= cap:
        return rows
    return next(tr for tr in range(cap, 0, -SUBLANES) if rows % tr == 0)


def _gelu_t(x):
    x2 = x * x
    t = jnp.tanh(x * (GELU_K + (GELU_K * GELU_C) * x2))
    hx = 0.5 * x
    return hx + hx * t, (x2, hx, t)


def _gelu_grad(shared):
    x2, hx, t = shared
    return (0.5 + 0.5 * t) + (hx * (1.0 - t * t)) * (GELU_K + (3.0 * GELU_K * GELU_C) * x2)


def _sigmoid(x):
    return 1.0 / (1.0 + jnp.exp(-x))


def _log_sigmoid(x):
    return -(jnp.maximum(-x, 0.0) + jnp.log1p(jnp.exp(-jnp.abs(x))))


def _row_iota(cols):
    return lax.broadcasted_iota(jnp.int32, (SUBLANES, cols), 0)


def _shift_down(x, k, prev8):
    if k == 0:
        return x
    r = pltpu.roll(x, k, 0)
    p = pltpu.roll(prev8, k, 0)
    head = jnp.where(_row_iota(x.shape[1]) < k, p, r[:SUBLANES])
    return jnp.concatenate([head, r[SUBLANES:]], axis=0)


def _shift_up(x, k, next8):
    if k == 0:
        return x
    n = x.shape[0]
    r = pltpu.roll(x, n - k, 0)
    q = pltpu.roll(next8, SUBLANES - k, 0)
    tail = jnp.where(_row_iota(x.shape[1]) >= SUBLANES - k, q, r[n - SUBLANES:])
    return jnp.concatenate([r[:n - SUBLANES], tail], axis=0)


def _heads_nn(x_bf, w_ref):
    return jnp.concatenate(
        [jnp.dot(x_bf[:, h * HD:(h + 1) * HD], w_ref[h], preferred_element_type=F32) for h in range(NH)], axis=1)


def _heads_nt(x_bf, w_ref):
    return jnp.concatenate(
        [lax.dot_general(x_bf[:, h * HD:(h + 1) * HD], w_ref[h], (((1,), (1,)), ((), ())), preferred_element_type=F32)
         for h in range(NH)], axis=1)


def _dot_nt(a, b):
    return lax.dot_general(a, b, (((1,), (1,)), ((), ())), preferred_element_type=F32)


def _dot_tn(a, b):
    return lax.dot_general(a, b, (((0,), (0,)), ((), ())), preferred_element_type=F32)


def _colsum(x):
    return jnp.sum(x, axis=0, keepdims=True)


def _prev_halo_map(tm, col):
    return lambda i, *_: (jnp.maximum(i * (tm // HALO) - 1, 0), col)


def _norm_proj(x, g, scale, shift, w, name):
    t, n = x.shape[0], w.shape[1]
    tm = _tile_big(t)

    def body(x_ref, g_ref, sc_ref, sh_ref, w_ref, h_ref, z_ref):
        xv = x_ref[...]
        r = lax.rsqrt(jnp.mean(xv * xv, axis=-1, keepdims=True) + EPS)
        hb = ((xv * r * g_ref[...]) * (1.0 + sc_ref[...]) + sh_ref[...]).astype(BF16)
        h_ref[...] = hb
        for c0 in range(0, n, D):
            z_ref[:, c0:c0 + D] = jnp.dot(hb, w_ref[:, c0:c0 + D], preferred_element_type=F32).astype(BF16)

    vec = _const_spec((1, D))
    (scale, sc_spec), (shift, sh_spec) = _vec_operand(scale), _vec_operand(shift)
    return pl.pallas_call(
        body, name=name, grid=(t // tm,),
        in_specs=[pl.BlockSpec((tm, D), lambda i: (i, 0)), vec, sc_spec, sh_spec, _const_spec((D, n), True)],
        out_specs=[pl.BlockSpec((tm, D), lambda i: (i, 0)), pl.BlockSpec((tm, n), lambda i: (i, 0))],
        out_shape=[jax.ShapeDtypeStruct((t, D), BF16), jax.ShapeDtypeStruct((t, n), BF16)],
        compiler_params=_cparams(1),
    )(x, g, scale, shift, w)


def _lru_gates(xc, wa_ref, ba, wx_ref, bx, ls):
    xb = xc.astype(BF16)
    ra = _sigmoid(_heads_nn(xb, wa_ref) + ba)
    ia = _sigmoid(_heads_nn(xb, wx_ref) + bx)
    la = LRU_C * ra * ls
    a = jnp.exp(la)
    mult = jnp.sqrt(-jnp.tanh(la) * (1.0 + a * a))
    return ra, ia, a, mult


def _conv4(xr, prev8, cw_ref, cb):
    return (cb + cw_ref[3:4, :] * xr + cw_ref[2:3, :] * _shift_down(xr, 1, prev8)
            + cw_ref[1:2, :] * _shift_down(xr, 2, prev8) + cw_ref[0:1, :] * _shift_down(xr, 3, prev8))


def _rnn_fwd(z, cw, cb, wa, ba, wx, bx, lam):
    t = z.shape[0]
    tm = _tile_seq(t)
    ngrp = tm // SUBLANES

    def body(xr_ref, xp_ref, gr_ref, cw_ref, cb_ref, wa_ref, ba_ref, wx_ref, bx_ref, lam_ref,
             h_ref, ya_ref, xc_ref, ra_ref, ia_ref, gg_ref, hg_ref, carry_ref, a_scr, u_scr):
        i = pl.program_id(0)

        @pl.when(i == 0)
        def _():
            carry_ref[...] = jnp.zeros_like(carry_ref)

        xr = xr_ref[...].astype(F32)
        prev8 = jnp.where(i == 0, 0.0, xp_ref[...].astype(F32)[HALO - SUBLANES:])
        xc = _conv4(xr, prev8, cw_ref, cb_ref[...])
        ra, ia, a, mult = _lru_gates(xc, wa_ref, ba_ref[...], wx_ref, bx_ref[...], _log_sigmoid(lam_ref[...]))
        xc_ref[...] = xc.astype(BF16)
        ra_ref[...] = ra.astype(BF16)
        ia_ref[...] = ia.astype(BF16)
        a_scr[...] = a
        u_scr[...] = mult * (ia * xc)
        row = _row_iota(D)

        def grp(j, carry):
            r0 = pl.multiple_of(j * SUBLANES, SUBLANES)
            av = a_scr[pl.ds(r0, SUBLANES), :]
            uv = u_scr[pl.ds(r0, SUBLANES), :]
            for d in (1, 2, 4):
                m = row >= d
                uv = jnp.where(m, av * pltpu.roll(uv, d, 0) + uv, uv)
                av = jnp.where(m, av * pltpu.roll(av, d, 0), av)
            hv = uv + av * carry
            h_ref[pl.ds(r0, SUBLANES), :] = hv
            return hv[SUBLANES - 1:SUBLANES, :]

        carry_ref[0:1, :] = lax.fori_loop(0, ngrp, grp, carry_ref[0:1, :])
        grv = gr_ref[...].astype(F32)
        gg, tg = _gelu_t(grv)
        hv = h_ref[...]
        ya_ref[...] = (hv * gg).astype(BF16)
        gg_ref[...] = gg.astype(BF16)
        hg_ref[...] = (hv * _gelu_grad(tg)).astype(BF16)

    vec = _const_spec((1, D))
    wspec = _const_spec((NH, HD, HD))
    tile = pl.BlockSpec((tm, D), lambda i: (i, 0))
    bshape = jax.ShapeDtypeStruct((t, D), BF16)
    return pl.pallas_call(
        body, name="rnn_fwd", grid=(t // tm,),
        in_specs=[tile, pl.BlockSpec((HALO, D), _prev_halo_map(tm, 0)),
                  pl.BlockSpec((tm, D), lambda i: (i, 1)), _const_spec((4, D)), vec, wspec, vec, wspec, vec, vec],
        out_specs=[tile] * 7,
        out_shape=[jax.ShapeDtypeStruct((t, D), F32)] + [bshape] * 6,
        scratch_shapes=[pltpu.VMEM((SUBLANES, D), F32), pltpu.VMEM((tm, D), F32), pltpu.VMEM((tm, D), F32)],
        compiler_params=_cparams(1),
    )(z, z, z, cw, cb, wa, ba, wx, bx, lam)


def _sgu_fwd(z, lng, lnb, wm, bst):
    t = z.shape[0]
    tm = _tile_seq(t)

    def body(zu_ref, zv_ref, lng_ref, lnb_ref, wm_ref, bst_ref, yb_ref, gu_ref, mg_ref, vh_ref, gpv_ref, rstd_ref):
        gu, su = _gelu_t(zu_ref[...].astype(F32))
        gv, sv = _gelu_t(zv_ref[...].astype(F32))
        mu = jnp.mean(gv, axis=-1, keepdims=True)
        cen = gv - mu
        rstd = lax.rsqrt(jnp.mean(cen * cen, axis=-1, keepdims=True) + EPS)
        vhat = cen * rstd
        vb = (vhat * lng_ref[...] + lnb_ref[...]).astype(BF16)
        rows = []
        for b0 in range(0, tm, HD):
            rows.append(jnp.concatenate(
                [jnp.dot(wm_ref[g], vb[b0:b0 + HD, g * HD:(g + 1) * HD], preferred_element_type=F32)
                 + bst_ref[:, g:g + 1] for g in range(NH)], axis=1))
        mixed = jnp.concatenate(rows, axis=0) if len(rows) > 1 else rows[0]
        yb_ref[...] = (gu * mixed).astype(BF16)
        gu_ref[...] = gu.astype(BF16)
        mg_ref[...] = (mixed * _gelu_grad(su)).astype(BF16)
        vh_ref[...] = vhat.astype(BF16)
        gpv_ref[...] = _gelu_grad(sv).astype(BF16)
        rstd_ref[...] = rstd

    vec = _const_spec((1, D))
    tile = pl.BlockSpec((tm, D), lambda i: (i, 0))
    bshape = jax.ShapeDtypeStruct((t, D), BF16)
    return pl.pallas_call(
        body, name="sgu_fwd", grid=(t // tm,),
        in_specs=[pl.BlockSpec((tm, D), lambda i: (i, 2)), pl.BlockSpec((tm, D), lambda i: (i, 3)), vec, vec,
                  _const_spec((NH, HD, HD)), _const_spec((HD, NH))],
        out_specs=[tile] * 5 + [pl.BlockSpec((tm, 1), lambda i: (i, 0))],
        out_shape=[bshape] * 5 + [jax.ShapeDtypeStruct((t, 1), F32)],
        compiler_params=_cparams(1),
    )(z, z, lng, lnb, wm, bst)


def _merge_fwd(ya_pre, yb_pre, z, x, gate1, wba, wbb, wout):
    t = x.shape[0]
    tm = _tile_big(t)

    def body(yap_ref, ybp_ref, ga_ref, gb_ref, x_ref, g1_ref, wba_ref, wbb_ref, wo_ref,
             x2_ref, ya_ref, yb_ref, mg_ref, o1_ref):
        ya = jnp.dot(yap_ref[...], wba_ref[...], preferred_element_type=F32)
        yb = jnp.dot(ybp_ref[...], wbb_ref[...], preferred_element_type=F32)
        merged = _sigmoid(ga_ref[...].astype(F32)) * ya + _sigmoid(gb_ref[...].astype(F32)) * yb
        mb = merged.astype(BF16)
        o1 = jnp.dot(mb, wo_ref[...], preferred_element_type=F32)
        x2_ref[...] = x_ref[...] + g1_ref[...] * o1
        ya_ref[...] = ya.astype(BF16)
        yb_ref[...] = yb.astype(BF16)
        mg_ref[...] = mb
        o1_ref[...] = o1.astype(BF16)

    tile = pl.BlockSpec((tm, D), lambda i: (i, 0))
    wspec = _const_spec((D, D))
    bshape = jax.ShapeDtypeStruct((t, D), BF16)
    gate1, g1_spec = _vec_operand(gate1)
    return pl.pallas_call(
        body, name="merge_fwd", grid=(t // tm,),
        in_specs=[tile, tile, pl.BlockSpec((tm, D), lambda i: (i, 4)), pl.BlockSpec((tm, D), lambda i: (i, 5)),
                  tile, g1_spec, wspec, wspec, wspec],
        out_specs=[tile] * 5,
        out_shape=[jax.ShapeDtypeStruct((t, D), F32), bshape, bshape, bshape, bshape],
        compiler_params=_cparams(1),
    )(ya_pre, yb_pre, z, z, x, gate1, wba, wbb, wout)


def _conv3(u, prev8, cw_ref, cb):
    return cb + cw_ref[2:3, :] * u + cw_ref[1:2, :] * _shift_down(u, 1, prev8) + cw_ref[0:1, :] * _shift_down(u, 2, prev8)


def _ffn_proj_mid(x2, g, scale, shift, w, cw, cb):
    t = x2.shape[0]
    tm = _tile_big(t)
    nc = DFF // D

    def body(x_ref, g_ref, sc_ref, sh_ref, wa_ref, wv_ref, cwa_ref, cwv_ref, cba_ref, cbv_ref,
             h_ref, upa_ref, upv_ref, ff_ref, fa_ref, fv_ref, hb_scr, prev_ref):
        i, c = pl.program_id(0), pl.program_id(1)

        @pl.when(i == 0)
        def _():
            prev_ref[c] = jnp.zeros((2, SUBLANES, D), F32)

        @pl.when(c == 0)
        def _():
            xv = x_ref[...]
            r = lax.rsqrt(jnp.mean(xv * xv, axis=-1, keepdims=True) + EPS)
            hb_scr[...] = ((xv * r * g_ref[...]) * (1.0 + sc_ref[...]) + sh_ref[...]).astype(BF16)
            h_ref[...] = hb_scr[...]

        hb = hb_scr[...]
        halves = []
        for s, (w_ref, up_ref, cw_ref, cb_ref) in enumerate(((wa_ref, upa_ref, cwa_ref, cba_ref),
                                                             (wv_ref, upv_ref, cwv_ref, cbv_ref))):
            u = jnp.dot(hb, w_ref[...], preferred_element_type=F32)
            up_ref[...] = u.astype(BF16)
            halves.append(_conv3(u, prev_ref[c, s], cw_ref, cb_ref[...]))
            prev_ref[c, s] = u[tm - SUBLANES:]
        act, val = halves
        ga, ta = _gelu_t(act)
        ff_ref[...] = (ga * val).astype(BF16)
        fa_ref[...] = (val * _gelu_grad(ta)).astype(BF16)
        fv_ref[...] = ga.astype(BF16)

    def cols(rows, off):
        return pl.BlockSpec((rows, D), lambda i, c: (0, off + c))

    vec = pl.BlockSpec((1, D), lambda i, c: (0, 0))
    row_tile = pl.BlockSpec((tm, D), lambda i, c: (i, 0))
    chunk = pl.BlockSpec((tm, D), lambda i, c: (i, c))
    hshape = jax.ShapeDtypeStruct((t, DFF), BF16)
    (scale, sc_spec), (shift, sh_spec) = _vec_operand(scale), _vec_operand(shift)
    return pl.pallas_call(
        body, name="ffn_proj_mid", grid=(t // tm, nc),
        in_specs=[row_tile, vec, sc_spec, sh_spec, cols(D, 0), cols(D, nc), cols(3, 0), cols(3, nc), cols(1, 0), cols(1, nc)],
        out_specs=[row_tile, chunk, chunk, chunk, chunk, chunk],
        out_shape=[jax.ShapeDtypeStruct((t, D), BF16), hshape, hshape, hshape, hshape, hshape],
        scratch_shapes=[pltpu.VMEM((tm, D), BF16), pltpu.VMEM((nc, 2, SUBLANES, D), F32)],
        compiler_params=_cparams(2),
    )(x2, g, scale, shift, w, w, cw, cw, cb, cb)


def _ffn_out_loss(ff, wd, x2, target, gate2, gfin):
    t = x2.shape[0]
    tm = _tile_big(t)

    def body(ff_ref, wd_ref, x2_ref, tg_ref, g2_ref, gf_ref, dx3_ref, do2_ref, loss_ref, dgf_ref, dg2_ref):
        @pl.when(pl.program_id(0) == 0)
        def _():
            loss_ref[...] = jnp.zeros_like(loss_ref)
            dgf_ref[...] = jnp.zeros_like(dgf_ref)
            dg2_ref[...] = jnp.zeros_like(dg2_ref)

        o2 = jnp.dot(ff_ref[...], wd_ref[...], preferred_element_type=F32)
        x3 = x2_ref[...] + g2_ref[...] * o2
        r = lax.rsqrt(jnp.mean(x3 * x3, axis=-1, keepdims=True) + EPS)
        xhat = x3 * r
        err = xhat * gf_ref[...] - tg_ref[...]
        loss_ref[...] += 0.5 * jnp.sum(jnp.mean(err * err, axis=-1, keepdims=True), axis=0, keepdims=True)
        dy = err * (1.0 / D)
        dgf_ref[...] += _colsum(dy * xhat)
        dxh = dy * gf_ref[...]
        dx3 = r * (dxh - xhat * jnp.mean(dxh * xhat, axis=-1, keepdims=True))
        dx3_ref[...] = dx3
        do2_ref[...] = (dx3 * g2_ref[...]).astype(BF16)
        dg2_ref[...] += _colsum(dx3 * o2)

    tile = pl.BlockSpec((tm, D), lambda i: (i, 0))
    vec = _const_spec((1, D))
    gate2, g2_spec = _vec_operand(gate2)
    return pl.pallas_call(
        body, name="ffn_out_loss", grid=(t // tm,),
        in_specs=[pl.BlockSpec((tm, DFF), lambda i: (i, 0)), _const_spec((DFF, D), True), tile, tile, g2_spec, vec],
        out_specs=[tile, tile, _const_spec((1, 1)), vec, vec],
        out_shape=[jax.ShapeDtypeStruct((t, D), F32), jax.ShapeDtypeStruct((t, D), BF16),
                   jax.ShapeDtypeStruct((1, 1), F32),
                   jax.ShapeDtypeStruct((1, D), F32), jax.ShapeDtypeStruct((1, D), F32)],
        compiler_params=_cparams(1),
    )(ff, wd, x2, target, gate2, gfin)


def _ffn_down_bwd(do2, ff, fa, fv, wd):
    t = do2.shape[0]
    tm = min(1024, t)
    nc = DFF // D

    def body(do2_ref, ff_ref, fa_ref, fv_ref, wd_ref, da_ref, dv_ref, dwd_ref, dcba_ref, dcbv_ref):
        @pl.when(pl.program_id(1) == 0)
        def _():
            for r in (dwd_ref, dcba_ref, dcbv_ref):
                r[...] = jnp.zeros_like(r)

        do2 = do2_ref[...]
        dwd_ref[...] += _dot_tn(ff_ref[...], do2)
        dff = _dot_nt(do2, wd_ref[...])
        dact = dff * fa_ref[...].astype(F32)
        dval = dff * fv_ref[...].astype(F32)
        da_ref[...] = dact.astype(BF16)
        dv_ref[...] = dval.astype(BF16)
        dcba_ref[...] += _colsum(dact)
        dcbv_ref[...] += _colsum(dval)

    blk = pl.BlockSpec((tm, D), lambda c, i: (i, c))
    vec = pl.BlockSpec((1, D), lambda c, i: (0, c))
    return pl.pallas_call(
        body, name="ffn_down_bwd", grid=(nc, t // tm),
        in_specs=[pl.BlockSpec((tm, D), lambda c, i: (i, 0)),
                  blk, blk, blk, pl.BlockSpec((D, D), lambda c, i: (c, 0))],
        out_specs=[blk, blk, pl.BlockSpec((D, D), lambda c, i: (c, 0)), vec, vec],
        out_shape=[jax.ShapeDtypeStruct((t, DFF), BF16), jax.ShapeDtypeStruct((t, DFF), BF16),
                   jax.ShapeDtypeStruct((DFF, D), F32),
                   jax.ShapeDtypeStruct((1, DFF), F32), jax.ShapeDtypeStruct((1, DFF), F32)],
        compiler_params=_cparams(2),
    )(do2, ff, fa, fv, wd)


def _modnorm_bwd(dh, xv, g, scale):
    r = lax.rsqrt(jnp.mean(xv * xv, axis=-1, keepdims=True) + EPS)
    xhat = xv * r
    dxn = dh * (1.0 + scale)
    dxh = dxn * g
    dx = r * (dxh - xhat * jnp.mean(dxh * xhat, axis=-1, keepdims=True))
    return dx, _colsum(dh), _colsum(dh * (xhat * g)), _colsum(dxn * xhat)


def _ffn_up_bwd(dact, dval, up_a, up_v, cw, wup, x2, dx3, gffn, scale2, o1, gate1):
    t = x2.shape[0]
    tm = _tile_seq(t)
    nt = t // tm
    nc = DFF // D

    def body(da_ref, dan_ref, dv_ref, dvn_ref, ua_ref, uv_ref, cw_ref, w_ref, x2_ref, dx3_ref, g_ref, sc_ref, o1_ref, g1_ref,
             dup_ref, dx2_ref, do1_ref, dcw_ref, dsh_ref, dsc_ref, dg_ref, dg1_ref):
        i = pl.program_id(0)

        @pl.when(i == 0)
        def _():
            for r in (dcw_ref, dsh_ref, dsc_ref, dg_ref, dg1_ref):
                r[...] = jnp.zeros_like(r)

        last = i == nt - 1
        dh = jnp.zeros((tm, D), F32)
        for half, (d_ref, dn_ref, u_ref) in enumerate(((da_ref, dan_ref, ua_ref), (dv_ref, dvn_ref, uv_ref))):
            nxt = jnp.where(last, 0.0, dn_ref[...].astype(F32)[:SUBLANES])
            for c in range(nc):
                c0 = half * DFF + c * D
                dv = d_ref[:, c * D:(c + 1) * D].astype(F32)
                nx = nxt[:, c * D:(c + 1) * D]
                taps = (_shift_up(dv, 2, nx), _shift_up(dv, 1, nx), dv)
                dup = (cw_ref[2:3, c0:c0 + D] * taps[2] + cw_ref[1:2, c0:c0 + D] * taps[1]
                       + cw_ref[0:1, c0:c0 + D] * taps[0]).astype(BF16)
                upv = u_ref[:, c * D:(c + 1) * D].astype(F32)
                for k in range(3):
                    dcw_ref[k:k + 1, c0:c0 + D] += _colsum(taps[k] * upv)
                dup_ref[:, c0:c0 + D] = dup
                dh = dh + _dot_nt(dup, w_ref[:, c0:c0 + D])
        dxn, dsh, dsc, dg = _modnorm_bwd(dh, x2_ref[...], g_ref[...], sc_ref[...])
        dx2 = dx3_ref[...] + dxn
        dx2_ref[...] = dx2
        do1_ref[...] = (dx2 * g1_ref[...]).astype(BF16)
        dsh_ref[...] += dsh
        dsc_ref[...] += dsc
        dg_ref[...] += dg
        dg1_ref[...] += _colsum(dx2 * o1_ref[...].astype(F32))

    tile = pl.BlockSpec((tm, D), lambda i: (i, 0))
    wide = pl.BlockSpec((tm, DFF), lambda i: (i, 0))
    nxt = pl.BlockSpec((HALO, DFF), lambda i: (jnp.minimum((i + 1) * (tm // HALO), t // HALO - 1), 0))
    vec = _const_spec((1, D))
    vshape = jax.ShapeDtypeStruct((1, D), F32)
    (scale2, sc_spec), (gate1, g1_spec) = _vec_operand(scale2), _vec_operand(gate1)
    return pl.pallas_call(
        body, name="ffn_up_bwd", grid=(nt,),
        in_specs=[wide, nxt, wide, nxt, wide, wide,
                  _const_spec((3, 2 * DFF)), _const_spec((D, 2 * DFF), True),
                  tile, tile, vec, sc_spec, tile, g1_spec],
        out_specs=[pl.BlockSpec((tm, 2 * DFF), lambda i: (i, 0)), tile, tile, _const_spec((3, 2 * DFF)),
                   vec, vec, vec, vec],
        out_shape=[jax.ShapeDtypeStruct((t, 2 * DFF), BF16), jax.ShapeDtypeStruct((t, D), F32),
                   jax.ShapeDtypeStruct((t, D), BF16), jax.ShapeDtypeStruct((3, 2 * DFF), F32),
                   vshape, vshape, vshape, vshape],
        compiler_params=_cparams(1),
    )(dact, dact, dval, dval, up_a, up_v, cw, wup, x2, dx3, gffn, scale2, o1, gate1)


def _xt_y(a, b, name):
    t, k = a.shape
    n = b.shape[1]
    tm = min(1024, t)
    bn = 3072 if n % 3072 == 0 else D

    def body(a_ref, b_ref, o_ref):
        @pl.when(pl.program_id(1) == 0)
        def _():
            o_ref[...] = jnp.zeros_like(o_ref)

        o_ref[...] += _dot_tn(a_ref[...], b_ref[...])

    return pl.pallas_call(
        body, name=name, grid=(n // bn, t // tm),
        in_specs=[pl.BlockSpec((tm, k), lambda j, i: (i, 0)), pl.BlockSpec((tm, bn), lambda j, i: (i, j))],
        out_specs=pl.BlockSpec((k, bn), lambda j, i: (0, j)),
        out_shape=jax.ShapeDtypeStruct((k, n), F32),
        compiler_params=_cparams(2),
    )(a, b)


def _acc_spec(shape, index):
    return pl.BlockSpec(shape, lambda *_: index, pipeline_mode=pl.Buffered(1))


def _out_bwd(do1, wout, merged, ya, yb, z, h1):
    t = do1.shape[0]
    tm = _tile_big(t)

    def body(do1_ref, wo_ref, mg_ref, ya_ref, yb_ref, ga_ref, gb_ref, h1_ref,
             dya_ref, dyb_ref, dz_ref, dwo_ref, dwin_ref):
        @pl.when(pl.program_id(0) == 0)
        def _():
            dwo_ref[...] = jnp.zeros_like(dwo_ref)
            dwin_ref[...] = jnp.zeros_like(dwin_ref)

        do1v = do1_ref[...]
        dwo_ref[...] += _dot_tn(mg_ref[...], do1v)
        dm = _dot_nt(do1v, wo_ref[...])
        sa = _sigmoid(ga_ref[...].astype(F32))
        sb = _sigmoid(gb_ref[...].astype(F32))
        dya_ref[...] = (dm * sa).astype(BF16)
        dyb_ref[...] = (dm * sb).astype(BF16)
        dga = (dm * ya_ref[...].astype(F32) * sa * (1.0 - sa)).astype(BF16)
        dgb = (dm * yb_ref[...].astype(F32) * sb * (1.0 - sb)).astype(BF16)
        dz_ref[:, 0:D] = dga
        dz_ref[:, D:2 * D] = dgb
        h1v = h1_ref[...]
        dwin_ref[:, 0:D] += _dot_tn(h1v, dga)
        dwin_ref[:, D:2 * D] += _dot_tn(h1v, dgb)

    tile = pl.BlockSpec((tm, D), lambda i: (i, 0))
    bshape = jax.ShapeDtypeStruct((t, D), BF16)
    return pl.pallas_call(
        body, name="out_bwd", grid=(t // tm,),
        in_specs=[tile, _const_spec((D, D), True), tile, tile, tile,
                  pl.BlockSpec((tm, D), lambda i: (i, 4)), pl.BlockSpec((tm, D), lambda i: (i, 5)), tile],
        out_specs=[tile, tile, pl.BlockSpec((tm, 2 * D), lambda i: (i, 2)), _acc_spec((D, D), (0, 0)),
                   _acc_spec((D, 2 * D), (0, 2))],
        out_shape=[bshape, bshape, jax.ShapeDtypeStruct((t, NCOL_IN), BF16), jax.ShapeDtypeStruct((D, D), F32),
                   jax.ShapeDtypeStruct((D, NCOL_IN), F32)],
        compiler_params=_cparams(1),
    )(do1, wout, merged, ya, yb, z, z, h1)


def _rnn_bwd(dya, ya_pre, wba, h1, z, saved, h, dz, dwin, cw, wa, wx, lam):
    t = z.shape[0]
    tm = _tile_seq(t)
    nt = t // tm
    ngrp = tm // SUBLANES
    hpt = tm // HALO

    def body(dya_ref, yap_ref, wba_ref, h1_ref, xr_ref, xc_ref, ra_ref, ia_ref, gg_ref, hg_ref, h_ref, hp_ref,
             dz_any, dwin_any, cw_ref, wa_ref, wx_ref, lam_ref,
             dz_ref, dwin_ref, dwba_ref, dcw_ref, dcb_ref, dwa_ref, dba_ref, dwx_ref, dbx_ref, dlam_ref,
             a_first, g_first, dxc_first, b_scr, d_scr, g_scr):
        del dz_any, dwin_any
        i = pl.program_id(0)

        @pl.when(i == 0)
        def _():
            for r in (dwin_ref, dwba_ref, dcw_ref, dcb_ref, dwa_ref, dba_ref, dwx_ref, dbx_ref, dlam_ref,
                      a_first, g_first, dxc_first):
                r[...] = jnp.zeros_like(r)

        dya_v = dya_ref[...]
        dwba_ref[...] += _dot_tn(yap_ref[...], dya_v)
        dyap_v = _dot_nt(dya_v, wba_ref[...])
        h1v = h1_ref[...]

        first_tile = i == nt - 1
        xc = xc_ref[...].astype(F32)
        ra = ra_ref[...].astype(F32)
        ia = ia_ref[...].astype(F32)
        lam_v = lam_ref[...]
        ls = _log_sigmoid(lam_v)
        la = LRU_C * ra * ls
        a = jnp.exp(la)
        mult = jnp.sqrt(-jnp.tanh(la) * (1.0 + a * a))
        hprev8 = jnp.where(first_tile, 0.0, hp_ref[...][HALO - SUBLANES:])
        h_prev = _shift_down(h_ref[...], 1, hprev8)
        dgr = (dyap_v * hg_ref[...].astype(F32)).astype(BF16)
        dz_ref[:, D:2 * D] = dgr
        dwin_ref[:, D:2 * D] += _dot_tn(h1v, dgr)

        b_scr[...] = _shift_up(a, 1, a_first[...])
        d_scr[...] = dyap_v * gg_ref[...].astype(F32)
        row = _row_iota(D)

        def grp(jj, carry):
            r0 = pl.multiple_of((ngrp - 1 - jj) * SUBLANES, SUBLANES)
            bv = b_scr[pl.ds(r0, SUBLANES), :]
            dv = d_scr[pl.ds(r0, SUBLANES), :]
            for d in (1, 2, 4):
                m = row < SUBLANES - d
                dv = jnp.where(m, dv + bv * pltpu.roll(dv, SUBLANES - d, 0), dv)
                bv = jnp.where(m, bv * pltpu.roll(bv, SUBLANES - d, 0), bv)
            gv = dv + bv * carry
            g_scr[pl.ds(r0, SUBLANES), :] = gv
            return gv[0:1, :]

        lax.fori_loop(0, ngrp, grp, g_first[0:1, :])
        g = g_scr[...]
        a_first[...] = a[:SUBLANES]
        g_first[...] = g[:SUBLANES]

        da = g * h_prev
        gx = g * xc
        dmult = gx * ia
        dia = gx * mult
        dxc = g * (mult * ia)
        dla = da * a - dmult * (a * a) / mult
        dra = dla * (LRU_C * ls)
        dlam_ref[...] += _colsum(dla * ra) * (LRU_C * _sigmoid(-lam_v))
        dpa = dra * ra * (1.0 - ra)
        dpx = dia * ia * (1.0 - ia)
        dba_ref[...] += _colsum(dpa)
        dbx_ref[...] += _colsum(dpx)
        dpab = dpa.astype(BF16)
        dpxb = dpx.astype(BF16)
        xcb = xc_ref[...]
        for hd in range(NH):
            sl = slice(hd * HD, (hd + 1) * HD)
            dwa_ref[hd] += _dot_tn(xcb[:, sl], dpab[:, sl])
            dwx_ref[hd] += _dot_tn(xcb[:, sl], dpxb[:, sl])
        dxc = dxc + _heads_nt(dpab, wa_ref) + _heads_nt(dpxb, wx_ref)

        nxt = dxc_first[...]
        taps = (_shift_up(dxc, 3, nxt), _shift_up(dxc, 2, nxt), _shift_up(dxc, 1, nxt), dxc)
        dxr = cw_ref[0:1, :] * taps[0]
        for k in range(1, 4):
            dxr = dxr + cw_ref[k:k + 1, :] * taps[k]
        dxrb = dxr.astype(BF16)
        dz_ref[:, 0:D] = dxrb
        dwin_ref[:, 0:D] += _dot_tn(h1v, dxrb)
        dxc_first[...] = dxc[:SUBLANES]
        dcb_ref[...] += _colsum(dxc)
        xr = xr_ref[...].astype(F32)
        for k in range(4):
            dcw_ref[k:k + 1, :] += _colsum(taps[k] * xr)

    def rev(col):
        return lambda i: (nt - 1 - i, col)

    vec = _const_spec((1, D))
    wspec = _const_spec((NH, HD, HD))
    vshape = jax.ShapeDtypeStruct((1, D), F32)
    wshape = jax.ShapeDtypeStruct((NH, HD, HD), F32)
    any_spec = pl.BlockSpec(memory_space=pl.ANY)
    tile = pl.BlockSpec((tm, D), rev(0))
    outs = pl.pallas_call(
        body, name="rnn_bwd", grid=(nt,),
        in_specs=[tile, tile, _const_spec((D, D), True), tile, tile, tile, tile, tile, tile, tile, tile,
                  pl.BlockSpec((HALO, D), lambda i: (jnp.maximum((nt - 1 - i) * hpt - 1, 0), 0)),
                  any_spec, any_spec, _const_spec((4, D)), wspec, wspec, vec],
        out_specs=[pl.BlockSpec((tm, 2 * D), rev(0)), _acc_spec((D, 2 * D), (0, 0)), _acc_spec((D, D), (0, 0)),
                   _const_spec((4, D)), vec, wspec, vec, wspec, vec, vec],
        out_shape=[jax.ShapeDtypeStruct((t, NCOL_IN), BF16), jax.ShapeDtypeStruct((D, NCOL_IN), F32),
                   jax.ShapeDtypeStruct((D, D), F32), jax.ShapeDtypeStruct((4, D), F32), vshape,
                   wshape, vshape, wshape, vshape, vshape],
        scratch_shapes=[pltpu.VMEM((SUBLANES, D), F32), pltpu.VMEM((SUBLANES, D), F32), pltpu.VMEM((SUBLANES, D), F32),
                        pltpu.VMEM((tm, D), F32), pltpu.VMEM((tm, D), F32), pltpu.VMEM((tm, D), F32)],
        input_output_aliases={12: 0, 13: 1},
        compiler_params=_cparams(1),
    )(dya, ya_pre, wba, h1, z, *saved, h, h, dz, dwin, cw, wa, wx, lam)
    return outs


def _sgu_bwd(dyb, yb_pre, wbb, h1, saved, dz, dwin, lng, lnb, wmt, mask):
    t = dyb.shape[0]
    tm = _tile_big(t)

    def body(dyb_ref, ybp_ref, wbb_ref, h1_ref, gu_ref, mg_ref, vh_ref, gpv_ref, rstd_ref, dz_any, dwin_any,
             lng_ref, lnb_ref, wmt_ref, mask_ref,
             dz_ref, dwin_ref, dwbb_ref, dws_ref, dbst_ref, dlng_ref, dlnb_ref):
        del dz_any, dwin_any

        @pl.when(pl.program_id(0) == 0)
        def _():
            for r in (dwin_ref, dwbb_ref, dws_ref, dbst_ref, dlng_ref, dlnb_ref):
                r[...] = jnp.zeros_like(r)

        lng_v = lng_ref[...]
        vhat = vh_ref[...].astype(F32)
        vb = (vhat * lng_v + lnb_ref[...]).astype(BF16)
        rstd = rstd_ref[...]
        dyb_v = dyb_ref[...]
        dwbb_ref[...] += _dot_tn(ybp_ref[...], dyb_v)
        dyb = _dot_nt(dyb_v, wbb_ref[...])
        h1v = h1_ref[...]
        dzu = (dyb * mg_ref[...].astype(F32)).astype(BF16)
        dz_ref[:, 0:D] = dzu
        dwin_ref[:, 0:D] += _dot_tn(h1v, dzu)
        dmix = dyb * gu_ref[...].astype(F32)
        dmb = dmix.astype(BF16)
        rows = []
        lane = lax.broadcasted_iota(jnp.int32, (HD, NH), 1)
        dbst = jnp.zeros((HD, NH), F32)
        for b0 in range(0, tm, HD):
            cols = []
            for g in range(NH):
                sl = slice(g * HD, (g + 1) * HD)
                dmg = dmb[b0:b0 + HD, sl]
                dws_ref[g] += _dot_nt(dmg, vb[b0:b0 + HD, sl]) * mask_ref[...]
                cols.append(jnp.dot(wmt_ref[g], dmg, preferred_element_type=F32))
                dbst = dbst + jnp.where(lane == g, jnp.sum(dmix[b0:b0 + HD, sl], axis=1, keepdims=True), 0.0)
            rows.append(jnp.concatenate(cols, axis=1))
        dbst_ref[...] += dbst
        dvln = jnp.concatenate(rows, axis=0) if len(rows) > 1 else rows[0]
        dlng_ref[...] += _colsum(dvln * vhat)
        dlnb_ref[...] += _colsum(dvln)
        dvh = dvln * lng_v
        dgv = rstd * (dvh - jnp.mean(dvh, axis=-1, keepdims=True)
                      - vhat * jnp.mean(dvh * vhat, axis=-1, keepdims=True))
        dzv = (dgv * gpv_ref[...].astype(F32)).astype(BF16)
        dz_ref[:, D:2 * D] = dzv
        dwin_ref[:, D:2 * D] += _dot_tn(h1v, dzv)

    vec = _const_spec((1, D))
    wspec = _const_spec((NH, HD, HD))
    vshape = jax.ShapeDtypeStruct((1, D), F32)
    tile = pl.BlockSpec((tm, D), lambda i: (i, 0))
    any_spec = pl.BlockSpec(memory_space=pl.ANY)
    return pl.pallas_call(
        body, name="sgu_bwd", grid=(t // tm,),
        in_specs=[tile, tile, _const_spec((D, D), True), tile, tile, tile, tile, tile,
                  pl.BlockSpec((tm, 1), lambda i: (i, 0)), any_spec, any_spec,
                  vec, vec, wspec, _const_spec((HD, HD))],
        out_specs=[pl.BlockSpec((tm, 2 * D), lambda i: (i, 1)), _acc_spec((D, 2 * D), (0, 1)), _acc_spec((D, D), (0, 0)),
                   wspec, _const_spec((HD, NH)), vec, vec],
        out_shape=[jax.ShapeDtypeStruct((t, NCOL_IN), BF16), jax.ShapeDtypeStruct((D, NCOL_IN), F32),
                   jax.ShapeDtypeStruct((D, D), F32), jax.ShapeDtypeStruct((NH, HD, HD), F32),
                   jax.ShapeDtypeStruct((HD, NH), F32), vshape, vshape],
        input_output_aliases={9: 0, 10: 1},
        compiler_params=_cparams(1),
    )(dyb, yb_pre, wbb, h1, *saved, dz, dwin, lng, lnb, wmt, mask)


def _in_bwd(dz, win, x, dx2, g, scale1):
    t = x.shape[0]
    tm = _tile_big(t)

    nt = t // tm

    def body(dz_hbm, w_ref, x_ref, dx2_ref, g_ref, sc_ref, dx_ref, dsh_ref, dsc_ref, dg_ref, ring, ring_sems):
        i = pl.program_id(0)

        def fetch(step):
            slot = lax.rem(step, 3)
            return pltpu.make_async_copy(
                dz_hbm.at[pl.ds(pl.multiple_of(step * tm, tm), tm), :], ring.at[slot], ring_sems.at[slot])

        @pl.when(i == 0)
        def _():
            for r in (dsh_ref, dsc_ref, dg_ref):
                r[...] = jnp.zeros_like(r)
            fetch(0).start()
            if nt > 1:
                fetch(1).start()

        @pl.when(i + 2 < nt)
        def _():
            fetch(i + 2).start()

        fetch(i).wait()
        slot = lax.rem(i, 3)
        dh = jnp.zeros((tm, D), F32)
        for c0 in range(0, NCOL_IN, D):
            dh = dh + _dot_nt(ring[slot, :, c0:c0 + D], w_ref[:, c0:c0 + D])
        dxn, dsh, dsc, dg = _modnorm_bwd(dh, x_ref[...], g_ref[...], sc_ref[...])
        dx_ref[...] = dx2_ref[...] + dxn
        dsh_ref[...] += dsh
        dsc_ref[...] += dsc
        dg_ref[...] += dg

    tile = pl.BlockSpec((tm, D), lambda i: (i, 0))
    vec = _const_spec((1, D))
    vshape = jax.ShapeDtypeStruct((1, D), F32)
    scale1, sc_spec = _vec_operand(scale1)
    return pl.pallas_call(
        body, name="in_bwd", grid=(t // tm,),
        in_specs=[pl.BlockSpec(memory_space=pl.ANY), _const_spec((D, NCOL_IN), True), tile, tile, vec, sc_spec],
        out_specs=[tile, vec, vec, vec],
        out_shape=[jax.ShapeDtypeStruct((t, D), F32), vshape, vshape, vshape],
        scratch_shapes=[pltpu.VMEM((3, tm, NCOL_IN), BF16), pltpu.SemaphoreType.DMA((3,))],
        compiler_params=_cparams(1),
    )(dz, win, x, dx2, g, scale1)


def _mod_cols(c_all, w_ada, b_cols):
    nb, cols = c_all.shape[0], w_ada.shape[1]

    def body(c_ref, w_ref, b_ref, o_ref):
        cv = c_ref[...]
        ca = (cv * _sigmoid(cv)).astype(BF16)
        o_ref[...] = jnp.dot(ca, w_ref[...].astype(BF16), preferred_element_type=F32) + b_ref[...]

    return pl.pallas_call(body, name="mod_cols", out_shape=jax.ShapeDtypeStruct((nb, cols), F32))(c_all, w_ada, b_cols)


def _ada_grad(c_all, dmod_cols):
    cols = dmod_cols.shape[1]

    def body(c_ref, d_ref, o_ref):
        cv = c_ref[...]
        ca = (cv * _sigmoid(cv)).astype(BF16)
        o_ref[...] = _dot_tn(ca, d_ref[...].astype(BF16))

    return pl.pallas_call(body, name="ada_grad", out_shape=jax.ShapeDtypeStruct((D, cols), F32))(c_all, dmod_cols)


def _adamw_update(w, m, v, g):
    bc1 = 1.0 - ADAM_B1 ** ADAM_STEP
    bc2 = 1.0 - ADAM_B2 ** ADAM_STEP
    mn = ADAM_B1 * m + (1.0 - ADAM_B1) * g
    vn = ADAM_B2 * v + (1.0 - ADAM_B2) * (g * g)
    return -ADAM_LR * ((mn / bc1) / (jnp.sqrt(vn / bc2) + ADAM_EPS) + ADAM_WD * w), mn, vn


def _adamw_group(names, ws, ms, vs, packs, name):
    n = len(names)
    starts, r0 = [], 0
    for w in ws:
        starts.append(r0)
        r0 += _pack_rows(w.shape)

    def body(*refs):
        w_refs, m_refs, v_refs, p_ref = refs[:n], refs[n:2 * n], refs[2 * n:3 * n], refs[3 * n]
        outs = refs[3 * n + 1:]
        for k in range(n):
            rows = _pack_rows(ws[k].shape)
            g = None
            for dev in range(N_DEV):
                if ws[k].shape[0] == 1:
                    term = jnp.concatenate(
                        [p_ref[dev, starts[k] + r:starts[k] + r + 1, :] for r in range(rows)], axis=1)
                else:
                    term = p_ref[dev, starts[k]:starts[k] + rows, :]
                g = term if g is None else g + term
            delta, mn, vn = _adamw_update(w_refs[k][...], m_refs[k][...], v_refs[k][...], g)
            for o_ref, val in zip(outs[4 * k:4 * k + 4], (g, delta, mn, vn)):
                o_ref[...] = val

    shapes = [jax.ShapeDtypeStruct(w.shape, F32) for w in ws for _ in range(4)]
    outs = pl.pallas_call(body, name=name, out_shape=shapes,
                          compiler_params=pltpu.CompilerParams(vmem_limit_bytes=VMEM_LIMIT))(*ws, *ms, *vs, packs)
    return {nm: tuple(outs[4 * k:4 * k + 4]) for k, nm in enumerate(names)}


def _adamw(w, m, v, parts, name):
    rows, cols = w.shape
    tr = _row_tile(rows, cols)
    stacked = [p.ndim == 3 for p in parts]

    def body(*refs):
        w_ref, m_ref, v_ref = refs[:3]
        p_refs = refs[3:3 + len(parts)]
        g_ref, d_ref, mo_ref, vo_ref = refs[3 + len(parts):]
        g = None
        for p_ref, st in zip(p_refs, stacked):
            terms = [p_ref[k].astype(F32) for k in range(p_ref.shape[0])] if st else [p_ref[...].astype(F32)]
            for term in terms:
                g = term if g is None else g + term
        delta, mn, vn = _adamw_update(w_ref[...], m_ref[...], v_ref[...], g)
        g_ref[...] = g
        mo_ref[...] = mn
        vo_ref[...] = vn
        d_ref[...] = delta

    tile = pl.BlockSpec((tr, cols), lambda i: (i, 0))
    p_specs = [pl.BlockSpec((p.shape[0], tr, cols), lambda i: (0, i, 0)) if st else tile for p, st in zip(parts, stacked)]
    shp = jax.ShapeDtypeStruct((rows, cols), F32)
    return pl.pallas_call(
        body, name=name, grid=(rows // tr,),
        in_specs=[tile, tile, tile] + p_specs, out_specs=[tile] * 4, out_shape=[shp] * 4,
        compiler_params=_cparams(1),
    )(w, m, v, *parts)


def _mesh_pos():
    return lax.axis_index("x"), lax.axis_index("y"), lax.axis_index("c")


def _other_chips(x, y):
    return [(1 - x, y), (x, 1 - y), (1 - x, 1 - y)]


def _block_of(ref, axis, index, size):
    if axis == 0:
        return ref.at[index]
    return ref.at[:, pl.ds(pl.multiple_of(index * size, 128), size)]


def _all_gather(shards, axes, name):
    n = len(shards)
    per = 7

    def body(*refs):
        ins, outs, done = refs[:n], refs[n:2 * n], refs[2 * n]
        send_sems, recv_sems, local_sems = refs[2 * n + 1:]
        x, y, c = _mesh_pos()
        me, sibling = (x, y, c), (x, y, 1 - c)
        chips = _other_chips(x, y)

        def rows(a, pos):
            return _block_of(outs[a], axes[a], 4 * pos[0] + 2 * pos[1] + pos[2], shards[a].shape[-1])

        def copy(a, k, block, to, src=None):
            return pltpu.make_async_remote_copy(
                src_ref=rows(a, block) if src is None else src, dst_ref=rows(a, block),
                send_sem=send_sems.at[a * per + k], recv_sem=recv_sems.at[a * per + k],
                device_id=to, device_id_type=MESH_IDS)

        mine = [pltpu.make_async_copy(ins[a], rows(a, me), local_sems.at[a]) for a in range(n)]
        for cp in mine:
            cp.start()
        first = []
        for a in range(n):
            first.append(copy(a, 0, me, sibling, src=ins[a]))
            first += [copy(a, 1 + j, me, (*chip, c), src=ins[a]) for j, chip in enumerate(chips)]
        for cp in first:
            cp.start()
        passed = []
        for j, chip in enumerate(chips):
            for a in range(n):
                copy(a, 1 + j, (*chip, c), me).wait_recv()
                fwd = copy(a, 4 + j, (*chip, c), sibling)
                fwd.start()
                passed.append(fwd)
        for a in range(n):
            copy(a, 0, sibling, me).wait_recv()
            for j, chip in enumerate(chips):
                copy(a, 4 + j, (*chip, 1 - c), me).wait_recv()
        for cp in first + passed:
            cp.wait_send()
        for cp in mine:
            cp.wait()
        done[...] = jnp.zeros_like(done)

    def full_shape(s, ax):
        return (N_DEV,) + s.shape if ax == 0 else s.shape[:-1] + (N_DEV * s.shape[-1],)

    any_spec = pl.BlockSpec(memory_space=pl.ANY)
    outs = pl.pallas_call(
        body, name=name,
        in_specs=[any_spec] * n, out_specs=[any_spec] * n + [pl.BlockSpec(memory_space=pltpu.VMEM)],
        out_shape=[jax.ShapeDtypeStruct(full_shape(s, ax), s.dtype) for s, ax in zip(shards, axes)]
        + [jax.ShapeDtypeStruct((SUBLANES, LANES), F32)],
        scratch_shapes=[pltpu.SemaphoreType.DMA((n * per,)), pltpu.SemaphoreType.DMA((n * per,)),
                        pltpu.SemaphoreType.DMA((n,))],
    )(*shards)
    return outs[:n], outs[n]


def _chip_blocks(x, y):
    return [(x, y)] + _other_chips(x, y)


def _sibling_reduce(gs, axis, name):
    g0, n = gs[0], len(gs)
    rows, cols = (g0.shape[1], g0.shape[2]) if axis == 0 else (g0.shape[0], g0.shape[1] // N_DEV)
    chunk = math.gcd(rows, 64)

    def body(*refs):
        g_refs, own_refs, pay_refs = refs[:n], refs[n:2 * n], refs[2 * n:3 * n]
        (stage_buf, send_buf, keep_buf, recv_buf, pay_buf,
         send_sems, recv_sems, stage_sems, keep_sems, out_sems) = refs[3 * n:]
        x, y, c = _mesh_pos()
        sibling = (x, y, 1 - c)
        chips = _chip_blocks(x, y)
        stage, keep, push = [], [], []
        for a in range(n):
            for j, (px, py) in enumerate(chips):
                s = 4 * a + j
                theirs = _block_of(g_refs[a], axis, 4 * px + 2 * py + (1 - c), cols)
                ours = _block_of(g_refs[a], axis, 4 * px + 2 * py + c, cols)
                stage.append(pltpu.make_async_copy(theirs, stage_buf.at[s], stage_sems.at[s]))
                keep.append(pltpu.make_async_copy(ours, keep_buf.at[s], keep_sems.at[s]))
                push.append(pltpu.make_async_remote_copy(
                    src_ref=send_buf.at[s], dst_ref=recv_buf.at[s], send_sem=send_sems.at[s],
                    recv_sem=recv_sems.at[s], device_id=sibling, device_id_type=MESH_IDS))
        for cp in stage[:2]:
            cp.start()
        for s in range(4 * n):
            if s + 2 < 4 * n:
                stage[s + 2].start()
            stage[s].wait()
            keep[s].start()

            def narrow(r, carry, s=s):
                sl = pl.ds(pl.multiple_of(r * chunk, chunk), chunk)
                send_buf[s, sl, :] = stage_buf[s, sl, :].astype(BF16)
                return carry

            lax.fori_loop(0, rows // chunk, narrow, 0)
            push[s].start()
        written = []
        for s in range(4 * n):
            push[s].wait_recv()
            keep[s].wait()
            a, j = divmod(s, 4)
            res = keep_buf.at[s] if j == 0 else pay_buf.at[3 * a + j - 1]

            def add(r, carry, s=s, res=res):
                sl = pl.ds(pl.multiple_of(r * chunk, chunk), chunk)
                res[sl, :] = (keep_buf[s, sl, :] + recv_buf[s, sl, :].astype(F32)).astype(res.dtype)
                return carry

            lax.fori_loop(0, rows // chunk, add, 0)
            out = pltpu.make_async_copy(res, own_refs[a] if j == 0 else pay_refs[a].at[j - 1], out_sems.at[s])
            out.start()
            written.append(out)
        for cp in push:
            cp.wait_send()
        for cp in written:
            cp.wait()

    any_spec = pl.BlockSpec(memory_space=pl.ANY)
    buf = pltpu.VMEM((4 * n, rows, cols), F32)
    buf16 = pltpu.VMEM((4 * n, rows, cols), BF16)
    sems = pltpu.SemaphoreType.DMA((4 * n,))
    outs = pl.pallas_call(
        body, name=name,
        in_specs=[any_spec] * n, out_specs=[any_spec] * (2 * n),
        out_shape=[jax.ShapeDtypeStruct((rows, cols), F32)] * n + [jax.ShapeDtypeStruct((3, rows, cols), BF16)] * n,
        scratch_shapes=[buf, buf16, buf, buf16, pltpu.VMEM((3 * n, rows, cols), BF16),
                        sems, sems, sems, sems, sems],
        compiler_params=pltpu.CompilerParams(vmem_limit_bytes=VMEM_LIMIT),
    )(*gs)
    return list(zip(outs[:n], outs[n:]))


_HBM_SPEC = pl.BlockSpec(memory_space=pltpu.HBM)
_SEM_SPEC = pl.BlockSpec(memory_space=pltpu.SEMAPHORE)
_SIDE_EFFECT = pltpu.SideEffectType.DATAFLOW_SIDE_EFFECTING


def _exchange_start(name, srcs, lands, plan, n_copies):
    nb = len(srcs) + len(lands)

    def body(*refs):
        bufs, send_sems, recv_sems, token = refs[:nb], refs[nb], refs[nb + 1], refs[-1]
        for cp in plan(bufs[:len(srcs)], bufs[len(srcs):], send_sems, recv_sems):
            cp.start()
        token[...] = jnp.zeros_like(token)

    arrays = list(srcs) + list(lands)
    outs = pl.pallas_call(
        body, name=name,
        out_shape=(pltpu.SemaphoreType.DMA((n_copies,)), pltpu.SemaphoreType.DMA((n_copies,)),
                   *[pltpu.HBM(a.shape, a.dtype) for a in arrays], jax.ShapeDtypeStruct((SUBLANES, LANES), F32)),
        in_specs=[_HBM_SPEC] * nb,
        out_specs=(_SEM_SPEC, _SEM_SPEC, *[_HBM_SPEC] * nb, pl.BlockSpec(memory_space=pltpu.VMEM)),
        input_output_aliases={k: 2 + k for k in range(nb)},
        compiler_params=pltpu.CompilerParams(has_side_effects=_SIDE_EFFECT),
    )(*[pltpu.with_memory_space_constraint(a, pltpu.HBM) for a in arrays])
    return outs[0], outs[1], outs[2:2 + len(srcs)], outs[2 + len(srcs):2 + nb], outs[-1]


def _exchange_wait(name, send_sems, recv_sems, srcs, lands, plan, after):
    nb = len(srcs) + len(lands)
    after = list(after)

    def body(*refs):
        bufs, send_ref, recv_ref = refs[:nb], refs[nb], refs[nb + 1]
        for cp in plan(bufs[:len(srcs)], bufs[len(srcs):], send_ref, recv_ref):
            cp.wait_send()
            cp.wait_recv()

    arrays = list(srcs) + list(lands)
    outs = pl.pallas_call(
        body, name=name,
        out_shape=tuple(pltpu.HBM(a.shape, a.dtype) for a in arrays),
        in_specs=[_HBM_SPEC] * nb + [_SEM_SPEC, _SEM_SPEC] + [pl.BlockSpec(memory_space=pl.ANY)] * len(after),
        out_specs=tuple([_HBM_SPEC] * nb),
        input_output_aliases={k: k for k in range(nb)},
        compiler_params=pltpu.CompilerParams(has_side_effects=_SIDE_EFFECT),
    )(*arrays, send_sems, recv_sems, *after)
    return outs[len(srcs):]


def _gather_plan(axes, sizes):
    def plan(src_refs, land_refs, send_sems, recv_sems):
        x, y, c = _mesh_pos()
        copies = []
        for a, (src, land) in enumerate(zip(src_refs, land_refs)):
            mine = _block_of(land, axes[a], 4 * x + 2 * y + c, sizes[a])
            for k in range(1, N_DEV):
                peer = (1 - x if k & 4 else x, 1 - y if k & 2 else y, 1 - c if k & 1 else c)
                idx = a * (N_DEV - 1) + k - 1
                copies.append(pltpu.make_async_remote_copy(
                    src_ref=src, dst_ref=mine, send_sem=send_sems.at[idx], recv_sem=recv_sems.at[idx],
                    device_id=peer, device_id_type=MESH_IDS))
        return copies
    return plan


def _chip_plan(src_refs, land_refs, send_sems, recv_sems):
    x, y, c = _mesh_pos()
    copies = []
    for a, (src, land) in enumerate(zip(src_refs, land_refs)):
        for j, chip in enumerate(_other_chips(x, y)):
            copies.append(pltpu.make_async_remote_copy(
                src_ref=src.at[j], dst_ref=land.at[j], send_sem=send_sems.at[3 * a + j],
                recv_sem=recv_sems.at[3 * a + j], device_id=(*chip, c), device_id_type=MESH_IDS))
    return copies


def _own_block_placed(shard, axis, me):
    if axis == 0:
        full = lax.empty((N_DEV,) + shard.shape, shard.dtype)
        return lax.dynamic_update_slice(full, shard[None], (me,) + (0,) * shard.ndim)
    rows, cols = shard.shape

    def body(me_ref, s_ref, o_ref):
        del me_ref
        o_ref[...] = s_ref[...]

    return pl.pallas_call(
        body, name="place_own_columns",
        grid_spec=pltpu.PrefetchScalarGridSpec(
            num_scalar_prefetch=1, grid=(1,),
            in_specs=[pl.BlockSpec((rows, cols), lambda i, me_ref: (0, 0))],
            out_specs=pl.BlockSpec((rows, cols), lambda i, me_ref: (0, me_ref[0]))),
        out_shape=jax.ShapeDtypeStruct((rows, N_DEV * cols), shard.dtype),
    )(jnp.reshape(me, (1,)).astype(jnp.int32), shard)


def _local_step(x, target, mod, win, late_weights, p, grads_ready=None):
    shift1, scale1, gate1, shift2, scale2, gate2 = ((mod, k) for k in range(6))

    def after_token(v, token):
        return v if token is None else v + token[0:1, 0:1]
    wa, wx = p["lru_w_a"].astype(BF16), p["lru_w_x"].astype(BF16)
    mask = jnp.tril(jnp.ones((HD, HD), F32))
    wm = (p["sgu_w_s"] * mask).astype(BF16)
    wmt = jnp.swapaxes(wm, 1, 2)
    bst = jnp.transpose(p["sgu_b_s"])

    h1, z = _norm_proj(x, p["norm_mix_g"], scale1, shift1, win, "mix_proj")
    hstate, ya_pre, *rnn_saved = _rnn_fwd(
        z, p["rnn_conv_w"], p["rnn_conv_b"], wa, p["lru_b_a"], wx, p["lru_b_x"], p["lru_lambda"])
    yb_pre, *sgu_saved = _sgu_fwd(z, p["sgu_ln_g"], p["sgu_ln_b"], wm, bst)
    wba, wbb, wout = late_weights("merge", [ya_pre, yb_pre])
    x2, ya, yb, merged, o1 = _merge_fwd(ya_pre, yb_pre, z, x, gate1, wba, wbb, wout)
    wup = late_weights("ffn_up", [x2])
    h2, up_a, up_v, ff, fa, fv = _ffn_proj_mid(
        x2, p["norm_ffn_g"], scale2, shift2, wup, p["ffn_conv_w"], p["ffn_conv_b"])
    wd = late_weights("ffn_down", [ff])
    dx3, do2, loss, d_gfin, d_gate2 = _ffn_out_loss(ff, wd, x2, target, gate2, p["norm_final_g"])

    dact, dval, d_wd, dcb_a, dcb_v = _ffn_down_bwd(do2, ff, fa, fv, wd)
    dup, dx2, do1, d_cwf, d_shift2, d_scale2, d_gffn, d_gate1 = _ffn_up_bwd(
        dact, dval, up_a, up_v, p["ffn_conv_w"], wup, x2, dx3, p["norm_ffn_g"], scale2, o1, gate1)
    d_wup = _xt_y(h2, dup, "w_up_grad")
    ready = grads_ready if grads_ready else (lambda stage, big, small: None)
    token = ready("ffn", {"w_up": d_wup, "w_down": d_wd}, {})

    dya, dyb, dz, d_wout, d_win = _out_bwd(do1, wout, merged, ya, yb, z, h1)
    dz, d_win, d_wba, d_cw, d_cb, d_wa, d_ba, d_wx, d_bx, d_lam = _rnn_bwd(
        dya, ya_pre, wba, h1, z, rnn_saved, hstate, dz, d_win, p["rnn_conv_w"], wa, wx,
        after_token(p["lru_lambda"], token))
    small = {
        "rnn_conv_w": d_cw, "rnn_conv_b": d_cb, "lru_w_a": d_wa, "lru_b_a": d_ba, "lru_w_x": d_wx, "lru_b_x": d_bx,
        "lru_lambda": d_lam, "norm_ffn_g": d_gffn, "ffn_conv_w": d_cwf,
        "ffn_conv_b": jnp.concatenate([dcb_a, dcb_v], axis=1), "norm_final_g": d_gfin,
    }
    token = ready("rnn", {}, small)
    dz, d_win, d_wbb, d_ws, d_bst, d_lng, d_lnb = _sgu_bwd(
        dyb, yb_pre, wbb, h1, sgu_saved, dz, d_win, p["sgu_ln_g"], after_token(p["sgu_ln_b"], token), wmt, mask)
    sgu_small = {"sgu_ln_g": d_lng, "sgu_ln_b": d_lnb, "sgu_w_s": d_ws, "sgu_b_s": jnp.transpose(d_bst)}
    mixer = {"w_in": d_win, "w_out": d_wout, "w_branch_a": d_wba, "w_branch_b": d_wbb}
    token = ready("mixer", mixer, sgu_small)
    grad_x, d_shift1, d_scale1, d_gmix = _in_bwd(dz, win, x, dx2, after_token(p["norm_mix_g"], token), scale1)

    small.update(sgu_small)
    small["norm_mix_g"] = d_gmix
    dmod = jnp.stack([d_shift1, d_scale1, d_gate1, d_shift2, d_scale2, d_gate2])
    big = {"w_in": d_win, "w_up": d_wup, "w_branch_a": d_wba, "w_branch_b": d_wbb, "w_out": d_wout, "w_down": d_wd}
    return loss, grad_x, big, small, dmod


LAST_REP = ["b_ada", "norm_mix_g"]
EARLY_REP = ["rnn_conv_b", "lru_w_a", "lru_b_a", "lru_w_x", "lru_b_x", "lru_lambda", "norm_ffn_g", "ffn_conv_b",
             "norm_final_g"]
MID_REP = ["sgu_ln_g", "sgu_ln_b", "sgu_w_s", "sgu_b_s"]
COL_SHARDED = ["rnn_conv_w", "ffn_conv_w"]
SMALL_GROUPS = {"rnn": EARLY_REP + COL_SHARDED, "mixer": MID_REP, "last": LAST_REP}
REPLICATED = LAST_REP + EARLY_REP + MID_REP
SMALL_NAMES = REPLICATED + COL_SHARDED
BIG_NAMES = ["w_in", "w_up", "w_branch_a", "w_branch_b", "w_out", "w_down"]
BIG_AXES = [1, 1, 0, 0, 0, 0]
WEIGHTS = ["w_ada", "b_ada", "norm_mix_g", "w_in", "rnn_conv_w", "rnn_conv_b", "lru_w_a", "lru_b_a", "lru_w_x",
           "lru_b_x", "lru_lambda", "sgu_ln_g", "sgu_ln_b", "sgu_w_s", "sgu_b_s", "w_branch_a", "w_branch_b",
           "w_out", "norm_ffn_g", "w_up", "ffn_conv_w", "ffn_conv_b", "w_down", "norm_final_g"]


def _pack_rows(shape):
    return math.prod(shape) // LANES


def _pack(arrays):
    return jnp.concatenate([a.reshape(-1, LANES) for a in arrays], axis=0)


def kernel(x, c, w_ada, b_ada, norm_mix_g, w_in, rnn_conv_w, rnn_conv_b, lru_w_a, lru_b_a, lru_w_x, lru_b_x, lru_lambda, sgu_ln_g, sgu_ln_b, sgu_w_s, sgu_b_s, w_branch_a, w_branch_b, w_out, norm_ffn_g, w_up, ffn_conv_w, ffn_conv_b, w_down, norm_final_g, loss_target, m_w_ada, m_b_ada, m_norm_mix_g, m_w_in, m_rnn_conv_w, m_rnn_conv_b, m_lru_w_a, m_lru_b_a, m_lru_w_x, m_lru_b_x, m_lru_lambda, m_sgu_ln_g, m_sgu_ln_b, m_sgu_w_s, m_sgu_b_s, m_w_branch_a, m_w_branch_b, m_w_out, m_norm_ffn_g, m_w_up, m_ffn_conv_w, m_ffn_conv_b, m_w_down, m_norm_final_g, v_w_ada, v_b_ada, v_norm_mix_g, v_w_in, v_rnn_conv_w, v_rnn_conv_b, v_lru_w_a, v_lru_b_a, v_lru_w_x, v_lru_b_x, v_lru_lambda, v_sgu_ln_g, v_sgu_ln_b, v_sgu_w_s, v_sgu_b_s, v_w_branch_a, v_w_branch_b, v_w_out, v_norm_ffn_g, v_w_up, v_ffn_conv_w, v_ffn_conv_b, v_w_down, v_norm_final_g):
    given = dict(locals())
    me = 4 * lax.axis_index("x") + 2 * lax.axis_index("y") + lax.axis_index("c")
    ada_cols = w_ada.shape[2]
    conv_cols = {"rnn_conv_w": rnn_conv_w.shape[2], "ffn_conv_w": ffn_conv_w.shape[2]}

    (win, c_all, cw_rnn, cw_ffn), _ = _all_gather(
        [w_in[0].astype(BF16), c.reshape(1, 1, D), rnn_conv_w[0], ffn_conv_w[0]], [1, 0, 1, 1], "gather_first")
    c_all = c_all.reshape(N_DEV, D)

    b_cols = lax.dynamic_slice_in_dim(b_ada, me * ada_cols, ada_cols, axis=1)
    (mod_all,), mod_done = _all_gather(
        [_mod_cols(c_all, w_ada[0], b_cols).reshape(1, N_DEV, ada_cols)], [0], "gather_mod")
    mod_all = mod_all.reshape(N_DEV, N_DEV, ada_cols)
    mod_mine = lax.dynamic_index_in_dim(mod_all, me, axis=1, keepdims=False).reshape(6, 1, D)

    late_groups = {"merge": (["w_branch_a", "w_branch_b", "w_out"], [0, 0, 0]), "ffn_up": (["w_up"], [1]),
                   "ffn_down": (["w_down"], [0])}
    in_flight, started = {}, mod_done[0:1, 0:1]
    for stage, (names, axes) in late_groups.items():
        shards = [(given[n][0] + started).astype(BF16) for n in names]
        plan = _gather_plan(axes, [s.shape[-1] for s in shards])
        send, recv, srcs, lands, token = _exchange_start(
            "gather_start_" + stage, shards, [_own_block_placed(s, ax, me) for s, ax in zip(shards, axes)], plan,
            len(shards) * (N_DEV - 1))
        in_flight[stage] = (send, recv, srcs, lands, plan)
        started = started + token[0:1, 0:1]

    def late_weights(stage, after):
        send, recv, srcs, lands, plan = in_flight[stage]
        full = _exchange_wait("gather_wait_" + stage, send, recv, srcs, lands, plan, after)
        full = [w.reshape(-1, D) if ax == 0 else w for w, ax in zip(full, late_groups[stage][1])]
        return full if len(full) > 1 else full[0]

    mod_mine = mod_mine + started

    reducing, packing = {}, {}

    def start_pack(stage, small):
        pack = _pack([small[n] for n in SMALL_GROUPS[stage]])[None]
        plan = _gather_plan([0], [LANES])
        send, recv, srcs, lands, tok = _exchange_start(
            "small_start_" + stage, [pack], [_own_block_placed(pack, 0, me)], plan, N_DEV - 1)
        packing[stage] = (send, recv, srcs, lands, plan)
        return tok

    def grads_ready(stage, grads, small):
        tokens = [start_pack(stage, small)] if small else []
        if grads:
            tokens.append(start_reduce(stage, grads))
        return sum(tokens[1:], tokens[0])

    def start_reduce(stage, grads):
        names = [n for n in BIG_NAMES if n in grads]
        blocked = {}
        for n in names:
            ax = BIG_AXES[BIG_NAMES.index(n)]
            g = grads[n] if ax == 1 else grads[n].reshape(N_DEV, grads[n].shape[0] // N_DEV, grads[n].shape[1])
            blocked.setdefault((ax, g.shape), []).append((n, g))
        sums = {}
        for (ax, _), group in blocked.items():
            reduced = _sibling_reduce([g for _, g in group], ax, "reduce_sibling_" + "_".join(n for n, _ in group))
            sums.update({n: r for (n, _), r in zip(group, reduced)})
        sums = [sums[n] for n in names]
        pays = [pay for _, pay in sums]
        send, recv, srcs, lands, tok = _exchange_start(
            "reduce_start_" + stage, pays, [lax.empty(p_.shape, p_.dtype) for p_ in pays], _chip_plan, 3 * len(pays))
        reducing[stage] = (names, [own for own, _ in sums], send, recv, srcs, lands)
        return tok

    p = {n: given[n][0] for n in REPLICATED if n not in ("b_ada", "norm_final_g")}
    p = {n: (a.reshape(1, -1) if a.ndim == 1 else a) for n, a in p.items()}
    p["rnn_conv_w"], p["ffn_conv_w"] = cw_rnn, cw_ffn
    p["norm_final_g"] = norm_final_g.reshape(1, D)
    loss, grad_x, _, small, dmod = _local_step(x[0], loss_target[0], mod_mine, win, late_weights, p, grads_ready)

    small["b_ada"] = dmod.reshape(1, 6 * D)
    rows_of = {n: _pack_rows(small[n].shape) for n in SMALL_NAMES}
    (last,), _ = _all_gather([_pack([small[n] for n in LAST_REP])[None]], [0], "gather_small")
    gathered = {"last": last}
    for stage, (send, recv, srcs, lands, plan) in packing.items():
        (gathered[stage],) = _exchange_wait("small_wait_" + stage, send, recv, srcs, lands, plan, [grad_x])
    gathered = {k: v.reshape(N_DEV, -1, LANES) for k, v in gathered.items()}

    out = {}
    for stage, (names, owns, send, recv, srcs, lands) in reducing.items():
        landed = _exchange_wait("reduce_wait_" + stage, send, recv, srcs, lands, _chip_plan, [last])
        for n, own, got in zip(names, owns, landed):
            out[n] = _adamw(given[n][0], given["m_" + n][0], given["v_" + n][0], [own, got], "adamw_" + n)

    dmod_all = gathered["last"][:, :rows_of["b_ada"]].reshape(N_DEV, 6 * D)
    dmod_cols = lax.dynamic_slice_in_dim(dmod_all, me * ada_cols, ada_cols, axis=1)
    out["w_ada"] = _adamw(w_ada[0], m_w_ada[0], v_w_ada[0], [_ada_grad(c_all, dmod_cols)], "adamw_w_ada")

    def rows_form(a):
        return a.reshape(1, -1) if a.size // a.shape[-1] == 1 or a.ndim == 1 else a.reshape(-1, LANES)

    for stage, names in (("last", LAST_REP), ("rnn", EARLY_REP), ("mixer", MID_REP)):
        out.update(_adamw_group(names, *[[rows_form(given[pre + n]) for n in names] for pre in ("", "m_", "v_")],
                                gathered[stage], "adamw_small_" + stage))

    row0 = sum(rows_of[n] for n in EARLY_REP)
    for n in COL_SHARDED:
        full = gathered["rnn"][:, row0:row0 + rows_of[n]].reshape(N_DEV, small[n].shape[0], small[n].shape[1])
        mine = lax.dynamic_slice_in_dim(full, me * conv_cols[n], conv_cols[n], axis=2)
        out[n] = _adamw(given[n][0], given["m_" + n][0], given["v_" + n][0], [mine], "adamw_" + n)
        row0 += rows_of[n]

    total = lax.psum(loss[0, 0], ("x", "y", "c"))
    results = [total, grad_x[None]]
    for kind in range(4):
        results += [out[n][kind].reshape(given[n].shape) for n in WEIGHTS]
    return tuple(results)
```
